```python
import math
import jax
import jax.numpy as jnp
from jax import lax
import numpy as np

D_MODEL = 2048
BATCH = 4
SEQ = 2048
DEPTH = 2

ATT_WIDTH = D_MODEL // 2
RWKV_WIDTH = D_MODEL - ATT_WIDTH
ATT_QK_DIM = 64
ATT_V_DIM = 2 * ATT_QK_DIM
ATT_HEADS = ATT_WIDTH // ATT_V_DIM
ATT_QK_COLS = ATT_HEADS * 2 * ATT_QK_DIM
ATT_COLS = 2 * ATT_QK_COLS + ATT_WIDTH
Q_BLOCK = 128
ALIBI_MAX_BIAS = 8.0
ATT_SUBLN_EPS = 1e-5
RWKV_HEAD = 64
RWKV_HEADS = RWKV_WIDTH // RWKV_HEAD
RWKV_W_RANK = 64
RWKV_A_RANK = 64
RWKV_G_RANK = 128
RWKV_V_RANK = 32
RWKV_COLS = 3 * RWKV_WIDTH + RWKV_W_RANK + RWKV_A_RANK + RWKV_G_RANK
RWKV_SPLIT = (RWKV_WIDTH, 2 * RWKV_WIDTH, 3 * RWKV_WIDTH,
              3 * RWKV_WIDTH + RWKV_W_RANK, 3 * RWKV_WIDTH + RWKV_W_RANK + RWKV_A_RANK)
RWKV_GN_EPS = 64e-5
IN_COLS = ATT_COLS + RWKV_COLS
N_EXPERTS = 64
N_GROUPS = 8
TOPK_GROUPS = 4
TOP_K = 8
D_EXPERT = 512
D_SHARED = 512
ROUTED_SCALE = 2.5
MOE_BLOCK = 128
NORM_EPS = 1e-6
N_MOD = 6

kernel_name = "hybrid_diffattn_rwkv7_moe_block"


def rms_norm(x, g, eps=NORM_EPS):
    xf = x.astype(jnp.float32)
    y = xf * lax.rsqrt(jnp.mean(xf * xf, axis=-1, keepdims=True) + eps)
    return (y * g.astype(jnp.float32)).astype(x.dtype)


def token_shift(h, mu):
    prev = jnp.pad(h, ((0, 0), (1, 0), (0, 0)))[:, :-1]
    return h + (prev - h) * mu


def alibi_slopes(n_heads):
    return 2.0 ** (-ALIBI_MAX_BIAS * jnp.arange(1, n_heads + 1, dtype=jnp.float32) / n_heads)


def diff_attention(q, k, v, lam, subln_g, lam_init):
    B, S = q.shape[0], q.shape[1]
    nb = S // Q_BLOCK
    f32 = jnp.float32
    scale = ATT_QK_DIM ** -0.5
    slopes = alibi_slopes(ATT_HEADS)[:, None, None, None]
    kf = k.astype(f32)
    vf = v.astype(f32)
    qb = q.astype(f32).reshape(B, nb, Q_BLOCK, ATT_HEADS, 2, ATT_QK_DIM).swapaxes(0, 1)
    kpos = jnp.arange(S)

    def block(args):
        qi, bi = args
        qpos = bi * Q_BLOCK + jnp.arange(Q_BLOCK)
        dist = qpos[:, None] - kpos[None, :]
        s = jnp.einsum("bqhcd,bkhcd->bhcqk", qi, kf) * scale - slopes * dist.astype(f32)
        s = jnp.where(dist >= 0, s, -jnp.inf)
        p = jax.nn.softmax(s, axis=-1)
        a = p[:, :, 0] - lam * p[:, :, 1]
        return jnp.einsum("bhqk,bkhd->bqhd", a, vf)

    o = lax.map(block, (qb, jnp.arange(nb)))
    o = o.swapaxes(0, 1).reshape(B, S, ATT_HEADS, ATT_V_DIM)
    o = o * lax.rsqrt(jnp.mean(o * o, axis=-1, keepdims=True) + ATT_SUBLN_EPS)
    o = o * subln_g.astype(f32) * (1.0 - lam_init)
    return o.reshape(B, S, ATT_WIDTH)


def wkv7_scan(r, w, k, v, a, b):
    B, S, H, N = r.shape
    xs = tuple(t.swapaxes(0, 1) for t in (r, w, k, v, a, b))

    def step(state, inp):
        r_t, w_t, k_t, v_t, a_t, b_t = inp
        sa = jnp.einsum("bhij,bhj->bhi", state, a_t)
        state = (state * w_t[:, :, None, :] + sa[..., None] * b_t[:, :, None, :]
                 + v_t[..., None] * k_t[:, :, None, :])
        return state, jnp.einsum("bhij,bhj->bhi", state, r_t)

    s0 = jnp.zeros((B, H, N, N), jnp.float32)
    _, ys = lax.scan(step, s0, xs)
    return ys.swapaxes(0, 1)


def rwkv7_mix(feats, v_first, w0, w2, a0, a2, g2, k_k, k_a, r_k, lnx_w, lnx_b, vres):
    B, S, _ = feats.shape
    f32 = jnp.float32
    feats = feats.astype(f32)
    r, k, v, w_lo, a_lo, g_lo = jnp.split(feats, list(RWKV_SPLIT), axis=-1)
    w = w0 + jnp.tanh(w_lo) @ w2
    decay = jnp.exp(-jnp.exp(-jax.nn.softplus(-w) - 0.5))
    a = jax.nn.sigmoid(a0 + a_lo @ a2)
    g = jax.nn.sigmoid(g_lo) @ g2
    if vres is None:
        v_first = v
    else:
        v0, v1, v2 = vres
        v = v + (v_first - v) * jax.nn.sigmoid(v0 + (v @ v1) @ v2)
    heads = lambda t: t.reshape(B, S, RWKV_HEADS, RWKV_HEAD)
    kk = heads(k * k_k)
    kk = kk / jnp.maximum(jnp.sqrt(jnp.sum(kk * kk, axis=-1, keepdims=True)), 1e-12)
    kh = heads(k * (1.0 + (a - 1.0) * k_a))
    rh, vh, ah = heads(r), heads(v), heads(a)
    y = wkv7_scan(rh, heads(decay), kh, vh, -kk, kk * ah)
    mu = jnp.mean(y, axis=-1, keepdims=True)
    var = jnp.mean(jnp.square(y - mu), axis=-1, keepdims=True)
    y = ((y - mu) * lax.rsqrt(var + RWKV_GN_EPS)).reshape(B, S, RWKV_WIDTH) * lnx_w + lnx_b
    bonus = jnp.sum(rh * kh * r_k, axis=-1, keepdims=True) * vh
    y = (y + bonus.reshape(B, S, RWKV_WIDTH)) * g
    return y, v_first


def swiglu(x, w_gate, w_up, w_down):
    return (jax.nn.silu(x @ w_gate) * (x @ w_up)) @ w_down


def routed_experts(xf, eidx, gate, w_gate, w_up, w_down):
    T, D = xf.shape
    A = T * TOP_K
    n_blocks = -(-A // MOE_BLOCK) + N_EXPERTS
    P = n_blocks * MOE_BLOCK
    flat_e = eidx.reshape(-1).astype(jnp.int32)
    flat_tok = jnp.repeat(jnp.arange(T, dtype=jnp.int32), TOP_K)
    flat_g = gate.reshape(-1)
    order = jnp.argsort(flat_e, stable=True)
    se, stok, sg = flat_e[order], flat_tok[order], flat_g[order]
    counts = jnp.bincount(flat_e, length=N_EXPERTS)
    padded = (counts + MOE_BLOCK - 1) // MOE_BLOCK * MOE_BLOCK
    pad_end = jnp.cumsum(padded)
    pad_start = pad_end - padded
    start = jnp.cumsum(counts) - counts
    dest = pad_start[se] + jnp.arange(A, dtype=jnp.int32) - start[se]
    row_tok = jnp.zeros((P,), jnp.int32).at[dest].set(stok)
    row_g = jnp.zeros((P,), xf.dtype).at[dest].set(sg)
    blk_start = jnp.arange(n_blocks, dtype=jnp.int32) * MOE_BLOCK
    blk_e = jnp.minimum(jnp.searchsorted(pad_end, blk_start, side="right"), N_EXPERTS - 1)

    def block(args):
        e, tok, g = args
        xb = xf[tok]
        return swiglu(xb, w_gate[e], w_up[e], w_down[e]) * g[:, None]

    rows = lax.map(block, (blk_e, row_tok.reshape(n_blocks, MOE_BLOCK),
                           row_g.reshape(n_blocks, MOE_BLOCK)))
    return jax.ops.segment_sum(rows.reshape(P, D), row_tok, num_segments=T)


def moe_ffn(xf, w_router, router_bias, w_exp_gate, w_exp_up, w_exp_down,
            w_sh_gate, w_sh_up, w_sh_down):
    T = xf.shape[0]
    scores = jax.nn.sigmoid((xf @ w_router).astype(jnp.float32))
    biased = scores + router_bias.astype(jnp.float32)
    grp = biased.reshape(T, N_GROUPS, N_EXPERTS // N_GROUPS)
    grp_score = jnp.sum(lax.top_k(grp, 2)[0], axis=-1)
    _, gidx = lax.top_k(grp_score, TOPK_GROUPS)
    gmask = jnp.sum(jax.nn.one_hot(gidx, N_GROUPS, dtype=jnp.float32), axis=1)
    emask = jnp.repeat(gmask, N_EXPERTS // N_GROUPS, axis=1)
    _, eidx = lax.top_k(jnp.where(emask > 0, biased, -jnp.inf), TOP_K)
    gate = jnp.take_along_axis(scores, eidx, axis=1)
    gate = gate / jnp.sum(gate, axis=-1, keepdims=True) * ROUTED_SCALE
    routed = routed_experts(xf, eidx, gate.astype(xf.dtype), w_exp_gate, w_exp_up, w_exp_down)
    return routed + swiglu(xf, w_sh_gate, w_sh_up, w_sh_down)


def setup_inputs(seed: int = 0) -> dict:
    key = jax.random.key(seed)
    ks = iter(jax.random.split(key, 48))
    f32 = jnp.float32
    L, D = DEPTH, D_MODEL
    LV = max(DEPTH - 1, 0)
    nrm = lambda shape, s: s * jax.random.normal(next(ks), shape, f32)
    gain = lambda shape: 1.0 + 0.05 * jax.random.normal(next(ks), shape, f32)
    return {
        "x": nrm((BATCH, SEQ, D), 1.0),
        "c": nrm((BATCH, D), 1.0),
        "w_ada": nrm((L, D, N_MOD * D), 0.5 * D ** -0.5),
        "b_ada": nrm((L, N_MOD * D), 0.01),
        "g_pre_mix": gain((L, D)),
        "g_post_mix": gain((L, D)),
        "g_pre_ffn": gain((L, D)),
        "g_post_ffn": gain((L, D)),
        "w_in": nrm((L, D, IN_COLS), D ** -0.5),
        "w_out": nrm((L, D, D), D ** -0.5),
        "lam_q1": nrm((L, ATT_QK_DIM), 0.1),
        "lam_k1": nrm((L, ATT_QK_DIM), 0.1),
        "lam_q2": nrm((L, ATT_QK_DIM), 0.1),
        "lam_k2": nrm((L, ATT_QK_DIM), 0.1),
        "att_subln_g": gain((L, ATT_V_DIM)),
        "rwkv_mu": jax.random.uniform(next(ks), (L, RWKV_COLS), f32),
        "rwkv_w0": jax.random.uniform(next(ks), (L, RWKV_WIDTH), f32, -6.5, 0.5),
        "rwkv_w2": nrm((L, RWKV_W_RANK, RWKV_WIDTH), 0.1),
        "rwkv_a0": nrm((L, RWKV_WIDTH), 0.1),
        "rwkv_a2": nrm((L, RWKV_A_RANK, RWKV_WIDTH), 0.1),
        "rwkv_g2": nrm((L, RWKV_G_RANK, RWKV_WIDTH), RWKV_G_RANK ** -0.5),
        "rwkv_k_k": gain((L, RWKV_WIDTH)),
        "rwkv_k_a": gain((L, RWKV_WIDTH)),
        "rwkv_r_k": nrm((L, RWKV_HEADS, RWKV_HEAD), 0.1),
        "rwkv_lnx_w": gain((L, RWKV_WIDTH)),
        "rwkv_lnx_b": nrm((L, RWKV_WIDTH), 0.01),
        "rwkv_v0": nrm((LV, RWKV_WIDTH), 0.1),
        "rwkv_v1": nrm((LV, RWKV_WIDTH, RWKV_V_RANK), RWKV_WIDTH ** -0.5),
        "rwkv_v2": nrm((LV, RWKV_V_RANK, RWKV_WIDTH), 0.1),
        "w_router": nrm((L, D, N_EXPERTS), D ** -0.5),
        "router_bias": nrm((L, N_EXPERTS), 0.01),
        "w_exp_gate": nrm((L, N_EXPERTS, D, D_EXPERT), D ** -0.5),
        "w_exp_up": nrm((L, N_EXPERTS, D, D_EXPERT), D ** -0.5),
        "w_exp_down": nrm((L, N_EXPERTS, D_EXPERT, D), D_EXPERT ** -0.5),
        "w_sh_gate": nrm((L, D, D_SHARED), D ** -0.5),
        "w_sh_up": nrm((L, D, D_SHARED), D ** -0.5),
        "w_sh_down": nrm((L, D_SHARED, D), D_SHARED ** -0.5),
    }


def reference(x, c, w_ada, b_ada, g_pre_mix, g_post_mix, g_pre_ffn, g_post_ffn, w_in, w_out,
              lam_q1, lam_k1, lam_q2, lam_k2, att_subln_g, rwkv_mu, rwkv_w0, rwkv_w2, rwkv_a0,
              rwkv_a2, rwkv_g2, rwkv_k_k, rwkv_k_a, rwkv_r_k, rwkv_lnx_w, rwkv_lnx_b, rwkv_v0,
              rwkv_v1, rwkv_v2, w_router, router_bias, w_exp_gate, w_exp_up, w_exp_down,
              w_sh_gate, w_sh_up, w_sh_down):
    B, S, D = x.shape
    cond = jax.nn.silu(c)
    v_first = None
    for i in range(DEPTH):
        mod = (cond @ w_ada[i] + b_ada[i])[:, None, :]
        sh_a, sc_a, g_a, sh_f, sc_f, g_f = jnp.split(mod, N_MOD, axis=-1)

        h = rms_norm(x, g_pre_mix[i]) * (1.0 + sc_a) + sh_a
        p = h @ w_in[i]
        q, k, v = jnp.split(p[..., :ATT_COLS], [ATT_QK_COLS, 2 * ATT_QK_COLS], axis=-1)
        lam_init = 0.8 - 0.6 * math.exp(-0.3 * i)
        lam = (jnp.exp(jnp.sum(lam_q1[i] * lam_k1[i])) - jnp.exp(jnp.sum(lam_q2[i] * lam_k2[i]))
               + lam_init).astype(jnp.float32)
        o_att = diff_attention(q.reshape(B, S, ATT_HEADS, 2, ATT_QK_DIM),
                               k.reshape(B, S, ATT_HEADS, 2, ATT_QK_DIM),
                               v.reshape(B, S, ATT_HEADS, ATT_V_DIM),
                               lam, att_subln_g[i], lam_init)
        feats = token_shift(p[..., ATT_COLS:], rwkv_mu[i])
        vres = None if i == 0 else (rwkv_v0[i - 1], rwkv_v1[i - 1], rwkv_v2[i - 1])
        o_rwkv, v_first = rwkv7_mix(feats, v_first, rwkv_w0[i], rwkv_w2[i], rwkv_a0[i],
                                    rwkv_a2[i], rwkv_g2[i], rwkv_k_k[i], rwkv_k_a[i],
                                    rwkv_r_k[i], rwkv_lnx_w[i], rwkv_lnx_b[i], vres)
        mixed = jnp.concatenate([o_att.astype(x.dtype), o_rwkv.astype(x.dtype)], axis=-1) @ w_out[i]
        x = x + g_a * rms_norm(mixed, g_post_mix[i])

        h = rms_norm(x, g_pre_ffn[i]) * (1.0 + sc_f) + sh_f
        y = moe_ffn(h.reshape(B * S, D), w_router[i], router_bias[i], w_exp_gate[i],
                    w_exp_up[i], w_exp_down[i], w_sh_gate[i], w_sh_up[i],
                    w_sh_down[i]).reshape(B, S, D)
        x = x + g_f * rms_norm(y.astype(x.dtype), g_post_ffn[i])
    return x
```

```python
import functools
import math

import jax
import jax.numpy as jnp
from jax import lax
from jax.experimental import pallas as pl
from jax.experimental.pallas import tpu as pltpu

f32 = jnp.float32
bf16 = jnp.bfloat16
i32 = jnp.int32

ATT_QK_DIM = 64
ATT_V_DIM = 128
ALIBI_MAX_BIAS = 8.0
ATT_SUBLN_EPS = 1e-5
RWKV_HEAD = 64
RWKV_W_RANK = 64
RWKV_A_RANK = 64
RWKV_G_RANK = 128
RWKV_V_RANK = 32
RWKV_GN_EPS = 64e-5
N_EXPERTS = 64
N_GROUPS = 8
TOPK_GROUPS = 4
TOP_K = 8
ROUTED_SCALE = 2.5
NORM_EPS = 1e-6
N_MOD = 6

LANES = 128
CHUNK = 64
VMEM_LIMIT = 56 * 1024 * 1024


def _cparams(sem):
    return pltpu.CompilerParams(dimension_semantics=sem, vmem_limit_bytes=VMEM_LIMIT)


def _dot(a, b):
    return jnp.dot(a, b, preferred_element_type=f32)


def _dot_nt(a, b):
    return lax.dot_general(a, b, (((1,), (1,)), ((), ())), preferred_element_type=f32)


def _split2(x):
    hi = x.astype(bf16)
    lo = (x - hi.astype(f32)).astype(bf16)
    return hi, lo


def _mm(a, b, passes=1):
    if passes == 1:
        return _dot(a.astype(bf16), b.astype(bf16))
    ah, al = _split2(a)
    bh, bl = _split2(b)
    return (_dot(al, bh) + _dot(ah, bl)) + _dot(ah, bh)


def _mm_exact_rhs(a, b_bf16):
    ah, al = _split2(a)
    return _dot(al, b_bf16) + _dot(ah, b_bf16)


def _sigmoid(x):
    return 1.0 / (1.0 + jnp.exp(-x))


def _ada_kernel(c_ref, w_ref, b_ref, o_ref):
    c = c_ref[...]
    cond = (c * _sigmoid(c)).astype(bf16)
    o_ref[0] = _dot(cond, w_ref[0].astype(bf16)) + b_ref[0]


def _ada_mod(c_pad, w_ada, b_ada, tn=1024):
    L, D, N = w_ada.shape
    bp = c_pad.shape[0]
    return pl.pallas_call(
        _ada_kernel,
        grid=(L, N // tn),
        in_specs=[pl.BlockSpec((bp, D), lambda l, j: (0, 0)),
                  pl.BlockSpec((1, D, tn), lambda l, j: (l, 0, j)),
                  pl.BlockSpec((1, 1, tn), lambda l, j: (l, 0, j))],
        out_specs=pl.BlockSpec((1, bp, tn), lambda l, j: (l, 0, j)),
        out_shape=jax.ShapeDtypeStruct((L, bp, N), f32),
        compiler_params=_cparams(("arbitrary", "arbitrary")),
        name="ada_mod",
    )(c_pad, w_ada, b_ada.reshape(L, 1, N))


def _inproj_kernel(x_ref, sc_ref, sh_ref, g_ref, w_ref, o_ref, h_ref):
    @pl.when(pl.program_id(1) == 0)
    def _():
        x = x_ref[...]
        ms = jnp.mean(x * x, axis=-1, keepdims=True)
        y = x * lax.rsqrt(ms + NORM_EPS) * g_ref[...]
        h_ref[...] = (y * (1.0 + sc_ref[0]) + sh_ref[0]).astype(bf16)

    o_ref[...] = _dot(h_ref[...], w_ref[...]).astype(o_ref.dtype)


def _inproj(x2, mod3, g, w_bf, out_dtype, S, tm, tn, seg_sc, seg_sh):
    T, D = x2.shape
    N = w_bf.shape[1]
    tpb = S // tm
    return pl.pallas_call(
        _inproj_kernel,
        grid=(T // tm, N // tn),
        in_specs=[pl.BlockSpec((tm, D), lambda i, j: (i, 0)),
                  pl.BlockSpec((1, 1, D), lambda i, j: ((i // tpb) * N_MOD + seg_sc, 0, 0)),
                  pl.BlockSpec((1, 1, D), lambda i, j: ((i // tpb) * N_MOD + seg_sh, 0, 0)),
                  pl.BlockSpec((1, D), lambda i, j: (0, 0)),
                  pl.BlockSpec((D, tn), lambda i, j: (0, j))],
        out_specs=pl.BlockSpec((tm, tn), lambda i, j: (i, j)),
        out_shape=jax.ShapeDtypeStruct((T, N), out_dtype),
        scratch_shapes=[pltpu.VMEM((tm, D), bf16)],
        compiler_params=_cparams(("arbitrary", "arbitrary")),
        name="inproj",
    )(x2, mod3, mod3, g.reshape(1, D), w_bf)


def _attn_kernel(q_ref, k_ref, v_ref, slope_ref, lamp_ref, g_ref, o_ref,
                 q2_ref, m_ref, l_ref, acc_ref, *, tq, lam_init):
    qi = pl.program_id(2)
    scale = ATT_QK_DIM ** -0.5
    slope = slope_ref[0][:, 0:1]

    q = q_ref[...].astype(f32) * scale
    lane = lax.broadcasted_iota(i32, (1, LANES), 1)
    first = lane < ATT_QK_DIM
    q2_ref[0:tq, :] = jnp.where(first, q, 0.0).astype(bf16)
    q2_ref[tq:2 * tq, :] = jnp.where(first, 0.0, q).astype(bf16)
    m_ref[...] = jnp.full(m_ref.shape, -jnp.inf, f32)
    l_ref[...] = jnp.zeros(l_ref.shape, f32)
    acc_ref[...] = jnp.zeros(acc_ref.shape, f32)

    rows = lax.broadcasted_iota(i32, (2 * tq, tq), 0)
    cols = lax.broadcasted_iota(i32, (2 * tq, tq), 1)
    rel = (jnp.where(rows >= tq, rows - tq, rows) - cols).astype(f32)

    def step(ki, masked):
        start = pl.multiple_of(ki * tq, tq)
        kb = k_ref[pl.ds(start, tq), :]
        vb = v_ref[pl.ds(start, tq), :]
        s = _dot_nt(q2_ref[...], kb)
        dist = rel + ((qi - ki) * tq).astype(f32)
        s = s - slope * dist
        if masked:
            s = jnp.where(dist >= 0.0, s, -jnp.inf)
        m_prev = m_ref[...]
        m_new = jnp.maximum(m_prev, jnp.max(s, axis=-1, keepdims=True))
        alpha = jnp.exp(m_prev - m_new)
        p = jnp.exp(s - m_new)
        l_ref[...] = alpha * l_ref[...] + jnp.sum(p, axis=-1, keepdims=True)
        acc_ref[...] = alpha * acc_ref[...] + _dot(p.astype(bf16), vb)
        m_ref[...] = m_new

    def body(ki, carry):
        step(ki, False)
        return carry

    lax.fori_loop(0, qi, body, 0)
    step(qi, True)

    lp = lamp_ref[...]
    lam = (jnp.exp(jnp.sum(lp[0:1] * lp[1:2], axis=-1, keepdims=True))
           - jnp.exp(jnp.sum(lp[2:3] * lp[3:4], axis=-1, keepdims=True)) + lam_init)
    acc = acc_ref[...]
    l = l_ref[...]
    o = acc[0:tq] / l[0:tq] - lam * (acc[tq:2 * tq] / l[tq:2 * tq])
    o = o * lax.rsqrt(jnp.mean(o * o, axis=-1, keepdims=True) + ATT_SUBLN_EPS)
    o_ref[...] = (o * g_ref[...] * (1.0 - lam_init)).astype(o_ref.dtype)


def _attention(att, slopes, lamp, subln_g, B, S, H, lam_init, tq):
    T = att.shape[0]
    nq = S // tq
    kern = functools.partial(_attn_kernel, tq=tq, lam_init=lam_init)
    return pl.pallas_call(
        kern,
        grid=(B, H, nq),
        in_specs=[pl.BlockSpec((tq, LANES), lambda b, h, q: (b * nq + q, h)),
                  pl.BlockSpec((S, LANES), lambda b, h, q: (b, H + h)),
                  pl.BlockSpec((S, LANES), lambda b, h, q: (b, 2 * H + h)),
                  pl.BlockSpec((1, 1, LANES), lambda b, h, q: (h, 0, 0)),
                  pl.BlockSpec((4, ATT_QK_DIM), lambda b, h, q: (0, 0)),
                  pl.BlockSpec((1, ATT_V_DIM), lambda b, h, q: (0, 0))],
        out_specs=pl.BlockSpec((tq, LANES), lambda b, h, q: (b * nq + q, h)),
        out_shape=jax.ShapeDtypeStruct((T, H * ATT_V_DIM), bf16),
        scratch_shapes=[pltpu.VMEM((2 * tq, LANES), bf16),
                        pltpu.VMEM((2 * tq, 1), f32),
                        pltpu.VMEM((2 * tq, 1), f32),
                        pltpu.VMEM((2 * tq, LANES), f32)],
        compiler_params=_cparams(("arbitrary", "arbitrary", "arbitrary")),
        name="diff_attention",
    )(att, att, att, slopes, lamp, subln_g.reshape(1, ATT_V_DIM))


def _head_sums(x, ind, indt):
    s = _mm_exact_rhs(x, ind)
    return _mm_exact_rhs(s, indt)


def _rwkv_prep_kernel(*refs, W, has_vres):
    if has_vres:
        (f_ref, mu_ref, w0_ref, w2_ref, a0_ref, a2_ref, g2_ref, kk_ref, ka_ref, ind_ref, indt_ref,
         vf_ref, v0_ref, v1_ref, v2_ref,
         r_o, lw_o, kh_o, v_o, kn_o, kb_o, g_o, carry_ref) = refs
    else:
        (f_ref, mu_ref, w0_ref, w2_ref, a0_ref, a2_ref, g2_ref, kk_ref, ka_ref, ind_ref, indt_ref,
         r_o, lw_o, kh_o, v_o, kn_o, kb_o, g_o, carry_ref) = refs

    ti = pl.program_id(1)
    h = f_ref[...]
    tm = h.shape[0]

    @pl.when(ti == 0)
    def _():
        carry_ref[...] = jnp.zeros(carry_ref.shape, f32)

    rolled = pltpu.roll(h, 1, axis=0)
    row = lax.broadcasted_iota(i32, (tm, 1), 0)
    prev = jnp.where(row == 0, carry_ref[...], rolled)
    carry_ref[...] = h[tm - 1:tm, :]
    feats = h + (prev - h) * mu_ref[...]

    r = feats[:, 0:W]
    k = feats[:, W:2 * W]
    v = feats[:, 2 * W:3 * W]
    wa = feats[:, 3 * W:3 * W + LANES]
    g_lo = feats[:, 3 * W + LANES:3 * W + 2 * LANES]

    w = w0_ref[...] + _mm(jnp.tanh(wa), w2_ref[...], passes=3)
    lw_o[...] = -math.exp(-0.5) * _sigmoid(w)
    a = _sigmoid(a0_ref[...] + _mm(wa, a2_ref[...], passes=3))
    g_o[...] = _mm(_sigmoid(g_lo), g2_ref[...])

    if has_vres:
        mix = _sigmoid(v0_ref[...] + _mm(_mm(v, v1_ref[...]), v2_ref[...]))
        v = v + (vf_ref[...] - v) * mix

    kk = k * kk_ref[...]
    ss = _head_sums(kk * kk, ind_ref[...], indt_ref[...])
    kk = kk / jnp.maximum(jnp.sqrt(ss), 1e-12)
    r_o[...] = r
    kh_o[...] = k * (1.0 + (a - 1.0) * ka_ref[...])
    v_o[...] = v
    kn_o[...] = kk
    kb_o[...] = kk * a


def _rwkv_prep(feats, prm, vfirst, B, S, W, tm):
    T, COLS = feats.shape
    tpb = S // tm
    has_vres = vfirst is not None
    row = lambda n: pl.BlockSpec((1, n), lambda b, t: (0, 0))
    full = lambda a: pl.BlockSpec(a.shape, lambda b, t: (0, 0))
    tile = pl.BlockSpec((tm, W), lambda b, t: (b * tpb + t, 0))
    args = [feats, prm["mu"], prm["w0"], prm["w2p"], prm["a0"], prm["a2p"], prm["g2"], prm["k_k"], prm["k_a"],
            prm["ind"], prm["indt"]]
    specs = [pl.BlockSpec((tm, COLS), lambda b, t: (b * tpb + t, 0)), row(COLS), row(W), full(prm["w2p"]),
             row(W), full(prm["a2p"]), full(prm["g2"]), row(W), row(W), full(prm["ind"]), full(prm["indt"])]
    if has_vres:
        args += [vfirst, prm["v0"], prm["v1p"], prm["v2p"]]
        specs += [tile, row(W), full(prm["v1p"]), full(prm["v2p"])]
    out = jax.ShapeDtypeStruct((T, W), f32)
    kern = functools.partial(_rwkv_prep_kernel, W=W, has_vres=has_vres)
    return pl.pallas_call(
        kern,
        grid=(B, tpb),
        in_specs=specs,
        out_specs=[tile] * 7,
        out_shape=[out] * 7,
        scratch_shapes=[pltpu.VMEM((1, COLS), f32)],
        compiler_params=_cparams(("arbitrary", "arbitrary")),
        name="rwkv_prep",
    )(*args)


def _wkv_chunk(r, lw, k, v, kn, kb, passes):
    C = CHUNK
    P2 = 2 * C
    ri = lax.broadcasted_iota(i32, (C, C), 0)
    ci = lax.broadcasted_iota(i32, (C, C), 1)
    tri = (ci <= ri).astype(bf16)
    h1 = lw.astype(bf16)
    r1 = lw - h1.astype(f32)
    h2 = r1.astype(bf16)
    h3 = (r1 - h2.astype(f32)).astype(bf16)
    cum = (_dot(tri, h3) + _dot(tri, h2)) + _dot(tri, h1)
    cumx = cum - lw
    cum_c = cum[C - 1:C, :]
    at = -kn * jnp.exp(cumx)
    rt = r * jnp.exp(cum)
    einv = jnp.exp(-cum)
    bt = kb * einv
    kt = k * einv
    eh = jnp.exp(cum_c - cum)
    bh = kb * eh
    kh = k * eh
    w_c = jnp.exp(cum_c)

    lane = lax.broadcasted_iota(i32, (1, LANES), 1)
    m0 = (lane < RWKV_HEAD).astype(f32)
    m1 = 1.0 - m0
    stack = lambda x: jnp.concatenate([x * m0, x * m1], axis=0)
    abd, rbd, bst, kst, vst = stack(at), stack(rt), stack(bt), stack(kt), stack(v)

    gram = _mm_nt(jnp.concatenate([abd, rbd], axis=0), jnp.concatenate([bst, kst], axis=0), passes)
    rr = lax.broadcasted_iota(i32, (P2, P2), 0)
    cc = lax.broadcasted_iota(i32, (P2, P2), 1)
    same = jnp.where(rr >= C, 1, 0) == jnp.where(cc >= C, 1, 0)
    strict = same & (cc < rr)
    incl = same & (cc <= rr)
    lab = jnp.where(strict, gram[0:P2, 0:P2], 0.0)
    lak = jnp.where(strict, gram[0:P2, P2:2 * P2], 0.0)
    mrb = jnp.where(incl, gram[P2:2 * P2, 0:P2], 0.0)
    mrk = jnp.where(incl, gram[P2:2 * P2, P2:2 * P2], 0.0)

    eye = (rr == cc).astype(f32)
    tinv = eye + lab
    lp = lab
    n_sq = int(math.log2(C)) - 1
    for _ in range(n_sq):
        lp = _mm(lp, lp, passes)
        tinv = tinv + _mm(lp, tinv, passes)

    x0 = _mm(lak, vst, passes)
    ta = _mm(tinv, jnp.concatenate([abd, x0], axis=1), passes)
    a2bd, u0bd = ta[:, 0:LANES], ta[:, LANES:2 * LANES]
    zeros = jnp.zeros((P2, LANES), f32)
    rhs2 = jnp.concatenate([jnp.concatenate([a2bd, u0bd], axis=1),
                            jnp.concatenate([zeros, vst], axis=1)], axis=0)
    z = _mm(jnp.concatenate([mrb, mrk], axis=1), rhs2, passes)
    fold = lambda x: x[0:C] + x[C:2 * C]
    r2 = rt + fold(z[:, 0:LANES])
    y0 = fold(z[:, LANES:2 * LANES])
    a2 = fold(a2bd)
    u0 = fold(u0bd)
    lhs3t = jnp.concatenate([bh, kh], axis=0).T
    rhs3 = jnp.concatenate([jnp.concatenate([a2, u0], axis=1),
                            jnp.concatenate([jnp.zeros((C, LANES), f32), v], axis=1)], axis=0)
    wmat = _mm(lhs3t, rhs3, passes)
    mmat = jnp.where(same, wmat[:, 0:LANES], 0.0) + eye * w_c
    g0 = jnp.where(same, wmat[:, LANES:2 * LANES], 0.0)
    return r2, y0, mmat, g0


def _mm_nt(a, b, passes):
    if passes == 1:
        return _dot_nt(a.astype(bf16), b.astype(bf16))
    ah, al = _split2(a)
    bh, bl = _split2(b)
    return (_dot_nt(al, bh) + _dot_nt(ah, bl)) + _dot_nt(ah, bh)


def _wkv_intra_kernel(r_ref, lw_ref, k_ref, v_ref, kn_ref, kb_ref, r2_o, y0_o, m_o, g_o, *, nc, passes):
    C = CHUNK
    for c in range(nc):
        sl = slice(c * C, (c + 1) * C)
        r2, y0, mmat, g0 = _wkv_chunk(r_ref[sl, :], lw_ref[sl, :], k_ref[sl, :], v_ref[sl, :],
                                      kn_ref[sl, :], kb_ref[sl, :], passes)
        r2_o[sl, :] = r2
        y0_o[sl, :] = y0
        m_o[0, c] = mmat
        g_o[0, c] = g0


def _wkv_intra(r, lw, kh, v, kn, kb, nc, passes):
    T, W = r.shape
    npair = W // LANES
    rows = nc * CHUNK
    tile = pl.BlockSpec((rows, LANES), lambda p, i: (i, p))
    mat = pl.BlockSpec((1, nc, LANES, LANES), lambda p, i: (p, i, 0, 0))
    kern = functools.partial(_wkv_intra_kernel, nc=nc, passes=passes)
    return pl.pallas_call(
        kern,
        grid=(npair, T // rows),
        in_specs=[tile] * 6,
        out_specs=[tile, tile, mat, mat],
        out_shape=[jax.ShapeDtypeStruct((T, W), f32), jax.ShapeDtypeStruct((T, W), f32),
                   jax.ShapeDtypeStruct((npair, T // CHUNK, LANES, LANES), f32),
                   jax.ShapeDtypeStruct((npair, T // CHUNK, LANES, LANES), f32)],
        compiler_params=_cparams(("arbitrary", "arbitrary")),
        name="wkv_intra",
    )(r, lw, kh, v, kn, kb)


def _wkv_state_kernel(r2_ref, y0_ref, m_ref, g0_ref, r_ref, kh_ref, v_ref, g_ref,
                      lnw_ref, lnb_ref, rk_ref, o_ref, y_ref, *, n_chunks, passes):
    C = CHUNK

    def body(c, st):
        start = pl.multiple_of(c * C, C)
        y_ref[pl.ds(start, C), :] = _mm(r2_ref[pl.ds(start, C), :], st, passes) + y0_ref[pl.ds(start, C), :]
        return _mm(m_ref[0, c], st, passes) + g0_ref[0, c]

    lax.fori_loop(0, n_chunks, body, jnp.zeros((LANES, LANES), f32))

    rr = lax.broadcasted_iota(i32, (LANES, LANES), 0)
    cc = lax.broadcasted_iota(i32, (LANES, LANES), 1)
    ones_bd = (jnp.where(rr >= RWKV_HEAD, 1, 0) == jnp.where(cc >= RWKV_HEAD, 1, 0)).astype(bf16)
    y = y_ref[...]
    mu = _mm_exact_rhs(y, ones_bd) * (1.0 / RWKV_HEAD)
    d = y - mu
    var = _mm_exact_rhs(d * d, ones_bd) * (1.0 / RWKV_HEAD)
    yn = d * lax.rsqrt(var + RWKV_GN_EPS) * lnw_ref[...] + lnb_ref[...]
    v = v_ref[...]
    bonus = _mm_exact_rhs(r_ref[...] * kh_ref[...] * rk_ref[...], ones_bd) * v
    o_ref[...] = ((yn + bonus) * g_ref[...]).astype(o_ref.dtype)


def _wkv_state(r2, y0, mm, g0, r, kh, v, g, lnw, lnb, rk, B, S, passes):
    T, W = r.shape
    npair = W // LANES
    n_chunks = S // CHUNK
    seq = pl.BlockSpec((S, LANES), lambda b, p: (b, p))
    mat = pl.BlockSpec((1, n_chunks, LANES, LANES), lambda b, p: (p, b, 0, 0))
    prow = pl.BlockSpec((1, LANES), lambda b, p: (0, p))
    kern = functools.partial(_wkv_state_kernel, n_chunks=n_chunks, passes=passes)
    return pl.pallas_call(
        kern,
        grid=(B, npair),
        in_specs=[seq, seq, mat, mat, seq, seq, seq, seq, prow, prow, prow],
        out_specs=seq,
        out_shape=jax.ShapeDtypeStruct((T, W), bf16),
        scratch_shapes=[pltpu.VMEM((S, LANES), f32)],
        compiler_params=_cparams(("arbitrary", "arbitrary")),
        name="wkv_state",
    )(r2, y0, mm, g0, r, kh, v, g, lnw, lnb, rk)


def _postmix_kernel(oa_ref, orw_ref, wa_ref, wr_ref, x_ref, ga_ref, gpost_ref, gpre_ref, sc_ref, sh_ref, wrt_ref,
                    x1_o, h2_o, lg_o):
    mixed = _dot(oa_ref[...], wa_ref[...]) + _dot(orw_ref[...], wr_ref[...])
    ms = jnp.mean(mixed * mixed, axis=-1, keepdims=True)
    x1 = x_ref[...] + ga_ref[0] * (mixed * lax.rsqrt(ms + NORM_EPS) * gpost_ref[...])
    x1_o[...] = x1
    ms1 = jnp.mean(x1 * x1, axis=-1, keepdims=True)
    h2 = (x1 * lax.rsqrt(ms1 + NORM_EPS) * gpre_ref[...]) * (1.0 + sc_ref[0]) + sh_ref[0]
    h2_o[...] = h2
    lg_o[...] = _mm_nt(wrt_ref[...], h2, 3)


def _postmix(o_att, o_rwkv, w_out_a, w_out_r, x2, mod3, g_post, g_pre, w_rt, S, tm):
    T, D = x2.shape
    WA = o_att.shape[1]
    WR = o_rwkv.shape[1]
    E = w_rt.shape[0]
    tpb = S // tm
    modspec = lambda seg: pl.BlockSpec((1, 1, D), lambda i: ((i // tpb) * N_MOD + seg, 0, 0))
    tile = pl.BlockSpec((tm, D), lambda i: (i, 0))
    return pl.pallas_call(
        _postmix_kernel,
        grid=(T // tm,),
        in_specs=[pl.BlockSpec((tm, WA), lambda i: (i, 0)),
                  pl.BlockSpec((tm, WR), lambda i: (i, 0)),
                  pl.BlockSpec((WA, D), lambda i: (0, 0)),
                  pl.BlockSpec((WR, D), lambda i: (0, 0)),
                  tile, modspec(2),
                  pl.BlockSpec((1, D), lambda i: (0, 0)),
                  pl.BlockSpec((1, D), lambda i: (0, 0)),
                  modspec(4), modspec(3),
                  pl.BlockSpec((E, D), lambda i: (0, 0))],
        out_specs=[tile, tile, pl.BlockSpec((E, tm), lambda i: (0, i))],
        out_shape=[jax.ShapeDtypeStruct((T, D), f32), jax.ShapeDtypeStruct((T, D), f32),
                   jax.ShapeDtypeStruct((E, T), f32)],
        compiler_params=_cparams(("arbitrary",)),
        name="postmix",
    )(o_att, o_rwkv, w_out_a, w_out_r, x2, mod3, g_post.reshape(1, D), g_pre.reshape(1, D), mod3, mod3, w_rt)


def _first_max(x, iota, n):
    mx = jnp.max(x, axis=0, keepdims=True)
    idx = jnp.min(jnp.where(x == mx, iota, n), axis=0, keepdims=True)
    return mx, idx


def _router_kernel(lg_ref, bias_ref, eidx_o, gate_o, rank_o, cnt_o, cnt_ref):
    E = N_EXPERTS
    G = N_GROUPS
    per = E // G
    tm = lg_ref.shape[1]

    @pl.when(pl.program_id(0) == 0)
    def _():
        cnt_ref[...] = jnp.zeros(cnt_ref.shape, f32)

    scores = _sigmoid(lg_ref[...])
    biased = scores + bias_ref[...]
    neg = -jnp.inf

    iota_p = lax.broadcasted_iota(i32, (per, tm), 0).astype(f32)
    gs = []
    for g in range(G):
        xg = biased[g * per:(g + 1) * per, :]
        m1, i1 = _first_max(xg, iota_p, per)
        m2 = jnp.max(jnp.where(iota_p == i1, neg, xg), axis=0, keepdims=True)
        gs.append(m1 + m2)
    gsc = jnp.concatenate(gs, axis=0)
    iota_g = lax.broadcasted_iota(i32, (G, tm), 0).astype(f32)
    gsel = jnp.zeros((G, tm), f32)
    for _ in range(TOPK_GROUPS):
        _, gi = _first_max(gsc, iota_g, G)
        hit = iota_g == gi
        gsel = jnp.where(hit, 1.0, gsel)
        gsc = jnp.where(hit, neg, gsc)
    masked = jnp.concatenate(
        [jnp.where(gsel[g:g + 1, :] > 0.0, biased[g * per:(g + 1) * per, :], neg) for g in range(G)], axis=0)

    iota_e = lax.broadcasted_iota(i32, (E, tm), 0).astype(f32)
    sel = jnp.zeros((E, tm), f32)
    idxs, vals = [], []
    for _ in range(TOP_K):
        _, ei = _first_max(masked, iota_e, E)
        hit = iota_e == ei
        idxs.append(ei)
        vals.append(jnp.sum(jnp.where(hit, scores, 0.0), axis=0, keepdims=True))
        sel = jnp.where(hit, 1.0, sel)
        masked = jnp.where(hit, neg, masked)
    tot = vals[0]
    for vv in vals[1:]:
        tot = tot + vv
    eidx_o[...] = jnp.concatenate(idxs, axis=0).astype(i32)
    gate_o[...] = jnp.concatenate([vv / tot * ROUTED_SCALE for vv in vals], axis=0)

    rr = lax.broadcasted_iota(i32, (tm, tm), 0)
    cc = lax.broadcasted_iota(i32, (tm, tm), 1)
    before = (rr < cc).astype(bf16)
    pos = _dot(sel.astype(bf16), before) + cnt_ref[...]
    rank_o[...] = jnp.concatenate(
        [jnp.sum(jnp.where(iota_e == ei, pos, 0.0), axis=0, keepdims=True) for ei in idxs], axis=0).astype(i32)
    cnt_ref[...] = cnt_ref[...] + jnp.sum(sel, axis=1, keepdims=True)
    cnt_o[...] = cnt_ref[...].astype(i32)


def _router(logits_t, bias, tm):
    E, T = logits_t.shape
    k_tile = pl.BlockSpec((TOP_K, tm), lambda i: (0, i))
    return pl.pallas_call(
        _router_kernel,
        grid=(T // tm,),
        in_specs=[pl.BlockSpec((E, tm), lambda i: (0, i)),
                  pl.BlockSpec((E, 1), lambda i: (0, 0))],
        out_specs=[k_tile, k_tile, k_tile, pl.BlockSpec((E, 1), lambda i: (0, 0))],
        out_shape=[jax.ShapeDtypeStruct((TOP_K, T), i32), jax.ShapeDtypeStruct((TOP_K, T), f32),
                   jax.ShapeDtypeStruct((TOP_K, T), i32), jax.ShapeDtypeStruct((E, 1), i32)],
        scratch_shapes=[pltpu.VMEM((E, 1), f32)],
        compiler_params=_cparams(("arbitrary",)),
        name="router",
    )(logits_t, bias.reshape(E, 1))


def _row_copy(src_ref, s, dst_ref, d, sem):
    return pltpu.make_async_copy(src_ref.at[pl.ds(s, 1), :], dst_ref.at[pl.ds(d, 1), :], sem)


def _dispatch_kernel(dest_hbm, h_ref, xs_in, xs_out, idx_ref, sem_idx, sem):
    del xs_in
    i = pl.program_id(0)
    tm = h_ref.shape[0]
    cp = pltpu.make_async_copy(dest_hbm.at[i], idx_ref, sem_idx)
    cp.start()
    cp.wait()

    def issue(r, carry):
        for k in range(TOP_K):
            _row_copy(h_ref, r, xs_out, idx_ref[0, r * TOP_K + k], sem).start()
        return carry

    lax.fori_loop(0, tm, issue, 0)

    def drain(r, carry):
        for k in range(TOP_K):
            _row_copy(h_ref, r, xs_out, idx_ref[0, r * TOP_K + k], sem).wait()
        return carry

    lax.fori_loop(0, tm, drain, 0)


def _dispatch(dest_tiles, h2, xs_zero, tm):
    T, D = h2.shape
    return pl.pallas_call(
        _dispatch_kernel,
        grid=(T // tm,),
        in_specs=[pl.BlockSpec(memory_space=pl.ANY),
                  pl.BlockSpec((tm, D), lambda i: (i, 0)),
                  pl.BlockSpec(memory_space=pl.ANY)],
        out_specs=pl.BlockSpec(memory_space=pl.ANY),
        out_shape=jax.ShapeDtypeStruct(xs_zero.shape, xs_zero.dtype),
        scratch_shapes=[pltpu.SMEM((1, tm * TOP_K), i32), pltpu.SemaphoreType.DMA, pltpu.SemaphoreType.DMA],
        input_output_aliases={2: 0},
        compiler_params=_cparams(("arbitrary",)),
        name="moe_dispatch",
    )(dest_tiles, h2, xs_zero)


def _experts_kernel(blk_e_ref, nused_ref, xs_ref, wg_ref, wu_ref, wd_ref, ys_ref, wg_s, wu_s, wd_s):
    i = pl.program_id(0)
    prev = blk_e_ref[jnp.maximum(i - 1, 0)]
    changed = jnp.logical_or(i == 0, blk_e_ref[i] != prev)

    @pl.when(changed)
    def _():
        wg_s[...] = wg_ref[0].astype(bf16)
        wu_s[...] = wu_ref[0].astype(bf16)
        wd_s[...] = wd_ref[0].astype(bf16)

    @pl.when(i < nused_ref[0])
    def _():
        x = xs_ref[...].astype(bf16)
        gt = _dot(x, wg_s[...])
        up = _dot(x, wu_s[...])
        hmid = (gt * _sigmoid(gt)) * up
        ys_ref[...] = _dot(hmid.astype(bf16), wd_s[...])

    @pl.when(i >= nused_ref[0])
    def _():
        ys_ref[...] = jnp.zeros(ys_ref.shape, f32)


def _experts(blk_e, nused, xs, w_gate, w_up, w_down, blk):
    P, D = xs.shape
    E, _, DE = w_gate.shape
    nblk = P // blk
    row_idx = lambda i, be, nu: (jnp.minimum(i, nu[0] - 1), 0)
    grid_spec = pltpu.PrefetchScalarGridSpec(
        num_scalar_prefetch=2,
        grid=(nblk,),
        in_specs=[pl.BlockSpec((blk, D), row_idx),
                  pl.BlockSpec((1, D, DE), lambda i, be, nu: (be[i], 0, 0)),
                  pl.BlockSpec((1, D, DE), lambda i, be, nu: (be[i], 0, 0)),
                  pl.BlockSpec((1, DE, D), lambda i, be, nu: (be[i], 0, 0))],
        out_specs=pl.BlockSpec((blk, D), lambda i, be, nu: (i, 0)),
        scratch_shapes=[pltpu.VMEM((D, DE), bf16), pltpu.VMEM((D, DE), bf16), pltpu.VMEM((DE, D), bf16)],
    )
    return pl.pallas_call(
        _experts_kernel,
        grid_spec=grid_spec,
        out_shape=jax.ShapeDtypeStruct((P, D), f32),
        compiler_params=_cparams(("arbitrary",)),
        name="moe_experts",
    )(blk_e, nused, xs, w_gate, w_up, w_down)


def _combine_kernel(dest_hbm, gate_ref, ys_hbm, h_ref, x1_ref, wsg_ref, wsu_ref, wsd_ref, gf_ref, gpost_ref,
                    x2_o, idx_ref, buf_ref, sem_idx, sem):
    i = pl.program_id(0)
    tm = h_ref.shape[0]
    cp = pltpu.make_async_copy(dest_hbm.at[i], idx_ref, sem_idx)
    cp.start()
    cp.wait()

    def issue(r, carry):
        for k in range(TOP_K):
            _row_copy(ys_hbm, idx_ref[0, r * TOP_K + k], buf_ref.at[k], r, sem).start()
        return carry

    lax.fori_loop(0, tm, issue, 0)

    hb = h_ref[...].astype(bf16)
    gt = _dot(hb, wsg_ref[...])
    up = _dot(hb, wsu_ref[...])
    y = _dot(((gt * _sigmoid(gt)) * up).astype(bf16), wsd_ref[...])

    def drain(r, carry):
        for k in range(TOP_K):
            _row_copy(ys_hbm, idx_ref[0, r * TOP_K + k], buf_ref.at[k], r, sem).wait()
        return carry

    lax.fori_loop(0, tm, drain, 0)

    gate = gate_ref[...]
    for k in range(TOP_K):
        y = y + gate[:, k:k + 1] * buf_ref[k]
    ms = jnp.mean(y * y, axis=-1, keepdims=True)
    x2_o[...] = x1_ref[...] + gf_ref[0] * (y * lax.rsqrt(ms + NORM_EPS) * gpost_ref[...])


def _combine(dest_tiles, gate_tk, ys, h2, x1, wsg, wsu, wsd, mod3, g_post, S, tm):
    T, D = h2.shape
    DS = wsg.shape[1]
    tpb = S // tm
    tile = pl.BlockSpec((tm, D), lambda i: (i, 0))
    return pl.pallas_call(
        _combine_kernel,
        grid=(T // tm,),
        in_specs=[pl.BlockSpec(memory_space=pl.ANY),
                  pl.BlockSpec((tm, TOP_K), lambda i: (i, 0)),
                  pl.BlockSpec(memory_space=pl.ANY),
                  tile, tile,
                  pl.BlockSpec((D, DS), lambda i: (0, 0)),
                  pl.BlockSpec((D, DS), lambda i: (0, 0)),
                  pl.BlockSpec((DS, D), lambda i: (0, 0)),
                  pl.BlockSpec((1, 1, D), lambda i: ((i // tpb) * N_MOD + 5, 0, 0)),
                  pl.BlockSpec((1, D), lambda i: (0, 0))],
        out_specs=tile,
        out_shape=jax.ShapeDtypeStruct((T, D), f32),
        scratch_shapes=[pltpu.SMEM((1, tm * TOP_K), i32), pltpu.VMEM((TOP_K, tm, D), f32),
                        pltpu.SemaphoreType.DMA, pltpu.SemaphoreType.DMA],
        compiler_params=_cparams(("arbitrary",)),
        name="moe_combine",
    )(dest_tiles, gate_tk, ys, h2, x1, wsg, wsu, wsd, mod3, g_post.reshape(1, D))


def _tile(n, pref):
    t = min(n, pref)
    assert n % t == 0, (n, t)
    return t


def _layer(i, x2, mod3, p, v_first, B, S, cfg):
    T, D = x2.shape
    H = (D // 2) // ATT_V_DIM
    W = D - D // 2
    att_cols = 2 * H * 2 * ATT_QK_DIM + H * ATT_V_DIM
    lam_init = 0.8 - 0.6 * math.exp(-0.3 * i)

    w_in_bf = p["w_in"].astype(bf16)
    att = _inproj(x2, mod3, p["g_pre_mix"], w_in_bf[:, :att_cols], bf16, S,
                  _tile(S, cfg["tm_in"]), cfg["tn_att"], 1, 0)
    feats = _inproj(x2, mod3, p["g_pre_mix"], w_in_bf[:, att_cols:], f32, S,
                    _tile(S, cfg["tm_in_rwkv"]), cfg["tn_rwkv"], 1, 0)

    slopes = jnp.broadcast_to(
        (2.0 ** (-ALIBI_MAX_BIAS * jnp.arange(1, H + 1, dtype=f32) / H))[:, None, None], (H, 1, LANES))
    lamp = jnp.stack([p["lam_q1"], p["lam_k1"], p["lam_q2"], p["lam_k2"]])
    o_att = _attention(att, slopes, lamp, p["att_subln_g"], B, S, H, lam_init, _tile(S, cfg["tq"]))

    cols = feats.shape[1]
    zw = jnp.zeros((RWKV_A_RANK, W), f32)
    heads = W // RWKV_HEAD
    ind = (jnp.arange(W)[:, None] // RWKV_HEAD == jnp.arange(LANES)[None, :]).astype(bf16)
    prm = {
        "mu": p["rwkv_mu"].reshape(1, cols), "w0": p["rwkv_w0"].reshape(1, W),
        "w2p": jnp.concatenate([p["rwkv_w2"], zw], axis=0),
        "a0": p["rwkv_a0"].reshape(1, W),
        "a2p": jnp.concatenate([jnp.zeros((RWKV_W_RANK, W), f32), p["rwkv_a2"]], axis=0),
        "g2": p["rwkv_g2"], "k_k": p["rwkv_k_k"].reshape(1, W), "k_a": p["rwkv_k_a"].reshape(1, W),
        "ind": ind, "indt": ind.T,
    }
    if v_first is not None:
        padc = LANES - RWKV_V_RANK
        prm["v0"] = p["rwkv_v0"].reshape(1, W)
        prm["v1p"] = jnp.pad(p["rwkv_v1"], ((0, 0), (0, padc)))
        prm["v2p"] = jnp.pad(p["rwkv_v2"], ((0, padc), (0, 0)))
    r, lw, kh, v, kn, kb, g = _rwkv_prep(feats, prm, v_first, B, S, W, _tile(S, cfg["tm_prep"]))
    if v_first is None:
        v_first = v
    r2, y0, mmat, g0 = _wkv_intra(r, lw, kh, v, kn, kb, cfg["nc"], cfg["passes_intra"])
    o_rwkv = _wkv_state(r2, y0, mmat, g0, r, kh, v, g, p["rwkv_lnx_w"].reshape(1, W),
                        p["rwkv_lnx_b"].reshape(1, W), p["rwkv_r_k"].reshape(1, W), B, S, cfg["passes_state"])
    del heads

    w_out_bf = p["w_out"].astype(bf16)
    x1, h2, logits_t = _postmix(o_att, o_rwkv, w_out_bf[:D // 2], w_out_bf[D // 2:], x2, mod3,
                                p["g_post_mix"], p["g_pre_ffn"], p["w_router"].T, S, _tile(S, cfg["tm_post"]))

    eidx_t, gate_t, rank_t, counts = _router(logits_t, p["router_bias"], _tile(T, cfg["tm_router"]))
    blk = cfg["blk"]
    counts = counts[:, 0]
    padded = (counts + blk - 1) // blk * blk
    pad_end = jnp.cumsum(padded)
    pad_start = pad_end - padded
    dest = (pad_start[eidx_t] + rank_t).T
    n_assign = T * TOP_K
    nblk = -(-n_assign // blk) + N_EXPERTS
    P = nblk * blk
    blk_start = jnp.arange(nblk, dtype=i32) * blk
    blk_e = jnp.minimum(jnp.searchsorted(pad_end, blk_start, side="right"), N_EXPERTS - 1).astype(i32)
    nused = (pad_end[-1] // blk).astype(i32).reshape(1)
    blk_e = jnp.where(jnp.arange(nblk) < nused[0], blk_e, blk_e[jnp.maximum(nused[0] - 1, 0)])

    tm_d = _tile(T, cfg["tm_disp"])
    xs = _dispatch(dest.reshape(T // tm_d, 1, tm_d * TOP_K), h2, jnp.zeros((P, D), f32), tm_d)
    ys = _experts(blk_e, nused, xs, p["w_exp_gate"], p["w_exp_up"], p["w_exp_down"], blk)
    tm_c = _tile(S, cfg["tm_comb"])
    x_out = _combine(dest.reshape(T // tm_c, 1, tm_c * TOP_K), gate_t.T, ys, h2, x1,
                     p["w_sh_gate"].astype(bf16), p["w_sh_up"].astype(bf16), p["w_sh_down"].astype(bf16),
                     mod3, p["g_post_ffn"], S, tm_c)
    return x_out, v_first


_CFG = dict(tm_in=1024, tn_att=1024, tm_in_rwkv=512, tn_rwkv=1664, tq=256, tm_prep=256, nc=4, passes_intra=1, passes_state=3,
            tm_post=256, tm_router=512, blk=256, tm_disp=128, tm_comb=128)

_LAYER_KEYS = ("g_pre_mix", "g_post_mix", "g_pre_ffn", "g_post_ffn", "w_in", "w_out", "lam_q1", "lam_k1",
               "lam_q2", "lam_k2", "att_subln_g", "rwkv_mu", "rwkv_w0", "rwkv_w2", "rwkv_a0", "rwkv_a2",
               "rwkv_g2", "rwkv_k_k", "rwkv_k_a", "rwkv_r_k", "rwkv_lnx_w", "rwkv_lnx_b", "w_router",
               "router_bias", "w_exp_gate", "w_exp_up", "w_exp_down", "w_sh_gate", "w_sh_up", "w_sh_down")


def _forward(x, c, params, cfg):
    B, S, D = x.shape
    L = params["w_in"].shape[0]
    bp = 16
    c_pad = jnp.zeros((bp, D), f32).at[:B].set(c)
    mod = _ada_mod(c_pad, params["w_ada"], params["b_ada"])
    x2 = x.reshape(B * S, D)
    v_first = None
    for i in range(L):
        p = {k: params[k][i] for k in _LAYER_KEYS}
        if i > 0:
            p["rwkv_v0"] = params["rwkv_v0"][i - 1]
            p["rwkv_v1"] = params["rwkv_v1"][i - 1]
            p["rwkv_v2"] = params["rwkv_v2"][i - 1]
        mod3 = mod[i, :B].reshape(B * N_MOD, 1, D)
        x2, v_first = _layer(i, x2, mod3, p, v_first, B, S, cfg)
    return x2.reshape(B, S, D)


def kernel(x, c, w_ada, b_ada, g_pre_mix, g_post_mix, g_pre_ffn, g_post_ffn, w_in, w_out, lam_q1, lam_k1, lam_q2, lam_k2, att_subln_g, rwkv_mu, rwkv_w0, rwkv_w2, rwkv_a0, rwkv_a2, rwkv_g2, rwkv_k_k, rwkv_k_a, rwkv_r_k, rwkv_lnx_w, rwkv_lnx_b, rwkv_v0, rwkv_v1, rwkv_v2, w_router, router_bias, w_exp_gate, w_exp_up, w_exp_down, w_sh_gate, w_sh_up, w_sh_down):
    params = dict(w_ada=w_ada, b_ada=b_ada, g_pre_mix=g_pre_mix, g_post_mix=g_post_mix, g_pre_ffn=g_pre_ffn,
                  g_post_ffn=g_post_ffn, w_in=w_in, w_out=w_out, lam_q1=lam_q1, lam_k1=lam_k1, lam_q2=lam_q2,
                  lam_k2=lam_k2, att_subln_g=att_subln_g, rwkv_mu=rwkv_mu, rwkv_w0=rwkv_w0, rwkv_w2=rwkv_w2,
                  rwkv_a0=rwkv_a0, rwkv_a2=rwkv_a2, rwkv_g2=rwkv_g2, rwkv_k_k=rwkv_k_k, rwkv_k_a=rwkv_k_a,
                  rwkv_r_k=rwkv_r_k, rwkv_lnx_w=rwkv_lnx_w, rwkv_lnx_b=rwkv_lnx_b, rwkv_v0=rwkv_v0,
                  rwkv_v1=rwkv_v1, rwkv_v2=rwkv_v2, w_router=w_router, router_bias=router_bias,
                  w_exp_gate=w_exp_gate, w_exp_up=w_exp_up, w_exp_down=w_exp_down, w_sh_gate=w_sh_gate,
                  w_sh_up=w_sh_up, w_sh_down=w_sh_down)
    return _forward(x, c, params, _CFG)
```

```python
import functools
import math

import jax
import jax.numpy as jnp
from jax import lax
from jax.experimental import pallas as pl
from jax.experimental.pallas import tpu as pltpu

f32 = jnp.float32
bf16 = jnp.bfloat16
i32 = jnp.int32

ATT_QK_DIM = 64
ATT_V_DIM = 128
ALIBI_MAX_BIAS = 8.0
ATT_SUBLN_EPS = 1e-5
RWKV_HEAD = 64
RWKV_W_RANK = 64
RWKV_A_RANK = 64
RWKV_G_RANK = 128
RWKV_V_RANK = 32
RWKV_GN_EPS = 64e-5
N_EXPERTS = 64
N_GROUPS = 8
TOPK_GROUPS = 4
TOP_K = 8
ROUTED_SCALE = 2.5
NORM_EPS = 1e-6
N_MOD = 6

LANES = 128
SUBLANES = 8
CHUNK = 64
VMEM_LIMIT = 56 * 1024 * 1024


def _cparams(sem):
    return pltpu.CompilerParams(dimension_semantics=sem, vmem_limit_bytes=VMEM_LIMIT)


def _dot(a, b):
    return jnp.dot(a, b, preferred_element_type=f32)


def _dot_nt(a, b):
    return lax.dot_general(a, b, (((1,), (1,)), ((), ())), preferred_element_type=f32)


def _split2(x):
    hi = x.astype(bf16)
    lo = (x - hi.astype(f32)).astype(bf16)
    return hi, lo


def _mm(a, b, passes=1):
    if passes == 1:
        return _dot(a.astype(bf16), b.astype(bf16))
    ah, al = _split2(a)
    bh, bl = _split2(b)
    return (_dot(al, bh) + _dot(ah, bl)) + _dot(ah, bh)


def _mm_exact_rhs(a, b_bf16):
    ah, al = _split2(a)
    return _dot(al, b_bf16) + _dot(ah, b_bf16)


def _sigmoid(x):
    return 1.0 / (1.0 + jnp.exp(-x))


def _ada_kernel(c_ref, w_ref, b_ref, o_ref):
    c = c_ref[...]
    cond = (c * _sigmoid(c)).astype(bf16)
    o_ref[0] = _dot(cond, w_ref[0].astype(bf16)) + b_ref[0]


def _ada_mod(c_pad, w_ada, b_ada, tn=1024):
    L, D, N = w_ada.shape
    bp = c_pad.shape[0]
    return pl.pallas_call(
        _ada_kernel,
        grid=(L, N // tn),
        in_specs=[pl.BlockSpec((bp, D), lambda l, j: (0, 0)),
                  pl.BlockSpec((1, D, tn), lambda l, j: (l, 0, j)),
                  pl.BlockSpec((1, 1, tn), lambda l, j: (l, 0, j))],
        out_specs=pl.BlockSpec((1, bp, tn), lambda l, j: (l, 0, j)),
        out_shape=jax.ShapeDtypeStruct((L, bp, N), f32),
        compiler_params=_cparams(("arbitrary", "arbitrary")),
        name="ada_mod",
    )(c_pad, w_ada, b_ada.reshape(L, 1, N))


def _inproj_kernel(x_ref, sc_ref, sh_ref, g_ref, w_ref, o_ref, h_ref):
    @pl.when(pl.program_id(1) == 0)
    def _():
        x = x_ref[...]
        ms = jnp.mean(x * x, axis=-1, keepdims=True)
        y = x * lax.rsqrt(ms + NORM_EPS) * g_ref[...]
        h_ref[...] = (y * (1.0 + sc_ref[0]) + sh_ref[0]).astype(bf16)

    o_ref[...] = _dot(h_ref[...], w_ref[...]).astype(o_ref.dtype)


def _inproj(x2, mod3, g, w_bf, out_dtype, S, tm, tn, seg_sc, seg_sh):
    T, D = x2.shape
    N = w_bf.shape[1]
    tpb = S // tm
    return pl.pallas_call(
        _inproj_kernel,
        grid=(T // tm, N // tn),
        in_specs=[pl.BlockSpec((tm, D), lambda i, j: (i, 0)),
                  pl.BlockSpec((1, 1, D), lambda i, j: ((i // tpb) * N_MOD + seg_sc, 0, 0)),
                  pl.BlockSpec((1, 1, D), lambda i, j: ((i // tpb) * N_MOD + seg_sh, 0, 0)),
                  pl.BlockSpec((1, D), lambda i, j: (0, 0)),
                  pl.BlockSpec((D, tn), lambda i, j: (0, j))],
        out_specs=pl.BlockSpec((tm, tn), lambda i, j: (i, j)),
        out_shape=jax.ShapeDtypeStruct((T, N), out_dtype),
        scratch_shapes=[pltpu.VMEM((tm, D), bf16)],
        compiler_params=_cparams(("arbitrary", "arbitrary")),
        name="inproj",
    )(x2, mod3, mod3, g.reshape(1, D), w_bf)


def _attn_kernel(q_ref, k_ref, v_ref, slope_ref, lamp_ref, g_ref, o_ref,
                 q2_ref, m_ref, l_ref, acc_ref, *, tq, lam_init):
    qi = pl.program_id(2)
    scale = ATT_QK_DIM ** -0.5
    slope = slope_ref[0][:, 0:1]

    q = q_ref[...].astype(f32) * scale
    lane = lax.broadcasted_iota(i32, (1, LANES), 1)
    first = lane < ATT_QK_DIM
    q2_ref[0:tq, :] = jnp.where(first, q, 0.0).astype(bf16)
    q2_ref[tq:2 * tq, :] = jnp.where(first, 0.0, q).astype(bf16)
    m_ref[...] = jnp.full(m_ref.shape, -jnp.inf, f32)
    l_ref[...] = jnp.zeros(l_ref.shape, f32)
    acc_ref[...] = jnp.zeros(acc_ref.shape, f32)

    rows = lax.broadcasted_iota(i32, (2 * tq, tq), 0)
    cols = lax.broadcasted_iota(i32, (2 * tq, tq), 1)
    rel = (jnp.where(rows >= tq, rows - tq, rows) - cols).astype(f32)

    def step(ki, masked):
        start = pl.multiple_of(ki * tq, tq)
        kb = k_ref[pl.ds(start, tq), :]
        vb = v_ref[pl.ds(start, tq), :]
        s = _dot_nt(q2_ref[...], kb)
        dist = rel + ((qi - ki) * tq).astype(f32)
        s = s - slope * dist
        if masked:
            s = jnp.where(dist >= 0.0, s, -jnp.inf)
        m_prev = m_ref[...]
        m_new = jnp.maximum(m_prev, jnp.max(s, axis=-1, keepdims=True))
        alpha = jnp.exp(m_prev - m_new)
        p = jnp.exp(s - m_new)
        l_ref[...] = alpha * l_ref[...] + jnp.sum(p, axis=-1, keepdims=True)
        acc_ref[...] = alpha * acc_ref[...] + _dot(p.astype(bf16), vb)
        m_ref[...] = m_new

    def body(ki, carry):
        step(ki, False)
        return carry

    lax.fori_loop(0, qi, body, 0)
    step(qi, True)

    lp = lamp_ref[...]
    lam = (jnp.exp(jnp.sum(lp[0:1] * lp[1:2], axis=-1, keepdims=True))
           - jnp.exp(jnp.sum(lp[2:3] * lp[3:4], axis=-1, keepdims=True)) + lam_init)
    acc = acc_ref[...]
    l = l_ref[...]
    o = acc[0:tq] / l[0:tq] - lam * (acc[tq:2 * tq] / l[tq:2 * tq])
    o = o * lax.rsqrt(jnp.mean(o * o, axis=-1, keepdims=True) + ATT_SUBLN_EPS)
    o_ref[...] = (o * g_ref[...] * (1.0 - lam_init)).astype(o_ref.dtype)


def _attention(att, slopes, lamp, subln_g, B, S, H, lam_init, tq):
    T = att.shape[0]
    nq = S // tq
    kern = functools.partial(_attn_kernel, tq=tq, lam_init=lam_init)
    return pl.pallas_call(
        kern,
        grid=(B, H, nq),
        in_specs=[pl.BlockSpec((tq, LANES), lambda b, h, q: (b * nq + q, h)),
                  pl.BlockSpec((S, LANES), lambda b, h, q: (b, H + h)),
                  pl.BlockSpec((S, LANES), lambda b, h, q: (b, 2 * H + h)),
                  pl.BlockSpec((1, 1, LANES), lambda b, h, q: (h, 0, 0)),
                  pl.BlockSpec((4, ATT_QK_DIM), lambda b, h, q: (0, 0)),
                  pl.BlockSpec((1, ATT_V_DIM), lambda b, h, q: (0, 0))],
        out_specs=pl.BlockSpec((tq, LANES), lambda b, h, q: (b * nq + q, h)),
        out_shape=jax.ShapeDtypeStruct((T, H * ATT_V_DIM), bf16),
        scratch_shapes=[pltpu.VMEM((2 * tq, LANES), bf16),
                        pltpu.VMEM((2 * tq, 1), f32),
                        pltpu.VMEM((2 * tq, 1), f32),
                        pltpu.VMEM((2 * tq, LANES), f32)],
        compiler_params=_cparams(("arbitrary", "arbitrary", "arbitrary")),
        name="diff_attention",
    )(att, att, att, slopes, lamp, subln_g.reshape(1, ATT_V_DIM))


def _head_sums(x, ind, indt):
    s = _mm_exact_rhs(x, ind)
    return _mm_exact_rhs(s, indt)


def _rwkv_prep_kernel(*refs, W, has_vres):
    if has_vres:
        (f_ref, mu_ref, w0_ref, w2_ref, a0_ref, a2_ref, g2_ref, kk_ref, ka_ref, ind_ref, indt_ref,
         vf_ref, v0_ref, v1_ref, v2_ref,
         r_o, lw_o, kh_o, v_o, kn_o, kb_o, g_o, carry_ref) = refs
    else:
        (f_ref, mu_ref, w0_ref, w2_ref, a0_ref, a2_ref, g2_ref, kk_ref, ka_ref, ind_ref, indt_ref,
         r_o, lw_o, kh_o, v_o, kn_o, kb_o, g_o, carry_ref) = refs

    ti = pl.program_id(1)
    h = f_ref[...]
    tm = h.shape[0]

    @pl.when(ti == 0)
    def _():
        carry_ref[...] = jnp.zeros(carry_ref.shape, f32)

    rolled = pltpu.roll(h, 1, axis=0)
    row = lax.broadcasted_iota(i32, (tm, 1), 0)
    prev = jnp.where(row == 0, carry_ref[...], rolled)
    carry_ref[...] = h[tm - 1:tm, :]
    feats = h + (prev - h) * mu_ref[...]

    r = feats[:, 0:W]
    k = feats[:, W:2 * W]
    v = feats[:, 2 * W:3 * W]
    wa = feats[:, 3 * W:3 * W + LANES]
    g_lo = feats[:, 3 * W + LANES:3 * W + 2 * LANES]

    w = w0_ref[...] + _mm(jnp.tanh(wa), w2_ref[...], passes=3)
    lw_o[...] = -math.exp(-0.5) * _sigmoid(w)
    a = _sigmoid(a0_ref[...] + _mm(wa, a2_ref[...], passes=3))
    g_o[...] = _mm(_sigmoid(g_lo), g2_ref[...])

    if has_vres:
        mix = _sigmoid(v0_ref[...] + _mm(_mm(v, v1_ref[...]), v2_ref[...]))
        v = v + (vf_ref[...] - v) * mix

    kk = k * kk_ref[...]
    ss = _head_sums(kk * kk, ind_ref[...], indt_ref[...])
    kk = kk / jnp.maximum(jnp.sqrt(ss), 1e-12)
    r_o[...] = r
    kh_o[...] = k * (1.0 + (a - 1.0) * ka_ref[...])
    v_o[...] = v
    kn_o[...] = kk
    kb_o[...] = kk * a


def _rwkv_prep(feats, prm, vfirst, B, S, W, tm):
    T, COLS = feats.shape
    tpb = S // tm
    has_vres = vfirst is not None
    row = lambda n: pl.BlockSpec((1, n), lambda b, t: (0, 0))
    full = lambda a: pl.BlockSpec(a.shape, lambda b, t: (0, 0))
    tile = pl.BlockSpec((tm, W), lambda b, t: (b * tpb + t, 0))
    args = [feats, prm["mu"], prm["w0"], prm["w2p"], prm["a0"], prm["a2p"], prm["g2"], prm["k_k"], prm["k_a"],
            prm["ind"], prm["indt"]]
    specs = [pl.BlockSpec((tm, COLS), lambda b, t: (b * tpb + t, 0)), row(COLS), row(W), full(prm["w2p"]),
             row(W), full(prm["a2p"]), full(prm["g2"]), row(W), row(W), full(prm["ind"]), full(prm["indt"])]
    if has_vres:
        args += [vfirst, prm["v0"], prm["v1p"], prm["v2p"]]
        specs += [tile, row(W), full(prm["v1p"]), full(prm["v2p"])]
    out = jax.ShapeDtypeStruct((T, W), f32)
    kern = functools.partial(_rwkv_prep_kernel, W=W, has_vres=has_vres)
    return pl.pallas_call(
        kern,
        grid=(B, tpb),
        in_specs=specs,
        out_specs=[tile] * 7,
        out_shape=[out] * 7,
        scratch_shapes=[pltpu.VMEM((1, COLS), f32)],
        compiler_params=_cparams(("arbitrary", "arbitrary")),
        name="rwkv_prep",
    )(*args)


def _wkv_chunk(r, lw, k, v, kn, kb, passes):
    C = CHUNK
    P2 = 2 * C
    ri = lax.broadcasted_iota(i32, (C, C), 0)
    ci = lax.broadcasted_iota(i32, (C, C), 1)
    tri = (ci <= ri).astype(bf16)
    h1 = lw.astype(bf16)
    r1 = lw - h1.astype(f32)
    h2 = r1.astype(bf16)
    h3 = (r1 - h2.astype(f32)).astype(bf16)
    cum = (_dot(tri, h3) + _dot(tri, h2)) + _dot(tri, h1)
    cumx = cum - lw
    cum_c = cum[C - 1:C, :]
    at = -kn * jnp.exp(cumx)
    rt = r * jnp.exp(cum)
    einv = jnp.exp(-cum)
    bt = kb * einv
    kt = k * einv
    eh = jnp.exp(cum_c - cum)
    bh = kb * eh
    kh = k * eh
    w_c = jnp.exp(cum_c)

    lane = lax.broadcasted_iota(i32, (1, LANES), 1)
    m0 = (lane < RWKV_HEAD).astype(f32)
    m1 = 1.0 - m0
    stack = lambda x: jnp.concatenate([x * m0, x * m1], axis=0)
    abd, rbd, bst, kst, vst = stack(at), stack(rt), stack(bt), stack(kt), stack(v)

    gram = _mm_nt(jnp.concatenate([abd, rbd], axis=0), jnp.concatenate([bst, kst], axis=0), passes)
    rr = lax.broadcasted_iota(i32, (P2, P2), 0)
    cc = lax.broadcasted_iota(i32, (P2, P2), 1)
    same = jnp.where(rr >= C, 1, 0) == jnp.where(cc >= C, 1, 0)
    strict = same & (cc < rr)
    incl = same & (cc <= rr)
    lab = jnp.where(strict, gram[0:P2, 0:P2], 0.0)
    lak = jnp.where(strict, gram[0:P2, P2:2 * P2], 0.0)
    mrb = jnp.where(incl, gram[P2:2 * P2, 0:P2], 0.0)
    mrk = jnp.where(incl, gram[P2:2 * P2, P2:2 * P2], 0.0)

    eye = (rr == cc).astype(f32)
    tinv = eye + lab
    lp = lab
    n_sq = int(math.log2(C)) - 1
    for _ in range(n_sq):
        lp = _mm(lp, lp, passes)
        tinv = tinv + _mm(lp, tinv, passes)

    x0 = _mm(lak, vst, passes)
    ta = _mm(tinv, jnp.concatenate([abd, x0], axis=1), passes)
    a2bd, u0bd = ta[:, 0:LANES], ta[:, LANES:2 * LANES]
    zeros = jnp.zeros((P2, LANES), f32)
    rhs2 = jnp.concatenate([jnp.concatenate([a2bd, u0bd], axis=1),
                            jnp.concatenate([zeros, vst], axis=1)], axis=0)
    z = _mm(jnp.concatenate([mrb, mrk], axis=1), rhs2, passes)
    fold = lambda x: x[0:C] + x[C:2 * C]
    r2 = rt + fold(z[:, 0:LANES])
    y0 = fold(z[:, LANES:2 * LANES])
    a2 = fold(a2bd)
    u0 = fold(u0bd)
    lhs3t = jnp.concatenate([bh, kh], axis=0).T
    rhs3 = jnp.concatenate([jnp.concatenate([a2, u0], axis=1),
                            jnp.concatenate([jnp.zeros((C, LANES), f32), v], axis=1)], axis=0)
    wmat = _mm(lhs3t, rhs3, passes)
    mmat = jnp.where(same, wmat[:, 0:LANES], 0.0) + eye * w_c
    g0 = jnp.where(same, wmat[:, LANES:2 * LANES], 0.0)
    return r2, y0, mmat, g0


def _mm_nt(a, b, passes):
    if passes == 1:
        return _dot_nt(a.astype(bf16), b.astype(bf16))
    ah, al = _split2(a)
    bh, bl = _split2(b)
    return (_dot_nt(al, bh) + _dot_nt(ah, bl)) + _dot_nt(ah, bh)


def _wkv_intra_kernel(r_ref, lw_ref, k_ref, v_ref, kn_ref, kb_ref, r2_o, y0_o, m_o, g_o, *, nc, passes):
    C = CHUNK
    for c in range(nc):
        sl = slice(c * C, (c + 1) * C)
        r2, y0, mmat, g0 = _wkv_chunk(r_ref[sl, :], lw_ref[sl, :], k_ref[sl, :], v_ref[sl, :],
                                      kn_ref[sl, :], kb_ref[sl, :], passes)
        r2_o[sl, :] = r2
        y0_o[sl, :] = y0
        m_o[0, c] = mmat
        g_o[0, c] = g0


def _wkv_intra(r, lw, kh, v, kn, kb, nc, passes):
    T, W = r.shape
    npair = W // LANES
    rows = nc * CHUNK
    tile = pl.BlockSpec((rows, LANES), lambda p, i: (i, p))
    mat = pl.BlockSpec((1, nc, LANES, LANES), lambda p, i: (p, i, 0, 0))
    kern = functools.partial(_wkv_intra_kernel, nc=nc, passes=passes)
    return pl.pallas_call(
        kern,
        grid=(npair, T // rows),
        in_specs=[tile] * 6,
        out_specs=[tile, tile, mat, mat],
        out_shape=[jax.ShapeDtypeStruct((T, W), f32), jax.ShapeDtypeStruct((T, W), f32),
                   jax.ShapeDtypeStruct((npair, T // CHUNK, LANES, LANES), f32),
                   jax.ShapeDtypeStruct((npair, T // CHUNK, LANES, LANES), f32)],
        compiler_params=_cparams(("arbitrary", "arbitrary")),
        name="wkv_intra",
    )(r, lw, kh, v, kn, kb)


def _wkv_state_kernel(r2_ref, y0_ref, m_ref, g0_ref, r_ref, kh_ref, v_ref, g_ref,
                      lnw_ref, lnb_ref, rk_ref, o_ref, y_ref, *, n_chunks, passes):
    C = CHUNK

    def body(c, st):
        start = pl.multiple_of(c * C, C)
        y_ref[pl.ds(start, C), :] = _mm(r2_ref[pl.ds(start, C), :], st, passes) + y0_ref[pl.ds(start, C), :]
        return _mm(m_ref[0, c], st, passes) + g0_ref[0, c]

    lax.fori_loop(0, n_chunks, body, jnp.zeros((LANES, LANES), f32))

    rr = lax.broadcasted_iota(i32, (LANES, LANES), 0)
    cc = lax.broadcasted_iota(i32, (LANES, LANES), 1)
    ones_bd = (jnp.where(rr >= RWKV_HEAD, 1, 0) == jnp.where(cc >= RWKV_HEAD, 1, 0)).astype(bf16)
    y = y_ref[...]
    mu = _mm_exact_rhs(y, ones_bd) * (1.0 / RWKV_HEAD)
    d = y - mu
    var = _mm_exact_rhs(d * d, ones_bd) * (1.0 / RWKV_HEAD)
    yn = d * lax.rsqrt(var + RWKV_GN_EPS) * lnw_ref[...] + lnb_ref[...]
    v = v_ref[...]
    bonus = _mm_exact_rhs(r_ref[...] * kh_ref[...] * rk_ref[...], ones_bd) * v
    o_ref[...] = ((yn + bonus) * g_ref[...]).astype(o_ref.dtype)


def _wkv_state(r2, y0, mm, g0, r, kh, v, g, lnw, lnb, rk, B, S, passes):
    T, W = r.shape
    npair = W // LANES
    n_chunks = S // CHUNK
    seq = pl.BlockSpec((S, LANES), lambda b, p: (b, p))
    mat = pl.BlockSpec((1, n_chunks, LANES, LANES), lambda b, p: (p, b, 0, 0))
    prow = pl.BlockSpec((1, LANES), lambda b, p: (0, p))
    kern = functools.partial(_wkv_state_kernel, n_chunks=n_chunks, passes=passes)
    return pl.pallas_call(
        kern,
        grid=(B, npair),
        in_specs=[seq, seq, mat, mat, seq, seq, seq, seq, prow, prow, prow],
        out_specs=seq,
        out_shape=jax.ShapeDtypeStruct((T, W), bf16),
        scratch_shapes=[pltpu.VMEM((S, LANES), f32)],
        compiler_params=_cparams(("arbitrary", "arbitrary")),
        name="wkv_state",
    )(r2, y0, mm, g0, r, kh, v, g, lnw, lnb, rk)


def _postmix_kernel(oa_ref, orw_ref, wa_ref, wr_ref, x_ref, ga_ref, gpost_ref, gpre_ref, sc_ref, sh_ref, wrt_ref,
                    x1_o, h2_o, lg_o):
    mixed = _dot(oa_ref[...], wa_ref[...]) + _dot(orw_ref[...], wr_ref[...])
    ms = jnp.mean(mixed * mixed, axis=-1, keepdims=True)
    x1 = x_ref[...] + ga_ref[0] * (mixed * lax.rsqrt(ms + NORM_EPS) * gpost_ref[...])
    x1_o[...] = x1
    ms1 = jnp.mean(x1 * x1, axis=-1, keepdims=True)
    h2 = (x1 * lax.rsqrt(ms1 + NORM_EPS) * gpre_ref[...]) * (1.0 + sc_ref[0]) + sh_ref[0]
    h2_o[...] = h2
    lg_o[...] = _mm_nt(wrt_ref[...], h2, 3)


def _postmix(o_att, o_rwkv, w_out_a, w_out_r, x2, mod3, g_post, g_pre, w_rt, S, tm):
    T, D = x2.shape
    WA = o_att.shape[1]
    WR = o_rwkv.shape[1]
    E = w_rt.shape[0]
    tpb = S // tm
    modspec = lambda seg: pl.BlockSpec((1, 1, D), lambda i: ((i // tpb) * N_MOD + seg, 0, 0))
    tile = pl.BlockSpec((tm, D), lambda i: (i, 0))
    return pl.pallas_call(
        _postmix_kernel,
        grid=(T // tm,),
        in_specs=[pl.BlockSpec((tm, WA), lambda i: (i, 0)),
                  pl.BlockSpec((tm, WR), lambda i: (i, 0)),
                  pl.BlockSpec((WA, D), lambda i: (0, 0)),
                  pl.BlockSpec((WR, D), lambda i: (0, 0)),
                  tile, modspec(2),
                  pl.BlockSpec((1, D), lambda i: (0, 0)),
                  pl.BlockSpec((1, D), lambda i: (0, 0)),
                  modspec(4), modspec(3),
                  pl.BlockSpec((E, D), lambda i: (0, 0))],
        out_specs=[tile, tile, pl.BlockSpec((E, tm), lambda i: (0, i))],
        out_shape=[jax.ShapeDtypeStruct((T, D), f32), jax.ShapeDtypeStruct((T, D), f32),
                   jax.ShapeDtypeStruct((E, T), f32)],
        compiler_params=_cparams(("arbitrary",)),
        name="postmix",
    )(o_att, o_rwkv, w_out_a, w_out_r, x2, mod3, g_post.reshape(1, D), g_pre.reshape(1, D), mod3, mod3, w_rt)


def _first_max(x, iota, n):
    mx = jnp.max(x, axis=0, keepdims=True)
    idx = jnp.min(jnp.where(x == mx, iota, n), axis=0, keepdims=True)
    return mx, idx


def _router_kernel(lg_ref, bias_ref, eidx_o, gate_o, rank_o, cnt_o, cnt_ref):
    E = N_EXPERTS
    G = N_GROUPS
    per = E // G
    tm = lg_ref.shape[1]

    @pl.when(pl.program_id(0) == 0)
    def _():
        cnt_ref[...] = jnp.zeros(cnt_ref.shape, f32)

    scores = _sigmoid(lg_ref[...])
    biased = scores + bias_ref[...]
    neg = -jnp.inf

    iota_p = lax.broadcasted_iota(i32, (per, tm), 0).astype(f32)
    gs = []
    for g in range(G):
        xg = biased[g * per:(g + 1) * per, :]
        m1, i1 = _first_max(xg, iota_p, per)
        m2 = jnp.max(jnp.where(iota_p == i1, neg, xg), axis=0, keepdims=True)
        gs.append(m1 + m2)
    gsc = jnp.concatenate(gs, axis=0)
    iota_g = lax.broadcasted_iota(i32, (G, tm), 0).astype(f32)
    gsel = jnp.zeros((G, tm), f32)
    for _ in range(TOPK_GROUPS):
        _, gi = _first_max(gsc, iota_g, G)
        hit = iota_g == gi
        gsel = jnp.where(hit, 1.0, gsel)
        gsc = jnp.where(hit, neg, gsc)
    masked = jnp.concatenate(
        [jnp.where(gsel[g:g + 1, :] > 0.0, biased[g * per:(g + 1) * per, :], neg) for g in range(G)], axis=0)

    iota_e = lax.broadcasted_iota(i32, (E, tm), 0).astype(f32)
    sel = jnp.zeros((E, tm), f32)
    idxs, vals = [], []
    for _ in range(TOP_K):
        _, ei = _first_max(masked, iota_e, E)
        hit = iota_e == ei
        idxs.append(ei)
        vals.append(jnp.sum(jnp.where(hit, scores, 0.0), axis=0, keepdims=True))
        sel = jnp.where(hit, 1.0, sel)
        masked = jnp.where(hit, neg, masked)
    tot = vals[0]
    for vv in vals[1:]:
        tot = tot + vv
    eidx_o[...] = jnp.concatenate(idxs, axis=0).astype(i32)
    gate_o[...] = jnp.concatenate([vv / tot * ROUTED_SCALE for vv in vals], axis=0)

    rr = lax.broadcasted_iota(i32, (tm, tm), 0)
    cc = lax.broadcasted_iota(i32, (tm, tm), 1)
    before = (rr < cc).astype(bf16)
    pos = _dot(sel.astype(bf16), before) + cnt_ref[...]
    rank_o[...] = jnp.concatenate(
        [jnp.sum(jnp.where(iota_e == ei, pos, 0.0), axis=0, keepdims=True) for ei in idxs], axis=0).astype(i32)
    cnt_ref[...] = cnt_ref[...] + jnp.sum(sel, axis=1, keepdims=True)
    cnt_o[...] = cnt_ref[...].astype(i32)


def _router(logits_t, bias, tm):
    E, T = logits_t.shape
    k_tile = pl.BlockSpec((TOP_K, tm), lambda i: (0, i))
    return pl.pallas_call(
        _router_kernel,
        grid=(T // tm,),
        in_specs=[pl.BlockSpec((E, tm), lambda i: (0, i)),
                  pl.BlockSpec((E, 1), lambda i: (0, 0))],
        out_specs=[k_tile, k_tile, k_tile, pl.BlockSpec((E, 1), lambda i: (0, 0))],
        out_shape=[jax.ShapeDtypeStruct((TOP_K, T), i32), jax.ShapeDtypeStruct((TOP_K, T), f32),
                   jax.ShapeDtypeStruct((TOP_K, T), i32), jax.ShapeDtypeStruct((E, 1), i32)],
        scratch_shapes=[pltpu.VMEM((E, 1), f32)],
        compiler_params=_cparams(("arbitrary",)),
        name="router",
    )(logits_t, bias.reshape(E, 1))


def _row_copy(src_ref, s, dst_ref, d, sem):
    return pltpu.make_async_copy(src_ref.at[pl.ds(s, 1), :], dst_ref.at[pl.ds(d, 1), :], sem)


def _zero_fill(cnt_ref, pstart_ref, nused_ref, z_ref, xs_out, sem, blk, nblk, start):
    def act(cp):
        if start:
            cp.start()
        else:
            cp.wait()

    def per_expert(e, carry):
        c = cnt_ref[e]
        base = pstart_ref[e] + c
        npad = (blk - (c & (blk - 1))) & (blk - 1)
        head = (-base) & (SUBLANES - 1)

        def one_row(j, carry2):
            act(_row_copy(z_ref, 0, xs_out, base + j, sem))
            return carry2

        lax.fori_loop(0, head, one_row, 0)
        rem = npad - head
        aligned = base + head
        p = blk // 2
        while p >= SUBLANES:
            off = pl.multiple_of(aligned + (rem & ~(2 * p - 1)), SUBLANES)

            @pl.when((rem & p) != 0)
            def _(p=p, off=off):
                act(pltpu.make_async_copy(z_ref.at[pl.ds(0, p), :], xs_out.at[pl.ds(off, p), :], sem))

            p //= 2
        return carry

    lax.fori_loop(0, N_EXPERTS, per_expert, 0)

    def per_block(b, carry):
        act(pltpu.make_async_copy(z_ref, xs_out.at[pl.ds(pl.multiple_of(b * blk, blk), blk), :], sem))
        return carry

    lax.fori_loop(nused_ref[0], nblk, per_block, 0)


def _dispatch_kernel(cnt_ref, pstart_ref, nused_ref, dest_hbm, h_ref, xs_out, idx_ref, z_ref, sem_idx, sem, sem_z,
                     *, blk, nblk):
    i = pl.program_id(0)
    tm = h_ref.shape[0]
    cp = pltpu.make_async_copy(dest_hbm.at[i], idx_ref, sem_idx)
    cp.start()
    cp.wait()

    def issue(r, carry):
        for k in range(TOP_K):
            _row_copy(h_ref, r, xs_out, idx_ref[0, r * TOP_K + k], sem).start(priority=k % 2)
        return carry

    lax.fori_loop(0, tm, issue, 0)

    @pl.when(i == pl.num_programs(0) - 1)
    def _():
        z_ref[...] = jnp.zeros(z_ref.shape, f32)
        _zero_fill(cnt_ref, pstart_ref, nused_ref, z_ref, xs_out, sem_z, blk, nblk, True)
        _zero_fill(cnt_ref, pstart_ref, nused_ref, z_ref, xs_out, sem_z, blk, nblk, False)

    for _ in range(TOP_K):
        pltpu.make_async_copy(h_ref, xs_out.at[pl.ds(0, tm), :], sem).wait()


def _dispatch(counts, pad_start, nused, dest_tiles, h2, P, blk, tm):
    T, D = h2.shape
    assert blk & (blk - 1) == 0
    kern = functools.partial(_dispatch_kernel, blk=blk, nblk=P // blk)
    grid_spec = pltpu.PrefetchScalarGridSpec(
        num_scalar_prefetch=3,
        grid=(T // tm,),
        in_specs=[pl.BlockSpec(memory_space=pl.ANY),
                  pl.BlockSpec((tm, D), lambda i, c, s, n: (i, 0))],
        out_specs=pl.BlockSpec(memory_space=pl.ANY),
        scratch_shapes=[pltpu.SMEM((1, tm * TOP_K), i32), pltpu.VMEM((blk, D), f32),
                        pltpu.SemaphoreType.DMA, pltpu.SemaphoreType.DMA, pltpu.SemaphoreType.DMA],
    )
    return pl.pallas_call(
        kern,
        grid_spec=grid_spec,
        out_shape=jax.ShapeDtypeStruct((P, D), f32),
        compiler_params=_cparams(("arbitrary",)),
        name="moe_dispatch",
    )(counts, pad_start, nused, dest_tiles, h2)


def _experts_kernel(blk_e_ref, nused_ref, xs_ref, wg_ref, wu_ref, wd_ref, ys_ref, wg_s, wu_s, wd_s):
    i = pl.program_id(0)
    prev = blk_e_ref[jnp.maximum(i - 1, 0)]
    changed = jnp.logical_or(i == 0, blk_e_ref[i] != prev)

    @pl.when(changed)
    def _():
        wg_s[...] = wg_ref[0, 0].astype(bf16)
        wu_s[...] = wu_ref[0, 0].astype(bf16)
        wd_s[...] = wd_ref[0, 0].astype(bf16)

    @pl.when(i < nused_ref[0])
    def _():
        x = xs_ref[...].astype(bf16)
        gt = _dot(x, wg_s[...])
        up = _dot(x, wu_s[...])
        hmid = (gt * _sigmoid(gt)) * up
        ys_ref[...] = _dot(hmid.astype(bf16), wd_s[...])

    @pl.when(i >= nused_ref[0])
    def _():
        ys_ref[...] = jnp.zeros(ys_ref.shape, f32)


def _experts(blk_e, nused, xs, w_gate, w_up, w_down, layer, blk):
    P, D = xs.shape
    DE = w_gate.shape[-1]
    nblk = P // blk
    row_idx = lambda i, be, nu: (jnp.minimum(i, nu[0] - 1), 0)
    grid_spec = pltpu.PrefetchScalarGridSpec(
        num_scalar_prefetch=2,
        grid=(nblk,),
        in_specs=[pl.BlockSpec((blk, D), row_idx),
                  pl.BlockSpec((1, 1, D, DE), lambda i, be, nu: (layer, be[i], 0, 0)),
                  pl.BlockSpec((1, 1, D, DE), lambda i, be, nu: (layer, be[i], 0, 0)),
                  pl.BlockSpec((1, 1, DE, D), lambda i, be, nu: (layer, be[i], 0, 0))],
        out_specs=pl.BlockSpec((blk, D), lambda i, be, nu: (i, 0)),
        scratch_shapes=[pltpu.VMEM((D, DE), bf16), pltpu.VMEM((D, DE), bf16), pltpu.VMEM((DE, D), bf16)],
    )
    return pl.pallas_call(
        _experts_kernel,
        grid_spec=grid_spec,
        out_shape=jax.ShapeDtypeStruct((P, D), f32),
        compiler_params=_cparams(("arbitrary",)),
        name="moe_experts",
    )(blk_e, nused, xs, w_gate, w_up, w_down)


def _combine_kernel(dest_hbm, gate_ref, ys_hbm, h_ref, x1_ref, wsg_ref, wsu_ref, wsd_ref, gf_ref, gpost_ref,
                    x2_o, idx_ref, buf_ref, sem_idx, sem):
    i = pl.program_id(0)
    tm = h_ref.shape[0]
    cp = pltpu.make_async_copy(dest_hbm.at[i], idx_ref, sem_idx)
    cp.start()
    cp.wait()

    def issue(r, carry):
        for k in range(TOP_K):
            _row_copy(ys_hbm, idx_ref[0, r * TOP_K + k], buf_ref.at[k], r, sem).start(priority=k % 2)
        return carry

    lax.fori_loop(0, tm, issue, 0)

    hb = h_ref[...].astype(bf16)
    gt = _dot(hb, wsg_ref[...])
    up = _dot(hb, wsu_ref[...])
    y = _dot(((gt * _sigmoid(gt)) * up).astype(bf16), wsd_ref[...])

    for k in range(TOP_K):
        pltpu.make_async_copy(ys_hbm.at[pl.ds(0, tm), :], buf_ref.at[k], sem).wait()

    gate = gate_ref[...]
    for k in range(TOP_K):
        y = y + gate[:, k:k + 1] * buf_ref[k]
    ms = jnp.mean(y * y, axis=-1, keepdims=True)
    x2_o[...] = x1_ref[...] + gf_ref[0] * (y * lax.rsqrt(ms + NORM_EPS) * gpost_ref[...])


def _combine(dest_tiles, gate_tk, ys, h2, x1, wsg, wsu, wsd, mod3, g_post, S, tm):
    T, D = h2.shape
    DS = wsg.shape[1]
    tpb = S // tm
    tile = pl.BlockSpec((tm, D), lambda i: (i, 0))
    return pl.pallas_call(
        _combine_kernel,
        grid=(T // tm,),
        in_specs=[pl.BlockSpec(memory_space=pl.ANY),
                  pl.BlockSpec((tm, TOP_K), lambda i: (i, 0)),
                  pl.BlockSpec(memory_space=pl.ANY),
                  tile, tile,
                  pl.BlockSpec((D, DS), lambda i: (0, 0)),
                  pl.BlockSpec((D, DS), lambda i: (0, 0)),
                  pl.BlockSpec((DS, D), lambda i: (0, 0)),
                  pl.BlockSpec((1, 1, D), lambda i: ((i // tpb) * N_MOD + 5, 0, 0)),
                  pl.BlockSpec((1, D), lambda i: (0, 0))],
        out_specs=tile,
        out_shape=jax.ShapeDtypeStruct((T, D), f32),
        scratch_shapes=[pltpu.SMEM((1, tm * TOP_K), i32), pltpu.VMEM((TOP_K, tm, D), f32),
                        pltpu.SemaphoreType.DMA, pltpu.SemaphoreType.DMA],
        compiler_params=_cparams(("arbitrary",)),
        name="moe_combine",
    )(dest_tiles, gate_tk, ys, h2, x1, wsg, wsu, wsd, mod3, g_post.reshape(1, D))


def _tile(n, pref):
    t = min(n, pref)
    assert n % t == 0, (n, t)
    return t


def _layer(i, x2, mod3, p, wexp, v_first, B, S, cfg):
    T, D = x2.shape
    H = (D // 2) // ATT_V_DIM
    W = D - D // 2
    att_cols = 2 * H * 2 * ATT_QK_DIM + H * ATT_V_DIM
    lam_init = 0.8 - 0.6 * math.exp(-0.3 * i)

    w_in_bf = p["w_in"].astype(bf16)
    att = _inproj(x2, mod3, p["g_pre_mix"], w_in_bf[:, :att_cols], bf16, S,
                  _tile(S, cfg["tm_in"]), cfg["tn_att"], 1, 0)
    feats = _inproj(x2, mod3, p["g_pre_mix"], w_in_bf[:, att_cols:], f32, S,
                    _tile(S, cfg["tm_in_rwkv"]), cfg["tn_rwkv"], 1, 0)

    slopes = jnp.broadcast_to(
        (2.0 ** (-ALIBI_MAX_BIAS * jnp.arange(1, H + 1, dtype=f32) / H))[:, None, None], (H, 1, LANES))
    lamp = jnp.stack([p["lam_q1"], p["lam_k1"], p["lam_q2"], p["lam_k2"]])
    o_att = _attention(att, slopes, lamp, p["att_subln_g"], B, S, H, lam_init, _tile(S, cfg["tq"]))

    cols = feats.shape[1]
    zw = jnp.zeros((RWKV_A_RANK, W), f32)
    heads = W // RWKV_HEAD
    ind = (jnp.arange(W)[:, None] // RWKV_HEAD == jnp.arange(LANES)[None, :]).astype(bf16)
    prm = {
        "mu": p["rwkv_mu"].reshape(1, cols), "w0": p["rwkv_w0"].reshape(1, W),
        "w2p": jnp.concatenate([p["rwkv_w2"], zw], axis=0),
        "a0": p["rwkv_a0"].reshape(1, W),
        "a2p": jnp.concatenate([jnp.zeros((RWKV_W_RANK, W), f32), p["rwkv_a2"]], axis=0),
        "g2": p["rwkv_g2"], "k_k": p["rwkv_k_k"].reshape(1, W), "k_a": p["rwkv_k_a"].reshape(1, W),
        "ind": ind, "indt": ind.T,
    }
    if v_first is not None:
        padc = LANES - RWKV_V_RANK
        prm["v0"] = p["rwkv_v0"].reshape(1, W)
        prm["v1p"] = jnp.pad(p["rwkv_v1"], ((0, 0), (0, padc)))
        prm["v2p"] = jnp.pad(p["rwkv_v2"], ((0, padc), (0, 0)))
    r, lw, kh, v, kn, kb, g = _rwkv_prep(feats, prm, v_first, B, S, W, _tile(S, cfg["tm_prep"]))
    if v_first is None:
        v_first = v
    r2, y0, mmat, g0 = _wkv_intra(r, lw, kh, v, kn, kb, cfg["nc"], cfg["passes_intra"])
    o_rwkv = _wkv_state(r2, y0, mmat, g0, r, kh, v, g, p["rwkv_lnx_w"].reshape(1, W),
                        p["rwkv_lnx_b"].reshape(1, W), p["rwkv_r_k"].reshape(1, W), B, S, cfg["passes_state"])
    del heads

    w_out_bf = p["w_out"].astype(bf16)
    x1, h2, logits_t = _postmix(o_att, o_rwkv, w_out_bf[:D // 2], w_out_bf[D // 2:], x2, mod3,
                                p["g_post_mix"], p["g_pre_ffn"], p["w_router"].T, S, _tile(S, cfg["tm_post"]))

    eidx_t, gate_t, rank_t, counts = _router(logits_t, p["router_bias"], _tile(T, cfg["tm_router"]))
    blk = cfg["blk"]
    counts = counts[:, 0]
    padded = (counts + blk - 1) // blk * blk
    pad_end = jnp.cumsum(padded)
    pad_start = pad_end - padded
    e_ids = jnp.arange(N_EXPERTS, dtype=i32)
    start_of = jnp.sum(jnp.where(eidx_t[:, :, None] == e_ids, pad_start, 0), axis=-1)
    dest = (start_of + rank_t).T
    n_assign = T * TOP_K
    nblk = -(-n_assign // blk) + N_EXPERTS
    P = nblk * blk
    blk_start = jnp.arange(nblk, dtype=i32) * blk
    nused = (pad_end[-1] // blk).astype(i32).reshape(1)
    blk_pos = jnp.minimum(blk_start, pad_end[-1] - blk)
    blk_e = jnp.minimum(jnp.sum((pad_end[None, :] <= blk_pos[:, None]).astype(i32), axis=1), N_EXPERTS - 1)

    tm_d = _tile(T, cfg["tm_disp"])
    xs = _dispatch(counts.astype(i32), pad_start.astype(i32), nused,
                   dest.reshape(T // tm_d, 1, tm_d * TOP_K), h2, P, blk, tm_d)
    ys = _experts(blk_e, nused, xs, wexp[0], wexp[1], wexp[2], i, blk)
    tm_c = _tile(S, cfg["tm_comb"])
    x_out = _combine(dest.reshape(T // tm_c, 1, tm_c * TOP_K), gate_t.T, ys, h2, x1,
                     p["w_sh_gate"].astype(bf16), p["w_sh_up"].astype(bf16), p["w_sh_down"].astype(bf16),
                     mod3, p["g_post_ffn"], S, tm_c)
    return x_out, v_first


_CFG = dict(tm_in=1024, tn_att=1024, tm_in_rwkv=512, tn_rwkv=1664, tq=256, tm_prep=256, nc=4, passes_intra=1, passes_state=3,
            tm_post=256, tm_router=512, blk=256, tm_disp=128, tm_comb=128)

_LAYER_KEYS = ("g_pre_mix", "g_post_mix", "g_pre_ffn", "g_post_ffn", "w_in", "w_out", "lam_q1", "lam_k1",
               "lam_q2", "lam_k2", "att_subln_g", "rwkv_mu", "rwkv_w0", "rwkv_w2", "rwkv_a0", "rwkv_a2",
               "rwkv_g2", "rwkv_k_k", "rwkv_k_a", "rwkv_r_k", "rwkv_lnx_w", "rwkv_lnx_b", "w_router",
               "router_bias", "w_sh_gate", "w_sh_up", "w_sh_down")


def _forward(x, c, params, cfg):
    B, S, D = x.shape
    L = params["w_in"].shape[0]
    bp = 16
    c_pad = jnp.zeros((bp, D), f32).at[:B].set(c)
    mod = _ada_mod(c_pad, params["w_ada"], params["b_ada"])
    x2 = x.reshape(B * S, D)
    v_first = None
    for i in range(L):
        p = {k: params[k][i] for k in _LAYER_KEYS}
        if i > 0:
            p["rwkv_v0"] = params["rwkv_v0"][i - 1]
            p["rwkv_v1"] = params["rwkv_v1"][i - 1]
            p["rwkv_v2"] = params["rwkv_v2"][i - 1]
        mod3 = mod[i, :B].reshape(B * N_MOD, 1, D)
        wexp = (params["w_exp_gate"], params["w_exp_up"], params["w_exp_down"])
        x2, v_first = _layer(i, x2, mod3, p, wexp, v_first, B, S, cfg)
    return x2.reshape(B, S, D)


def kernel(x, c, w_ada, b_ada, g_pre_mix, g_post_mix, g_pre_ffn, g_post_ffn, w_in, w_out, lam_q1, lam_k1, lam_q2, lam_k2, att_subln_g, rwkv_mu, rwkv_w0, rwkv_w2, rwkv_a0, rwkv_a2, rwkv_g2, rwkv_k_k, rwkv_k_a, rwkv_r_k, rwkv_lnx_w, rwkv_lnx_b, rwkv_v0, rwkv_v1, rwkv_v2, w_router, router_bias, w_exp_gate, w_exp_up, w_exp_down, w_sh_gate, w_sh_up, w_sh_down):
    params = dict(w_ada=w_ada, b_ada=b_ada, g_pre_mix=g_pre_mix, g_post_mix=g_post_mix, g_pre_ffn=g_pre_ffn,
                  g_post_ffn=g_post_ffn, w_in=w_in, w_out=w_out, lam_q1=lam_q1, lam_k1=lam_k1, lam_q2=lam_q2,
                  lam_k2=lam_k2, att_subln_g=att_subln_g, rwkv_mu=rwkv_mu, rwkv_w0=rwkv_w0, rwkv_w2=rwkv_w2,
                  rwkv_a0=rwkv_a0, rwkv_a2=rwkv_a2, rwkv_g2=rwkv_g2, rwkv_k_k=rwkv_k_k, rwkv_k_a=rwkv_k_a,
                  rwkv_r_k=rwkv_r_k, rwkv_lnx_w=rwkv_lnx_w, rwkv_lnx_b=rwkv_lnx_b, rwkv_v0=rwkv_v0,
                  rwkv_v1=rwkv_v1, rwkv_v2=rwkv_v2, w_router=w_router, router_bias=router_bias,
                  w_exp_gate=w_exp_gate, w_exp_up=w_exp_up, w_exp_down=w_exp_down, w_sh_gate=w_sh_gate,
                  w_sh_up=w_sh_up, w_sh_down=w_sh_down)
    return _forward(x, c, params, _CFG)
```

```python
import functools
import math

import jax
import jax.numpy as jnp
from jax import lax
from jax.experimental import pallas as pl
from jax.experimental.pallas import tpu as pltpu

f32 = jnp.float32
bf16 = jnp.bfloat16
i32 = jnp.int32

ATT_QK_DIM = 64
ATT_V_DIM = 128
ALIBI_MAX_BIAS = 8.0
ATT_SUBLN_EPS = 1e-5
RWKV_HEAD = 64
RWKV_W_RANK = 64
RWKV_A_RANK = 64
RWKV_G_RANK = 128
RWKV_V_RANK = 32
RWKV_GN_EPS = 64e-5
N_EXPERTS = 64
N_GROUPS = 8
TOPK_GROUPS = 4
TOP_K = 8
ROUTED_SCALE = 2.5
NORM_EPS = 1e-6
N_MOD = 6

LANES = 128
SUBLANES = 8
CHUNK = 64
VMEM_LIMIT = 56 * 1024 * 1024


def _cparams(sem):
    return pltpu.CompilerParams(dimension_semantics=sem, vmem_limit_bytes=VMEM_LIMIT)


def _dot(a, b):
    return jnp.dot(a, b, preferred_element_type=f32)


def _dot_nt(a, b):
    return lax.dot_general(a, b, (((1,), (1,)), ((), ())), preferred_element_type=f32)


def _split2(x):
    hi = x.astype(bf16)
    lo = (x - hi.astype(f32)).astype(bf16)
    return hi, lo


def _mm(a, b, passes=1):
    if passes == 1:
        return _dot(a.astype(bf16), b.astype(bf16))
    ah, al = _split2(a)
    bh, bl = _split2(b)
    return (_dot(al, bh) + _dot(ah, bl)) + _dot(ah, bh)


def _mm_exact_rhs(a, b_bf16):
    ah, al = _split2(a)
    return _dot(al, b_bf16) + _dot(ah, b_bf16)


def _sigmoid(x):
    return 1.0 / (1.0 + jnp.exp(-x))


def _ada_kernel(c_ref, w_ref, b_ref, o_ref):
    c = c_ref[...]
    cond = (c * _sigmoid(c)).astype(bf16)
    o_ref[0] = _dot(cond, w_ref[0].astype(bf16)) + b_ref[0]


def _ada_mod(c_pad, w_ada, b_ada, tn=1024):
    L, D, N = w_ada.shape
    bp = c_pad.shape[0]
    return pl.pallas_call(
        _ada_kernel,
        grid=(L, N // tn),
        in_specs=[pl.BlockSpec((bp, D), lambda l, j: (0, 0)),
                  pl.BlockSpec((1, D, tn), lambda l, j: (l, 0, j)),
                  pl.BlockSpec((1, 1, tn), lambda l, j: (l, 0, j))],
        out_specs=pl.BlockSpec((1, bp, tn), lambda l, j: (l, 0, j)),
        out_shape=jax.ShapeDtypeStruct((L, bp, N), f32),
        compiler_params=_cparams(("arbitrary", "arbitrary")),
        name="ada_mod",
    )(c_pad, w_ada, b_ada.reshape(L, 1, N))


def _inproj_kernel(x_ref, sc_ref, sh_ref, g_ref, w_ref, o_ref, h_ref):
    @pl.when(pl.program_id(1) == 0)
    def _():
        x = x_ref[...]
        ms = jnp.mean(x * x, axis=-1, keepdims=True)
        y = x * lax.rsqrt(ms + NORM_EPS) * g_ref[...]
        h_ref[...] = (y * (1.0 + sc_ref[0]) + sh_ref[0]).astype(bf16)

    o_ref[...] = _dot(h_ref[...], w_ref[...]).astype(o_ref.dtype)


def _inproj(x2, mod3, g, w_bf, out_dtype, S, tm, tn, seg_sc, seg_sh):
    T, D = x2.shape
    N = w_bf.shape[1]
    tpb = S // tm
    return pl.pallas_call(
        _inproj_kernel,
        grid=(T // tm, N // tn),
        in_specs=[pl.BlockSpec((tm, D), lambda i, j: (i, 0)),
                  pl.BlockSpec((1, 1, D), lambda i, j: ((i // tpb) * N_MOD + seg_sc, 0, 0)),
                  pl.BlockSpec((1, 1, D), lambda i, j: ((i // tpb) * N_MOD + seg_sh, 0, 0)),
                  pl.BlockSpec((1, D), lambda i, j: (0, 0)),
                  pl.BlockSpec((D, tn), lambda i, j: (0, j))],
        out_specs=pl.BlockSpec((tm, tn), lambda i, j: (i, j)),
        out_shape=jax.ShapeDtypeStruct((T, N), out_dtype),
        scratch_shapes=[pltpu.VMEM((tm, D), bf16)],
        compiler_params=_cparams(("arbitrary", "arbitrary")),
        name="inproj",
    )(x2, mod3, mod3, g.reshape(1, D), w_bf)


def _attn_kernel(q_ref, k_ref, v_ref, slope_ref, lamp_ref, g_ref, o_ref,
                 q2t_ref, vt_ref, m_ref, l_ref, acc_ref, *, tq, lam_init):
    qi = pl.program_id(2)
    scale = ATT_QK_DIM ** -0.5
    slope = slope_ref[0][:, 0:1]

    @pl.when(qi == 0)
    def _():
        vt_ref[...] = v_ref[...].astype(f32).T.astype(bf16)

    qt = (q_ref[...].astype(f32) * scale).T
    dim = lax.broadcasted_iota(i32, (LANES, 1), 0)
    first = dim < ATT_QK_DIM
    q2t_ref[:, 0:tq] = jnp.where(first, qt, 0.0).astype(bf16)
    q2t_ref[:, tq:2 * tq] = jnp.where(first, 0.0, qt).astype(bf16)
    m_ref[...] = jnp.full(m_ref.shape, -jnp.inf, f32)
    l_ref[...] = jnp.zeros(l_ref.shape, f32)
    acc_ref[...] = jnp.zeros(acc_ref.shape, f32)

    kr = lax.broadcasted_iota(i32, (tq, 2 * tq), 0)
    qc = lax.broadcasted_iota(i32, (tq, 2 * tq), 1)
    rel = (jnp.where(qc >= tq, qc - tq, qc) - kr).astype(f32)

    def step(ki, masked):
        start = pl.multiple_of(ki * tq, tq)
        kb = k_ref[pl.ds(start, tq), :]
        vtb = vt_ref[:, pl.ds(start, tq)]
        s = _dot(kb, q2t_ref[...])
        dist = rel + ((qi - ki) * tq).astype(f32)
        s = s - slope * dist
        if masked:
            s = jnp.where(dist >= 0.0, s, -jnp.inf)
        m_prev = m_ref[...]
        m_new = jnp.maximum(m_prev, jnp.max(s, axis=0, keepdims=True))
        alpha = jnp.exp(m_prev - m_new)
        p = jnp.exp(s - m_new)
        l_ref[...] = alpha * l_ref[...] + jnp.sum(p, axis=0, keepdims=True)
        acc_ref[...] = alpha * acc_ref[...] + _dot(vtb, p.astype(bf16))
        m_ref[...] = m_new

    def body(ki, carry):
        step(ki, False)
        return carry

    lax.fori_loop(0, qi, body, 0)
    step(qi, True)

    lp = lamp_ref[...]
    lam = (jnp.exp(jnp.sum(lp[0:1] * lp[1:2], axis=-1, keepdims=True))
           - jnp.exp(jnp.sum(lp[2:3] * lp[3:4], axis=-1, keepdims=True)) + lam_init)
    on = acc_ref[...] * (1.0 / l_ref[...])
    o = on[:, 0:tq] - lam * on[:, tq:2 * tq]
    o = o * lax.rsqrt(jnp.mean(o * o, axis=0, keepdims=True) + ATT_SUBLN_EPS)
    o = o * g_ref[...] * (1.0 - lam_init)
    o_ref[...] = o.T.astype(o_ref.dtype)


def _attention(att, slopes, lamp, subln_g, B, S, H, lam_init, tq):
    T = att.shape[0]
    nq = S // tq
    kern = functools.partial(_attn_kernel, tq=tq, lam_init=lam_init)
    return pl.pallas_call(
        kern,
        grid=(B, H, nq),
        in_specs=[pl.BlockSpec((tq, LANES), lambda b, h, q: (b * nq + q, h)),
                  pl.BlockSpec((S, LANES), lambda b, h, q: (b, H + h)),
                  pl.BlockSpec((S, LANES), lambda b, h, q: (b, 2 * H + h)),
                  pl.BlockSpec((1, 1, LANES), lambda b, h, q: (h, 0, 0)),
                  pl.BlockSpec((4, ATT_QK_DIM), lambda b, h, q: (0, 0)),
                  pl.BlockSpec((ATT_V_DIM, 1), lambda b, h, q: (0, 0))],
        out_specs=pl.BlockSpec((tq, LANES), lambda b, h, q: (b * nq + q, h)),
        out_shape=jax.ShapeDtypeStruct((T, H * ATT_V_DIM), bf16),
        scratch_shapes=[pltpu.VMEM((LANES, 2 * tq), bf16),
                        pltpu.VMEM((LANES, S), bf16),
                        pltpu.VMEM((1, 2 * tq), f32),
                        pltpu.VMEM((1, 2 * tq), f32),
                        pltpu.VMEM((LANES, 2 * tq), f32)],
        compiler_params=_cparams(("arbitrary", "arbitrary", "arbitrary")),
        name="diff_attention",
    )(att, att, att, slopes, lamp, subln_g.reshape(ATT_V_DIM, 1))


def _head_sums(x, ind, indt):
    s = _mm_exact_rhs(x, ind)
    return _mm_exact_rhs(s, indt)


def _rwkv_prep_kernel(*refs, W, has_vres):
    if has_vres:
        (f_ref, mu_ref, w0_ref, w2_ref, a0_ref, a2_ref, g2_ref, kk_ref, ka_ref, ind_ref, indt_ref,
         vf_ref, v0_ref, v1_ref, v2_ref,
         r_o, lw_o, kh_o, v_o, kn_o, kb_o, g_o, carry_ref) = refs
    else:
        (f_ref, mu_ref, w0_ref, w2_ref, a0_ref, a2_ref, g2_ref, kk_ref, ka_ref, ind_ref, indt_ref,
         r_o, lw_o, kh_o, v_o, kn_o, kb_o, g_o, carry_ref) = refs

    ti = pl.program_id(1)
    h = f_ref[...]
    tm = h.shape[0]

    @pl.when(ti == 0)
    def _():
        carry_ref[...] = jnp.zeros(carry_ref.shape, f32)

    rolled = pltpu.roll(h, 1, axis=0)
    row = lax.broadcasted_iota(i32, (tm, 1), 0)
    prev = jnp.where(row == 0, carry_ref[...], rolled)
    carry_ref[...] = h[tm - 1:tm, :]
    feats = h + (prev - h) * mu_ref[...]

    r = feats[:, 0:W]
    k = feats[:, W:2 * W]
    v = feats[:, 2 * W:3 * W]
    wa = feats[:, 3 * W:3 * W + LANES]
    g_lo = feats[:, 3 * W + LANES:3 * W + 2 * LANES]

    w = w0_ref[...] + _mm(jnp.tanh(wa), w2_ref[...], passes=3)
    lw_o[...] = -math.exp(-0.5) * _sigmoid(w)
    a = _sigmoid(a0_ref[...] + _mm(wa, a2_ref[...], passes=3))
    g_o[...] = _mm(_sigmoid(g_lo), g2_ref[...])

    if has_vres:
        mix = _sigmoid(v0_ref[...] + _mm(_mm(v, v1_ref[...]), v2_ref[...]))
        v = v + (vf_ref[...] - v) * mix

    kk = k * kk_ref[...]
    ss = _head_sums(kk * kk, ind_ref[...], indt_ref[...])
    kk = kk / jnp.maximum(jnp.sqrt(ss), 1e-12)
    r_o[...] = r
    kh_o[...] = k * (1.0 + (a - 1.0) * ka_ref[...])
    v_o[...] = v
    kn_o[...] = kk
    kb_o[...] = kk * a


def _rwkv_prep(feats, prm, vfirst, B, S, W, tm):
    T, COLS = feats.shape
    tpb = S // tm
    has_vres = vfirst is not None
    row = lambda n: pl.BlockSpec((1, n), lambda b, t: (0, 0))
    full = lambda a: pl.BlockSpec(a.shape, lambda b, t: (0, 0))
    tile = pl.BlockSpec((tm, W), lambda b, t: (b * tpb + t, 0))
    args = [feats, prm["mu"], prm["w0"], prm["w2p"], prm["a0"], prm["a2p"], prm["g2"], prm["k_k"], prm["k_a"],
            prm["ind"], prm["indt"]]
    specs = [pl.BlockSpec((tm, COLS), lambda b, t: (b * tpb + t, 0)), row(COLS), row(W), full(prm["w2p"]),
             row(W), full(prm["a2p"]), full(prm["g2"]), row(W), row(W), full(prm["ind"]), full(prm["indt"])]
    if has_vres:
        args += [vfirst, prm["v0"], prm["v1p"], prm["v2p"]]
        specs += [tile, row(W), full(prm["v1p"]), full(prm["v2p"])]
    out = jax.ShapeDtypeStruct((T, W), f32)
    kern = functools.partial(_rwkv_prep_kernel, W=W, has_vres=has_vres)
    return pl.pallas_call(
        kern,
        grid=(B, tpb),
        in_specs=specs,
        out_specs=[tile] * 7,
        out_shape=[out] * 7,
        scratch_shapes=[pltpu.VMEM((1, COLS), f32)],
        compiler_params=_cparams(("arbitrary", "arbitrary")),
        name="rwkv_prep",
    )(*args)


def _wkv_chunks(rs, lws, ks, vs, kns, kbs, passes):
    C = CHUNK
    P2 = 2 * C
    n = range(len(rs))
    ri = lax.broadcasted_iota(i32, (C, C), 0)
    ci = lax.broadcasted_iota(i32, (C, C), 1)
    tri = (ci <= ri).astype(bf16)
    lane = lax.broadcasted_iota(i32, (1, LANES), 1)
    m0 = (lane < RWKV_HEAD).astype(f32)
    m1 = 1.0 - m0
    rr = lax.broadcasted_iota(i32, (P2, P2), 0)
    cc = lax.broadcasted_iota(i32, (P2, P2), 1)
    same = jnp.where(rr >= C, 1, 0) == jnp.where(cc >= C, 1, 0)
    strict = same & (cc < rr)
    incl = same & (cc <= rr)
    incl2 = jnp.concatenate([incl, incl], axis=1)
    eye = (rr == cc).astype(f32)
    zeros_p = jnp.zeros((P2, LANES), f32)
    zeros_c = jnp.zeros((C, LANES), f32)
    stack = lambda x: jnp.concatenate([x * m0, x * m1], axis=0)
    fold = lambda x: x[0:C] + x[C:2 * C]

    def cumsum(lw):
        h1 = lw.astype(bf16)
        r1 = lw - h1.astype(f32)
        h2 = r1.astype(bf16)
        h3 = (r1 - h2.astype(f32)).astype(bf16)
        return (_dot(tri, h3) + _dot(tri, h2)) + _dot(tri, h1)

    cum = [cumsum(lws[j]) for j in n]
    cum_c = [cum[j][C - 1:C, :] for j in n]
    at = [-kns[j] * jnp.exp(cum[j] - lws[j]) for j in n]
    rt = [rs[j] * jnp.exp(cum[j]) for j in n]
    einv = [jnp.exp(-cum[j]) for j in n]
    bt = [kbs[j] * einv[j] for j in n]
    kt = [ks[j] * einv[j] for j in n]
    eh = [jnp.exp(cum_c[j] - cum[j]) for j in n]
    bh = [kbs[j] * eh[j] for j in n]
    kh = [ks[j] * eh[j] for j in n]
    w_c = [jnp.exp(cum_c[j]) for j in n]
    abd = [stack(at[j]) for j in n]
    vst = [stack(vs[j]) for j in n]
    lhs = [jnp.concatenate([abd[j], stack(rt[j])], axis=0) for j in n]
    rhs = [jnp.concatenate([stack(bt[j]), stack(kt[j])], axis=0) for j in n]
    gram = [_mm_nt(lhs[j], rhs[j], passes) for j in n]
    lab = [jnp.where(strict, gram[j][0:P2, 0:P2], 0.0) for j in n]
    lak = [jnp.where(strict, gram[j][0:P2, P2:2 * P2], 0.0) for j in n]
    mrbk = [jnp.where(incl2, gram[j][P2:2 * P2, :], 0.0) for j in n]

    x0 = [_mm(lak[j], vst[j], passes) for j in n]
    tinv = [eye + lab[j] for j in n]
    lp = lab
    for _ in range(int(math.log2(C)) - 1):
        lp = [_mm(lp[j], lp[j], passes) for j in n]
        tinv = [tinv[j] + _mm(lp[j], tinv[j], passes) for j in n]

    ta = [_mm(tinv[j], jnp.concatenate([abd[j], x0[j]], axis=1), passes) for j in n]
    rhs2 = [jnp.concatenate([ta[j], jnp.concatenate([zeros_p, vst[j]], axis=1)], axis=0) for j in n]
    z = [_mm(mrbk[j], rhs2[j], passes) for j in n]
    r2 = [rt[j] + fold(z[j][:, 0:LANES]) for j in n]
    y0 = [fold(z[j][:, LANES:2 * LANES]) for j in n]
    lhs3t = [jnp.concatenate([bh[j], kh[j]], axis=0).T for j in n]
    rhs3 = [jnp.concatenate([fold(ta[j]), jnp.concatenate([zeros_c, vs[j]], axis=1)], axis=0) for j in n]
    wmat = [_mm(lhs3t[j], rhs3[j], passes) for j in n]
    mmat = [jnp.where(same, wmat[j][:, 0:LANES], 0.0) + eye * w_c[j] for j in n]
    g0 = [jnp.where(same, wmat[j][:, LANES:2 * LANES], 0.0) for j in n]
    return r2, y0, mmat, g0


def _mm_nt(a, b, passes):
    if passes == 1:
        return _dot_nt(a.astype(bf16), b.astype(bf16))
    ah, al = _split2(a)
    bh, bl = _split2(b)
    return (_dot_nt(al, bh) + _dot_nt(ah, bl)) + _dot_nt(ah, bh)


def _wkv_intra_kernel(r_ref, lw_ref, k_ref, v_ref, kn_ref, kb_ref, r2_o, y0_o, m_o, g_o, *, nc, passes):
    C = CHUNK
    sls = [slice(c * C, (c + 1) * C) for c in range(nc)]
    take = lambda ref: [ref[sl, :] for sl in sls]
    r2, y0, mmat, g0 = _wkv_chunks(take(r_ref), take(lw_ref), take(k_ref), take(v_ref),
                                   take(kn_ref), take(kb_ref), passes)
    for c, sl in enumerate(sls):
        r2_o[sl, :] = r2[c]
        y0_o[sl, :] = y0[c]
        m_o[0, c] = mmat[c]
        g_o[0, c] = g0[c]


def _wkv_intra(r, lw, kh, v, kn, kb, nc, passes):
    T, W = r.shape
    npair = W // LANES
    rows = nc * CHUNK
    tile = pl.BlockSpec((rows, LANES), lambda p, i: (i, p))
    mat = pl.BlockSpec((1, nc, LANES, LANES), lambda p, i: (p, i, 0, 0))
    kern = functools.partial(_wkv_intra_kernel, nc=nc, passes=passes)
    return pl.pallas_call(
        kern,
        grid=(npair, T // rows),
        in_specs=[tile] * 6,
        out_specs=[tile, tile, mat, mat],
        out_shape=[jax.ShapeDtypeStruct((T, W), f32), jax.ShapeDtypeStruct((T, W), f32),
                   jax.ShapeDtypeStruct((npair, T // CHUNK, LANES, LANES), f32),
                   jax.ShapeDtypeStruct((npair, T // CHUNK, LANES, LANES), f32)],
        compiler_params=_cparams(("arbitrary", "arbitrary")),
        name="wkv_intra",
    )(r, lw, kh, v, kn, kb)


def _wkv_state_kernel(r2_ref, y0_ref, m_ref, g0_ref, r_ref, kh_ref, v_ref, g_ref,
                      lnw_ref, lnb_ref, rk_ref, o_ref, st_ref, y_ref, *, pg, cb, passes):
    C = CHUNK

    @pl.when(pl.program_id(2) == 0)
    def _():
        st_ref[...] = jnp.zeros(st_ref.shape, f32)

    pairs = range(pg)
    lanes = [slice(p * LANES, (p + 1) * LANES) for p in pairs]
    st = [st_ref[p] for p in pairs]
    for c in range(cb):
        rows = slice(c * C, (c + 1) * C)
        for p in pairs:
            y_ref[rows, lanes[p]] = _mm(r2_ref[rows, lanes[p]], st[p], passes) + y0_ref[rows, lanes[p]]
        st = [_mm(m_ref[p, c], st[p], passes) + g0_ref[p, c] for p in pairs]
    for p in pairs:
        st_ref[p] = st[p]

    rr = lax.broadcasted_iota(i32, (LANES, LANES), 0)
    cc = lax.broadcasted_iota(i32, (LANES, LANES), 1)
    ones_bd = (jnp.where(rr >= RWKV_HEAD, 1, 0) == jnp.where(cc >= RWKV_HEAD, 1, 0)).astype(bf16)
    for p in pairs:
        y = y_ref[:, lanes[p]]
        mu = _mm_exact_rhs(y, ones_bd) * (1.0 / RWKV_HEAD)
        d = y - mu
        var = _mm_exact_rhs(d * d, ones_bd) * (1.0 / RWKV_HEAD)
        yn = d * lax.rsqrt(var + RWKV_GN_EPS) * lnw_ref[:, lanes[p]] + lnb_ref[:, lanes[p]]
        rk = r_ref[:, lanes[p]] * kh_ref[:, lanes[p]] * rk_ref[:, lanes[p]]
        bonus = _mm_exact_rhs(rk, ones_bd) * v_ref[:, lanes[p]]
        o_ref[:, lanes[p]] = ((yn + bonus) * g_ref[:, lanes[p]]).astype(o_ref.dtype)


def _wkv_state(r2, y0, mm, g0, r, kh, v, g, lnw, lnb, rk, B, S, pg, cb, passes):
    T, W = r.shape
    npair = W // LANES
    rows = cb * CHUNK
    steps = S // rows
    seq = pl.BlockSpec((rows, pg * LANES), lambda b, q, c: (b * steps + c, q))
    mat = pl.BlockSpec((pg, cb, LANES, LANES), lambda b, q, c: (q, b * steps + c, 0, 0))
    prow = pl.BlockSpec((1, pg * LANES), lambda b, q, c: (0, q))
    kern = functools.partial(_wkv_state_kernel, pg=pg, cb=cb, passes=passes)
    return pl.pallas_call(
        kern,
        grid=(B, npair // pg, steps),
        in_specs=[seq, seq, mat, mat, seq, seq, seq, seq, prow, prow, prow],
        out_specs=seq,
        out_shape=jax.ShapeDtypeStruct((T, W), bf16),
        scratch_shapes=[pltpu.VMEM((pg, LANES, LANES), f32), pltpu.VMEM((rows, pg * LANES), f32)],
        compiler_params=_cparams(("arbitrary", "arbitrary", "arbitrary")),
        name="wkv_state",
    )(r2, y0, mm, g0, r, kh, v, g, lnw, lnb, rk)


def _postmix_kernel(oa_ref, orw_ref, wa_ref, wr_ref, x_ref, ga_ref, gpost_ref, gpre_ref, sc_ref, sh_ref, wrt_ref,
                    x1_o, h2_o, lg_o):
    mixed = _dot(oa_ref[...], wa_ref[...]) + _dot(orw_ref[...], wr_ref[...])
    ms = jnp.mean(mixed * mixed, axis=-1, keepdims=True)
    x1 = x_ref[...] + ga_ref[0] * (mixed * lax.rsqrt(ms + NORM_EPS) * gpost_ref[...])
    x1_o[...] = x1
    ms1 = jnp.mean(x1 * x1, axis=-1, keepdims=True)
    h2 = (x1 * lax.rsqrt(ms1 + NORM_EPS) * gpre_ref[...]) * (1.0 + sc_ref[0]) + sh_ref[0]
    h2_o[...] = h2
    lg_o[...] = _mm_nt(wrt_ref[...], h2, 3)


def _postmix(o_att, o_rwkv, w_out_a, w_out_r, x2, mod3, g_post, g_pre, w_rt, S, tm):
    T, D = x2.shape
    WA = o_att.shape[1]
    WR = o_rwkv.shape[1]
    E = w_rt.shape[0]
    tpb = S // tm
    modspec = lambda seg: pl.BlockSpec((1, 1, D), lambda i: ((i // tpb) * N_MOD + seg, 0, 0))
    tile = pl.BlockSpec((tm, D), lambda i: (i, 0))
    return pl.pallas_call(
        _postmix_kernel,
        grid=(T // tm,),
        in_specs=[pl.BlockSpec((tm, WA), lambda i: (i, 0)),
                  pl.BlockSpec((tm, WR), lambda i: (i, 0)),
                  pl.BlockSpec((WA, D), lambda i: (0, 0)),
                  pl.BlockSpec((WR, D), lambda i: (0, 0)),
                  tile, modspec(2),
                  pl.BlockSpec((1, D), lambda i: (0, 0)),
                  pl.BlockSpec((1, D), lambda i: (0, 0)),
                  modspec(4), modspec(3),
                  pl.BlockSpec((E, D), lambda i: (0, 0))],
        out_specs=[tile, tile, pl.BlockSpec((E, tm), lambda i: (0, i))],
        out_shape=[jax.ShapeDtypeStruct((T, D), f32), jax.ShapeDtypeStruct((T, D), f32),
                   jax.ShapeDtypeStruct((E, T), f32)],
        compiler_params=_cparams(("arbitrary",)),
        name="postmix",
    )(o_att, o_rwkv, w_out_a, w_out_r, x2, mod3, g_post.reshape(1, D), g_pre.reshape(1, D), mod3, mod3, w_rt)


def _first_max(x, iota, n):
    mx = jnp.max(x, axis=0, keepdims=True)
    idx = jnp.min(jnp.where(x == mx, iota, n), axis=0, keepdims=True)
    return mx, idx


def _router_kernel(lg_ref, bias_ref, eidx_o, gate_o, rank_o, cnt_o, cnt_ref):
    E = N_EXPERTS
    G = N_GROUPS
    per = E // G
    tm = lg_ref.shape[1]

    @pl.when(pl.program_id(0) == 0)
    def _():
        cnt_ref[...] = jnp.zeros(cnt_ref.shape, f32)

    scores = _sigmoid(lg_ref[...])
    biased = scores + bias_ref[...]
    neg = -jnp.inf

    iota_p = lax.broadcasted_iota(i32, (per, tm), 0).astype(f32)
    gs = []
    for g in range(G):
        xg = biased[g * per:(g + 1) * per, :]
        m1, i1 = _first_max(xg, iota_p, per)
        m2 = jnp.max(jnp.where(iota_p == i1, neg, xg), axis=0, keepdims=True)
        gs.append(m1 + m2)
    gsc = jnp.concatenate(gs, axis=0)
    iota_g = lax.broadcasted_iota(i32, (G, tm), 0).astype(f32)
    gsel = jnp.zeros((G, tm), f32)
    for _ in range(TOPK_GROUPS):
        _, gi = _first_max(gsc, iota_g, G)
        hit = iota_g == gi
        gsel = jnp.where(hit, 1.0, gsel)
        gsc = jnp.where(hit, neg, gsc)
    masked = jnp.concatenate(
        [jnp.where(gsel[g:g + 1, :] > 0.0, biased[g * per:(g + 1) * per, :], neg) for g in range(G)], axis=0)

    iota_e = lax.broadcasted_iota(i32, (E, tm), 0).astype(f32)
    sel = jnp.zeros((E, tm), f32)
    idxs, vals = [], []
    for _ in range(TOP_K):
        _, ei = _first_max(masked, iota_e, E)
        hit = iota_e == ei
        idxs.append(ei)
        vals.append(jnp.sum(jnp.where(hit, scores, 0.0), axis=0, keepdims=True))
        sel = jnp.where(hit, 1.0, sel)
        masked = jnp.where(hit, neg, masked)
    tot = vals[0]
    for vv in vals[1:]:
        tot = tot + vv
    eidx_o[...] = jnp.concatenate(idxs, axis=0).astype(i32)
    gate_o[...] = jnp.concatenate([vv / tot * ROUTED_SCALE for vv in vals], axis=0)

    rr = lax.broadcasted_iota(i32, (tm, tm), 0)
    cc = lax.broadcasted_iota(i32, (tm, tm), 1)
    before = (rr < cc).astype(bf16)
    pos = _dot(sel.astype(bf16), before) + cnt_ref[...]
    rank_o[...] = jnp.concatenate(
        [jnp.sum(jnp.where(iota_e == ei, pos, 0.0), axis=0, keepdims=True) for ei in idxs], axis=0).astype(i32)
    cnt_ref[...] = cnt_ref[...] + jnp.sum(sel, axis=1, keepdims=True)
    cnt_o[...] = cnt_ref[...].astype(i32)


def _router(logits_t, bias, tm):
    E, T = logits_t.shape
    k_tile = pl.BlockSpec((TOP_K, tm), lambda i: (0, i))
    return pl.pallas_call(
        _router_kernel,
        grid=(T // tm,),
        in_specs=[pl.BlockSpec((E, tm), lambda i: (0, i)),
                  pl.BlockSpec((E, 1), lambda i: (0, 0))],
        out_specs=[k_tile, k_tile, k_tile, pl.BlockSpec((E, 1), lambda i: (0, 0))],
        out_shape=[jax.ShapeDtypeStruct((TOP_K, T), i32), jax.ShapeDtypeStruct((TOP_K, T), f32),
                   jax.ShapeDtypeStruct((TOP_K, T), i32), jax.ShapeDtypeStruct((E, 1), i32)],
        scratch_shapes=[pltpu.VMEM((E, 1), f32)],
        compiler_params=_cparams(("arbitrary",)),
        name="router",
    )(logits_t, bias.reshape(E, 1))


def _row_copy(src_ref, s, dst_ref, d, sem):
    return pltpu.make_async_copy(src_ref.at[pl.ds(s, 1), :], dst_ref.at[pl.ds(d, 1), :], sem)


def _zero_fill(cnt_ref, pstart_ref, nused_ref, z_ref, xs_out, sem, blk, nblk, start):
    def act(cp):
        if start:
            cp.start()
        else:
            cp.wait()

    def per_expert(e, carry):
        c = cnt_ref[e]
        base = pstart_ref[e] + c
        npad = (blk - (c & (blk - 1))) & (blk - 1)
        head = (-base) & (SUBLANES - 1)

        def one_row(j, carry2):
            act(_row_copy(z_ref, 0, xs_out, base + j, sem))
            return carry2

        lax.fori_loop(0, head, one_row, 0)
        rem = npad - head
        aligned = base + head
        p = blk // 2
        while p >= SUBLANES:
            off = pl.multiple_of(aligned + (rem & ~(2 * p - 1)), SUBLANES)

            @pl.when((rem & p) != 0)
            def _(p=p, off=off):
                act(pltpu.make_async_copy(z_ref.at[pl.ds(0, p), :], xs_out.at[pl.ds(off, p), :], sem))

            p //= 2
        return carry

    lax.fori_loop(0, N_EXPERTS, per_expert, 0)

    def per_block(b, carry):
        act(pltpu.make_async_copy(z_ref, xs_out.at[pl.ds(pl.multiple_of(b * blk, blk), blk), :], sem))
        return carry

    lax.fori_loop(nused_ref[0], nblk, per_block, 0)


def _dispatch_kernel(cnt_ref, pstart_ref, nused_ref, dest_hbm, h_ref, xs_out, idx_ref, z_ref, sem_idx, sem, sem_z,
                     *, blk, nblk):
    i = pl.program_id(0)
    tm = h_ref.shape[0]
    cp = pltpu.make_async_copy(dest_hbm.at[i], idx_ref, sem_idx)
    cp.start()
    cp.wait()

    def issue(r, carry):
        for k in range(TOP_K):
            _row_copy(h_ref, r, xs_out, idx_ref[0, r * TOP_K + k], sem).start(priority=k % 2)
        return carry

    lax.fori_loop(0, tm, issue, 0)

    @pl.when(i == pl.num_programs(0) - 1)
    def _():
        z_ref[...] = jnp.zeros(z_ref.shape, f32)
        _zero_fill(cnt_ref, pstart_ref, nused_ref, z_ref, xs_out, sem_z, blk, nblk, True)
        _zero_fill(cnt_ref, pstart_ref, nused_ref, z_ref, xs_out, sem_z, blk, nblk, False)

    for _ in range(TOP_K):
        pltpu.make_async_copy(h_ref, xs_out.at[pl.ds(0, tm), :], sem).wait()


def _dispatch(counts, pad_start, nused, dest_tiles, h2, P, blk, tm):
    T, D = h2.shape
    assert blk & (blk - 1) == 0
    kern = functools.partial(_dispatch_kernel, blk=blk, nblk=P // blk)
    grid_spec = pltpu.PrefetchScalarGridSpec(
        num_scalar_prefetch=3,
        grid=(T // tm,),
        in_specs=[pl.BlockSpec(memory_space=pl.ANY),
                  pl.BlockSpec((tm, D), lambda i, c, s, n: (i, 0))],
        out_specs=pl.BlockSpec(memory_space=pl.ANY),
        scratch_shapes=[pltpu.SMEM((1, tm * TOP_K), i32), pltpu.VMEM((blk, D), f32),
                        pltpu.SemaphoreType.DMA, pltpu.SemaphoreType.DMA, pltpu.SemaphoreType.DMA],
    )
    return pl.pallas_call(
        kern,
        grid_spec=grid_spec,
        out_shape=jax.ShapeDtypeStruct((P, D), f32),
        compiler_params=_cparams(("arbitrary",)),
        name="moe_dispatch",
    )(counts, pad_start, nused, dest_tiles, h2)


def _experts_kernel(blk_e_ref, nused_ref, xs_ref, wg_ref, wu_ref, wd_ref, ys_ref, wg_s, wu_s, wd_s):
    i = pl.program_id(0)
    prev = blk_e_ref[jnp.maximum(i - 1, 0)]
    changed = jnp.logical_or(i == 0, blk_e_ref[i] != prev)

    @pl.when(changed)
    def _():
        wg_s[...] = wg_ref[0, 0].astype(bf16)
        wu_s[...] = wu_ref[0, 0].astype(bf16)
        wd_s[...] = wd_ref[0, 0].astype(bf16)

    @pl.when(i < nused_ref[0])
    def _():
        x = xs_ref[...].astype(bf16)
        gt = _dot(x, wg_s[...])
        up = _dot(x, wu_s[...])
        hmid = (gt * _sigmoid(gt)) * up
        ys_ref[...] = _dot(hmid.astype(bf16), wd_s[...])

    @pl.when(i >= nused_ref[0])
    def _():
        ys_ref[...] = jnp.zeros(ys_ref.shape, f32)


def _experts(blk_e, nused, xs, w_gate, w_up, w_down, layer, blk):
    P, D = xs.shape
    DE = w_gate.shape[-1]
    nblk = P // blk
    row_idx = lambda i, be, nu: (jnp.minimum(i, nu[0] - 1), 0)
    grid_spec = pltpu.PrefetchScalarGridSpec(
        num_scalar_prefetch=2,
        grid=(nblk,),
        in_specs=[pl.BlockSpec((blk, D), row_idx),
                  pl.BlockSpec((1, 1, D, DE), lambda i, be, nu: (layer, be[i], 0, 0)),
                  pl.BlockSpec((1, 1, D, DE), lambda i, be, nu: (layer, be[i], 0, 0)),
                  pl.BlockSpec((1, 1, DE, D), lambda i, be, nu: (layer, be[i], 0, 0))],
        out_specs=pl.BlockSpec((blk, D), lambda i, be, nu: (i, 0)),
        scratch_shapes=[pltpu.VMEM((D, DE), bf16), pltpu.VMEM((D, DE), bf16), pltpu.VMEM((DE, D), bf16)],
    )
    return pl.pallas_call(
        _experts_kernel,
        grid_spec=grid_spec,
        out_shape=jax.ShapeDtypeStruct((P, D), f32),
        compiler_params=_cparams(("arbitrary",)),
        name="moe_experts",
    )(blk_e, nused, xs, w_gate, w_up, w_down)


def _combine_kernel(dest_hbm, gate_ref, ys_hbm, h_ref, x1_ref, wsg_ref, wsu_ref, wsd_ref, gf_ref, gpost_ref,
                    x2_o, idx_ref, buf_ref, sem_idx, sem):
    i = pl.program_id(0)
    tm = h_ref.shape[0]
    cp = pltpu.make_async_copy(dest_hbm.at[i], idx_ref, sem_idx)
    cp.start()
    cp.wait()

    def issue(r, carry):
        for k in range(TOP_K):
            _row_copy(ys_hbm, idx_ref[0, r * TOP_K + k], buf_ref.at[k], r, sem).start(priority=k % 2)
        return carry

    lax.fori_loop(0, tm, issue, 0)

    hb = h_ref[...].astype(bf16)
    gt = _dot(hb, wsg_ref[...])
    up = _dot(hb, wsu_ref[...])
    y = _dot(((gt * _sigmoid(gt)) * up).astype(bf16), wsd_ref[...])

    for k in range(TOP_K):
        pltpu.make_async_copy(ys_hbm.at[pl.ds(0, tm), :], buf_ref.at[k], sem).wait()

    gate = gate_ref[...]
    for k in range(TOP_K):
        y = y + gate[:, k:k + 1] * buf_ref[k]
    ms = jnp.mean(y * y, axis=-1, keepdims=True)
    x2_o[...] = x1_ref[...] + gf_ref[0] * (y * lax.rsqrt(ms + NORM_EPS) * gpost_ref[...])


def _combine(dest_tiles, gate_tk, ys, h2, x1, wsg, wsu, wsd, mod3, g_post, S, tm):
    T, D = h2.shape
    DS = wsg.shape[1]
    tpb = S // tm
    tile = pl.BlockSpec((tm, D), lambda i: (i, 0))
    return pl.pallas_call(
        _combine_kernel,
        grid=(T // tm,),
        in_specs=[pl.BlockSpec(memory_space=pl.ANY),
                  pl.BlockSpec((tm, TOP_K), lambda i: (i, 0)),
                  pl.BlockSpec(memory_space=pl.ANY),
                  tile, tile,
                  pl.BlockSpec((D, DS), lambda i: (0, 0)),
                  pl.BlockSpec((D, DS), lambda i: (0, 0)),
                  pl.BlockSpec((DS, D), lambda i: (0, 0)),
                  pl.BlockSpec((1, 1, D), lambda i: ((i // tpb) * N_MOD + 5, 0, 0)),
                  pl.BlockSpec((1, D), lambda i: (0, 0))],
        out_specs=tile,
        out_shape=jax.ShapeDtypeStruct((T, D), f32),
        scratch_shapes=[pltpu.SMEM((1, tm * TOP_K), i32), pltpu.VMEM((TOP_K, tm, D), f32),
                        pltpu.SemaphoreType.DMA, pltpu.SemaphoreType.DMA],
        compiler_params=_cparams(("arbitrary",)),
        name="moe_combine",
    )(dest_tiles, gate_tk, ys, h2, x1, wsg, wsu, wsd, mod3, g_post.reshape(1, D))


def _tile(n, pref):
    t = min(n, pref)
    assert n % t == 0, (n, t)
    return t


def _layer(i, x2, mod3, p, wexp, v_first, B, S, cfg):
    T, D = x2.shape
    H = (D // 2) // ATT_V_DIM
    W = D - D // 2
    att_cols = 2 * H * 2 * ATT_QK_DIM + H * ATT_V_DIM
    lam_init = 0.8 - 0.6 * math.exp(-0.3 * i)

    w_in_bf = p["w_in"].astype(bf16)
    att = _inproj(x2, mod3, p["g_pre_mix"], w_in_bf[:, :att_cols], bf16, S,
                  _tile(S, cfg["tm_in"]), cfg["tn_att"], 1, 0)
    feats = _inproj(x2, mod3, p["g_pre_mix"], w_in_bf[:, att_cols:], f32, S,
                    _tile(S, cfg["tm_in_rwkv"]), cfg["tn_rwkv"], 1, 0)

    slopes = jnp.broadcast_to(
        (2.0 ** (-ALIBI_MAX_BIAS * jnp.arange(1, H + 1, dtype=f32) / H))[:, None, None], (H, 1, LANES))
    lamp = jnp.stack([p["lam_q1"], p["lam_k1"], p["lam_q2"], p["lam_k2"]])
    o_att = _attention(att, slopes, lamp, p["att_subln_g"], B, S, H, lam_init, _tile(S, cfg["tq"]))

    cols = feats.shape[1]
    zw = jnp.zeros((RWKV_A_RANK, W), f32)
    heads = W // RWKV_HEAD
    ind = (jnp.arange(W)[:, None] // RWKV_HEAD == jnp.arange(LANES)[None, :]).astype(bf16)
    prm = {
        "mu": p["rwkv_mu"].reshape(1, cols), "w0": p["rwkv_w0"].reshape(1, W),
        "w2p": jnp.concatenate([p["rwkv_w2"], zw], axis=0),
        "a0": p["rwkv_a0"].reshape(1, W),
        "a2p": jnp.concatenate([jnp.zeros((RWKV_W_RANK, W), f32), p["rwkv_a2"]], axis=0),
        "g2": p["rwkv_g2"], "k_k": p["rwkv_k_k"].reshape(1, W), "k_a": p["rwkv_k_a"].reshape(1, W),
        "ind": ind, "indt": ind.T,
    }
    if v_first is not None:
        padc = LANES - RWKV_V_RANK
        prm["v0"] = p["rwkv_v0"].reshape(1, W)
        prm["v1p"] = jnp.pad(p["rwkv_v1"], ((0, 0), (0, padc)))
        prm["v2p"] = jnp.pad(p["rwkv_v2"], ((0, padc), (0, 0)))
    r, lw, kh, v, kn, kb, g = _rwkv_prep(feats, prm, v_first, B, S, W, _tile(S, cfg["tm_prep"]))
    if v_first is None:
        v_first = v
    r2, y0, mmat, g0 = _wkv_intra(r, lw, kh, v, kn, kb, cfg["nc"], cfg["passes_intra"])
    o_rwkv = _wkv_state(r2, y0, mmat, g0, r, kh, v, g, p["rwkv_lnx_w"].reshape(1, W),
                        p["rwkv_lnx_b"].reshape(1, W), p["rwkv_r_k"].reshape(1, W), B, S,
                        min(cfg["pg"], W // LANES), min(cfg["cb"], S // CHUNK), cfg["passes_state"])
    del heads

    w_out_bf = p["w_out"].astype(bf16)
    x1, h2, logits_t = _postmix(o_att, o_rwkv, w_out_bf[:D // 2], w_out_bf[D // 2:], x2, mod3,
                                p["g_post_mix"], p["g_pre_ffn"], p["w_router"].T, S, _tile(S, cfg["tm_post"]))

    eidx_t, gate_t, rank_t, counts = _router(logits_t, p["router_bias"], _tile(T, cfg["tm_router"]))
    blk = cfg["blk"]
    counts = counts[:, 0]
    padded = (counts + blk - 1) // blk * blk
    pad_end = jnp.cumsum(padded)
    pad_start = pad_end - padded
    e_ids = jnp.arange(N_EXPERTS, dtype=i32)
    start_of = jnp.sum(jnp.where(eidx_t[:, :, None] == e_ids, pad_start, 0), axis=-1)
    dest = (start_of + rank_t).T
    n_assign = T * TOP_K
    nblk = -(-n_assign // blk) + N_EXPERTS
    P = nblk * blk
    blk_start = jnp.arange(nblk, dtype=i32) * blk
    nused = (pad_end[-1] // blk).astype(i32).reshape(1)
    blk_pos = jnp.minimum(blk_start, pad_end[-1] - blk)
    blk_e = jnp.minimum(jnp.sum((pad_end[None, :] <= blk_pos[:, None]).astype(i32), axis=1), N_EXPERTS - 1)

    tm_d = _tile(T, cfg["tm_disp"])
    xs = _dispatch(counts.astype(i32), pad_start.astype(i32), nused,
                   dest.reshape(T // tm_d, 1, tm_d * TOP_K), h2, P, blk, tm_d)
    ys = _experts(blk_e, nused, xs, wexp[0], wexp[1], wexp[2], i, blk)
    tm_c = _tile(S, cfg["tm_comb"])
    x_out = _combine(dest.reshape(T // tm_c, 1, tm_c * TOP_K), gate_t.T, ys, h2, x1,
                     p["w_sh_gate"].astype(bf16), p["w_sh_up"].astype(bf16), p["w_sh_down"].astype(bf16),
                     mod3, p["g_post_ffn"], S, tm_c)
    return x_out, v_first


_CFG = dict(tm_in=1024, tn_att=1024, tm_in_rwkv=512, tn_rwkv=1664, tq=256, tm_prep=256, nc=8, passes_intra=1, passes_state=3, pg=4, cb=8,
            tm_post=256, tm_router=512, blk=256, tm_disp=128, tm_comb=128)

_LAYER_KEYS = ("g_pre_mix", "g_post_mix", "g_pre_ffn", "g_post_ffn", "w_in", "w_out", "lam_q1", "lam_k1",
               "lam_q2", "lam_k2", "att_subln_g", "rwkv_mu", "rwkv_w0", "rwkv_w2", "rwkv_a0", "rwkv_a2",
               "rwkv_g2", "rwkv_k_k", "rwkv_k_a", "rwkv_r_k", "rwkv_lnx_w", "rwkv_lnx_b", "w_router",
               "router_bias", "w_sh_gate", "w_sh_up", "w_sh_down")


def _forward(x, c, params, cfg):
    B, S, D = x.shape
    L = params["w_in"].shape[0]
    bp = 16
    c_pad = jnp.zeros((bp, D), f32).at[:B].set(c)
    mod = _ada_mod(c_pad, params["w_ada"], params["b_ada"])
    x2 = x.reshape(B * S, D)
    v_first = None
    for i in range(L):
        p = {k: params[k][i] for k in _LAYER_KEYS}
        if i > 0:
            p["rwkv_v0"] = params["rwkv_v0"][i - 1]
            p["rwkv_v1"] = params["rwkv_v1"][i - 1]
            p["rwkv_v2"] = params["rwkv_v2"][i - 1]
        mod3 = mod[i, :B].reshape(B * N_MOD, 1, D)
        wexp = (params["w_exp_gate"], params["w_exp_up"], params["w_exp_down"])
        x2, v_first = _layer(i, x2, mod3, p, wexp, v_first, B, S, cfg)
    return x2.reshape(B, S, D)


def kernel(x, c, w_ada, b_ada, g_pre_mix, g_post_mix, g_pre_ffn, g_post_ffn, w_in, w_out, lam_q1, lam_k1, lam_q2, lam_k2, att_subln_g, rwkv_mu, rwkv_w0, rwkv_w2, rwkv_a0, rwkv_a2, rwkv_g2, rwkv_k_k, rwkv_k_a, rwkv_r_k, rwkv_lnx_w, rwkv_lnx_b, rwkv_v0, rwkv_v1, rwkv_v2, w_router, router_bias, w_exp_gate, w_exp_up, w_exp_down, w_sh_gate, w_sh_up, w_sh_down):
    params = dict(w_ada=w_ada, b_ada=b_ada, g_pre_mix=g_pre_mix, g_post_mix=g_post_mix, g_pre_ffn=g_pre_ffn,
                  g_post_ffn=g_post_ffn, w_in=w_in, w_out=w_out, lam_q1=lam_q1, lam_k1=lam_k1, lam_q2=lam_q2,
                  lam_k2=lam_k2, att_subln_g=att_subln_g, rwkv_mu=rwkv_mu, rwkv_w0=rwkv_w0, rwkv_w2=rwkv_w2,
                  rwkv_a0=rwkv_a0, rwkv_a2=rwkv_a2, rwkv_g2=rwkv_g2, rwkv_k_k=rwkv_k_k, rwkv_k_a=rwkv_k_a,
                  rwkv_r_k=rwkv_r_k, rwkv_lnx_w=rwkv_lnx_w, rwkv_lnx_b=rwkv_lnx_b, rwkv_v0=rwkv_v0,
                  rwkv_v1=rwkv_v1, rwkv_v2=rwkv_v2, w_router=w_router, router_bias=router_bias,
                  w_exp_gate=w_exp_gate, w_exp_up=w_exp_up, w_exp_down=w_exp_down, w_sh_gate=w_sh_gate,
                  w_sh_up=w_sh_up, w_sh_down=w_sh_down)
    return _forward(x, c, params, _CFG)
```

```python
import functools
import math

import jax
import jax.numpy as jnp
from jax import lax
from jax.experimental import pallas as pl
from jax.experimental.pallas import tpu as pltpu

f32 = jnp.float32
bf16 = jnp.bfloat16
i32 = jnp.int32
u32 = jnp.uint32

ATT_QK_DIM = 64
ATT_V_DIM = 128
ALIBI_MAX_BIAS = 8.0
ATT_SUBLN_EPS = 1e-5
RWKV_HEAD = 64
RWKV_W_RANK = 64
RWKV_A_RANK = 64
RWKV_G_RANK = 128
RWKV_V_RANK = 32
RWKV_GN_EPS = 64e-5
N_EXPERTS = 64
N_GROUPS = 8
TOPK_GROUPS = 4
TOP_K = 8
ROUTED_SCALE = 2.5
NORM_EPS = 1e-6
N_MOD = 6

LANES = 128
SUBLANES = 8
CHUNK = 64
VMEM_LIMIT = 56 * 1024 * 1024


def _cparams(sem):
    return pltpu.CompilerParams(dimension_semantics=sem, vmem_limit_bytes=VMEM_LIMIT)


def _dot(a, b):
    return jnp.dot(a, b, preferred_element_type=f32)


def _dot_nt(a, b):
    return lax.dot_general(a, b, (((1,), (1,)), ((), ())), preferred_element_type=f32)


def _split2(x):
    hi = x.astype(bf16)
    lo = (x - hi.astype(f32)).astype(bf16)
    return hi, lo


def _mm(a, b, passes=1):
    if passes == 1:
        return _dot(a.astype(bf16), b.astype(bf16))
    ah, al = _split2(a)
    bh, bl = _split2(b)
    return (_dot(al, bh) + _dot(ah, bl)) + _dot(ah, bh)


def _mm_exact_rhs(a, b_bf16):
    ah, al = _split2(a)
    return _dot(al, b_bf16) + _dot(ah, b_bf16)


def _sigmoid(x):
    return 1.0 / (1.0 + jnp.exp(-x))


def _pack_rows(x):
    half = x.shape[1] // 2
    a = x[:, :half].astype(bf16).astype(f32)
    b = x[:, half:].astype(bf16).astype(f32)
    lo = lax.shift_right_logical(lax.bitcast_convert_type(a, u32), jnp.uint32(16))
    hi = lax.bitcast_convert_type(b, u32) & jnp.uint32(0xFFFF0000)
    return lo | hi


def _unpack_rows(w):
    lo = lax.bitcast_convert_type(lax.shift_left(w, jnp.uint32(16)), f32)
    hi = lax.bitcast_convert_type(w & jnp.uint32(0xFFFF0000), f32)
    return lo, hi


def _ada_kernel(c_ref, w_ref, b_ref, o_ref):
    c = c_ref[...]
    cond = (c * _sigmoid(c)).astype(bf16)
    o_ref[0] = _dot(cond, w_ref[0].astype(bf16)) + b_ref[0]


def _ada_mod(c_pad, w_ada, b_ada, tn=1024):
    L, D, N = w_ada.shape
    bp = c_pad.shape[0]
    return pl.pallas_call(
        _ada_kernel,
        grid=(L, N // tn),
        in_specs=[pl.BlockSpec((bp, D), lambda l, j: (0, 0)),
                  pl.BlockSpec((1, D, tn), lambda l, j: (l, 0, j)),
                  pl.BlockSpec((1, 1, tn), lambda l, j: (l, 0, j))],
        out_specs=pl.BlockSpec((1, bp, tn), lambda l, j: (l, 0, j)),
        out_shape=jax.ShapeDtypeStruct((L, bp, N), f32),
        compiler_params=_cparams(("arbitrary", "arbitrary")),
        name="ada_mod",
    )(c_pad, w_ada, b_ada.reshape(L, 1, N))


def _inproj_kernel(x_ref, sc_ref, sh_ref, g_ref, w_ref, o_ref, h_ref):
    @pl.when(pl.program_id(1) == 0)
    def _():
        x = x_ref[...]
        ms = jnp.mean(x * x, axis=-1, keepdims=True)
        y = x * lax.rsqrt(ms + NORM_EPS) * g_ref[...]
        h_ref[...] = (y * (1.0 + sc_ref[0]) + sh_ref[0]).astype(bf16)

    o_ref[...] = _dot(h_ref[...], w_ref[...]).astype(o_ref.dtype)


def _inproj(x2, mod3, g, w_bf, out_dtype, S, tm, tn, seg_sc, seg_sh):
    T, D = x2.shape
    N = w_bf.shape[1]
    tpb = S // tm
    return pl.pallas_call(
        _inproj_kernel,
        grid=(T // tm, N // tn),
        in_specs=[pl.BlockSpec((tm, D), lambda i, j: (i, 0)),
                  pl.BlockSpec((1, 1, D), lambda i, j: ((i // tpb) * N_MOD + seg_sc, 0, 0)),
                  pl.BlockSpec((1, 1, D), lambda i, j: ((i // tpb) * N_MOD + seg_sh, 0, 0)),
                  pl.BlockSpec((1, D), lambda i, j: (0, 0)),
                  pl.BlockSpec((D, tn), lambda i, j: (0, j))],
        out_specs=pl.BlockSpec((tm, tn), lambda i, j: (i, j)),
        out_shape=jax.ShapeDtypeStruct((T, N), out_dtype),
        scratch_shapes=[pltpu.VMEM((tm, D), bf16)],
        compiler_params=_cparams(("arbitrary", "arbitrary")),
        name="inproj",
    )(x2, mod3, mod3, g.reshape(1, D), w_bf)


def _attn_kernel(q_ref, k_ref, v_ref, slope_ref, lamp_ref, g_ref, o_ref,
                 q2t_ref, vt_ref, m_ref, l_ref, acc_ref, *, tq, lam_init):
    qi = pl.program_id(2)
    scale = ATT_QK_DIM ** -0.5
    slope = slope_ref[0][:, 0:1]

    @pl.when(qi == 0)
    def _():
        vt_ref[...] = v_ref[...].astype(f32).T.astype(bf16)

    qt = (q_ref[...].astype(f32) * scale).T
    dim = lax.broadcasted_iota(i32, (LANES, 1), 0)
    first = dim < ATT_QK_DIM
    q2t_ref[:, 0:tq] = jnp.where(first, qt, 0.0).astype(bf16)
    q2t_ref[:, tq:2 * tq] = jnp.where(first, 0.0, qt).astype(bf16)
    m_ref[...] = jnp.full(m_ref.shape, -jnp.inf, f32)
    l_ref[...] = jnp.zeros(l_ref.shape, f32)
    acc_ref[...] = jnp.zeros(acc_ref.shape, f32)

    kr = lax.broadcasted_iota(i32, (tq, 2 * tq), 0)
    qc = lax.broadcasted_iota(i32, (tq, 2 * tq), 1)
    rel = (jnp.where(qc >= tq, qc - tq, qc) - kr).astype(f32)

    def step(ki, masked):
        start = pl.multiple_of(ki * tq, tq)
        kb = k_ref[pl.ds(start, tq), :]
        vtb = vt_ref[:, pl.ds(start, tq)]
        s = _dot(kb, q2t_ref[...])
        dist = rel + ((qi - ki) * tq).astype(f32)
        s = s - slope * dist
        if masked:
            s = jnp.where(dist >= 0.0, s, -jnp.inf)
        m_prev = m_ref[...]
        m_new = jnp.maximum(m_prev, jnp.max(s, axis=0, keepdims=True))
        alpha = jnp.exp(m_prev - m_new)
        p = jnp.exp(s - m_new)
        l_ref[...] = alpha * l_ref[...] + jnp.sum(p, axis=0, keepdims=True)
        acc_ref[...] = alpha * acc_ref[...] + _dot(vtb, p.astype(bf16))
        m_ref[...] = m_new

    def body(ki, carry):
        step(ki, False)
        return carry

    lax.fori_loop(0, qi, body, 0)
    step(qi, True)

    lp = lamp_ref[...]
    lam = (jnp.exp(jnp.sum(lp[0:1] * lp[1:2], axis=-1, keepdims=True))
           - jnp.exp(jnp.sum(lp[2:3] * lp[3:4], axis=-1, keepdims=True)) + lam_init)
    on = acc_ref[...] * (1.0 / l_ref[...])
    o = on[:, 0:tq] - lam * on[:, tq:2 * tq]
    o = o * lax.rsqrt(jnp.mean(o * o, axis=0, keepdims=True) + ATT_SUBLN_EPS)
    o = o * g_ref[...] * (1.0 - lam_init)
    o_ref[...] = o.T.astype(o_ref.dtype)


def _attention(att, slopes, lamp, subln_g, B, S, H, lam_init, tq):
    T = att.shape[0]
    nq = S // tq
    kern = functools.partial(_attn_kernel, tq=tq, lam_init=lam_init)
    return pl.pallas_call(
        kern,
        grid=(B, H, nq),
        in_specs=[pl.BlockSpec((tq, LANES), lambda b, h, q: (b * nq + q, h)),
                  pl.BlockSpec((S, LANES), lambda b, h, q: (b, H + h)),
                  pl.BlockSpec((S, LANES), lambda b, h, q: (b, 2 * H + h)),
                  pl.BlockSpec((1, 1, LANES), lambda b, h, q: (h, 0, 0)),
                  pl.BlockSpec((4, ATT_QK_DIM), lambda b, h, q: (0, 0)),
                  pl.BlockSpec((ATT_V_DIM, 1), lambda b, h, q: (0, 0))],
        out_specs=pl.BlockSpec((tq, LANES), lambda b, h, q: (b * nq + q, h)),
        out_shape=jax.ShapeDtypeStruct((T, H * ATT_V_DIM), bf16),
        scratch_shapes=[pltpu.VMEM((LANES, 2 * tq), bf16),
                        pltpu.VMEM((LANES, S), bf16),
                        pltpu.VMEM((1, 2 * tq), f32),
                        pltpu.VMEM((1, 2 * tq), f32),
                        pltpu.VMEM((LANES, 2 * tq), f32)],
        compiler_params=_cparams(("arbitrary", "arbitrary", "arbitrary")),
        name="diff_attention",
    )(att, att, att, slopes, lamp, subln_g.reshape(ATT_V_DIM, 1))


def _head_sums(x, ind, indt):
    s = _mm_exact_rhs(x, ind)
    return _mm_exact_rhs(s, indt)


def _rwkv_prep_kernel(*refs, W, has_vres):
    if has_vres:
        (f_ref, mu_ref, w0_ref, w2_ref, a0_ref, a2_ref, g2_ref, kk_ref, ka_ref, ind_ref, indt_ref,
         vf_ref, v0_ref, v1_ref, v2_ref,
         r_o, lw_o, kh_o, v_o, kn_o, kb_o, g_o, carry_ref) = refs
    else:
        (f_ref, mu_ref, w0_ref, w2_ref, a0_ref, a2_ref, g2_ref, kk_ref, ka_ref, ind_ref, indt_ref,
         r_o, lw_o, kh_o, v_o, kn_o, kb_o, g_o, carry_ref) = refs

    ti = pl.program_id(1)
    h = f_ref[...]
    tm = h.shape[0]

    @pl.when(ti == 0)
    def _():
        carry_ref[...] = jnp.zeros(carry_ref.shape, f32)

    rolled = pltpu.roll(h, 1, axis=0)
    row = lax.broadcasted_iota(i32, (tm, 1), 0)
    prev = jnp.where(row == 0, carry_ref[...], rolled)
    carry_ref[...] = h[tm - 1:tm, :]
    feats = h + (prev - h) * mu_ref[...]

    r = feats[:, 0:W]
    k = feats[:, W:2 * W]
    v = feats[:, 2 * W:3 * W]
    wa = feats[:, 3 * W:3 * W + LANES]
    g_lo = feats[:, 3 * W + LANES:3 * W + 2 * LANES]

    w = w0_ref[...] + _mm(jnp.tanh(wa), w2_ref[...], passes=3)
    lw_o[...] = -math.exp(-0.5) * _sigmoid(w)
    a = _sigmoid(a0_ref[...] + _mm(wa, a2_ref[...], passes=3))
    g_o[...] = _mm(_sigmoid(g_lo), g2_ref[...])

    if has_vres:
        mix = _sigmoid(v0_ref[...] + _mm(_mm(v, v1_ref[...]), v2_ref[...]))
        v = v + (vf_ref[...] - v) * mix

    kk = k * kk_ref[...]
    ss = _head_sums(kk * kk, ind_ref[...], indt_ref[...])
    kk = kk / jnp.maximum(jnp.sqrt(ss), 1e-12)
    r_o[...] = r
    kh_o[...] = k * (1.0 + (a - 1.0) * ka_ref[...])
    v_o[...] = v
    kn_o[...] = kk
    kb_o[...] = kk * a


def _rwkv_prep(feats, prm, vfirst, B, S, W, tm):
    T, COLS = feats.shape
    tpb = S // tm
    has_vres = vfirst is not None
    row = lambda n: pl.BlockSpec((1, n), lambda b, t: (0, 0))
    full = lambda a: pl.BlockSpec(a.shape, lambda b, t: (0, 0))
    tile = pl.BlockSpec((tm, W), lambda b, t: (b * tpb + t, 0))
    args = [feats, prm["mu"], prm["w0"], prm["w2p"], prm["a0"], prm["a2p"], prm["g2"], prm["k_k"], prm["k_a"],
            prm["ind"], prm["indt"]]
    specs = [pl.BlockSpec((tm, COLS), lambda b, t: (b * tpb + t, 0)), row(COLS), row(W), full(prm["w2p"]),
             row(W), full(prm["a2p"]), full(prm["g2"]), row(W), row(W), full(prm["ind"]), full(prm["indt"])]
    if has_vres:
        args += [vfirst, prm["v0"], prm["v1p"], prm["v2p"]]
        specs += [tile, row(W), full(prm["v1p"]), full(prm["v2p"])]
    out = jax.ShapeDtypeStruct((T, W), f32)
    kern = functools.partial(_rwkv_prep_kernel, W=W, has_vres=has_vres)
    return pl.pallas_call(
        kern,
        grid=(B, tpb),
        in_specs=specs,
        out_specs=[tile] * 7,
        out_shape=[out] * 7,
        scratch_shapes=[pltpu.VMEM((1, COLS), f32)],
        compiler_params=_cparams(("arbitrary", "arbitrary")),
        name="rwkv_prep",
    )(*args)


def _wkv_chunks(rs, lws, ks, vs, kns, kbs, passes):
    C = CHUNK
    P2 = 2 * C
    n = range(len(rs))
    ri = lax.broadcasted_iota(i32, (C, C), 0)
    ci = lax.broadcasted_iota(i32, (C, C), 1)
    tri = (ci <= ri).astype(bf16)
    lane = lax.broadcasted_iota(i32, (1, LANES), 1)
    m0 = (lane < RWKV_HEAD).astype(f32)
    m1 = 1.0 - m0
    rr = lax.broadcasted_iota(i32, (P2, P2), 0)
    cc = lax.broadcasted_iota(i32, (P2, P2), 1)
    same = jnp.where(rr >= C, 1, 0) == jnp.where(cc >= C, 1, 0)
    strict = same & (cc < rr)
    incl = same & (cc <= rr)
    incl2 = jnp.concatenate([incl, incl], axis=1)
    eye = (rr == cc).astype(f32)
    zeros_p = jnp.zeros((P2, LANES), f32)
    zeros_c = jnp.zeros((C, LANES), f32)
    stack = lambda x: jnp.concatenate([x * m0, x * m1], axis=0)
    fold = lambda x: x[0:C] + x[C:2 * C]

    def cumsum(lw):
        h1 = lw.astype(bf16)
        r1 = lw - h1.astype(f32)
        h2 = r1.astype(bf16)
        h3 = (r1 - h2.astype(f32)).astype(bf16)
        return (_dot(tri, h3) + _dot(tri, h2)) + _dot(tri, h1)

    cum = [cumsum(lws[j]) for j in n]
    cum_c = [cum[j][C - 1:C, :] for j in n]
    at = [-kns[j] * jnp.exp(cum[j] - lws[j]) for j in n]
    rt = [rs[j] * jnp.exp(cum[j]) for j in n]
    einv = [jnp.exp(-cum[j]) for j in n]
    bt = [kbs[j] * einv[j] for j in n]
    kt = [ks[j] * einv[j] for j in n]
    eh = [jnp.exp(cum_c[j] - cum[j]) for j in n]
    bh = [kbs[j] * eh[j] for j in n]
    kh = [ks[j] * eh[j] for j in n]
    w_c = [jnp.exp(cum_c[j]) for j in n]
    abd = [stack(at[j]) for j in n]
    vst = [stack(vs[j]) for j in n]
    lhs = [jnp.concatenate([abd[j], stack(rt[j])], axis=0) for j in n]
    rhs = [jnp.concatenate([stack(bt[j]), stack(kt[j])], axis=0) for j in n]
    gram = [_mm_nt(lhs[j], rhs[j], passes) for j in n]
    lab = [jnp.where(strict, gram[j][0:P2, 0:P2], 0.0) for j in n]
    lak = [jnp.where(strict, gram[j][0:P2, P2:2 * P2], 0.0) for j in n]
    mrbk = [jnp.where(incl2, gram[j][P2:2 * P2, :], 0.0) for j in n]

    x0 = [_mm(lak[j], vst[j], passes) for j in n]
    tinv = [eye + lab[j] for j in n]
    lp = lab
    for _ in range(int(math.log2(C)) - 1):
        lp = [_mm(lp[j], lp[j], passes) for j in n]
        tinv = [tinv[j] + _mm(lp[j], tinv[j], passes) for j in n]

    ta = [_mm(tinv[j], jnp.concatenate([abd[j], x0[j]], axis=1), passes) for j in n]
    rhs2 = [jnp.concatenate([ta[j], jnp.concatenate([zeros_p, vst[j]], axis=1)], axis=0) for j in n]
    z = [_mm(mrbk[j], rhs2[j], passes) for j in n]
    r2 = [rt[j] + fold(z[j][:, 0:LANES]) for j in n]
    y0 = [fold(z[j][:, LANES:2 * LANES]) for j in n]
    lhs3t = [jnp.concatenate([bh[j], kh[j]], axis=0).T for j in n]
    rhs3 = [jnp.concatenate([fold(ta[j]), jnp.concatenate([zeros_c, vs[j]], axis=1)], axis=0) for j in n]
    wmat = [_mm(lhs3t[j], rhs3[j], passes) for j in n]
    mmat = [jnp.where(same, wmat[j][:, 0:LANES], 0.0) + eye * w_c[j] for j in n]
    g0 = [jnp.where(same, wmat[j][:, LANES:2 * LANES], 0.0) for j in n]
    return r2, y0, mmat, g0


def _mm_nt(a, b, passes):
    if passes == 1:
        return _dot_nt(a.astype(bf16), b.astype(bf16))
    ah, al = _split2(a)
    bh, bl = _split2(b)
    return (_dot_nt(al, bh) + _dot_nt(ah, bl)) + _dot_nt(ah, bh)


def _wkv_intra_kernel(r_ref, lw_ref, k_ref, v_ref, kn_ref, kb_ref, r2_o, y0_o, m_o, g_o, *, nc, passes):
    C = CHUNK
    sls = [slice(c * C, (c + 1) * C) for c in range(nc)]
    take = lambda ref: [ref[sl, :] for sl in sls]
    r2, y0, mmat, g0 = _wkv_chunks(take(r_ref), take(lw_ref), take(k_ref), take(v_ref),
                                   take(kn_ref), take(kb_ref), passes)
    for c, sl in enumerate(sls):
        r2_o[sl, :] = r2[c]
        y0_o[sl, :] = y0[c]
        m_o[0, c] = mmat[c]
        g_o[0, c] = g0[c]


def _wkv_intra(r, lw, kh, v, kn, kb, nc, passes):
    T, W = r.shape
    npair = W // LANES
    rows = nc * CHUNK
    tile = pl.BlockSpec((rows, LANES), lambda p, i: (i, p))
    mat = pl.BlockSpec((1, nc, LANES, LANES), lambda p, i: (p, i, 0, 0))
    kern = functools.partial(_wkv_intra_kernel, nc=nc, passes=passes)
    return pl.pallas_call(
        kern,
        grid=(npair, T // rows),
        in_specs=[tile] * 6,
        out_specs=[tile, tile, mat, mat],
        out_shape=[jax.ShapeDtypeStruct((T, W), f32), jax.ShapeDtypeStruct((T, W), f32),
                   jax.ShapeDtypeStruct((npair, T // CHUNK, LANES, LANES), f32),
                   jax.ShapeDtypeStruct((npair, T // CHUNK, LANES, LANES), f32)],
        compiler_params=_cparams(("arbitrary", "arbitrary")),
        name="wkv_intra",
    )(r, lw, kh, v, kn, kb)


def _wkv_state_kernel(r2_ref, y0_ref, m_ref, g0_ref, r_ref, kh_ref, v_ref, g_ref,
                      lnw_ref, lnb_ref, rk_ref, o_ref, st_ref, y_ref, *, pg, cb, passes):
    C = CHUNK

    @pl.when(pl.program_id(2) == 0)
    def _():
        st_ref[...] = jnp.zeros(st_ref.shape, f32)

    pairs = range(pg)
    lanes = [slice(p * LANES, (p + 1) * LANES) for p in pairs]
    st = [st_ref[p] for p in pairs]
    for c in range(cb):
        rows = slice(c * C, (c + 1) * C)
        for p in pairs:
            y_ref[rows, lanes[p]] = _mm(r2_ref[rows, lanes[p]], st[p], passes) + y0_ref[rows, lanes[p]]
        st = [_mm(m_ref[p, c], st[p], passes) + g0_ref[p, c] for p in pairs]
    for p in pairs:
        st_ref[p] = st[p]

    rr = lax.broadcasted_iota(i32, (LANES, LANES), 0)
    cc = lax.broadcasted_iota(i32, (LANES, LANES), 1)
    ones_bd = (jnp.where(rr >= RWKV_HEAD, 1, 0) == jnp.where(cc >= RWKV_HEAD, 1, 0)).astype(bf16)
    for p in pairs:
        y = y_ref[:, lanes[p]]
        mu = _mm_exact_rhs(y, ones_bd) * (1.0 / RWKV_HEAD)
        d = y - mu
        var = _mm_exact_rhs(d * d, ones_bd) * (1.0 / RWKV_HEAD)
        yn = d * lax.rsqrt(var + RWKV_GN_EPS) * lnw_ref[:, lanes[p]] + lnb_ref[:, lanes[p]]
        rk = r_ref[:, lanes[p]] * kh_ref[:, lanes[p]] * rk_ref[:, lanes[p]]
        bonus = _mm_exact_rhs(rk, ones_bd) * v_ref[:, lanes[p]]
        o_ref[:, lanes[p]] = ((yn + bonus) * g_ref[:, lanes[p]]).astype(o_ref.dtype)


def _wkv_state(r2, y0, mm, g0, r, kh, v, g, lnw, lnb, rk, B, S, pg, cb, passes):
    T, W = r.shape
    npair = W // LANES
    rows = cb * CHUNK
    steps = S // rows
    seq = pl.BlockSpec((rows, pg * LANES), lambda b, q, c: (b * steps + c, q))
    mat = pl.BlockSpec((pg, cb, LANES, LANES), lambda b, q, c: (q, b * steps + c, 0, 0))
    prow = pl.BlockSpec((1, pg * LANES), lambda b, q, c: (0, q))
    kern = functools.partial(_wkv_state_kernel, pg=pg, cb=cb, passes=passes)
    return pl.pallas_call(
        kern,
        grid=(B, npair // pg, steps),
        in_specs=[seq, seq, mat, mat, seq, seq, seq, seq, prow, prow, prow],
        out_specs=seq,
        out_shape=jax.ShapeDtypeStruct((T, W), bf16),
        scratch_shapes=[pltpu.VMEM((pg, LANES, LANES), f32), pltpu.VMEM((rows, pg * LANES), f32)],
        compiler_params=_cparams(("arbitrary", "arbitrary", "arbitrary")),
        name="wkv_state",
    )(r2, y0, mm, g0, r, kh, v, g, lnw, lnb, rk)


def _postmix_kernel(oa_ref, orw_ref, wa_ref, wr_ref, x_ref, ga_ref, gpost_ref, gpre_ref, sc_ref, sh_ref, wrt_ref,
                    x1_o, h2_o, lg_o):
    mixed = _dot(oa_ref[...], wa_ref[...]) + _dot(orw_ref[...], wr_ref[...])
    ms = jnp.mean(mixed * mixed, axis=-1, keepdims=True)
    x1 = x_ref[...] + ga_ref[0] * (mixed * lax.rsqrt(ms + NORM_EPS) * gpost_ref[...])
    x1_o[...] = x1
    ms1 = jnp.mean(x1 * x1, axis=-1, keepdims=True)
    h2 = (x1 * lax.rsqrt(ms1 + NORM_EPS) * gpre_ref[...]) * (1.0 + sc_ref[0]) + sh_ref[0]
    h2_o[...] = _pack_rows(h2)
    lg_o[...] = _mm_nt(wrt_ref[...], h2, 3)


def _postmix(o_att, o_rwkv, w_out_a, w_out_r, x2, mod3, g_post, g_pre, w_rt, S, tm):
    T, D = x2.shape
    WA = o_att.shape[1]
    WR = o_rwkv.shape[1]
    E = w_rt.shape[0]
    tpb = S // tm
    modspec = lambda seg: pl.BlockSpec((1, 1, D), lambda i: ((i // tpb) * N_MOD + seg, 0, 0))
    tile = pl.BlockSpec((tm, D), lambda i: (i, 0))
    return pl.pallas_call(
        _postmix_kernel,
        grid=(T // tm,),
        in_specs=[pl.BlockSpec((tm, WA), lambda i: (i, 0)),
                  pl.BlockSpec((tm, WR), lambda i: (i, 0)),
                  pl.BlockSpec((WA, D), lambda i: (0, 0)),
                  pl.BlockSpec((WR, D), lambda i: (0, 0)),
                  tile, modspec(2),
                  pl.BlockSpec((1, D), lambda i: (0, 0)),
                  pl.BlockSpec((1, D), lambda i: (0, 0)),
                  modspec(4), modspec(3),
                  pl.BlockSpec((E, D), lambda i: (0, 0))],
        out_specs=[tile, pl.BlockSpec((tm, D // 2), lambda i: (i, 0)),
                   pl.BlockSpec((E, tm), lambda i: (0, i))],
        out_shape=[jax.ShapeDtypeStruct((T, D), f32), jax.ShapeDtypeStruct((T, D // 2), u32),
                   jax.ShapeDtypeStruct((E, T), f32)],
        compiler_params=_cparams(("arbitrary",)),
        name="postmix",
    )(o_att, o_rwkv, w_out_a, w_out_r, x2, mod3, g_post.reshape(1, D), g_pre.reshape(1, D), mod3, mod3, w_rt)


def _first_max(x, iota, n):
    mx = jnp.max(x, axis=0, keepdims=True)
    idx = jnp.min(jnp.where(x == mx, iota, n), axis=0, keepdims=True)
    return mx, idx


def _router_kernel(lg_ref, bias_ref, eidx_o, gate_o, rank_o, cnt_o, cnt_ref):
    E = N_EXPERTS
    G = N_GROUPS
    per = E // G
    tm = lg_ref.shape[1]

    @pl.when(pl.program_id(0) == 0)
    def _():
        cnt_ref[...] = jnp.zeros(cnt_ref.shape, f32)

    scores = _sigmoid(lg_ref[...])
    biased = scores + bias_ref[...]
    neg = -jnp.inf

    iota_p = lax.broadcasted_iota(i32, (per, tm), 0).astype(f32)
    gs = []
    for g in range(G):
        xg = biased[g * per:(g + 1) * per, :]
        m1, i1 = _first_max(xg, iota_p, per)
        m2 = jnp.max(jnp.where(iota_p == i1, neg, xg), axis=0, keepdims=True)
        gs.append(m1 + m2)
    gsc = jnp.concatenate(gs, axis=0)
    iota_g = lax.broadcasted_iota(i32, (G, tm), 0).astype(f32)
    gsel = jnp.zeros((G, tm), f32)
    for _ in range(TOPK_GROUPS):
        _, gi = _first_max(gsc, iota_g, G)
        hit = iota_g == gi
        gsel = jnp.where(hit, 1.0, gsel)
        gsc = jnp.where(hit, neg, gsc)
    masked = jnp.concatenate(
        [jnp.where(gsel[g:g + 1, :] > 0.0, biased[g * per:(g + 1) * per, :], neg) for g in range(G)], axis=0)

    iota_e = lax.broadcasted_iota(i32, (E, tm), 0).astype(f32)
    sel = jnp.zeros((E, tm), f32)
    idxs, vals = [], []
    for _ in range(TOP_K):
        _, ei = _first_max(masked, iota_e, E)
        hit = iota_e == ei
        idxs.append(ei)
        vals.append(jnp.sum(jnp.where(hit, scores, 0.0), axis=0, keepdims=True))
        sel = jnp.where(hit, 1.0, sel)
        masked = jnp.where(hit, neg, masked)
    tot = vals[0]
    for vv in vals[1:]:
        tot = tot + vv
    eidx_o[...] = jnp.concatenate(idxs, axis=0).astype(i32)
    gate_o[...] = jnp.concatenate([vv / tot * ROUTED_SCALE for vv in vals], axis=0)

    rr = lax.broadcasted_iota(i32, (tm, tm), 0)
    cc = lax.broadcasted_iota(i32, (tm, tm), 1)
    before = (rr < cc).astype(bf16)
    pos = _dot(sel.astype(bf16), before) + cnt_ref[...]
    rank_o[...] = jnp.concatenate(
        [jnp.sum(jnp.where(iota_e == ei, pos, 0.0), axis=0, keepdims=True) for ei in idxs], axis=0).astype(i32)
    cnt_ref[...] = cnt_ref[...] + jnp.sum(sel, axis=1, keepdims=True)
    cnt_o[...] = cnt_ref[...].astype(i32)


def _router(logits_t, bias, tm):
    E, T = logits_t.shape
    k_tile = pl.BlockSpec((TOP_K, tm), lambda i: (0, i))
    return pl.pallas_call(
        _router_kernel,
        grid=(T // tm,),
        in_specs=[pl.BlockSpec((E, tm), lambda i: (0, i)),
                  pl.BlockSpec((E, 1), lambda i: (0, 0))],
        out_specs=[k_tile, k_tile, k_tile, pl.BlockSpec((E, 1), lambda i: (0, 0))],
        out_shape=[jax.ShapeDtypeStruct((TOP_K, T), i32), jax.ShapeDtypeStruct((TOP_K, T), f32),
                   jax.ShapeDtypeStruct((TOP_K, T), i32), jax.ShapeDtypeStruct((E, 1), i32)],
        scratch_shapes=[pltpu.VMEM((E, 1), f32)],
        compiler_params=_cparams(("arbitrary",)),
        name="router",
    )(logits_t, bias.reshape(E, 1))


def _row_copy(src_ref, s, dst_ref, d, sem):
    return pltpu.make_async_copy(src_ref.at[pl.ds(s, 1), :], dst_ref.at[pl.ds(d, 1), :], sem)


def _zero_fill(cnt_ref, pstart_ref, nused_ref, z_ref, xs_out, sem, blk, nblk, start):
    def act(cp):
        if start:
            cp.start()
        else:
            cp.wait()

    def per_expert(e, carry):
        c = cnt_ref[e]
        base = pstart_ref[e] + c
        npad = (blk - (c & (blk - 1))) & (blk - 1)
        head = (-base) & (SUBLANES - 1)

        def one_row(j, carry2):
            act(_row_copy(z_ref, 0, xs_out, base + j, sem))
            return carry2

        lax.fori_loop(0, head, one_row, 0)
        rem = npad - head
        aligned = base + head
        p = blk // 2
        while p >= SUBLANES:
            off = pl.multiple_of(aligned + (rem & ~(2 * p - 1)), SUBLANES)

            @pl.when((rem & p) != 0)
            def _(p=p, off=off):
                act(pltpu.make_async_copy(z_ref.at[pl.ds(0, p), :], xs_out.at[pl.ds(off, p), :], sem))

            p //= 2
        return carry

    lax.fori_loop(0, N_EXPERTS, per_expert, 0)

    def per_block(b, carry):
        act(pltpu.make_async_copy(z_ref, xs_out.at[pl.ds(pl.multiple_of(b * blk, blk), blk), :], sem))
        return carry

    lax.fori_loop(nused_ref[0], nblk, per_block, 0)


def _dispatch_kernel(cnt_ref, pstart_ref, nused_ref, dest_hbm, h_ref, xs_out, idx_ref, z_ref, sem_idx, sem, sem_z,
                     *, blk, nblk):
    i = pl.program_id(0)
    tm = h_ref.shape[0]
    cp = pltpu.make_async_copy(dest_hbm.at[i], idx_ref, sem_idx)
    cp.start()
    cp.wait()

    def issue(r, carry):
        for k in range(TOP_K):
            _row_copy(h_ref, r, xs_out, idx_ref[0, r * TOP_K + k], sem).start(priority=k % 2)
        return carry

    lax.fori_loop(0, tm, issue, 0)

    @pl.when(i == pl.num_programs(0) - 1)
    def _():
        z_ref[...] = jnp.zeros(z_ref.shape, u32)
        _zero_fill(cnt_ref, pstart_ref, nused_ref, z_ref, xs_out, sem_z, blk, nblk, True)
        _zero_fill(cnt_ref, pstart_ref, nused_ref, z_ref, xs_out, sem_z, blk, nblk, False)

    for _ in range(TOP_K):
        pltpu.make_async_copy(h_ref, xs_out.at[pl.ds(0, tm), :], sem).wait()


def _dispatch(counts, pad_start, nused, dest_tiles, h2p, P, blk, tm):
    T, DW = h2p.shape
    assert blk & (blk - 1) == 0
    kern = functools.partial(_dispatch_kernel, blk=blk, nblk=P // blk)
    grid_spec = pltpu.PrefetchScalarGridSpec(
        num_scalar_prefetch=3,
        grid=(T // tm,),
        in_specs=[pl.BlockSpec(memory_space=pl.ANY),
                  pl.BlockSpec((tm, DW), lambda i, c, s, n: (i, 0))],
        out_specs=pl.BlockSpec(memory_space=pl.ANY),
        scratch_shapes=[pltpu.SMEM((1, tm * TOP_K), i32), pltpu.VMEM((blk, DW), u32),
                        pltpu.SemaphoreType.DMA, pltpu.SemaphoreType.DMA, pltpu.SemaphoreType.DMA],
    )
    return pl.pallas_call(
        kern,
        grid_spec=grid_spec,
        out_shape=jax.ShapeDtypeStruct((P, DW), u32),
        compiler_params=_cparams(("arbitrary",)),
        name="moe_dispatch",
    )(counts, pad_start, nused, dest_tiles, h2p)


def _experts_kernel(blk_e_ref, nused_ref, xs_ref, wg_ref, wu_ref, wd_ref, ys_ref, wg_s, wu_s, wd_s):
    i = pl.program_id(0)
    prev = blk_e_ref[jnp.maximum(i - 1, 0)]
    changed = jnp.logical_or(i == 0, blk_e_ref[i] != prev)

    @pl.when(changed)
    def _():
        wg_s[...] = wg_ref[0, 0].astype(bf16)
        wu_s[...] = wu_ref[0, 0].astype(bf16)
        wd_s[...] = wd_ref[0, 0].astype(bf16)

    @pl.when(i < nused_ref[0])
    def _():
        lo, hi = _unpack_rows(xs_ref[...])
        x = jnp.concatenate([lo.astype(bf16), hi.astype(bf16)], axis=1)
        gt = _dot(x, wg_s[...])
        up = _dot(x, wu_s[...])
        hmid = (gt * _sigmoid(gt)) * up
        ys_ref[...] = _pack_rows(_dot(hmid.astype(bf16), wd_s[...]))

    @pl.when(i >= nused_ref[0])
    def _():
        ys_ref[...] = jnp.zeros(ys_ref.shape, u32)


def _experts(blk_e, nused, xs, w_gate, w_up, w_down, layer, blk):
    P, DW = xs.shape
    D, DE = w_gate.shape[-2:]
    nblk = P // blk
    row_idx = lambda i, be, nu: (jnp.minimum(i, nu[0] - 1), 0)
    grid_spec = pltpu.PrefetchScalarGridSpec(
        num_scalar_prefetch=2,
        grid=(nblk,),
        in_specs=[pl.BlockSpec((blk, DW), row_idx),
                  pl.BlockSpec((1, 1, D, DE), lambda i, be, nu: (layer, be[i], 0, 0)),
                  pl.BlockSpec((1, 1, D, DE), lambda i, be, nu: (layer, be[i], 0, 0)),
                  pl.BlockSpec((1, 1, DE, D), lambda i, be, nu: (layer, be[i], 0, 0))],
        out_specs=pl.BlockSpec((blk, DW), lambda i, be, nu: (i, 0)),
        scratch_shapes=[pltpu.VMEM((D, DE), bf16), pltpu.VMEM((D, DE), bf16), pltpu.VMEM((DE, D), bf16)],
    )
    return pl.pallas_call(
        _experts_kernel,
        grid_spec=grid_spec,
        out_shape=jax.ShapeDtypeStruct((P, DW), u32),
        compiler_params=_cparams(("arbitrary",)),
        name="moe_experts",
    )(blk_e, nused, xs, w_gate, w_up, w_down)


def _combine_kernel(dest_hbm, gate_ref, ys_hbm, h_ref, x1_ref, wsg_ref, wsu_ref, wsd_ref, gf_ref, gpost_ref,
                    x2_o, idx_ref, buf_ref, sem_idx, sem):
    i = pl.program_id(0)
    tm = h_ref.shape[0]
    cp = pltpu.make_async_copy(dest_hbm.at[i], idx_ref, sem_idx)
    cp.start()
    cp.wait()

    def issue(r, carry):
        for k in range(TOP_K):
            _row_copy(ys_hbm, idx_ref[0, r * TOP_K + k], buf_ref.at[k], r, sem).start(priority=k % 2)
        return carry

    lax.fori_loop(0, tm, issue, 0)

    hlo, hhi = _unpack_rows(h_ref[...])
    hb = jnp.concatenate([hlo.astype(bf16), hhi.astype(bf16)], axis=1)
    gt = _dot(hb, wsg_ref[...])
    up = _dot(hb, wsu_ref[...])
    ysh = _dot(((gt * _sigmoid(gt)) * up).astype(bf16), wsd_ref[...])

    for k in range(TOP_K):
        pltpu.make_async_copy(ys_hbm.at[pl.ds(0, tm), :], buf_ref.at[k], sem).wait()

    gate = gate_ref[...]
    half = ysh.shape[1] // 2
    lo = ysh[:, :half]
    hi = ysh[:, half:]
    for k in range(TOP_K):
        a, b = _unpack_rows(buf_ref[k])
        gk = gate[:, k:k + 1]
        lo = lo + gk * a
        hi = hi + gk * b
    y = jnp.concatenate([lo, hi], axis=1)
    ms = jnp.mean(y * y, axis=-1, keepdims=True)
    x2_o[...] = x1_ref[...] + gf_ref[0] * (y * lax.rsqrt(ms + NORM_EPS) * gpost_ref[...])


def _combine(dest_tiles, gate_tk, ys, h2p, x1, wsg, wsu, wsd, mod3, g_post, S, tm):
    T, D = x1.shape
    DW = h2p.shape[1]
    DS = wsg.shape[1]
    tpb = S // tm
    tile = pl.BlockSpec((tm, D), lambda i: (i, 0))
    return pl.pallas_call(
        _combine_kernel,
        grid=(T // tm,),
        in_specs=[pl.BlockSpec(memory_space=pl.ANY),
                  pl.BlockSpec((tm, TOP_K), lambda i: (i, 0)),
                  pl.BlockSpec(memory_space=pl.ANY),
                  pl.BlockSpec((tm, DW), lambda i: (i, 0)), tile,
                  pl.BlockSpec((D, DS), lambda i: (0, 0)),
                  pl.BlockSpec((D, DS), lambda i: (0, 0)),
                  pl.BlockSpec((DS, D), lambda i: (0, 0)),
                  pl.BlockSpec((1, 1, D), lambda i: ((i // tpb) * N_MOD + 5, 0, 0)),
                  pl.BlockSpec((1, D), lambda i: (0, 0))],
        out_specs=tile,
        out_shape=jax.ShapeDtypeStruct((T, D), f32),
        scratch_shapes=[pltpu.SMEM((1, tm * TOP_K), i32), pltpu.VMEM((TOP_K, tm, DW), u32),
                        pltpu.SemaphoreType.DMA, pltpu.SemaphoreType.DMA],
        compiler_params=_cparams(("arbitrary",)),
        name="moe_combine",
    )(dest_tiles, gate_tk, ys, h2p, x1, wsg, wsu, wsd, mod3, g_post.reshape(1, D))


def _tile(n, pref):
    t = min(n, pref)
    assert n % t == 0, (n, t)
    return t


def _layer(i, x2, mod3, p, wexp, v_first, B, S, cfg):
    T, D = x2.shape
    H = (D // 2) // ATT_V_DIM
    W = D - D // 2
    att_cols = 2 * H * 2 * ATT_QK_DIM + H * ATT_V_DIM
    lam_init = 0.8 - 0.6 * math.exp(-0.3 * i)

    w_in_bf = p["w_in"].astype(bf16)
    att = _inproj(x2, mod3, p["g_pre_mix"], w_in_bf[:, :att_cols], bf16, S,
                  _tile(S, cfg["tm_in"]), cfg["tn_att"], 1, 0)
    feats = _inproj(x2, mod3, p["g_pre_mix"], w_in_bf[:, att_cols:], f32, S,
                    _tile(S, cfg["tm_in_rwkv"]), cfg["tn_rwkv"], 1, 0)

    slopes = jnp.broadcast_to(
        (2.0 ** (-ALIBI_MAX_BIAS * jnp.arange(1, H + 1, dtype=f32) / H))[:, None, None], (H, 1, LANES))
    lamp = jnp.stack([p["lam_q1"], p["lam_k1"], p["lam_q2"], p["lam_k2"]])
    o_att = _attention(att, slopes, lamp, p["att_subln_g"], B, S, H, lam_init, _tile(S, cfg["tq"]))

    cols = feats.shape[1]
    zw = jnp.zeros((RWKV_A_RANK, W), f32)
    heads = W // RWKV_HEAD
    ind = (jnp.arange(W)[:, None] // RWKV_HEAD == jnp.arange(LANES)[None, :]).astype(bf16)
    prm = {
        "mu": p["rwkv_mu"].reshape(1, cols), "w0": p["rwkv_w0"].reshape(1, W),
        "w2p": jnp.concatenate([p["rwkv_w2"], zw], axis=0),
        "a0": p["rwkv_a0"].reshape(1, W),
        "a2p": jnp.concatenate([jnp.zeros((RWKV_W_RANK, W), f32), p["rwkv_a2"]], axis=0),
        "g2": p["rwkv_g2"], "k_k": p["rwkv_k_k"].reshape(1, W), "k_a": p["rwkv_k_a"].reshape(1, W),
        "ind": ind, "indt": ind.T,
    }
    if v_first is not None:
        padc = LANES - RWKV_V_RANK
        prm["v0"] = p["rwkv_v0"].reshape(1, W)
        prm["v1p"] = jnp.pad(p["rwkv_v1"], ((0, 0), (0, padc)))
        prm["v2p"] = jnp.pad(p["rwkv_v2"], ((0, padc), (0, 0)))
    r, lw, kh, v, kn, kb, g = _rwkv_prep(feats, prm, v_first, B, S, W, _tile(S, cfg["tm_prep"]))
    if v_first is None:
        v_first = v
    r2, y0, mmat, g0 = _wkv_intra(r, lw, kh, v, kn, kb, min(cfg["nc"], S // CHUNK), cfg["passes_intra"])
    o_rwkv = _wkv_state(r2, y0, mmat, g0, r, kh, v, g, p["rwkv_lnx_w"].reshape(1, W),
                        p["rwkv_lnx_b"].reshape(1, W), p["rwkv_r_k"].reshape(1, W), B, S,
                        min(cfg["pg"], W // LANES), min(cfg["cb"], S // CHUNK), cfg["passes_state"])
    del heads

    w_out_bf = p["w_out"].astype(bf16)
    x1, h2, logits_t = _postmix(o_att, o_rwkv, w_out_bf[:D // 2], w_out_bf[D // 2:], x2, mod3,
                                p["g_post_mix"], p["g_pre_ffn"], p["w_router"].T, S, _tile(S, cfg["tm_post"]))

    eidx_t, gate_t, rank_t, counts = _router(logits_t, p["router_bias"], _tile(T, cfg["tm_router"]))
    blk = cfg["blk"]
    counts = counts[:, 0]
    padded = (counts + blk - 1) // blk * blk
    pad_end = jnp.cumsum(padded)
    pad_start = pad_end - padded
    e_ids = jnp.arange(N_EXPERTS, dtype=i32)
    start_of = jnp.sum(jnp.where(eidx_t[:, :, None] == e_ids, pad_start, 0), axis=-1)
    dest = (start_of + rank_t).T
    n_assign = T * TOP_K
    nblk = -(-n_assign // blk) + N_EXPERTS
    P = nblk * blk
    blk_start = jnp.arange(nblk, dtype=i32) * blk
    nused = (pad_end[-1] // blk).astype(i32).reshape(1)
    blk_pos = jnp.minimum(blk_start, pad_end[-1] - blk)
    blk_e = jnp.minimum(jnp.sum((pad_end[None, :] <= blk_pos[:, None]).astype(i32), axis=1), N_EXPERTS - 1)

    tm_d = _tile(T, cfg["tm_disp"])
    xs = _dispatch(counts.astype(i32), pad_start.astype(i32), nused,
                   dest.reshape(T // tm_d, 1, tm_d * TOP_K), h2, P, blk, tm_d)
    ys = _experts(blk_e, nused, xs, wexp[0], wexp[1], wexp[2], i, blk)
    tm_c = _tile(S, cfg["tm_comb"])
    x_out = _combine(dest.reshape(T // tm_c, 1, tm_c * TOP_K), gate_t.T, ys, h2, x1,
                     p["w_sh_gate"].astype(bf16), p["w_sh_up"].astype(bf16), p["w_sh_down"].astype(bf16),
                     mod3, p["g_post_ffn"], S, tm_c)
    return x_out, v_first


_CFG = dict(tm_in=1024, tn_att=1024, tm_in_rwkv=512, tn_rwkv=1664, tq=256, tm_prep=256, nc=8, passes_intra=1, passes_state=3, pg=4, cb=8,
            tm_post=256, tm_router=512, blk=256, tm_disp=128, tm_comb=128)

_LAYER_KEYS = ("g_pre_mix", "g_post_mix", "g_pre_ffn", "g_post_ffn", "w_in", "w_out", "lam_q1", "lam_k1",
               "lam_q2", "lam_k2", "att_subln_g", "rwkv_mu", "rwkv_w0", "rwkv_w2", "rwkv_a0", "rwkv_a2",
               "rwkv_g2", "rwkv_k_k", "rwkv_k_a", "rwkv_r_k", "rwkv_lnx_w", "rwkv_lnx_b", "w_router",
               "router_bias", "w_sh_gate", "w_sh_up", "w_sh_down")


def _forward(x, c, params, cfg):
    B, S, D = x.shape
    L = params["w_in"].shape[0]
    bp = 16
    c_pad = jnp.zeros((bp, D), f32).at[:B].set(c)
    mod = _ada_mod(c_pad, params["w_ada"], params["b_ada"])
    x2 = x.reshape(B * S, D)
    v_first = None
    for i in range(L):
        p = {k: params[k][i] for k in _LAYER_KEYS}
        if i > 0:
            p["rwkv_v0"] = params["rwkv_v0"][i - 1]
            p["rwkv_v1"] = params["rwkv_v1"][i - 1]
            p["rwkv_v2"] = params["rwkv_v2"][i - 1]
        mod3 = mod[i, :B].reshape(B * N_MOD, 1, D)
        wexp = (params["w_exp_gate"], params["w_exp_up"], params["w_exp_down"])
        x2, v_first = _layer(i, x2, mod3, p, wexp, v_first, B, S, cfg)
    return x2.reshape(B, S, D)


def kernel(x, c, w_ada, b_ada, g_pre_mix, g_post_mix, g_pre_ffn, g_post_ffn, w_in, w_out, lam_q1, lam_k1, lam_q2, lam_k2, att_subln_g, rwkv_mu, rwkv_w0, rwkv_w2, rwkv_a0, rwkv_a2, rwkv_g2, rwkv_k_k, rwkv_k_a, rwkv_r_k, rwkv_lnx_w, rwkv_lnx_b, rwkv_v0, rwkv_v1, rwkv_v2, w_router, router_bias, w_exp_gate, w_exp_up, w_exp_down, w_sh_gate, w_sh_up, w_sh_down):
    params = dict(w_ada=w_ada, b_ada=b_ada, g_pre_mix=g_pre_mix, g_post_mix=g_post_mix, g_pre_ffn=g_pre_ffn,
                  g_post_ffn=g_post_ffn, w_in=w_in, w_out=w_out, lam_q1=lam_q1, lam_k1=lam_k1, lam_q2=lam_q2,
                  lam_k2=lam_k2, att_subln_g=att_subln_g, rwkv_mu=rwkv_mu, rwkv_w0=rwkv_w0, rwkv_w2=rwkv_w2,
                  rwkv_a0=rwkv_a0, rwkv_a2=rwkv_a2, rwkv_g2=rwkv_g2, rwkv_k_k=rwkv_k_k, rwkv_k_a=rwkv_k_a,
                  rwkv_r_k=rwkv_r_k, rwkv_lnx_w=rwkv_lnx_w, rwkv_lnx_b=rwkv_lnx_b, rwkv_v0=rwkv_v0,
                  rwkv_v1=rwkv_v1, rwkv_v2=rwkv_v2, w_router=w_router, router_bias=router_bias,
                  w_exp_gate=w_exp_gate, w_exp_up=w_exp_up, w_exp_down=w_exp_down, w_sh_gate=w_sh_gate,
                  w_sh_up=w_sh_up, w_sh_down=w_sh_down)
    return _forward(x, c, params, _CFG)
```

```python
import functools
import math

import jax
import jax.numpy as jnp
from jax import lax
from jax.experimental import pallas as pl
from jax.experimental.pallas import tpu as pltpu

f32 = jnp.float32
bf16 = jnp.bfloat16
i32 = jnp.int32
u32 = jnp.uint32

ATT_QK_DIM = 64
ATT_V_DIM = 128
ALIBI_MAX_BIAS = 8.0
ATT_SUBLN_EPS = 1e-5
RWKV_HEAD = 64
RWKV_W_RANK = 64
RWKV_A_RANK = 64
RWKV_G_RANK = 128
RWKV_V_RANK = 32
RWKV_GN_EPS = 64e-5
N_EXPERTS = 64
N_GROUPS = 8
TOPK_GROUPS = 4
TOP_K = 8
ROUTED_SCALE = 2.5
NORM_EPS = 1e-6
N_MOD = 6

LANES = 128
SUBLANES = 8
CHUNK = 64
VMEM_LIMIT = 56 * 1024 * 1024


def _cparams(sem):
    return pltpu.CompilerParams(dimension_semantics=sem, vmem_limit_bytes=VMEM_LIMIT)


def _dot(a, b):
    return jnp.dot(a, b, preferred_element_type=f32)


def _dot_nt(a, b):
    return lax.dot_general(a, b, (((1,), (1,)), ((), ())), preferred_element_type=f32)


def _split2(x):
    hi = x.astype(bf16)
    lo = (x - hi.astype(f32)).astype(bf16)
    return hi, lo


def _mm(a, b, passes=1):
    if passes == 1:
        return _dot(a.astype(bf16), b.astype(bf16))
    ah, al = _split2(a)
    bh, bl = _split2(b)
    return (_dot(al, bh) + _dot(ah, bl)) + _dot(ah, bh)


def _mm_exact_rhs(a, b_bf16):
    ah, al = _split2(a)
    return _dot(al, b_bf16) + _dot(ah, b_bf16)


def _sigmoid(x):
    return 1.0 / (1.0 + jnp.exp(-x))


def _pack_rows(x):
    half = x.shape[1] // 2
    a = x[:, :half].astype(bf16).astype(f32)
    b = x[:, half:].astype(bf16).astype(f32)
    lo = lax.shift_right_logical(lax.bitcast_convert_type(a, u32), jnp.uint32(16))
    hi = lax.bitcast_convert_type(b, u32) & jnp.uint32(0xFFFF0000)
    return lo | hi


def _unpack_rows(w):
    lo = lax.bitcast_convert_type(lax.shift_left(w, jnp.uint32(16)), f32)
    hi = lax.bitcast_convert_type(w & jnp.uint32(0xFFFF0000), f32)
    return lo, hi


def _ada_kernel(c_ref, w_ref, b_ref, o_ref):
    c = c_ref[...]
    cond = (c * _sigmoid(c)).astype(bf16)
    o_ref[0] = _dot(cond, w_ref[0].astype(bf16)) + b_ref[0]


def _ada_mod(c_pad, w_ada, b_ada, tn=1024):
    L, D, N = w_ada.shape
    bp = c_pad.shape[0]
    return pl.pallas_call(
        _ada_kernel,
        grid=(L, N // tn),
        in_specs=[pl.BlockSpec((bp, D), lambda l, j: (0, 0)),
                  pl.BlockSpec((1, D, tn), lambda l, j: (l, 0, j)),
                  pl.BlockSpec((1, 1, tn), lambda l, j: (l, 0, j))],
        out_specs=pl.BlockSpec((1, bp, tn), lambda l, j: (l, 0, j)),
        out_shape=jax.ShapeDtypeStruct((L, bp, N), f32),
        compiler_params=_cparams(("arbitrary", "arbitrary")),
        name="ada_mod",
    )(c_pad, w_ada, b_ada.reshape(L, 1, N))


def _inproj_kernel(x_ref, sc_ref, sh_ref, g_ref, w_ref, o_ref, h_ref):
    @pl.when(pl.program_id(1) == 0)
    def _():
        x = x_ref[...]
        ms = jnp.mean(x * x, axis=-1, keepdims=True)
        y = x * lax.rsqrt(ms + NORM_EPS) * g_ref[...]
        h_ref[...] = (y * (1.0 + sc_ref[0]) + sh_ref[0]).astype(bf16)

    o_ref[...] = _dot(h_ref[...], w_ref[...]).astype(o_ref.dtype)


def _inproj(x2, mod3, g, w_bf, out_dtype, S, tm, tn, seg_sc, seg_sh):
    T, D = x2.shape
    N = w_bf.shape[1]
    tpb = S // tm
    return pl.pallas_call(
        _inproj_kernel,
        grid=(T // tm, N // tn),
        in_specs=[pl.BlockSpec((tm, D), lambda i, j: (i, 0)),
                  pl.BlockSpec((1, 1, D), lambda i, j: ((i // tpb) * N_MOD + seg_sc, 0, 0)),
                  pl.BlockSpec((1, 1, D), lambda i, j: ((i // tpb) * N_MOD + seg_sh, 0, 0)),
                  pl.BlockSpec((1, D), lambda i, j: (0, 0)),
                  pl.BlockSpec((D, tn), lambda i, j: (0, j))],
        out_specs=pl.BlockSpec((tm, tn), lambda i, j: (i, j)),
        out_shape=jax.ShapeDtypeStruct((T, N), out_dtype),
        scratch_shapes=[pltpu.VMEM((tm, D), bf16)],
        compiler_params=_cparams(("arbitrary", "arbitrary")),
        name="inproj",
    )(x2, mod3, mod3, g.reshape(1, D), w_bf)


def _attn_kernel(q_ref, k_ref, v_ref, slope_ref, lamp_ref, g_ref, o_ref,
                 q2t_ref, vt_ref, m_ref, l_ref, acc_ref, *, tq, lam_init):
    qi = pl.program_id(2)
    scale = ATT_QK_DIM ** -0.5
    slope = slope_ref[0][:, 0:1]

    @pl.when(qi == 0)
    def _():
        vt_ref[...] = v_ref[...].astype(f32).T.astype(bf16)

    qt = (q_ref[...].astype(f32) * scale).T
    dim = lax.broadcasted_iota(i32, (LANES, 1), 0)
    first = dim < ATT_QK_DIM
    q2t_ref[:, 0:tq] = jnp.where(first, qt, 0.0).astype(bf16)
    q2t_ref[:, tq:2 * tq] = jnp.where(first, 0.0, qt).astype(bf16)
    m_ref[...] = jnp.full(m_ref.shape, -jnp.inf, f32)
    l_ref[...] = jnp.zeros(l_ref.shape, f32)
    acc_ref[...] = jnp.zeros(acc_ref.shape, f32)

    kr = lax.broadcasted_iota(i32, (tq, 2 * tq), 0)
    qc = lax.broadcasted_iota(i32, (tq, 2 * tq), 1)
    rel = (jnp.where(qc >= tq, qc - tq, qc) - kr).astype(f32)

    def step(ki, masked):
        start = pl.multiple_of(ki * tq, tq)
        kb = k_ref[pl.ds(start, tq), :]
        vtb = vt_ref[:, pl.ds(start, tq)]
        s = _dot(kb, q2t_ref[...])
        dist = rel + ((qi - ki) * tq).astype(f32)
        s = s - slope * dist
        if masked:
            s = jnp.where(dist >= 0.0, s, -jnp.inf)
        m_prev = m_ref[...]
        m_new = jnp.maximum(m_prev, jnp.max(s, axis=0, keepdims=True))
        alpha = jnp.exp(m_prev - m_new)
        p = jnp.exp(s - m_new)
        l_ref[...] = alpha * l_ref[...] + jnp.sum(p, axis=0, keepdims=True)
        acc_ref[...] = alpha * acc_ref[...] + _dot(vtb, p.astype(bf16))
        m_ref[...] = m_new

    def body(ki, carry):
        step(ki, False)
        return carry

    lax.fori_loop(0, qi, body, 0)
    step(qi, True)

    lp = lamp_ref[...]
    lam = (jnp.exp(jnp.sum(lp[0:1] * lp[1:2], axis=-1, keepdims=True))
           - jnp.exp(jnp.sum(lp[2:3] * lp[3:4], axis=-1, keepdims=True)) + lam_init)
    on = acc_ref[...] * (1.0 / l_ref[...])
    o = on[:, 0:tq] - lam * on[:, tq:2 * tq]
    o = o * lax.rsqrt(jnp.mean(o * o, axis=0, keepdims=True) + ATT_SUBLN_EPS)
    o = o * g_ref[...] * (1.0 - lam_init)
    o_ref[...] = o.T.astype(o_ref.dtype)


def _attention(att, slopes, lamp, subln_g, B, S, H, lam_init, tq):
    T = att.shape[0]
    nq = S // tq
    kern = functools.partial(_attn_kernel, tq=tq, lam_init=lam_init)
    return pl.pallas_call(
        kern,
        grid=(B, H, nq),
        in_specs=[pl.BlockSpec((tq, LANES), lambda b, h, q: (b * nq + q, h)),
                  pl.BlockSpec((S, LANES), lambda b, h, q: (b, H + h)),
                  pl.BlockSpec((S, LANES), lambda b, h, q: (b, 2 * H + h)),
                  pl.BlockSpec((1, 1, LANES), lambda b, h, q: (h, 0, 0)),
                  pl.BlockSpec((4, ATT_QK_DIM), lambda b, h, q: (0, 0)),
                  pl.BlockSpec((ATT_V_DIM, 1), lambda b, h, q: (0, 0))],
        out_specs=pl.BlockSpec((tq, LANES), lambda b, h, q: (b * nq + q, h)),
        out_shape=jax.ShapeDtypeStruct((T, H * ATT_V_DIM), bf16),
        scratch_shapes=[pltpu.VMEM((LANES, 2 * tq), bf16),
                        pltpu.VMEM((LANES, S), bf16),
                        pltpu.VMEM((1, 2 * tq), f32),
                        pltpu.VMEM((1, 2 * tq), f32),
                        pltpu.VMEM((LANES, 2 * tq), f32)],
        compiler_params=_cparams(("arbitrary", "arbitrary", "arbitrary")),
        name="diff_attention",
    )(att, att, att, slopes, lamp, subln_g.reshape(ATT_V_DIM, 1))


def _head_sums(x, ind, indt):
    s = _mm_exact_rhs(x, ind)
    return _mm_exact_rhs(s, indt)


def _rwkv_prep_kernel(*refs, W, has_vres):
    if has_vres:
        (f_ref, mu_ref, w0_ref, w2_ref, a0_ref, a2_ref, g2_ref, kk_ref, ka_ref, ind_ref, indt_ref,
         vf_ref, v0_ref, v1_ref, v2_ref,
         r_o, lw_o, kh_o, v_o, kn_o, kb_o, g_o, carry_ref) = refs
    else:
        (f_ref, mu_ref, w0_ref, w2_ref, a0_ref, a2_ref, g2_ref, kk_ref, ka_ref, ind_ref, indt_ref,
         r_o, lw_o, kh_o, v_o, kn_o, kb_o, g_o, carry_ref) = refs

    ti = pl.program_id(1)
    h = f_ref[...]
    tm = h.shape[0]

    @pl.when(ti == 0)
    def _():
        carry_ref[...] = jnp.zeros(carry_ref.shape, f32)

    rolled = pltpu.roll(h, 1, axis=0)
    row = lax.broadcasted_iota(i32, (tm, 1), 0)
    prev = jnp.where(row == 0, carry_ref[...], rolled)
    carry_ref[...] = h[tm - 1:tm, :]
    feats = h + (prev - h) * mu_ref[...]

    r = feats[:, 0:W]
    k = feats[:, W:2 * W]
    v = feats[:, 2 * W:3 * W]
    wa = feats[:, 3 * W:3 * W + LANES]
    g_lo = feats[:, 3 * W + LANES:3 * W + 2 * LANES]

    w = w0_ref[...] + _mm(jnp.tanh(wa), w2_ref[...], passes=3)
    lw_o[...] = -math.exp(-0.5) * _sigmoid(w)
    a = _sigmoid(a0_ref[...] + _mm(wa, a2_ref[...], passes=3))
    g_o[...] = _mm(_sigmoid(g_lo), g2_ref[...])

    if has_vres:
        mix = _sigmoid(v0_ref[...] + _mm(_mm(v, v1_ref[...]), v2_ref[...]))
        v = v + (vf_ref[...] - v) * mix

    kk = k * kk_ref[...]
    ss = _head_sums(kk * kk, ind_ref[...], indt_ref[...])
    kk = kk / jnp.maximum(jnp.sqrt(ss), 1e-12)
    r_o[...] = r
    kh_o[...] = k * (1.0 + (a - 1.0) * ka_ref[...])
    v_o[...] = v
    kn_o[...] = kk
    kb_o[...] = kk * a


def _rwkv_prep(feats, prm, vfirst, B, S, W, tm):
    T, COLS = feats.shape
    tpb = S // tm
    has_vres = vfirst is not None
    row = lambda n: pl.BlockSpec((1, n), lambda b, t: (0, 0))
    full = lambda a: pl.BlockSpec(a.shape, lambda b, t: (0, 0))
    tile = pl.BlockSpec((tm, W), lambda b, t: (b * tpb + t, 0))
    args = [feats, prm["mu"], prm["w0"], prm["w2p"], prm["a0"], prm["a2p"], prm["g2"], prm["k_k"], prm["k_a"],
            prm["ind"], prm["indt"]]
    specs = [pl.BlockSpec((tm, COLS), lambda b, t: (b * tpb + t, 0)), row(COLS), row(W), full(prm["w2p"]),
             row(W), full(prm["a2p"]), full(prm["g2"]), row(W), row(W), full(prm["ind"]), full(prm["indt"])]
    if has_vres:
        args += [vfirst, prm["v0"], prm["v1p"], prm["v2p"]]
        specs += [tile, row(W), full(prm["v1p"]), full(prm["v2p"])]
    out = jax.ShapeDtypeStruct((T, W), f32)
    kern = functools.partial(_rwkv_prep_kernel, W=W, has_vres=has_vres)
    return pl.pallas_call(
        kern,
        grid=(B, tpb),
        in_specs=specs,
        out_specs=[tile] * 7,
        out_shape=[out] * 7,
        scratch_shapes=[pltpu.VMEM((1, COLS), f32)],
        compiler_params=_cparams(("arbitrary", "arbitrary")),
        name="rwkv_prep",
    )(*args)


def _wkv_chunks(rs, lws, ks, vs, kns, kbs, passes):
    C = CHUNK
    P2 = 2 * C
    n = range(len(rs))
    ri = lax.broadcasted_iota(i32, (C, C), 0)
    ci = lax.broadcasted_iota(i32, (C, C), 1)
    tri = (ci <= ri).astype(bf16)
    lane = lax.broadcasted_iota(i32, (1, LANES), 1)
    m0 = (lane < RWKV_HEAD).astype(f32)
    m1 = 1.0 - m0
    rr = lax.broadcasted_iota(i32, (P2, P2), 0)
    cc = lax.broadcasted_iota(i32, (P2, P2), 1)
    same = jnp.where(rr >= C, 1, 0) == jnp.where(cc >= C, 1, 0)
    strict = same & (cc < rr)
    incl = same & (cc <= rr)
    incl2 = jnp.concatenate([incl, incl], axis=1)
    eye = (rr == cc).astype(f32)
    zeros_p = jnp.zeros((P2, LANES), f32)
    zeros_c = jnp.zeros((C, LANES), f32)
    stack = lambda x: jnp.concatenate([x * m0, x * m1], axis=0)
    fold = lambda x: x[0:C] + x[C:2 * C]

    def cumsum(lw):
        h1 = lw.astype(bf16)
        r1 = lw - h1.astype(f32)
        h2 = r1.astype(bf16)
        h3 = (r1 - h2.astype(f32)).astype(bf16)
        return (_dot(tri, h3) + _dot(tri, h2)) + _dot(tri, h1)

    cum = [cumsum(lws[j]) for j in n]
    cum_c = [cum[j][C - 1:C, :] for j in n]
    at = [-kns[j] * jnp.exp(cum[j] - lws[j]) for j in n]
    rt = [rs[j] * jnp.exp(cum[j]) for j in n]
    einv = [jnp.exp(-cum[j]) for j in n]
    bt = [kbs[j] * einv[j] for j in n]
    kt = [ks[j] * einv[j] for j in n]
    eh = [jnp.exp(cum_c[j] - cum[j]) for j in n]
    bh = [kbs[j] * eh[j] for j in n]
    kh = [ks[j] * eh[j] for j in n]
    w_c = [jnp.exp(cum_c[j]) for j in n]
    abd = [stack(at[j]) for j in n]
    vst = [stack(vs[j]) for j in n]
    lhs = [jnp.concatenate([abd[j], stack(rt[j])], axis=0) for j in n]
    rhs = [jnp.concatenate([stack(bt[j]), stack(kt[j])], axis=0) for j in n]
    gram = [_mm_nt(lhs[j], rhs[j], passes) for j in n]
    lab = [jnp.where(strict, gram[j][0:P2, 0:P2], 0.0) for j in n]
    lak = [jnp.where(strict, gram[j][0:P2, P2:2 * P2], 0.0) for j in n]
    mrbk = [jnp.where(incl2, gram[j][P2:2 * P2, :], 0.0) for j in n]

    x0 = [_mm(lak[j], vst[j], passes) for j in n]
    tinv = [eye + lab[j] for j in n]
    lp = lab
    for _ in range(int(math.log2(C)) - 1):
        lp = [_mm(lp[j], lp[j], passes) for j in n]
        tinv = [tinv[j] + _mm(lp[j], tinv[j], passes) for j in n]

    ta = [_mm(tinv[j], jnp.concatenate([abd[j], x0[j]], axis=1), passes) for j in n]
    rhs2 = [jnp.concatenate([ta[j], jnp.concatenate([zeros_p, vst[j]], axis=1)], axis=0) for j in n]
    z = [_mm(mrbk[j], rhs2[j], passes) for j in n]
    r2 = [rt[j] + fold(z[j][:, 0:LANES]) for j in n]
    y0 = [fold(z[j][:, LANES:2 * LANES]) for j in n]
    lhs3t = [jnp.concatenate([bh[j], kh[j]], axis=0).T for j in n]
    rhs3 = [jnp.concatenate([fold(ta[j]), jnp.concatenate([zeros_c, vs[j]], axis=1)], axis=0) for j in n]
    wmat = [_mm(lhs3t[j], rhs3[j], passes) for j in n]
    mmat = [jnp.where(same, wmat[j][:, 0:LANES], 0.0) + eye * w_c[j] for j in n]
    g0 = [jnp.where(same, wmat[j][:, LANES:2 * LANES], 0.0) for j in n]
    return r2, y0, mmat, g0


def _mm_nt(a, b, passes):
    if passes == 1:
        return _dot_nt(a.astype(bf16), b.astype(bf16))
    ah, al = _split2(a)
    bh, bl = _split2(b)
    return (_dot_nt(al, bh) + _dot_nt(ah, bl)) + _dot_nt(ah, bh)


def _wkv_intra_kernel(r_ref, lw_ref, k_ref, v_ref, kn_ref, kb_ref, r2_o, y0_o, m_o, g_o, *, nc, passes):
    C = CHUNK
    sls = [slice(c * C, (c + 1) * C) for c in range(nc)]
    take = lambda ref: [ref[sl, :] for sl in sls]
    r2, y0, mmat, g0 = _wkv_chunks(take(r_ref), take(lw_ref), take(k_ref), take(v_ref),
                                   take(kn_ref), take(kb_ref), passes)
    for c, sl in enumerate(sls):
        r2_o[sl, :] = r2[c]
        y0_o[sl, :] = y0[c]
        m_o[0, c] = mmat[c]
        g_o[0, c] = g0[c]


def _wkv_intra(r, lw, kh, v, kn, kb, nc, passes):
    T, W = r.shape
    npair = W // LANES
    rows = nc * CHUNK
    tile = pl.BlockSpec((rows, LANES), lambda p, i: (i, p))
    mat = pl.BlockSpec((1, nc, LANES, LANES), lambda p, i: (p, i, 0, 0))
    kern = functools.partial(_wkv_intra_kernel, nc=nc, passes=passes)
    return pl.pallas_call(
        kern,
        grid=(npair, T // rows),
        in_specs=[tile] * 6,
        out_specs=[tile, tile, mat, mat],
        out_shape=[jax.ShapeDtypeStruct((T, W), f32), jax.ShapeDtypeStruct((T, W), f32),
                   jax.ShapeDtypeStruct((npair, T // CHUNK, LANES, LANES), f32),
                   jax.ShapeDtypeStruct((npair, T // CHUNK, LANES, LANES), f32)],
        compiler_params=_cparams(("arbitrary", "arbitrary")),
        name="wkv_intra",
    )(r, lw, kh, v, kn, kb)


def _wkv_state_kernel(r2_ref, y0_ref, m_ref, g0_ref, r_ref, kh_ref, v_ref, g_ref,
                      lnw_ref, lnb_ref, rk_ref, o_ref, st_ref, y_ref, *, pg, cb, passes):
    C = CHUNK

    @pl.when(pl.program_id(2) == 0)
    def _():
        st_ref[...] = jnp.zeros(st_ref.shape, f32)

    pairs = range(pg)
    lanes = [slice(p * LANES, (p + 1) * LANES) for p in pairs]
    st = [st_ref[p] for p in pairs]
    for c in range(cb):
        rows = slice(c * C, (c + 1) * C)
        for p in pairs:
            y_ref[rows, lanes[p]] = _mm(r2_ref[rows, lanes[p]], st[p], passes) + y0_ref[rows, lanes[p]]
        st = [_mm(m_ref[p, c], st[p], passes) + g0_ref[p, c] for p in pairs]
    for p in pairs:
        st_ref[p] = st[p]

    rr = lax.broadcasted_iota(i32, (LANES, LANES), 0)
    cc = lax.broadcasted_iota(i32, (LANES, LANES), 1)
    ones_bd = (jnp.where(rr >= RWKV_HEAD, 1, 0) == jnp.where(cc >= RWKV_HEAD, 1, 0)).astype(bf16)
    for p in pairs:
        y = y_ref[:, lanes[p]]
        mu = _mm_exact_rhs(y, ones_bd) * (1.0 / RWKV_HEAD)
        d = y - mu
        var = _mm_exact_rhs(d * d, ones_bd) * (1.0 / RWKV_HEAD)
        yn = d * lax.rsqrt(var + RWKV_GN_EPS) * lnw_ref[:, lanes[p]] + lnb_ref[:, lanes[p]]
        rk = r_ref[:, lanes[p]] * kh_ref[:, lanes[p]] * rk_ref[:, lanes[p]]
        bonus = _mm_exact_rhs(rk, ones_bd) * v_ref[:, lanes[p]]
        o_ref[:, lanes[p]] = ((yn + bonus) * g_ref[:, lanes[p]]).astype(o_ref.dtype)


def _wkv_state(r2, y0, mm, g0, r, kh, v, g, lnw, lnb, rk, B, S, pg, cb, passes):
    T, W = r.shape
    npair = W // LANES
    rows = cb * CHUNK
    steps = S // rows
    seq = pl.BlockSpec((rows, pg * LANES), lambda b, q, c: (b * steps + c, q))
    mat = pl.BlockSpec((pg, cb, LANES, LANES), lambda b, q, c: (q, b * steps + c, 0, 0))
    prow = pl.BlockSpec((1, pg * LANES), lambda b, q, c: (0, q))
    kern = functools.partial(_wkv_state_kernel, pg=pg, cb=cb, passes=passes)
    return pl.pallas_call(
        kern,
        grid=(B, npair // pg, steps),
        in_specs=[seq, seq, mat, mat, seq, seq, seq, seq, prow, prow, prow],
        out_specs=seq,
        out_shape=jax.ShapeDtypeStruct((T, W), bf16),
        scratch_shapes=[pltpu.VMEM((pg, LANES, LANES), f32), pltpu.VMEM((rows, pg * LANES), f32)],
        compiler_params=_cparams(("arbitrary", "arbitrary", "arbitrary")),
        name="wkv_state",
    )(r2, y0, mm, g0, r, kh, v, g, lnw, lnb, rk)


def _postmix_kernel(oa_ref, orw_ref, wa_ref, wr_ref, x_ref, ga_ref, gpost_ref, gpre_ref, sc_ref, sh_ref, wrt_ref,
                    x1_o, h2_o, lg_o):
    mixed = _dot(oa_ref[...], wa_ref[...]) + _dot(orw_ref[...], wr_ref[...])
    ms = jnp.mean(mixed * mixed, axis=-1, keepdims=True)
    x1 = x_ref[...] + ga_ref[0] * (mixed * lax.rsqrt(ms + NORM_EPS) * gpost_ref[...])
    x1_o[...] = x1
    ms1 = jnp.mean(x1 * x1, axis=-1, keepdims=True)
    h2 = (x1 * lax.rsqrt(ms1 + NORM_EPS) * gpre_ref[...]) * (1.0 + sc_ref[0]) + sh_ref[0]
    h2_o[...] = _pack_rows(h2)
    lg_o[...] = _mm_nt(wrt_ref[...], h2, 3)


def _postmix(o_att, o_rwkv, w_out_a, w_out_r, x2, mod3, g_post, g_pre, w_rt, S, tm):
    T, D = x2.shape
    WA = o_att.shape[1]
    WR = o_rwkv.shape[1]
    E = w_rt.shape[0]
    tpb = S // tm
    modspec = lambda seg: pl.BlockSpec((1, 1, D), lambda i: ((i // tpb) * N_MOD + seg, 0, 0))
    tile = pl.BlockSpec((tm, D), lambda i: (i, 0))
    return pl.pallas_call(
        _postmix_kernel,
        grid=(T // tm,),
        in_specs=[pl.BlockSpec((tm, WA), lambda i: (i, 0)),
                  pl.BlockSpec((tm, WR), lambda i: (i, 0)),
                  pl.BlockSpec((WA, D), lambda i: (0, 0)),
                  pl.BlockSpec((WR, D), lambda i: (0, 0)),
                  tile, modspec(2),
                  pl.BlockSpec((1, D), lambda i: (0, 0)),
                  pl.BlockSpec((1, D), lambda i: (0, 0)),
                  modspec(4), modspec(3),
                  pl.BlockSpec((E, D), lambda i: (0, 0))],
        out_specs=[tile, pl.BlockSpec((tm, D // 2), lambda i: (i, 0)),
                   pl.BlockSpec((E, tm), lambda i: (0, i))],
        out_shape=[jax.ShapeDtypeStruct((T, D), f32), jax.ShapeDtypeStruct((T, D // 2), u32),
                   jax.ShapeDtypeStruct((E, T), f32)],
        compiler_params=_cparams(("arbitrary",)),
        name="postmix",
    )(o_att, o_rwkv, w_out_a, w_out_r, x2, mod3, g_post.reshape(1, D), g_pre.reshape(1, D), mod3, mod3, w_rt)


def _first_max(x, iota, n):
    mx = jnp.max(x, axis=0, keepdims=True)
    idx = jnp.min(jnp.where(x == mx, iota, n), axis=0, keepdims=True)
    return mx, idx


def _router_kernel(lg_ref, bias_ref, eidx_o, gate_o, rank_o, cnt_o, cnt_ref):
    E = N_EXPERTS
    G = N_GROUPS
    per = E // G
    tm = lg_ref.shape[1]

    @pl.when(pl.program_id(0) == 0)
    def _():
        cnt_ref[...] = jnp.zeros(cnt_ref.shape, f32)

    scores = _sigmoid(lg_ref[...])
    biased = scores + bias_ref[...]
    neg = -jnp.inf

    iota_p = lax.broadcasted_iota(i32, (per, tm), 0).astype(f32)
    gs = []
    for g in range(G):
        xg = biased[g * per:(g + 1) * per, :]
        m1, i1 = _first_max(xg, iota_p, per)
        m2 = jnp.max(jnp.where(iota_p == i1, neg, xg), axis=0, keepdims=True)
        gs.append(m1 + m2)
    gsc = jnp.concatenate(gs, axis=0)
    iota_g = lax.broadcasted_iota(i32, (G, tm), 0).astype(f32)
    gsel = jnp.zeros((G, tm), f32)
    for _ in range(TOPK_GROUPS):
        _, gi = _first_max(gsc, iota_g, G)
        hit = iota_g == gi
        gsel = jnp.where(hit, 1.0, gsel)
        gsc = jnp.where(hit, neg, gsc)
    masked = jnp.concatenate(
        [jnp.where(gsel[g:g + 1, :] > 0.0, biased[g * per:(g + 1) * per, :], neg) for g in range(G)], axis=0)

    iota_e = lax.broadcasted_iota(i32, (E, tm), 0).astype(f32)
    sel = jnp.zeros((E, tm), f32)
    idxs, vals = [], []
    for _ in range(TOP_K):
        _, ei = _first_max(masked, iota_e, E)
        hit = iota_e == ei
        idxs.append(ei)
        vals.append(jnp.sum(jnp.where(hit, scores, 0.0), axis=0, keepdims=True))
        sel = jnp.where(hit, 1.0, sel)
        masked = jnp.where(hit, neg, masked)
    tot = vals[0]
    for vv in vals[1:]:
        tot = tot + vv
    eidx_o[...] = jnp.concatenate(idxs, axis=0).astype(i32)
    gate_o[...] = jnp.concatenate([vv / tot * ROUTED_SCALE for vv in vals], axis=0)

    rr = lax.broadcasted_iota(i32, (tm, tm), 0)
    cc = lax.broadcasted_iota(i32, (tm, tm), 1)
    before = (rr < cc).astype(bf16)
    pos = _dot(sel.astype(bf16), before) + cnt_ref[...]
    rank_o[...] = jnp.concatenate(
        [jnp.sum(jnp.where(iota_e == ei, pos, 0.0), axis=0, keepdims=True) for ei in idxs], axis=0).astype(i32)
    cnt_ref[...] = cnt_ref[...] + jnp.sum(sel, axis=1, keepdims=True)
    cnt_o[...] = cnt_ref[...].astype(i32)


def _router(logits_t, bias, tm):
    E, T = logits_t.shape
    k_tile = pl.BlockSpec((TOP_K, tm), lambda i: (0, i))
    return pl.pallas_call(
        _router_kernel,
        grid=(T // tm,),
        in_specs=[pl.BlockSpec((E, tm), lambda i: (0, i)),
                  pl.BlockSpec((E, 1), lambda i: (0, 0))],
        out_specs=[k_tile, k_tile, k_tile, pl.BlockSpec((E, 1), lambda i: (0, 0))],
        out_shape=[jax.ShapeDtypeStruct((TOP_K, T), i32), jax.ShapeDtypeStruct((TOP_K, T), f32),
                   jax.ShapeDtypeStruct((TOP_K, T), i32), jax.ShapeDtypeStruct((E, 1), i32)],
        scratch_shapes=[pltpu.VMEM((E, 1), f32)],
        compiler_params=_cparams(("arbitrary",)),
        name="router",
    )(logits_t, bias.reshape(E, 1))


def _row_copy(src_ref, s, dst_ref, d, sem):
    return pltpu.make_async_copy(src_ref.at[pl.ds(s, 1), :], dst_ref.at[pl.ds(d, 1), :], sem)


def _zero_fill(cnt_ref, pstart_ref, nused_ref, z_ref, xs_out, sem, blk, nblk, start):
    def act(cp):
        if start:
            cp.start()
        else:
            cp.wait()

    def per_expert(e, carry):
        c = cnt_ref[e]
        base = pstart_ref[e] + c
        npad = (blk - (c & (blk - 1))) & (blk - 1)
        head = (-base) & (SUBLANES - 1)

        def one_row(j, carry2):
            act(_row_copy(z_ref, 0, xs_out, base + j, sem))
            return carry2

        lax.fori_loop(0, head, one_row, 0)
        rem = npad - head
        aligned = base + head
        p = blk // 2
        while p >= SUBLANES:
            off = pl.multiple_of(aligned + (rem & ~(2 * p - 1)), SUBLANES)

            @pl.when((rem & p) != 0)
            def _(p=p, off=off):
                act(pltpu.make_async_copy(z_ref.at[pl.ds(0, p), :], xs_out.at[pl.ds(off, p), :], sem))

            p //= 2
        return carry

    lax.fori_loop(0, N_EXPERTS, per_expert, 0)

    def per_block(b, carry):
        act(pltpu.make_async_copy(z_ref, xs_out.at[pl.ds(pl.multiple_of(b * blk, blk), blk), :], sem))
        return carry

    lax.fori_loop(nused_ref[0], nblk, per_block, 0)


def _dispatch_kernel(cnt_ref, pstart_ref, nused_ref, dest_hbm, h_ref, xs_out, idx_ref, z_ref, sem_idx, sem, sem_z,
                     *, blk, nblk):
    i = pl.program_id(0)
    tm = h_ref.shape[0]
    cp = pltpu.make_async_copy(dest_hbm.at[i], idx_ref, sem_idx)
    cp.start()
    cp.wait()

    def issue(r, carry):
        for k in range(TOP_K):
            _row_copy(h_ref, r, xs_out, idx_ref[0, r * TOP_K + k], sem).start(priority=k % 2)
        return carry

    lax.fori_loop(0, tm, issue, 0)

    @pl.when(i == pl.num_programs(0) - 1)
    def _():
        z_ref[...] = jnp.zeros(z_ref.shape, u32)
        _zero_fill(cnt_ref, pstart_ref, nused_ref, z_ref, xs_out, sem_z, blk, nblk, True)
        _zero_fill(cnt_ref, pstart_ref, nused_ref, z_ref, xs_out, sem_z, blk, nblk, False)

    for _ in range(TOP_K):
        pltpu.make_async_copy(h_ref, xs_out.at[pl.ds(0, tm), :], sem).wait()


def _dispatch(counts, pad_start, nused, dest_tiles, h2p, P, blk, tm):
    T, DW = h2p.shape
    assert blk & (blk - 1) == 0
    kern = functools.partial(_dispatch_kernel, blk=blk, nblk=P // blk)
    grid_spec = pltpu.PrefetchScalarGridSpec(
        num_scalar_prefetch=3,
        grid=(T // tm,),
        in_specs=[pl.BlockSpec(memory_space=pl.ANY),
                  pl.BlockSpec((tm, DW), lambda i, c, s, n: (i, 0))],
        out_specs=pl.BlockSpec(memory_space=pl.ANY),
        scratch_shapes=[pltpu.SMEM((1, tm * TOP_K), i32), pltpu.VMEM((blk, DW), u32),
                        pltpu.SemaphoreType.DMA, pltpu.SemaphoreType.DMA, pltpu.SemaphoreType.DMA],
    )
    return pl.pallas_call(
        kern,
        grid_spec=grid_spec,
        out_shape=jax.ShapeDtypeStruct((P, DW), u32),
        compiler_params=_cparams(("arbitrary",)),
        name="moe_dispatch",
    )(counts, pad_start, nused, dest_tiles, h2p)


def _experts_kernel(blk_e_ref, nxt_e_ref, nused_ref, xs_ref, wg_hbm, wu_hbm, wd_hbm, ys_ref,
                    wg_f, wu_f, wd_f, wg_s, wu_s, wd_s, sems, *, layer):
    i = pl.program_id(0)
    e = blk_e_ref[i]
    changed = jnp.logical_or(i == 0, e != blk_e_ref[jnp.maximum(i - 1, 0)])

    def weight_copies(ex):
        return (pltpu.make_async_copy(wg_hbm.at[layer, ex], wg_f, sems.at[0]),
                pltpu.make_async_copy(wu_hbm.at[layer, ex], wu_f, sems.at[1]),
                pltpu.make_async_copy(wd_hbm.at[layer, ex], wd_f, sems.at[2]))

    @pl.when(i == 0)
    def _():
        for cp in weight_copies(e):
            cp.start()

    @pl.when(changed)
    def _():
        for cp in weight_copies(e):
            cp.wait()
        for src, dst in ((wg_f, wg_s), (wu_f, wu_s), (wd_f, wd_s)):
            rows = src.shape[0] // 8
            for c in range(8):
                dst[c * rows:(c + 1) * rows, :] = src[c * rows:(c + 1) * rows, :].astype(bf16)
        nxt = nxt_e_ref[i]

        @pl.when(nxt >= 0)
        def _():
            for cp in weight_copies(nxt):
                cp.start()

    @pl.when(i < nused_ref[0])
    def _():
        lo, hi = _unpack_rows(xs_ref[...])
        x = jnp.concatenate([lo.astype(bf16), hi.astype(bf16)], axis=1)
        gt = _dot(x, wg_s[...])
        up = _dot(x, wu_s[...])
        hmid = (gt * _sigmoid(gt)) * up
        ys_ref[...] = _pack_rows(_dot(hmid.astype(bf16), wd_s[...]))

    @pl.when(i >= nused_ref[0])
    def _():
        ys_ref[...] = jnp.zeros(ys_ref.shape, u32)


def _experts(blk_e, nxt_e, nused, xs, w_gate, w_up, w_down, layer, blk):
    P, DW = xs.shape
    D, DE = w_gate.shape[-2:]
    nblk = P // blk
    row_idx = lambda i, be, nx, nu: (jnp.minimum(i, nu[0] - 1), 0)
    hbm = pl.BlockSpec(memory_space=pl.ANY)
    grid_spec = pltpu.PrefetchScalarGridSpec(
        num_scalar_prefetch=3,
        grid=(nblk,),
        in_specs=[pl.BlockSpec((blk, DW), row_idx), hbm, hbm, hbm],
        out_specs=pl.BlockSpec((blk, DW), lambda i, be, nx, nu: (i, 0)),
        scratch_shapes=[pltpu.VMEM((D, DE), f32), pltpu.VMEM((D, DE), f32), pltpu.VMEM((DE, D), f32),
                        pltpu.VMEM((D, DE), bf16), pltpu.VMEM((D, DE), bf16), pltpu.VMEM((DE, D), bf16),
                        pltpu.SemaphoreType.DMA((3,))],
    )
    return pl.pallas_call(
        functools.partial(_experts_kernel, layer=layer),
        grid_spec=grid_spec,
        out_shape=jax.ShapeDtypeStruct((P, DW), u32),
        compiler_params=_cparams(("arbitrary",)),
        name="moe_experts",
    )(blk_e, nxt_e, nused, xs, w_gate, w_up, w_down)


def _combine_kernel(dest_hbm, gate_ref, ys_hbm, h_ref, x1_ref, wsg_ref, wsu_ref, wsd_ref, gf_ref, gpost_ref,
                    x2_o, idx_ref, buf_ref, sem_idx, sem):
    i = pl.program_id(0)
    tm = h_ref.shape[0]
    cp = pltpu.make_async_copy(dest_hbm.at[i], idx_ref, sem_idx)
    cp.start()
    cp.wait()

    def issue(r, carry):
        for k in range(TOP_K):
            _row_copy(ys_hbm, idx_ref[0, r * TOP_K + k], buf_ref.at[k], r, sem).start(priority=k % 2)
        return carry

    lax.fori_loop(0, tm, issue, 0)

    hlo, hhi = _unpack_rows(h_ref[...])
    hb = jnp.concatenate([hlo.astype(bf16), hhi.astype(bf16)], axis=1)
    gt = _dot(hb, wsg_ref[...])
    up = _dot(hb, wsu_ref[...])
    ysh = _dot(((gt * _sigmoid(gt)) * up).astype(bf16), wsd_ref[...])

    for k in range(TOP_K):
        pltpu.make_async_copy(ys_hbm.at[pl.ds(0, tm), :], buf_ref.at[k], sem).wait()

    gate = gate_ref[...]
    half = ysh.shape[1] // 2
    lo = ysh[:, :half]
    hi = ysh[:, half:]
    for k in range(TOP_K):
        a, b = _unpack_rows(buf_ref[k])
        gk = gate[:, k:k + 1]
        lo = lo + gk * a
        hi = hi + gk * b
    y = jnp.concatenate([lo, hi], axis=1)
    ms = jnp.mean(y * y, axis=-1, keepdims=True)
    x2_o[...] = x1_ref[...] + gf_ref[0] * (y * lax.rsqrt(ms + NORM_EPS) * gpost_ref[...])


def _combine(dest_tiles, gate_tk, ys, h2p, x1, wsg, wsu, wsd, mod3, g_post, S, tm):
    T, D = x1.shape
    DW = h2p.shape[1]
    DS = wsg.shape[1]
    tpb = S // tm
    tile = pl.BlockSpec((tm, D), lambda i: (i, 0))
    return pl.pallas_call(
        _combine_kernel,
        grid=(T // tm,),
        in_specs=[pl.BlockSpec(memory_space=pl.ANY),
                  pl.BlockSpec((tm, TOP_K), lambda i: (i, 0)),
                  pl.BlockSpec(memory_space=pl.ANY),
                  pl.BlockSpec((tm, DW), lambda i: (i, 0)), tile,
                  pl.BlockSpec((D, DS), lambda i: (0, 0)),
                  pl.BlockSpec((D, DS), lambda i: (0, 0)),
                  pl.BlockSpec((DS, D), lambda i: (0, 0)),
                  pl.BlockSpec((1, 1, D), lambda i: ((i // tpb) * N_MOD + 5, 0, 0)),
                  pl.BlockSpec((1, D), lambda i: (0, 0))],
        out_specs=tile,
        out_shape=jax.ShapeDtypeStruct((T, D), f32),
        scratch_shapes=[pltpu.SMEM((1, tm * TOP_K), i32), pltpu.VMEM((TOP_K, tm, DW), u32),
                        pltpu.SemaphoreType.DMA, pltpu.SemaphoreType.DMA],
        compiler_params=_cparams(("arbitrary",)),
        name="moe_combine",
    )(dest_tiles, gate_tk, ys, h2p, x1, wsg, wsu, wsd, mod3, g_post.reshape(1, D))


def _tile(n, pref):
    t = min(n, pref)
    assert n % t == 0, (n, t)
    return t


def _layer(i, x2, mod3, p, wexp, v_first, B, S, cfg):
    T, D = x2.shape
    H = (D // 2) // ATT_V_DIM
    W = D - D // 2
    att_cols = 2 * H * 2 * ATT_QK_DIM + H * ATT_V_DIM
    lam_init = 0.8 - 0.6 * math.exp(-0.3 * i)

    w_in_bf = p["w_in"].astype(bf16)
    att = _inproj(x2, mod3, p["g_pre_mix"], w_in_bf[:, :att_cols], bf16, S,
                  _tile(S, cfg["tm_in"]), cfg["tn_att"], 1, 0)
    feats = _inproj(x2, mod3, p["g_pre_mix"], w_in_bf[:, att_cols:], f32, S,
                    _tile(S, cfg["tm_in_rwkv"]), cfg["tn_rwkv"], 1, 0)

    slopes = jnp.broadcast_to(
        (2.0 ** (-ALIBI_MAX_BIAS * jnp.arange(1, H + 1, dtype=f32) / H))[:, None, None], (H, 1, LANES))
    lamp = jnp.stack([p["lam_q1"], p["lam_k1"], p["lam_q2"], p["lam_k2"]])
    o_att = _attention(att, slopes, lamp, p["att_subln_g"], B, S, H, lam_init, _tile(S, cfg["tq"]))

    cols = feats.shape[1]
    zw = jnp.zeros((RWKV_A_RANK, W), f32)
    heads = W // RWKV_HEAD
    ind = (jnp.arange(W)[:, None] // RWKV_HEAD == jnp.arange(LANES)[None, :]).astype(bf16)
    prm = {
        "mu": p["rwkv_mu"].reshape(1, cols), "w0": p["rwkv_w0"].reshape(1, W),
        "w2p": jnp.concatenate([p["rwkv_w2"], zw], axis=0),
        "a0": p["rwkv_a0"].reshape(1, W),
        "a2p": jnp.concatenate([jnp.zeros((RWKV_W_RANK, W), f32), p["rwkv_a2"]], axis=0),
        "g2": p["rwkv_g2"], "k_k": p["rwkv_k_k"].reshape(1, W), "k_a": p["rwkv_k_a"].reshape(1, W),
        "ind": ind, "indt": ind.T,
    }
    if v_first is not None:
        padc = LANES - RWKV_V_RANK
        prm["v0"] = p["rwkv_v0"].reshape(1, W)
        prm["v1p"] = jnp.pad(p["rwkv_v1"], ((0, 0), (0, padc)))
        prm["v2p"] = jnp.pad(p["rwkv_v2"], ((0, padc), (0, 0)))
    r, lw, kh, v, kn, kb, g = _rwkv_prep(feats, prm, v_first, B, S, W, _tile(S, cfg["tm_prep"]))
    if v_first is None:
        v_first = v
    r2, y0, mmat, g0 = _wkv_intra(r, lw, kh, v, kn, kb, min(cfg["nc"], S // CHUNK), cfg["passes_intra"])
    o_rwkv = _wkv_state(r2, y0, mmat, g0, r, kh, v, g, p["rwkv_lnx_w"].reshape(1, W),
                        p["rwkv_lnx_b"].reshape(1, W), p["rwkv_r_k"].reshape(1, W), B, S,
                        min(cfg["pg"], W // LANES), min(cfg["cb"], S // CHUNK), cfg["passes_state"])
    del heads

    w_out_bf = p["w_out"].astype(bf16)
    x1, h2, logits_t = _postmix(o_att, o_rwkv, w_out_bf[:D // 2], w_out_bf[D // 2:], x2, mod3,
                                p["g_post_mix"], p["g_pre_ffn"], p["w_router"].T, S, _tile(S, cfg["tm_post"]))

    eidx_t, gate_t, rank_t, counts = _router(logits_t, p["router_bias"], _tile(T, cfg["tm_router"]))
    blk = cfg["blk"]
    counts = counts[:, 0]
    padded = (counts + blk - 1) // blk * blk
    pad_end = jnp.cumsum(padded)
    pad_start = pad_end - padded
    e_ids = jnp.arange(N_EXPERTS, dtype=i32)
    start_of = jnp.sum(jnp.where(eidx_t[:, :, None] == e_ids, pad_start, 0), axis=-1)
    dest = (start_of + rank_t).T
    n_assign = T * TOP_K
    nblk = -(-n_assign // blk) + N_EXPERTS
    P = nblk * blk
    blk_start = jnp.arange(nblk, dtype=i32) * blk
    nused = (pad_end[-1] // blk).astype(i32).reshape(1)
    blk_pos = jnp.minimum(blk_start, pad_end[-1] - blk)
    blk_e = jnp.minimum(jnp.sum((pad_end[None, :] <= blk_pos[:, None]).astype(i32), axis=1), N_EXPERTS - 1)
    cand = jnp.where(counts > 0, e_ids, N_EXPERTS)
    later = jnp.where(e_ids[None, :] > blk_e[:, None], cand[None, :], N_EXPERTS)
    nxt_e = jnp.min(later, axis=1)
    nxt_e = jnp.where(nxt_e >= N_EXPERTS, -1, nxt_e).astype(i32)

    tm_d = _tile(T, cfg["tm_disp"])
    xs = _dispatch(counts.astype(i32), pad_start.astype(i32), nused,
                   dest.reshape(T // tm_d, 1, tm_d * TOP_K), h2, P, blk, tm_d)
    ys = _experts(blk_e, nxt_e, nused, xs, wexp[0], wexp[1], wexp[2], i, blk)
    tm_c = _tile(S, cfg["tm_comb"])
    x_out = _combine(dest.reshape(T // tm_c, 1, tm_c * TOP_K), gate_t.T, ys, h2, x1,
                     p["w_sh_gate"].astype(bf16), p["w_sh_up"].astype(bf16), p["w_sh_down"].astype(bf16),
                     mod3, p["g_post_ffn"], S, tm_c)
    return x_out, v_first


_CFG = dict(tm_in=1024, tn_att=1024, tm_in_rwkv=512, tn_rwkv=1664, tq=512, tm_prep=256, nc=8, passes_intra=1, passes_state=3, pg=4, cb=8,
            tm_post=256, tm_router=512, blk=256, tm_disp=128, tm_comb=128)

_LAYER_KEYS = ("g_pre_mix", "g_post_mix", "g_pre_ffn", "g_post_ffn", "w_in", "w_out", "lam_q1", "lam_k1",
               "lam_q2", "lam_k2", "att_subln_g", "rwkv_mu", "rwkv_w0", "rwkv_w2", "rwkv_a0", "rwkv_a2",
               "rwkv_g2", "rwkv_k_k", "rwkv_k_a", "rwkv_r_k", "rwkv_lnx_w", "rwkv_lnx_b", "w_router",
               "router_bias", "w_sh_gate", "w_sh_up", "w_sh_down")


def _forward(x, c, params, cfg):
    B, S, D = x.shape
    L = params["w_in"].shape[0]
    bp = 16
    c_pad = jnp.zeros((bp, D), f32).at[:B].set(c)
    mod = _ada_mod(c_pad, params["w_ada"], params["b_ada"])
    x2 = x.reshape(B * S, D)
    v_first = None
    for i in range(L):
        p = {k: params[k][i] for k in _LAYER_KEYS}
        if i > 0:
            p["rwkv_v0"] = params["rwkv_v0"][i - 1]
            p["rwkv_v1"] = params["rwkv_v1"][i - 1]
            p["rwkv_v2"] = params["rwkv_v2"][i - 1]
        mod3 = mod[i, :B].reshape(B * N_MOD, 1, D)
        wexp = (params["w_exp_gate"], params["w_exp_up"], params["w_exp_down"])
        x2, v_first = _layer(i, x2, mod3, p, wexp, v_first, B, S, cfg)
    return x2.reshape(B, S, D)


def kernel(x, c, w_ada, b_ada, g_pre_mix, g_post_mix, g_pre_ffn, g_post_ffn, w_in, w_out, lam_q1, lam_k1, lam_q2, lam_k2, att_subln_g, rwkv_mu, rwkv_w0, rwkv_w2, rwkv_a0, rwkv_a2, rwkv_g2, rwkv_k_k, rwkv_k_a, rwkv_r_k, rwkv_lnx_w, rwkv_lnx_b, rwkv_v0, rwkv_v1, rwkv_v2, w_router, router_bias, w_exp_gate, w_exp_up, w_exp_down, w_sh_gate, w_sh_up, w_sh_down):
    params = dict(w_ada=w_ada, b_ada=b_ada, g_pre_mix=g_pre_mix, g_post_mix=g_post_mix, g_pre_ffn=g_pre_ffn,
                  g_post_ffn=g_post_ffn, w_in=w_in, w_out=w_out, lam_q1=lam_q1, lam_k1=lam_k1, lam_q2=lam_q2,
                  lam_k2=lam_k2, att_subln_g=att_subln_g, rwkv_mu=rwkv_mu, rwkv_w0=rwkv_w0, rwkv_w2=rwkv_w2,
                  rwkv_a0=rwkv_a0, rwkv_a2=rwkv_a2, rwkv_g2=rwkv_g2, rwkv_k_k=rwkv_k_k, rwkv_k_a=rwkv_k_a,
                  rwkv_r_k=rwkv_r_k, rwkv_lnx_w=rwkv_lnx_w, rwkv_lnx_b=rwkv_lnx_b, rwkv_v0=rwkv_v0,
                  rwkv_v1=rwkv_v1, rwkv_v2=rwkv_v2, w_router=w_router, router_bias=router_bias,
                  w_exp_gate=w_exp_gate, w_exp_up=w_exp_up, w_exp_down=w_exp_down, w_sh_gate=w_sh_gate,
                  w_sh_up=w_sh_up, w_sh_down=w_sh_down)
    return _forward(x, c, params, _CFG)
```

```python
import functools
import math

import jax
import jax.numpy as jnp
from jax import lax
from jax.experimental import pallas as pl
from jax.experimental.pallas import tpu as pltpu

f32 = jnp.float32
bf16 = jnp.bfloat16
i32 = jnp.int32
u32 = jnp.uint32

ATT_QK_DIM = 64
ATT_V_DIM = 128
ALIBI_MAX_BIAS = 8.0
ATT_SUBLN_EPS = 1e-5
RWKV_HEAD = 64
RWKV_W_RANK = 64
RWKV_A_RANK = 64
RWKV_G_RANK = 128
RWKV_V_RANK = 32
RWKV_GN_EPS = 64e-5
N_EXPERTS = 64
N_GROUPS = 8
TOPK_GROUPS = 4
TOP_K = 8
ROUTED_SCALE = 2.5
NORM_EPS = 1e-6
N_MOD = 6

LANES = 128
SUBLANES = 8
CHUNK = 64
VMEM_LIMIT = 56 * 1024 * 1024


def _cparams(sem):
    return pltpu.CompilerParams(dimension_semantics=sem, vmem_limit_bytes=VMEM_LIMIT)


def _dot(a, b):
    return jnp.dot(a, b, preferred_element_type=f32)


def _dot_nt(a, b):
    return lax.dot_general(a, b, (((1,), (1,)), ((), ())), preferred_element_type=f32)


def _split2(x):
    hi = x.astype(bf16)
    lo = (x - hi.astype(f32)).astype(bf16)
    return hi, lo


def _mm(a, b, passes=1):
    if passes == 1:
        return _dot(a.astype(bf16), b.astype(bf16))
    ah, al = _split2(a)
    bh, bl = _split2(b)
    return (_dot(al, bh) + _dot(ah, bl)) + _dot(ah, bh)


def _mm_exact_rhs(a, b_bf16):
    ah, al = _split2(a)
    return _dot(al, b_bf16) + _dot(ah, b_bf16)


def _sigmoid(x):
    return 1.0 / (1.0 + jnp.exp(-x))


def _pack_rows(x):
    half = x.shape[1] // 2
    a = x[:, :half].astype(bf16).astype(f32)
    b = x[:, half:].astype(bf16).astype(f32)
    lo = lax.shift_right_logical(lax.bitcast_convert_type(a, u32), jnp.uint32(16))
    hi = lax.bitcast_convert_type(b, u32) & jnp.uint32(0xFFFF0000)
    return lo | hi


def _unpack_rows(w):
    lo = lax.bitcast_convert_type(lax.shift_left(w, jnp.uint32(16)), f32)
    hi = lax.bitcast_convert_type(w & jnp.uint32(0xFFFF0000), f32)
    return lo, hi


def _ada_kernel(c_ref, w_ref, b_ref, o_ref):
    c = c_ref[...]
    cond = (c * _sigmoid(c)).astype(bf16)
    o_ref[0] = _dot(cond, w_ref[0].astype(bf16)) + b_ref[0]


def _ada_mod(c_pad, w_ada, b_ada, tn=1024):
    L, D, N = w_ada.shape
    bp = c_pad.shape[0]
    return pl.pallas_call(
        _ada_kernel,
        grid=(L, N // tn),
        in_specs=[pl.BlockSpec((bp, D), lambda l, j: (0, 0)),
                  pl.BlockSpec((1, D, tn), lambda l, j: (l, 0, j)),
                  pl.BlockSpec((1, 1, tn), lambda l, j: (l, 0, j))],
        out_specs=pl.BlockSpec((1, bp, tn), lambda l, j: (l, 0, j)),
        out_shape=jax.ShapeDtypeStruct((L, bp, N), f32),
        compiler_params=_cparams(("arbitrary", "arbitrary")),
        name="ada_mod",
    )(c_pad, w_ada, b_ada.reshape(L, 1, N))


def _inproj_kernel(x_ref, sc_ref, sh_ref, g_ref, w_ref, o_ref, h_ref):
    @pl.when(pl.program_id(1) == 0)
    def _():
        x = x_ref[...]
        ms = jnp.mean(x * x, axis=-1, keepdims=True)
        y = x * lax.rsqrt(ms + NORM_EPS) * g_ref[...]
        h_ref[...] = (y * (1.0 + sc_ref[0]) + sh_ref[0]).astype(bf16)

    o_ref[...] = _dot(h_ref[...], w_ref[...]).astype(o_ref.dtype)


def _inproj(x2, mod3, g, w_bf, out_dtype, S, tm, tn, seg_sc, seg_sh):
    T, D = x2.shape
    N = w_bf.shape[1]
    tpb = S // tm
    return pl.pallas_call(
        _inproj_kernel,
        grid=(T // tm, N // tn),
        in_specs=[pl.BlockSpec((tm, D), lambda i, j: (i, 0)),
                  pl.BlockSpec((1, 1, D), lambda i, j: ((i // tpb) * N_MOD + seg_sc, 0, 0)),
                  pl.BlockSpec((1, 1, D), lambda i, j: ((i // tpb) * N_MOD + seg_sh, 0, 0)),
                  pl.BlockSpec((1, D), lambda i, j: (0, 0)),
                  pl.BlockSpec((D, tn), lambda i, j: (0, j))],
        out_specs=pl.BlockSpec((tm, tn), lambda i, j: (i, j)),
        out_shape=jax.ShapeDtypeStruct((T, N), out_dtype),
        scratch_shapes=[pltpu.VMEM((tm, D), bf16)],
        compiler_params=_cparams(("arbitrary", "arbitrary")),
        name="inproj",
    )(x2, mod3, mod3, g.reshape(1, D), w_bf)


def _attn_kernel(q_ref, k_ref, v_ref, slope_ref, lamp_ref, g_ref, o_ref,
                 q2t_ref, vt_ref, m_ref, l_ref, acc_ref, *, tq, lam_init):
    qi = pl.program_id(2)
    scale = ATT_QK_DIM ** -0.5
    slope = slope_ref[0][:, 0:1]

    @pl.when(qi == 0)
    def _():
        vt_ref[...] = v_ref[...].astype(f32).T.astype(bf16)

    qt = (q_ref[...].astype(f32) * scale).T
    dim = lax.broadcasted_iota(i32, (LANES, 1), 0)
    first = dim < ATT_QK_DIM
    q2t_ref[:, 0:tq] = jnp.where(first, qt, 0.0).astype(bf16)
    q2t_ref[:, tq:2 * tq] = jnp.where(first, 0.0, qt).astype(bf16)
    m_ref[...] = jnp.full(m_ref.shape, -jnp.inf, f32)
    l_ref[...] = jnp.zeros(l_ref.shape, f32)
    acc_ref[...] = jnp.zeros(acc_ref.shape, f32)

    kr = lax.broadcasted_iota(i32, (tq, 2 * tq), 0)
    qc = lax.broadcasted_iota(i32, (tq, 2 * tq), 1)
    causal = jnp.where(qc >= tq, qc - tq, qc) >= kr
    krow = lax.broadcasted_iota(i32, (tq, 1), 0).astype(f32)

    def step(ki, masked):
        start = pl.multiple_of(ki * tq, tq)
        kb = k_ref[pl.ds(start, tq), :]
        vtb = vt_ref[:, pl.ds(start, tq)]
        s = _dot(kb, q2t_ref[...])
        s = s + slope * (krow + (ki * tq).astype(f32))
        if masked:
            s = jnp.where(causal, s, -jnp.inf)
        m_prev = m_ref[...]
        m_new = jnp.maximum(m_prev, jnp.max(s, axis=0, keepdims=True))
        alpha = jnp.exp(m_prev - m_new)
        p = jnp.exp(s - m_new)
        l_ref[...] = alpha * l_ref[...] + jnp.sum(p, axis=0, keepdims=True)
        acc_ref[...] = alpha * acc_ref[...] + _dot(vtb, p.astype(bf16))
        m_ref[...] = m_new

    def body(ki, carry):
        step(ki, False)
        return carry

    lax.fori_loop(0, qi, body, 0)
    step(qi, True)

    lp = lamp_ref[...]
    lam = (jnp.exp(jnp.sum(lp[0:1] * lp[1:2], axis=-1, keepdims=True))
           - jnp.exp(jnp.sum(lp[2:3] * lp[3:4], axis=-1, keepdims=True)) + lam_init)
    on = acc_ref[...] * (1.0 / l_ref[...])
    o = on[:, 0:tq] - lam * on[:, tq:2 * tq]
    o = o * lax.rsqrt(jnp.mean(o * o, axis=0, keepdims=True) + ATT_SUBLN_EPS)
    o = o * g_ref[...] * (1.0 - lam_init)
    o_ref[...] = o.T.astype(o_ref.dtype)


def _attention(att, slopes, lamp, subln_g, B, S, H, lam_init, tq):
    T = att.shape[0]
    nq = S // tq
    kern = functools.partial(_attn_kernel, tq=tq, lam_init=lam_init)
    return pl.pallas_call(
        kern,
        grid=(B, H, nq),
        in_specs=[pl.BlockSpec((tq, LANES), lambda b, h, q: (b * nq + q, h)),
                  pl.BlockSpec((S, LANES), lambda b, h, q: (b, H + h)),
                  pl.BlockSpec((S, LANES), lambda b, h, q: (b, 2 * H + h)),
                  pl.BlockSpec((1, 1, LANES), lambda b, h, q: (h, 0, 0)),
                  pl.BlockSpec((4, ATT_QK_DIM), lambda b, h, q: (0, 0)),
                  pl.BlockSpec((ATT_V_DIM, 1), lambda b, h, q: (0, 0))],
        out_specs=pl.BlockSpec((tq, LANES), lambda b, h, q: (b * nq + q, h)),
        out_shape=jax.ShapeDtypeStruct((T, H * ATT_V_DIM), bf16),
        scratch_shapes=[pltpu.VMEM((LANES, 2 * tq), bf16),
                        pltpu.VMEM((LANES, S), bf16),
                        pltpu.VMEM((1, 2 * tq), f32),
                        pltpu.VMEM((1, 2 * tq), f32),
                        pltpu.VMEM((LANES, 2 * tq), f32)],
        compiler_params=_cparams(("arbitrary", "arbitrary", "arbitrary")),
        name="diff_attention",
    )(att, att, att, slopes, lamp, subln_g.reshape(ATT_V_DIM, 1))


def _head_sums(x, ind, indt):
    s = _mm_exact_rhs(x, ind)
    return _mm_exact_rhs(s, indt)


def _rwkv_prep_kernel(*refs, W, has_vres):
    if has_vres:
        (f_ref, mu_ref, w0_ref, w2_ref, a0_ref, a2_ref, g2_ref, kk_ref, ka_ref, ind_ref, indt_ref,
         vf_ref, v0_ref, v1_ref, v2_ref,
         r_o, lw_o, kh_o, v_o, kn_o, kb_o, g_o, carry_ref) = refs
    else:
        (f_ref, mu_ref, w0_ref, w2_ref, a0_ref, a2_ref, g2_ref, kk_ref, ka_ref, ind_ref, indt_ref,
         r_o, lw_o, kh_o, v_o, kn_o, kb_o, g_o, carry_ref) = refs

    ti = pl.program_id(1)
    h = f_ref[...]
    tm = h.shape[0]

    @pl.when(ti == 0)
    def _():
        carry_ref[...] = jnp.zeros(carry_ref.shape, f32)

    rolled = pltpu.roll(h, 1, axis=0)
    row = lax.broadcasted_iota(i32, (tm, 1), 0)
    prev = jnp.where(row == 0, carry_ref[...], rolled)
    carry_ref[...] = h[tm - 1:tm, :]
    feats = h + (prev - h) * mu_ref[...]

    r = feats[:, 0:W]
    k = feats[:, W:2 * W]
    v = feats[:, 2 * W:3 * W]
    wa = feats[:, 3 * W:3 * W + LANES]
    g_lo = feats[:, 3 * W + LANES:3 * W + 2 * LANES]

    w = w0_ref[...] + _mm(jnp.tanh(wa), w2_ref[...], passes=3)
    lw_o[...] = -math.exp(-0.5) * _sigmoid(w)
    a = _sigmoid(a0_ref[...] + _mm(wa, a2_ref[...], passes=3))
    g_o[...] = _mm(_sigmoid(g_lo), g2_ref[...])

    if has_vres:
        mix = _sigmoid(v0_ref[...] + _mm(_mm(v, v1_ref[...]), v2_ref[...]))
        v = v + (vf_ref[...] - v) * mix

    kk = k * kk_ref[...]
    ss = _head_sums(kk * kk, ind_ref[...], indt_ref[...])
    kk = kk / jnp.maximum(jnp.sqrt(ss), 1e-12)
    r_o[...] = r
    kh_o[...] = k * (1.0 + (a - 1.0) * ka_ref[...])
    v_o[...] = v
    kn_o[...] = kk
    kb_o[...] = kk * a


def _rwkv_prep(feats, prm, vfirst, B, S, W, tm):
    T, COLS = feats.shape
    tpb = S // tm
    has_vres = vfirst is not None
    row = lambda n: pl.BlockSpec((1, n), lambda b, t: (0, 0))
    full = lambda a: pl.BlockSpec(a.shape, lambda b, t: (0, 0))
    tile = pl.BlockSpec((tm, W), lambda b, t: (b * tpb + t, 0))
    args = [feats, prm["mu"], prm["w0"], prm["w2p"], prm["a0"], prm["a2p"], prm["g2"], prm["k_k"], prm["k_a"],
            prm["ind"], prm["indt"]]
    specs = [pl.BlockSpec((tm, COLS), lambda b, t: (b * tpb + t, 0)), row(COLS), row(W), full(prm["w2p"]),
             row(W), full(prm["a2p"]), full(prm["g2"]), row(W), row(W), full(prm["ind"]), full(prm["indt"])]
    if has_vres:
        args += [vfirst, prm["v0"], prm["v1p"], prm["v2p"]]
        specs += [tile, row(W), full(prm["v1p"]), full(prm["v2p"])]
    out = jax.ShapeDtypeStruct((T, W), f32)
    kern = functools.partial(_rwkv_prep_kernel, W=W, has_vres=has_vres)
    return pl.pallas_call(
        kern,
        grid=(B, tpb),
        in_specs=specs,
        out_specs=[tile] * 7,
        out_shape=[out] * 7,
        scratch_shapes=[pltpu.VMEM((1, COLS), f32)],
        compiler_params=_cparams(("arbitrary", "arbitrary")),
        name="rwkv_prep",
    )(*args)


def _wkv_chunks(rs, lws, ks, vs, kns, kbs, passes):
    C = CHUNK
    P2 = 2 * C
    n = range(len(rs))
    ri = lax.broadcasted_iota(i32, (C, C), 0)
    ci = lax.broadcasted_iota(i32, (C, C), 1)
    tri = (ci <= ri).astype(bf16)
    lane = lax.broadcasted_iota(i32, (1, LANES), 1)
    m0 = (lane < RWKV_HEAD).astype(f32)
    m1 = 1.0 - m0
    rr = lax.broadcasted_iota(i32, (P2, P2), 0)
    cc = lax.broadcasted_iota(i32, (P2, P2), 1)
    same = jnp.where(rr >= C, 1, 0) == jnp.where(cc >= C, 1, 0)
    strict = same & (cc < rr)
    incl = same & (cc <= rr)
    incl2 = jnp.concatenate([incl, incl], axis=1)
    eye = (rr == cc).astype(f32)
    zeros_p = jnp.zeros((P2, LANES), f32)
    zeros_c = jnp.zeros((C, LANES), f32)
    stack = lambda x: jnp.concatenate([x * m0, x * m1], axis=0)
    fold = lambda x: x[0:C] + x[C:2 * C]

    def cumsum(lw):
        h1 = lw.astype(bf16)
        r1 = lw - h1.astype(f32)
        h2 = r1.astype(bf16)
        h3 = (r1 - h2.astype(f32)).astype(bf16)
        return (_dot(tri, h3) + _dot(tri, h2)) + _dot(tri, h1)

    cum = [cumsum(lws[j]) for j in n]
    cum_c = [cum[j][C - 1:C, :] for j in n]
    at = [-kns[j] * jnp.exp(cum[j] - lws[j]) for j in n]
    rt = [rs[j] * jnp.exp(cum[j]) for j in n]
    einv = [jnp.exp(-cum[j]) for j in n]
    bt = [kbs[j] * einv[j] for j in n]
    kt = [ks[j] * einv[j] for j in n]
    eh = [jnp.exp(cum_c[j] - cum[j]) for j in n]
    bh = [kbs[j] * eh[j] for j in n]
    kh = [ks[j] * eh[j] for j in n]
    w_c = [jnp.exp(cum_c[j]) for j in n]
    abd = [stack(at[j]) for j in n]
    vst = [stack(vs[j]) for j in n]
    lhs = [jnp.concatenate([abd[j], stack(rt[j])], axis=0) for j in n]
    rhs = [jnp.concatenate([stack(bt[j]), stack(kt[j])], axis=0) for j in n]
    gram = [_mm_nt(lhs[j], rhs[j], passes) for j in n]
    lab = [jnp.where(strict, gram[j][0:P2, 0:P2], 0.0) for j in n]
    lak = [jnp.where(strict, gram[j][0:P2, P2:2 * P2], 0.0) for j in n]
    mrbk = [jnp.where(incl2, gram[j][P2:2 * P2, :], 0.0) for j in n]

    x0 = [_mm(lak[j], vst[j], passes) for j in n]
    tinv = [eye + lab[j] for j in n]
    lp = lab
    for _ in range(int(math.log2(C)) - 1):
        lp = [_mm(lp[j], lp[j], passes) for j in n]
        tinv = [tinv[j] + _mm(lp[j], tinv[j], passes) for j in n]

    ta = [_mm(tinv[j], jnp.concatenate([abd[j], x0[j]], axis=1), passes) for j in n]
    rhs2 = [jnp.concatenate([ta[j], jnp.concatenate([zeros_p, vst[j]], axis=1)], axis=0) for j in n]
    z = [_mm(mrbk[j], rhs2[j], passes) for j in n]
    r2 = [rt[j] + fold(z[j][:, 0:LANES]) for j in n]
    y0 = [fold(z[j][:, LANES:2 * LANES]) for j in n]
    lhs3t = [jnp.concatenate([bh[j], kh[j]], axis=0).T for j in n]
    rhs3 = [jnp.concatenate([fold(ta[j]), jnp.concatenate([zeros_c, vs[j]], axis=1)], axis=0) for j in n]
    wmat = [_mm(lhs3t[j], rhs3[j], passes) for j in n]
    mmat = [jnp.where(same, wmat[j][:, 0:LANES], 0.0) + eye * w_c[j] for j in n]
    g0 = [jnp.where(same, wmat[j][:, LANES:2 * LANES], 0.0) for j in n]
    return r2, y0, mmat, g0


def _mm_nt(a, b, passes):
    if passes == 1:
        return _dot_nt(a.astype(bf16), b.astype(bf16))
    ah, al = _split2(a)
    bh, bl = _split2(b)
    return (_dot_nt(al, bh) + _dot_nt(ah, bl)) + _dot_nt(ah, bh)


def _wkv_intra_kernel(r_ref, lw_ref, k_ref, v_ref, kn_ref, kb_ref, r2_o, y0_o, m_o, g_o, *, nc, passes):
    C = CHUNK
    sls = [slice(c * C, (c + 1) * C) for c in range(nc)]
    take = lambda ref: [ref[sl, :] for sl in sls]
    r2, y0, mmat, g0 = _wkv_chunks(take(r_ref), take(lw_ref), take(k_ref), take(v_ref),
                                   take(kn_ref), take(kb_ref), passes)
    for c, sl in enumerate(sls):
        r2_o[sl, :] = r2[c]
        y0_o[sl, :] = y0[c]
        m_o[0, c] = mmat[c]
        g_o[0, c] = g0[c]


def _wkv_intra(r, lw, kh, v, kn, kb, nc, passes):
    T, W = r.shape
    npair = W // LANES
    rows = nc * CHUNK
    tile = pl.BlockSpec((rows, LANES), lambda p, i: (i, p))
    mat = pl.BlockSpec((1, nc, LANES, LANES), lambda p, i: (p, i, 0, 0))
    kern = functools.partial(_wkv_intra_kernel, nc=nc, passes=passes)
    return pl.pallas_call(
        kern,
        grid=(npair, T // rows),
        in_specs=[tile] * 6,
        out_specs=[tile, tile, mat, mat],
        out_shape=[jax.ShapeDtypeStruct((T, W), f32), jax.ShapeDtypeStruct((T, W), f32),
                   jax.ShapeDtypeStruct((npair, T // CHUNK, LANES, LANES), f32),
                   jax.ShapeDtypeStruct((npair, T // CHUNK, LANES, LANES), f32)],
        compiler_params=_cparams(("arbitrary", "arbitrary")),
        name="wkv_intra",
    )(r, lw, kh, v, kn, kb)


def _wkv_state_kernel(r2_ref, y0_ref, m_ref, g0_ref, r_ref, kh_ref, v_ref, g_ref,
                      lnw_ref, lnb_ref, rk_ref, o_ref, st_ref, y_ref, *, pg, cb, passes):
    C = CHUNK

    @pl.when(pl.program_id(2) == 0)
    def _():
        st_ref[...] = jnp.zeros(st_ref.shape, f32)

    pairs = range(pg)
    lanes = [slice(p * LANES, (p + 1) * LANES) for p in pairs]
    st = [st_ref[p] for p in pairs]
    for c in range(cb):
        rows = slice(c * C, (c + 1) * C)
        for p in pairs:
            y_ref[rows, lanes[p]] = _mm(r2_ref[rows, lanes[p]], st[p], passes) + y0_ref[rows, lanes[p]]
        st = [_mm(m_ref[p, c], st[p], passes) + g0_ref[p, c] for p in pairs]
    for p in pairs:
        st_ref[p] = st[p]

    rr = lax.broadcasted_iota(i32, (LANES, LANES), 0)
    cc = lax.broadcasted_iota(i32, (LANES, LANES), 1)
    ones_bd = (jnp.where(rr >= RWKV_HEAD, 1, 0) == jnp.where(cc >= RWKV_HEAD, 1, 0)).astype(bf16)
    for p in pairs:
        y = y_ref[:, lanes[p]]
        mu = _mm_exact_rhs(y, ones_bd) * (1.0 / RWKV_HEAD)
        d = y - mu
        var = _mm_exact_rhs(d * d, ones_bd) * (1.0 / RWKV_HEAD)
        yn = d * lax.rsqrt(var + RWKV_GN_EPS) * lnw_ref[:, lanes[p]] + lnb_ref[:, lanes[p]]
        rk = r_ref[:, lanes[p]] * kh_ref[:, lanes[p]] * rk_ref[:, lanes[p]]
        bonus = _mm_exact_rhs(rk, ones_bd) * v_ref[:, lanes[p]]
        o_ref[:, lanes[p]] = ((yn + bonus) * g_ref[:, lanes[p]]).astype(o_ref.dtype)


def _wkv_state(r2, y0, mm, g0, r, kh, v, g, lnw, lnb, rk, B, S, pg, cb, passes):
    T, W = r.shape
    npair = W // LANES
    rows = cb * CHUNK
    steps = S // rows
    seq = pl.BlockSpec((rows, pg * LANES), lambda b, q, c: (b * steps + c, q))
    mat = pl.BlockSpec((pg, cb, LANES, LANES), lambda b, q, c: (q, b * steps + c, 0, 0))
    prow = pl.BlockSpec((1, pg * LANES), lambda b, q, c: (0, q))
    kern = functools.partial(_wkv_state_kernel, pg=pg, cb=cb, passes=passes)
    return pl.pallas_call(
        kern,
        grid=(B, npair // pg, steps),
        in_specs=[seq, seq, mat, mat, seq, seq, seq, seq, prow, prow, prow],
        out_specs=seq,
        out_shape=jax.ShapeDtypeStruct((T, W), bf16),
        scratch_shapes=[pltpu.VMEM((pg, LANES, LANES), f32), pltpu.VMEM((rows, pg * LANES), f32)],
        compiler_params=_cparams(("arbitrary", "arbitrary", "arbitrary")),
        name="wkv_state",
    )(r2, y0, mm, g0, r, kh, v, g, lnw, lnb, rk)


def _postmix_kernel(oa_ref, orw_ref, wa_ref, wr_ref, x_ref, ga_ref, gpost_ref, gpre_ref, sc_ref, sh_ref, wrt_ref,
                    x1_o, h2_o, lg_o):
    mixed = _dot(oa_ref[...], wa_ref[...]) + _dot(orw_ref[...], wr_ref[...])
    ms = jnp.mean(mixed * mixed, axis=-1, keepdims=True)
    x1 = x_ref[...] + ga_ref[0] * (mixed * lax.rsqrt(ms + NORM_EPS) * gpost_ref[...])
    x1_o[...] = x1
    ms1 = jnp.mean(x1 * x1, axis=-1, keepdims=True)
    h2 = (x1 * lax.rsqrt(ms1 + NORM_EPS) * gpre_ref[...]) * (1.0 + sc_ref[0]) + sh_ref[0]
    h2_o[...] = _pack_rows(h2)
    lg_o[...] = _mm_nt(wrt_ref[...], h2, 3)


def _postmix(o_att, o_rwkv, w_out_a, w_out_r, x2, mod3, g_post, g_pre, w_rt, S, tm):
    T, D = x2.shape
    WA = o_att.shape[1]
    WR = o_rwkv.shape[1]
    E = w_rt.shape[0]
    tpb = S // tm
    modspec = lambda seg: pl.BlockSpec((1, 1, D), lambda i: ((i // tpb) * N_MOD + seg, 0, 0))
    tile = pl.BlockSpec((tm, D), lambda i: (i, 0))
    return pl.pallas_call(
        _postmix_kernel,
        grid=(T // tm,),
        in_specs=[pl.BlockSpec((tm, WA), lambda i: (i, 0)),
                  pl.BlockSpec((tm, WR), lambda i: (i, 0)),
                  pl.BlockSpec((WA, D), lambda i: (0, 0), pipeline_mode=pl.Buffered(1)),
                  pl.BlockSpec((WR, D), lambda i: (0, 0), pipeline_mode=pl.Buffered(1)),
                  tile, modspec(2),
                  pl.BlockSpec((1, D), lambda i: (0, 0)),
                  pl.BlockSpec((1, D), lambda i: (0, 0)),
                  modspec(4), modspec(3),
                  pl.BlockSpec((E, D), lambda i: (0, 0))],
        out_specs=[tile, pl.BlockSpec((tm, D // 2), lambda i: (i, 0)),
                   pl.BlockSpec((E, tm), lambda i: (0, i))],
        out_shape=[jax.ShapeDtypeStruct((T, D), f32), jax.ShapeDtypeStruct((T, D // 2), u32),
                   jax.ShapeDtypeStruct((E, T), f32)],
        compiler_params=_cparams(("arbitrary",)),
        name="postmix",
    )(o_att, o_rwkv, w_out_a, w_out_r, x2, mod3, g_post.reshape(1, D), g_pre.reshape(1, D), mod3, mod3, w_rt)


def _first_max(x, iota, n):
    mx = jnp.max(x, axis=0, keepdims=True)
    idx = jnp.min(jnp.where(x == mx, iota, n), axis=0, keepdims=True)
    return mx, idx


def _router_kernel(lg_ref, bias_ref, eidx_o, gate_o, rank_o, cnt_o, cnt_ref):
    E = N_EXPERTS
    G = N_GROUPS
    per = E // G
    tm = lg_ref.shape[1]

    @pl.when(pl.program_id(0) == 0)
    def _():
        cnt_ref[...] = jnp.zeros(cnt_ref.shape, f32)

    scores = _sigmoid(lg_ref[...])
    biased = scores + bias_ref[...]
    neg = -jnp.inf

    iota_p = lax.broadcasted_iota(i32, (per, tm), 0).astype(f32)
    gs = []
    for g in range(G):
        xg = biased[g * per:(g + 1) * per, :]
        m1, i1 = _first_max(xg, iota_p, per)
        m2 = jnp.max(jnp.where(iota_p == i1, neg, xg), axis=0, keepdims=True)
        gs.append(m1 + m2)
    gsc = jnp.concatenate(gs, axis=0)
    iota_g = lax.broadcasted_iota(i32, (G, tm), 0).astype(f32)
    gsel = jnp.zeros((G, tm), f32)
    for _ in range(TOPK_GROUPS):
        _, gi = _first_max(gsc, iota_g, G)
        hit = iota_g == gi
        gsel = jnp.where(hit, 1.0, gsel)
        gsc = jnp.where(hit, neg, gsc)
    masked = jnp.concatenate(
        [jnp.where(gsel[g:g + 1, :] > 0.0, biased[g * per:(g + 1) * per, :], neg) for g in range(G)], axis=0)

    iota_e = lax.broadcasted_iota(i32, (E, tm), 0).astype(f32)
    sel = jnp.zeros((E, tm), f32)
    idxs, vals = [], []
    for _ in range(TOP_K):
        _, ei = _first_max(masked, iota_e, E)
        hit = iota_e == ei
        idxs.append(ei)
        vals.append(jnp.sum(jnp.where(hit, scores, 0.0), axis=0, keepdims=True))
        sel = jnp.where(hit, 1.0, sel)
        masked = jnp.where(hit, neg, masked)
    tot = vals[0]
    for vv in vals[1:]:
        tot = tot + vv
    eidx_o[...] = jnp.concatenate(idxs, axis=0).astype(i32)
    gate_o[...] = jnp.concatenate([vv / tot * ROUTED_SCALE for vv in vals], axis=0)

    rr = lax.broadcasted_iota(i32, (tm, tm), 0)
    cc = lax.broadcasted_iota(i32, (tm, tm), 1)
    before = (rr < cc).astype(bf16)
    pos = _dot(sel.astype(bf16), before) + cnt_ref[...]
    rank_o[...] = jnp.concatenate(
        [jnp.sum(jnp.where(iota_e == ei, pos, 0.0), axis=0, keepdims=True) for ei in idxs], axis=0).astype(i32)
    cnt_ref[...] = cnt_ref[...] + jnp.sum(sel, axis=1, keepdims=True)
    cnt_o[...] = cnt_ref[...].astype(i32)


def _router(logits_t, bias, tm):
    E, T = logits_t.shape
    k_tile = pl.BlockSpec((TOP_K, tm), lambda i: (0, i))
    return pl.pallas_call(
        _router_kernel,
        grid=(T // tm,),
        in_specs=[pl.BlockSpec((E, tm), lambda i: (0, i)),
                  pl.BlockSpec((E, 1), lambda i: (0, 0))],
        out_specs=[k_tile, k_tile, k_tile, pl.BlockSpec((E, 1), lambda i: (0, 0))],
        out_shape=[jax.ShapeDtypeStruct((TOP_K, T), i32), jax.ShapeDtypeStruct((TOP_K, T), f32),
                   jax.ShapeDtypeStruct((TOP_K, T), i32), jax.ShapeDtypeStruct((E, 1), i32)],
        scratch_shapes=[pltpu.VMEM((E, 1), f32)],
        compiler_params=_cparams(("arbitrary",)),
        name="router",
    )(logits_t, bias.reshape(E, 1))


def _row_copy(src_ref, s, dst_ref, d, sem):
    return pltpu.make_async_copy(src_ref.at[pl.ds(s, 1), :], dst_ref.at[pl.ds(d, 1), :], sem)


def _zero_fill(cnt_ref, pstart_ref, nused_ref, z_ref, xs_out, sem, blk, nblk, start):
    def act(cp):
        if start:
            cp.start()
        else:
            cp.wait()

    def per_expert(e, carry):
        c = cnt_ref[e]
        base = pstart_ref[e] + c
        npad = (blk - (c & (blk - 1))) & (blk - 1)
        head = (-base) & (SUBLANES - 1)

        def one_row(j, carry2):
            act(_row_copy(z_ref, 0, xs_out, base + j, sem))
            return carry2

        lax.fori_loop(0, head, one_row, 0)
        rem = npad - head
        aligned = base + head
        p = blk // 2
        while p >= SUBLANES:
            off = pl.multiple_of(aligned + (rem & ~(2 * p - 1)), SUBLANES)

            @pl.when((rem & p) != 0)
            def _(p=p, off=off):
                act(pltpu.make_async_copy(z_ref.at[pl.ds(0, p), :], xs_out.at[pl.ds(off, p), :], sem))

            p //= 2
        return carry

    lax.fori_loop(0, N_EXPERTS, per_expert, 0)

    def per_block(b, carry):
        act(pltpu.make_async_copy(z_ref, xs_out.at[pl.ds(pl.multiple_of(b * blk, blk), blk), :], sem))
        return carry

    lax.fori_loop(nused_ref[0], nblk, per_block, 0)


def _dispatch_kernel(cnt_ref, pstart_ref, nused_ref, dest_hbm, h_ref, wsg_ref, wsu_ref, wsd_ref, xs_out, ysh_o,
                     idx_ref, z_ref, sem_idx, sem, sem_z, *, blk, nblk):
    i = pl.program_id(0)
    tm = h_ref.shape[0]
    cp = pltpu.make_async_copy(dest_hbm.at[i], idx_ref, sem_idx)
    cp.start()
    cp.wait()

    def issue(r, carry):
        for k in range(TOP_K):
            _row_copy(h_ref, r, xs_out, idx_ref[0, r * TOP_K + k], sem).start(priority=k % 2)
        return carry

    lax.fori_loop(0, tm, issue, 0)

    hlo, hhi = _unpack_rows(h_ref[...])
    hb = jnp.concatenate([hlo.astype(bf16), hhi.astype(bf16)], axis=1)
    gt = _dot(hb, wsg_ref[...])
    up = _dot(hb, wsu_ref[...])
    ysh_o[...] = _dot(((gt * _sigmoid(gt)) * up).astype(bf16), wsd_ref[...]).astype(ysh_o.dtype)

    @pl.when(i == pl.num_programs(0) - 1)
    def _():
        z_ref[...] = jnp.zeros(z_ref.shape, u32)
        _zero_fill(cnt_ref, pstart_ref, nused_ref, z_ref, xs_out, sem_z, blk, nblk, True)
        _zero_fill(cnt_ref, pstart_ref, nused_ref, z_ref, xs_out, sem_z, blk, nblk, False)

    for _ in range(TOP_K):
        pltpu.make_async_copy(h_ref, xs_out.at[pl.ds(0, tm), :], sem).wait()


def _dispatch(counts, pad_start, nused, dest_tiles, h2p, wsg, wsu, wsd, P, blk, tm):
    T, DW = h2p.shape
    D, DS = wsg.shape
    assert blk & (blk - 1) == 0
    kern = functools.partial(_dispatch_kernel, blk=blk, nblk=P // blk)
    const = lambda shape: pl.BlockSpec(shape, lambda i, c, s, n: (0, 0))
    grid_spec = pltpu.PrefetchScalarGridSpec(
        num_scalar_prefetch=3,
        grid=(T // tm,),
        in_specs=[pl.BlockSpec(memory_space=pl.ANY),
                  pl.BlockSpec((tm, DW), lambda i, c, s, n: (i, 0)),
                  const((D, DS)), const((D, DS)), const((DS, D))],
        out_specs=[pl.BlockSpec(memory_space=pl.ANY), pl.BlockSpec((tm, D), lambda i, c, s, n: (i, 0))],
        scratch_shapes=[pltpu.SMEM((1, tm * TOP_K), i32), pltpu.VMEM((blk, DW), u32),
                        pltpu.SemaphoreType.DMA, pltpu.SemaphoreType.DMA, pltpu.SemaphoreType.DMA],
    )
    return pl.pallas_call(
        kern,
        grid_spec=grid_spec,
        out_shape=[jax.ShapeDtypeStruct((P, DW), u32), jax.ShapeDtypeStruct((T, D), bf16)],
        compiler_params=_cparams(("arbitrary",)),
        name="moe_dispatch",
    )(counts, pad_start, nused, dest_tiles, h2p, wsg, wsu, wsd)


def _experts_kernel(blk_e_ref, nxt_e_ref, nused_ref, xs_ref, wg_hbm, wu_hbm, wd_hbm, ys_ref,
                    wg_f, wu_f, wd_f, wg_s, wu_s, wd_s, sems, *, layer):
    i = pl.program_id(0)
    e = blk_e_ref[i]
    changed = jnp.logical_or(i == 0, e != blk_e_ref[jnp.maximum(i - 1, 0)])

    def weight_copies(ex):
        return (pltpu.make_async_copy(wg_hbm.at[layer, ex], wg_f, sems.at[0]),
                pltpu.make_async_copy(wu_hbm.at[layer, ex], wu_f, sems.at[1]),
                pltpu.make_async_copy(wd_hbm.at[layer, ex], wd_f, sems.at[2]))

    @pl.when(i == 0)
    def _():
        for cp in weight_copies(e):
            cp.start()

    @pl.when(changed)
    def _():
        for cp in weight_copies(e):
            cp.wait()
        for src, dst in ((wg_f, wg_s), (wu_f, wu_s), (wd_f, wd_s)):
            rows = src.shape[0] // 8
            for c in range(8):
                dst[c * rows:(c + 1) * rows, :] = src[c * rows:(c + 1) * rows, :].astype(bf16)
        nxt = nxt_e_ref[i]

        @pl.when(nxt >= 0)
        def _():
            for cp in weight_copies(nxt):
                cp.start()

    @pl.when(i < nused_ref[0])
    def _():
        lo, hi = _unpack_rows(xs_ref[...])
        x = jnp.concatenate([lo.astype(bf16), hi.astype(bf16)], axis=1)
        gt = _dot(x, wg_s[...])
        up = _dot(x, wu_s[...])
        hmid = (gt * _sigmoid(gt)) * up
        ys_ref[...] = _pack_rows(_dot(hmid.astype(bf16), wd_s[...]))

    @pl.when(i >= nused_ref[0])
    def _():
        ys_ref[...] = jnp.zeros(ys_ref.shape, u32)


def _experts(blk_e, nxt_e, nused, xs, w_gate, w_up, w_down, layer, blk):
    P, DW = xs.shape
    D, DE = w_gate.shape[-2:]
    nblk = P // blk
    row_idx = lambda i, be, nx, nu: (jnp.minimum(i, nu[0] - 1), 0)
    hbm = pl.BlockSpec(memory_space=pl.ANY)
    grid_spec = pltpu.PrefetchScalarGridSpec(
        num_scalar_prefetch=3,
        grid=(nblk,),
        in_specs=[pl.BlockSpec((blk, DW), row_idx), hbm, hbm, hbm],
        out_specs=pl.BlockSpec((blk, DW), lambda i, be, nx, nu: (i, 0)),
        scratch_shapes=[pltpu.VMEM((D, DE), f32), pltpu.VMEM((D, DE), f32), pltpu.VMEM((DE, D), f32),
                        pltpu.VMEM((D, DE), bf16), pltpu.VMEM((D, DE), bf16), pltpu.VMEM((DE, D), bf16),
                        pltpu.SemaphoreType.DMA((3,))],
    )
    return pl.pallas_call(
        functools.partial(_experts_kernel, layer=layer),
        grid_spec=grid_spec,
        out_shape=jax.ShapeDtypeStruct((P, DW), u32),
        compiler_params=_cparams(("arbitrary",)),
        name="moe_experts",
    )(blk_e, nxt_e, nused, xs, w_gate, w_up, w_down)


def _combine_kernel(dest_hbm, gate_ref, ys_hbm, ysh_ref, x1_ref, gf_ref, gpost_ref,
                    x2_o, idx_ref, buf_ref, sem_idx, sem):
    i = pl.program_id(0)
    tm = x1_ref.shape[0]
    cp = pltpu.make_async_copy(dest_hbm.at[i], idx_ref, sem_idx)
    cp.start()
    cp.wait()

    def issue(r, carry):
        for k in range(TOP_K):
            _row_copy(ys_hbm, idx_ref[0, r * TOP_K + k], buf_ref.at[k], r, sem).start(priority=k % 2)
        return carry

    lax.fori_loop(0, tm, issue, 0)

    for k in range(TOP_K):
        pltpu.make_async_copy(ys_hbm.at[pl.ds(0, tm), :], buf_ref.at[k], sem).wait()

    gate = gate_ref[...]
    ysh = ysh_ref[...].astype(f32)
    half = ysh.shape[1] // 2
    lo = ysh[:, :half]
    hi = ysh[:, half:]
    for k in range(TOP_K):
        a, b = _unpack_rows(buf_ref[k])
        gk = gate[:, k:k + 1]
        lo = lo + gk * a
        hi = hi + gk * b
    y = jnp.concatenate([lo, hi], axis=1)
    ms = jnp.mean(y * y, axis=-1, keepdims=True)
    x2_o[...] = x1_ref[...] + gf_ref[0] * (y * lax.rsqrt(ms + NORM_EPS) * gpost_ref[...])


def _combine(dest_tiles, gate_tk, ys, ysh, x1, mod3, g_post, S, tm):
    T, D = x1.shape
    DW = ys.shape[1]
    tpb = S // tm
    tile = pl.BlockSpec((tm, D), lambda i: (i, 0))
    return pl.pallas_call(
        _combine_kernel,
        grid=(T // tm,),
        in_specs=[pl.BlockSpec(memory_space=pl.ANY),
                  pl.BlockSpec((tm, TOP_K), lambda i: (i, 0)),
                  pl.BlockSpec(memory_space=pl.ANY),
                  tile, tile,
                  pl.BlockSpec((1, 1, D), lambda i: ((i // tpb) * N_MOD + 5, 0, 0)),
                  pl.BlockSpec((1, D), lambda i: (0, 0))],
        out_specs=tile,
        out_shape=jax.ShapeDtypeStruct((T, D), f32),
        scratch_shapes=[pltpu.SMEM((1, tm * TOP_K), i32), pltpu.VMEM((TOP_K, tm, DW), u32),
                        pltpu.SemaphoreType.DMA, pltpu.SemaphoreType.DMA],
        compiler_params=_cparams(("arbitrary",)),
        name="moe_combine",
    )(dest_tiles, gate_tk, ys, ysh, x1, mod3, g_post.reshape(1, D))


def _tile(n, pref):
    t = min(n, pref)
    assert n % t == 0, (n, t)
    return t


def _layer(i, x2, mod3, p, wexp, v_first, B, S, cfg):
    T, D = x2.shape
    H = (D // 2) // ATT_V_DIM
    W = D - D // 2
    att_cols = 2 * H * 2 * ATT_QK_DIM + H * ATT_V_DIM
    lam_init = 0.8 - 0.6 * math.exp(-0.3 * i)

    w_in_bf = p["w_in"].astype(bf16)
    att = _inproj(x2, mod3, p["g_pre_mix"], w_in_bf[:, :att_cols], bf16, S,
                  _tile(S, cfg["tm_in"]), cfg["tn_att"], 1, 0)
    feats = _inproj(x2, mod3, p["g_pre_mix"], w_in_bf[:, att_cols:], f32, S,
                    _tile(S, cfg["tm_in_rwkv"]), cfg["tn_rwkv"], 1, 0)

    slopes = jnp.broadcast_to(
        (2.0 ** (-ALIBI_MAX_BIAS * jnp.arange(1, H + 1, dtype=f32) / H))[:, None, None], (H, 1, LANES))
    lamp = jnp.stack([p["lam_q1"], p["lam_k1"], p["lam_q2"], p["lam_k2"]])
    o_att = _attention(att, slopes, lamp, p["att_subln_g"], B, S, H, lam_init, _tile(S, cfg["tq"]))

    cols = feats.shape[1]
    zw = jnp.zeros((RWKV_A_RANK, W), f32)
    heads = W // RWKV_HEAD
    ind = (jnp.arange(W)[:, None] // RWKV_HEAD == jnp.arange(LANES)[None, :]).astype(bf16)
    prm = {
        "mu": p["rwkv_mu"].reshape(1, cols), "w0": p["rwkv_w0"].reshape(1, W),
        "w2p": jnp.concatenate([p["rwkv_w2"], zw], axis=0),
        "a0": p["rwkv_a0"].reshape(1, W),
        "a2p": jnp.concatenate([jnp.zeros((RWKV_W_RANK, W), f32), p["rwkv_a2"]], axis=0),
        "g2": p["rwkv_g2"], "k_k": p["rwkv_k_k"].reshape(1, W), "k_a": p["rwkv_k_a"].reshape(1, W),
        "ind": ind, "indt": ind.T,
    }
    if v_first is not None:
        padc = LANES - RWKV_V_RANK
        prm["v0"] = p["rwkv_v0"].reshape(1, W)
        prm["v1p"] = jnp.pad(p["rwkv_v1"], ((0, 0), (0, padc)))
        prm["v2p"] = jnp.pad(p["rwkv_v2"], ((0, padc), (0, 0)))
    r, lw, kh, v, kn, kb, g = _rwkv_prep(feats, prm, v_first, B, S, W, _tile(S, cfg["tm_prep"]))
    if v_first is None:
        v_first = v
    r2, y0, mmat, g0 = _wkv_intra(r, lw, kh, v, kn, kb, min(cfg["nc"], S // CHUNK), cfg["passes_intra"])
    o_rwkv = _wkv_state(r2, y0, mmat, g0, r, kh, v, g, p["rwkv_lnx_w"].reshape(1, W),
                        p["rwkv_lnx_b"].reshape(1, W), p["rwkv_r_k"].reshape(1, W), B, S,
                        min(cfg["pg"], W // LANES), min(cfg["cb"], S // CHUNK), cfg["passes_state"])
    del heads

    w_out_bf = p["w_out"].astype(bf16)
    x1, h2, logits_t = _postmix(o_att, o_rwkv, w_out_bf[:D // 2], w_out_bf[D // 2:], x2, mod3,
                                p["g_post_mix"], p["g_pre_ffn"], p["w_router"].T, S, _tile(S, cfg["tm_post"]))

    eidx_t, gate_t, rank_t, counts = _router(logits_t, p["router_bias"], _tile(T, cfg["tm_router"]))
    blk = cfg["blk"]
    counts = counts[:, 0]
    padded = (counts + blk - 1) // blk * blk
    pad_end = jnp.cumsum(padded)
    pad_start = pad_end - padded
    e_ids = jnp.arange(N_EXPERTS, dtype=i32)
    start_of = jnp.sum(jnp.where(eidx_t[:, :, None] == e_ids, pad_start, 0), axis=-1)
    dest = (start_of + rank_t).T
    n_assign = T * TOP_K
    nblk = -(-n_assign // blk) + N_EXPERTS
    P = nblk * blk
    blk_start = jnp.arange(nblk, dtype=i32) * blk
    nused = (pad_end[-1] // blk).astype(i32).reshape(1)
    blk_pos = jnp.minimum(blk_start, pad_end[-1] - blk)
    blk_e = jnp.minimum(jnp.sum((pad_end[None, :] <= blk_pos[:, None]).astype(i32), axis=1), N_EXPERTS - 1)
    cand = jnp.where(counts > 0, e_ids, N_EXPERTS)
    later = jnp.where(e_ids[None, :] > blk_e[:, None], cand[None, :], N_EXPERTS)
    nxt_e = jnp.min(later, axis=1)
    nxt_e = jnp.where(nxt_e >= N_EXPERTS, -1, nxt_e).astype(i32)

    tm_d = _tile(T, cfg["tm_disp"])
    xs, ysh = _dispatch(counts.astype(i32), pad_start.astype(i32), nused,
                        dest.reshape(T // tm_d, 1, tm_d * TOP_K), h2,
                        p["w_sh_gate"].astype(bf16), p["w_sh_up"].astype(bf16), p["w_sh_down"].astype(bf16),
                        P, blk, tm_d)
    ys = _experts(blk_e, nxt_e, nused, xs, wexp[0], wexp[1], wexp[2], i, blk)
    tm_c = _tile(S, cfg["tm_comb"])
    x_out = _combine(dest.reshape(T // tm_c, 1, tm_c * TOP_K), gate_t.T, ys, ysh, x1,
                     mod3, p["g_post_ffn"], S, tm_c)
    return x_out, v_first


_CFG = dict(tm_in=1024, tn_att=1024, tm_in_rwkv=512, tn_rwkv=1664, tq=512, tm_prep=256, nc=8, passes_intra=1, passes_state=1, pg=4, cb=8,
            tm_post=512, tm_router=512, blk=256, tm_disp=256, tm_comb=128)

_LAYER_KEYS = ("g_pre_mix", "g_post_mix", "g_pre_ffn", "g_post_ffn", "w_in", "w_out", "lam_q1", "lam_k1",
               "lam_q2", "lam_k2", "att_subln_g", "rwkv_mu", "rwkv_w0", "rwkv_w2", "rwkv_a0", "rwkv_a2",
               "rwkv_g2", "rwkv_k_k", "rwkv_k_a", "rwkv_r_k", "rwkv_lnx_w", "rwkv_lnx_b", "w_router",
               "router_bias", "w_sh_gate", "w_sh_up", "w_sh_down")


def _forward(x, c, params, cfg):
    B, S, D = x.shape
    L = params["w_in"].shape[0]
    bp = 16
    c_pad = jnp.zeros((bp, D), f32).at[:B].set(c)
    mod = _ada_mod(c_pad, params["w_ada"], params["b_ada"])
    x2 = x.reshape(B * S, D)
    v_first = None
    for i in range(L):
        p = {k: params[k][i] for k in _LAYER_KEYS}
        if i > 0:
            p["rwkv_v0"] = params["rwkv_v0"][i - 1]
            p["rwkv_v1"] = params["rwkv_v1"][i - 1]
            p["rwkv_v2"] = params["rwkv_v2"][i - 1]
        mod3 = mod[i, :B].reshape(B * N_MOD, 1, D)
        wexp = (params["w_exp_gate"], params["w_exp_up"], params["w_exp_down"])
        x2, v_first = _layer(i, x2, mod3, p, wexp, v_first, B, S, cfg)
    return x2.reshape(B, S, D)


def kernel(x, c, w_ada, b_ada, g_pre_mix, g_post_mix, g_pre_ffn, g_post_ffn, w_in, w_out, lam_q1, lam_k1, lam_q2, lam_k2, att_subln_g, rwkv_mu, rwkv_w0, rwkv_w2, rwkv_a0, rwkv_a2, rwkv_g2, rwkv_k_k, rwkv_k_a, rwkv_r_k, rwkv_lnx_w, rwkv_lnx_b, rwkv_v0, rwkv_v1, rwkv_v2, w_router, router_bias, w_exp_gate, w_exp_up, w_exp_down, w_sh_gate, w_sh_up, w_sh_down):
    params = dict(w_ada=w_ada, b_ada=b_ada, g_pre_mix=g_pre_mix, g_post_mix=g_post_mix, g_pre_ffn=g_pre_ffn,
                  g_post_ffn=g_post_ffn, w_in=w_in, w_out=w_out, lam_q1=lam_q1, lam_k1=lam_k1, lam_q2=lam_q2,
                  lam_k2=lam_k2, att_subln_g=att_subln_g, rwkv_mu=rwkv_mu, rwkv_w0=rwkv_w0, rwkv_w2=rwkv_w2,
                  rwkv_a0=rwkv_a0, rwkv_a2=rwkv_a2, rwkv_g2=rwkv_g2, rwkv_k_k=rwkv_k_k, rwkv_k_a=rwkv_k_a,
                  rwkv_r_k=rwkv_r_k, rwkv_lnx_w=rwkv_lnx_w, rwkv_lnx_b=rwkv_lnx_b, rwkv_v0=rwkv_v0,
                  rwkv_v1=rwkv_v1, rwkv_v2=rwkv_v2, w_router=w_router, router_bias=router_bias,
                  w_exp_gate=w_exp_gate, w_exp_up=w_exp_up, w_exp_down=w_exp_down, w_sh_gate=w_sh_gate,
                  w_sh_up=w_sh_up, w_sh_down=w_sh_down)
    return _forward(x, c, params, _CFG)
```

```python
import functools
import math

import jax
import jax.numpy as jnp
from jax import lax
from jax.experimental import pallas as pl
from jax.experimental.pallas import tpu as pltpu

f32 = jnp.float32
bf16 = jnp.bfloat16
i32 = jnp.int32
u32 = jnp.uint32

ATT_QK_DIM = 64
ATT_V_DIM = 128
ALIBI_MAX_BIAS = 8.0
ATT_SUBLN_EPS = 1e-5
RWKV_HEAD = 64
RWKV_W_RANK = 64
RWKV_A_RANK = 64
RWKV_G_RANK = 128
RWKV_V_RANK = 32
RWKV_GN_EPS = 64e-5
N_EXPERTS = 64
N_GROUPS = 8
TOPK_GROUPS = 4
TOP_K = 8
ROUTED_SCALE = 2.5
NORM_EPS = 1e-6
N_MOD = 6

LANES = 128
SUBLANES = 8
CHUNK = 64
VMEM_LIMIT = 56 * 1024 * 1024


def _cparams(sem):
    return pltpu.CompilerParams(dimension_semantics=sem, vmem_limit_bytes=VMEM_LIMIT)


def _dot(a, b):
    return jnp.dot(a, b, preferred_element_type=f32)


def _dot_nt(a, b):
    return lax.dot_general(a, b, (((1,), (1,)), ((), ())), preferred_element_type=f32)


def _split2(x):
    hi = x.astype(bf16)
    lo = (x - hi.astype(f32)).astype(bf16)
    return hi, lo


def _mm(a, b, passes=1):
    if passes == 1:
        return _dot(a.astype(bf16), b.astype(bf16))
    ah, al = _split2(a)
    bh, bl = _split2(b)
    return (_dot(al, bh) + _dot(ah, bl)) + _dot(ah, bh)


def _mm_exact_rhs(a, b_bf16):
    ah, al = _split2(a)
    return _dot(al, b_bf16) + _dot(ah, b_bf16)


def _sigmoid(x):
    return 1.0 / (1.0 + jnp.exp(-x))


def _pack_rows(x):
    half = x.shape[1] // 2
    a = x[:, :half].astype(bf16).astype(f32)
    b = x[:, half:].astype(bf16).astype(f32)
    lo = lax.shift_right_logical(lax.bitcast_convert_type(a, u32), jnp.uint32(16))
    hi = lax.bitcast_convert_type(b, u32) & jnp.uint32(0xFFFF0000)
    return lo | hi


def _unpack_rows(w):
    lo = lax.bitcast_convert_type(lax.shift_left(w, jnp.uint32(16)), f32)
    hi = lax.bitcast_convert_type(w & jnp.uint32(0xFFFF0000), f32)
    return lo, hi


def _ada_kernel(c_ref, w_ref, b_ref, o_ref):
    c = c_ref[...]
    cond = (c * _sigmoid(c)).astype(bf16)
    o_ref[0] = _dot(cond, w_ref[0].astype(bf16)) + b_ref[0]


def _ada_mod(c_pad, w_ada, b_ada, tn=1024):
    L, D, N = w_ada.shape
    bp = c_pad.shape[0]
    return pl.pallas_call(
        _ada_kernel,
        grid=(L, N // tn),
        in_specs=[pl.BlockSpec((bp, D), lambda l, j: (0, 0)),
                  pl.BlockSpec((1, D, tn), lambda l, j: (l, 0, j)),
                  pl.BlockSpec((1, 1, tn), lambda l, j: (l, 0, j))],
        out_specs=pl.BlockSpec((1, bp, tn), lambda l, j: (l, 0, j)),
        out_shape=jax.ShapeDtypeStruct((L, bp, N), f32),
        compiler_params=_cparams(("arbitrary", "arbitrary")),
        name="ada_mod",
    )(c_pad, w_ada, b_ada.reshape(L, 1, N))


def _inproj_kernel(x_ref, sc_ref, sh_ref, g_ref, w_ref, o_ref, h_ref):
    @pl.when(pl.program_id(1) == 0)
    def _():
        x = x_ref[...]
        ms = jnp.mean(x * x, axis=-1, keepdims=True)
        y = x * lax.rsqrt(ms + NORM_EPS) * g_ref[...]
        h_ref[...] = (y * (1.0 + sc_ref[0]) + sh_ref[0]).astype(bf16)

    o_ref[...] = _dot(h_ref[...], w_ref[...]).astype(o_ref.dtype)


def _inproj(x2, mod3, g, w_bf, out_dtype, S, tm, tn, seg_sc, seg_sh):
    T, D = x2.shape
    N = w_bf.shape[1]
    tpb = S // tm
    return pl.pallas_call(
        _inproj_kernel,
        grid=(T // tm, N // tn),
        in_specs=[pl.BlockSpec((tm, D), lambda i, j: (i, 0)),
                  pl.BlockSpec((1, 1, D), lambda i, j: ((i // tpb) * N_MOD + seg_sc, 0, 0)),
                  pl.BlockSpec((1, 1, D), lambda i, j: ((i // tpb) * N_MOD + seg_sh, 0, 0)),
                  pl.BlockSpec((1, D), lambda i, j: (0, 0)),
                  pl.BlockSpec((D, tn), lambda i, j: (0, j))],
        out_specs=pl.BlockSpec((tm, tn), lambda i, j: (i, j)),
        out_shape=jax.ShapeDtypeStruct((T, N), out_dtype),
        scratch_shapes=[pltpu.VMEM((tm, D), bf16)],
        compiler_params=_cparams(("arbitrary", "arbitrary")),
        name="inproj",
    )(x2, mod3, mod3, g.reshape(1, D), w_bf)


def _attn_kernel(q_ref, k_ref, v_ref, slope_ref, lamp_ref, g_ref, o_ref,
                 q2t_ref, vt_ref, m_ref, l_ref, acc_ref, *, tq, lam_init):
    qi = pl.program_id(2)
    scale = ATT_QK_DIM ** -0.5
    slope = slope_ref[0][:, 0:1]

    @pl.when(qi == 0)
    def _():
        vt_ref[...] = v_ref[...].astype(f32).T.astype(bf16)

    qt = (q_ref[...].astype(f32) * scale).T
    dim = lax.broadcasted_iota(i32, (LANES, 1), 0)
    first = dim < ATT_QK_DIM
    q2t_ref[:, 0:tq] = jnp.where(first, qt, 0.0).astype(bf16)
    q2t_ref[:, tq:2 * tq] = jnp.where(first, 0.0, qt).astype(bf16)
    m_ref[...] = jnp.full(m_ref.shape, -jnp.inf, f32)
    l_ref[...] = jnp.zeros(l_ref.shape, f32)
    acc_ref[...] = jnp.zeros(acc_ref.shape, f32)

    kr = lax.broadcasted_iota(i32, (tq, 2 * tq), 0)
    qc = lax.broadcasted_iota(i32, (tq, 2 * tq), 1)
    causal = jnp.where(qc >= tq, qc - tq, qc) >= kr
    krow = lax.broadcasted_iota(i32, (tq, 1), 0).astype(f32)

    def step(ki, masked):
        start = pl.multiple_of(ki * tq, tq)
        kb = k_ref[pl.ds(start, tq), :]
        vtb = vt_ref[:, pl.ds(start, tq)]
        s = _dot(kb, q2t_ref[...])
        s = s + slope * (krow + (ki * tq).astype(f32))
        if masked:
            s = jnp.where(causal, s, -jnp.inf)
        m_prev = m_ref[...]
        m_new = jnp.maximum(m_prev, jnp.max(s, axis=0, keepdims=True))
        alpha = jnp.exp(m_prev - m_new)
        p = jnp.exp(s - m_new)
        l_ref[...] = alpha * l_ref[...] + jnp.sum(p, axis=0, keepdims=True)
        acc_ref[...] = alpha * acc_ref[...] + _dot(vtb, p.astype(bf16))
        m_ref[...] = m_new

    def body(ki, carry):
        step(ki, False)
        return carry

    lax.fori_loop(0, qi, body, 0)
    step(qi, True)

    lp = lamp_ref[...]
    lam = (jnp.exp(jnp.sum(lp[0:1] * lp[1:2], axis=-1, keepdims=True))
           - jnp.exp(jnp.sum(lp[2:3] * lp[3:4], axis=-1, keepdims=True)) + lam_init)
    on = acc_ref[...] * (1.0 / l_ref[...])
    o = on[:, 0:tq] - lam * on[:, tq:2 * tq]
    o = o * lax.rsqrt(jnp.mean(o * o, axis=0, keepdims=True) + ATT_SUBLN_EPS)
    o = o * g_ref[...] * (1.0 - lam_init)
    o_ref[...] = o.T.astype(o_ref.dtype)


def _attention(att, slopes, lamp, subln_g, B, S, H, lam_init, tq):
    T = att.shape[0]
    nq = S // tq
    kern = functools.partial(_attn_kernel, tq=tq, lam_init=lam_init)
    return pl.pallas_call(
        kern,
        grid=(B, H, nq),
        in_specs=[pl.BlockSpec((tq, LANES), lambda b, h, q: (b * nq + q, h)),
                  pl.BlockSpec((S, LANES), lambda b, h, q: (b, H + h)),
                  pl.BlockSpec((S, LANES), lambda b, h, q: (b, 2 * H + h)),
                  pl.BlockSpec((1, 1, LANES), lambda b, h, q: (h, 0, 0)),
                  pl.BlockSpec((4, ATT_QK_DIM), lambda b, h, q: (0, 0)),
                  pl.BlockSpec((ATT_V_DIM, 1), lambda b, h, q: (0, 0))],
        out_specs=pl.BlockSpec((tq, LANES), lambda b, h, q: (b * nq + q, h)),
        out_shape=jax.ShapeDtypeStruct((T, H * ATT_V_DIM), bf16),
        scratch_shapes=[pltpu.VMEM((LANES, 2 * tq), bf16),
                        pltpu.VMEM((LANES, S), bf16),
                        pltpu.VMEM((1, 2 * tq), f32),
                        pltpu.VMEM((1, 2 * tq), f32),
                        pltpu.VMEM((LANES, 2 * tq), f32)],
        compiler_params=_cparams(("arbitrary", "arbitrary", "arbitrary")),
        name="diff_attention",
    )(att, att, att, slopes, lamp, subln_g.reshape(ATT_V_DIM, 1))


def _head_sums(x, ind, indt):
    s = _mm_exact_rhs(x, ind)
    return _mm_exact_rhs(s, indt)


def _rwkv_prep_kernel(*refs, W, has_vres):
    if has_vres:
        (f_ref, mu_ref, w0_ref, w2_ref, a0_ref, a2_ref, g2_ref, kk_ref, ka_ref, ind_ref, indt_ref,
         vf_ref, v0_ref, v1_ref, v2_ref,
         r_o, lw_o, kh_o, v_o, kn_o, kb_o, g_o, carry_ref) = refs
    else:
        (f_ref, mu_ref, w0_ref, w2_ref, a0_ref, a2_ref, g2_ref, kk_ref, ka_ref, ind_ref, indt_ref,
         r_o, lw_o, kh_o, v_o, kn_o, kb_o, g_o, carry_ref) = refs

    ti = pl.program_id(1)
    h = f_ref[...]
    tm = h.shape[0]

    @pl.when(ti == 0)
    def _():
        carry_ref[...] = jnp.zeros(carry_ref.shape, f32)

    rolled = pltpu.roll(h, 1, axis=0)
    row = lax.broadcasted_iota(i32, (tm, 1), 0)
    prev = jnp.where(row == 0, carry_ref[...], rolled)
    carry_ref[...] = h[tm - 1:tm, :]
    feats = h + (prev - h) * mu_ref[...]

    r = feats[:, 0:W]
    k = feats[:, W:2 * W]
    v = feats[:, 2 * W:3 * W]
    wa = feats[:, 3 * W:3 * W + LANES]
    g_lo = feats[:, 3 * W + LANES:3 * W + 2 * LANES]

    w = w0_ref[...] + _mm(jnp.tanh(wa), w2_ref[...], passes=3)
    lw_o[...] = -math.exp(-0.5) * _sigmoid(w)
    a = _sigmoid(a0_ref[...] + _mm(wa, a2_ref[...], passes=3))
    g_o[...] = _mm(_sigmoid(g_lo), g2_ref[...])

    if has_vres:
        mix = _sigmoid(v0_ref[...] + _mm(_mm(v, v1_ref[...]), v2_ref[...]))
        v = v + (vf_ref[...] - v) * mix

    kk = k * kk_ref[...]
    ss = _head_sums(kk * kk, ind_ref[...], indt_ref[...])
    kk = kk / jnp.maximum(jnp.sqrt(ss), 1e-12)
    r_o[...] = r
    kh_o[...] = k * (1.0 + (a - 1.0) * ka_ref[...])
    v_o[...] = v
    kn_o[...] = kk
    kb_o[...] = kk * a


def _rwkv_prep(feats, prm, vfirst, B, S, W, tm):
    T, COLS = feats.shape
    tpb = S // tm
    has_vres = vfirst is not None
    row = lambda n: pl.BlockSpec((1, n), lambda b, t: (0, 0))
    full = lambda a: pl.BlockSpec(a.shape, lambda b, t: (0, 0))
    tile = pl.BlockSpec((tm, W), lambda b, t: (b * tpb + t, 0))
    args = [feats, prm["mu"], prm["w0"], prm["w2p"], prm["a0"], prm["a2p"], prm["g2"], prm["k_k"], prm["k_a"],
            prm["ind"], prm["indt"]]
    specs = [pl.BlockSpec((tm, COLS), lambda b, t: (b * tpb + t, 0)), row(COLS), row(W), full(prm["w2p"]),
             row(W), full(prm["a2p"]), full(prm["g2"]), row(W), row(W), full(prm["ind"]), full(prm["indt"])]
    if has_vres:
        args += [vfirst, prm["v0"], prm["v1p"], prm["v2p"]]
        specs += [tile, row(W), full(prm["v1p"]), full(prm["v2p"])]
    out = jax.ShapeDtypeStruct((T, W), f32)
    kern = functools.partial(_rwkv_prep_kernel, W=W, has_vres=has_vres)
    return pl.pallas_call(
        kern,
        grid=(B, tpb),
        in_specs=specs,
        out_specs=[tile] * 7,
        out_shape=[out] * 7,
        scratch_shapes=[pltpu.VMEM((1, COLS), f32)],
        compiler_params=_cparams(("arbitrary", "arbitrary")),
        name="rwkv_prep",
    )(*args)


def _wkv_chunks(rs, lws, ks, vs, kns, kbs, passes):
    C = CHUNK
    P2 = 2 * C
    n = range(len(rs))
    ri = lax.broadcasted_iota(i32, (C, C), 0)
    ci = lax.broadcasted_iota(i32, (C, C), 1)
    tri = (ci <= ri).astype(bf16)
    lane = lax.broadcasted_iota(i32, (1, LANES), 1)
    m0 = (lane < RWKV_HEAD).astype(f32)
    m1 = 1.0 - m0
    rr = lax.broadcasted_iota(i32, (P2, P2), 0)
    cc = lax.broadcasted_iota(i32, (P2, P2), 1)
    same = jnp.where(rr >= C, 1, 0) == jnp.where(cc >= C, 1, 0)
    strict = same & (cc < rr)
    incl = same & (cc <= rr)
    incl2 = jnp.concatenate([incl, incl], axis=1)
    eye = (rr == cc).astype(f32)
    zeros_p = jnp.zeros((P2, LANES), f32)
    zeros_c = jnp.zeros((C, LANES), f32)
    stack = lambda x: jnp.concatenate([x * m0, x * m1], axis=0)
    fold = lambda x: x[0:C] + x[C:2 * C]

    def cumsum(lw):
        h1 = lw.astype(bf16)
        r1 = lw - h1.astype(f32)
        h2 = r1.astype(bf16)
        h3 = (r1 - h2.astype(f32)).astype(bf16)
        return (_dot(tri, h3) + _dot(tri, h2)) + _dot(tri, h1)

    cum = [cumsum(lws[j]) for j in n]
    cum_c = [cum[j][C - 1:C, :] for j in n]
    at = [-kns[j] * jnp.exp(cum[j] - lws[j]) for j in n]
    rt = [rs[j] * jnp.exp(cum[j]) for j in n]
    einv = [jnp.exp(-cum[j]) for j in n]
    bt = [kbs[j] * einv[j] for j in n]
    kt = [ks[j] * einv[j] for j in n]
    eh = [jnp.exp(cum_c[j] - cum[j]) for j in n]
    bh = [kbs[j] * eh[j] for j in n]
    kh = [ks[j] * eh[j] for j in n]
    w_c = [jnp.exp(cum_c[j]) for j in n]
    abd = [stack(at[j]) for j in n]
    vst = [stack(vs[j]) for j in n]
    lhs = [jnp.concatenate([abd[j], stack(rt[j])], axis=0) for j in n]
    rhs = [jnp.concatenate([stack(bt[j]), stack(kt[j])], axis=0) for j in n]
    gram = [_mm_nt(lhs[j], rhs[j], passes) for j in n]
    lab = [jnp.where(strict, gram[j][0:P2, 0:P2], 0.0) for j in n]
    lak = [jnp.where(strict, gram[j][0:P2, P2:2 * P2], 0.0) for j in n]
    mrbk = [jnp.where(incl2, gram[j][P2:2 * P2, :], 0.0) for j in n]

    x0 = [_mm(lak[j], vst[j], passes) for j in n]
    tinv = [eye + lab[j] for j in n]
    lp = lab
    for _ in range(int(math.log2(C)) - 1):
        lp = [_mm(lp[j], lp[j], passes) for j in n]
        tinv = [tinv[j] + _mm(lp[j], tinv[j], passes) for j in n]

    ta = [_mm(tinv[j], jnp.concatenate([abd[j], x0[j]], axis=1), passes) for j in n]
    rhs2 = [jnp.concatenate([ta[j], jnp.concatenate([zeros_p, vst[j]], axis=1)], axis=0) for j in n]
    z = [_mm(mrbk[j], rhs2[j], passes) for j in n]
    r2 = [rt[j] + fold(z[j][:, 0:LANES]) for j in n]
    y0 = [fold(z[j][:, LANES:2 * LANES]) for j in n]
    lhs3t = [jnp.concatenate([bh[j], kh[j]], axis=0).T for j in n]
    rhs3 = [jnp.concatenate([fold(ta[j]), jnp.concatenate([zeros_c, vs[j]], axis=1)], axis=0) for j in n]
    wmat = [_mm(lhs3t[j], rhs3[j], passes) for j in n]
    mmat = [jnp.where(same, wmat[j][:, 0:LANES], 0.0) + eye * w_c[j] for j in n]
    g0 = [jnp.where(same, wmat[j][:, LANES:2 * LANES], 0.0) for j in n]
    return r2, y0, mmat, g0


def _mm_nt(a, b, passes):
    if passes == 1:
        return _dot_nt(a.astype(bf16), b.astype(bf16))
    ah, al = _split2(a)
    bh, bl = _split2(b)
    return (_dot_nt(al, bh) + _dot_nt(ah, bl)) + _dot_nt(ah, bh)


def _wkv_intra_kernel(r_ref, lw_ref, k_ref, v_ref, kn_ref, kb_ref, r2_o, y0_o, m_o, g_o, *, nc, passes):
    C = CHUNK
    sls = [slice(c * C, (c + 1) * C) for c in range(nc)]
    take = lambda ref: [ref[sl, :] for sl in sls]
    r2, y0, mmat, g0 = _wkv_chunks(take(r_ref), take(lw_ref), take(k_ref), take(v_ref),
                                   take(kn_ref), take(kb_ref), passes)
    for c, sl in enumerate(sls):
        r2_o[sl, :] = r2[c]
        y0_o[sl, :] = y0[c]
        m_o[0, c] = mmat[c]
        g_o[0, c] = g0[c]


def _wkv_intra(r, lw, kh, v, kn, kb, nc, passes):
    T, W = r.shape
    npair = W // LANES
    rows = nc * CHUNK
    tile = pl.BlockSpec((rows, LANES), lambda p, i: (i, p))
    mat = pl.BlockSpec((1, nc, LANES, LANES), lambda p, i: (p, i, 0, 0))
    kern = functools.partial(_wkv_intra_kernel, nc=nc, passes=passes)
    return pl.pallas_call(
        kern,
        grid=(npair, T // rows),
        in_specs=[tile] * 6,
        out_specs=[tile, tile, mat, mat],
        out_shape=[jax.ShapeDtypeStruct((T, W), f32), jax.ShapeDtypeStruct((T, W), f32),
                   jax.ShapeDtypeStruct((npair, T // CHUNK, LANES, LANES), f32),
                   jax.ShapeDtypeStruct((npair, T // CHUNK, LANES, LANES), f32)],
        compiler_params=_cparams(("arbitrary", "arbitrary")),
        name="wkv_intra",
    )(r, lw, kh, v, kn, kb)


def _wkv_state_kernel(r2_ref, y0_ref, m_ref, g0_ref, r_ref, kh_ref, v_ref, g_ref,
                      lnw_ref, lnb_ref, rk_ref, o_ref, st_ref, y_ref, *, pg, cb, passes):
    C = CHUNK

    @pl.when(pl.program_id(2) == 0)
    def _():
        st_ref[...] = jnp.zeros(st_ref.shape, f32)

    pairs = range(pg)
    lanes = [slice(p * LANES, (p + 1) * LANES) for p in pairs]
    st = [st_ref[p] for p in pairs]
    for c in range(cb):
        rows = slice(c * C, (c + 1) * C)
        for p in pairs:
            y_ref[rows, lanes[p]] = _mm(r2_ref[rows, lanes[p]], st[p], passes) + y0_ref[rows, lanes[p]]
        st = [_mm(m_ref[p, c], st[p], passes) + g0_ref[p, c] for p in pairs]
    for p in pairs:
        st_ref[p] = st[p]

    rr = lax.broadcasted_iota(i32, (LANES, LANES), 0)
    cc = lax.broadcasted_iota(i32, (LANES, LANES), 1)
    ones_bd = (jnp.where(rr >= RWKV_HEAD, 1, 0) == jnp.where(cc >= RWKV_HEAD, 1, 0)).astype(bf16)
    for p in pairs:
        y = y_ref[:, lanes[p]]
        mu = _mm_exact_rhs(y, ones_bd) * (1.0 / RWKV_HEAD)
        d = y - mu
        var = _mm_exact_rhs(d * d, ones_bd) * (1.0 / RWKV_HEAD)
        yn = d * lax.rsqrt(var + RWKV_GN_EPS) * lnw_ref[:, lanes[p]] + lnb_ref[:, lanes[p]]
        rk = r_ref[:, lanes[p]] * kh_ref[:, lanes[p]] * rk_ref[:, lanes[p]]
        bonus = _mm_exact_rhs(rk, ones_bd) * v_ref[:, lanes[p]]
        o_ref[:, lanes[p]] = ((yn + bonus) * g_ref[:, lanes[p]]).astype(o_ref.dtype)


def _wkv_state(r2, y0, mm, g0, r, kh, v, g, lnw, lnb, rk, B, S, pg, cb, passes):
    T, W = r.shape
    npair = W // LANES
    rows = cb * CHUNK
    steps = S // rows
    seq = pl.BlockSpec((rows, pg * LANES), lambda b, q, c: (b * steps + c, q))
    mat = pl.BlockSpec((pg, cb, LANES, LANES), lambda b, q, c: (q, b * steps + c, 0, 0))
    prow = pl.BlockSpec((1, pg * LANES), lambda b, q, c: (0, q))
    kern = functools.partial(_wkv_state_kernel, pg=pg, cb=cb, passes=passes)
    return pl.pallas_call(
        kern,
        grid=(B, npair // pg, steps),
        in_specs=[seq, seq, mat, mat, seq, seq, seq, seq, prow, prow, prow],
        out_specs=seq,
        out_shape=jax.ShapeDtypeStruct((T, W), bf16),
        scratch_shapes=[pltpu.VMEM((pg, LANES, LANES), f32), pltpu.VMEM((rows, pg * LANES), f32)],
        compiler_params=_cparams(("arbitrary", "arbitrary", "arbitrary")),
        name="wkv_state",
    )(r2, y0, mm, g0, r, kh, v, g, lnw, lnb, rk)


def _postmix_kernel(oa_ref, orw_ref, wa_ref, wr_ref, x_ref, ga_ref, gpost_ref, gpre_ref, sc_ref, sh_ref, wrt_ref,
                    x1_o, h2_o, lg_o):
    mixed = _dot(oa_ref[...], wa_ref[...]) + _dot(orw_ref[...], wr_ref[...])
    ms = jnp.mean(mixed * mixed, axis=-1, keepdims=True)
    x1 = x_ref[...] + ga_ref[0] * (mixed * lax.rsqrt(ms + NORM_EPS) * gpost_ref[...])
    x1_o[...] = x1
    ms1 = jnp.mean(x1 * x1, axis=-1, keepdims=True)
    h2 = (x1 * lax.rsqrt(ms1 + NORM_EPS) * gpre_ref[...]) * (1.0 + sc_ref[0]) + sh_ref[0]
    h2_o[...] = _pack_rows(h2)
    lg_o[...] = _mm_nt(wrt_ref[...], h2, 3)


def _postmix(o_att, o_rwkv, w_out_a, w_out_r, x2, mod3, g_post, g_pre, w_rt, S, tm):
    T, D = x2.shape
    WA = o_att.shape[1]
    WR = o_rwkv.shape[1]
    E = w_rt.shape[0]
    tpb = S // tm
    modspec = lambda seg: pl.BlockSpec((1, 1, D), lambda i: ((i // tpb) * N_MOD + seg, 0, 0))
    tile = pl.BlockSpec((tm, D), lambda i: (i, 0))
    return pl.pallas_call(
        _postmix_kernel,
        grid=(T // tm,),
        in_specs=[pl.BlockSpec((tm, WA), lambda i: (i, 0)),
                  pl.BlockSpec((tm, WR), lambda i: (i, 0)),
                  pl.BlockSpec((WA, D), lambda i: (0, 0), pipeline_mode=pl.Buffered(1)),
                  pl.BlockSpec((WR, D), lambda i: (0, 0), pipeline_mode=pl.Buffered(1)),
                  tile, modspec(2),
                  pl.BlockSpec((1, D), lambda i: (0, 0)),
                  pl.BlockSpec((1, D), lambda i: (0, 0)),
                  modspec(4), modspec(3),
                  pl.BlockSpec((E, D), lambda i: (0, 0))],
        out_specs=[tile, pl.BlockSpec((tm, D // 2), lambda i: (i, 0)),
                   pl.BlockSpec((E, tm), lambda i: (0, i))],
        out_shape=[jax.ShapeDtypeStruct((T, D), f32), jax.ShapeDtypeStruct((T, D // 2), u32),
                   jax.ShapeDtypeStruct((E, T), f32)],
        compiler_params=_cparams(("arbitrary",)),
        name="postmix",
    )(o_att, o_rwkv, w_out_a, w_out_r, x2, mod3, g_post.reshape(1, D), g_pre.reshape(1, D), mod3, mod3, w_rt)


def _first_max(x, iota, n):
    mx = jnp.max(x, axis=0, keepdims=True)
    idx = jnp.min(jnp.where(x == mx, iota, n), axis=0, keepdims=True)
    return mx, idx


def _router_kernel(lg_ref, bias_ref, eidx_o, gate_o, rank_o, cnt_o, cnt_ref):
    E = N_EXPERTS
    G = N_GROUPS
    per = E // G
    tm = lg_ref.shape[1]

    @pl.when(pl.program_id(0) == 0)
    def _():
        cnt_ref[...] = jnp.zeros(cnt_ref.shape, f32)

    scores = _sigmoid(lg_ref[...])
    biased = scores + bias_ref[...]
    neg = -jnp.inf

    iota_p = lax.broadcasted_iota(i32, (per, tm), 0).astype(f32)
    gs = []
    for g in range(G):
        xg = biased[g * per:(g + 1) * per, :]
        m1, i1 = _first_max(xg, iota_p, per)
        m2 = jnp.max(jnp.where(iota_p == i1, neg, xg), axis=0, keepdims=True)
        gs.append(m1 + m2)
    gsc = jnp.concatenate(gs, axis=0)
    iota_g = lax.broadcasted_iota(i32, (G, tm), 0).astype(f32)
    gsel = jnp.zeros((G, tm), f32)
    for _ in range(TOPK_GROUPS):
        _, gi = _first_max(gsc, iota_g, G)
        hit = iota_g == gi
        gsel = jnp.where(hit, 1.0, gsel)
        gsc = jnp.where(hit, neg, gsc)
    masked = jnp.concatenate(
        [jnp.where(gsel[g:g + 1, :] > 0.0, biased[g * per:(g + 1) * per, :], neg) for g in range(G)], axis=0)

    iota_e = lax.broadcasted_iota(i32, (E, tm), 0).astype(f32)
    sel = jnp.zeros((E, tm), f32)
    idxs, vals = [], []
    for _ in range(TOP_K):
        _, ei = _first_max(masked, iota_e, E)
        hit = iota_e == ei
        idxs.append(ei)
        vals.append(jnp.sum(jnp.where(hit, scores, 0.0), axis=0, keepdims=True))
        sel = jnp.where(hit, 1.0, sel)
        masked = jnp.where(hit, neg, masked)
    tot = vals[0]
    for vv in vals[1:]:
        tot = tot + vv
    eidx_o[...] = jnp.concatenate(idxs, axis=0).astype(i32)
    gate_o[...] = jnp.concatenate([vv / tot * ROUTED_SCALE for vv in vals], axis=0)

    rr = lax.broadcasted_iota(i32, (tm, tm), 0)
    cc = lax.broadcasted_iota(i32, (tm, tm), 1)
    before = (rr < cc).astype(bf16)
    pos = _dot(sel.astype(bf16), before)
    rank_o[...] = jnp.concatenate(
        [jnp.sum(jnp.where(iota_e == ei, pos, 0.0), axis=0, keepdims=True) for ei in idxs], axis=0)
    lane = lax.broadcasted_iota(i32, cnt_ref.shape, 1)
    cnt_ref[...] = jnp.where(lane == pl.program_id(0), jnp.sum(sel, axis=1, keepdims=True), cnt_ref[...])
    cnt_o[...] = cnt_ref[...].astype(i32)


def _router(logits_t, bias, tm):
    E, T = logits_t.shape
    assert T // tm <= LANES
    k_tile = pl.BlockSpec((TOP_K, tm), lambda i: (0, i))
    return pl.pallas_call(
        _router_kernel,
        grid=(T // tm,),
        in_specs=[pl.BlockSpec((E, tm), lambda i: (0, i)),
                  pl.BlockSpec((E, 1), lambda i: (0, 0))],
        out_specs=[k_tile, k_tile, k_tile, pl.BlockSpec((E, LANES), lambda i: (0, 0))],
        out_shape=[jax.ShapeDtypeStruct((TOP_K, T), i32), jax.ShapeDtypeStruct((TOP_K, T), f32),
                   jax.ShapeDtypeStruct((TOP_K, T), f32), jax.ShapeDtypeStruct((E, LANES), i32)],
        scratch_shapes=[pltpu.VMEM((E, LANES), f32)],
        compiler_params=_cparams(("arbitrary",)),
        name="router",
    )(logits_t, bias.reshape(E, 1))


def _row_copy(src_ref, s, dst_ref, d, sem):
    return pltpu.make_async_copy(src_ref.at[pl.ds(s, 1), :], dst_ref.at[pl.ds(d, 1), :], sem)


def _zero_fill(cnt_ref, pstart_ref, nused_ref, z_ref, xs_out, sem, blk, nblk, start):
    def act(cp):
        if start:
            cp.start()
        else:
            cp.wait()

    def per_expert(e, carry):
        c = cnt_ref[e]
        base = pstart_ref[e] + c
        npad = (blk - (c & (blk - 1))) & (blk - 1)
        head = (-base) & (SUBLANES - 1)

        def one_row(j, carry2):
            act(_row_copy(z_ref, 0, xs_out, base + j, sem))
            return carry2

        lax.fori_loop(0, head, one_row, 0)
        rem = npad - head
        aligned = base + head
        p = blk // 2
        while p >= SUBLANES:
            off = pl.multiple_of(aligned + (rem & ~(2 * p - 1)), SUBLANES)

            @pl.when((rem & p) != 0)
            def _(p=p, off=off):
                act(pltpu.make_async_copy(z_ref.at[pl.ds(0, p), :], xs_out.at[pl.ds(off, p), :], sem))

            p //= 2
        return carry

    lax.fori_loop(0, N_EXPERTS, per_expert, 0)

    def per_block(b, carry):
        act(pltpu.make_async_copy(z_ref, xs_out.at[pl.ds(pl.multiple_of(b * blk, blk), blk), :], sem))
        return carry

    lax.fori_loop(nused_ref[0], nblk, per_block, 0)


def _seg_copies(ssrc_ref, sdst_ref, sn_ref, tile, buf_ref, hbm_ref, sem, tm, to_hbm, start):
    def act(cp):
        if start:
            cp.start()
        else:
            cp.wait()

    def per_expert(e, carry):
        idx = tile * N_EXPERTS + e
        n = sn_ref[idx]
        src = ssrc_ref[idx]
        dst = sdst_ref[idx]
        p = tm
        while p >= SUBLANES:
            off = n & ~(2 * p - 1)

            @pl.when((n & p) != 0)
            def _(p=p, off=off):
                v = buf_ref.at[pl.ds(pl.multiple_of(src + off, SUBLANES), p), :]
                h = hbm_ref.at[pl.ds(pl.multiple_of(dst + off, SUBLANES), p), :]
                act(pltpu.make_async_copy(v, h, sem) if to_hbm else pltpu.make_async_copy(h, v, sem))

            p //= 2
        return carry

    lax.fori_loop(0, N_EXPERTS, per_expert, 0)


def _seg_wait(total_rows, buf_ref, hbm_ref, sem, to_hbm):
    p = 1 << (buf_ref.shape[0].bit_length() - 1)
    while p >= SUBLANES:
        @pl.when((total_rows & p) != 0)
        def _(p=p):
            v = buf_ref.at[pl.ds(0, p), :]
            h = hbm_ref.at[pl.ds(0, p), :]
            (pltpu.make_async_copy(v, h, sem) if to_hbm else pltpu.make_async_copy(h, v, sem)).wait()

        p //= 2


SORT_CHUNK = 512


def _dispatch_kernel(ssrc_ref, sdst_ref, sn_ref, stot_ref, cnt_ref, pstart_ref, nused_ref, lp_ref, h_ref,
                     wsg_ref, wsu_ref, wsd_ref, xs_out, ysh_o, sb_ref, z_ref, sems, sem_z, *, blk, nblk):
    i = pl.program_id(0)
    last = pl.num_programs(0) - 1
    tm = h_ref.shape[0]
    lmax = sb_ref.shape[1]
    slot = i % 2
    seg = functools.partial(_seg_copies, ssrc_ref, sdst_ref, sn_ref, hbm_ref=xs_out, tm=tm, to_hbm=True)

    def seg_wait(tile, s):
        _seg_wait(stot_ref[tile], sb_ref.at[s], xs_out, sems.at[s], True)

    @pl.when(i >= 2)
    def _():
        seg_wait(i - 2, slot)

    hlo, hhi = _unpack_rows(h_ref[...])
    hb = jnp.concatenate([hlo.astype(bf16), hhi.astype(bf16)], axis=1)
    lpv = lp_ref[...]
    half = hb.shape[1] // 2
    for c in range(lmax // SORT_CHUNK):
        jj = (lax.broadcasted_iota(i32, (SORT_CHUNK, tm), 0) + c * SORT_CHUNK).astype(f32)
        onehot = jnp.zeros((SORT_CHUNK, tm), f32)
        for k in range(TOP_K):
            onehot = jnp.where(jj == lpv[k:k + 1, :], 1.0, onehot)
        rows = _dot(onehot.astype(bf16), hb)
        lo = lax.shift_right_logical(lax.bitcast_convert_type(rows[:, :half], u32), jnp.uint32(16))
        hi = lax.bitcast_convert_type(rows[:, half:], u32) & jnp.uint32(0xFFFF0000)
        sb_ref[slot, c * SORT_CHUNK:(c + 1) * SORT_CHUNK, :] = lo | hi

    seg(i, buf_ref=sb_ref.at[slot], sem=sems.at[slot], start=True)

    gt = _dot(hb, wsg_ref[...])
    up = _dot(hb, wsu_ref[...])
    ysh_o[...] = _dot(((gt * _sigmoid(gt)) * up).astype(bf16), wsd_ref[...]).astype(ysh_o.dtype)

    @pl.when(i == last)
    def _():
        z_ref[...] = jnp.zeros(z_ref.shape, u32)
        _zero_fill(cnt_ref, pstart_ref, nused_ref, z_ref, xs_out, sem_z, blk, nblk, True)
        _zero_fill(cnt_ref, pstart_ref, nused_ref, z_ref, xs_out, sem_z, blk, nblk, False)
        seg_wait(i, slot)

        @pl.when(i >= 1)
        def _():
            seg_wait(i - 1, 1 - slot)


def _dispatch(seg_src, seg_dst, seg_n, seg_tot, counts, pad_start, nused, lp_t, h2p, wsg, wsu, wsd,
              P, blk, tm, lmax):
    T, DW = h2p.shape
    D, DS = wsg.shape
    assert blk & (blk - 1) == 0 and tm & (tm - 1) == 0 and lmax % SORT_CHUNK == 0
    kern = functools.partial(_dispatch_kernel, blk=blk, nblk=P // blk)
    const = lambda shape: pl.BlockSpec(shape, lambda i, *_: (0, 0))
    grid_spec = pltpu.PrefetchScalarGridSpec(
        num_scalar_prefetch=7,
        grid=(T // tm,),
        in_specs=[pl.BlockSpec((TOP_K, tm), lambda i, *_: (0, i)),
                  pl.BlockSpec((tm, DW), lambda i, *_: (i, 0)),
                  const((D, DS)), const((D, DS)), const((DS, D))],
        out_specs=[pl.BlockSpec(memory_space=pl.ANY), pl.BlockSpec((tm, D), lambda i, *_: (i, 0))],
        scratch_shapes=[pltpu.VMEM((2, lmax, DW), u32), pltpu.VMEM((blk, DW), u32),
                        pltpu.SemaphoreType.DMA((2,)), pltpu.SemaphoreType.DMA],
    )
    return pl.pallas_call(
        kern,
        grid_spec=grid_spec,
        out_shape=[jax.ShapeDtypeStruct((P, DW), u32), jax.ShapeDtypeStruct((T, D), bf16)],
        compiler_params=_cparams(("arbitrary",)),
        name="moe_dispatch",
    )(seg_src, seg_dst, seg_n, seg_tot, counts, pad_start, nused, lp_t, h2p, wsg, wsu, wsd)


def _experts_kernel(blk_e_ref, nxt_e_ref, nused_ref, xs_ref, wg_hbm, wu_hbm, wd_hbm, ys_ref,
                    wg_f, wu_f, wd_f, wg_s, wu_s, wd_s, sems, *, layer):
    i = pl.program_id(0)
    e = blk_e_ref[i]
    changed = jnp.logical_or(i == 0, e != blk_e_ref[jnp.maximum(i - 1, 0)])

    def weight_copies(ex):
        return (pltpu.make_async_copy(wg_hbm.at[layer, ex], wg_f, sems.at[0]),
                pltpu.make_async_copy(wu_hbm.at[layer, ex], wu_f, sems.at[1]),
                pltpu.make_async_copy(wd_hbm.at[layer, ex], wd_f, sems.at[2]))

    @pl.when(i == 0)
    def _():
        for cp in weight_copies(e):
            cp.start()

    @pl.when(changed)
    def _():
        for cp in weight_copies(e):
            cp.wait()
        for src, dst in ((wg_f, wg_s), (wu_f, wu_s), (wd_f, wd_s)):
            rows = src.shape[0] // 8
            for c in range(8):
                dst[c * rows:(c + 1) * rows, :] = src[c * rows:(c + 1) * rows, :].astype(bf16)
        nxt = nxt_e_ref[i]

        @pl.when(nxt >= 0)
        def _():
            for cp in weight_copies(nxt):
                cp.start()

    @pl.when(i < nused_ref[0])
    def _():
        lo, hi = _unpack_rows(xs_ref[...])
        x = jnp.concatenate([lo.astype(bf16), hi.astype(bf16)], axis=1)
        gt = _dot(x, wg_s[...])
        up = _dot(x, wu_s[...])
        hmid = (gt * _sigmoid(gt)) * up
        ys_ref[...] = _pack_rows(_dot(hmid.astype(bf16), wd_s[...]))

    @pl.when(i >= nused_ref[0])
    def _():
        ys_ref[...] = jnp.zeros(ys_ref.shape, u32)


def _experts(blk_e, nxt_e, nused, xs, w_gate, w_up, w_down, layer, blk):
    P, DW = xs.shape
    D, DE = w_gate.shape[-2:]
    nblk = P // blk
    row_idx = lambda i, be, nx, nu: (jnp.minimum(i, nu[0] - 1), 0)
    hbm = pl.BlockSpec(memory_space=pl.ANY)
    grid_spec = pltpu.PrefetchScalarGridSpec(
        num_scalar_prefetch=3,
        grid=(nblk,),
        in_specs=[pl.BlockSpec((blk, DW), row_idx), hbm, hbm, hbm],
        out_specs=pl.BlockSpec((blk, DW), lambda i, be, nx, nu: (i, 0)),
        scratch_shapes=[pltpu.VMEM((D, DE), f32), pltpu.VMEM((D, DE), f32), pltpu.VMEM((DE, D), f32),
                        pltpu.VMEM((D, DE), bf16), pltpu.VMEM((D, DE), bf16), pltpu.VMEM((DE, D), bf16),
                        pltpu.SemaphoreType.DMA((3,))],
    )
    return pl.pallas_call(
        functools.partial(_experts_kernel, layer=layer),
        grid_spec=grid_spec,
        out_shape=jax.ShapeDtypeStruct((P, DW), u32),
        compiler_params=_cparams(("arbitrary",)),
        name="moe_experts",
    )(blk_e, nxt_e, nused, xs, w_gate, w_up, w_down)


def _combine_kernel(ssrc_ref, sdst_ref, sn_ref, stot_ref, lp_ref, gate_ref, ys_hbm, ysh_ref, x1_ref, gf_ref,
                    gpost_ref, x2_o, yb_ref, sems):
    i = pl.program_id(0)
    n = pl.num_programs(0)
    tm = x1_ref.shape[0]
    lmax = yb_ref.shape[1]
    slot = i % 2
    seg = functools.partial(_seg_copies, ssrc_ref, sdst_ref, sn_ref, hbm_ref=ys_hbm, tm=tm, to_hbm=False)

    @pl.when(i == 0)
    def _():
        yb_ref[...] = jnp.zeros(yb_ref.shape, u32)
        seg(i, buf_ref=yb_ref.at[slot], sem=sems.at[slot], start=True)

    @pl.when(i + 1 < n)
    def _():
        seg(i + 1, buf_ref=yb_ref.at[1 - slot], sem=sems.at[1 - slot], start=True)

    _seg_wait(stot_ref[i], yb_ref.at[slot], ys_hbm, sems.at[slot], False)

    lp = lp_ref[...]
    gate = gate_ref[...]
    ysh = ysh_ref[...].astype(f32)
    half = ysh.shape[1] // 2
    lo = ysh[:, :half]
    hi = ysh[:, half:]
    for c in range(lmax // SORT_CHUNK):
        jl = (lax.broadcasted_iota(i32, (tm, SORT_CHUNK), 1) + c * SORT_CHUNK).astype(f32)
        g = jnp.zeros((tm, SORT_CHUNK), f32)
        for k in range(TOP_K):
            g = jnp.where(jl == lp[:, k:k + 1], gate[:, k:k + 1], g)
        g_hi, g_lo = _split2(g)
        a, b = _unpack_rows(yb_ref[slot, c * SORT_CHUNK:(c + 1) * SORT_CHUNK, :])
        a = a.astype(bf16)
        b = b.astype(bf16)
        lo = lo + (_dot(g_lo, a) + _dot(g_hi, a))
        hi = hi + (_dot(g_lo, b) + _dot(g_hi, b))
    y = jnp.concatenate([lo, hi], axis=1)
    ms = jnp.mean(y * y, axis=-1, keepdims=True)
    x2_o[...] = x1_ref[...] + gf_ref[0] * (y * lax.rsqrt(ms + NORM_EPS) * gpost_ref[...])


def _combine(seg_src, seg_dst, seg_n, seg_tot, lp_tk, gate_tk, ys, ysh, x1, mod3, g_post, S, tm, lmax):
    T, D = x1.shape
    DW = ys.shape[1]
    tpb = S // tm
    tile = pl.BlockSpec((tm, D), lambda i, *_: (i, 0))
    ktile = pl.BlockSpec((tm, TOP_K), lambda i, *_: (i, 0))
    grid_spec = pltpu.PrefetchScalarGridSpec(
        num_scalar_prefetch=4,
        grid=(T // tm,),
        in_specs=[ktile, ktile,
                  pl.BlockSpec(memory_space=pl.ANY),
                  tile, tile,
                  pl.BlockSpec((1, 1, D), lambda i, *_: ((i // tpb) * N_MOD + 5, 0, 0)),
                  pl.BlockSpec((1, D), lambda i, *_: (0, 0))],
        out_specs=tile,
        scratch_shapes=[pltpu.VMEM((2, lmax, DW), u32), pltpu.SemaphoreType.DMA((2,))],
    )
    return pl.pallas_call(
        _combine_kernel,
        grid_spec=grid_spec,
        out_shape=jax.ShapeDtypeStruct((T, D), f32),
        compiler_params=_cparams(("arbitrary",)),
        name="moe_combine",
    )(seg_src, seg_dst, seg_n, seg_tot, lp_tk, gate_tk, ys, ysh, x1, mod3, g_post.reshape(1, D))


def _tile(n, pref):
    t = min(n, pref)
    assert n % t == 0, (n, t)
    return t


def _layer(i, x2, mod3, p, wexp, v_first, B, S, cfg):
    T, D = x2.shape
    H = (D // 2) // ATT_V_DIM
    W = D - D // 2
    att_cols = 2 * H * 2 * ATT_QK_DIM + H * ATT_V_DIM
    lam_init = 0.8 - 0.6 * math.exp(-0.3 * i)

    w_in_bf = p["w_in"].astype(bf16)
    att = _inproj(x2, mod3, p["g_pre_mix"], w_in_bf[:, :att_cols], bf16, S,
                  _tile(S, cfg["tm_in"]), cfg["tn_att"], 1, 0)
    feats = _inproj(x2, mod3, p["g_pre_mix"], w_in_bf[:, att_cols:], f32, S,
                    _tile(S, cfg["tm_in_rwkv"]), cfg["tn_rwkv"], 1, 0)

    slopes = jnp.broadcast_to(
        (2.0 ** (-ALIBI_MAX_BIAS * jnp.arange(1, H + 1, dtype=f32) / H))[:, None, None], (H, 1, LANES))
    lamp = jnp.stack([p["lam_q1"], p["lam_k1"], p["lam_q2"], p["lam_k2"]])
    o_att = _attention(att, slopes, lamp, p["att_subln_g"], B, S, H, lam_init, _tile(S, cfg["tq"]))

    cols = feats.shape[1]
    zw = jnp.zeros((RWKV_A_RANK, W), f32)
    heads = W // RWKV_HEAD
    ind = (jnp.arange(W)[:, None] // RWKV_HEAD == jnp.arange(LANES)[None, :]).astype(bf16)
    prm = {
        "mu": p["rwkv_mu"].reshape(1, cols), "w0": p["rwkv_w0"].reshape(1, W),
        "w2p": jnp.concatenate([p["rwkv_w2"], zw], axis=0),
        "a0": p["rwkv_a0"].reshape(1, W),
        "a2p": jnp.concatenate([jnp.zeros((RWKV_W_RANK, W), f32), p["rwkv_a2"]], axis=0),
        "g2": p["rwkv_g2"], "k_k": p["rwkv_k_k"].reshape(1, W), "k_a": p["rwkv_k_a"].reshape(1, W),
        "ind": ind, "indt": ind.T,
    }
    if v_first is not None:
        padc = LANES - RWKV_V_RANK
        prm["v0"] = p["rwkv_v0"].reshape(1, W)
        prm["v1p"] = jnp.pad(p["rwkv_v1"], ((0, 0), (0, padc)))
        prm["v2p"] = jnp.pad(p["rwkv_v2"], ((0, padc), (0, 0)))
    r, lw, kh, v, kn, kb, g = _rwkv_prep(feats, prm, v_first, B, S, W, _tile(S, cfg["tm_prep"]))
    if v_first is None:
        v_first = v
    r2, y0, mmat, g0 = _wkv_intra(r, lw, kh, v, kn, kb, min(cfg["nc"], S // CHUNK), cfg["passes_intra"])
    o_rwkv = _wkv_state(r2, y0, mmat, g0, r, kh, v, g, p["rwkv_lnx_w"].reshape(1, W),
                        p["rwkv_lnx_b"].reshape(1, W), p["rwkv_r_k"].reshape(1, W), B, S,
                        min(cfg["pg"], W // LANES), min(cfg["cb"], S // CHUNK), cfg["passes_state"])
    del heads

    w_out_bf = p["w_out"].astype(bf16)
    x1, h2, logits_t = _postmix(o_att, o_rwkv, w_out_bf[:D // 2], w_out_bf[D // 2:], x2, mod3,
                                p["g_post_mix"], p["g_pre_ffn"], p["w_router"].T, S, _tile(S, cfg["tm_post"]))

    tm_t = _tile(T, cfg["tm_tile"])
    ntiles = T // tm_t
    eidx_t, gate_t, lrank_t, cnt_tbl = _router(logits_t, p["router_bias"], tm_t)
    blk = cfg["blk"]
    run = (cnt_tbl[:, :ntiles].T + SUBLANES - 1) // SUBLANES * SUBLANES
    counts = jnp.sum(run, axis=0)
    padded = (counts + blk - 1) // blk * blk
    pad_end = jnp.cumsum(padded)
    pad_start = pad_end - padded
    e_ids = jnp.arange(N_EXPERTS, dtype=i32)
    tile_off = jnp.cumsum(run, axis=0) - run
    loc_off = jnp.cumsum(run, axis=1) - run
    seg_dst = (pad_start[None, :] + tile_off).reshape(-1).astype(i32)
    seg_src = loc_off.reshape(-1).astype(i32)
    seg_n = run.reshape(-1).astype(i32)
    seg_tot = jnp.sum(run, axis=1).astype(i32)
    loc_tok = jnp.repeat(loc_off, tm_t, axis=0)
    lp_t = jnp.sum(jnp.where(eidx_t[:, :, None] == e_ids, loc_tok[None], 0), axis=-1).astype(f32) + lrank_t
    lmax = -(-(tm_t * TOP_K + N_EXPERTS * (SUBLANES - 1)) // SORT_CHUNK) * SORT_CHUNK
    nblk = -(-(T * TOP_K + ntiles * N_EXPERTS * (SUBLANES - 1)) // blk) + N_EXPERTS
    P = nblk * blk
    blk_start = jnp.arange(nblk, dtype=i32) * blk
    nused = (pad_end[-1] // blk).astype(i32).reshape(1)
    blk_pos = jnp.minimum(blk_start, pad_end[-1] - blk)
    blk_e = jnp.minimum(jnp.sum((pad_end[None, :] <= blk_pos[:, None]).astype(i32), axis=1), N_EXPERTS - 1)
    cand = jnp.where(counts > 0, e_ids, N_EXPERTS)
    later = jnp.where(e_ids[None, :] > blk_e[:, None], cand[None, :], N_EXPERTS)
    nxt_e = jnp.min(later, axis=1)
    nxt_e = jnp.where(nxt_e >= N_EXPERTS, -1, nxt_e).astype(i32)

    xs, ysh = _dispatch(seg_src, seg_dst, seg_n, seg_tot, counts.astype(i32), pad_start.astype(i32), nused, lp_t, h2,
                        p["w_sh_gate"].astype(bf16), p["w_sh_up"].astype(bf16), p["w_sh_down"].astype(bf16),
                        P, blk, tm_t, lmax)
    ys = _experts(blk_e, nxt_e, nused, xs, wexp[0], wexp[1], wexp[2], i, blk)
    x_out = _combine(seg_src, seg_dst, seg_n, seg_tot, lp_t.T, gate_t.T, ys, ysh, x1,
                     mod3, p["g_post_ffn"], S, tm_t, lmax)
    return x_out, v_first


_CFG = dict(tm_in=1024, tn_att=1024, tm_in_rwkv=512, tn_rwkv=1664, tq=512, tm_prep=256, nc=8, passes_intra=1, passes_state=1, pg=4, cb=8,
            tm_post=512, blk=256, tm_tile=256)

_LAYER_KEYS = ("g_pre_mix", "g_post_mix", "g_pre_ffn", "g_post_ffn", "w_in", "w_out", "lam_q1", "lam_k1",
               "lam_q2", "lam_k2", "att_subln_g", "rwkv_mu", "rwkv_w0", "rwkv_w2", "rwkv_a0", "rwkv_a2",
               "rwkv_g2", "rwkv_k_k", "rwkv_k_a", "rwkv_r_k", "rwkv_lnx_w", "rwkv_lnx_b", "w_router",
               "router_bias", "w_sh_gate", "w_sh_up", "w_sh_down")


def _forward(x, c, params, cfg):
    B, S, D = x.shape
    L = params["w_in"].shape[0]
    bp = 16
    c_pad = jnp.zeros((bp, D), f32).at[:B].set(c)
    mod = _ada_mod(c_pad, params["w_ada"], params["b_ada"])
    x2 = x.reshape(B * S, D)
    v_first = None
    for i in range(L):
        p = {k: params[k][i] for k in _LAYER_KEYS}
        if i > 0:
            p["rwkv_v0"] = params["rwkv_v0"][i - 1]
            p["rwkv_v1"] = params["rwkv_v1"][i - 1]
            p["rwkv_v2"] = params["rwkv_v2"][i - 1]
        mod3 = mod[i, :B].reshape(B * N_MOD, 1, D)
        wexp = (params["w_exp_gate"], params["w_exp_up"], params["w_exp_down"])
        x2, v_first = _layer(i, x2, mod3, p, wexp, v_first, B, S, cfg)
    return x2.reshape(B, S, D)


def kernel(x, c, w_ada, b_ada, g_pre_mix, g_post_mix, g_pre_ffn, g_post_ffn, w_in, w_out, lam_q1, lam_k1, lam_q2, lam_k2, att_subln_g, rwkv_mu, rwkv_w0, rwkv_w2, rwkv_a0, rwkv_a2, rwkv_g2, rwkv_k_k, rwkv_k_a, rwkv_r_k, rwkv_lnx_w, rwkv_lnx_b, rwkv_v0, rwkv_v1, rwkv_v2, w_router, router_bias, w_exp_gate, w_exp_up, w_exp_down, w_sh_gate, w_sh_up, w_sh_down):
    params = dict(w_ada=w_ada, b_ada=b_ada, g_pre_mix=g_pre_mix, g_post_mix=g_post_mix, g_pre_ffn=g_pre_ffn,
                  g_post_ffn=g_post_ffn, w_in=w_in, w_out=w_out, lam_q1=lam_q1, lam_k1=lam_k1, lam_q2=lam_q2,
                  lam_k2=lam_k2, att_subln_g=att_subln_g, rwkv_mu=rwkv_mu, rwkv_w0=rwkv_w0, rwkv_w2=rwkv_w2,
                  rwkv_a0=rwkv_a0, rwkv_a2=rwkv_a2, rwkv_g2=rwkv_g2, rwkv_k_k=rwkv_k_k, rwkv_k_a=rwkv_k_a,
                  rwkv_r_k=rwkv_r_k, rwkv_lnx_w=rwkv_lnx_w, rwkv_lnx_b=rwkv_lnx_b, rwkv_v0=rwkv_v0,
                  rwkv_v1=rwkv_v1, rwkv_v2=rwkv_v2, w_router=w_router, router_bias=router_bias,
                  w_exp_gate=w_exp_gate, w_exp_up=w_exp_up, w_exp_down=w_exp_down, w_sh_gate=w_sh_gate,
                  w_sh_up=w_sh_up, w_sh_down=w_sh_down)
    return _forward(x, c, params, _CFG)
```

```python
import functools
import math

import jax
import jax.numpy as jnp
from jax import lax
from jax.experimental import pallas as pl
from jax.experimental.pallas import tpu as pltpu

f32 = jnp.float32
bf16 = jnp.bfloat16
i32 = jnp.int32
u32 = jnp.uint32

ATT_QK_DIM = 64
ATT_V_DIM = 128
ALIBI_MAX_BIAS = 8.0
ATT_SUBLN_EPS = 1e-5
RWKV_HEAD = 64
RWKV_W_RANK = 64
RWKV_A_RANK = 64
RWKV_G_RANK = 128
RWKV_V_RANK = 32
RWKV_GN_EPS = 64e-5
N_EXPERTS = 64
N_GROUPS = 8
TOPK_GROUPS = 4
TOP_K = 8
ROUTED_SCALE = 2.5
NORM_EPS = 1e-6
N_MOD = 6

LANES = 128
SUBLANES = 8
CHUNK = 64
VMEM_LIMIT = 56 * 1024 * 1024


def _cparams(sem):
    return pltpu.CompilerParams(dimension_semantics=sem, vmem_limit_bytes=VMEM_LIMIT)


def _dot(a, b):
    return jnp.dot(a, b, preferred_element_type=f32)


def _dot_nt(a, b):
    return lax.dot_general(a, b, (((1,), (1,)), ((), ())), preferred_element_type=f32)


def _split2(x):
    hi = x.astype(bf16)
    lo = (x - hi.astype(f32)).astype(bf16)
    return hi, lo


def _mm(a, b, passes=1):
    if passes == 1:
        return _dot(a.astype(bf16), b.astype(bf16))
    ah, al = _split2(a.astype(f32))
    bh, bl = _split2(b.astype(f32))
    return (_dot(al, bh) + _dot(ah, bl)) + _dot(ah, bh)


def _mm_exact_rhs(a, b_bf16):
    ah, al = _split2(a)
    return _dot(al, b_bf16) + _dot(ah, b_bf16)


def _sigmoid(x):
    return 1.0 / (1.0 + jnp.exp(-x))


def _pack_rows(x):
    half = x.shape[1] // 2
    a = x[:, :half].astype(bf16).astype(f32)
    b = x[:, half:].astype(bf16).astype(f32)
    lo = lax.shift_right_logical(lax.bitcast_convert_type(a, u32), jnp.uint32(16))
    hi = lax.bitcast_convert_type(b, u32) & jnp.uint32(0xFFFF0000)
    return lo | hi


def _unpack_rows(w):
    lo = lax.bitcast_convert_type(lax.shift_left(w, jnp.uint32(16)), f32)
    hi = lax.bitcast_convert_type(w & jnp.uint32(0xFFFF0000), f32)
    return lo, hi


def _ada_kernel(c_ref, w_ref, b_ref, o_ref):
    c = c_ref[...]
    cond = (c * _sigmoid(c)).astype(bf16)
    o_ref[0] = _dot(cond, w_ref[0].astype(bf16)) + b_ref[0]


def _ada_mod(c_pad, w_ada, b_ada, tn=1024):
    L, D, N = w_ada.shape
    bp = c_pad.shape[0]
    return pl.pallas_call(
        _ada_kernel,
        grid=(L, N // tn),
        in_specs=[pl.BlockSpec((bp, D), lambda l, j: (0, 0)),
                  pl.BlockSpec((1, D, tn), lambda l, j: (l, 0, j)),
                  pl.BlockSpec((1, 1, tn), lambda l, j: (l, 0, j))],
        out_specs=pl.BlockSpec((1, bp, tn), lambda l, j: (l, 0, j)),
        out_shape=jax.ShapeDtypeStruct((L, bp, N), f32),
        compiler_params=_cparams(("arbitrary", "arbitrary")),
        name="ada_mod",
    )(c_pad, w_ada, b_ada.reshape(L, 1, N))


def _inproj_kernel(x_ref, sc_ref, sh_ref, g_ref, w_ref, o_ref, h_ref):
    @pl.when(pl.program_id(1) == 0)
    def _():
        x = x_ref[...]
        ms = jnp.mean(x * x, axis=-1, keepdims=True)
        y = x * lax.rsqrt(ms + NORM_EPS) * g_ref[...]
        h_ref[...] = (y * (1.0 + sc_ref[0]) + sh_ref[0]).astype(bf16)

    o_ref[...] = _dot(h_ref[...], w_ref[...]).astype(o_ref.dtype)


def _inproj(x2, mod3, g, w_bf, out_dtype, S, tm, tn, seg_sc, seg_sh):
    T, D = x2.shape
    N = w_bf.shape[1]
    tpb = S // tm
    return pl.pallas_call(
        _inproj_kernel,
        grid=(T // tm, N // tn),
        in_specs=[pl.BlockSpec((tm, D), lambda i, j: (i, 0)),
                  pl.BlockSpec((1, 1, D), lambda i, j: ((i // tpb) * N_MOD + seg_sc, 0, 0)),
                  pl.BlockSpec((1, 1, D), lambda i, j: ((i // tpb) * N_MOD + seg_sh, 0, 0)),
                  pl.BlockSpec((1, D), lambda i, j: (0, 0)),
                  pl.BlockSpec((D, tn), lambda i, j: (0, j))],
        out_specs=pl.BlockSpec((tm, tn), lambda i, j: (i, j)),
        out_shape=jax.ShapeDtypeStruct((T, N), out_dtype),
        scratch_shapes=[pltpu.VMEM((tm, D), bf16)],
        compiler_params=_cparams(("arbitrary", "arbitrary")),
        name="inproj",
    )(x2, mod3, mod3, g.reshape(1, D), w_bf)


def _attn_kernel(q_ref, k_ref, v_ref, slope_ref, lamp_ref, g_ref, o_ref,
                 q2t_ref, vt_ref, m_ref, l_ref, acc_ref, *, tq, lam_init):
    qi = pl.program_id(2)
    scale = ATT_QK_DIM ** -0.5
    slope = slope_ref[0][:, 0:1]

    @pl.when(qi == 0)
    def _():
        vt_ref[...] = v_ref[...].astype(f32).T.astype(bf16)

    qt = (q_ref[...].astype(f32) * scale).T
    dim = lax.broadcasted_iota(i32, (LANES, 1), 0)
    first = dim < ATT_QK_DIM
    q2t_ref[:, 0:tq] = jnp.where(first, qt, 0.0).astype(bf16)
    q2t_ref[:, tq:2 * tq] = jnp.where(first, 0.0, qt).astype(bf16)
    m_ref[...] = jnp.full(m_ref.shape, -jnp.inf, f32)
    l_ref[...] = jnp.zeros(l_ref.shape, f32)
    acc_ref[...] = jnp.zeros(acc_ref.shape, f32)

    kr = lax.broadcasted_iota(i32, (tq, 2 * tq), 0)
    qc = lax.broadcasted_iota(i32, (tq, 2 * tq), 1)
    causal = jnp.where(qc >= tq, qc - tq, qc) >= kr
    krow = lax.broadcasted_iota(i32, (tq, 1), 0).astype(f32)

    def step(ki, masked):
        start = pl.multiple_of(ki * tq, tq)
        kb = k_ref[pl.ds(start, tq), :]
        vtb = vt_ref[:, pl.ds(start, tq)]
        s = _dot(kb, q2t_ref[...])
        s = s + slope * (krow + (ki * tq).astype(f32))
        if masked:
            s = jnp.where(causal, s, -jnp.inf)
        m_prev = m_ref[...]
        m_new = jnp.maximum(m_prev, jnp.max(s, axis=0, keepdims=True))
        alpha = jnp.exp(m_prev - m_new)
        p = jnp.exp(s - m_new)
        l_ref[...] = alpha * l_ref[...] + jnp.sum(p, axis=0, keepdims=True)
        acc_ref[...] = alpha * acc_ref[...] + _dot(vtb, p.astype(bf16))
        m_ref[...] = m_new

    def body(ki, carry):
        step(ki, False)
        return carry

    lax.fori_loop(0, qi, body, 0)
    step(qi, True)

    lp = lamp_ref[...]
    lam = (jnp.exp(jnp.sum(lp[0:1] * lp[1:2], axis=-1, keepdims=True))
           - jnp.exp(jnp.sum(lp[2:3] * lp[3:4], axis=-1, keepdims=True)) + lam_init)
    on = acc_ref[...] * (1.0 / l_ref[...])
    o = on[:, 0:tq] - lam * on[:, tq:2 * tq]
    o = o * lax.rsqrt(jnp.mean(o * o, axis=0, keepdims=True) + ATT_SUBLN_EPS)
    o = o * g_ref[...] * (1.0 - lam_init)
    o_ref[...] = o.T.astype(o_ref.dtype)


def _attention(att, slopes, lamp, subln_g, B, S, H, lam_init, tq):
    T = att.shape[0]
    nq = S // tq
    kern = functools.partial(_attn_kernel, tq=tq, lam_init=lam_init)
    return pl.pallas_call(
        kern,
        grid=(B, H, nq),
        in_specs=[pl.BlockSpec((tq, LANES), lambda b, h, q: (b * nq + q, h)),
                  pl.BlockSpec((S, LANES), lambda b, h, q: (b, H + h)),
                  pl.BlockSpec((S, LANES), lambda b, h, q: (b, 2 * H + h)),
                  pl.BlockSpec((1, 1, LANES), lambda b, h, q: (h, 0, 0)),
                  pl.BlockSpec((4, ATT_QK_DIM), lambda b, h, q: (0, 0)),
                  pl.BlockSpec((ATT_V_DIM, 1), lambda b, h, q: (0, 0))],
        out_specs=pl.BlockSpec((tq, LANES), lambda b, h, q: (b * nq + q, h)),
        out_shape=jax.ShapeDtypeStruct((T, H * ATT_V_DIM), bf16),
        scratch_shapes=[pltpu.VMEM((LANES, 2 * tq), bf16),
                        pltpu.VMEM((LANES, S), bf16),
                        pltpu.VMEM((1, 2 * tq), f32),
                        pltpu.VMEM((1, 2 * tq), f32),
                        pltpu.VMEM((LANES, 2 * tq), f32)],
        compiler_params=_cparams(("arbitrary", "arbitrary", "arbitrary")),
        name="diff_attention",
    )(att, att, att, slopes, lamp, subln_g.reshape(ATT_V_DIM, 1))


def _head_sums(x, ind, indt):
    s = _mm_exact_rhs(x, ind)
    return _mm_exact_rhs(s, indt)


def _rwkv_prep_kernel(*refs, W, has_vres):
    if has_vres:
        (f_ref, mu_ref, w0_ref, w2_ref, a0_ref, a2_ref, g2_ref, kk_ref, ka_ref, ind_ref, indt_ref,
         vf_ref, v0_ref, v1_ref, v2_ref,
         r_o, lw_o, kh_o, v_o, kn_o, kb_o, g_o, carry_ref) = refs
    else:
        (f_ref, mu_ref, w0_ref, w2_ref, a0_ref, a2_ref, g2_ref, kk_ref, ka_ref, ind_ref, indt_ref,
         r_o, lw_o, kh_o, v_o, kn_o, kb_o, g_o, carry_ref) = refs

    ti = pl.program_id(1)
    h = f_ref[...]
    tm = h.shape[0]

    @pl.when(ti == 0)
    def _():
        carry_ref[...] = jnp.zeros(carry_ref.shape, f32)

    rolled = pltpu.roll(h, 1, axis=0)
    row = lax.broadcasted_iota(i32, (tm, 1), 0)
    prev = jnp.where(row == 0, carry_ref[...], rolled)
    carry_ref[...] = h[tm - 1:tm, :]
    feats = h + (prev - h) * mu_ref[...]

    r = feats[:, 0:W]
    k = feats[:, W:2 * W]
    v = feats[:, 2 * W:3 * W]
    wa = feats[:, 3 * W:3 * W + LANES]
    g_lo = feats[:, 3 * W + LANES:3 * W + 2 * LANES]

    w = w0_ref[...] + _mm(jnp.tanh(wa), w2_ref[...], passes=3)
    lw_o[...] = -math.exp(-0.5) * _sigmoid(w)
    a = _sigmoid(a0_ref[...] + _mm(wa, a2_ref[...], passes=3))
    g_o[...] = _mm(_sigmoid(g_lo), g2_ref[...]).astype(g_o.dtype)

    if has_vres:
        mix = _sigmoid(v0_ref[...] + _mm(_mm(v, v1_ref[...]), v2_ref[...]))
        v = v + (vf_ref[...].astype(f32) - v) * mix

    kk = k * kk_ref[...]
    ss = _head_sums(kk * kk, ind_ref[...], indt_ref[...])
    kk = kk / jnp.maximum(jnp.sqrt(ss), 1e-12)
    r_o[...] = r.astype(r_o.dtype)
    kh_o[...] = (k * (1.0 + (a - 1.0) * ka_ref[...])).astype(kh_o.dtype)
    v_o[...] = v.astype(v_o.dtype)
    kn_o[...] = kk.astype(kn_o.dtype)
    kb_o[...] = (kk * a).astype(kb_o.dtype)


def _rwkv_prep(feats, prm, vfirst, B, S, W, tm):
    T, COLS = feats.shape
    tpb = S // tm
    has_vres = vfirst is not None
    row = lambda n: pl.BlockSpec((1, n), lambda b, t: (0, 0))
    full = lambda a: pl.BlockSpec(a.shape, lambda b, t: (0, 0))
    tile = pl.BlockSpec((tm, W), lambda b, t: (b * tpb + t, 0))
    args = [feats, prm["mu"], prm["w0"], prm["w2p"], prm["a0"], prm["a2p"], prm["g2"], prm["k_k"], prm["k_a"],
            prm["ind"], prm["indt"]]
    specs = [pl.BlockSpec((tm, COLS), lambda b, t: (b * tpb + t, 0)), row(COLS), row(W), full(prm["w2p"]),
             row(W), full(prm["a2p"]), full(prm["g2"]), row(W), row(W), full(prm["ind"]), full(prm["indt"])]
    if has_vres:
        args += [vfirst, prm["v0"], prm["v1p"], prm["v2p"]]
        specs += [tile, row(W), full(prm["v1p"]), full(prm["v2p"])]
    out_dtypes = (bf16, f32, bf16, bf16, bf16, bf16, bf16)
    kern = functools.partial(_rwkv_prep_kernel, W=W, has_vres=has_vres)
    return pl.pallas_call(
        kern,
        grid=(B, tpb),
        in_specs=specs,
        out_specs=[tile] * 7,
        out_shape=[jax.ShapeDtypeStruct((T, W), dt) for dt in out_dtypes],
        scratch_shapes=[pltpu.VMEM((1, COLS), f32)],
        compiler_params=_cparams(("arbitrary", "arbitrary")),
        name="rwkv_prep",
    )(*args)


def _wkv_chunks(rs, lws, ks, vs, kns, kbs, passes):
    C = CHUNK
    P2 = 2 * C
    n = range(len(rs))
    ri = lax.broadcasted_iota(i32, (C, C), 0)
    ci = lax.broadcasted_iota(i32, (C, C), 1)
    tri = (ci <= ri).astype(bf16)
    lane = lax.broadcasted_iota(i32, (1, LANES), 1)
    m0 = (lane < RWKV_HEAD).astype(f32)
    m1 = 1.0 - m0
    rr = lax.broadcasted_iota(i32, (P2, P2), 0)
    cc = lax.broadcasted_iota(i32, (P2, P2), 1)
    same = jnp.where(rr >= C, 1, 0) == jnp.where(cc >= C, 1, 0)
    strict = same & (cc < rr)
    incl = same & (cc <= rr)
    incl2 = jnp.concatenate([incl, incl], axis=1)
    eye = (rr == cc).astype(f32)
    zeros_p = jnp.zeros((P2, LANES), f32)
    zeros_c = jnp.zeros((C, LANES), f32)
    stack = lambda x: jnp.concatenate([x * m0, x * m1], axis=0)
    fold = lambda x: x[0:C] + x[C:2 * C]

    def cumsum(lw):
        h1 = lw.astype(bf16)
        r1 = lw - h1.astype(f32)
        h2 = r1.astype(bf16)
        h3 = (r1 - h2.astype(f32)).astype(bf16)
        return (_dot(tri, h3) + _dot(tri, h2)) + _dot(tri, h1)

    cum = [cumsum(lws[j]) for j in n]
    cum_c = [cum[j][C - 1:C, :] for j in n]
    at = [-kns[j] * jnp.exp(cum[j] - lws[j]) for j in n]
    rt = [rs[j] * jnp.exp(cum[j]) for j in n]
    einv = [jnp.exp(-cum[j]) for j in n]
    bt = [kbs[j] * einv[j] for j in n]
    kt = [ks[j] * einv[j] for j in n]
    eh = [jnp.exp(cum_c[j] - cum[j]) for j in n]
    bh = [kbs[j] * eh[j] for j in n]
    kh = [ks[j] * eh[j] for j in n]
    w_c = [jnp.exp(cum_c[j]) for j in n]
    abd = [stack(at[j]) for j in n]
    vst = [stack(vs[j]) for j in n]
    lhs = [jnp.concatenate([abd[j], stack(rt[j])], axis=0) for j in n]
    rhs = [jnp.concatenate([stack(bt[j]), stack(kt[j])], axis=0) for j in n]
    gram = [_mm_nt(lhs[j], rhs[j], passes) for j in n]
    lab = [jnp.where(strict, gram[j][0:P2, 0:P2], 0.0) for j in n]
    lak = [jnp.where(strict, gram[j][0:P2, P2:2 * P2], 0.0) for j in n]
    mrbk = [jnp.where(incl2, gram[j][P2:2 * P2, :], 0.0) for j in n]

    x0 = [_mm(lak[j], vst[j], passes) for j in n]
    tinv = [eye + lab[j] for j in n]
    lp = lab
    for _ in range(int(math.log2(C)) - 1):
        lp = [_mm(lp[j], lp[j], passes) for j in n]
        tinv = [tinv[j] + _mm(lp[j], tinv[j], passes) for j in n]

    ta = [_mm(tinv[j], jnp.concatenate([abd[j], x0[j]], axis=1), passes) for j in n]
    rhs2 = [jnp.concatenate([ta[j], jnp.concatenate([zeros_p, vst[j]], axis=1)], axis=0) for j in n]
    z = [_mm(mrbk[j], rhs2[j], passes) for j in n]
    r2 = [rt[j] + fold(z[j][:, 0:LANES]) for j in n]
    y0 = [fold(z[j][:, LANES:2 * LANES]) for j in n]
    lhs3t = [jnp.concatenate([bh[j], kh[j]], axis=0).T for j in n]
    rhs3 = [jnp.concatenate([fold(ta[j]), jnp.concatenate([zeros_c, vs[j]], axis=1)], axis=0) for j in n]
    wmat = [_mm(lhs3t[j], rhs3[j], passes) for j in n]
    mmat = [jnp.where(same, wmat[j][:, 0:LANES], 0.0) + eye * w_c[j] for j in n]
    g0 = [jnp.where(same, wmat[j][:, LANES:2 * LANES], 0.0) for j in n]
    return r2, y0, mmat, g0


def _mm_nt(a, b, passes):
    if passes == 1:
        return _dot_nt(a.astype(bf16), b.astype(bf16))
    ah, al = _split2(a)
    bh, bl = _split2(b)
    return (_dot_nt(al, bh) + _dot_nt(ah, bl)) + _dot_nt(ah, bh)


def _wkv_intra_kernel(r_ref, lw_ref, k_ref, v_ref, kn_ref, kb_ref, r2_o, y0_o, m_o, g_o, *, nc, passes):
    C = CHUNK
    sls = [slice(c * C, (c + 1) * C) for c in range(nc)]
    take = lambda ref: [ref[sl, :].astype(f32) for sl in sls]
    r2, y0, mmat, g0 = _wkv_chunks(take(r_ref), take(lw_ref), take(k_ref), take(v_ref),
                                   take(kn_ref), take(kb_ref), passes)
    for c, sl in enumerate(sls):
        r2_o[sl, :] = r2[c].astype(r2_o.dtype)
        y0_o[sl, :] = y0[c].astype(y0_o.dtype)
        m_o[0, c] = mmat[c].astype(m_o.dtype)
        g_o[0, c] = g0[c]


def _wkv_intra(r, lw, kh, v, kn, kb, nc, passes):
    T, W = r.shape
    npair = W // LANES
    rows = nc * CHUNK
    tile = pl.BlockSpec((rows, LANES), lambda p, i: (i, p))
    mat = pl.BlockSpec((1, nc, LANES, LANES), lambda p, i: (p, i, 0, 0))
    kern = functools.partial(_wkv_intra_kernel, nc=nc, passes=passes)
    return pl.pallas_call(
        kern,
        grid=(npair, T // rows),
        in_specs=[tile] * 6,
        out_specs=[tile, tile, mat, mat],
        out_shape=[jax.ShapeDtypeStruct((T, W), bf16), jax.ShapeDtypeStruct((T, W), bf16),
                   jax.ShapeDtypeStruct((npair, T // CHUNK, LANES, LANES), bf16),
                   jax.ShapeDtypeStruct((npair, T // CHUNK, LANES, LANES), f32)],
        compiler_params=_cparams(("arbitrary", "arbitrary")),
        name="wkv_intra",
    )(r, lw, kh, v, kn, kb)


def _wkv_state_kernel(r2_ref, y0_ref, m_ref, g0_ref, r_ref, kh_ref, v_ref, g_ref,
                      lnw_ref, lnb_ref, rk_ref, o_ref, st_ref, y_ref, *, pg, cb, passes):
    C = CHUNK

    @pl.when(pl.program_id(2) == 0)
    def _():
        st_ref[...] = jnp.zeros(st_ref.shape, f32)

    pairs = range(pg)
    lanes = [slice(p * LANES, (p + 1) * LANES) for p in pairs]
    st = [st_ref[p] for p in pairs]
    for c in range(cb):
        rows = slice(c * C, (c + 1) * C)
        for p in pairs:
            y_ref[rows, lanes[p]] = (_mm(r2_ref[rows, lanes[p]], st[p], passes)
                                     + y0_ref[rows, lanes[p]].astype(f32))
        st = [_mm(m_ref[p, c], st[p], passes) + g0_ref[p, c] for p in pairs]
    for p in pairs:
        st_ref[p] = st[p]

    rr = lax.broadcasted_iota(i32, (LANES, LANES), 0)
    cc = lax.broadcasted_iota(i32, (LANES, LANES), 1)
    ones_bd = (jnp.where(rr >= RWKV_HEAD, 1, 0) == jnp.where(cc >= RWKV_HEAD, 1, 0)).astype(bf16)
    for p in pairs:
        y = y_ref[:, lanes[p]]
        mu = _mm_exact_rhs(y, ones_bd) * (1.0 / RWKV_HEAD)
        d = y - mu
        var = _mm_exact_rhs(d * d, ones_bd) * (1.0 / RWKV_HEAD)
        yn = d * lax.rsqrt(var + RWKV_GN_EPS) * lnw_ref[:, lanes[p]] + lnb_ref[:, lanes[p]]
        rk = r_ref[:, lanes[p]].astype(f32) * kh_ref[:, lanes[p]].astype(f32) * rk_ref[:, lanes[p]]
        bonus = _mm_exact_rhs(rk, ones_bd) * v_ref[:, lanes[p]]
        o_ref[:, lanes[p]] = ((yn + bonus) * g_ref[:, lanes[p]]).astype(o_ref.dtype)


def _wkv_state(r2, y0, mm, g0, r, kh, v, g, lnw, lnb, rk, B, S, pg, cb, passes):
    T, W = r.shape
    npair = W // LANES
    rows = cb * CHUNK
    steps = S // rows
    seq = pl.BlockSpec((rows, pg * LANES), lambda b, q, c: (b * steps + c, q))
    mat = pl.BlockSpec((pg, cb, LANES, LANES), lambda b, q, c: (q, b * steps + c, 0, 0))
    prow = pl.BlockSpec((1, pg * LANES), lambda b, q, c: (0, q))
    kern = functools.partial(_wkv_state_kernel, pg=pg, cb=cb, passes=passes)
    return pl.pallas_call(
        kern,
        grid=(B, npair // pg, steps),
        in_specs=[seq, seq, mat, mat, seq, seq, seq, seq, prow, prow, prow],
        out_specs=seq,
        out_shape=jax.ShapeDtypeStruct((T, W), bf16),
        scratch_shapes=[pltpu.VMEM((pg, LANES, LANES), f32), pltpu.VMEM((rows, pg * LANES), f32)],
        compiler_params=_cparams(("arbitrary", "arbitrary", "arbitrary")),
        name="wkv_state",
    )(r2, y0, mm, g0, r, kh, v, g, lnw, lnb, rk)


def _postmix_kernel(oa_ref, orw_ref, wa_ref, wr_ref, x_ref, ga_ref, gpost_ref, gpre_ref, sc_ref, sh_ref, wrt_ref,
                    x1_o, h2_o, lg_o):
    mixed = _dot(oa_ref[...], wa_ref[...]) + _dot(orw_ref[...], wr_ref[...])
    ms = jnp.mean(mixed * mixed, axis=-1, keepdims=True)
    x1 = x_ref[...] + ga_ref[0] * (mixed * lax.rsqrt(ms + NORM_EPS) * gpost_ref[...])
    x1_o[...] = x1
    ms1 = jnp.mean(x1 * x1, axis=-1, keepdims=True)
    h2 = (x1 * lax.rsqrt(ms1 + NORM_EPS) * gpre_ref[...]) * (1.0 + sc_ref[0]) + sh_ref[0]
    h2_o[...] = _pack_rows(h2)
    lg_o[...] = _mm_nt(wrt_ref[...], h2, 3)


def _postmix(o_att, o_rwkv, w_out_a, w_out_r, x2, mod3, g_post, g_pre, w_rt, S, tm):
    T, D = x2.shape
    WA = o_att.shape[1]
    WR = o_rwkv.shape[1]
    E = w_rt.shape[0]
    tpb = S // tm
    modspec = lambda seg: pl.BlockSpec((1, 1, D), lambda i: ((i // tpb) * N_MOD + seg, 0, 0))
    tile = pl.BlockSpec((tm, D), lambda i: (i, 0))
    return pl.pallas_call(
        _postmix_kernel,
        grid=(T // tm,),
        in_specs=[pl.BlockSpec((tm, WA), lambda i: (i, 0)),
                  pl.BlockSpec((tm, WR), lambda i: (i, 0)),
                  pl.BlockSpec((WA, D), lambda i: (0, 0), pipeline_mode=pl.Buffered(1)),
                  pl.BlockSpec((WR, D), lambda i: (0, 0), pipeline_mode=pl.Buffered(1)),
                  tile, modspec(2),
                  pl.BlockSpec((1, D), lambda i: (0, 0)),
                  pl.BlockSpec((1, D), lambda i: (0, 0)),
                  modspec(4), modspec(3),
                  pl.BlockSpec((E, D), lambda i: (0, 0))],
        out_specs=[tile, pl.BlockSpec((tm, D // 2), lambda i: (i, 0)),
                   pl.BlockSpec((E, tm), lambda i: (0, i))],
        out_shape=[jax.ShapeDtypeStruct((T, D), f32), jax.ShapeDtypeStruct((T, D // 2), u32),
                   jax.ShapeDtypeStruct((E, T), f32)],
        compiler_params=_cparams(("arbitrary",)),
        name="postmix",
    )(o_att, o_rwkv, w_out_a, w_out_r, x2, mod3, g_post.reshape(1, D), g_pre.reshape(1, D), mod3, mod3, w_rt)


def _first_max(x, iota, n):
    mx = jnp.max(x, axis=0, keepdims=True)
    idx = jnp.min(jnp.where(x == mx, iota, n), axis=0, keepdims=True)
    return mx, idx


def _router_kernel(lg_ref, bias_ref, eidx_o, gate_o, rank_o, cnt_o, cnt_ref):
    E = N_EXPERTS
    G = N_GROUPS
    per = E // G
    tm = lg_ref.shape[1]

    @pl.when(pl.program_id(0) == 0)
    def _():
        cnt_ref[...] = jnp.zeros(cnt_ref.shape, f32)

    scores = _sigmoid(lg_ref[...])
    biased = scores + bias_ref[...]
    neg = -jnp.inf

    iota_p = lax.broadcasted_iota(i32, (per, tm), 0).astype(f32)
    gs = []
    for g in range(G):
        xg = biased[g * per:(g + 1) * per, :]
        m1, i1 = _first_max(xg, iota_p, per)
        m2 = jnp.max(jnp.where(iota_p == i1, neg, xg), axis=0, keepdims=True)
        gs.append(m1 + m2)
    gsc = jnp.concatenate(gs, axis=0)
    iota_g = lax.broadcasted_iota(i32, (G, tm), 0).astype(f32)
    gsel = jnp.zeros((G, tm), f32)
    for _ in range(TOPK_GROUPS):
        _, gi = _first_max(gsc, iota_g, G)
        hit = iota_g == gi
        gsel = jnp.where(hit, 1.0, gsel)
        gsc = jnp.where(hit, neg, gsc)
    masked = jnp.concatenate(
        [jnp.where(gsel[g:g + 1, :] > 0.0, biased[g * per:(g + 1) * per, :], neg) for g in range(G)], axis=0)

    iota_e = lax.broadcasted_iota(i32, (E, tm), 0).astype(f32)
    sel = jnp.zeros((E, tm), f32)
    idxs, vals = [], []
    for _ in range(TOP_K):
        _, ei = _first_max(masked, iota_e, E)
        hit = iota_e == ei
        idxs.append(ei)
        vals.append(jnp.sum(jnp.where(hit, scores, 0.0), axis=0, keepdims=True))
        sel = jnp.where(hit, 1.0, sel)
        masked = jnp.where(hit, neg, masked)
    tot = vals[0]
    for vv in vals[1:]:
        tot = tot + vv
    eidx_o[...] = jnp.concatenate(idxs, axis=0).astype(i32)
    gate_o[...] = jnp.concatenate([vv / tot * ROUTED_SCALE for vv in vals], axis=0)

    rr = lax.broadcasted_iota(i32, (tm, tm), 0)
    cc = lax.broadcasted_iota(i32, (tm, tm), 1)
    before = (rr < cc).astype(bf16)
    pos = _dot(sel.astype(bf16), before)
    rank_o[...] = jnp.concatenate(
        [jnp.sum(jnp.where(iota_e == ei, pos, 0.0), axis=0, keepdims=True) for ei in idxs], axis=0)
    lane = lax.broadcasted_iota(i32, cnt_ref.shape, 1)
    cnt_ref[...] = jnp.where(lane == pl.program_id(0), jnp.sum(sel, axis=1, keepdims=True), cnt_ref[...])
    cnt_o[...] = cnt_ref[...].astype(i32)


def _router(logits_t, bias, tm):
    E, T = logits_t.shape
    assert T // tm <= LANES
    k_tile = pl.BlockSpec((TOP_K, tm), lambda i: (0, i))
    return pl.pallas_call(
        _router_kernel,
        grid=(T // tm,),
        in_specs=[pl.BlockSpec((E, tm), lambda i: (0, i)),
                  pl.BlockSpec((E, 1), lambda i: (0, 0))],
        out_specs=[k_tile, k_tile, k_tile, pl.BlockSpec((E, LANES), lambda i: (0, 0))],
        out_shape=[jax.ShapeDtypeStruct((TOP_K, T), i32), jax.ShapeDtypeStruct((TOP_K, T), f32),
                   jax.ShapeDtypeStruct((TOP_K, T), f32), jax.ShapeDtypeStruct((E, LANES), i32)],
        scratch_shapes=[pltpu.VMEM((E, LANES), f32)],
        compiler_params=_cparams(("arbitrary",)),
        name="router",
    )(logits_t, bias.reshape(E, 1))


def _row_copy(src_ref, s, dst_ref, d, sem):
    return pltpu.make_async_copy(src_ref.at[pl.ds(s, 1), :], dst_ref.at[pl.ds(d, 1), :], sem)


def _zero_fill(cnt_ref, pstart_ref, nused_ref, z_ref, xs_out, sem, blk, nblk, start):
    def act(cp):
        if start:
            cp.start()
        else:
            cp.wait()

    def per_expert(e, carry):
        c = cnt_ref[e]
        base = pstart_ref[e] + c
        npad = (blk - (c & (blk - 1))) & (blk - 1)
        head = (-base) & (SUBLANES - 1)

        def one_row(j, carry2):
            act(_row_copy(z_ref, 0, xs_out, base + j, sem))
            return carry2

        lax.fori_loop(0, head, one_row, 0)
        rem = npad - head
        aligned = base + head
        p = blk // 2
        while p >= SUBLANES:
            off = pl.multiple_of(aligned + (rem & ~(2 * p - 1)), SUBLANES)

            @pl.when((rem & p) != 0)
            def _(p=p, off=off):
                act(pltpu.make_async_copy(z_ref.at[pl.ds(0, p), :], xs_out.at[pl.ds(off, p), :], sem))

            p //= 2
        return carry

    lax.fori_loop(0, N_EXPERTS, per_expert, 0)

    def per_block(b, carry):
        act(pltpu.make_async_copy(z_ref, xs_out.at[pl.ds(pl.multiple_of(b * blk, blk), blk), :], sem))
        return carry

    lax.fori_loop(nused_ref[0], nblk, per_block, 0)


def _seg_copies(ssrc_ref, sdst_ref, sn_ref, tile, buf_ref, hbm_ref, sem, tm, to_hbm, start):
    def act(cp):
        if start:
            cp.start()
        else:
            cp.wait()

    def per_expert(e, carry):
        idx = tile * N_EXPERTS + e
        n = sn_ref[idx]
        src = ssrc_ref[idx]
        dst = sdst_ref[idx]
        p = tm
        while p >= SUBLANES:
            off = n & ~(2 * p - 1)

            @pl.when((n & p) != 0)
            def _(p=p, off=off):
                v = buf_ref.at[pl.ds(pl.multiple_of(src + off, SUBLANES), p), :]
                h = hbm_ref.at[pl.ds(pl.multiple_of(dst + off, SUBLANES), p), :]
                act(pltpu.make_async_copy(v, h, sem) if to_hbm else pltpu.make_async_copy(h, v, sem))

            p //= 2
        return carry

    lax.fori_loop(0, N_EXPERTS, per_expert, 0)


def _seg_wait(total_rows, buf_ref, hbm_ref, sem, to_hbm):
    p = 1 << (buf_ref.shape[0].bit_length() - 1)
    while p >= SUBLANES:
        @pl.when((total_rows & p) != 0)
        def _(p=p):
            v = buf_ref.at[pl.ds(0, p), :]
            h = hbm_ref.at[pl.ds(0, p), :]
            (pltpu.make_async_copy(v, h, sem) if to_hbm else pltpu.make_async_copy(h, v, sem)).wait()

        p //= 2


SORT_CHUNK = 512


def _dispatch_kernel(ssrc_ref, sdst_ref, sn_ref, stot_ref, cnt_ref, pstart_ref, nused_ref, lp_ref, h_ref,
                     wsg_ref, wsu_ref, wsd_ref, xs_out, ysh_o, sb_ref, z_ref, sems, sem_z, *, blk, nblk):
    i = pl.program_id(0)
    last = pl.num_programs(0) - 1
    tm = h_ref.shape[0]
    lmax = sb_ref.shape[1]
    slot = i % 2
    seg = functools.partial(_seg_copies, ssrc_ref, sdst_ref, sn_ref, hbm_ref=xs_out, tm=tm, to_hbm=True)

    def seg_wait(tile, s):
        _seg_wait(stot_ref[tile], sb_ref.at[s], xs_out, sems.at[s], True)

    @pl.when(i >= 2)
    def _():
        seg_wait(i - 2, slot)

    hlo, hhi = _unpack_rows(h_ref[...])
    hb = jnp.concatenate([hlo.astype(bf16), hhi.astype(bf16)], axis=1)
    lpv = lp_ref[...]
    half = hb.shape[1] // 2
    for c in range(lmax // SORT_CHUNK):
        jj = (lax.broadcasted_iota(i32, (SORT_CHUNK, tm), 0) + c * SORT_CHUNK).astype(f32)
        onehot = jnp.zeros((SORT_CHUNK, tm), f32)
        for k in range(TOP_K):
            onehot = jnp.where(jj == lpv[k:k + 1, :], 1.0, onehot)
        rows = _dot(onehot.astype(bf16), hb)
        lo = lax.shift_right_logical(lax.bitcast_convert_type(rows[:, :half], u32), jnp.uint32(16))
        hi = lax.bitcast_convert_type(rows[:, half:], u32) & jnp.uint32(0xFFFF0000)
        sb_ref[slot, c * SORT_CHUNK:(c + 1) * SORT_CHUNK, :] = lo | hi

    seg(i, buf_ref=sb_ref.at[slot], sem=sems.at[slot], start=True)

    gt = _dot(hb, wsg_ref[...])
    up = _dot(hb, wsu_ref[...])
    ysh_o[...] = _dot(((gt * _sigmoid(gt)) * up).astype(bf16), wsd_ref[...]).astype(ysh_o.dtype)

    @pl.when(i == last)
    def _():
        z_ref[...] = jnp.zeros(z_ref.shape, u32)
        _zero_fill(cnt_ref, pstart_ref, nused_ref, z_ref, xs_out, sem_z, blk, nblk, True)
        _zero_fill(cnt_ref, pstart_ref, nused_ref, z_ref, xs_out, sem_z, blk, nblk, False)
        seg_wait(i, slot)

        @pl.when(i >= 1)
        def _():
            seg_wait(i - 1, 1 - slot)


def _dispatch(seg_src, seg_dst, seg_n, seg_tot, counts, pad_start, nused, lp_t, h2p, wsg, wsu, wsd,
              P, blk, tm, lmax):
    T, DW = h2p.shape
    D, DS = wsg.shape
    assert blk & (blk - 1) == 0 and tm & (tm - 1) == 0 and lmax % SORT_CHUNK == 0
    kern = functools.partial(_dispatch_kernel, blk=blk, nblk=P // blk)
    const = lambda shape: pl.BlockSpec(shape, lambda i, *_: (0, 0))
    grid_spec = pltpu.PrefetchScalarGridSpec(
        num_scalar_prefetch=7,
        grid=(T // tm,),
        in_specs=[pl.BlockSpec((TOP_K, tm), lambda i, *_: (0, i)),
                  pl.BlockSpec((tm, DW), lambda i, *_: (i, 0)),
                  const((D, DS)), const((D, DS)), const((DS, D))],
        out_specs=[pl.BlockSpec(memory_space=pl.ANY), pl.BlockSpec((tm, D), lambda i, *_: (i, 0))],
        scratch_shapes=[pltpu.VMEM((2, lmax, DW), u32), pltpu.VMEM((blk, DW), u32),
                        pltpu.SemaphoreType.DMA((2,)), pltpu.SemaphoreType.DMA],
    )
    return pl.pallas_call(
        kern,
        grid_spec=grid_spec,
        out_shape=[jax.ShapeDtypeStruct((P, DW), u32), jax.ShapeDtypeStruct((T, D), bf16)],
        compiler_params=_cparams(("arbitrary",)),
        name="moe_dispatch",
    )(seg_src, seg_dst, seg_n, seg_tot, counts, pad_start, nused, lp_t, h2p, wsg, wsu, wsd)


def _experts_kernel(blk_e_ref, nxt_e_ref, nused_ref, xs_ref, wg_hbm, wu_hbm, wd_hbm, ys_ref,
                    wg_f, wu_f, wd_f, wg_s, wu_s, wd_s, sems, *, layer):
    i = pl.program_id(0)
    e = blk_e_ref[i]
    changed = jnp.logical_or(i == 0, e != blk_e_ref[jnp.maximum(i - 1, 0)])

    def weight_copies(ex):
        return (pltpu.make_async_copy(wg_hbm.at[layer, ex], wg_f, sems.at[0]),
                pltpu.make_async_copy(wu_hbm.at[layer, ex], wu_f, sems.at[1]),
                pltpu.make_async_copy(wd_hbm.at[layer, ex], wd_f, sems.at[2]))

    @pl.when(i == 0)
    def _():
        for cp in weight_copies(e):
            cp.start()

    @pl.when(changed)
    def _():
        for cp in weight_copies(e):
            cp.wait()
        for src, dst in ((wg_f, wg_s), (wu_f, wu_s), (wd_f, wd_s)):
            rows = src.shape[0] // 8
            for c in range(8):
                dst[c * rows:(c + 1) * rows, :] = src[c * rows:(c + 1) * rows, :].astype(bf16)
        nxt = nxt_e_ref[i]

        @pl.when(nxt >= 0)
        def _():
            for cp in weight_copies(nxt):
                cp.start()

    @pl.when(i < nused_ref[0])
    def _():
        lo, hi = _unpack_rows(xs_ref[...])
        x = jnp.concatenate([lo.astype(bf16), hi.astype(bf16)], axis=1)
        gt = _dot(x, wg_s[...])
        up = _dot(x, wu_s[...])
        hmid = (gt * _sigmoid(gt)) * up
        ys_ref[...] = _pack_rows(_dot(hmid.astype(bf16), wd_s[...]))

    @pl.when(i >= nused_ref[0])
    def _():
        ys_ref[...] = jnp.zeros(ys_ref.shape, u32)


def _experts(blk_e, nxt_e, nused, xs, w_gate, w_up, w_down, layer, blk):
    P, DW = xs.shape
    D, DE = w_gate.shape[-2:]
    nblk = P // blk
    row_idx = lambda i, be, nx, nu: (jnp.minimum(i, nu[0] - 1), 0)
    hbm = pl.BlockSpec(memory_space=pl.ANY)
    grid_spec = pltpu.PrefetchScalarGridSpec(
        num_scalar_prefetch=3,
        grid=(nblk,),
        in_specs=[pl.BlockSpec((blk, DW), row_idx), hbm, hbm, hbm],
        out_specs=pl.BlockSpec((blk, DW), lambda i, be, nx, nu: (i, 0)),
        scratch_shapes=[pltpu.VMEM((D, DE), f32), pltpu.VMEM((D, DE), f32), pltpu.VMEM((DE, D), f32),
                        pltpu.VMEM((D, DE), bf16), pltpu.VMEM((D, DE), bf16), pltpu.VMEM((DE, D), bf16),
                        pltpu.SemaphoreType.DMA((3,))],
    )
    return pl.pallas_call(
        functools.partial(_experts_kernel, layer=layer),
        grid_spec=grid_spec,
        out_shape=jax.ShapeDtypeStruct((P, DW), u32),
        compiler_params=_cparams(("arbitrary",)),
        name="moe_experts",
    )(blk_e, nxt_e, nused, xs, w_gate, w_up, w_down)


def _combine_kernel(ssrc_ref, sdst_ref, sn_ref, stot_ref, lp_ref, gate_ref, ys_hbm, ysh_ref, x1_ref, gf_ref,
                    gpost_ref, x2_o, yb_ref, sems):
    i = pl.program_id(0)
    n = pl.num_programs(0)
    tm = x1_ref.shape[0]
    lmax = yb_ref.shape[1]
    slot = i % 2
    seg = functools.partial(_seg_copies, ssrc_ref, sdst_ref, sn_ref, hbm_ref=ys_hbm, tm=tm, to_hbm=False)

    @pl.when(i == 0)
    def _():
        yb_ref[...] = jnp.zeros(yb_ref.shape, u32)
        seg(i, buf_ref=yb_ref.at[slot], sem=sems.at[slot], start=True)

    @pl.when(i + 1 < n)
    def _():
        seg(i + 1, buf_ref=yb_ref.at[1 - slot], sem=sems.at[1 - slot], start=True)

    _seg_wait(stot_ref[i], yb_ref.at[slot], ys_hbm, sems.at[slot], False)

    lp = lp_ref[...]
    gate = gate_ref[...]
    ysh = ysh_ref[...].astype(f32)
    half = ysh.shape[1] // 2
    lo = ysh[:, :half]
    hi = ysh[:, half:]
    for c in range(lmax // SORT_CHUNK):
        jl = (lax.broadcasted_iota(i32, (tm, SORT_CHUNK), 1) + c * SORT_CHUNK).astype(f32)
        g = jnp.zeros((tm, SORT_CHUNK), f32)
        for k in range(TOP_K):
            g = jnp.where(jl == lp[:, k:k + 1], gate[:, k:k + 1], g)
        gb = g.astype(bf16)
        a, b = _unpack_rows(yb_ref[slot, c * SORT_CHUNK:(c + 1) * SORT_CHUNK, :])
        lo = lo + _dot(gb, a.astype(bf16))
        hi = hi + _dot(gb, b.astype(bf16))
    y = jnp.concatenate([lo, hi], axis=1)
    ms = jnp.mean(y * y, axis=-1, keepdims=True)
    x2_o[...] = x1_ref[...] + gf_ref[0] * (y * lax.rsqrt(ms + NORM_EPS) * gpost_ref[...])


def _combine(seg_src, seg_dst, seg_n, seg_tot, lp_tk, gate_tk, ys, ysh, x1, mod3, g_post, S, tm, lmax):
    T, D = x1.shape
    DW = ys.shape[1]
    tpb = S // tm
    tile = pl.BlockSpec((tm, D), lambda i, *_: (i, 0))
    ktile = pl.BlockSpec((tm, TOP_K), lambda i, *_: (i, 0))
    grid_spec = pltpu.PrefetchScalarGridSpec(
        num_scalar_prefetch=4,
        grid=(T // tm,),
        in_specs=[ktile, ktile,
                  pl.BlockSpec(memory_space=pl.ANY),
                  tile, tile,
                  pl.BlockSpec((1, 1, D), lambda i, *_: ((i // tpb) * N_MOD + 5, 0, 0)),
                  pl.BlockSpec((1, D), lambda i, *_: (0, 0))],
        out_specs=tile,
        scratch_shapes=[pltpu.VMEM((2, lmax, DW), u32), pltpu.SemaphoreType.DMA((2,))],
    )
    return pl.pallas_call(
        _combine_kernel,
        grid_spec=grid_spec,
        out_shape=jax.ShapeDtypeStruct((T, D), f32),
        compiler_params=_cparams(("arbitrary",)),
        name="moe_combine",
    )(seg_src, seg_dst, seg_n, seg_tot, lp_tk, gate_tk, ys, ysh, x1, mod3, g_post.reshape(1, D))


def _tile(n, pref):
    t = min(n, pref)
    assert n % t == 0, (n, t)
    return t


def _layer(i, x2, mod3, p, wexp, v_first, B, S, cfg):
    T, D = x2.shape
    H = (D // 2) // ATT_V_DIM
    W = D - D // 2
    att_cols = 2 * H * 2 * ATT_QK_DIM + H * ATT_V_DIM
    lam_init = 0.8 - 0.6 * math.exp(-0.3 * i)

    w_in_bf = p["w_in"].astype(bf16)
    att = _inproj(x2, mod3, p["g_pre_mix"], w_in_bf[:, :att_cols], bf16, S,
                  _tile(S, cfg["tm_in"]), cfg["tn_att"], 1, 0)
    feats = _inproj(x2, mod3, p["g_pre_mix"], w_in_bf[:, att_cols:], f32, S,
                    _tile(S, cfg["tm_in_rwkv"]), cfg["tn_rwkv"], 1, 0)

    slopes = jnp.broadcast_to(
        (2.0 ** (-ALIBI_MAX_BIAS * jnp.arange(1, H + 1, dtype=f32) / H))[:, None, None], (H, 1, LANES))
    lamp = jnp.stack([p["lam_q1"], p["lam_k1"], p["lam_q2"], p["lam_k2"]])
    o_att = _attention(att, slopes, lamp, p["att_subln_g"], B, S, H, lam_init, _tile(S, cfg["tq"]))

    cols = feats.shape[1]
    zw = jnp.zeros((RWKV_A_RANK, W), f32)
    heads = W // RWKV_HEAD
    ind = (jnp.arange(W)[:, None] // RWKV_HEAD == jnp.arange(LANES)[None, :]).astype(bf16)
    prm = {
        "mu": p["rwkv_mu"].reshape(1, cols), "w0": p["rwkv_w0"].reshape(1, W),
        "w2p": jnp.concatenate([p["rwkv_w2"], zw], axis=0),
        "a0": p["rwkv_a0"].reshape(1, W),
        "a2p": jnp.concatenate([jnp.zeros((RWKV_W_RANK, W), f32), p["rwkv_a2"]], axis=0),
        "g2": p["rwkv_g2"], "k_k": p["rwkv_k_k"].reshape(1, W), "k_a": p["rwkv_k_a"].reshape(1, W),
        "ind": ind, "indt": ind.T,
    }
    if v_first is not None:
        padc = LANES - RWKV_V_RANK
        prm["v0"] = p["rwkv_v0"].reshape(1, W)
        prm["v1p"] = jnp.pad(p["rwkv_v1"], ((0, 0), (0, padc)))
        prm["v2p"] = jnp.pad(p["rwkv_v2"], ((0, padc), (0, 0)))
    r, lw, kh, v, kn, kb, g = _rwkv_prep(feats, prm, v_first, B, S, W, _tile(S, cfg["tm_prep"]))
    if v_first is None:
        v_first = v
    r2, y0, mmat, g0 = _wkv_intra(r, lw, kh, v, kn, kb, min(cfg["nc"], S // CHUNK), cfg["passes_intra"])
    o_rwkv = _wkv_state(r2, y0, mmat, g0, r, kh, v, g, p["rwkv_lnx_w"].reshape(1, W),
                        p["rwkv_lnx_b"].reshape(1, W), p["rwkv_r_k"].reshape(1, W), B, S,
                        min(cfg["pg"], W // LANES), min(cfg["cb"], S // CHUNK), cfg["passes_state"])
    del heads

    w_out_bf = p["w_out"].astype(bf16)
    x1, h2, logits_t = _postmix(o_att, o_rwkv, w_out_bf[:D // 2], w_out_bf[D // 2:], x2, mod3,
                                p["g_post_mix"], p["g_pre_ffn"], p["w_router"].T, S, _tile(S, cfg["tm_post"]))

    tm_t = _tile(T, cfg["tm_tile"])
    ntiles = T // tm_t
    eidx_t, gate_t, lrank_t, cnt_tbl = _router(logits_t, p["router_bias"], tm_t)
    blk = cfg["blk"]
    run = (cnt_tbl[:, :ntiles].T + SUBLANES - 1) // SUBLANES * SUBLANES
    counts = jnp.sum(run, axis=0)
    padded = (counts + blk - 1) // blk * blk
    pad_end = jnp.cumsum(padded)
    pad_start = pad_end - padded
    e_ids = jnp.arange(N_EXPERTS, dtype=i32)
    tile_off = jnp.cumsum(run, axis=0) - run
    loc_off = jnp.cumsum(run, axis=1) - run
    seg_dst = (pad_start[None, :] + tile_off).reshape(-1).astype(i32)
    seg_src = loc_off.reshape(-1).astype(i32)
    seg_n = run.reshape(-1).astype(i32)
    seg_tot = jnp.sum(run, axis=1).astype(i32)
    loc_tok = jnp.repeat(loc_off, tm_t, axis=0)
    lp_t = jnp.sum(jnp.where(eidx_t[:, :, None] == e_ids, loc_tok[None], 0), axis=-1).astype(f32) + lrank_t
    lmax = -(-(tm_t * TOP_K + N_EXPERTS * (SUBLANES - 1)) // SORT_CHUNK) * SORT_CHUNK
    nblk = -(-(T * TOP_K + ntiles * N_EXPERTS * (SUBLANES - 1)) // blk) + N_EXPERTS
    P = nblk * blk
    blk_start = jnp.arange(nblk, dtype=i32) * blk
    nused = (pad_end[-1] // blk).astype(i32).reshape(1)
    blk_pos = jnp.minimum(blk_start, pad_end[-1] - blk)
    blk_e = jnp.minimum(jnp.sum((pad_end[None, :] <= blk_pos[:, None]).astype(i32), axis=1), N_EXPERTS - 1)
    cand = jnp.where(counts > 0, e_ids, N_EXPERTS)
    later = jnp.where(e_ids[None, :] > blk_e[:, None], cand[None, :], N_EXPERTS)
    nxt_e = jnp.min(later, axis=1)
    nxt_e = jnp.where(nxt_e >= N_EXPERTS, -1, nxt_e).astype(i32)

    xs, ysh = _dispatch(seg_src, seg_dst, seg_n, seg_tot, counts.astype(i32), pad_start.astype(i32), nused, lp_t, h2,
                        p["w_sh_gate"].astype(bf16), p["w_sh_up"].astype(bf16), p["w_sh_down"].astype(bf16),
                        P, blk, tm_t, lmax)
    ys = _experts(blk_e, nxt_e, nused, xs, wexp[0], wexp[1], wexp[2], i, blk)
    x_out = _combine(seg_src, seg_dst, seg_n, seg_tot, lp_t.T, gate_t.T, ys, ysh, x1,
                     mod3, p["g_post_ffn"], S, tm_t, lmax)
    return x_out, v_first


_CFG = dict(tm_in=1024, tn_att=1024, tm_in_rwkv=512, tn_rwkv=1664, tq=512, tm_prep=256, nc=8, passes_intra=1, passes_state=1, pg=4, cb=8,
            tm_post=512, blk=256, tm_tile=256)

_LAYER_KEYS = ("g_pre_mix", "g_post_mix", "g_pre_ffn", "g_post_ffn", "w_in", "w_out", "lam_q1", "lam_k1",
               "lam_q2", "lam_k2", "att_subln_g", "rwkv_mu", "rwkv_w0", "rwkv_w2", "rwkv_a0", "rwkv_a2",
               "rwkv_g2", "rwkv_k_k", "rwkv_k_a", "rwkv_r_k", "rwkv_lnx_w", "rwkv_lnx_b", "w_router",
               "router_bias", "w_sh_gate", "w_sh_up", "w_sh_down")


def _forward(x, c, params, cfg):
    B, S, D = x.shape
    L = params["w_in"].shape[0]
    bp = 16
    c_pad = jnp.zeros((bp, D), f32).at[:B].set(c)
    mod = _ada_mod(c_pad, params["w_ada"], params["b_ada"])
    x2 = x.reshape(B * S, D)
    v_first = None
    for i in range(L):
        p = {k: params[k][i] for k in _LAYER_KEYS}
        if i > 0:
            p["rwkv_v0"] = params["rwkv_v0"][i - 1]
            p["rwkv_v1"] = params["rwkv_v1"][i - 1]
            p["rwkv_v2"] = params["rwkv_v2"][i - 1]
        mod3 = mod[i, :B].reshape(B * N_MOD, 1, D)
        wexp = (params["w_exp_gate"], params["w_exp_up"], params["w_exp_down"])
        x2, v_first = _layer(i, x2, mod3, p, wexp, v_first, B, S, cfg)
    return x2.reshape(B, S, D)


def kernel(x, c, w_ada, b_ada, g_pre_mix, g_post_mix, g_pre_ffn, g_post_ffn, w_in, w_out, lam_q1, lam_k1, lam_q2, lam_k2, att_subln_g, rwkv_mu, rwkv_w0, rwkv_w2, rwkv_a0, rwkv_a2, rwkv_g2, rwkv_k_k, rwkv_k_a, rwkv_r_k, rwkv_lnx_w, rwkv_lnx_b, rwkv_v0, rwkv_v1, rwkv_v2, w_router, router_bias, w_exp_gate, w_exp_up, w_exp_down, w_sh_gate, w_sh_up, w_sh_down):
    params = dict(w_ada=w_ada, b_ada=b_ada, g_pre_mix=g_pre_mix, g_post_mix=g_post_mix, g_pre_ffn=g_pre_ffn,
                  g_post_ffn=g_post_ffn, w_in=w_in, w_out=w_out, lam_q1=lam_q1, lam_k1=lam_k1, lam_q2=lam_q2,
                  lam_k2=lam_k2, att_subln_g=att_subln_g, rwkv_mu=rwkv_mu, rwkv_w0=rwkv_w0, rwkv_w2=rwkv_w2,
                  rwkv_a0=rwkv_a0, rwkv_a2=rwkv_a2, rwkv_g2=rwkv_g2, rwkv_k_k=rwkv_k_k, rwkv_k_a=rwkv_k_a,
                  rwkv_r_k=rwkv_r_k, rwkv_lnx_w=rwkv_lnx_w, rwkv_lnx_b=rwkv_lnx_b, rwkv_v0=rwkv_v0,
                  rwkv_v1=rwkv_v1, rwkv_v2=rwkv_v2, w_router=w_router, router_bias=router_bias,
                  w_exp_gate=w_exp_gate, w_exp_up=w_exp_up, w_exp_down=w_exp_down, w_sh_gate=w_sh_gate,
                  w_sh_up=w_sh_up, w_sh_down=w_sh_down)
    return _forward(x, c, params, _CFG)
```

```python
import functools
import math

import jax
import jax.numpy as jnp
from jax import lax
from jax.experimental import pallas as pl
from jax.experimental.pallas import tpu as pltpu

f32 = jnp.float32
bf16 = jnp.bfloat16
i32 = jnp.int32
u32 = jnp.uint32

ATT_QK_DIM = 64
ATT_V_DIM = 128
ALIBI_MAX_BIAS = 8.0
ATT_SUBLN_EPS = 1e-5
RWKV_HEAD = 64
RWKV_W_RANK = 64
RWKV_A_RANK = 64
RWKV_G_RANK = 128
RWKV_V_RANK = 32
RWKV_GN_EPS = 64e-5
N_EXPERTS = 64
N_GROUPS = 8
TOPK_GROUPS = 4
TOP_K = 8
ROUTED_SCALE = 2.5
NORM_EPS = 1e-6
N_MOD = 6

LANES = 128
SUBLANES = 8
CHUNK = 64
VMEM_LIMIT = 56 * 1024 * 1024


def _cparams(sem):
    return pltpu.CompilerParams(dimension_semantics=sem, vmem_limit_bytes=VMEM_LIMIT)


def _dot(a, b):
    return jnp.dot(a, b, preferred_element_type=f32)


def _dot_nt(a, b):
    return lax.dot_general(a, b, (((1,), (1,)), ((), ())), preferred_element_type=f32)


def _split2(x):
    hi = x.astype(bf16)
    lo = (x - hi.astype(f32)).astype(bf16)
    return hi, lo


def _mm(a, b, passes=1):
    if passes == 1:
        return _dot(a.astype(bf16), b.astype(bf16))
    ah, al = _split2(a.astype(f32))
    bh, bl = _split2(b.astype(f32))
    return (_dot(al, bh) + _dot(ah, bl)) + _dot(ah, bh)


def _mm_exact_rhs(a, b_bf16):
    ah, al = _split2(a)
    return _dot(al, b_bf16) + _dot(ah, b_bf16)


def _sigmoid(x):
    return 1.0 / (1.0 + jnp.exp(-x))


def _pack_rows(x):
    half = x.shape[1] // 2
    a = x[:, :half].astype(bf16).astype(f32)
    b = x[:, half:].astype(bf16).astype(f32)
    lo = lax.shift_right_logical(lax.bitcast_convert_type(a, u32), jnp.uint32(16))
    hi = lax.bitcast_convert_type(b, u32) & jnp.uint32(0xFFFF0000)
    return lo | hi


def _unpack_rows(w):
    lo = lax.bitcast_convert_type(lax.shift_left(w, jnp.uint32(16)), f32)
    hi = lax.bitcast_convert_type(w & jnp.uint32(0xFFFF0000), f32)
    return lo, hi


def _ada_kernel(c_ref, w_ref, b_ref, o_ref):
    c = c_ref[...]
    cond = (c * _sigmoid(c)).astype(bf16)
    o_ref[0] = _dot(cond, w_ref[0].astype(bf16)) + b_ref[0]


def _ada_mod(c_pad, w_ada, b_ada, tn=1024):
    L, D, N = w_ada.shape
    bp = c_pad.shape[0]
    return pl.pallas_call(
        _ada_kernel,
        grid=(L, N // tn),
        in_specs=[pl.BlockSpec((bp, D), lambda l, j: (0, 0)),
                  pl.BlockSpec((1, D, tn), lambda l, j: (l, 0, j)),
                  pl.BlockSpec((1, 1, tn), lambda l, j: (l, 0, j))],
        out_specs=pl.BlockSpec((1, bp, tn), lambda l, j: (l, 0, j)),
        out_shape=jax.ShapeDtypeStruct((L, bp, N), f32),
        compiler_params=_cparams(("arbitrary", "arbitrary")),
        name="ada_mod",
    )(c_pad, w_ada, b_ada.reshape(L, 1, N))


def _prenorm_kernel(x_ref, sc_ref, sh_ref, g_ref, h_ref):
    x = x_ref[...]
    ms = jnp.mean(x * x, axis=-1, keepdims=True)
    y = x * lax.rsqrt(ms + NORM_EPS) * g_ref[...]
    h_ref[...] = (y * (1.0 + sc_ref[0]) + sh_ref[0]).astype(h_ref.dtype)


def _prenorm(x2, mod3, g, S, tm, seg_sc, seg_sh):
    T, D = x2.shape
    tpb = S // tm
    tile = pl.BlockSpec((tm, D), lambda i: (i, 0))
    return pl.pallas_call(
        _prenorm_kernel,
        grid=(T // tm,),
        in_specs=[tile,
                  pl.BlockSpec((1, 1, D), lambda i: ((i // tpb) * N_MOD + seg_sc, 0, 0)),
                  pl.BlockSpec((1, 1, D), lambda i: ((i // tpb) * N_MOD + seg_sh, 0, 0)),
                  pl.BlockSpec((1, D), lambda i: (0, 0))],
        out_specs=tile,
        out_shape=jax.ShapeDtypeStruct((T, D), bf16),
        compiler_params=_cparams(("arbitrary",)),
        name="prenorm",
    )(x2, mod3, mod3, g.reshape(1, D))


def _inproj_kernel(h_ref, w_ref, o_ref):
    o_ref[...] = _dot(h_ref[...], w_ref[...]).astype(o_ref.dtype)


def _inproj(h, w_bf, out_dtype, tm, tn):
    T, D = h.shape
    N = w_bf.shape[1]
    return pl.pallas_call(
        _inproj_kernel,
        grid=(T // tm, N // tn),
        in_specs=[pl.BlockSpec((tm, D), lambda i, j: (i, 0)),
                  pl.BlockSpec((D, tn), lambda i, j: (0, j))],
        out_specs=pl.BlockSpec((tm, tn), lambda i, j: (i, j)),
        out_shape=jax.ShapeDtypeStruct((T, N), out_dtype),
        compiler_params=_cparams(("arbitrary", "arbitrary")),
        name="inproj",
    )(h, w_bf)


def _attn_kernel(q_ref, k_ref, v_ref, slope_ref, lamp_ref, g_ref, o_ref,
                 q2t_ref, vt_ref, m_ref, l_ref, acc_ref, *, tq, hp, lam_init):
    qi = pl.program_id(2)
    scale = ATT_QK_DIM ** -0.5
    heads = range(hp)
    hl = [slice(h * LANES, (h + 1) * LANES) for h in heads]
    slope = [slope_ref[h][:, 0:1] for h in heads]

    @pl.when(qi == 0)
    def _():
        for h in heads:
            vt_ref[h] = v_ref[:, hl[h]].astype(f32).T.astype(bf16)

    dim = lax.broadcasted_iota(i32, (LANES, 1), 0)
    first = dim < ATT_QK_DIM
    for h in heads:
        qt = (q_ref[:, hl[h]].astype(f32) * scale).T
        q2t_ref[h, :, 0:tq] = jnp.where(first, qt, 0.0).astype(bf16)
        q2t_ref[h, :, tq:2 * tq] = jnp.where(first, 0.0, qt).astype(bf16)
    m_ref[...] = jnp.full(m_ref.shape, -jnp.inf, f32)
    l_ref[...] = jnp.zeros(l_ref.shape, f32)
    acc_ref[...] = jnp.zeros(acc_ref.shape, f32)

    kr = lax.broadcasted_iota(i32, (tq, 2 * tq), 0)
    qc = lax.broadcasted_iota(i32, (tq, 2 * tq), 1)
    causal = jnp.where(qc >= tq, qc - tq, qc) >= kr
    krow = lax.broadcasted_iota(i32, (tq, 1), 0).astype(f32)

    def step(ki, masked):
        start = pl.multiple_of(ki * tq, tq)
        kpos = krow + (ki * tq).astype(f32)
        s = [_dot(k_ref[pl.ds(start, tq), hl[h]], q2t_ref[h]) for h in heads]
        s = [s[h] + slope[h] * kpos for h in heads]
        if masked:
            s = [jnp.where(causal, s[h], -jnp.inf) for h in heads]
        m_prev = [m_ref[h] for h in heads]
        m_new = [jnp.maximum(m_prev[h], jnp.max(s[h], axis=0, keepdims=True)) for h in heads]
        alpha = [jnp.exp(m_prev[h] - m_new[h]) for h in heads]
        p = [jnp.exp(s[h] - m_new[h]) for h in heads]
        for h in heads:
            l_ref[h] = alpha[h] * l_ref[h] + jnp.sum(p[h], axis=0, keepdims=True)
            acc_ref[h] = alpha[h] * acc_ref[h] + _dot(vt_ref[h, :, pl.ds(start, tq)], p[h].astype(bf16))
            m_ref[h] = m_new[h]

    def body(ki, carry):
        step(ki, False)
        return carry

    lax.fori_loop(0, qi, body, 0)
    step(qi, True)

    lp = lamp_ref[...]
    lam = (jnp.exp(jnp.sum(lp[0:1] * lp[1:2], axis=-1, keepdims=True))
           - jnp.exp(jnp.sum(lp[2:3] * lp[3:4], axis=-1, keepdims=True)) + lam_init)
    for h in heads:
        on = acc_ref[h] * (1.0 / l_ref[h])
        o = on[:, 0:tq] - lam * on[:, tq:2 * tq]
        o = o * lax.rsqrt(jnp.mean(o * o, axis=0, keepdims=True) + ATT_SUBLN_EPS)
        o = o * g_ref[...] * (1.0 - lam_init)
        o_ref[:, hl[h]] = o.T.astype(o_ref.dtype)


def _attention(att, slopes, lamp, subln_g, B, S, H, lam_init, tq, hp):
    T = att.shape[0]
    nq = S // tq
    kern = functools.partial(_attn_kernel, tq=tq, hp=hp, lam_init=lam_init)
    hw = hp * LANES
    return pl.pallas_call(
        kern,
        grid=(B, H // hp, nq),
        in_specs=[pl.BlockSpec((tq, hw), lambda b, g, q: (b * nq + q, g)),
                  pl.BlockSpec((S, hw), lambda b, g, q: (b, H // hp + g)),
                  pl.BlockSpec((S, hw), lambda b, g, q: (b, 2 * (H // hp) + g)),
                  pl.BlockSpec((hp, 1, LANES), lambda b, g, q: (g, 0, 0)),
                  pl.BlockSpec((4, ATT_QK_DIM), lambda b, g, q: (0, 0)),
                  pl.BlockSpec((ATT_V_DIM, 1), lambda b, g, q: (0, 0))],
        out_specs=pl.BlockSpec((tq, hw), lambda b, g, q: (b * nq + q, g)),
        out_shape=jax.ShapeDtypeStruct((T, H * ATT_V_DIM), bf16),
        scratch_shapes=[pltpu.VMEM((hp, LANES, 2 * tq), bf16),
                        pltpu.VMEM((hp, LANES, S), bf16),
                        pltpu.VMEM((hp, 1, 2 * tq), f32),
                        pltpu.VMEM((hp, 1, 2 * tq), f32),
                        pltpu.VMEM((hp, LANES, 2 * tq), f32)],
        compiler_params=_cparams(("arbitrary", "arbitrary", "arbitrary")),
        name="diff_attention",
    )(att, att, att, slopes, lamp, subln_g.reshape(ATT_V_DIM, 1))


def _head_sums(x, ind, indt):
    s = _mm_exact_rhs(x, ind)
    return _mm_exact_rhs(s, indt)


def _rwkv_prep_kernel(*refs, W, has_vres):
    if has_vres:
        (f_ref, mu_ref, w0_ref, w2_ref, a0_ref, a2_ref, g2_ref, kk_ref, ka_ref, ind_ref, indt_ref,
         vf_ref, v0_ref, v1_ref, v2_ref,
         r_o, lw_o, kh_o, v_o, kn_o, kb_o, g_o, carry_ref) = refs
    else:
        (f_ref, mu_ref, w0_ref, w2_ref, a0_ref, a2_ref, g2_ref, kk_ref, ka_ref, ind_ref, indt_ref,
         r_o, lw_o, kh_o, v_o, kn_o, kb_o, g_o, carry_ref) = refs

    ti = pl.program_id(1)
    h = f_ref[...]
    tm = h.shape[0]

    @pl.when(ti == 0)
    def _():
        carry_ref[...] = jnp.zeros(carry_ref.shape, f32)

    rolled = pltpu.roll(h, 1, axis=0)
    row = lax.broadcasted_iota(i32, (tm, 1), 0)
    prev = jnp.where(row == 0, carry_ref[...], rolled)
    carry_ref[...] = h[tm - 1:tm, :]
    feats = h + (prev - h) * mu_ref[...]

    r = feats[:, 0:W]
    k = feats[:, W:2 * W]
    v = feats[:, 2 * W:3 * W]
    wa = feats[:, 3 * W:3 * W + LANES]
    g_lo = feats[:, 3 * W + LANES:3 * W + 2 * LANES]

    w = w0_ref[...] + _mm(jnp.tanh(wa), w2_ref[...], passes=3)
    lw_o[...] = -math.exp(-0.5) * _sigmoid(w)
    a = _sigmoid(a0_ref[...] + _mm(wa, a2_ref[...], passes=3))
    g_o[...] = _mm(_sigmoid(g_lo), g2_ref[...]).astype(g_o.dtype)

    if has_vres:
        mix = _sigmoid(v0_ref[...] + _mm(_mm(v, v1_ref[...]), v2_ref[...]))
        v = v + (vf_ref[...].astype(f32) - v) * mix

    kk = k * kk_ref[...]
    ss = _head_sums(kk * kk, ind_ref[...], indt_ref[...])
    kk = kk / jnp.maximum(jnp.sqrt(ss), 1e-12)
    r_o[...] = r.astype(r_o.dtype)
    kh_o[...] = (k * (1.0 + (a - 1.0) * ka_ref[...])).astype(kh_o.dtype)
    v_o[...] = v.astype(v_o.dtype)
    kn_o[...] = kk.astype(kn_o.dtype)
    kb_o[...] = (kk * a).astype(kb_o.dtype)


def _rwkv_prep(feats, prm, vfirst, B, S, W, tm):
    T, COLS = feats.shape
    tpb = S // tm
    has_vres = vfirst is not None
    row = lambda n: pl.BlockSpec((1, n), lambda b, t: (0, 0))
    full = lambda a: pl.BlockSpec(a.shape, lambda b, t: (0, 0))
    tile = pl.BlockSpec((tm, W), lambda b, t: (b * tpb + t, 0))
    args = [feats, prm["mu"], prm["w0"], prm["w2p"], prm["a0"], prm["a2p"], prm["g2"], prm["k_k"], prm["k_a"],
            prm["ind"], prm["indt"]]
    specs = [pl.BlockSpec((tm, COLS), lambda b, t: (b * tpb + t, 0)), row(COLS), row(W), full(prm["w2p"]),
             row(W), full(prm["a2p"]), full(prm["g2"]), row(W), row(W), full(prm["ind"]), full(prm["indt"])]
    if has_vres:
        args += [vfirst, prm["v0"], prm["v1p"], prm["v2p"]]
        specs += [tile, row(W), full(prm["v1p"]), full(prm["v2p"])]
    out_dtypes = (bf16, f32, bf16, bf16, bf16, bf16, bf16)
    kern = functools.partial(_rwkv_prep_kernel, W=W, has_vres=has_vres)
    return pl.pallas_call(
        kern,
        grid=(B, tpb),
        in_specs=specs,
        out_specs=[tile] * 7,
        out_shape=[jax.ShapeDtypeStruct((T, W), dt) for dt in out_dtypes],
        scratch_shapes=[pltpu.VMEM((1, COLS), f32)],
        compiler_params=_cparams(("arbitrary", "arbitrary")),
        name="rwkv_prep",
    )(*args)


def _wkv_chunks(rs, lws, ks, vs, kns, kbs, passes):
    C = CHUNK
    P2 = 2 * C
    n = range(len(rs))
    ri = lax.broadcasted_iota(i32, (C, C), 0)
    ci = lax.broadcasted_iota(i32, (C, C), 1)
    tri = (ci <= ri).astype(bf16)
    lane = lax.broadcasted_iota(i32, (1, LANES), 1)
    m0 = (lane < RWKV_HEAD).astype(f32)
    m1 = 1.0 - m0
    rr = lax.broadcasted_iota(i32, (P2, P2), 0)
    cc = lax.broadcasted_iota(i32, (P2, P2), 1)
    same = jnp.where(rr >= C, 1, 0) == jnp.where(cc >= C, 1, 0)
    strict = same & (cc < rr)
    incl = same & (cc <= rr)
    incl2 = jnp.concatenate([incl, incl], axis=1)
    eye = (rr == cc).astype(f32)
    zeros_p = jnp.zeros((P2, LANES), f32)
    zeros_c = jnp.zeros((C, LANES), f32)
    stack = lambda x: jnp.concatenate([x * m0, x * m1], axis=0)
    fold = lambda x: x[0:C] + x[C:2 * C]

    def cumsum(lw):
        h1 = lw.astype(bf16)
        r1 = lw - h1.astype(f32)
        h2 = r1.astype(bf16)
        h3 = (r1 - h2.astype(f32)).astype(bf16)
        return (_dot(tri, h3) + _dot(tri, h2)) + _dot(tri, h1)

    cum = [cumsum(lws[j]) for j in n]
    cum_c = [cum[j][C - 1:C, :] for j in n]
    at = [-kns[j] * jnp.exp(cum[j] - lws[j]) for j in n]
    rt = [rs[j] * jnp.exp(cum[j]) for j in n]
    einv = [jnp.exp(-cum[j]) for j in n]
    bt = [kbs[j] * einv[j] for j in n]
    kt = [ks[j] * einv[j] for j in n]
    eh = [jnp.exp(cum_c[j] - cum[j]) for j in n]
    bh = [kbs[j] * eh[j] for j in n]
    kh = [ks[j] * eh[j] for j in n]
    w_c = [jnp.exp(cum_c[j]) for j in n]
    abd = [stack(at[j]) for j in n]
    vst = [stack(vs[j]) for j in n]
    lhs = [jnp.concatenate([abd[j], stack(rt[j])], axis=0) for j in n]
    rhs = [jnp.concatenate([stack(bt[j]), stack(kt[j])], axis=0) for j in n]
    gram = [_mm_nt(lhs[j], rhs[j], passes) for j in n]
    lab = [jnp.where(strict, gram[j][0:P2, 0:P2], 0.0) for j in n]
    lak = [jnp.where(strict, gram[j][0:P2, P2:2 * P2], 0.0) for j in n]
    mrbk = [jnp.where(incl2, gram[j][P2:2 * P2, :], 0.0) for j in n]

    x0 = [_mm(lak[j], vst[j], passes) for j in n]
    tinv = [eye + lab[j] for j in n]
    lp = lab
    for _ in range(int(math.log2(C)) - 1):
        lp = [_mm(lp[j], lp[j], passes) for j in n]
        tinv = [tinv[j] + _mm(lp[j], tinv[j], passes) for j in n]

    ta = [_mm(tinv[j], jnp.concatenate([abd[j], x0[j]], axis=1), passes) for j in n]
    rhs2 = [jnp.concatenate([ta[j], jnp.concatenate([zeros_p, vst[j]], axis=1)], axis=0) for j in n]
    z = [_mm(mrbk[j], rhs2[j], passes) for j in n]
    r2 = [rt[j] + fold(z[j][:, 0:LANES]) for j in n]
    y0 = [fold(z[j][:, LANES:2 * LANES]) for j in n]
    lhs3t = [jnp.concatenate([bh[j], kh[j]], axis=0).T for j in n]
    rhs3 = [jnp.concatenate([fold(ta[j]), jnp.concatenate([zeros_c, vs[j]], axis=1)], axis=0) for j in n]
    wmat = [_mm(lhs3t[j], rhs3[j], passes) for j in n]
    mmat = [jnp.where(same, wmat[j][:, 0:LANES], 0.0) + eye * w_c[j] for j in n]
    g0 = [jnp.where(same, wmat[j][:, LANES:2 * LANES], 0.0) for j in n]
    return r2, y0, mmat, g0


def _mm_nt(a, b, passes):
    if passes == 1:
        return _dot_nt(a.astype(bf16), b.astype(bf16))
    ah, al = _split2(a)
    bh, bl = _split2(b)
    return (_dot_nt(al, bh) + _dot_nt(ah, bl)) + _dot_nt(ah, bh)


def _wkv_intra_kernel(r_ref, lw_ref, k_ref, v_ref, kn_ref, kb_ref, r2_o, y0_o, m_o, g_o, *, nc, passes):
    C = CHUNK
    sls = [slice(c * C, (c + 1) * C) for c in range(nc)]
    take = lambda ref: [ref[sl, :].astype(f32) for sl in sls]
    r2, y0, mmat, g0 = _wkv_chunks(take(r_ref), take(lw_ref), take(k_ref), take(v_ref),
                                   take(kn_ref), take(kb_ref), passes)
    for c, sl in enumerate(sls):
        r2_o[sl, :] = r2[c].astype(r2_o.dtype)
        y0_o[sl, :] = y0[c].astype(y0_o.dtype)
        m_o[0, c] = mmat[c].astype(m_o.dtype)
        g_o[0, c] = g0[c]


def _wkv_intra(r, lw, kh, v, kn, kb, nc, passes):
    T, W = r.shape
    npair = W // LANES
    rows = nc * CHUNK
    tile = pl.BlockSpec((rows, LANES), lambda p, i: (i, p))
    mat = pl.BlockSpec((1, nc, LANES, LANES), lambda p, i: (p, i, 0, 0))
    kern = functools.partial(_wkv_intra_kernel, nc=nc, passes=passes)
    return pl.pallas_call(
        kern,
        grid=(npair, T // rows),
        in_specs=[tile] * 6,
        out_specs=[tile, tile, mat, mat],
        out_shape=[jax.ShapeDtypeStruct((T, W), bf16), jax.ShapeDtypeStruct((T, W), bf16),
                   jax.ShapeDtypeStruct((npair, T // CHUNK, LANES, LANES), bf16),
                   jax.ShapeDtypeStruct((npair, T // CHUNK, LANES, LANES), f32)],
        compiler_params=_cparams(("arbitrary", "arbitrary")),
        name="wkv_intra",
    )(r, lw, kh, v, kn, kb)


def _wkv_state_kernel(r2_ref, y0_ref, m_ref, g0_ref, r_ref, kh_ref, v_ref, g_ref,
                      lnw_ref, lnb_ref, rk_ref, o_ref, st_ref, y_ref, *, pg, cb, passes):
    C = CHUNK

    @pl.when(pl.program_id(2) == 0)
    def _():
        st_ref[...] = jnp.zeros(st_ref.shape, f32)

    pairs = range(pg)
    lanes = [slice(p * LANES, (p + 1) * LANES) for p in pairs]
    st = [st_ref[p] for p in pairs]
    for c in range(cb):
        rows = slice(c * C, (c + 1) * C)
        for p in pairs:
            y_ref[rows, lanes[p]] = (_mm(r2_ref[rows, lanes[p]], st[p], passes)
                                     + y0_ref[rows, lanes[p]].astype(f32))
        st = [_mm(m_ref[p, c], st[p], passes) + g0_ref[p, c] for p in pairs]
    for p in pairs:
        st_ref[p] = st[p]

    rr = lax.broadcasted_iota(i32, (LANES, LANES), 0)
    cc = lax.broadcasted_iota(i32, (LANES, LANES), 1)
    ones_bd = (jnp.where(rr >= RWKV_HEAD, 1, 0) == jnp.where(cc >= RWKV_HEAD, 1, 0)).astype(bf16)
    for p in pairs:
        y = y_ref[:, lanes[p]]
        mu = _mm_exact_rhs(y, ones_bd) * (1.0 / RWKV_HEAD)
        d = y - mu
        var = _mm_exact_rhs(d * d, ones_bd) * (1.0 / RWKV_HEAD)
        yn = d * lax.rsqrt(var + RWKV_GN_EPS) * lnw_ref[:, lanes[p]] + lnb_ref[:, lanes[p]]
        rk = r_ref[:, lanes[p]].astype(f32) * kh_ref[:, lanes[p]].astype(f32) * rk_ref[:, lanes[p]]
        bonus = _mm_exact_rhs(rk, ones_bd) * v_ref[:, lanes[p]]
        o_ref[:, lanes[p]] = ((yn + bonus) * g_ref[:, lanes[p]]).astype(o_ref.dtype)


def _wkv_state(r2, y0, mm, g0, r, kh, v, g, lnw, lnb, rk, B, S, pg, cb, passes):
    T, W = r.shape
    npair = W // LANES
    rows = cb * CHUNK
    steps = S // rows
    seq = pl.BlockSpec((rows, pg * LANES), lambda b, q, c: (b * steps + c, q))
    mat = pl.BlockSpec((pg, cb, LANES, LANES), lambda b, q, c: (q, b * steps + c, 0, 0))
    prow = pl.BlockSpec((1, pg * LANES), lambda b, q, c: (0, q))
    kern = functools.partial(_wkv_state_kernel, pg=pg, cb=cb, passes=passes)
    return pl.pallas_call(
        kern,
        grid=(B, npair // pg, steps),
        in_specs=[seq, seq, mat, mat, seq, seq, seq, seq, prow, prow, prow],
        out_specs=seq,
        out_shape=jax.ShapeDtypeStruct((T, W), bf16),
        scratch_shapes=[pltpu.VMEM((pg, LANES, LANES), f32), pltpu.VMEM((rows, pg * LANES), f32)],
        compiler_params=_cparams(("arbitrary", "arbitrary", "arbitrary")),
        name="wkv_state",
    )(r2, y0, mm, g0, r, kh, v, g, lnw, lnb, rk)


def _postmix_kernel(oa_ref, orw_ref, wa_ref, wr_ref, x_ref, ga_ref, gpost_ref, gpre_ref, sc_ref, sh_ref, wrt_ref,
                    x1_o, h2_o, lg_o):
    mixed = _dot(oa_ref[...], wa_ref[...]) + _dot(orw_ref[...], wr_ref[...])
    ms = jnp.mean(mixed * mixed, axis=-1, keepdims=True)
    x1 = x_ref[...] + ga_ref[0] * (mixed * lax.rsqrt(ms + NORM_EPS) * gpost_ref[...])
    x1_o[...] = x1
    ms1 = jnp.mean(x1 * x1, axis=-1, keepdims=True)
    h2 = (x1 * lax.rsqrt(ms1 + NORM_EPS) * gpre_ref[...]) * (1.0 + sc_ref[0]) + sh_ref[0]
    h2_o[...] = _pack_rows(h2)
    lg_o[...] = _mm_nt(wrt_ref[...], h2, 3)


def _postmix(o_att, o_rwkv, w_out_a, w_out_r, x2, mod3, g_post, g_pre, w_rt, S, tm):
    T, D = x2.shape
    WA = o_att.shape[1]
    WR = o_rwkv.shape[1]
    E = w_rt.shape[0]
    tpb = S // tm
    modspec = lambda seg: pl.BlockSpec((1, 1, D), lambda i: ((i // tpb) * N_MOD + seg, 0, 0))
    tile = pl.BlockSpec((tm, D), lambda i: (i, 0))
    return pl.pallas_call(
        _postmix_kernel,
        grid=(T // tm,),
        in_specs=[pl.BlockSpec((tm, WA), lambda i: (i, 0)),
                  pl.BlockSpec((tm, WR), lambda i: (i, 0)),
                  pl.BlockSpec((WA, D), lambda i: (0, 0), pipeline_mode=pl.Buffered(1)),
                  pl.BlockSpec((WR, D), lambda i: (0, 0), pipeline_mode=pl.Buffered(1)),
                  tile, modspec(2),
                  pl.BlockSpec((1, D), lambda i: (0, 0)),
                  pl.BlockSpec((1, D), lambda i: (0, 0)),
                  modspec(4), modspec(3),
                  pl.BlockSpec((E, D), lambda i: (0, 0))],
        out_specs=[tile, pl.BlockSpec((tm, D // 2), lambda i: (i, 0)),
                   pl.BlockSpec((E, tm), lambda i: (0, i))],
        out_shape=[jax.ShapeDtypeStruct((T, D), f32), jax.ShapeDtypeStruct((T, D // 2), u32),
                   jax.ShapeDtypeStruct((E, T), f32)],
        compiler_params=_cparams(("arbitrary",)),
        name="postmix",
    )(o_att, o_rwkv, w_out_a, w_out_r, x2, mod3, g_post.reshape(1, D), g_pre.reshape(1, D), mod3, mod3, w_rt)


def _first_max(x, iota, n):
    mx = jnp.max(x, axis=0, keepdims=True)
    idx = jnp.min(jnp.where(x == mx, iota, n), axis=0, keepdims=True)
    return mx, idx


def _router_kernel(lg_ref, bias_ref, eidx_o, gate_o, rank_o, cnt_o, cnt_ref):
    E = N_EXPERTS
    G = N_GROUPS
    per = E // G
    tm = lg_ref.shape[1]

    @pl.when(pl.program_id(0) == 0)
    def _():
        cnt_ref[...] = jnp.zeros(cnt_ref.shape, f32)

    scores = _sigmoid(lg_ref[...])
    biased = scores + bias_ref[...]
    neg = -jnp.inf

    iota_p = lax.broadcasted_iota(i32, (per, tm), 0).astype(f32)
    gs = []
    for g in range(G):
        xg = biased[g * per:(g + 1) * per, :]
        m1, i1 = _first_max(xg, iota_p, per)
        m2 = jnp.max(jnp.where(iota_p == i1, neg, xg), axis=0, keepdims=True)
        gs.append(m1 + m2)
    gsc = jnp.concatenate(gs, axis=0)
    iota_g = lax.broadcasted_iota(i32, (G, tm), 0).astype(f32)
    gsel = jnp.zeros((G, tm), f32)
    for _ in range(TOPK_GROUPS):
        _, gi = _first_max(gsc, iota_g, G)
        hit = iota_g == gi
        gsel = jnp.where(hit, 1.0, gsel)
        gsc = jnp.where(hit, neg, gsc)
    masked = jnp.concatenate(
        [jnp.where(gsel[g:g + 1, :] > 0.0, biased[g * per:(g + 1) * per, :], neg) for g in range(G)], axis=0)

    iota_e = lax.broadcasted_iota(i32, (E, tm), 0).astype(f32)
    sel = jnp.zeros((E, tm), f32)
    idxs, vals = [], []
    for _ in range(TOP_K):
        _, ei = _first_max(masked, iota_e, E)
        hit = iota_e == ei
        idxs.append(ei)
        vals.append(jnp.sum(jnp.where(hit, scores, 0.0), axis=0, keepdims=True))
        sel = jnp.where(hit, 1.0, sel)
        masked = jnp.where(hit, neg, masked)
    tot = vals[0]
    for vv in vals[1:]:
        tot = tot + vv
    eidx_o[...] = jnp.concatenate(idxs, axis=0).astype(i32)
    gate_o[...] = jnp.concatenate([vv / tot * ROUTED_SCALE for vv in vals], axis=0)

    rr = lax.broadcasted_iota(i32, (tm, tm), 0)
    cc = lax.broadcasted_iota(i32, (tm, tm), 1)
    before = (rr < cc).astype(bf16)
    pos = _dot(sel.astype(bf16), before)
    rank_o[...] = jnp.concatenate(
        [jnp.sum(jnp.where(iota_e == ei, pos, 0.0), axis=0, keepdims=True) for ei in idxs], axis=0)
    lane = lax.broadcasted_iota(i32, cnt_ref.shape, 1)
    cnt_ref[...] = jnp.where(lane == pl.program_id(0), jnp.sum(sel, axis=1, keepdims=True), cnt_ref[...])
    cnt_o[...] = cnt_ref[...].astype(i32)


def _router(logits_t, bias, tm):
    E, T = logits_t.shape
    assert T // tm <= LANES
    k_tile = pl.BlockSpec((TOP_K, tm), lambda i: (0, i))
    return pl.pallas_call(
        _router_kernel,
        grid=(T // tm,),
        in_specs=[pl.BlockSpec((E, tm), lambda i: (0, i)),
                  pl.BlockSpec((E, 1), lambda i: (0, 0))],
        out_specs=[k_tile, k_tile, k_tile, pl.BlockSpec((E, LANES), lambda i: (0, 0))],
        out_shape=[jax.ShapeDtypeStruct((TOP_K, T), i32), jax.ShapeDtypeStruct((TOP_K, T), f32),
                   jax.ShapeDtypeStruct((TOP_K, T), f32), jax.ShapeDtypeStruct((E, LANES), i32)],
        scratch_shapes=[pltpu.VMEM((E, LANES), f32)],
        compiler_params=_cparams(("arbitrary",)),
        name="router",
    )(logits_t, bias.reshape(E, 1))


def _row_copy(src_ref, s, dst_ref, d, sem):
    return pltpu.make_async_copy(src_ref.at[pl.ds(s, 1), :], dst_ref.at[pl.ds(d, 1), :], sem)


def _zero_fill(cnt_ref, pstart_ref, nused_ref, z_ref, xs_out, sem, blk, nblk, start):
    def act(cp):
        if start:
            cp.start()
        else:
            cp.wait()

    def per_expert(e, carry):
        c = cnt_ref[e]
        base = pstart_ref[e] + c
        npad = (blk - (c & (blk - 1))) & (blk - 1)
        head = (-base) & (SUBLANES - 1)

        def one_row(j, carry2):
            act(_row_copy(z_ref, 0, xs_out, base + j, sem))
            return carry2

        lax.fori_loop(0, head, one_row, 0)
        rem = npad - head
        aligned = base + head
        p = blk // 2
        while p >= SUBLANES:
            off = pl.multiple_of(aligned + (rem & ~(2 * p - 1)), SUBLANES)

            @pl.when((rem & p) != 0)
            def _(p=p, off=off):
                act(pltpu.make_async_copy(z_ref.at[pl.ds(0, p), :], xs_out.at[pl.ds(off, p), :], sem))

            p //= 2
        return carry

    lax.fori_loop(0, N_EXPERTS, per_expert, 0)

    def per_block(b, carry):
        act(pltpu.make_async_copy(z_ref, xs_out.at[pl.ds(pl.multiple_of(b * blk, blk), blk), :], sem))
        return carry

    lax.fori_loop(nused_ref[0], nblk, per_block, 0)


def _seg_copies(ssrc_ref, sdst_ref, sn_ref, tile, buf_ref, hbm_ref, sem, tm, to_hbm, start):
    def act(cp):
        if start:
            cp.start()
        else:
            cp.wait()

    def per_expert(e, carry):
        idx = tile * N_EXPERTS + e
        n = sn_ref[idx]
        src = ssrc_ref[idx]
        dst = sdst_ref[idx]
        p = tm
        while p >= SUBLANES:
            off = n & ~(2 * p - 1)

            @pl.when((n & p) != 0)
            def _(p=p, off=off):
                v = buf_ref.at[pl.ds(pl.multiple_of(src + off, SUBLANES), p), :]
                h = hbm_ref.at[pl.ds(pl.multiple_of(dst + off, SUBLANES), p), :]
                act(pltpu.make_async_copy(v, h, sem) if to_hbm else pltpu.make_async_copy(h, v, sem))

            p //= 2
        return carry

    lax.fori_loop(0, N_EXPERTS, per_expert, 0)


def _seg_wait(total_rows, buf_ref, hbm_ref, sem, to_hbm):
    p = 1 << (buf_ref.shape[0].bit_length() - 1)
    while p >= SUBLANES:
        @pl.when((total_rows & p) != 0)
        def _(p=p):
            v = buf_ref.at[pl.ds(0, p), :]
            h = hbm_ref.at[pl.ds(0, p), :]
            (pltpu.make_async_copy(v, h, sem) if to_hbm else pltpu.make_async_copy(h, v, sem)).wait()

        p //= 2


SORT_CHUNK = 512


def _dispatch_kernel(ssrc_ref, sdst_ref, sn_ref, stot_ref, cnt_ref, pstart_ref, nused_ref, lp_ref, h_ref,
                     wsg_ref, wsu_ref, wsd_ref, xs_out, ysh_o, sb_ref, z_ref, sems, sem_z, *, blk, nblk):
    i = pl.program_id(0)
    last = pl.num_programs(0) - 1
    tm = h_ref.shape[0]
    lmax = sb_ref.shape[1]
    slot = i % 2
    seg = functools.partial(_seg_copies, ssrc_ref, sdst_ref, sn_ref, hbm_ref=xs_out, tm=tm, to_hbm=True)

    def seg_wait(tile, s):
        _seg_wait(stot_ref[tile], sb_ref.at[s], xs_out, sems.at[s], True)

    @pl.when(i >= 2)
    def _():
        seg_wait(i - 2, slot)

    hlo, hhi = _unpack_rows(h_ref[...])
    hb = jnp.concatenate([hlo.astype(bf16), hhi.astype(bf16)], axis=1)
    lpv = lp_ref[...]
    half = hb.shape[1] // 2
    for c in range(lmax // SORT_CHUNK):
        jj = (lax.broadcasted_iota(i32, (SORT_CHUNK, tm), 0) + c * SORT_CHUNK).astype(f32)
        onehot = jnp.zeros((SORT_CHUNK, tm), f32)
        for k in range(TOP_K):
            onehot = jnp.where(jj == lpv[k:k + 1, :], 1.0, onehot)
        rows = _dot(onehot.astype(bf16), hb)
        lo = lax.shift_right_logical(lax.bitcast_convert_type(rows[:, :half], u32), jnp.uint32(16))
        hi = lax.bitcast_convert_type(rows[:, half:], u32) & jnp.uint32(0xFFFF0000)
        sb_ref[slot, c * SORT_CHUNK:(c + 1) * SORT_CHUNK, :] = lo | hi

    seg(i, buf_ref=sb_ref.at[slot], sem=sems.at[slot], start=True)

    gt = _dot(hb, wsg_ref[...])
    up = _dot(hb, wsu_ref[...])
    ysh_o[...] = _dot(((gt * _sigmoid(gt)) * up).astype(bf16), wsd_ref[...]).astype(ysh_o.dtype)

    @pl.when(i == last)
    def _():
        z_ref[...] = jnp.zeros(z_ref.shape, u32)
        _zero_fill(cnt_ref, pstart_ref, nused_ref, z_ref, xs_out, sem_z, blk, nblk, True)
        _zero_fill(cnt_ref, pstart_ref, nused_ref, z_ref, xs_out, sem_z, blk, nblk, False)
        seg_wait(i, slot)

        @pl.when(i >= 1)
        def _():
            seg_wait(i - 1, 1 - slot)


def _dispatch(seg_src, seg_dst, seg_n, seg_tot, counts, pad_start, nused, lp_t, h2p, wsg, wsu, wsd,
              P, blk, tm, lmax):
    T, DW = h2p.shape
    D, DS = wsg.shape
    assert blk & (blk - 1) == 0 and tm & (tm - 1) == 0 and lmax % SORT_CHUNK == 0
    kern = functools.partial(_dispatch_kernel, blk=blk, nblk=P // blk)
    const = lambda shape: pl.BlockSpec(shape, lambda i, *_: (0, 0))
    grid_spec = pltpu.PrefetchScalarGridSpec(
        num_scalar_prefetch=7,
        grid=(T // tm,),
        in_specs=[pl.BlockSpec((TOP_K, tm), lambda i, *_: (0, i)),
                  pl.BlockSpec((tm, DW), lambda i, *_: (i, 0)),
                  const((D, DS)), const((D, DS)), const((DS, D))],
        out_specs=[pl.BlockSpec(memory_space=pl.ANY), pl.BlockSpec((tm, D), lambda i, *_: (i, 0))],
        scratch_shapes=[pltpu.VMEM((2, lmax, DW), u32), pltpu.VMEM((blk, DW), u32),
                        pltpu.SemaphoreType.DMA((2,)), pltpu.SemaphoreType.DMA],
    )
    return pl.pallas_call(
        kern,
        grid_spec=grid_spec,
        out_shape=[jax.ShapeDtypeStruct((P, DW), u32), jax.ShapeDtypeStruct((T, D), bf16)],
        compiler_params=_cparams(("arbitrary",)),
        name="moe_dispatch",
    )(seg_src, seg_dst, seg_n, seg_tot, counts, pad_start, nused, lp_t, h2p, wsg, wsu, wsd)


def _experts_kernel(blk_e_ref, nxt_e_ref, nused_ref, xs_ref, wg_hbm, wu_hbm, wd_hbm, ys_ref,
                    wg_f, wu_f, wd_f, wg_s, wu_s, wd_s, sems, *, layer):
    i = pl.program_id(0)
    e = blk_e_ref[i]
    changed = jnp.logical_or(i == 0, e != blk_e_ref[jnp.maximum(i - 1, 0)])

    def weight_copies(ex):
        return (pltpu.make_async_copy(wg_hbm.at[layer, ex], wg_f, sems.at[0]),
                pltpu.make_async_copy(wu_hbm.at[layer, ex], wu_f, sems.at[1]),
                pltpu.make_async_copy(wd_hbm.at[layer, ex], wd_f, sems.at[2]))

    @pl.when(i == 0)
    def _():
        for cp in weight_copies(e):
            cp.start()

    @pl.when(changed)
    def _():
        for cp in weight_copies(e):
            cp.wait()
        for src, dst in ((wg_f, wg_s), (wu_f, wu_s), (wd_f, wd_s)):
            rows = src.shape[0] // 8
            for c in range(8):
                dst[c * rows:(c + 1) * rows, :] = src[c * rows:(c + 1) * rows, :].astype(bf16)
        nxt = nxt_e_ref[i]

        @pl.when(nxt >= 0)
        def _():
            for cp in weight_copies(nxt):
                cp.start()

    @pl.when(i < nused_ref[0])
    def _():
        lo, hi = _unpack_rows(xs_ref[...])
        x = jnp.concatenate([lo.astype(bf16), hi.astype(bf16)], axis=1)
        gt = _dot(x, wg_s[...])
        up = _dot(x, wu_s[...])
        hmid = (gt * _sigmoid(gt)) * up
        ys_ref[...] = _pack_rows(_dot(hmid.astype(bf16), wd_s[...]))

    @pl.when(i >= nused_ref[0])
    def _():
        ys_ref[...] = jnp.zeros(ys_ref.shape, u32)


def _experts(blk_e, nxt_e, nused, xs, w_gate, w_up, w_down, layer, blk):
    P, DW = xs.shape
    D, DE = w_gate.shape[-2:]
    nblk = P // blk
    row_idx = lambda i, be, nx, nu: (jnp.minimum(i, nu[0] - 1), 0)
    hbm = pl.BlockSpec(memory_space=pl.ANY)
    grid_spec = pltpu.PrefetchScalarGridSpec(
        num_scalar_prefetch=3,
        grid=(nblk,),
        in_specs=[pl.BlockSpec((blk, DW), row_idx), hbm, hbm, hbm],
        out_specs=pl.BlockSpec((blk, DW), lambda i, be, nx, nu: (i, 0)),
        scratch_shapes=[pltpu.VMEM((D, DE), f32), pltpu.VMEM((D, DE), f32), pltpu.VMEM((DE, D), f32),
                        pltpu.VMEM((D, DE), bf16), pltpu.VMEM((D, DE), bf16), pltpu.VMEM((DE, D), bf16),
                        pltpu.SemaphoreType.DMA((3,))],
    )
    return pl.pallas_call(
        functools.partial(_experts_kernel, layer=layer),
        grid_spec=grid_spec,
        out_shape=jax.ShapeDtypeStruct((P, DW), u32),
        compiler_params=_cparams(("arbitrary",)),
        name="moe_experts",
    )(blk_e, nxt_e, nused, xs, w_gate, w_up, w_down)


def _combine_kernel(ssrc_ref, sdst_ref, sn_ref, stot_ref, lp_ref, gate_ref, ys_hbm, ysh_ref, x1_ref, gf_ref,
                    gpost_ref, x2_o, yb_ref, sems):
    i = pl.program_id(0)
    n = pl.num_programs(0)
    tm = x1_ref.shape[0]
    lmax = yb_ref.shape[1]
    slot = i % 2
    seg = functools.partial(_seg_copies, ssrc_ref, sdst_ref, sn_ref, hbm_ref=ys_hbm, tm=tm, to_hbm=False)

    @pl.when(i == 0)
    def _():
        yb_ref[...] = jnp.zeros(yb_ref.shape, u32)
        seg(i, buf_ref=yb_ref.at[slot], sem=sems.at[slot], start=True)

    @pl.when(i + 1 < n)
    def _():
        seg(i + 1, buf_ref=yb_ref.at[1 - slot], sem=sems.at[1 - slot], start=True)

    _seg_wait(stot_ref[i], yb_ref.at[slot], ys_hbm, sems.at[slot], False)

    lp = lp_ref[...]
    gate = gate_ref[...]
    ysh = ysh_ref[...].astype(f32)
    half = ysh.shape[1] // 2
    lo = ysh[:, :half]
    hi = ysh[:, half:]
    for c in range(lmax // SORT_CHUNK):
        jl = (lax.broadcasted_iota(i32, (tm, SORT_CHUNK), 1) + c * SORT_CHUNK).astype(f32)
        g = jnp.zeros((tm, SORT_CHUNK), f32)
        for k in range(TOP_K):
            g = jnp.where(jl == lp[:, k:k + 1], gate[:, k:k + 1], g)
        gb = g.astype(bf16)
        a, b = _unpack_rows(yb_ref[slot, c * SORT_CHUNK:(c + 1) * SORT_CHUNK, :])
        lo = lo + _dot(gb, a.astype(bf16))
        hi = hi + _dot(gb, b.astype(bf16))
    y = jnp.concatenate([lo, hi], axis=1)
    ms = jnp.mean(y * y, axis=-1, keepdims=True)
    x2_o[...] = x1_ref[...] + gf_ref[0] * (y * lax.rsqrt(ms + NORM_EPS) * gpost_ref[...])


def _combine(seg_src, seg_dst, seg_n, seg_tot, lp_tk, gate_tk, ys, ysh, x1, mod3, g_post, S, tm, lmax):
    T, D = x1.shape
    DW = ys.shape[1]
    tpb = S // tm
    tile = pl.BlockSpec((tm, D), lambda i, *_: (i, 0))
    ktile = pl.BlockSpec((tm, TOP_K), lambda i, *_: (i, 0))
    grid_spec = pltpu.PrefetchScalarGridSpec(
        num_scalar_prefetch=4,
        grid=(T // tm,),
        in_specs=[ktile, ktile,
                  pl.BlockSpec(memory_space=pl.ANY),
                  tile, tile,
                  pl.BlockSpec((1, 1, D), lambda i, *_: ((i // tpb) * N_MOD + 5, 0, 0)),
                  pl.BlockSpec((1, D), lambda i, *_: (0, 0))],
        out_specs=tile,
        scratch_shapes=[pltpu.VMEM((2, lmax, DW), u32), pltpu.SemaphoreType.DMA((2,))],
    )
    return pl.pallas_call(
        _combine_kernel,
        grid_spec=grid_spec,
        out_shape=jax.ShapeDtypeStruct((T, D), f32),
        compiler_params=_cparams(("arbitrary",)),
        name="moe_combine",
    )(seg_src, seg_dst, seg_n, seg_tot, lp_tk, gate_tk, ys, ysh, x1, mod3, g_post.reshape(1, D))


def _tile(n, pref):
    t = min(n, pref)
    assert n % t == 0, (n, t)
    return t


def _layer(i, x2, mod3, p, wexp, v_first, B, S, cfg):
    T, D = x2.shape
    H = (D // 2) // ATT_V_DIM
    W = D - D // 2
    att_cols = 2 * H * 2 * ATT_QK_DIM + H * ATT_V_DIM
    lam_init = 0.8 - 0.6 * math.exp(-0.3 * i)

    w_in_bf = p["w_in"].astype(bf16)
    h1 = _prenorm(x2, mod3, p["g_pre_mix"], S, _tile(S, cfg["tm_norm"]), 1, 0)
    att = _inproj(h1, w_in_bf[:, :att_cols], bf16, _tile(T, cfg["tm_in"]), cfg["tn_att"])
    feats = _inproj(h1, w_in_bf[:, att_cols:], f32, _tile(T, cfg["tm_in_rwkv"]), cfg["tn_rwkv"])

    slopes = jnp.broadcast_to(
        (2.0 ** (-ALIBI_MAX_BIAS * jnp.arange(1, H + 1, dtype=f32) / H))[:, None, None], (H, 1, LANES))
    lamp = jnp.stack([p["lam_q1"], p["lam_k1"], p["lam_q2"], p["lam_k2"]])
    o_att = _attention(att, slopes, lamp, p["att_subln_g"], B, S, H, lam_init, _tile(S, cfg["tq"]), cfg["hp"])

    cols = feats.shape[1]
    zw = jnp.zeros((RWKV_A_RANK, W), f32)
    heads = W // RWKV_HEAD
    ind = (jnp.arange(W)[:, None] // RWKV_HEAD == jnp.arange(LANES)[None, :]).astype(bf16)
    prm = {
        "mu": p["rwkv_mu"].reshape(1, cols), "w0": p["rwkv_w0"].reshape(1, W),
        "w2p": jnp.concatenate([p["rwkv_w2"], zw], axis=0),
        "a0": p["rwkv_a0"].reshape(1, W),
        "a2p": jnp.concatenate([jnp.zeros((RWKV_W_RANK, W), f32), p["rwkv_a2"]], axis=0),
        "g2": p["rwkv_g2"], "k_k": p["rwkv_k_k"].reshape(1, W), "k_a": p["rwkv_k_a"].reshape(1, W),
        "ind": ind, "indt": ind.T,
    }
    if v_first is not None:
        padc = LANES - RWKV_V_RANK
        prm["v0"] = p["rwkv_v0"].reshape(1, W)
        prm["v1p"] = jnp.pad(p["rwkv_v1"], ((0, 0), (0, padc)))
        prm["v2p"] = jnp.pad(p["rwkv_v2"], ((0, padc), (0, 0)))
    r, lw, kh, v, kn, kb, g = _rwkv_prep(feats, prm, v_first, B, S, W, _tile(S, cfg["tm_prep"]))
    if v_first is None:
        v_first = v
    r2, y0, mmat, g0 = _wkv_intra(r, lw, kh, v, kn, kb, min(cfg["nc"], S // CHUNK), cfg["passes_intra"])
    o_rwkv = _wkv_state(r2, y0, mmat, g0, r, kh, v, g, p["rwkv_lnx_w"].reshape(1, W),
                        p["rwkv_lnx_b"].reshape(1, W), p["rwkv_r_k"].reshape(1, W), B, S,
                        min(cfg["pg"], W // LANES), min(cfg["cb"], S // CHUNK), cfg["passes_state"])
    del heads

    w_out_bf = p["w_out"].astype(bf16)
    x1, h2, logits_t = _postmix(o_att, o_rwkv, w_out_bf[:D // 2], w_out_bf[D // 2:], x2, mod3,
                                p["g_post_mix"], p["g_pre_ffn"], p["w_router"].T, S, _tile(S, cfg["tm_post"]))

    tm_t = _tile(T, cfg["tm_tile"])
    ntiles = T // tm_t
    eidx_t, gate_t, lrank_t, cnt_tbl = _router(logits_t, p["router_bias"], tm_t)
    blk = cfg["blk"]
    run = (cnt_tbl[:, :ntiles].T + SUBLANES - 1) // SUBLANES * SUBLANES
    counts = jnp.sum(run, axis=0)
    padded = (counts + blk - 1) // blk * blk
    pad_end = jnp.cumsum(padded)
    pad_start = pad_end - padded
    e_ids = jnp.arange(N_EXPERTS, dtype=i32)
    tile_off = jnp.cumsum(run, axis=0) - run
    loc_off = jnp.cumsum(run, axis=1) - run
    seg_dst = (pad_start[None, :] + tile_off).reshape(-1).astype(i32)
    seg_src = loc_off.reshape(-1).astype(i32)
    seg_n = run.reshape(-1).astype(i32)
    seg_tot = jnp.sum(run, axis=1).astype(i32)
    loc_tok = jnp.repeat(loc_off, tm_t, axis=0)
    lp_t = jnp.sum(jnp.where(eidx_t[:, :, None] == e_ids, loc_tok[None], 0), axis=-1).astype(f32) + lrank_t
    lmax = -(-(tm_t * TOP_K + N_EXPERTS * (SUBLANES - 1)) // SORT_CHUNK) * SORT_CHUNK
    nblk = -(-(T * TOP_K + ntiles * N_EXPERTS * (SUBLANES - 1)) // blk) + N_EXPERTS
    P = nblk * blk
    blk_start = jnp.arange(nblk, dtype=i32) * blk
    nused = (pad_end[-1] // blk).astype(i32).reshape(1)
    blk_pos = jnp.minimum(blk_start, pad_end[-1] - blk)
    blk_e = jnp.minimum(jnp.sum((pad_end[None, :] <= blk_pos[:, None]).astype(i32), axis=1), N_EXPERTS - 1)
    cand = jnp.where(counts > 0, e_ids, N_EXPERTS)
    later = jnp.where(e_ids[None, :] > blk_e[:, None], cand[None, :], N_EXPERTS)
    nxt_e = jnp.min(later, axis=1)
    nxt_e = jnp.where(nxt_e >= N_EXPERTS, -1, nxt_e).astype(i32)

    xs, ysh = _dispatch(seg_src, seg_dst, seg_n, seg_tot, counts.astype(i32), pad_start.astype(i32), nused, lp_t, h2,
                        p["w_sh_gate"].astype(bf16), p["w_sh_up"].astype(bf16), p["w_sh_down"].astype(bf16),
                        P, blk, tm_t, lmax)
    ys = _experts(blk_e, nxt_e, nused, xs, wexp[0], wexp[1], wexp[2], i, blk)
    x_out = _combine(seg_src, seg_dst, seg_n, seg_tot, lp_t.T, gate_t.T, ys, ysh, x1,
                     mod3, p["g_post_ffn"], S, tm_t, lmax)
    return x_out, v_first


_CFG = dict(tm_norm=512, tm_in=2048, tn_att=1024, tm_in_rwkv=1024, tn_rwkv=1664, tq=512, hp=2, tm_prep=256, nc=8, passes_intra=1, passes_state=1, pg=4, cb=8,
            tm_post=512, blk=256, tm_tile=256)

_LAYER_KEYS = ("g_pre_mix", "g_post_mix", "g_pre_ffn", "g_post_ffn", "w_in", "w_out", "lam_q1", "lam_k1",
               "lam_q2", "lam_k2", "att_subln_g", "rwkv_mu", "rwkv_w0", "rwkv_w2", "rwkv_a0", "rwkv_a2",
               "rwkv_g2", "rwkv_k_k", "rwkv_k_a", "rwkv_r_k", "rwkv_lnx_w", "rwkv_lnx_b", "w_router",
               "router_bias", "w_sh_gate", "w_sh_up", "w_sh_down")


def _forward(x, c, params, cfg):
    B, S, D = x.shape
    L = params["w_in"].shape[0]
    bp = 16
    c_pad = jnp.zeros((bp, D), f32).at[:B].set(c)
    mod = _ada_mod(c_pad, params["w_ada"], params["b_ada"])
    x2 = x.reshape(B * S, D)
    v_first = None
    for i in range(L):
        p = {k: params[k][i] for k in _LAYER_KEYS}
        if i > 0:
            p["rwkv_v0"] = params["rwkv_v0"][i - 1]
            p["rwkv_v1"] = params["rwkv_v1"][i - 1]
            p["rwkv_v2"] = params["rwkv_v2"][i - 1]
        mod3 = mod[i, :B].reshape(B * N_MOD, 1, D)
        wexp = (params["w_exp_gate"], params["w_exp_up"], params["w_exp_down"])
        x2, v_first = _layer(i, x2, mod3, p, wexp, v_first, B, S, cfg)
    return x2.reshape(B, S, D)


def kernel(x, c, w_ada, b_ada, g_pre_mix, g_post_mix, g_pre_ffn, g_post_ffn, w_in, w_out, lam_q1, lam_k1, lam_q2, lam_k2, att_subln_g, rwkv_mu, rwkv_w0, rwkv_w2, rwkv_a0, rwkv_a2, rwkv_g2, rwkv_k_k, rwkv_k_a, rwkv_r_k, rwkv_lnx_w, rwkv_lnx_b, rwkv_v0, rwkv_v1, rwkv_v2, w_router, router_bias, w_exp_gate, w_exp_up, w_exp_down, w_sh_gate, w_sh_up, w_sh_down):
    params = dict(w_ada=w_ada, b_ada=b_ada, g_pre_mix=g_pre_mix, g_post_mix=g_post_mix, g_pre_ffn=g_pre_ffn,
                  g_post_ffn=g_post_ffn, w_in=w_in, w_out=w_out, lam_q1=lam_q1, lam_k1=lam_k1, lam_q2=lam_q2,
                  lam_k2=lam_k2, att_subln_g=att_subln_g, rwkv_mu=rwkv_mu, rwkv_w0=rwkv_w0, rwkv_w2=rwkv_w2,
                  rwkv_a0=rwkv_a0, rwkv_a2=rwkv_a2, rwkv_g2=rwkv_g2, rwkv_k_k=rwkv_k_k, rwkv_k_a=rwkv_k_a,
                  rwkv_r_k=rwkv_r_k, rwkv_lnx_w=rwkv_lnx_w, rwkv_lnx_b=rwkv_lnx_b, rwkv_v0=rwkv_v0,
                  rwkv_v1=rwkv_v1, rwkv_v2=rwkv_v2, w_router=w_router, router_bias=router_bias,
                  w_exp_gate=w_exp_gate, w_exp_up=w_exp_up, w_exp_down=w_exp_down, w_sh_gate=w_sh_gate,
                  w_sh_up=w_sh_up, w_sh_down=w_sh_down)
    return _forward(x, c, params, _CFG)
```

```python
import functools
import math

import jax
import jax.numpy as jnp
from jax import lax
from jax.experimental import pallas as pl
from jax.experimental.pallas import tpu as pltpu

f32 = jnp.float32
bf16 = jnp.bfloat16
i32 = jnp.int32
u32 = jnp.uint32

ATT_QK_DIM = 64
ATT_V_DIM = 128
ALIBI_MAX_BIAS = 8.0
ATT_SUBLN_EPS = 1e-5
RWKV_HEAD = 64
RWKV_W_RANK = 64
RWKV_A_RANK = 64
RWKV_G_RANK = 128
RWKV_V_RANK = 32
RWKV_GN_EPS = 64e-5
N_EXPERTS = 64
N_GROUPS = 8
TOPK_GROUPS = 4
TOP_K = 8
ROUTED_SCALE = 2.5
NORM_EPS = 1e-6
N_MOD = 6

LANES = 128
SUBLANES = 8
CHUNK = 64
VMEM_LIMIT = 56 * 1024 * 1024


def _cparams(sem):
    return pltpu.CompilerParams(dimension_semantics=sem, vmem_limit_bytes=VMEM_LIMIT)


def _dot(a, b):
    return jnp.dot(a, b, preferred_element_type=f32)


def _dot_nt(a, b):
    return lax.dot_general(a, b, (((1,), (1,)), ((), ())), preferred_element_type=f32)


def _split2(x):
    hi = x.astype(bf16)
    lo = (x - hi.astype(f32)).astype(bf16)
    return hi, lo


def _mm(a, b, passes=1):
    if passes == 1:
        return _dot(a.astype(bf16), b.astype(bf16))
    ah, al = _split2(a.astype(f32))
    bh, bl = _split2(b.astype(f32))
    return (_dot(al, bh) + _dot(ah, bl)) + _dot(ah, bh)


def _mm_exact_rhs(a, b_bf16):
    ah, al = _split2(a)
    return _dot(al, b_bf16) + _dot(ah, b_bf16)


def _sigmoid(x):
    return 1.0 / (1.0 + jnp.exp(-x))


def _pack_rows(x):
    half = x.shape[1] // 2
    a = x[:, :half].astype(bf16).astype(f32)
    b = x[:, half:].astype(bf16).astype(f32)
    lo = lax.shift_right_logical(lax.bitcast_convert_type(a, u32), jnp.uint32(16))
    hi = lax.bitcast_convert_type(b, u32) & jnp.uint32(0xFFFF0000)
    return lo | hi


def _unpack_rows(w):
    lo = lax.bitcast_convert_type(lax.shift_left(w, jnp.uint32(16)), f32)
    hi = lax.bitcast_convert_type(w & jnp.uint32(0xFFFF0000), f32)
    return lo, hi


def _ada_kernel(c_ref, w_ref, b_ref, o_ref):
    c = c_ref[...]
    cond = (c * _sigmoid(c)).astype(bf16)
    o_ref[0] = _dot(cond, w_ref[0].astype(bf16)) + b_ref[0]


def _ada_mod(c_pad, w_ada, b_ada, tn=1024):
    L, D, N = w_ada.shape
    bp = c_pad.shape[0]
    return pl.pallas_call(
        _ada_kernel,
        grid=(L, N // tn),
        in_specs=[pl.BlockSpec((bp, D), lambda l, j: (0, 0)),
                  pl.BlockSpec((1, D, tn), lambda l, j: (l, 0, j)),
                  pl.BlockSpec((1, 1, tn), lambda l, j: (l, 0, j))],
        out_specs=pl.BlockSpec((1, bp, tn), lambda l, j: (l, 0, j)),
        out_shape=jax.ShapeDtypeStruct((L, bp, N), f32),
        compiler_params=_cparams(("arbitrary", "arbitrary")),
        name="ada_mod",
    )(c_pad, w_ada, b_ada.reshape(L, 1, N))


def _prenorm_kernel(x_ref, sc_ref, sh_ref, g_ref, h_ref):
    x = x_ref[...]
    ms = jnp.mean(x * x, axis=-1, keepdims=True)
    y = x * lax.rsqrt(ms + NORM_EPS) * g_ref[...]
    h_ref[...] = (y * (1.0 + sc_ref[0]) + sh_ref[0]).astype(h_ref.dtype)


def _prenorm(x2, mod3, g, S, tm, seg_sc, seg_sh):
    T, D = x2.shape
    tpb = S // tm
    tile = pl.BlockSpec((tm, D), lambda i: (i, 0))
    return pl.pallas_call(
        _prenorm_kernel,
        grid=(T // tm,),
        in_specs=[tile,
                  pl.BlockSpec((1, 1, D), lambda i: ((i // tpb) * N_MOD + seg_sc, 0, 0)),
                  pl.BlockSpec((1, 1, D), lambda i: ((i // tpb) * N_MOD + seg_sh, 0, 0)),
                  pl.BlockSpec((1, D), lambda i: (0, 0))],
        out_specs=tile,
        out_shape=jax.ShapeDtypeStruct((T, D), bf16),
        compiler_params=_cparams(("arbitrary",)),
        name="prenorm",
    )(x2, mod3, mod3, g.reshape(1, D))


def _inproj_kernel(h_ref, w_ref, o_ref):
    o_ref[...] = _dot(h_ref[...], w_ref[...]).astype(o_ref.dtype)


def _inproj(h, w_bf, out_dtype, tm, tn):
    T, D = h.shape
    N = w_bf.shape[1]
    return pl.pallas_call(
        _inproj_kernel,
        grid=(T // tm, N // tn),
        in_specs=[pl.BlockSpec((tm, D), lambda i, j: (i, 0)),
                  pl.BlockSpec((D, tn), lambda i, j: (0, j))],
        out_specs=pl.BlockSpec((tm, tn), lambda i, j: (i, j)),
        out_shape=jax.ShapeDtypeStruct((T, N), out_dtype),
        compiler_params=_cparams(("arbitrary", "arbitrary")),
        name="inproj",
    )(h, w_bf)


def _attn_kernel(q_ref, k_ref, v_ref, slope_ref, lamp_ref, g_ref, o_ref,
                 q2t_ref, vt_ref, m_ref, l_ref, acc_ref, *, tq, hp, lam_init):
    qi = pl.program_id(2)
    scale = ATT_QK_DIM ** -0.5
    heads = range(hp)
    hl = [slice(h * LANES, (h + 1) * LANES) for h in heads]
    slope = [slope_ref[h][:, 0:1] for h in heads]

    @pl.when(qi == 0)
    def _():
        for h in heads:
            vt_ref[h] = v_ref[:, hl[h]].astype(f32).T.astype(bf16)

    dim = lax.broadcasted_iota(i32, (LANES, 1), 0)
    first = dim < ATT_QK_DIM
    for h in heads:
        qt = (q_ref[:, hl[h]].astype(f32) * scale).T
        q2t_ref[h, :, 0:tq] = jnp.where(first, qt, 0.0).astype(bf16)
        q2t_ref[h, :, tq:2 * tq] = jnp.where(first, 0.0, qt).astype(bf16)
    m_ref[...] = jnp.full(m_ref.shape, -jnp.inf, f32)
    l_ref[...] = jnp.zeros(l_ref.shape, f32)
    acc_ref[...] = jnp.zeros(acc_ref.shape, f32)

    kr = lax.broadcasted_iota(i32, (tq, 2 * tq), 0)
    qc = lax.broadcasted_iota(i32, (tq, 2 * tq), 1)
    causal = jnp.where(qc >= tq, qc - tq, qc) >= kr
    krow = lax.broadcasted_iota(i32, (tq, 1), 0).astype(f32)

    def step(ki, masked):
        start = pl.multiple_of(ki * tq, tq)
        kpos = krow + (ki * tq).astype(f32)
        s = [_dot(k_ref[pl.ds(start, tq), hl[h]], q2t_ref[h]) for h in heads]
        s = [s[h] + slope[h] * kpos for h in heads]
        if masked:
            s = [jnp.where(causal, s[h], -jnp.inf) for h in heads]
        m_prev = [m_ref[h] for h in heads]
        m_new = [jnp.maximum(m_prev[h], jnp.max(s[h], axis=0, keepdims=True)) for h in heads]
        alpha = [jnp.exp(m_prev[h] - m_new[h]) for h in heads]
        p = [jnp.exp(s[h] - m_new[h]) for h in heads]
        for h in heads:
            l_ref[h] = alpha[h] * l_ref[h] + jnp.sum(p[h], axis=0, keepdims=True)
            acc_ref[h] = alpha[h] * acc_ref[h] + _dot(vt_ref[h, :, pl.ds(start, tq)], p[h].astype(bf16))
            m_ref[h] = m_new[h]

    def body(ki, carry):
        step(ki, False)
        return carry

    lax.fori_loop(0, qi, body, 0)
    step(qi, True)

    lp = lamp_ref[...]
    lam = (jnp.exp(jnp.sum(lp[0:1] * lp[1:2], axis=-1, keepdims=True))
           - jnp.exp(jnp.sum(lp[2:3] * lp[3:4], axis=-1, keepdims=True)) + lam_init)
    for h in heads:
        on = acc_ref[h] * (1.0 / l_ref[h])
        o = on[:, 0:tq] - lam * on[:, tq:2 * tq]
        o = o * lax.rsqrt(jnp.mean(o * o, axis=0, keepdims=True) + ATT_SUBLN_EPS)
        o = o * g_ref[...] * (1.0 - lam_init)
        o_ref[:, hl[h]] = o.T.astype(o_ref.dtype)


def _attention(att, slopes, lamp, subln_g, B, S, H, lam_init, tq, hp):
    T = att.shape[0]
    nq = S // tq
    kern = functools.partial(_attn_kernel, tq=tq, hp=hp, lam_init=lam_init)
    hw = hp * LANES
    return pl.pallas_call(
        kern,
        grid=(B, H // hp, nq),
        in_specs=[pl.BlockSpec((tq, hw), lambda b, g, q: (b * nq + q, g)),
                  pl.BlockSpec((S, hw), lambda b, g, q: (b, H // hp + g)),
                  pl.BlockSpec((S, hw), lambda b, g, q: (b, 2 * (H // hp) + g)),
                  pl.BlockSpec((hp, 1, LANES), lambda b, g, q: (g, 0, 0)),
                  pl.BlockSpec((4, ATT_QK_DIM), lambda b, g, q: (0, 0)),
                  pl.BlockSpec((ATT_V_DIM, 1), lambda b, g, q: (0, 0))],
        out_specs=pl.BlockSpec((tq, hw), lambda b, g, q: (b * nq + q, g)),
        out_shape=jax.ShapeDtypeStruct((T, H * ATT_V_DIM), bf16),
        scratch_shapes=[pltpu.VMEM((hp, LANES, 2 * tq), bf16),
                        pltpu.VMEM((hp, LANES, S), bf16),
                        pltpu.VMEM((hp, 1, 2 * tq), f32),
                        pltpu.VMEM((hp, 1, 2 * tq), f32),
                        pltpu.VMEM((hp, LANES, 2 * tq), f32)],
        compiler_params=_cparams(("arbitrary", "arbitrary", "arbitrary")),
        name="diff_attention",
    )(att, att, att, slopes, lamp, subln_g.reshape(ATT_V_DIM, 1))


def _head_sums(x, ind, indt):
    s = _mm_exact_rhs(x, ind)
    return _mm_exact_rhs(s, indt)


def _rwkv_prep_kernel(*refs, W, has_vres):
    if has_vres:
        (f_ref, mu_ref, w0_ref, w2_ref, a0_ref, a2_ref, g2_ref, kk_ref, ka_ref, ind_ref, indt_ref,
         vf_ref, v0_ref, v1_ref, v2_ref,
         r_o, lw_o, kh_o, v_o, kn_o, kb_o, g_o, carry_ref) = refs
    else:
        (f_ref, mu_ref, w0_ref, w2_ref, a0_ref, a2_ref, g2_ref, kk_ref, ka_ref, ind_ref, indt_ref,
         r_o, lw_o, kh_o, v_o, kn_o, kb_o, g_o, carry_ref) = refs

    ti = pl.program_id(1)
    h = f_ref[...]
    tm = h.shape[0]

    @pl.when(ti == 0)
    def _():
        carry_ref[...] = jnp.zeros(carry_ref.shape, f32)

    rolled = pltpu.roll(h, 1, axis=0)
    row = lax.broadcasted_iota(i32, (tm, 1), 0)
    prev = jnp.where(row == 0, carry_ref[...], rolled)
    carry_ref[...] = h[tm - 1:tm, :]
    feats = h + (prev - h) * mu_ref[...]

    r = feats[:, 0:W]
    k = feats[:, W:2 * W]
    v = feats[:, 2 * W:3 * W]
    wa = feats[:, 3 * W:3 * W + LANES]
    g_lo = feats[:, 3 * W + LANES:3 * W + 2 * LANES]

    w = w0_ref[...] + _mm(jnp.tanh(wa), w2_ref[...], passes=3)
    lw_o[...] = -math.exp(-0.5) * _sigmoid(w)
    a = _sigmoid(a0_ref[...] + _mm(wa, a2_ref[...], passes=3))
    g_o[...] = _mm(_sigmoid(g_lo), g2_ref[...]).astype(g_o.dtype)

    if has_vres:
        mix = _sigmoid(v0_ref[...] + _mm(_mm(v, v1_ref[...]), v2_ref[...]))
        v = v + (vf_ref[...].astype(f32) - v) * mix

    kk = k * kk_ref[...]
    ss = _head_sums(kk * kk, ind_ref[...], indt_ref[...])
    kk = kk / jnp.maximum(jnp.sqrt(ss), 1e-12)
    r_o[...] = r.astype(r_o.dtype)
    kh_o[...] = (k * (1.0 + (a - 1.0) * ka_ref[...])).astype(kh_o.dtype)
    v_o[...] = v.astype(v_o.dtype)
    kn_o[...] = kk.astype(kn_o.dtype)
    kb_o[...] = (kk * a).astype(kb_o.dtype)


def _rwkv_prep(feats, prm, vfirst, B, S, W, tm):
    T, COLS = feats.shape
    tpb = S // tm
    has_vres = vfirst is not None
    row = lambda n: pl.BlockSpec((1, n), lambda b, t: (0, 0))
    full = lambda a: pl.BlockSpec(a.shape, lambda b, t: (0, 0))
    tile = pl.BlockSpec((tm, W), lambda b, t: (b * tpb + t, 0))
    args = [feats, prm["mu"], prm["w0"], prm["w2p"], prm["a0"], prm["a2p"], prm["g2"], prm["k_k"], prm["k_a"],
            prm["ind"], prm["indt"]]
    specs = [pl.BlockSpec((tm, COLS), lambda b, t: (b * tpb + t, 0)), row(COLS), row(W), full(prm["w2p"]),
             row(W), full(prm["a2p"]), full(prm["g2"]), row(W), row(W), full(prm["ind"]), full(prm["indt"])]
    if has_vres:
        args += [vfirst, prm["v0"], prm["v1p"], prm["v2p"]]
        specs += [tile, row(W), full(prm["v1p"]), full(prm["v2p"])]
    out_dtypes = (bf16, f32, bf16, bf16, bf16, bf16, bf16)
    kern = functools.partial(_rwkv_prep_kernel, W=W, has_vres=has_vres)
    return pl.pallas_call(
        kern,
        grid=(B, tpb),
        in_specs=specs,
        out_specs=[tile] * 7,
        out_shape=[jax.ShapeDtypeStruct((T, W), dt) for dt in out_dtypes],
        scratch_shapes=[pltpu.VMEM((1, COLS), f32)],
        compiler_params=_cparams(("arbitrary", "arbitrary")),
        name="rwkv_prep",
    )(*args)


def _wkv_chunks(rs, lws, ks, vs, kns, kbs, passes):
    C = CHUNK
    P2 = 2 * C
    n = range(len(rs))
    ri = lax.broadcasted_iota(i32, (C, C), 0)
    ci = lax.broadcasted_iota(i32, (C, C), 1)
    tri = (ci <= ri).astype(bf16)
    lane = lax.broadcasted_iota(i32, (1, LANES), 1)
    m0 = (lane < RWKV_HEAD).astype(f32)
    m1 = 1.0 - m0
    rr = lax.broadcasted_iota(i32, (P2, P2), 0)
    cc = lax.broadcasted_iota(i32, (P2, P2), 1)
    same = jnp.where(rr >= C, 1, 0) == jnp.where(cc >= C, 1, 0)
    strict = same & (cc < rr)
    incl = same & (cc <= rr)
    incl2 = jnp.concatenate([incl, incl], axis=1)
    eye = (rr == cc).astype(f32)
    zeros_p = jnp.zeros((P2, LANES), f32)
    zeros_c = jnp.zeros((C, LANES), f32)
    stack = lambda x: jnp.concatenate([x * m0, x * m1], axis=0)
    fold = lambda x: x[0:C] + x[C:2 * C]

    def cumsum(lw):
        h1 = lw.astype(bf16)
        r1 = lw - h1.astype(f32)
        h2 = r1.astype(bf16)
        h3 = (r1 - h2.astype(f32)).astype(bf16)
        return (_dot(tri, h3) + _dot(tri, h2)) + _dot(tri, h1)

    cum = [cumsum(lws[j]) for j in n]
    cum_c = [cum[j][C - 1:C, :] for j in n]
    at = [-kns[j] * jnp.exp(cum[j] - lws[j]) for j in n]
    rt = [rs[j] * jnp.exp(cum[j]) for j in n]
    einv = [jnp.exp(-cum[j]) for j in n]
    bt = [kbs[j] * einv[j] for j in n]
    kt = [ks[j] * einv[j] for j in n]
    eh = [jnp.exp(cum_c[j] - cum[j]) for j in n]
    bh = [kbs[j] * eh[j] for j in n]
    kh = [ks[j] * eh[j] for j in n]
    w_c = [jnp.exp(cum_c[j]) for j in n]
    abd = [stack(at[j]) for j in n]
    vst = [stack(vs[j]) for j in n]
    lhs = [jnp.concatenate([abd[j], stack(rt[j])], axis=0) for j in n]
    rhs = [jnp.concatenate([stack(bt[j]), stack(kt[j])], axis=0) for j in n]
    gram = [_mm_nt(lhs[j], rhs[j], passes) for j in n]
    lab = [jnp.where(strict, gram[j][0:P2, 0:P2], 0.0) for j in n]
    lak = [jnp.where(strict, gram[j][0:P2, P2:2 * P2], 0.0) for j in n]
    mrbk = [jnp.where(incl2, gram[j][P2:2 * P2, :], 0.0) for j in n]

    x0 = [_mm(lak[j], vst[j], passes) for j in n]
    tinv = [eye + lab[j] for j in n]
    lp = [_mm(lab[j], lab[j], passes) for j in n]
    n_sq = int(math.log2(C)) - 1
    for it in range(n_sq):
        if it < n_sq - 1:
            both = [_mm(lp[j], jnp.concatenate([lp[j], tinv[j]], axis=1), passes) for j in n]
            tinv = [tinv[j] + both[j][:, P2:2 * P2] for j in n]
            lp = [both[j][:, 0:P2] for j in n]
        else:
            tinv = [tinv[j] + _mm(lp[j], tinv[j], passes) for j in n]

    ta = [_mm(tinv[j], jnp.concatenate([abd[j], x0[j]], axis=1), passes) for j in n]
    rhs2 = [jnp.concatenate([ta[j], jnp.concatenate([zeros_p, vst[j]], axis=1)], axis=0) for j in n]
    z = [_mm(mrbk[j], rhs2[j], passes) for j in n]
    r2 = [rt[j] + fold(z[j][:, 0:LANES]) for j in n]
    y0 = [fold(z[j][:, LANES:2 * LANES]) for j in n]
    lhs3t = [jnp.concatenate([bh[j], kh[j]], axis=0).T for j in n]
    rhs3 = [jnp.concatenate([fold(ta[j]), jnp.concatenate([zeros_c, vs[j]], axis=1)], axis=0) for j in n]
    wmat = [_mm(lhs3t[j], rhs3[j], passes) for j in n]
    mmat = [jnp.where(same, wmat[j][:, 0:LANES], 0.0) + eye * w_c[j] for j in n]
    g0 = [jnp.where(same, wmat[j][:, LANES:2 * LANES], 0.0) for j in n]
    return r2, y0, mmat, g0


def _mm_nt(a, b, passes):
    if passes == 1:
        return _dot_nt(a.astype(bf16), b.astype(bf16))
    ah, al = _split2(a)
    bh, bl = _split2(b)
    return (_dot_nt(al, bh) + _dot_nt(ah, bl)) + _dot_nt(ah, bh)


def _wkv_intra_kernel(r_ref, lw_ref, k_ref, v_ref, kn_ref, kb_ref, r2_o, y0_o, m_o, g_o, *, nc, passes):
    C = CHUNK
    sls = [slice(c * C, (c + 1) * C) for c in range(nc)]
    take = lambda ref: [ref[sl, :].astype(f32) for sl in sls]
    r2, y0, mmat, g0 = _wkv_chunks(take(r_ref), take(lw_ref), take(k_ref), take(v_ref),
                                   take(kn_ref), take(kb_ref), passes)
    for c, sl in enumerate(sls):
        r2_o[sl, :] = r2[c].astype(r2_o.dtype)
        y0_o[sl, :] = y0[c].astype(y0_o.dtype)
        m_o[0, c] = mmat[c].astype(m_o.dtype)
        g_o[0, c] = g0[c]


def _wkv_intra(r, lw, kh, v, kn, kb, nc, passes):
    T, W = r.shape
    npair = W // LANES
    rows = nc * CHUNK
    tile = pl.BlockSpec((rows, LANES), lambda p, i: (i, p))
    mat = pl.BlockSpec((1, nc, LANES, LANES), lambda p, i: (p, i, 0, 0))
    kern = functools.partial(_wkv_intra_kernel, nc=nc, passes=passes)
    return pl.pallas_call(
        kern,
        grid=(npair, T // rows),
        in_specs=[tile] * 6,
        out_specs=[tile, tile, mat, mat],
        out_shape=[jax.ShapeDtypeStruct((T, W), bf16), jax.ShapeDtypeStruct((T, W), bf16),
                   jax.ShapeDtypeStruct((npair, T // CHUNK, LANES, LANES), bf16),
                   jax.ShapeDtypeStruct((npair, T // CHUNK, LANES, LANES), f32)],
        compiler_params=_cparams(("arbitrary", "arbitrary")),
        name="wkv_intra",
    )(r, lw, kh, v, kn, kb)


def _wkv_state_kernel(r2_ref, y0_ref, m_ref, g0_ref, r_ref, kh_ref, v_ref, g_ref,
                      lnw_ref, lnb_ref, rk_ref, o_ref, st_ref, y_ref, *, pg, cb, passes):
    C = CHUNK

    @pl.when(pl.program_id(2) == 0)
    def _():
        st_ref[...] = jnp.zeros(st_ref.shape, f32)

    pairs = range(pg)
    lanes = [slice(p * LANES, (p + 1) * LANES) for p in pairs]
    st = [st_ref[p] for p in pairs]
    for c in range(cb):
        rows = slice(c * C, (c + 1) * C)
        for p in pairs:
            y_ref[rows, lanes[p]] = (_mm(r2_ref[rows, lanes[p]], st[p], passes)
                                     + y0_ref[rows, lanes[p]].astype(f32))
        st = [_mm(m_ref[p, c], st[p], passes) + g0_ref[p, c] for p in pairs]
    for p in pairs:
        st_ref[p] = st[p]

    rr = lax.broadcasted_iota(i32, (LANES, LANES), 0)
    cc = lax.broadcasted_iota(i32, (LANES, LANES), 1)
    ones_bd = (jnp.where(rr >= RWKV_HEAD, 1, 0) == jnp.where(cc >= RWKV_HEAD, 1, 0)).astype(bf16)
    for p in pairs:
        y = y_ref[:, lanes[p]]
        mu = _mm_exact_rhs(y, ones_bd) * (1.0 / RWKV_HEAD)
        d = y - mu
        var = _mm_exact_rhs(d * d, ones_bd) * (1.0 / RWKV_HEAD)
        yn = d * lax.rsqrt(var + RWKV_GN_EPS) * lnw_ref[:, lanes[p]] + lnb_ref[:, lanes[p]]
        rk = r_ref[:, lanes[p]].astype(f32) * kh_ref[:, lanes[p]].astype(f32) * rk_ref[:, lanes[p]]
        bonus = _mm_exact_rhs(rk, ones_bd) * v_ref[:, lanes[p]]
        o_ref[:, lanes[p]] = ((yn + bonus) * g_ref[:, lanes[p]]).astype(o_ref.dtype)


def _wkv_state(r2, y0, mm, g0, r, kh, v, g, lnw, lnb, rk, B, S, pg, cb, passes):
    T, W = r.shape
    npair = W // LANES
    rows = cb * CHUNK
    steps = S // rows
    seq = pl.BlockSpec((rows, pg * LANES), lambda b, q, c: (b * steps + c, q))
    mat = pl.BlockSpec((pg, cb, LANES, LANES), lambda b, q, c: (q, b * steps + c, 0, 0))
    prow = pl.BlockSpec((1, pg * LANES), lambda b, q, c: (0, q))
    kern = functools.partial(_wkv_state_kernel, pg=pg, cb=cb, passes=passes)
    return pl.pallas_call(
        kern,
        grid=(B, npair // pg, steps),
        in_specs=[seq, seq, mat, mat, seq, seq, seq, seq, prow, prow, prow],
        out_specs=seq,
        out_shape=jax.ShapeDtypeStruct((T, W), bf16),
        scratch_shapes=[pltpu.VMEM((pg, LANES, LANES), f32), pltpu.VMEM((rows, pg * LANES), f32)],
        compiler_params=_cparams(("arbitrary", "arbitrary", "arbitrary")),
        name="wkv_state",
    )(r2, y0, mm, g0, r, kh, v, g, lnw, lnb, rk)


def _postmix_kernel(oa_ref, orw_ref, wa_ref, wr_ref, x_ref, ga_ref, gpost_ref, gpre_ref, sc_ref, sh_ref, wrt_ref,
                    x1_o, h2_o, lg_o):
    mixed = _dot(oa_ref[...], wa_ref[...]) + _dot(orw_ref[...], wr_ref[...])
    ms = jnp.mean(mixed * mixed, axis=-1, keepdims=True)
    x1 = x_ref[...] + ga_ref[0] * (mixed * lax.rsqrt(ms + NORM_EPS) * gpost_ref[...])
    x1_o[...] = x1
    ms1 = jnp.mean(x1 * x1, axis=-1, keepdims=True)
    h2 = (x1 * lax.rsqrt(ms1 + NORM_EPS) * gpre_ref[...]) * (1.0 + sc_ref[0]) + sh_ref[0]
    h2_o[...] = _pack_rows(h2)
    lg_o[...] = _mm_nt(wrt_ref[...], h2, 3)


def _postmix(o_att, o_rwkv, w_out_a, w_out_r, x2, mod3, g_post, g_pre, w_rt, S, tm):
    T, D = x2.shape
    WA = o_att.shape[1]
    WR = o_rwkv.shape[1]
    E = w_rt.shape[0]
    tpb = S // tm
    modspec = lambda seg: pl.BlockSpec((1, 1, D), lambda i: ((i // tpb) * N_MOD + seg, 0, 0))
    tile = pl.BlockSpec((tm, D), lambda i: (i, 0))
    return pl.pallas_call(
        _postmix_kernel,
        grid=(T // tm,),
        in_specs=[pl.BlockSpec((tm, WA), lambda i: (i, 0)),
                  pl.BlockSpec((tm, WR), lambda i: (i, 0)),
                  pl.BlockSpec((WA, D), lambda i: (0, 0), pipeline_mode=pl.Buffered(1)),
                  pl.BlockSpec((WR, D), lambda i: (0, 0), pipeline_mode=pl.Buffered(1)),
                  tile, modspec(2),
                  pl.BlockSpec((1, D), lambda i: (0, 0)),
                  pl.BlockSpec((1, D), lambda i: (0, 0)),
                  modspec(4), modspec(3),
                  pl.BlockSpec((E, D), lambda i: (0, 0))],
        out_specs=[tile, pl.BlockSpec((tm, D // 2), lambda i: (i, 0)),
                   pl.BlockSpec((E, tm), lambda i: (0, i))],
        out_shape=[jax.ShapeDtypeStruct((T, D), f32), jax.ShapeDtypeStruct((T, D // 2), u32),
                   jax.ShapeDtypeStruct((E, T), f32)],
        compiler_params=_cparams(("arbitrary",)),
        name="postmix",
    )(o_att, o_rwkv, w_out_a, w_out_r, x2, mod3, g_post.reshape(1, D), g_pre.reshape(1, D), mod3, mod3, w_rt)


def _first_max(x, iota, n):
    mx = jnp.max(x, axis=0, keepdims=True)
    idx = jnp.min(jnp.where(x == mx, iota, n), axis=0, keepdims=True)
    return mx, idx


def _router_kernel(lg_ref, bias_ref, eidx_o, gate_o, rank_o, cnt_o, cnt_ref):
    E = N_EXPERTS
    G = N_GROUPS
    per = E // G
    tm = lg_ref.shape[1]

    @pl.when(pl.program_id(0) == 0)
    def _():
        cnt_ref[...] = jnp.zeros(cnt_ref.shape, f32)

    scores = _sigmoid(lg_ref[...])
    biased = scores + bias_ref[...]
    neg = -jnp.inf

    iota_p = lax.broadcasted_iota(i32, (per, tm), 0).astype(f32)
    gs = []
    for g in range(G):
        xg = biased[g * per:(g + 1) * per, :]
        m1, i1 = _first_max(xg, iota_p, per)
        m2 = jnp.max(jnp.where(iota_p == i1, neg, xg), axis=0, keepdims=True)
        gs.append(m1 + m2)
    gsc = jnp.concatenate(gs, axis=0)
    iota_g = lax.broadcasted_iota(i32, (G, tm), 0).astype(f32)
    gsel = jnp.zeros((G, tm), f32)
    for _ in range(TOPK_GROUPS):
        _, gi = _first_max(gsc, iota_g, G)
        hit = iota_g == gi
        gsel = jnp.where(hit, 1.0, gsel)
        gsc = jnp.where(hit, neg, gsc)
    masked = jnp.concatenate(
        [jnp.where(gsel[g:g + 1, :] > 0.0, biased[g * per:(g + 1) * per, :], neg) for g in range(G)], axis=0)

    iota_e = lax.broadcasted_iota(i32, (E, tm), 0).astype(f32)
    sel = jnp.zeros((E, tm), f32)
    idxs, vals = [], []
    for _ in range(TOP_K):
        _, ei = _first_max(masked, iota_e, E)
        hit = iota_e == ei
        idxs.append(ei)
        vals.append(jnp.sum(jnp.where(hit, scores, 0.0), axis=0, keepdims=True))
        sel = jnp.where(hit, 1.0, sel)
        masked = jnp.where(hit, neg, masked)
    tot = vals[0]
    for vv in vals[1:]:
        tot = tot + vv
    eidx_o[...] = jnp.concatenate(idxs, axis=0).astype(i32)
    gate_o[...] = jnp.concatenate([vv / tot * ROUTED_SCALE for vv in vals], axis=0)

    rr = lax.broadcasted_iota(i32, (tm, tm), 0)
    cc = lax.broadcasted_iota(i32, (tm, tm), 1)
    before = (rr < cc).astype(bf16)
    pos = _dot(sel.astype(bf16), before)
    rank_o[...] = jnp.concatenate(
        [jnp.sum(jnp.where(iota_e == ei, pos, 0.0), axis=0, keepdims=True) for ei in idxs], axis=0)
    lane = lax.broadcasted_iota(i32, cnt_ref.shape, 1)
    cnt_ref[...] = jnp.where(lane == pl.program_id(0), jnp.sum(sel, axis=1, keepdims=True), cnt_ref[...])
    cnt_o[...] = cnt_ref[...].astype(i32)


def _router(logits_t, bias, tm):
    E, T = logits_t.shape
    assert T // tm <= LANES
    k_tile = pl.BlockSpec((TOP_K, tm), lambda i: (0, i))
    return pl.pallas_call(
        _router_kernel,
        grid=(T // tm,),
        in_specs=[pl.BlockSpec((E, tm), lambda i: (0, i)),
                  pl.BlockSpec((E, 1), lambda i: (0, 0))],
        out_specs=[k_tile, k_tile, k_tile, pl.BlockSpec((E, LANES), lambda i: (0, 0))],
        out_shape=[jax.ShapeDtypeStruct((TOP_K, T), i32), jax.ShapeDtypeStruct((TOP_K, T), f32),
                   jax.ShapeDtypeStruct((TOP_K, T), f32), jax.ShapeDtypeStruct((E, LANES), i32)],
        scratch_shapes=[pltpu.VMEM((E, LANES), f32)],
        compiler_params=_cparams(("arbitrary",)),
        name="router",
    )(logits_t, bias.reshape(E, 1))


def _row_copy(src_ref, s, dst_ref, d, sem):
    return pltpu.make_async_copy(src_ref.at[pl.ds(s, 1), :], dst_ref.at[pl.ds(d, 1), :], sem)


def _zero_fill(cnt_ref, pstart_ref, nused_ref, z_ref, xs_out, sem, blk, nblk, start):
    def act(cp):
        if start:
            cp.start()
        else:
            cp.wait()

    def per_expert(e, carry):
        c = cnt_ref[e]
        base = pstart_ref[e] + c
        npad = (blk - (c & (blk - 1))) & (blk - 1)
        head = (-base) & (SUBLANES - 1)

        def one_row(j, carry2):
            act(_row_copy(z_ref, 0, xs_out, base + j, sem))
            return carry2

        lax.fori_loop(0, head, one_row, 0)
        rem = npad - head
        aligned = base + head
        p = blk // 2
        while p >= SUBLANES:
            off = pl.multiple_of(aligned + (rem & ~(2 * p - 1)), SUBLANES)

            @pl.when((rem & p) != 0)
            def _(p=p, off=off):
                act(pltpu.make_async_copy(z_ref.at[pl.ds(0, p), :], xs_out.at[pl.ds(off, p), :], sem))

            p //= 2
        return carry

    lax.fori_loop(0, N_EXPERTS, per_expert, 0)

    def per_block(b, carry):
        act(pltpu.make_async_copy(z_ref, xs_out.at[pl.ds(pl.multiple_of(b * blk, blk), blk), :], sem))
        return carry

    lax.fori_loop(nused_ref[0], nblk, per_block, 0)


def _seg_copies(ssrc_ref, sdst_ref, sn_ref, tile, buf_ref, hbm_ref, sem, tm, to_hbm, start):
    def act(cp):
        if start:
            cp.start()
        else:
            cp.wait()

    def per_expert(e, carry):
        idx = tile * N_EXPERTS + e
        n = sn_ref[idx]
        src = ssrc_ref[idx]
        dst = sdst_ref[idx]
        p = tm
        while p >= SUBLANES:
            off = n & ~(2 * p - 1)

            @pl.when((n & p) != 0)
            def _(p=p, off=off):
                v = buf_ref.at[pl.ds(pl.multiple_of(src + off, SUBLANES), p), :]
                h = hbm_ref.at[pl.ds(pl.multiple_of(dst + off, SUBLANES), p), :]
                act(pltpu.make_async_copy(v, h, sem) if to_hbm else pltpu.make_async_copy(h, v, sem))

            p //= 2
        return carry

    lax.fori_loop(0, N_EXPERTS, per_expert, 0)


def _seg_wait(total_rows, buf_ref, hbm_ref, sem, to_hbm):
    p = 1 << (buf_ref.shape[0].bit_length() - 1)
    while p >= SUBLANES:
        @pl.when((total_rows & p) != 0)
        def _(p=p):
            v = buf_ref.at[pl.ds(0, p), :]
            h = hbm_ref.at[pl.ds(0, p), :]
            (pltpu.make_async_copy(v, h, sem) if to_hbm else pltpu.make_async_copy(h, v, sem)).wait()

        p //= 2


SORT_CHUNK = 512


def _dispatch_kernel(ssrc_ref, sdst_ref, sn_ref, stot_ref, cnt_ref, pstart_ref, nused_ref, lp_ref, h_ref,
                     wsg_ref, wsu_ref, wsd_ref, xs_out, ysh_o, sb_ref, z_ref, sems, sem_z, *, blk, nblk):
    i = pl.program_id(0)
    last = pl.num_programs(0) - 1
    tm = h_ref.shape[0]
    lmax = sb_ref.shape[1]
    slot = i % 2
    seg = functools.partial(_seg_copies, ssrc_ref, sdst_ref, sn_ref, hbm_ref=xs_out, tm=tm, to_hbm=True)

    def seg_wait(tile, s):
        _seg_wait(stot_ref[tile], sb_ref.at[s], xs_out, sems.at[s], True)

    @pl.when(i >= 2)
    def _():
        seg_wait(i - 2, slot)

    hlo, hhi = _unpack_rows(h_ref[...])
    hb = jnp.concatenate([hlo.astype(bf16), hhi.astype(bf16)], axis=1)
    lpv = lp_ref[...]
    half = hb.shape[1] // 2
    for c in range(lmax // SORT_CHUNK):
        jj = (lax.broadcasted_iota(i32, (SORT_CHUNK, tm), 0) + c * SORT_CHUNK).astype(f32)
        onehot = jnp.zeros((SORT_CHUNK, tm), f32)
        for k in range(TOP_K):
            onehot = jnp.where(jj == lpv[k:k + 1, :], 1.0, onehot)
        rows = _dot(onehot.astype(bf16), hb)
        lo = lax.shift_right_logical(lax.bitcast_convert_type(rows[:, :half], u32), jnp.uint32(16))
        hi = lax.bitcast_convert_type(rows[:, half:], u32) & jnp.uint32(0xFFFF0000)
        sb_ref[slot, c * SORT_CHUNK:(c + 1) * SORT_CHUNK, :] = lo | hi

    seg(i, buf_ref=sb_ref.at[slot], sem=sems.at[slot], start=True)

    gt = _dot(hb, wsg_ref[...])
    up = _dot(hb, wsu_ref[...])
    ysh_o[...] = _dot(((gt * _sigmoid(gt)) * up).astype(bf16), wsd_ref[...]).astype(ysh_o.dtype)

    @pl.when(i == last)
    def _():
        z_ref[...] = jnp.zeros(z_ref.shape, u32)
        _zero_fill(cnt_ref, pstart_ref, nused_ref, z_ref, xs_out, sem_z, blk, nblk, True)
        _zero_fill(cnt_ref, pstart_ref, nused_ref, z_ref, xs_out, sem_z, blk, nblk, False)
        seg_wait(i, slot)

        @pl.when(i >= 1)
        def _():
            seg_wait(i - 1, 1 - slot)


def _dispatch(seg_src, seg_dst, seg_n, seg_tot, counts, pad_start, nused, lp_t, h2p, wsg, wsu, wsd,
              P, blk, tm, lmax):
    T, DW = h2p.shape
    D, DS = wsg.shape
    assert blk & (blk - 1) == 0 and tm & (tm - 1) == 0 and lmax % SORT_CHUNK == 0
    kern = functools.partial(_dispatch_kernel, blk=blk, nblk=P // blk)
    const = lambda shape: pl.BlockSpec(shape, lambda i, *_: (0, 0))
    grid_spec = pltpu.PrefetchScalarGridSpec(
        num_scalar_prefetch=7,
        grid=(T // tm,),
        in_specs=[pl.BlockSpec((TOP_K, tm), lambda i, *_: (0, i)),
                  pl.BlockSpec((tm, DW), lambda i, *_: (i, 0)),
                  const((D, DS)), const((D, DS)), const((DS, D))],
        out_specs=[pl.BlockSpec(memory_space=pl.ANY), pl.BlockSpec((tm, D), lambda i, *_: (i, 0))],
        scratch_shapes=[pltpu.VMEM((2, lmax, DW), u32), pltpu.VMEM((blk, DW), u32),
                        pltpu.SemaphoreType.DMA((2,)), pltpu.SemaphoreType.DMA],
    )
    return pl.pallas_call(
        kern,
        grid_spec=grid_spec,
        out_shape=[jax.ShapeDtypeStruct((P, DW), u32), jax.ShapeDtypeStruct((T, D), bf16)],
        compiler_params=_cparams(("arbitrary",)),
        name="moe_dispatch",
    )(seg_src, seg_dst, seg_n, seg_tot, counts, pad_start, nused, lp_t, h2p, wsg, wsu, wsd)


def _experts_kernel(blk_e_ref, nxt_e_ref, nused_ref, xs_ref, wg_hbm, wu_hbm, wd_hbm, ys_ref,
                    wg_f, wu_f, wd_f, wg_s, wu_s, wd_s, sems, *, layer):
    i = pl.program_id(0)
    e = blk_e_ref[i]
    changed = jnp.logical_or(i == 0, e != blk_e_ref[jnp.maximum(i - 1, 0)])

    def weight_copies(ex):
        return (pltpu.make_async_copy(wg_hbm.at[layer, ex], wg_f, sems.at[0]),
                pltpu.make_async_copy(wu_hbm.at[layer, ex], wu_f, sems.at[1]),
                pltpu.make_async_copy(wd_hbm.at[layer, ex], wd_f, sems.at[2]))

    @pl.when(i == 0)
    def _():
        for cp in weight_copies(e):
            cp.start()

    @pl.when(changed)
    def _():
        for cp in weight_copies(e):
            cp.wait()
        for src, dst in ((wg_f, wg_s), (wu_f, wu_s), (wd_f, wd_s)):
            rows = src.shape[0] // 8
            for c in range(8):
                dst[c * rows:(c + 1) * rows, :] = src[c * rows:(c + 1) * rows, :].astype(bf16)
        nxt = nxt_e_ref[i]

        @pl.when(nxt >= 0)
        def _():
            for cp in weight_copies(nxt):
                cp.start()

    @pl.when(i < nused_ref[0])
    def _():
        lo, hi = _unpack_rows(xs_ref[...])
        x = jnp.concatenate([lo.astype(bf16), hi.astype(bf16)], axis=1)
        gt = _dot(x, wg_s[...])
        up = _dot(x, wu_s[...])
        hmid = (gt * _sigmoid(gt)) * up
        ys_ref[...] = _pack_rows(_dot(hmid.astype(bf16), wd_s[...]))

    @pl.when(i >= nused_ref[0])
    def _():
        ys_ref[...] = jnp.zeros(ys_ref.shape, u32)


def _experts(blk_e, nxt_e, nused, xs, w_gate, w_up, w_down, layer, blk):
    P, DW = xs.shape
    D, DE = w_gate.shape[-2:]
    nblk = P // blk
    row_idx = lambda i, be, nx, nu: (jnp.minimum(i, nu[0] - 1), 0)
    hbm = pl.BlockSpec(memory_space=pl.ANY)
    grid_spec = pltpu.PrefetchScalarGridSpec(
        num_scalar_prefetch=3,
        grid=(nblk,),
        in_specs=[pl.BlockSpec((blk, DW), row_idx), hbm, hbm, hbm],
        out_specs=pl.BlockSpec((blk, DW), lambda i, be, nx, nu: (i, 0)),
        scratch_shapes=[pltpu.VMEM((D, DE), f32), pltpu.VMEM((D, DE), f32), pltpu.VMEM((DE, D), f32),
                        pltpu.VMEM((D, DE), bf16), pltpu.VMEM((D, DE), bf16), pltpu.VMEM((DE, D), bf16),
                        pltpu.SemaphoreType.DMA((3,))],
    )
    return pl.pallas_call(
        functools.partial(_experts_kernel, layer=layer),
        grid_spec=grid_spec,
        out_shape=jax.ShapeDtypeStruct((P, DW), u32),
        compiler_params=_cparams(("arbitrary",)),
        name="moe_experts",
    )(blk_e, nxt_e, nused, xs, w_gate, w_up, w_down)


def _combine_kernel(ssrc_ref, sdst_ref, sn_ref, stot_ref, lp_ref, gate_ref, ys_hbm, ysh_ref, x1_ref, gf_ref,
                    gpost_ref, x2_o, yb_ref, sems):
    i = pl.program_id(0)
    n = pl.num_programs(0)
    tm = x1_ref.shape[0]
    lmax = yb_ref.shape[1]
    slot = i % 2
    seg = functools.partial(_seg_copies, ssrc_ref, sdst_ref, sn_ref, hbm_ref=ys_hbm, tm=tm, to_hbm=False)

    @pl.when(i == 0)
    def _():
        yb_ref[...] = jnp.zeros(yb_ref.shape, u32)
        seg(i, buf_ref=yb_ref.at[slot], sem=sems.at[slot], start=True)

    @pl.when(i + 1 < n)
    def _():
        seg(i + 1, buf_ref=yb_ref.at[1 - slot], sem=sems.at[1 - slot], start=True)

    _seg_wait(stot_ref[i], yb_ref.at[slot], ys_hbm, sems.at[slot], False)

    lp = lp_ref[...]
    gate = gate_ref[...]
    ysh = ysh_ref[...].astype(f32)
    half = ysh.shape[1] // 2
    lo = ysh[:, :half]
    hi = ysh[:, half:]
    for c in range(lmax // SORT_CHUNK):
        jl = (lax.broadcasted_iota(i32, (tm, SORT_CHUNK), 1) + c * SORT_CHUNK).astype(f32)
        g = jnp.zeros((tm, SORT_CHUNK), f32)
        for k in range(TOP_K):
            g = jnp.where(jl == lp[:, k:k + 1], gate[:, k:k + 1], g)
        gb = g.astype(bf16)
        a, b = _unpack_rows(yb_ref[slot, c * SORT_CHUNK:(c + 1) * SORT_CHUNK, :])
        lo = lo + _dot(gb, a.astype(bf16))
        hi = hi + _dot(gb, b.astype(bf16))
    y = jnp.concatenate([lo, hi], axis=1)
    ms = jnp.mean(y * y, axis=-1, keepdims=True)
    x2_o[...] = x1_ref[...] + gf_ref[0] * (y * lax.rsqrt(ms + NORM_EPS) * gpost_ref[...])


def _combine(seg_src, seg_dst, seg_n, seg_tot, lp_tk, gate_tk, ys, ysh, x1, mod3, g_post, S, tm, lmax):
    T, D = x1.shape
    DW = ys.shape[1]
    tpb = S // tm
    tile = pl.BlockSpec((tm, D), lambda i, *_: (i, 0))
    ktile = pl.BlockSpec((tm, TOP_K), lambda i, *_: (i, 0))
    grid_spec = pltpu.PrefetchScalarGridSpec(
        num_scalar_prefetch=4,
        grid=(T // tm,),
        in_specs=[ktile, ktile,
                  pl.BlockSpec(memory_space=pl.ANY),
                  tile, tile,
                  pl.BlockSpec((1, 1, D), lambda i, *_: ((i // tpb) * N_MOD + 5, 0, 0)),
                  pl.BlockSpec((1, D), lambda i, *_: (0, 0))],
        out_specs=tile,
        scratch_shapes=[pltpu.VMEM((2, lmax, DW), u32), pltpu.SemaphoreType.DMA((2,))],
    )
    return pl.pallas_call(
        _combine_kernel,
        grid_spec=grid_spec,
        out_shape=jax.ShapeDtypeStruct((T, D), f32),
        compiler_params=_cparams(("arbitrary",)),
        name="moe_combine",
    )(seg_src, seg_dst, seg_n, seg_tot, lp_tk, gate_tk, ys, ysh, x1, mod3, g_post.reshape(1, D))


def _tile(n, pref):
    t = min(n, pref)
    assert n % t == 0, (n, t)
    return t


def _layer(i, x2, mod3, p, wexp, v_first, B, S, cfg):
    T, D = x2.shape
    H = (D // 2) // ATT_V_DIM
    W = D - D // 2
    att_cols = 2 * H * 2 * ATT_QK_DIM + H * ATT_V_DIM
    lam_init = 0.8 - 0.6 * math.exp(-0.3 * i)

    h1 = _prenorm(x2, mod3, p["g_pre_mix"], S, _tile(S, cfg["tm_norm"]), 1, 0)
    att = _inproj(h1, p["w_in"][:, :att_cols].astype(bf16), bf16, _tile(T, cfg["tm_in"]), cfg["tn_att"])
    feats = _inproj(h1, p["w_in"][:, att_cols:].astype(bf16), f32, _tile(T, cfg["tm_in_rwkv"]), cfg["tn_rwkv"])

    slopes = jnp.broadcast_to(
        (2.0 ** (-ALIBI_MAX_BIAS * jnp.arange(1, H + 1, dtype=f32) / H))[:, None, None], (H, 1, LANES))
    lamp = jnp.stack([p["lam_q1"], p["lam_k1"], p["lam_q2"], p["lam_k2"]])
    o_att = _attention(att, slopes, lamp, p["att_subln_g"], B, S, H, lam_init, _tile(S, cfg["tq"]), cfg["hp"])

    cols = feats.shape[1]
    zw = jnp.zeros((RWKV_A_RANK, W), f32)
    heads = W // RWKV_HEAD
    ind = (jnp.arange(W)[:, None] // RWKV_HEAD == jnp.arange(LANES)[None, :]).astype(bf16)
    prm = {
        "mu": p["rwkv_mu"].reshape(1, cols), "w0": p["rwkv_w0"].reshape(1, W),
        "w2p": jnp.concatenate([p["rwkv_w2"], zw], axis=0),
        "a0": p["rwkv_a0"].reshape(1, W),
        "a2p": jnp.concatenate([jnp.zeros((RWKV_W_RANK, W), f32), p["rwkv_a2"]], axis=0),
        "g2": p["rwkv_g2"], "k_k": p["rwkv_k_k"].reshape(1, W), "k_a": p["rwkv_k_a"].reshape(1, W),
        "ind": ind, "indt": ind.T,
    }
    if v_first is not None:
        padc = LANES - RWKV_V_RANK
        prm["v0"] = p["rwkv_v0"].reshape(1, W)
        prm["v1p"] = jnp.pad(p["rwkv_v1"], ((0, 0), (0, padc)))
        prm["v2p"] = jnp.pad(p["rwkv_v2"], ((0, padc), (0, 0)))
    r, lw, kh, v, kn, kb, g = _rwkv_prep(feats, prm, v_first, B, S, W, _tile(S, cfg["tm_prep"]))
    if v_first is None:
        v_first = v
    r2, y0, mmat, g0 = _wkv_intra(r, lw, kh, v, kn, kb, min(cfg["nc"], S // CHUNK), cfg["passes_intra"])
    o_rwkv = _wkv_state(r2, y0, mmat, g0, r, kh, v, g, p["rwkv_lnx_w"].reshape(1, W),
                        p["rwkv_lnx_b"].reshape(1, W), p["rwkv_r_k"].reshape(1, W), B, S,
                        min(cfg["pg"], W // LANES), min(cfg["cb"], S // CHUNK), cfg["passes_state"])
    del heads

    x1, h2, logits_t = _postmix(o_att, o_rwkv, p["w_out"][:D // 2].astype(bf16), p["w_out"][D // 2:].astype(bf16),
                                x2, mod3,
                                p["g_post_mix"], p["g_pre_ffn"], p["w_router"].T, S, _tile(S, cfg["tm_post"]))

    tm_t = _tile(T, cfg["tm_tile"])
    ntiles = T // tm_t
    eidx_t, gate_t, lrank_t, cnt_tbl = _router(logits_t, p["router_bias"], tm_t)
    blk = cfg["blk"]
    run = (cnt_tbl[:, :ntiles].T + SUBLANES - 1) // SUBLANES * SUBLANES
    counts = jnp.sum(run, axis=0)
    padded = (counts + blk - 1) // blk * blk
    pad_end = jnp.cumsum(padded)
    pad_start = pad_end - padded
    e_ids = jnp.arange(N_EXPERTS, dtype=i32)
    tile_off = jnp.cumsum(run, axis=0) - run
    loc_off = jnp.cumsum(run, axis=1) - run
    seg_dst = (pad_start[None, :] + tile_off).reshape(-1).astype(i32)
    seg_src = loc_off.reshape(-1).astype(i32)
    seg_n = run.reshape(-1).astype(i32)
    seg_tot = jnp.sum(run, axis=1).astype(i32)
    loc_tok = jnp.repeat(loc_off, tm_t, axis=0)
    lp_t = jnp.sum(jnp.where(eidx_t[:, :, None] == e_ids, loc_tok[None], 0), axis=-1).astype(f32) + lrank_t
    lmax = -(-(tm_t * TOP_K + N_EXPERTS * (SUBLANES - 1)) // SORT_CHUNK) * SORT_CHUNK
    nblk = -(-(T * TOP_K + ntiles * N_EXPERTS * (SUBLANES - 1)) // blk) + N_EXPERTS
    P = nblk * blk
    blk_start = jnp.arange(nblk, dtype=i32) * blk
    nused = (pad_end[-1] // blk).astype(i32).reshape(1)
    blk_pos = jnp.minimum(blk_start, pad_end[-1] - blk)
    blk_e = jnp.minimum(jnp.sum((pad_end[None, :] <= blk_pos[:, None]).astype(i32), axis=1), N_EXPERTS - 1)
    cand = jnp.where(counts > 0, e_ids, N_EXPERTS)
    later = jnp.where(e_ids[None, :] > blk_e[:, None], cand[None, :], N_EXPERTS)
    nxt_e = jnp.min(later, axis=1)
    nxt_e = jnp.where(nxt_e >= N_EXPERTS, -1, nxt_e).astype(i32)

    xs, ysh = _dispatch(seg_src, seg_dst, seg_n, seg_tot, counts.astype(i32), pad_start.astype(i32), nused, lp_t, h2,
                        p["w_sh_gate"].astype(bf16), p["w_sh_up"].astype(bf16), p["w_sh_down"].astype(bf16),
                        P, blk, tm_t, lmax)
    ys = _experts(blk_e, nxt_e, nused, xs, wexp[0], wexp[1], wexp[2], i, blk)
    x_out = _combine(seg_src, seg_dst, seg_n, seg_tot, lp_t.T, gate_t.T, ys, ysh, x1,
                     mod3, p["g_post_ffn"], S, tm_t, lmax)
    return x_out, v_first


_CFG = dict(tm_norm=512, tm_in=2048, tn_att=1024, tm_in_rwkv=1024, tn_rwkv=1664, tq=512, hp=2, tm_prep=256, nc=8, passes_intra=1, passes_state=1, pg=4, cb=8,
            tm_post=512, blk=256, tm_tile=256)

_LAYER_KEYS = ("g_pre_mix", "g_post_mix", "g_pre_ffn", "g_post_ffn", "w_in", "w_out", "lam_q1", "lam_k1",
               "lam_q2", "lam_k2", "att_subln_g", "rwkv_mu", "rwkv_w0", "rwkv_w2", "rwkv_a0", "rwkv_a2",
               "rwkv_g2", "rwkv_k_k", "rwkv_k_a", "rwkv_r_k", "rwkv_lnx_w", "rwkv_lnx_b", "w_router",
               "router_bias", "w_sh_gate", "w_sh_up", "w_sh_down")


def _forward(x, c, params, cfg):
    B, S, D = x.shape
    L = params["w_in"].shape[0]
    bp = 16
    c_pad = jnp.zeros((bp, D), f32).at[:B].set(c)
    mod = _ada_mod(c_pad, params["w_ada"], params["b_ada"])
    x2 = x.reshape(B * S, D)
    v_first = None
    for i in range(L):
        p = {k: params[k][i] for k in _LAYER_KEYS}
        if i > 0:
            p["rwkv_v0"] = params["rwkv_v0"][i - 1]
            p["rwkv_v1"] = params["rwkv_v1"][i - 1]
            p["rwkv_v2"] = params["rwkv_v2"][i - 1]
        mod3 = mod[i, :B].reshape(B * N_MOD, 1, D)
        wexp = (params["w_exp_gate"], params["w_exp_up"], params["w_exp_down"])
        x2, v_first = _layer(i, x2, mod3, p, wexp, v_first, B, S, cfg)
    return x2.reshape(B, S, D)


def kernel(x, c, w_ada, b_ada, g_pre_mix, g_post_mix, g_pre_ffn, g_post_ffn, w_in, w_out, lam_q1, lam_k1, lam_q2, lam_k2, att_subln_g, rwkv_mu, rwkv_w0, rwkv_w2, rwkv_a0, rwkv_a2, rwkv_g2, rwkv_k_k, rwkv_k_a, rwkv_r_k, rwkv_lnx_w, rwkv_lnx_b, rwkv_v0, rwkv_v1, rwkv_v2, w_router, router_bias, w_exp_gate, w_exp_up, w_exp_down, w_sh_gate, w_sh_up, w_sh_down):
    params = dict(w_ada=w_ada, b_ada=b_ada, g_pre_mix=g_pre_mix, g_post_mix=g_post_mix, g_pre_ffn=g_pre_ffn,
                  g_post_ffn=g_post_ffn, w_in=w_in, w_out=w_out, lam_q1=lam_q1, lam_k1=lam_k1, lam_q2=lam_q2,
                  lam_k2=lam_k2, att_subln_g=att_subln_g, rwkv_mu=rwkv_mu, rwkv_w0=rwkv_w0, rwkv_w2=rwkv_w2,
                  rwkv_a0=rwkv_a0, rwkv_a2=rwkv_a2, rwkv_g2=rwkv_g2, rwkv_k_k=rwkv_k_k, rwkv_k_a=rwkv_k_a,
                  rwkv_r_k=rwkv_r_k, rwkv_lnx_w=rwkv_lnx_w, rwkv_lnx_b=rwkv_lnx_b, rwkv_v0=rwkv_v0,
                  rwkv_v1=rwkv_v1, rwkv_v2=rwkv_v2, w_router=w_router, router_bias=router_bias,
                  w_exp_gate=w_exp_gate, w_exp_up=w_exp_up, w_exp_down=w_exp_down, w_sh_gate=w_sh_gate,
                  w_sh_up=w_sh_up, w_sh_down=w_sh_down)
    return _forward(x, c, params, _CFG)
```

```python
import functools
import math

import jax
import jax.numpy as jnp
from jax import lax
from jax.experimental import pallas as pl
from jax.experimental.pallas import tpu as pltpu

f32 = jnp.float32
bf16 = jnp.bfloat16
i32 = jnp.int32
u32 = jnp.uint32

ATT_QK_DIM = 64
ATT_V_DIM = 128
ALIBI_MAX_BIAS = 8.0
ATT_SUBLN_EPS = 1e-5
RWKV_HEAD = 64
RWKV_W_RANK = 64
RWKV_A_RANK = 64
RWKV_G_RANK = 128
RWKV_V_RANK = 32
RWKV_GN_EPS = 64e-5
N_EXPERTS = 64
N_GROUPS = 8
TOPK_GROUPS = 4
TOP_K = 8
ROUTED_SCALE = 2.5
NORM_EPS = 1e-6
N_MOD = 6

LANES = 128
SUBLANES = 8
CHUNK = 64
VMEM_LIMIT = 56 * 1024 * 1024


def _cparams(sem):
    return pltpu.CompilerParams(dimension_semantics=sem, vmem_limit_bytes=VMEM_LIMIT)


def _dot(a, b):
    return jnp.dot(a, b, preferred_element_type=f32)


def _dot_nt(a, b):
    return lax.dot_general(a, b, (((1,), (1,)), ((), ())), preferred_element_type=f32)


def _split2(x):
    hi = x.astype(bf16)
    lo = (x - hi.astype(f32)).astype(bf16)
    return hi, lo


def _mm(a, b, passes=1):
    if passes == 1:
        return _dot(a.astype(bf16), b.astype(bf16))
    ah, al = _split2(a.astype(f32))
    bh, bl = _split2(b.astype(f32))
    return (_dot(al, bh) + _dot(ah, bl)) + _dot(ah, bh)


def _mm_exact_rhs(a, b_bf16):
    ah, al = _split2(a)
    return _dot(al, b_bf16) + _dot(ah, b_bf16)


def _sigmoid(x):
    return 1.0 / (1.0 + jnp.exp(-x))


def _pack_rows(x):
    half = x.shape[1] // 2
    a = x[:, :half].astype(bf16).astype(f32)
    b = x[:, half:].astype(bf16).astype(f32)
    lo = lax.shift_right_logical(lax.bitcast_convert_type(a, u32), jnp.uint32(16))
    hi = lax.bitcast_convert_type(b, u32) & jnp.uint32(0xFFFF0000)
    return lo | hi


def _unpack_rows(w):
    lo = lax.bitcast_convert_type(lax.shift_left(w, jnp.uint32(16)), f32)
    hi = lax.bitcast_convert_type(w & jnp.uint32(0xFFFF0000), f32)
    return lo, hi


def _ada_kernel(c_ref, w_ref, b_ref, o_ref):
    c = c_ref[...]
    cond = (c * _sigmoid(c)).astype(bf16)
    o_ref[0] = _dot(cond, w_ref[0].astype(bf16)) + b_ref[0]


def _ada_mod(c_pad, w_ada, b_ada, tn=1024):
    L, D, N = w_ada.shape
    bp = c_pad.shape[0]
    return pl.pallas_call(
        _ada_kernel,
        grid=(L, N // tn),
        in_specs=[pl.BlockSpec((bp, D), lambda l, j: (0, 0)),
                  pl.BlockSpec((1, D, tn), lambda l, j: (l, 0, j)),
                  pl.BlockSpec((1, 1, tn), lambda l, j: (l, 0, j))],
        out_specs=pl.BlockSpec((1, bp, tn), lambda l, j: (l, 0, j)),
        out_shape=jax.ShapeDtypeStruct((L, bp, N), f32),
        compiler_params=_cparams(("arbitrary", "arbitrary")),
        name="ada_mod",
    )(c_pad, w_ada, b_ada.reshape(L, 1, N))


def _prenorm_kernel(x_ref, sc_ref, sh_ref, g_ref, h_ref):
    x = x_ref[...]
    ms = jnp.mean(x * x, axis=-1, keepdims=True)
    y = x * lax.rsqrt(ms + NORM_EPS) * g_ref[...]
    h_ref[...] = (y * (1.0 + sc_ref[0]) + sh_ref[0]).astype(h_ref.dtype)


def _prenorm(x2, mod3, g, S, tm, seg_sc, seg_sh):
    T, D = x2.shape
    tpb = S // tm
    tile = pl.BlockSpec((tm, D), lambda i: (i, 0))
    return pl.pallas_call(
        _prenorm_kernel,
        grid=(T // tm,),
        in_specs=[tile,
                  pl.BlockSpec((1, 1, D), lambda i: ((i // tpb) * N_MOD + seg_sc, 0, 0)),
                  pl.BlockSpec((1, 1, D), lambda i: ((i // tpb) * N_MOD + seg_sh, 0, 0)),
                  pl.BlockSpec((1, D), lambda i: (0, 0))],
        out_specs=tile,
        out_shape=jax.ShapeDtypeStruct((T, D), bf16),
        compiler_params=_cparams(("arbitrary",)),
        name="prenorm",
    )(x2, mod3, mod3, g.reshape(1, D))


def _inproj_kernel(h_ref, w_ref, o_ref):
    o_ref[...] = _dot(h_ref[...], w_ref[...]).astype(o_ref.dtype)


def _inproj(h, w_bf, out_dtype, tm, tn):
    T, D = h.shape
    N = w_bf.shape[1]
    return pl.pallas_call(
        _inproj_kernel,
        grid=(T // tm, N // tn),
        in_specs=[pl.BlockSpec((tm, D), lambda i, j: (i, 0)),
                  pl.BlockSpec((D, tn), lambda i, j: (0, j))],
        out_specs=pl.BlockSpec((tm, tn), lambda i, j: (i, j)),
        out_shape=jax.ShapeDtypeStruct((T, N), out_dtype),
        compiler_params=_cparams(("arbitrary", "arbitrary")),
        name="inproj",
    )(h, w_bf)


def _attn_kernel(q_ref, k_ref, v_ref, slope_ref, lamp_ref, g_ref, o_ref,
                 q2t_ref, vt_ref, m_ref, l_ref, acc_ref, *, tq, hp, lam_init):
    qi = pl.program_id(2)
    scale = ATT_QK_DIM ** -0.5
    heads = range(hp)
    hl = [slice(h * LANES, (h + 1) * LANES) for h in heads]
    slope = [slope_ref[h][:, 0:1] for h in heads]

    @pl.when(qi == 0)
    def _():
        for h in heads:
            vt_ref[h] = v_ref[:, hl[h]].astype(f32).T.astype(bf16)

    dim = lax.broadcasted_iota(i32, (LANES, 1), 0)
    first = dim < ATT_QK_DIM
    for h in heads:
        qt = (q_ref[:, hl[h]].astype(f32) * scale).T
        q2t_ref[h, :, 0:tq] = jnp.where(first, qt, 0.0).astype(bf16)
        q2t_ref[h, :, tq:2 * tq] = jnp.where(first, 0.0, qt).astype(bf16)
    m_ref[...] = jnp.full(m_ref.shape, -jnp.inf, f32)
    l_ref[...] = jnp.zeros(l_ref.shape, f32)
    acc_ref[...] = jnp.zeros(acc_ref.shape, f32)

    kr = lax.broadcasted_iota(i32, (tq, 2 * tq), 0)
    qc = lax.broadcasted_iota(i32, (tq, 2 * tq), 1)
    causal = jnp.where(qc >= tq, qc - tq, qc) >= kr
    krow = lax.broadcasted_iota(i32, (tq, 1), 0).astype(f32)

    def step(ki, masked):
        start = pl.multiple_of(ki * tq, tq)
        kpos = krow + (ki * tq).astype(f32)
        s = [_dot(k_ref[pl.ds(start, tq), hl[h]], q2t_ref[h]) for h in heads]
        s = [s[h] + slope[h] * kpos for h in heads]
        if masked:
            s = [jnp.where(causal, s[h], -jnp.inf) for h in heads]
        m_prev = [m_ref[h] for h in heads]
        m_new = [jnp.maximum(m_prev[h], jnp.max(s[h], axis=0, keepdims=True)) for h in heads]
        alpha = [jnp.exp(m_prev[h] - m_new[h]) for h in heads]
        p = [jnp.exp(s[h] - m_new[h]) for h in heads]
        for h in heads:
            l_ref[h] = alpha[h] * l_ref[h] + jnp.sum(p[h], axis=0, keepdims=True)
            acc_ref[h] = alpha[h] * acc_ref[h] + _dot(vt_ref[h, :, pl.ds(start, tq)], p[h].astype(bf16))
            m_ref[h] = m_new[h]

    def body(ki, carry):
        step(ki, False)
        return carry

    lax.fori_loop(0, qi, body, 0)
    step(qi, True)

    lp = lamp_ref[...]
    lam = (jnp.exp(jnp.sum(lp[0:1] * lp[1:2], axis=-1, keepdims=True))
           - jnp.exp(jnp.sum(lp[2:3] * lp[3:4], axis=-1, keepdims=True)) + lam_init)
    for h in heads:
        on = acc_ref[h] * (1.0 / l_ref[h])
        o = on[:, 0:tq] - lam * on[:, tq:2 * tq]
        o = o * lax.rsqrt(jnp.mean(o * o, axis=0, keepdims=True) + ATT_SUBLN_EPS)
        o = o * g_ref[...] * (1.0 - lam_init)
        o_ref[:, hl[h]] = o.T.astype(o_ref.dtype)


def _attention(att, slopes, lamp, subln_g, B, S, H, lam_init, tq, hp):
    T = att.shape[0]
    nq = S // tq
    kern = functools.partial(_attn_kernel, tq=tq, hp=hp, lam_init=lam_init)
    hw = hp * LANES
    return pl.pallas_call(
        kern,
        grid=(B, H // hp, nq),
        in_specs=[pl.BlockSpec((tq, hw), lambda b, g, q: (b * nq + q, g)),
                  pl.BlockSpec((S, hw), lambda b, g, q: (b, H // hp + g)),
                  pl.BlockSpec((S, hw), lambda b, g, q: (b, 2 * (H // hp) + g)),
                  pl.BlockSpec((hp, 1, LANES), lambda b, g, q: (g, 0, 0)),
                  pl.BlockSpec((4, ATT_QK_DIM), lambda b, g, q: (0, 0)),
                  pl.BlockSpec((ATT_V_DIM, 1), lambda b, g, q: (0, 0))],
        out_specs=pl.BlockSpec((tq, hw), lambda b, g, q: (b * nq + q, g)),
        out_shape=jax.ShapeDtypeStruct((T, H * ATT_V_DIM), bf16),
        scratch_shapes=[pltpu.VMEM((hp, LANES, 2 * tq), bf16),
                        pltpu.VMEM((hp, LANES, S), bf16),
                        pltpu.VMEM((hp, 1, 2 * tq), f32),
                        pltpu.VMEM((hp, 1, 2 * tq), f32),
                        pltpu.VMEM((hp, LANES, 2 * tq), f32)],
        compiler_params=_cparams(("arbitrary", "arbitrary", "arbitrary")),
        name="diff_attention",
    )(att, att, att, slopes, lamp, subln_g.reshape(ATT_V_DIM, 1))


def _head_sums(x, ind, indt):
    s = _mm_exact_rhs(x, ind)
    return _mm_exact_rhs(s, indt)


def _rwkv_prep_kernel(*refs, W, has_vres):
    if has_vres:
        (f_ref, mu_ref, w0_ref, w2_ref, a0_ref, a2_ref, g2_ref, kk_ref, ka_ref, ind_ref, indt_ref,
         vf_ref, v0_ref, v1_ref, v2_ref,
         r_o, lw_o, kh_o, v_o, kn_o, kb_o, g_o, carry_ref) = refs
    else:
        (f_ref, mu_ref, w0_ref, w2_ref, a0_ref, a2_ref, g2_ref, kk_ref, ka_ref, ind_ref, indt_ref,
         r_o, lw_o, kh_o, v_o, kn_o, kb_o, g_o, carry_ref) = refs

    ti = pl.program_id(1)
    h = f_ref[...]
    tm = h.shape[0]

    @pl.when(ti == 0)
    def _():
        carry_ref[...] = jnp.zeros(carry_ref.shape, f32)

    rolled = pltpu.roll(h, 1, axis=0)
    row = lax.broadcasted_iota(i32, (tm, 1), 0)
    prev = jnp.where(row == 0, carry_ref[...], rolled)
    carry_ref[...] = h[tm - 1:tm, :]
    feats = h + (prev - h) * mu_ref[...]

    r = feats[:, 0:W]
    k = feats[:, W:2 * W]
    v = feats[:, 2 * W:3 * W]
    wa = feats[:, 3 * W:3 * W + LANES]
    g_lo = feats[:, 3 * W + LANES:3 * W + 2 * LANES]

    w = w0_ref[...] + _mm(jnp.tanh(wa), w2_ref[...], passes=3)
    lw_o[...] = -math.exp(-0.5) * _sigmoid(w)
    a = _sigmoid(a0_ref[...] + _mm(wa, a2_ref[...], passes=3))
    g_o[...] = _mm(_sigmoid(g_lo), g2_ref[...]).astype(g_o.dtype)

    if has_vres:
        mix = _sigmoid(v0_ref[...] + _mm(_mm(v, v1_ref[...]), v2_ref[...]))
        v = v + (vf_ref[...].astype(f32) - v) * mix

    kk = k * kk_ref[...]
    ss = _head_sums(kk * kk, ind_ref[...], indt_ref[...])
    kk = kk / jnp.maximum(jnp.sqrt(ss), 1e-12)
    r_o[...] = r.astype(r_o.dtype)
    kh_o[...] = (k * (1.0 + (a - 1.0) * ka_ref[...])).astype(kh_o.dtype)
    v_o[...] = v.astype(v_o.dtype)
    kn_o[...] = kk.astype(kn_o.dtype)
    kb_o[...] = (kk * a).astype(kb_o.dtype)


def _rwkv_prep(feats, prm, vfirst, B, S, W, tm):
    T, COLS = feats.shape
    tpb = S // tm
    has_vres = vfirst is not None
    row = lambda n: pl.BlockSpec((1, n), lambda b, t: (0, 0))
    full = lambda a: pl.BlockSpec(a.shape, lambda b, t: (0, 0))
    tile = pl.BlockSpec((tm, W), lambda b, t: (b * tpb + t, 0))
    args = [feats, prm["mu"], prm["w0"], prm["w2p"], prm["a0"], prm["a2p"], prm["g2"], prm["k_k"], prm["k_a"],
            prm["ind"], prm["indt"]]
    specs = [pl.BlockSpec((tm, COLS), lambda b, t: (b * tpb + t, 0)), row(COLS), row(W), full(prm["w2p"]),
             row(W), full(prm["a2p"]), full(prm["g2"]), row(W), row(W), full(prm["ind"]), full(prm["indt"])]
    if has_vres:
        args += [vfirst, prm["v0"], prm["v1p"], prm["v2p"]]
        specs += [tile, row(W), full(prm["v1p"]), full(prm["v2p"])]
    out_dtypes = (bf16, f32, bf16, bf16, bf16, bf16, bf16)
    kern = functools.partial(_rwkv_prep_kernel, W=W, has_vres=has_vres)
    return pl.pallas_call(
        kern,
        grid=(B, tpb),
        in_specs=specs,
        out_specs=[tile] * 7,
        out_shape=[jax.ShapeDtypeStruct((T, W), dt) for dt in out_dtypes],
        scratch_shapes=[pltpu.VMEM((1, COLS), f32)],
        compiler_params=_cparams(("arbitrary", "arbitrary")),
        name="rwkv_prep",
    )(*args)


def _wkv_chunks(rs, lws, ks, vs, kns, kbs, passes):
    C = CHUNK
    P2 = 2 * C
    n = range(len(rs))
    ri = lax.broadcasted_iota(i32, (C, C), 0)
    ci = lax.broadcasted_iota(i32, (C, C), 1)
    tri = (ci <= ri).astype(bf16)
    lane = lax.broadcasted_iota(i32, (1, LANES), 1)
    m0 = (lane < RWKV_HEAD).astype(f32)
    m1 = 1.0 - m0
    rr = lax.broadcasted_iota(i32, (P2, P2), 0)
    cc = lax.broadcasted_iota(i32, (P2, P2), 1)
    same = jnp.where(rr >= C, 1, 0) == jnp.where(cc >= C, 1, 0)
    strict = same & (cc < rr)
    incl = same & (cc <= rr)
    incl2 = jnp.concatenate([incl, incl], axis=1)
    eye = (rr == cc).astype(f32)
    zeros_p = jnp.zeros((P2, LANES), f32)
    zeros_c = jnp.zeros((C, LANES), f32)
    stack = lambda x: jnp.concatenate([x * m0, x * m1], axis=0)
    fold = lambda x: x[0:C] + x[C:2 * C]

    def cumsum(lw):
        h1 = lw.astype(bf16)
        r1 = lw - h1.astype(f32)
        h2 = r1.astype(bf16)
        h3 = (r1 - h2.astype(f32)).astype(bf16)
        return (_dot(tri, h3) + _dot(tri, h2)) + _dot(tri, h1)

    cum = [cumsum(lws[j]) for j in n]
    cum_c = [cum[j][C - 1:C, :] for j in n]
    at = [-kns[j] * jnp.exp(cum[j] - lws[j]) for j in n]
    rt = [rs[j] * jnp.exp(cum[j]) for j in n]
    einv = [jnp.exp(-cum[j]) for j in n]
    bt = [kbs[j] * einv[j] for j in n]
    kt = [ks[j] * einv[j] for j in n]
    eh = [jnp.exp(cum_c[j] - cum[j]) for j in n]
    bh = [kbs[j] * eh[j] for j in n]
    kh = [ks[j] * eh[j] for j in n]
    w_c = [jnp.exp(cum_c[j]) for j in n]
    abd = [stack(at[j]) for j in n]
    vst = [stack(vs[j]) for j in n]
    lhs = [jnp.concatenate([abd[j], stack(rt[j])], axis=0) for j in n]
    rhs = [jnp.concatenate([stack(bt[j]), stack(kt[j])], axis=0) for j in n]
    gram = [_mm_nt(lhs[j], rhs[j], passes) for j in n]
    lab = [jnp.where(strict, gram[j][0:P2, 0:P2], 0.0) for j in n]
    lak = [jnp.where(strict, gram[j][0:P2, P2:2 * P2], 0.0) for j in n]
    mrbk = [jnp.where(incl2, gram[j][P2:2 * P2, :], 0.0) for j in n]

    x0 = [_mm(lak[j], vst[j], passes) for j in n]
    tinv = [eye + lab[j] for j in n]
    lp = [_mm(lab[j], lab[j], passes) for j in n]
    n_sq = int(math.log2(C)) - 1
    for it in range(n_sq):
        if it < n_sq - 1:
            both = [_mm(lp[j], jnp.concatenate([lp[j], tinv[j]], axis=1), passes) for j in n]
            tinv = [tinv[j] + both[j][:, P2:2 * P2] for j in n]
            lp = [both[j][:, 0:P2] for j in n]
        else:
            tinv = [tinv[j] + _mm(lp[j], tinv[j], passes) for j in n]

    ta = [_mm(tinv[j], jnp.concatenate([abd[j], x0[j]], axis=1), passes) for j in n]
    rhs2 = [jnp.concatenate([ta[j], jnp.concatenate([zeros_p, vst[j]], axis=1)], axis=0) for j in n]
    z = [_mm(mrbk[j], rhs2[j], passes) for j in n]
    r2 = [rt[j] + fold(z[j][:, 0:LANES]) for j in n]
    y0 = [fold(z[j][:, LANES:2 * LANES]) for j in n]
    lhs3t = [jnp.concatenate([bh[j], kh[j]], axis=0).T for j in n]
    rhs3 = [jnp.concatenate([fold(ta[j]), jnp.concatenate([zeros_c, vs[j]], axis=1)], axis=0) for j in n]
    wmat = [_mm(lhs3t[j], rhs3[j], passes) for j in n]
    mmat = [jnp.where(same, wmat[j][:, 0:LANES], 0.0) + eye * w_c[j] for j in n]
    g0 = [jnp.where(same, wmat[j][:, LANES:2 * LANES], 0.0) for j in n]
    return r2, y0, mmat, g0


def _mm_nt(a, b, passes):
    if passes == 1:
        return _dot_nt(a.astype(bf16), b.astype(bf16))
    ah, al = _split2(a)
    bh, bl = _split2(b)
    return (_dot_nt(al, bh) + _dot_nt(ah, bl)) + _dot_nt(ah, bh)


def _wkv_intra_kernel(r_ref, lw_ref, k_ref, v_ref, kn_ref, kb_ref, r2_o, y0_o, m_o, g_o, *, nc, passes):
    C = CHUNK
    sls = [slice(c * C, (c + 1) * C) for c in range(nc)]
    take = lambda ref: [ref[sl, :].astype(f32) for sl in sls]
    r2, y0, mmat, g0 = _wkv_chunks(take(r_ref), take(lw_ref), take(k_ref), take(v_ref),
                                   take(kn_ref), take(kb_ref), passes)
    for c, sl in enumerate(sls):
        r2_o[sl, :] = r2[c].astype(r2_o.dtype)
        y0_o[sl, :] = y0[c].astype(y0_o.dtype)
        m_o[0, c] = mmat[c].astype(m_o.dtype)
        g_o[0, c] = g0[c]


def _wkv_intra(r, lw, kh, v, kn, kb, nc, passes):
    T, W = r.shape
    npair = W // LANES
    rows = nc * CHUNK
    tile = pl.BlockSpec((rows, LANES), lambda p, i: (i, p))
    mat = pl.BlockSpec((1, nc, LANES, LANES), lambda p, i: (p, i, 0, 0))
    kern = functools.partial(_wkv_intra_kernel, nc=nc, passes=passes)
    return pl.pallas_call(
        kern,
        grid=(npair, T // rows),
        in_specs=[tile] * 6,
        out_specs=[tile, tile, mat, mat],
        out_shape=[jax.ShapeDtypeStruct((T, W), bf16), jax.ShapeDtypeStruct((T, W), bf16),
                   jax.ShapeDtypeStruct((npair, T // CHUNK, LANES, LANES), bf16),
                   jax.ShapeDtypeStruct((npair, T // CHUNK, LANES, LANES), f32)],
        compiler_params=_cparams(("arbitrary", "arbitrary")),
        name="wkv_intra",
    )(r, lw, kh, v, kn, kb)


def _wkv_state_kernel(r2_ref, y0_ref, m_ref, g0_ref, r_ref, kh_ref, v_ref, g_ref,
                      lnw_ref, lnb_ref, rk_ref, o_ref, st_ref, y_ref, *, pg, cb, passes):
    C = CHUNK

    @pl.when(pl.program_id(2) == 0)
    def _():
        st_ref[...] = jnp.zeros(st_ref.shape, f32)

    pairs = range(pg)
    lanes = [slice(p * LANES, (p + 1) * LANES) for p in pairs]
    st = [st_ref[p] for p in pairs]
    for c in range(cb):
        rows = slice(c * C, (c + 1) * C)
        for p in pairs:
            y_ref[rows, lanes[p]] = (_mm(r2_ref[rows, lanes[p]], st[p], passes)
                                     + y0_ref[rows, lanes[p]].astype(f32))
        st = [_mm(m_ref[p, c], st[p], passes) + g0_ref[p, c] for p in pairs]
    for p in pairs:
        st_ref[p] = st[p]

    rr = lax.broadcasted_iota(i32, (LANES, LANES), 0)
    cc = lax.broadcasted_iota(i32, (LANES, LANES), 1)
    ones_bd = (jnp.where(rr >= RWKV_HEAD, 1, 0) == jnp.where(cc >= RWKV_HEAD, 1, 0)).astype(bf16)
    for p in pairs:
        y = y_ref[:, lanes[p]]
        mu = _mm_exact_rhs(y, ones_bd) * (1.0 / RWKV_HEAD)
        d = y - mu
        var = _mm_exact_rhs(d * d, ones_bd) * (1.0 / RWKV_HEAD)
        yn = d * lax.rsqrt(var + RWKV_GN_EPS) * lnw_ref[:, lanes[p]] + lnb_ref[:, lanes[p]]
        rk = r_ref[:, lanes[p]].astype(f32) * kh_ref[:, lanes[p]].astype(f32) * rk_ref[:, lanes[p]]
        bonus = _mm_exact_rhs(rk, ones_bd) * v_ref[:, lanes[p]]
        o_ref[:, lanes[p]] = ((yn + bonus) * g_ref[:, lanes[p]]).astype(o_ref.dtype)


def _wkv_state(r2, y0, mm, g0, r, kh, v, g, lnw, lnb, rk, B, S, pg, cb, passes):
    T, W = r.shape
    npair = W // LANES
    rows = cb * CHUNK
    steps = S // rows
    seq = pl.BlockSpec((rows, pg * LANES), lambda b, q, c: (b * steps + c, q))
    mat = pl.BlockSpec((pg, cb, LANES, LANES), lambda b, q, c: (q, b * steps + c, 0, 0))
    prow = pl.BlockSpec((1, pg * LANES), lambda b, q, c: (0, q))
    kern = functools.partial(_wkv_state_kernel, pg=pg, cb=cb, passes=passes)
    return pl.pallas_call(
        kern,
        grid=(B, npair // pg, steps),
        in_specs=[seq, seq, mat, mat, seq, seq, seq, seq, prow, prow, prow],
        out_specs=seq,
        out_shape=jax.ShapeDtypeStruct((T, W), bf16),
        scratch_shapes=[pltpu.VMEM((pg, LANES, LANES), f32), pltpu.VMEM((rows, pg * LANES), f32)],
        compiler_params=_cparams(("arbitrary", "arbitrary", "arbitrary")),
        name="wkv_state",
    )(r2, y0, mm, g0, r, kh, v, g, lnw, lnb, rk)


def _postmix_kernel(oa_ref, orw_ref, wa_ref, wr_ref, x_ref, ga_ref, gpost_ref, gpre_ref, sc_ref, sh_ref, wrt_ref,
                    x1_o, h2_o, lg_o):
    mixed = _dot(oa_ref[...], wa_ref[...]) + _dot(orw_ref[...], wr_ref[...])
    ms = jnp.mean(mixed * mixed, axis=-1, keepdims=True)
    x1 = x_ref[...] + ga_ref[0] * (mixed * lax.rsqrt(ms + NORM_EPS) * gpost_ref[...])
    x1_o[...] = x1
    ms1 = jnp.mean(x1 * x1, axis=-1, keepdims=True)
    h2 = (x1 * lax.rsqrt(ms1 + NORM_EPS) * gpre_ref[...]) * (1.0 + sc_ref[0]) + sh_ref[0]
    h2_o[...] = _pack_rows(h2)
    lg_o[...] = _mm_nt(wrt_ref[...], h2, 3)


def _postmix(o_att, o_rwkv, w_out_a, w_out_r, x2, mod3, g_post, g_pre, w_rt, S, tm):
    T, D = x2.shape
    WA = o_att.shape[1]
    WR = o_rwkv.shape[1]
    E = w_rt.shape[0]
    tpb = S // tm
    modspec = lambda seg: pl.BlockSpec((1, 1, D), lambda i: ((i // tpb) * N_MOD + seg, 0, 0))
    tile = pl.BlockSpec((tm, D), lambda i: (i, 0))
    return pl.pallas_call(
        _postmix_kernel,
        grid=(T // tm,),
        in_specs=[pl.BlockSpec((tm, WA), lambda i: (i, 0)),
                  pl.BlockSpec((tm, WR), lambda i: (i, 0)),
                  pl.BlockSpec((WA, D), lambda i: (0, 0), pipeline_mode=pl.Buffered(1)),
                  pl.BlockSpec((WR, D), lambda i: (0, 0), pipeline_mode=pl.Buffered(1)),
                  tile, modspec(2),
                  pl.BlockSpec((1, D), lambda i: (0, 0)),
                  pl.BlockSpec((1, D), lambda i: (0, 0)),
                  modspec(4), modspec(3),
                  pl.BlockSpec((E, D), lambda i: (0, 0))],
        out_specs=[tile, pl.BlockSpec((tm, D // 2), lambda i: (i, 0)),
                   pl.BlockSpec((E, tm), lambda i: (0, i))],
        out_shape=[jax.ShapeDtypeStruct((T, D), f32), jax.ShapeDtypeStruct((T, D // 2), u32),
                   jax.ShapeDtypeStruct((E, T), f32)],
        compiler_params=_cparams(("arbitrary",)),
        name="postmix",
    )(o_att, o_rwkv, w_out_a, w_out_r, x2, mod3, g_post.reshape(1, D), g_pre.reshape(1, D), mod3, mod3, w_rt)


def _first_max(x, iota, n):
    mx = jnp.max(x, axis=0, keepdims=True)
    idx = jnp.min(jnp.where(x == mx, iota, n), axis=0, keepdims=True)
    return mx, idx


def _router_kernel(lg_ref, bias_ref, eidx_o, gate_o, rank_o, cnt_o, cnt_ref):
    E = N_EXPERTS
    G = N_GROUPS
    per = E // G
    tm = lg_ref.shape[1]

    @pl.when(pl.program_id(0) == 0)
    def _():
        cnt_ref[...] = jnp.zeros(cnt_ref.shape, f32)

    scores = _sigmoid(lg_ref[...])
    biased = scores + bias_ref[...]
    neg = -jnp.inf

    iota_p = lax.broadcasted_iota(i32, (per, tm), 0).astype(f32)
    gs = []
    for g in range(G):
        xg = biased[g * per:(g + 1) * per, :]
        m1, i1 = _first_max(xg, iota_p, per)
        m2 = jnp.max(jnp.where(iota_p == i1, neg, xg), axis=0, keepdims=True)
        gs.append(m1 + m2)
    gsc = jnp.concatenate(gs, axis=0)
    iota_g = lax.broadcasted_iota(i32, (G, tm), 0).astype(f32)
    gsel = jnp.zeros((G, tm), f32)
    for _ in range(TOPK_GROUPS):
        _, gi = _first_max(gsc, iota_g, G)
        hit = iota_g == gi
        gsel = jnp.where(hit, 1.0, gsel)
        gsc = jnp.where(hit, neg, gsc)
    masked = jnp.concatenate(
        [jnp.where(gsel[g:g + 1, :] > 0.0, biased[g * per:(g + 1) * per, :], neg) for g in range(G)], axis=0)

    iota_e = lax.broadcasted_iota(i32, (E, tm), 0).astype(f32)
    sel = jnp.zeros((E, tm), f32)
    idxs, vals = [], []
    for _ in range(TOP_K):
        _, ei = _first_max(masked, iota_e, E)
        hit = iota_e == ei
        idxs.append(ei)
        vals.append(jnp.sum(jnp.where(hit, scores, 0.0), axis=0, keepdims=True))
        sel = jnp.where(hit, 1.0, sel)
        masked = jnp.where(hit, neg, masked)
    tot = vals[0]
    for vv in vals[1:]:
        tot = tot + vv
    eidx_o[...] = jnp.concatenate(idxs, axis=0).astype(i32)
    gate_o[...] = jnp.concatenate([vv / tot * ROUTED_SCALE for vv in vals], axis=0)

    rr = lax.broadcasted_iota(i32, (tm, tm), 0)
    cc = lax.broadcasted_iota(i32, (tm, tm), 1)
    before = (rr < cc).astype(bf16)
    pos = _dot(sel.astype(bf16), before)
    rank_o[...] = jnp.concatenate(
        [jnp.sum(jnp.where(iota_e == ei, pos, 0.0), axis=0, keepdims=True) for ei in idxs], axis=0)
    lane = lax.broadcasted_iota(i32, cnt_ref.shape, 1)
    cnt_ref[...] = jnp.where(lane == pl.program_id(0), jnp.sum(sel, axis=1, keepdims=True), cnt_ref[...])
    cnt_o[...] = cnt_ref[...].astype(i32)


def _router(logits_t, bias, tm):
    E, T = logits_t.shape
    assert T // tm <= LANES
    k_tile = pl.BlockSpec((TOP_K, tm), lambda i: (0, i))
    return pl.pallas_call(
        _router_kernel,
        grid=(T // tm,),
        in_specs=[pl.BlockSpec((E, tm), lambda i: (0, i)),
                  pl.BlockSpec((E, 1), lambda i: (0, 0))],
        out_specs=[k_tile, k_tile, k_tile, pl.BlockSpec((E, LANES), lambda i: (0, 0))],
        out_shape=[jax.ShapeDtypeStruct((TOP_K, T), i32), jax.ShapeDtypeStruct((TOP_K, T), f32),
                   jax.ShapeDtypeStruct((TOP_K, T), f32), jax.ShapeDtypeStruct((E, LANES), i32)],
        scratch_shapes=[pltpu.VMEM((E, LANES), f32)],
        compiler_params=_cparams(("arbitrary",)),
        name="router",
    )(logits_t, bias.reshape(E, 1))


def _row_copy(src_ref, s, dst_ref, d, sem):
    return pltpu.make_async_copy(src_ref.at[pl.ds(s, 1), :], dst_ref.at[pl.ds(d, 1), :], sem)


def _zero_fill(cnt_ref, pstart_ref, nused_ref, z_ref, xs_out, sem, blk, nblk, start):
    def act(cp):
        if start:
            cp.start()
        else:
            cp.wait()

    def per_expert(e, carry):
        c = cnt_ref[e]
        base = pstart_ref[e] + c
        npad = (blk - (c & (blk - 1))) & (blk - 1)
        head = (-base) & (SUBLANES - 1)

        def one_row(j, carry2):
            act(_row_copy(z_ref, 0, xs_out, base + j, sem))
            return carry2

        lax.fori_loop(0, head, one_row, 0)
        rem = npad - head
        aligned = base + head
        p = blk // 2
        while p >= SUBLANES:
            off = pl.multiple_of(aligned + (rem & ~(2 * p - 1)), SUBLANES)

            @pl.when((rem & p) != 0)
            def _(p=p, off=off):
                act(pltpu.make_async_copy(z_ref.at[pl.ds(0, p), :], xs_out.at[pl.ds(off, p), :], sem))

            p //= 2
        return carry

    lax.fori_loop(0, N_EXPERTS, per_expert, 0)

    def per_block(b, carry):
        act(pltpu.make_async_copy(z_ref, xs_out.at[pl.ds(pl.multiple_of(b * blk, blk), blk), :], sem))
        return carry

    lax.fori_loop(nused_ref[0], nblk, per_block, 0)


def _piece_sizes(tm):
    return [tm >> s for s in range(tm.bit_length()) if (tm >> s) >= SUBLANES]


def _seg_copies(psrc_ref, pdst_ref, pcnt_ref, tile, buf_ref, hbm_ref, sem, tm, to_hbm):
    ntiles = pl.num_programs(0)
    for s, p in enumerate(_piece_sizes(tm)):
        base = (s * ntiles + tile) * N_EXPERTS

        def one(j, carry, p=p, base=base):
            v = buf_ref.at[pl.ds(pl.multiple_of(psrc_ref[base + j], SUBLANES), p), :]
            h = hbm_ref.at[pl.ds(pl.multiple_of(pdst_ref[base + j], SUBLANES), p), :]
            (pltpu.make_async_copy(v, h, sem) if to_hbm else pltpu.make_async_copy(h, v, sem)).start()
            return carry

        lax.fori_loop(0, pcnt_ref[s * ntiles + tile], one, 0)


def _seg_wait(total_rows, buf_ref, hbm_ref, sem, to_hbm):
    p = 1 << (buf_ref.shape[0].bit_length() - 1)
    while p >= SUBLANES:
        @pl.when((total_rows & p) != 0)
        def _(p=p):
            v = buf_ref.at[pl.ds(0, p), :]
            h = hbm_ref.at[pl.ds(0, p), :]
            (pltpu.make_async_copy(v, h, sem) if to_hbm else pltpu.make_async_copy(h, v, sem)).wait()

        p //= 2


SORT_CHUNK = 512


def _dispatch_kernel(ssrc_ref, sdst_ref, sn_ref, stot_ref, cnt_ref, pstart_ref, nused_ref, lp_ref, h_ref,
                     wsg_ref, wsu_ref, wsd_ref, xs_out, ysh_o, sb_ref, z_ref, sems, sem_z, *, blk, nblk):
    i = pl.program_id(0)
    last = pl.num_programs(0) - 1
    tm = h_ref.shape[0]
    lmax = sb_ref.shape[1]
    slot = i % 2
    seg = functools.partial(_seg_copies, ssrc_ref, sdst_ref, sn_ref, hbm_ref=xs_out, tm=tm, to_hbm=True)

    def seg_wait(tile, s):
        _seg_wait(stot_ref[tile], sb_ref.at[s], xs_out, sems.at[s], True)

    @pl.when(i >= 2)
    def _():
        seg_wait(i - 2, slot)

    hlo, hhi = _unpack_rows(h_ref[...])
    hb = jnp.concatenate([hlo.astype(bf16), hhi.astype(bf16)], axis=1)
    lpv = lp_ref[...]
    half = hb.shape[1] // 2
    for c in range(lmax // SORT_CHUNK):
        jj = (lax.broadcasted_iota(i32, (SORT_CHUNK, tm), 0) + c * SORT_CHUNK).astype(f32)
        onehot = jnp.zeros((SORT_CHUNK, tm), f32)
        for k in range(TOP_K):
            onehot = jnp.where(jj == lpv[k:k + 1, :], 1.0, onehot)
        rows = _dot(onehot.astype(bf16), hb)
        lo = lax.shift_right_logical(lax.bitcast_convert_type(rows[:, :half], u32), jnp.uint32(16))
        hi = lax.bitcast_convert_type(rows[:, half:], u32) & jnp.uint32(0xFFFF0000)
        sb_ref[slot, c * SORT_CHUNK:(c + 1) * SORT_CHUNK, :] = lo | hi

    seg(i, buf_ref=sb_ref.at[slot], sem=sems.at[slot])

    gt = _dot(hb, wsg_ref[...])
    up = _dot(hb, wsu_ref[...])
    ysh_o[...] = _dot(((gt * _sigmoid(gt)) * up).astype(bf16), wsd_ref[...]).astype(ysh_o.dtype)

    @pl.when(i == last)
    def _():
        z_ref[...] = jnp.zeros(z_ref.shape, u32)
        _zero_fill(cnt_ref, pstart_ref, nused_ref, z_ref, xs_out, sem_z, blk, nblk, True)
        _zero_fill(cnt_ref, pstart_ref, nused_ref, z_ref, xs_out, sem_z, blk, nblk, False)
        seg_wait(i, slot)

        @pl.when(i >= 1)
        def _():
            seg_wait(i - 1, 1 - slot)


def _dispatch(seg_src, seg_dst, seg_n, seg_tot, counts, pad_start, nused, lp_t, h2p, wsg, wsu, wsd,
              P, blk, tm, lmax):
    T, DW = h2p.shape
    D, DS = wsg.shape
    assert blk & (blk - 1) == 0 and tm & (tm - 1) == 0 and lmax % SORT_CHUNK == 0
    kern = functools.partial(_dispatch_kernel, blk=blk, nblk=P // blk)
    const = lambda shape: pl.BlockSpec(shape, lambda i, *_: (0, 0))
    grid_spec = pltpu.PrefetchScalarGridSpec(
        num_scalar_prefetch=7,
        grid=(T // tm,),
        in_specs=[pl.BlockSpec((TOP_K, tm), lambda i, *_: (0, i)),
                  pl.BlockSpec((tm, DW), lambda i, *_: (i, 0)),
                  const((D, DS)), const((D, DS)), const((DS, D))],
        out_specs=[pl.BlockSpec(memory_space=pl.ANY), pl.BlockSpec((tm, D), lambda i, *_: (i, 0))],
        scratch_shapes=[pltpu.VMEM((2, lmax, DW), u32), pltpu.VMEM((blk, DW), u32),
                        pltpu.SemaphoreType.DMA((2,)), pltpu.SemaphoreType.DMA],
    )
    return pl.pallas_call(
        kern,
        grid_spec=grid_spec,
        out_shape=[jax.ShapeDtypeStruct((P, DW), u32), jax.ShapeDtypeStruct((T, D), bf16)],
        compiler_params=_cparams(("arbitrary",)),
        name="moe_dispatch",
    )(seg_src, seg_dst, seg_n, seg_tot, counts, pad_start, nused, lp_t, h2p, wsg, wsu, wsd)


def _experts_kernel(blk_e_ref, nxt_e_ref, nused_ref, xs_ref, wg_hbm, wu_hbm, wd_hbm, ys_ref,
                    wg_f, wu_f, wd_f, wg_s, wu_s, wd_s, sems, *, layer):
    i = pl.program_id(0)
    e = blk_e_ref[i]
    changed = jnp.logical_or(i == 0, e != blk_e_ref[jnp.maximum(i - 1, 0)])

    def weight_copies(ex):
        return (pltpu.make_async_copy(wg_hbm.at[layer, ex], wg_f, sems.at[0]),
                pltpu.make_async_copy(wu_hbm.at[layer, ex], wu_f, sems.at[1]),
                pltpu.make_async_copy(wd_hbm.at[layer, ex], wd_f, sems.at[2]))

    @pl.when(i == 0)
    def _():
        for cp in weight_copies(e):
            cp.start()

    @pl.when(changed)
    def _():
        for cp in weight_copies(e):
            cp.wait()
        for src, dst in ((wg_f, wg_s), (wu_f, wu_s), (wd_f, wd_s)):
            rows = src.shape[0] // 8
            for c in range(8):
                dst[c * rows:(c + 1) * rows, :] = src[c * rows:(c + 1) * rows, :].astype(bf16)
        nxt = nxt_e_ref[i]

        @pl.when(nxt >= 0)
        def _():
            for cp in weight_copies(nxt):
                cp.start()

    @pl.when(i < nused_ref[0])
    def _():
        lo, hi = _unpack_rows(xs_ref[...])
        x = jnp.concatenate([lo.astype(bf16), hi.astype(bf16)], axis=1)
        gt = _dot(x, wg_s[...])
        up = _dot(x, wu_s[...])
        hmid = (gt * _sigmoid(gt)) * up
        ys_ref[...] = _pack_rows(_dot(hmid.astype(bf16), wd_s[...]))

    @pl.when(i >= nused_ref[0])
    def _():
        ys_ref[...] = jnp.zeros(ys_ref.shape, u32)


def _experts(blk_e, nxt_e, nused, xs, w_gate, w_up, w_down, layer, blk):
    P, DW = xs.shape
    D, DE = w_gate.shape[-2:]
    nblk = P // blk
    row_idx = lambda i, be, nx, nu: (jnp.minimum(i, nu[0] - 1), 0)
    hbm = pl.BlockSpec(memory_space=pl.ANY)
    grid_spec = pltpu.PrefetchScalarGridSpec(
        num_scalar_prefetch=3,
        grid=(nblk,),
        in_specs=[pl.BlockSpec((blk, DW), row_idx), hbm, hbm, hbm],
        out_specs=pl.BlockSpec((blk, DW), lambda i, be, nx, nu: (i, 0)),
        scratch_shapes=[pltpu.VMEM((D, DE), f32), pltpu.VMEM((D, DE), f32), pltpu.VMEM((DE, D), f32),
                        pltpu.VMEM((D, DE), bf16), pltpu.VMEM((D, DE), bf16), pltpu.VMEM((DE, D), bf16),
                        pltpu.SemaphoreType.DMA((3,))],
    )
    return pl.pallas_call(
        functools.partial(_experts_kernel, layer=layer),
        grid_spec=grid_spec,
        out_shape=jax.ShapeDtypeStruct((P, DW), u32),
        compiler_params=_cparams(("arbitrary",)),
        name="moe_experts",
    )(blk_e, nxt_e, nused, xs, w_gate, w_up, w_down)


def _combine_kernel(ssrc_ref, sdst_ref, sn_ref, stot_ref, lp_ref, gate_ref, ys_hbm, ysh_ref, x1_ref, gf_ref,
                    gpost_ref, x2_o, yb_ref, sems):
    i = pl.program_id(0)
    n = pl.num_programs(0)
    tm = x1_ref.shape[0]
    lmax = yb_ref.shape[1]
    slot = i % 2
    seg = functools.partial(_seg_copies, ssrc_ref, sdst_ref, sn_ref, hbm_ref=ys_hbm, tm=tm, to_hbm=False)

    @pl.when(i == 0)
    def _():
        yb_ref[...] = jnp.zeros(yb_ref.shape, u32)
        seg(i, buf_ref=yb_ref.at[slot], sem=sems.at[slot])

    @pl.when(i + 1 < n)
    def _():
        seg(i + 1, buf_ref=yb_ref.at[1 - slot], sem=sems.at[1 - slot])

    _seg_wait(stot_ref[i], yb_ref.at[slot], ys_hbm, sems.at[slot], False)

    lp = lp_ref[...]
    gate = gate_ref[...]
    ysh = ysh_ref[...].astype(f32)
    half = ysh.shape[1] // 2
    lo = ysh[:, :half]
    hi = ysh[:, half:]
    for c in range(lmax // SORT_CHUNK):
        jl = (lax.broadcasted_iota(i32, (tm, SORT_CHUNK), 1) + c * SORT_CHUNK).astype(f32)
        g = jnp.zeros((tm, SORT_CHUNK), f32)
        for k in range(TOP_K):
            g = jnp.where(jl == lp[:, k:k + 1], gate[:, k:k + 1], g)
        gb = g.astype(bf16)
        a, b = _unpack_rows(yb_ref[slot, c * SORT_CHUNK:(c + 1) * SORT_CHUNK, :])
        lo = lo + _dot(gb, a.astype(bf16))
        hi = hi + _dot(gb, b.astype(bf16))
    y = jnp.concatenate([lo, hi], axis=1)
    ms = jnp.mean(y * y, axis=-1, keepdims=True)
    x2_o[...] = x1_ref[...] + gf_ref[0] * (y * lax.rsqrt(ms + NORM_EPS) * gpost_ref[...])


def _combine(seg_src, seg_dst, seg_n, seg_tot, lp_tk, gate_tk, ys, ysh, x1, mod3, g_post, S, tm, lmax):
    T, D = x1.shape
    DW = ys.shape[1]
    tpb = S // tm
    tile = pl.BlockSpec((tm, D), lambda i, *_: (i, 0))
    ktile = pl.BlockSpec((tm, TOP_K), lambda i, *_: (i, 0))
    grid_spec = pltpu.PrefetchScalarGridSpec(
        num_scalar_prefetch=4,
        grid=(T // tm,),
        in_specs=[ktile, ktile,
                  pl.BlockSpec(memory_space=pl.ANY),
                  tile, tile,
                  pl.BlockSpec((1, 1, D), lambda i, *_: ((i // tpb) * N_MOD + 5, 0, 0)),
                  pl.BlockSpec((1, D), lambda i, *_: (0, 0))],
        out_specs=tile,
        scratch_shapes=[pltpu.VMEM((2, lmax, DW), u32), pltpu.SemaphoreType.DMA((2,))],
    )
    return pl.pallas_call(
        _combine_kernel,
        grid_spec=grid_spec,
        out_shape=jax.ShapeDtypeStruct((T, D), f32),
        compiler_params=_cparams(("arbitrary",)),
        name="moe_combine",
    )(seg_src, seg_dst, seg_n, seg_tot, lp_tk, gate_tk, ys, ysh, x1, mod3, g_post.reshape(1, D))


def _tile(n, pref):
    t = min(n, pref)
    assert n % t == 0, (n, t)
    return t


def _layer(i, x2, mod3, p, wexp, v_first, B, S, cfg):
    T, D = x2.shape
    H = (D // 2) // ATT_V_DIM
    W = D - D // 2
    att_cols = 2 * H * 2 * ATT_QK_DIM + H * ATT_V_DIM
    lam_init = 0.8 - 0.6 * math.exp(-0.3 * i)

    h1 = _prenorm(x2, mod3, p["g_pre_mix"], S, _tile(S, cfg["tm_norm"]), 1, 0)
    att = _inproj(h1, p["w_in"][:, :att_cols].astype(bf16), bf16, _tile(T, cfg["tm_in"]), cfg["tn_att"])
    feats = _inproj(h1, p["w_in"][:, att_cols:].astype(bf16), f32, _tile(T, cfg["tm_in_rwkv"]), cfg["tn_rwkv"])

    slopes = jnp.broadcast_to(
        (2.0 ** (-ALIBI_MAX_BIAS * jnp.arange(1, H + 1, dtype=f32) / H))[:, None, None], (H, 1, LANES))
    lamp = jnp.stack([p["lam_q1"], p["lam_k1"], p["lam_q2"], p["lam_k2"]])
    o_att = _attention(att, slopes, lamp, p["att_subln_g"], B, S, H, lam_init, _tile(S, cfg["tq"]), cfg["hp"])

    cols = feats.shape[1]
    zw = jnp.zeros((RWKV_A_RANK, W), f32)
    heads = W // RWKV_HEAD
    ind = (jnp.arange(W)[:, None] // RWKV_HEAD == jnp.arange(LANES)[None, :]).astype(bf16)
    prm = {
        "mu": p["rwkv_mu"].reshape(1, cols), "w0": p["rwkv_w0"].reshape(1, W),
        "w2p": jnp.concatenate([p["rwkv_w2"], zw], axis=0),
        "a0": p["rwkv_a0"].reshape(1, W),
        "a2p": jnp.concatenate([jnp.zeros((RWKV_W_RANK, W), f32), p["rwkv_a2"]], axis=0),
        "g2": p["rwkv_g2"], "k_k": p["rwkv_k_k"].reshape(1, W), "k_a": p["rwkv_k_a"].reshape(1, W),
        "ind": ind, "indt": ind.T,
    }
    if v_first is not None:
        padc = LANES - RWKV_V_RANK
        prm["v0"] = p["rwkv_v0"].reshape(1, W)
        prm["v1p"] = jnp.pad(p["rwkv_v1"], ((0, 0), (0, padc)))
        prm["v2p"] = jnp.pad(p["rwkv_v2"], ((0, padc), (0, 0)))
    r, lw, kh, v, kn, kb, g = _rwkv_prep(feats, prm, v_first, B, S, W, _tile(S, cfg["tm_prep"]))
    if v_first is None:
        v_first = v
    r2, y0, mmat, g0 = _wkv_intra(r, lw, kh, v, kn, kb, min(cfg["nc"], S // CHUNK), cfg["passes_intra"])
    o_rwkv = _wkv_state(r2, y0, mmat, g0, r, kh, v, g, p["rwkv_lnx_w"].reshape(1, W),
                        p["rwkv_lnx_b"].reshape(1, W), p["rwkv_r_k"].reshape(1, W), B, S,
                        min(cfg["pg"], W // LANES), min(cfg["cb"], S // CHUNK), cfg["passes_state"])
    del heads

    x1, h2, logits_t = _postmix(o_att, o_rwkv, p["w_out"][:D // 2].astype(bf16), p["w_out"][D // 2:].astype(bf16),
                                x2, mod3,
                                p["g_post_mix"], p["g_pre_ffn"], p["w_router"].T, S, _tile(S, cfg["tm_post"]))

    tm_t = _tile(T, cfg["tm_tile"])
    ntiles = T // tm_t
    eidx_t, gate_t, lrank_t, cnt_tbl = _router(logits_t, p["router_bias"], tm_t)
    blk = cfg["blk"]
    run = (cnt_tbl[:, :ntiles].T + SUBLANES - 1) // SUBLANES * SUBLANES
    counts = jnp.sum(run, axis=0)
    padded = (counts + blk - 1) // blk * blk
    pad_end = jnp.cumsum(padded)
    pad_start = pad_end - padded
    e_ids = jnp.arange(N_EXPERTS, dtype=i32)
    tile_off = jnp.cumsum(run, axis=0) - run
    loc_off = jnp.cumsum(run, axis=1) - run
    run_dst = pad_start[None, :] + tile_off
    srcs, dsts, cnts = [], [], []
    for psz in _piece_sizes(tm_t):
        has = (run & psz) != 0
        before = run & ~(2 * psz - 1)
        slot_of = jnp.cumsum(has, axis=1) - 1
        hit = has[:, None, :] & (slot_of[:, None, :] == e_ids[None, :, None])
        srcs.append(jnp.sum(jnp.where(hit, (loc_off + before)[:, None, :], 0), axis=-1))
        dsts.append(jnp.sum(jnp.where(hit, (run_dst + before)[:, None, :], 0), axis=-1))
        cnts.append(jnp.sum(has, axis=1))
    seg_src = jnp.stack(srcs).reshape(-1).astype(i32)
    seg_dst = jnp.stack(dsts).reshape(-1).astype(i32)
    seg_n = jnp.stack(cnts).reshape(-1).astype(i32)
    seg_tot = jnp.sum(run, axis=1).astype(i32)
    loc_tok = jnp.repeat(loc_off, tm_t, axis=0)
    lp_t = jnp.sum(jnp.where(eidx_t[:, :, None] == e_ids, loc_tok[None], 0), axis=-1).astype(f32) + lrank_t
    lmax = -(-(tm_t * TOP_K + N_EXPERTS * (SUBLANES - 1)) // SORT_CHUNK) * SORT_CHUNK
    nblk = -(-(T * TOP_K + ntiles * N_EXPERTS * (SUBLANES - 1)) // blk) + N_EXPERTS
    P = nblk * blk
    blk_start = jnp.arange(nblk, dtype=i32) * blk
    nused = (pad_end[-1] // blk).astype(i32).reshape(1)
    blk_pos = jnp.minimum(blk_start, pad_end[-1] - blk)
    blk_e = jnp.minimum(jnp.sum((pad_end[None, :] <= blk_pos[:, None]).astype(i32), axis=1), N_EXPERTS - 1)
    cand = jnp.where(counts > 0, e_ids, N_EXPERTS)
    later = jnp.where(e_ids[None, :] > blk_e[:, None], cand[None, :], N_EXPERTS)
    nxt_e = jnp.min(later, axis=1)
    nxt_e = jnp.where(nxt_e >= N_EXPERTS, -1, nxt_e).astype(i32)

    xs, ysh = _dispatch(seg_src, seg_dst, seg_n, seg_tot, counts.astype(i32), pad_start.astype(i32), nused, lp_t, h2,
                        p["w_sh_gate"].astype(bf16), p["w_sh_up"].astype(bf16), p["w_sh_down"].astype(bf16),
                        P, blk, tm_t, lmax)
    ys = _experts(blk_e, nxt_e, nused, xs, wexp[0], wexp[1], wexp[2], i, blk)
    x_out = _combine(seg_src, seg_dst, seg_n, seg_tot, lp_t.T, gate_t.T, ys, ysh, x1,
                     mod3, p["g_post_ffn"], S, tm_t, lmax)
    return x_out, v_first


_CFG = dict(tm_norm=512, tm_in=2048, tn_att=1024, tm_in_rwkv=1024, tn_rwkv=1664, tq=512, hp=4, tm_prep=256, nc=8, passes_intra=1, passes_state=1, pg=4, cb=8,
            tm_post=512, blk=256, tm_tile=256)

_LAYER_KEYS = ("g_pre_mix", "g_post_mix", "g_pre_ffn", "g_post_ffn", "w_in", "w_out", "lam_q1", "lam_k1",
               "lam_q2", "lam_k2", "att_subln_g", "rwkv_mu", "rwkv_w0", "rwkv_w2", "rwkv_a0", "rwkv_a2",
               "rwkv_g2", "rwkv_k_k", "rwkv_k_a", "rwkv_r_k", "rwkv_lnx_w", "rwkv_lnx_b", "w_router",
               "router_bias", "w_sh_gate", "w_sh_up", "w_sh_down")


def _forward(x, c, params, cfg):
    B, S, D = x.shape
    L = params["w_in"].shape[0]
    bp = 16
    c_pad = jnp.zeros((bp, D), f32).at[:B].set(c)
    mod = _ada_mod(c_pad, params["w_ada"], params["b_ada"])
    x2 = x.reshape(B * S, D)
    v_first = None
    for i in range(L):
        p = {k: params[k][i] for k in _LAYER_KEYS}
        if i > 0:
            p["rwkv_v0"] = params["rwkv_v0"][i - 1]
            p["rwkv_v1"] = params["rwkv_v1"][i - 1]
            p["rwkv_v2"] = params["rwkv_v2"][i - 1]
        mod3 = mod[i, :B].reshape(B * N_MOD, 1, D)
        wexp = (params["w_exp_gate"], params["w_exp_up"], params["w_exp_down"])
        x2, v_first = _layer(i, x2, mod3, p, wexp, v_first, B, S, cfg)
    return x2.reshape(B, S, D)


def kernel(x, c, w_ada, b_ada, g_pre_mix, g_post_mix, g_pre_ffn, g_post_ffn, w_in, w_out, lam_q1, lam_k1, lam_q2, lam_k2, att_subln_g, rwkv_mu, rwkv_w0, rwkv_w2, rwkv_a0, rwkv_a2, rwkv_g2, rwkv_k_k, rwkv_k_a, rwkv_r_k, rwkv_lnx_w, rwkv_lnx_b, rwkv_v0, rwkv_v1, rwkv_v2, w_router, router_bias, w_exp_gate, w_exp_up, w_exp_down, w_sh_gate, w_sh_up, w_sh_down):
    params = dict(w_ada=w_ada, b_ada=b_ada, g_pre_mix=g_pre_mix, g_post_mix=g_post_mix, g_pre_ffn=g_pre_ffn,
                  g_post_ffn=g_post_ffn, w_in=w_in, w_out=w_out, lam_q1=lam_q1, lam_k1=lam_k1, lam_q2=lam_q2,
                  lam_k2=lam_k2, att_subln_g=att_subln_g, rwkv_mu=rwkv_mu, rwkv_w0=rwkv_w0, rwkv_w2=rwkv_w2,
                  rwkv_a0=rwkv_a0, rwkv_a2=rwkv_a2, rwkv_g2=rwkv_g2, rwkv_k_k=rwkv_k_k, rwkv_k_a=rwkv_k_a,
                  rwkv_r_k=rwkv_r_k, rwkv_lnx_w=rwkv_lnx_w, rwkv_lnx_b=rwkv_lnx_b, rwkv_v0=rwkv_v0,
                  rwkv_v1=rwkv_v1, rwkv_v2=rwkv_v2, w_router=w_router, router_bias=router_bias,
                  w_exp_gate=w_exp_gate, w_exp_up=w_exp_up, w_exp_down=w_exp_down, w_sh_gate=w_sh_gate,
                  w_sh_up=w_sh_up, w_sh_down=w_sh_down)
    return _forward(x, c, params, _CFG)
```

```python
import functools
import math

import jax
import jax.numpy as jnp
from jax import lax
from jax.experimental import pallas as pl
from jax.experimental.pallas import tpu as pltpu

f32 = jnp.float32
bf16 = jnp.bfloat16
i32 = jnp.int32
u32 = jnp.uint32

ATT_QK_DIM = 64
ATT_V_DIM = 128
ALIBI_MAX_BIAS = 8.0
ATT_SUBLN_EPS = 1e-5
RWKV_HEAD = 64
RWKV_W_RANK = 64
RWKV_A_RANK = 64
RWKV_G_RANK = 128
RWKV_V_RANK = 32
RWKV_GN_EPS = 64e-5
N_EXPERTS = 64
N_GROUPS = 8
TOPK_GROUPS = 4
TOP_K = 8
ROUTED_SCALE = 2.5
NORM_EPS = 1e-6
N_MOD = 6

LANES = 128
SUBLANES = 8
CHUNK = 64
VMEM_LIMIT = 56 * 1024 * 1024


def _cparams(sem):
    return pltpu.CompilerParams(dimension_semantics=sem, vmem_limit_bytes=VMEM_LIMIT)


def _dot(a, b):
    return jnp.dot(a, b, preferred_element_type=f32)


def _dot_nt(a, b):
    return lax.dot_general(a, b, (((1,), (1,)), ((), ())), preferred_element_type=f32)


def _split2(x):
    hi = x.astype(bf16)
    lo = (x - hi.astype(f32)).astype(bf16)
    return hi, lo


def _mm(a, b, passes=1):
    if passes == 1:
        return _dot(a.astype(bf16), b.astype(bf16))
    ah, al = _split2(a.astype(f32))
    bh, bl = _split2(b.astype(f32))
    return (_dot(al, bh) + _dot(ah, bl)) + _dot(ah, bh)


def _mm_exact_rhs(a, b_bf16):
    ah, al = _split2(a)
    return _dot(al, b_bf16) + _dot(ah, b_bf16)


def _sigmoid(x):
    return 1.0 / (1.0 + jnp.exp(-x))


def _pack_rows(x):
    half = x.shape[1] // 2
    a = x[:, :half].astype(bf16).astype(f32)
    b = x[:, half:].astype(bf16).astype(f32)
    lo = lax.shift_right_logical(lax.bitcast_convert_type(a, u32), jnp.uint32(16))
    hi = lax.bitcast_convert_type(b, u32) & jnp.uint32(0xFFFF0000)
    return lo | hi


def _unpack_rows(w):
    lo = lax.bitcast_convert_type(lax.shift_left(w, jnp.uint32(16)), f32)
    hi = lax.bitcast_convert_type(w & jnp.uint32(0xFFFF0000), f32)
    return lo, hi


def _ada_kernel(c_ref, w_ref, b_ref, o_ref):
    c = c_ref[...]
    cond = (c * _sigmoid(c)).astype(bf16)
    o_ref[0] = _dot(cond, w_ref[0].astype(bf16)) + b_ref[0]


def _ada_mod(c_pad, w_ada, b_ada, tn=1024):
    L, D, N = w_ada.shape
    bp = c_pad.shape[0]
    return pl.pallas_call(
        _ada_kernel,
        grid=(L, N // tn),
        in_specs=[pl.BlockSpec((bp, D), lambda l, j: (0, 0)),
                  pl.BlockSpec((1, D, tn), lambda l, j: (l, 0, j)),
                  pl.BlockSpec((1, 1, tn), lambda l, j: (l, 0, j))],
        out_specs=pl.BlockSpec((1, bp, tn), lambda l, j: (l, 0, j)),
        out_shape=jax.ShapeDtypeStruct((L, bp, N), f32),
        compiler_params=_cparams(("arbitrary", "arbitrary")),
        name="ada_mod",
    )(c_pad, w_ada, b_ada.reshape(L, 1, N))


def _prenorm_kernel(x_ref, sc_ref, sh_ref, g_ref, h_ref):
    x = x_ref[...]
    ms = jnp.mean(x * x, axis=-1, keepdims=True)
    y = x * lax.rsqrt(ms + NORM_EPS) * g_ref[...]
    h_ref[...] = (y * (1.0 + sc_ref[0]) + sh_ref[0]).astype(h_ref.dtype)


def _prenorm(x2, mod3, g, S, tm, seg_sc, seg_sh):
    T, D = x2.shape
    tpb = S // tm
    tile = pl.BlockSpec((tm, D), lambda i: (i, 0))
    return pl.pallas_call(
        _prenorm_kernel,
        grid=(T // tm,),
        in_specs=[tile,
                  pl.BlockSpec((1, 1, D), lambda i: ((i // tpb) * N_MOD + seg_sc, 0, 0)),
                  pl.BlockSpec((1, 1, D), lambda i: ((i // tpb) * N_MOD + seg_sh, 0, 0)),
                  pl.BlockSpec((1, D), lambda i: (0, 0))],
        out_specs=tile,
        out_shape=jax.ShapeDtypeStruct((T, D), bf16),
        compiler_params=_cparams(("arbitrary",)),
        name="prenorm",
    )(x2, mod3, mod3, g.reshape(1, D))


def _inproj_kernel(h_ref, w_ref, o_ref):
    o_ref[...] = _dot(h_ref[...], w_ref[...]).astype(o_ref.dtype)


def _inproj(h, w_bf, out_dtype, tm, tn):
    T, D = h.shape
    N = w_bf.shape[1]
    return pl.pallas_call(
        _inproj_kernel,
        grid=(T // tm, N // tn),
        in_specs=[pl.BlockSpec((tm, D), lambda i, j: (i, 0)),
                  pl.BlockSpec((D, tn), lambda i, j: (0, j))],
        out_specs=pl.BlockSpec((tm, tn), lambda i, j: (i, j)),
        out_shape=jax.ShapeDtypeStruct((T, N), out_dtype),
        compiler_params=_cparams(("arbitrary", "arbitrary")),
        name="inproj",
    )(h, w_bf)


def _attn_kernel(q_ref, k_ref, v_ref, slope_ref, lamp_ref, g_ref, o_ref,
                 q2t_ref, vt_ref, m_ref, l_ref, acc_ref, *, tq, hp, lam_init):
    qi = pl.program_id(2)
    scale = ATT_QK_DIM ** -0.5
    heads = range(hp)
    hl = [slice(h * LANES, (h + 1) * LANES) for h in heads]
    slope = [slope_ref[h][:, 0:1] for h in heads]

    @pl.when(qi == 0)
    def _():
        for h in heads:
            vt_ref[h] = v_ref[:, hl[h]].astype(f32).T.astype(bf16)

    dim = lax.broadcasted_iota(i32, (LANES, 1), 0)
    first = dim < ATT_QK_DIM
    for h in heads:
        qt = (q_ref[:, hl[h]].astype(f32) * scale).T
        q2t_ref[h, :, 0:tq] = jnp.where(first, qt, 0.0).astype(bf16)
        q2t_ref[h, :, tq:2 * tq] = jnp.where(first, 0.0, qt).astype(bf16)
    m_ref[...] = jnp.full(m_ref.shape, -jnp.inf, f32)
    l_ref[...] = jnp.zeros(l_ref.shape, f32)
    acc_ref[...] = jnp.zeros(acc_ref.shape, f32)

    kr = lax.broadcasted_iota(i32, (tq, 2 * tq), 0)
    qc = lax.broadcasted_iota(i32, (tq, 2 * tq), 1)
    causal = jnp.where(qc >= tq, qc - tq, qc) >= kr
    krow = lax.broadcasted_iota(i32, (tq, 1), 0).astype(f32)

    def step(ki, masked):
        start = pl.multiple_of(ki * tq, tq)
        kpos = krow + (ki * tq).astype(f32)
        s = [_dot(k_ref[pl.ds(start, tq), hl[h]], q2t_ref[h]) for h in heads]
        s = [s[h] + slope[h] * kpos for h in heads]
        if masked:
            s = [jnp.where(causal, s[h], -jnp.inf) for h in heads]
        m_prev = [m_ref[h] for h in heads]
        m_new = [jnp.maximum(m_prev[h], jnp.max(s[h], axis=0, keepdims=True)) for h in heads]
        alpha = [jnp.exp(m_prev[h] - m_new[h]) for h in heads]
        p = [jnp.exp(s[h] - m_new[h]) for h in heads]
        for h in heads:
            l_ref[h] = alpha[h] * l_ref[h] + jnp.sum(p[h], axis=0, keepdims=True)
            acc_ref[h] = alpha[h] * acc_ref[h] + _dot(vt_ref[h, :, pl.ds(start, tq)], p[h].astype(bf16))
            m_ref[h] = m_new[h]

    def body(ki, carry):
        step(ki, False)
        return carry

    lax.fori_loop(0, qi, body, 0)
    step(qi, True)

    lp = lamp_ref[...]
    lam = (jnp.exp(jnp.sum(lp[0:1] * lp[1:2], axis=-1, keepdims=True))
           - jnp.exp(jnp.sum(lp[2:3] * lp[3:4], axis=-1, keepdims=True)) + lam_init)
    for h in heads:
        on = acc_ref[h] * (1.0 / l_ref[h])
        o = on[:, 0:tq] - lam * on[:, tq:2 * tq]
        o = o * lax.rsqrt(jnp.mean(o * o, axis=0, keepdims=True) + ATT_SUBLN_EPS)
        o = o * g_ref[...] * (1.0 - lam_init)
        o_ref[:, hl[h]] = o.T.astype(o_ref.dtype)


def _attention(att, slopes, lamp, subln_g, B, S, H, lam_init, tq, hp):
    T = att.shape[0]
    nq = S // tq
    kern = functools.partial(_attn_kernel, tq=tq, hp=hp, lam_init=lam_init)
    hw = hp * LANES
    return pl.pallas_call(
        kern,
        grid=(B, H // hp, nq),
        in_specs=[pl.BlockSpec((tq, hw), lambda b, g, q: (b * nq + q, g)),
                  pl.BlockSpec((S, hw), lambda b, g, q: (b, H // hp + g)),
                  pl.BlockSpec((S, hw), lambda b, g, q: (b, 2 * (H // hp) + g)),
                  pl.BlockSpec((hp, 1, LANES), lambda b, g, q: (g, 0, 0)),
                  pl.BlockSpec((4, ATT_QK_DIM), lambda b, g, q: (0, 0)),
                  pl.BlockSpec((ATT_V_DIM, 1), lambda b, g, q: (0, 0))],
        out_specs=pl.BlockSpec((tq, hw), lambda b, g, q: (b * nq + q, g)),
        out_shape=jax.ShapeDtypeStruct((T, H * ATT_V_DIM), bf16),
        scratch_shapes=[pltpu.VMEM((hp, LANES, 2 * tq), bf16),
                        pltpu.VMEM((hp, LANES, S), bf16),
                        pltpu.VMEM((hp, 1, 2 * tq), f32),
                        pltpu.VMEM((hp, 1, 2 * tq), f32),
                        pltpu.VMEM((hp, LANES, 2 * tq), f32)],
        compiler_params=_cparams(("arbitrary", "arbitrary", "arbitrary")),
        name="diff_attention",
    )(att, att, att, slopes, lamp, subln_g.reshape(ATT_V_DIM, 1))


def _head_sums(x, ind, indt):
    s = _mm_exact_rhs(x, ind)
    return _mm_exact_rhs(s, indt)


def _rwkv_prep_kernel(*refs, W, has_vres, tpb):
    if has_vres:
        (hx_ref, win_ref, mu_ref, w0_ref, w2_ref, a0_ref, a2_ref, g2_ref, kk_ref, ka_ref, ind_ref, indt_ref,
         vf_ref, v0_ref, v1_ref, v2_ref,
         r_o, lw_o, kh_o, v_o, kn_o, kb_o, g_o, carry_ref, feats_ref) = refs
    else:
        (hx_ref, win_ref, mu_ref, w0_ref, w2_ref, a0_ref, a2_ref, g2_ref, kk_ref, ka_ref, ind_ref, indt_ref,
         r_o, lw_o, kh_o, v_o, kn_o, kb_o, g_o, carry_ref, feats_ref) = refs

    i = pl.program_id(0)

    @pl.when(i == 0)
    def _():
        feats_ref[...] = jnp.zeros(feats_ref.shape, f32)
        carry_ref[...] = jnp.zeros(carry_ref.shape, f32)

    h = feats_ref[...]
    nxt = _dot(hx_ref[...], win_ref[...])
    tm = h.shape[0]

    first_in_batch = lax.rem(jnp.maximum(i - 1, 0), tpb) == 0
    carry = jnp.where(first_in_batch, 0.0, carry_ref[...])
    rolled = pltpu.roll(h, 1, axis=0)
    row = lax.broadcasted_iota(i32, (tm, 1), 0)
    prev = jnp.where(row == 0, carry, rolled)
    carry_ref[...] = h[tm - 1:tm, :]
    feats_ref[...] = nxt
    feats = h + (prev - h) * mu_ref[...]

    r = feats[:, 0:W]
    k = feats[:, W:2 * W]
    v = feats[:, 2 * W:3 * W]
    wa = feats[:, 3 * W:3 * W + LANES]
    g_lo = feats[:, 3 * W + LANES:3 * W + 2 * LANES]

    w = w0_ref[...] + _mm(jnp.tanh(wa), w2_ref[...], passes=3)
    lw_o[...] = -math.exp(-0.5) * _sigmoid(w)
    a = _sigmoid(a0_ref[...] + _mm(wa, a2_ref[...], passes=3))
    g_o[...] = _mm(_sigmoid(g_lo), g2_ref[...]).astype(g_o.dtype)

    if has_vres:
        mix = _sigmoid(v0_ref[...] + _mm(_mm(v, v1_ref[...]), v2_ref[...]))
        v = v + (vf_ref[...].astype(f32) - v) * mix

    kk = k * kk_ref[...]
    ss = _head_sums(kk * kk, ind_ref[...], indt_ref[...])
    kk = kk / jnp.maximum(jnp.sqrt(ss), 1e-12)
    r_o[...] = r.astype(r_o.dtype)
    kh_o[...] = (k * (1.0 + (a - 1.0) * ka_ref[...])).astype(kh_o.dtype)
    v_o[...] = v.astype(v_o.dtype)
    kn_o[...] = kk.astype(kn_o.dtype)
    kb_o[...] = (kk * a).astype(kb_o.dtype)


def _rwkv_prep(hx, w_rwkv, prm, vfirst, B, S, W, tm):
    T, D = hx.shape
    COLS = w_rwkv.shape[1]
    tpb = S // tm
    n = T // tm
    has_vres = vfirst is not None
    row = lambda c: pl.BlockSpec((1, c), lambda i: (0, 0))
    full = lambda a: pl.BlockSpec(a.shape, lambda i: (0, 0))
    tile = pl.BlockSpec((tm, W), lambda i: (jnp.maximum(i - 1, 0), 0))
    args = [hx, w_rwkv, prm["mu"], prm["w0"], prm["w2p"], prm["a0"], prm["a2p"], prm["g2"], prm["k_k"], prm["k_a"],
            prm["ind"], prm["indt"]]
    specs = [pl.BlockSpec((tm, D), lambda i: (jnp.minimum(i, n - 1), 0)),
             pl.BlockSpec((D, COLS), lambda i: (0, 0), pipeline_mode=pl.Buffered(1)),
             row(COLS), row(W), full(prm["w2p"]),
             row(W), full(prm["a2p"]), full(prm["g2"]), row(W), row(W), full(prm["ind"]), full(prm["indt"])]
    if has_vres:
        args += [vfirst, prm["v0"], prm["v1p"], prm["v2p"]]
        specs += [tile, row(W), full(prm["v1p"]), full(prm["v2p"])]
    out_dtypes = (bf16, f32, bf16, bf16, bf16, bf16, bf16)
    kern = functools.partial(_rwkv_prep_kernel, W=W, has_vres=has_vres, tpb=tpb)
    return pl.pallas_call(
        kern,
        grid=(n + 1,),
        in_specs=specs,
        out_specs=[tile] * 7,
        out_shape=[jax.ShapeDtypeStruct((T, W), dt) for dt in out_dtypes],
        scratch_shapes=[pltpu.VMEM((1, COLS), f32), pltpu.VMEM((tm, COLS), f32)],
        compiler_params=_cparams(("arbitrary",)),
        name="rwkv_prep",
    )(*args)


def _wkv_chunks(rs, lws, ks, vs, kns, kbs, passes):
    C = CHUNK
    P2 = 2 * C
    n = range(len(rs))
    ri = lax.broadcasted_iota(i32, (C, C), 0)
    ci = lax.broadcasted_iota(i32, (C, C), 1)
    tri = (ci <= ri).astype(bf16)
    lane = lax.broadcasted_iota(i32, (1, LANES), 1)
    m0 = (lane < RWKV_HEAD).astype(f32)
    m1 = 1.0 - m0
    rr = lax.broadcasted_iota(i32, (P2, P2), 0)
    cc = lax.broadcasted_iota(i32, (P2, P2), 1)
    same = jnp.where(rr >= C, 1, 0) == jnp.where(cc >= C, 1, 0)
    strict = same & (cc < rr)
    incl = same & (cc <= rr)
    incl2 = jnp.concatenate([incl, incl], axis=1)
    eye = (rr == cc).astype(f32)
    zeros_p = jnp.zeros((P2, LANES), f32)
    zeros_c = jnp.zeros((C, LANES), f32)
    stack = lambda x: jnp.concatenate([x * m0, x * m1], axis=0)
    fold = lambda x: x[0:C] + x[C:2 * C]

    def cumsum(lw):
        h1 = lw.astype(bf16)
        r1 = lw - h1.astype(f32)
        h2 = r1.astype(bf16)
        h3 = (r1 - h2.astype(f32)).astype(bf16)
        return (_dot(tri, h3) + _dot(tri, h2)) + _dot(tri, h1)

    cum = [cumsum(lws[j]) for j in n]
    cum_c = [cum[j][C - 1:C, :] for j in n]
    at = [-kns[j] * jnp.exp(cum[j] - lws[j]) for j in n]
    rt = [rs[j] * jnp.exp(cum[j]) for j in n]
    einv = [jnp.exp(-cum[j]) for j in n]
    bt = [kbs[j] * einv[j] for j in n]
    kt = [ks[j] * einv[j] for j in n]
    eh = [jnp.exp(cum_c[j] - cum[j]) for j in n]
    bh = [kbs[j] * eh[j] for j in n]
    kh = [ks[j] * eh[j] for j in n]
    w_c = [jnp.exp(cum_c[j]) for j in n]
    abd = [stack(at[j]) for j in n]
    vst = [stack(vs[j]) for j in n]
    lhs = [jnp.concatenate([abd[j], stack(rt[j])], axis=0) for j in n]
    rhs = [jnp.concatenate([stack(bt[j]), stack(kt[j])], axis=0) for j in n]
    gram = [_mm_nt(lhs[j], rhs[j], passes) for j in n]
    lab = [jnp.where(strict, gram[j][0:P2, 0:P2], 0.0) for j in n]
    lak = [jnp.where(strict, gram[j][0:P2, P2:2 * P2], 0.0) for j in n]
    mrbk = [jnp.where(incl2, gram[j][P2:2 * P2, :], 0.0) for j in n]

    x0 = [_mm(lak[j], vst[j], passes) for j in n]
    tinv = [eye + lab[j] for j in n]
    lp = [_mm(lab[j], lab[j], passes) for j in n]
    n_sq = int(math.log2(C)) - 1
    for it in range(n_sq):
        if it < n_sq - 1:
            both = [_mm(lp[j], jnp.concatenate([lp[j], tinv[j]], axis=1), passes) for j in n]
            tinv = [tinv[j] + both[j][:, P2:2 * P2] for j in n]
            lp = [both[j][:, 0:P2] for j in n]
        else:
            tinv = [tinv[j] + _mm(lp[j], tinv[j], passes) for j in n]

    ta = [_mm(tinv[j], jnp.concatenate([abd[j], x0[j]], axis=1), passes) for j in n]
    rhs2 = [jnp.concatenate([ta[j], jnp.concatenate([zeros_p, vst[j]], axis=1)], axis=0) for j in n]
    z = [_mm(mrbk[j], rhs2[j], passes) for j in n]
    r2 = [rt[j] + fold(z[j][:, 0:LANES]) for j in n]
    y0 = [fold(z[j][:, LANES:2 * LANES]) for j in n]
    lhs3t = [jnp.concatenate([bh[j], kh[j]], axis=0).T for j in n]
    rhs3 = [jnp.concatenate([fold(ta[j]), jnp.concatenate([zeros_c, vs[j]], axis=1)], axis=0) for j in n]
    wmat = [_mm(lhs3t[j], rhs3[j], passes) for j in n]
    mmat = [jnp.where(same, wmat[j][:, 0:LANES], 0.0) + eye * w_c[j] for j in n]
    g0 = [jnp.where(same, wmat[j][:, LANES:2 * LANES], 0.0) for j in n]
    return r2, y0, mmat, g0


def _mm_nt(a, b, passes):
    if passes == 1:
        return _dot_nt(a.astype(bf16), b.astype(bf16))
    ah, al = _split2(a)
    bh, bl = _split2(b)
    return (_dot_nt(al, bh) + _dot_nt(ah, bl)) + _dot_nt(ah, bh)


def _wkv_intra_kernel(r_ref, lw_ref, k_ref, v_ref, kn_ref, kb_ref, r2_o, y0_o, m_o, g_o, *, nc, passes):
    C = CHUNK
    sls = [slice(c * C, (c + 1) * C) for c in range(nc)]
    take = lambda ref: [ref[sl, :].astype(f32) for sl in sls]
    r2, y0, mmat, g0 = _wkv_chunks(take(r_ref), take(lw_ref), take(k_ref), take(v_ref),
                                   take(kn_ref), take(kb_ref), passes)
    for c, sl in enumerate(sls):
        r2_o[sl, :] = r2[c].astype(r2_o.dtype)
        y0_o[sl, :] = y0[c].astype(y0_o.dtype)
        m_o[0, c] = mmat[c].astype(m_o.dtype)
        g_o[0, c] = g0[c]


def _wkv_intra(r, lw, kh, v, kn, kb, nc, passes):
    T, W = r.shape
    npair = W // LANES
    rows = nc * CHUNK
    tile = pl.BlockSpec((rows, LANES), lambda p, i: (i, p))
    mat = pl.BlockSpec((1, nc, LANES, LANES), lambda p, i: (p, i, 0, 0))
    kern = functools.partial(_wkv_intra_kernel, nc=nc, passes=passes)
    return pl.pallas_call(
        kern,
        grid=(npair, T // rows),
        in_specs=[tile] * 6,
        out_specs=[tile, tile, mat, mat],
        out_shape=[jax.ShapeDtypeStruct((T, W), bf16), jax.ShapeDtypeStruct((T, W), bf16),
                   jax.ShapeDtypeStruct((npair, T // CHUNK, LANES, LANES), bf16),
                   jax.ShapeDtypeStruct((npair, T // CHUNK, LANES, LANES), f32)],
        compiler_params=_cparams(("arbitrary", "arbitrary")),
        name="wkv_intra",
    )(r, lw, kh, v, kn, kb)


def _wkv_state_kernel(r2_ref, y0_ref, m_ref, g0_ref, r_ref, kh_ref, v_ref, g_ref,
                      lnw_ref, lnb_ref, rk_ref, o_ref, st_ref, y_ref, *, pg, cb, passes):
    C = CHUNK

    @pl.when(pl.program_id(2) == 0)
    def _():
        st_ref[...] = jnp.zeros(st_ref.shape, f32)

    pairs = range(pg)
    lanes = [slice(p * LANES, (p + 1) * LANES) for p in pairs]
    st = [st_ref[p] for p in pairs]
    for c in range(cb):
        rows = slice(c * C, (c + 1) * C)
        for p in pairs:
            y_ref[rows, lanes[p]] = (_mm(r2_ref[rows, lanes[p]], st[p], passes)
                                     + y0_ref[rows, lanes[p]].astype(f32))
        st = [_mm(m_ref[p, c], st[p], passes) + g0_ref[p, c] for p in pairs]
    for p in pairs:
        st_ref[p] = st[p]

    rr = lax.broadcasted_iota(i32, (LANES, LANES), 0)
    cc = lax.broadcasted_iota(i32, (LANES, LANES), 1)
    ones_bd = (jnp.where(rr >= RWKV_HEAD, 1, 0) == jnp.where(cc >= RWKV_HEAD, 1, 0)).astype(bf16)
    for p in pairs:
        y = y_ref[:, lanes[p]]
        mu = _mm_exact_rhs(y, ones_bd) * (1.0 / RWKV_HEAD)
        d = y - mu
        var = _mm_exact_rhs(d * d, ones_bd) * (1.0 / RWKV_HEAD)
        yn = d * lax.rsqrt(var + RWKV_GN_EPS) * lnw_ref[:, lanes[p]] + lnb_ref[:, lanes[p]]
        rk = r_ref[:, lanes[p]].astype(f32) * kh_ref[:, lanes[p]].astype(f32) * rk_ref[:, lanes[p]]
        bonus = _mm_exact_rhs(rk, ones_bd) * v_ref[:, lanes[p]]
        o_ref[:, lanes[p]] = ((yn + bonus) * g_ref[:, lanes[p]]).astype(o_ref.dtype)


def _wkv_state(r2, y0, mm, g0, r, kh, v, g, lnw, lnb, rk, B, S, pg, cb, passes):
    T, W = r.shape
    npair = W // LANES
    rows = cb * CHUNK
    steps = S // rows
    seq = pl.BlockSpec((rows, pg * LANES), lambda b, q, c: (b * steps + c, q))
    mat = pl.BlockSpec((pg, cb, LANES, LANES), lambda b, q, c: (q, b * steps + c, 0, 0))
    prow = pl.BlockSpec((1, pg * LANES), lambda b, q, c: (0, q))
    kern = functools.partial(_wkv_state_kernel, pg=pg, cb=cb, passes=passes)
    return pl.pallas_call(
        kern,
        grid=(B, npair // pg, steps),
        in_specs=[seq, seq, mat, mat, seq, seq, seq, seq, prow, prow, prow],
        out_specs=seq,
        out_shape=jax.ShapeDtypeStruct((T, W), bf16),
        scratch_shapes=[pltpu.VMEM((pg, LANES, LANES), f32), pltpu.VMEM((rows, pg * LANES), f32)],
        compiler_params=_cparams(("arbitrary", "arbitrary", "arbitrary")),
        name="wkv_state",
    )(r2, y0, mm, g0, r, kh, v, g, lnw, lnb, rk)


def _postmix_kernel(oa_ref, orw_ref, wa_ref, wr_ref, x_ref, ga_ref, gpost_ref, gpre_ref, sc_ref, sh_ref, wrt_ref,
                    x1_o, h2_o, lg_o):
    mixed = _dot(oa_ref[...], wa_ref[...]) + _dot(orw_ref[...], wr_ref[...])
    ms = jnp.mean(mixed * mixed, axis=-1, keepdims=True)
    x1 = x_ref[...] + ga_ref[0] * (mixed * lax.rsqrt(ms + NORM_EPS) * gpost_ref[...])
    x1_o[...] = x1
    ms1 = jnp.mean(x1 * x1, axis=-1, keepdims=True)
    h2 = (x1 * lax.rsqrt(ms1 + NORM_EPS) * gpre_ref[...]) * (1.0 + sc_ref[0]) + sh_ref[0]
    h2_o[...] = _pack_rows(h2)
    lg_o[...] = _mm_nt(wrt_ref[...], h2, 3)


def _postmix(o_att, o_rwkv, w_out_a, w_out_r, x2, mod3, g_post, g_pre, w_rt, S, tm):
    T, D = x2.shape
    WA = o_att.shape[1]
    WR = o_rwkv.shape[1]
    E = w_rt.shape[0]
    tpb = S // tm
    modspec = lambda seg: pl.BlockSpec((1, 1, D), lambda i: ((i // tpb) * N_MOD + seg, 0, 0))
    tile = pl.BlockSpec((tm, D), lambda i: (i, 0))
    return pl.pallas_call(
        _postmix_kernel,
        grid=(T // tm,),
        in_specs=[pl.BlockSpec((tm, WA), lambda i: (i, 0)),
                  pl.BlockSpec((tm, WR), lambda i: (i, 0)),
                  pl.BlockSpec((WA, D), lambda i: (0, 0), pipeline_mode=pl.Buffered(1)),
                  pl.BlockSpec((WR, D), lambda i: (0, 0), pipeline_mode=pl.Buffered(1)),
                  tile, modspec(2),
                  pl.BlockSpec((1, D), lambda i: (0, 0)),
                  pl.BlockSpec((1, D), lambda i: (0, 0)),
                  modspec(4), modspec(3),
                  pl.BlockSpec((E, D), lambda i: (0, 0))],
        out_specs=[tile, pl.BlockSpec((tm, D // 2), lambda i: (i, 0)),
                   pl.BlockSpec((E, tm), lambda i: (0, i))],
        out_shape=[jax.ShapeDtypeStruct((T, D), f32), jax.ShapeDtypeStruct((T, D // 2), u32),
                   jax.ShapeDtypeStruct((E, T), f32)],
        compiler_params=_cparams(("arbitrary",)),
        name="postmix",
    )(o_att, o_rwkv, w_out_a, w_out_r, x2, mod3, g_post.reshape(1, D), g_pre.reshape(1, D), mod3, mod3, w_rt)


def _first_max(x, iota, n):
    mx = jnp.max(x, axis=0, keepdims=True)
    idx = jnp.min(jnp.where(x == mx, iota, n), axis=0, keepdims=True)
    return mx, idx


def _router_kernel(lg_ref, bias_ref, eidx_o, gate_o, rank_o, cnt_o, cnt_ref):
    E = N_EXPERTS
    G = N_GROUPS
    per = E // G
    tm = lg_ref.shape[1]

    @pl.when(pl.program_id(0) == 0)
    def _():
        cnt_ref[...] = jnp.zeros(cnt_ref.shape, f32)

    scores = _sigmoid(lg_ref[...])
    biased = scores + bias_ref[...]
    neg = -jnp.inf

    iota_p = lax.broadcasted_iota(i32, (per, tm), 0).astype(f32)
    gs = []
    for g in range(G):
        xg = biased[g * per:(g + 1) * per, :]
        m1, i1 = _first_max(xg, iota_p, per)
        m2 = jnp.max(jnp.where(iota_p == i1, neg, xg), axis=0, keepdims=True)
        gs.append(m1 + m2)
    gsc = jnp.concatenate(gs, axis=0)
    iota_g = lax.broadcasted_iota(i32, (G, tm), 0).astype(f32)
    gsel = jnp.zeros((G, tm), f32)
    for _ in range(TOPK_GROUPS):
        _, gi = _first_max(gsc, iota_g, G)
        hit = iota_g == gi
        gsel = jnp.where(hit, 1.0, gsel)
        gsc = jnp.where(hit, neg, gsc)
    masked = jnp.concatenate(
        [jnp.where(gsel[g:g + 1, :] > 0.0, biased[g * per:(g + 1) * per, :], neg) for g in range(G)], axis=0)

    iota_e = lax.broadcasted_iota(i32, (E, tm), 0).astype(f32)
    sel = jnp.zeros((E, tm), f32)
    idxs, vals = [], []
    for _ in range(TOP_K):
        _, ei = _first_max(masked, iota_e, E)
        hit = iota_e == ei
        idxs.append(ei)
        vals.append(jnp.sum(jnp.where(hit, scores, 0.0), axis=0, keepdims=True))
        sel = jnp.where(hit, 1.0, sel)
        masked = jnp.where(hit, neg, masked)
    tot = vals[0]
    for vv in vals[1:]:
        tot = tot + vv
    eidx_o[...] = jnp.concatenate(idxs, axis=0).astype(i32)
    gate_o[...] = jnp.concatenate([vv / tot * ROUTED_SCALE for vv in vals], axis=0)

    rr = lax.broadcasted_iota(i32, (tm, tm), 0)
    cc = lax.broadcasted_iota(i32, (tm, tm), 1)
    before = (rr < cc).astype(bf16)
    pos = _dot(sel.astype(bf16), before)
    rank_o[...] = jnp.concatenate(
        [jnp.sum(jnp.where(iota_e == ei, pos, 0.0), axis=0, keepdims=True) for ei in idxs], axis=0)
    lane = lax.broadcasted_iota(i32, cnt_ref.shape, 1)
    cnt_ref[...] = jnp.where(lane == pl.program_id(0), jnp.sum(sel, axis=1, keepdims=True), cnt_ref[...])
    cnt_o[...] = cnt_ref[...].astype(i32)


def _router(logits_t, bias, tm):
    E, T = logits_t.shape
    assert T // tm <= LANES
    k_tile = pl.BlockSpec((TOP_K, tm), lambda i: (0, i))
    return pl.pallas_call(
        _router_kernel,
        grid=(T // tm,),
        in_specs=[pl.BlockSpec((E, tm), lambda i: (0, i)),
                  pl.BlockSpec((E, 1), lambda i: (0, 0))],
        out_specs=[k_tile, k_tile, k_tile, pl.BlockSpec((E, LANES), lambda i: (0, 0))],
        out_shape=[jax.ShapeDtypeStruct((TOP_K, T), i32), jax.ShapeDtypeStruct((TOP_K, T), f32),
                   jax.ShapeDtypeStruct((TOP_K, T), f32), jax.ShapeDtypeStruct((E, LANES), i32)],
        scratch_shapes=[pltpu.VMEM((E, LANES), f32)],
        compiler_params=_cparams(("arbitrary",)),
        name="router",
    )(logits_t, bias.reshape(E, 1))


def _row_copy(src_ref, s, dst_ref, d, sem):
    return pltpu.make_async_copy(src_ref.at[pl.ds(s, 1), :], dst_ref.at[pl.ds(d, 1), :], sem)


def _zero_fill(cnt_ref, pstart_ref, nused_ref, z_ref, xs_out, sem, blk, nblk, start):
    def act(cp):
        if start:
            cp.start()
        else:
            cp.wait()

    def per_expert(e, carry):
        c = cnt_ref[e]
        base = pstart_ref[e] + c
        npad = (blk - (c & (blk - 1))) & (blk - 1)
        head = (-base) & (SUBLANES - 1)

        def one_row(j, carry2):
            act(_row_copy(z_ref, 0, xs_out, base + j, sem))
            return carry2

        lax.fori_loop(0, head, one_row, 0)
        rem = npad - head
        aligned = base + head
        p = blk // 2
        while p >= SUBLANES:
            off = pl.multiple_of(aligned + (rem & ~(2 * p - 1)), SUBLANES)

            @pl.when((rem & p) != 0)
            def _(p=p, off=off):
                act(pltpu.make_async_copy(z_ref.at[pl.ds(0, p), :], xs_out.at[pl.ds(off, p), :], sem))

            p //= 2
        return carry

    lax.fori_loop(0, N_EXPERTS, per_expert, 0)

    def per_block(b, carry):
        act(pltpu.make_async_copy(z_ref, xs_out.at[pl.ds(pl.multiple_of(b * blk, blk), blk), :], sem))
        return carry

    lax.fori_loop(nused_ref[0], nblk, per_block, 0)


def _piece_sizes(tm):
    return [tm >> s for s in range(tm.bit_length()) if (tm >> s) >= SUBLANES]


def _seg_copies(psrc_ref, pdst_ref, pcnt_ref, tile, buf_ref, hbm_ref, sem, tm, to_hbm):
    ntiles = pl.num_programs(0)
    for s, p in enumerate(_piece_sizes(tm)):
        base = (s * ntiles + tile) * N_EXPERTS

        def one(j, carry, p=p, base=base):
            v = buf_ref.at[pl.ds(pl.multiple_of(psrc_ref[base + j], SUBLANES), p), :]
            h = hbm_ref.at[pl.ds(pl.multiple_of(pdst_ref[base + j], SUBLANES), p), :]
            (pltpu.make_async_copy(v, h, sem) if to_hbm else pltpu.make_async_copy(h, v, sem)).start()
            return carry

        lax.fori_loop(0, pcnt_ref[s * ntiles + tile], one, 0)


def _seg_wait(total_rows, buf_ref, hbm_ref, sem, to_hbm):
    p = 1 << (buf_ref.shape[0].bit_length() - 1)
    while p >= SUBLANES:
        @pl.when((total_rows & p) != 0)
        def _(p=p):
            v = buf_ref.at[pl.ds(0, p), :]
            h = hbm_ref.at[pl.ds(0, p), :]
            (pltpu.make_async_copy(v, h, sem) if to_hbm else pltpu.make_async_copy(h, v, sem)).wait()

        p //= 2


SORT_CHUNK = 512


def _dispatch_kernel(ssrc_ref, sdst_ref, sn_ref, stot_ref, cnt_ref, pstart_ref, nused_ref, lp_ref, h_ref,
                     wsg_ref, wsu_ref, wsd_ref, xs_out, ysh_o, sb_ref, z_ref, sems, sem_z, *, blk, nblk):
    i = pl.program_id(0)
    last = pl.num_programs(0) - 1
    tm = h_ref.shape[0]
    lmax = sb_ref.shape[1]
    slot = i % 2
    seg = functools.partial(_seg_copies, ssrc_ref, sdst_ref, sn_ref, hbm_ref=xs_out, tm=tm, to_hbm=True)

    def seg_wait(tile, s):
        _seg_wait(stot_ref[tile], sb_ref.at[s], xs_out, sems.at[s], True)

    @pl.when(i >= 2)
    def _():
        seg_wait(i - 2, slot)

    hlo, hhi = _unpack_rows(h_ref[...])
    hb = jnp.concatenate([hlo.astype(bf16), hhi.astype(bf16)], axis=1)
    lpv = lp_ref[...]
    half = hb.shape[1] // 2
    for c in range(lmax // SORT_CHUNK):
        jj = (lax.broadcasted_iota(i32, (SORT_CHUNK, tm), 0) + c * SORT_CHUNK).astype(f32)
        onehot = jnp.zeros((SORT_CHUNK, tm), f32)
        for k in range(TOP_K):
            onehot = jnp.where(jj == lpv[k:k + 1, :], 1.0, onehot)
        rows = _dot(onehot.astype(bf16), hb)
        lo = lax.shift_right_logical(lax.bitcast_convert_type(rows[:, :half], u32), jnp.uint32(16))
        hi = lax.bitcast_convert_type(rows[:, half:], u32) & jnp.uint32(0xFFFF0000)
        sb_ref[slot, c * SORT_CHUNK:(c + 1) * SORT_CHUNK, :] = lo | hi

    seg(i, buf_ref=sb_ref.at[slot], sem=sems.at[slot])

    gt = _dot(hb, wsg_ref[...])
    up = _dot(hb, wsu_ref[...])
    ysh_o[...] = _dot(((gt * _sigmoid(gt)) * up).astype(bf16), wsd_ref[...]).astype(ysh_o.dtype)

    @pl.when(i == last)
    def _():
        z_ref[...] = jnp.zeros(z_ref.shape, u32)
        _zero_fill(cnt_ref, pstart_ref, nused_ref, z_ref, xs_out, sem_z, blk, nblk, True)
        _zero_fill(cnt_ref, pstart_ref, nused_ref, z_ref, xs_out, sem_z, blk, nblk, False)
        seg_wait(i, slot)

        @pl.when(i >= 1)
        def _():
            seg_wait(i - 1, 1 - slot)


def _dispatch(seg_src, seg_dst, seg_n, seg_tot, counts, pad_start, nused, lp_t, h2p, wsg, wsu, wsd,
              P, blk, tm, lmax):
    T, DW = h2p.shape
    D, DS = wsg.shape
    assert blk & (blk - 1) == 0 and tm & (tm - 1) == 0 and lmax % SORT_CHUNK == 0
    kern = functools.partial(_dispatch_kernel, blk=blk, nblk=P // blk)
    const = lambda shape: pl.BlockSpec(shape, lambda i, *_: (0, 0))
    grid_spec = pltpu.PrefetchScalarGridSpec(
        num_scalar_prefetch=7,
        grid=(T // tm,),
        in_specs=[pl.BlockSpec((TOP_K, tm), lambda i, *_: (0, i)),
                  pl.BlockSpec((tm, DW), lambda i, *_: (i, 0)),
                  const((D, DS)), const((D, DS)), const((DS, D))],
        out_specs=[pl.BlockSpec(memory_space=pl.ANY), pl.BlockSpec((tm, D), lambda i, *_: (i, 0))],
        scratch_shapes=[pltpu.VMEM((2, lmax, DW), u32), pltpu.VMEM((blk, DW), u32),
                        pltpu.SemaphoreType.DMA((2,)), pltpu.SemaphoreType.DMA],
    )
    return pl.pallas_call(
        kern,
        grid_spec=grid_spec,
        out_shape=[jax.ShapeDtypeStruct((P, DW), u32), jax.ShapeDtypeStruct((T, D), bf16)],
        compiler_params=_cparams(("arbitrary",)),
        name="moe_dispatch",
    )(seg_src, seg_dst, seg_n, seg_tot, counts, pad_start, nused, lp_t, h2p, wsg, wsu, wsd)


def _experts_kernel(blk_e_ref, nxt_e_ref, nused_ref, xs_ref, wg_hbm, wu_hbm, wd_hbm, ys_ref,
                    wg_f, wu_f, wd_f, wg_s, wu_s, wd_s, sems, *, layer):
    i = pl.program_id(0)
    e = blk_e_ref[i]
    changed = jnp.logical_or(i == 0, e != blk_e_ref[jnp.maximum(i - 1, 0)])

    def weight_copies(ex):
        return (pltpu.make_async_copy(wg_hbm.at[layer, ex], wg_f, sems.at[0]),
                pltpu.make_async_copy(wu_hbm.at[layer, ex], wu_f, sems.at[1]),
                pltpu.make_async_copy(wd_hbm.at[layer, ex], wd_f, sems.at[2]))

    @pl.when(i == 0)
    def _():
        for cp in weight_copies(e):
            cp.start()

    @pl.when(changed)
    def _():
        for cp in weight_copies(e):
            cp.wait()
        for src, dst in ((wg_f, wg_s), (wu_f, wu_s), (wd_f, wd_s)):
            rows = src.shape[0] // 8
            for c in range(8):
                dst[c * rows:(c + 1) * rows, :] = src[c * rows:(c + 1) * rows, :].astype(bf16)
        nxt = nxt_e_ref[i]

        @pl.when(nxt >= 0)
        def _():
            for cp in weight_copies(nxt):
                cp.start()

    @pl.when(i < nused_ref[0])
    def _():
        lo, hi = _unpack_rows(xs_ref[...])
        x = jnp.concatenate([lo.astype(bf16), hi.astype(bf16)], axis=1)
        gt = _dot(x, wg_s[...])
        up = _dot(x, wu_s[...])
        hmid = (gt * _sigmoid(gt)) * up
        ys_ref[...] = _pack_rows(_dot(hmid.astype(bf16), wd_s[...]))

    @pl.when(i >= nused_ref[0])
    def _():
        ys_ref[...] = jnp.zeros(ys_ref.shape, u32)


def _experts(blk_e, nxt_e, nused, xs, w_gate, w_up, w_down, layer, blk):
    P, DW = xs.shape
    D, DE = w_gate.shape[-2:]
    nblk = P // blk
    row_idx = lambda i, be, nx, nu: (jnp.minimum(i, nu[0] - 1), 0)
    hbm = pl.BlockSpec(memory_space=pl.ANY)
    grid_spec = pltpu.PrefetchScalarGridSpec(
        num_scalar_prefetch=3,
        grid=(nblk,),
        in_specs=[pl.BlockSpec((blk, DW), row_idx), hbm, hbm, hbm],
        out_specs=pl.BlockSpec((blk, DW), lambda i, be, nx, nu: (i, 0)),
        scratch_shapes=[pltpu.VMEM((D, DE), f32), pltpu.VMEM((D, DE), f32), pltpu.VMEM((DE, D), f32),
                        pltpu.VMEM((D, DE), bf16), pltpu.VMEM((D, DE), bf16), pltpu.VMEM((DE, D), bf16),
                        pltpu.SemaphoreType.DMA((3,))],
    )
    return pl.pallas_call(
        functools.partial(_experts_kernel, layer=layer),
        grid_spec=grid_spec,
        out_shape=jax.ShapeDtypeStruct((P, DW), u32),
        compiler_params=_cparams(("arbitrary",)),
        name="moe_experts",
    )(blk_e, nxt_e, nused, xs, w_gate, w_up, w_down)


def _combine_kernel(ssrc_ref, sdst_ref, sn_ref, stot_ref, lp_ref, gate_ref, ys_hbm, ysh_ref, x1_ref, gf_ref,
                    gpost_ref, x2_o, yb_ref, sems):
    i = pl.program_id(0)
    n = pl.num_programs(0)
    tm = x1_ref.shape[0]
    lmax = yb_ref.shape[1]
    slot = i % 2
    seg = functools.partial(_seg_copies, ssrc_ref, sdst_ref, sn_ref, hbm_ref=ys_hbm, tm=tm, to_hbm=False)

    @pl.when(i == 0)
    def _():
        yb_ref[...] = jnp.zeros(yb_ref.shape, u32)
        seg(i, buf_ref=yb_ref.at[slot], sem=sems.at[slot])

    @pl.when(i + 1 < n)
    def _():
        seg(i + 1, buf_ref=yb_ref.at[1 - slot], sem=sems.at[1 - slot])

    _seg_wait(stot_ref[i], yb_ref.at[slot], ys_hbm, sems.at[slot], False)

    lp = lp_ref[...]
    gate = gate_ref[...]
    ysh = ysh_ref[...].astype(f32)
    half = ysh.shape[1] // 2
    lo = ysh[:, :half]
    hi = ysh[:, half:]
    for c in range(lmax // SORT_CHUNK):
        jl = (lax.broadcasted_iota(i32, (tm, SORT_CHUNK), 1) + c * SORT_CHUNK).astype(f32)
        g = jnp.zeros((tm, SORT_CHUNK), f32)
        for k in range(TOP_K):
            g = jnp.where(jl == lp[:, k:k + 1], gate[:, k:k + 1], g)
        gb = g.astype(bf16)
        a, b = _unpack_rows(yb_ref[slot, c * SORT_CHUNK:(c + 1) * SORT_CHUNK, :])
        lo = lo + _dot(gb, a.astype(bf16))
        hi = hi + _dot(gb, b.astype(bf16))
    y = jnp.concatenate([lo, hi], axis=1)
    ms = jnp.mean(y * y, axis=-1, keepdims=True)
    x2_o[...] = x1_ref[...] + gf_ref[0] * (y * lax.rsqrt(ms + NORM_EPS) * gpost_ref[...])


def _combine(seg_src, seg_dst, seg_n, seg_tot, lp_tk, gate_tk, ys, ysh, x1, mod3, g_post, S, tm, lmax):
    T, D = x1.shape
    DW = ys.shape[1]
    tpb = S // tm
    tile = pl.BlockSpec((tm, D), lambda i, *_: (i, 0))
    ktile = pl.BlockSpec((tm, TOP_K), lambda i, *_: (i, 0))
    grid_spec = pltpu.PrefetchScalarGridSpec(
        num_scalar_prefetch=4,
        grid=(T // tm,),
        in_specs=[ktile, ktile,
                  pl.BlockSpec(memory_space=pl.ANY),
                  tile, tile,
                  pl.BlockSpec((1, 1, D), lambda i, *_: ((i // tpb) * N_MOD + 5, 0, 0)),
                  pl.BlockSpec((1, D), lambda i, *_: (0, 0))],
        out_specs=tile,
        scratch_shapes=[pltpu.VMEM((2, lmax, DW), u32), pltpu.SemaphoreType.DMA((2,))],
    )
    return pl.pallas_call(
        _combine_kernel,
        grid_spec=grid_spec,
        out_shape=jax.ShapeDtypeStruct((T, D), f32),
        compiler_params=_cparams(("arbitrary",)),
        name="moe_combine",
    )(seg_src, seg_dst, seg_n, seg_tot, lp_tk, gate_tk, ys, ysh, x1, mod3, g_post.reshape(1, D))


def _tile(n, pref):
    t = min(n, pref)
    assert n % t == 0, (n, t)
    return t


def _layer(i, x2, mod3, p, wexp, v_first, B, S, cfg):
    T, D = x2.shape
    H = (D // 2) // ATT_V_DIM
    W = D - D // 2
    att_cols = 2 * H * 2 * ATT_QK_DIM + H * ATT_V_DIM
    lam_init = 0.8 - 0.6 * math.exp(-0.3 * i)

    h1 = _prenorm(x2, mod3, p["g_pre_mix"], S, _tile(S, cfg["tm_norm"]), 1, 0)
    att = _inproj(h1, p["w_in"][:, :att_cols].astype(bf16), bf16, _tile(T, cfg["tm_in"]), cfg["tn_att"])
    w_rwkv = p["w_in"][:, att_cols:].astype(bf16)

    slopes = jnp.broadcast_to(
        (2.0 ** (-ALIBI_MAX_BIAS * jnp.arange(1, H + 1, dtype=f32) / H))[:, None, None], (H, 1, LANES))
    lamp = jnp.stack([p["lam_q1"], p["lam_k1"], p["lam_q2"], p["lam_k2"]])
    o_att = _attention(att, slopes, lamp, p["att_subln_g"], B, S, H, lam_init, _tile(S, cfg["tq"]), cfg["hp"])

    cols = w_rwkv.shape[1]
    zw = jnp.zeros((RWKV_A_RANK, W), f32)
    heads = W // RWKV_HEAD
    ind = (jnp.arange(W)[:, None] // RWKV_HEAD == jnp.arange(LANES)[None, :]).astype(bf16)
    prm = {
        "mu": p["rwkv_mu"].reshape(1, cols), "w0": p["rwkv_w0"].reshape(1, W),
        "w2p": jnp.concatenate([p["rwkv_w2"], zw], axis=0),
        "a0": p["rwkv_a0"].reshape(1, W),
        "a2p": jnp.concatenate([jnp.zeros((RWKV_W_RANK, W), f32), p["rwkv_a2"]], axis=0),
        "g2": p["rwkv_g2"], "k_k": p["rwkv_k_k"].reshape(1, W), "k_a": p["rwkv_k_a"].reshape(1, W),
        "ind": ind, "indt": ind.T,
    }
    if v_first is not None:
        padc = LANES - RWKV_V_RANK
        prm["v0"] = p["rwkv_v0"].reshape(1, W)
        prm["v1p"] = jnp.pad(p["rwkv_v1"], ((0, 0), (0, padc)))
        prm["v2p"] = jnp.pad(p["rwkv_v2"], ((0, padc), (0, 0)))
    r, lw, kh, v, kn, kb, g = _rwkv_prep(h1, w_rwkv, prm, v_first, B, S, W, _tile(S, cfg["tm_prep"]))
    if v_first is None:
        v_first = v
    r2, y0, mmat, g0 = _wkv_intra(r, lw, kh, v, kn, kb, min(cfg["nc"], S // CHUNK), cfg["passes_intra"])
    o_rwkv = _wkv_state(r2, y0, mmat, g0, r, kh, v, g, p["rwkv_lnx_w"].reshape(1, W),
                        p["rwkv_lnx_b"].reshape(1, W), p["rwkv_r_k"].reshape(1, W), B, S,
                        min(cfg["pg"], W // LANES), min(cfg["cb"], S // CHUNK), cfg["passes_state"])
    del heads

    x1, h2, logits_t = _postmix(o_att, o_rwkv, p["w_out"][:D // 2].astype(bf16), p["w_out"][D // 2:].astype(bf16),
                                x2, mod3,
                                p["g_post_mix"], p["g_pre_ffn"], p["w_router"].T, S, _tile(S, cfg["tm_post"]))

    tm_t = _tile(T, cfg["tm_tile"])
    ntiles = T // tm_t
    eidx_t, gate_t, lrank_t, cnt_tbl = _router(logits_t, p["router_bias"], tm_t)
    blk = cfg["blk"]
    run = (cnt_tbl[:, :ntiles].T + SUBLANES - 1) // SUBLANES * SUBLANES
    counts = jnp.sum(run, axis=0)
    padded = (counts + blk - 1) // blk * blk
    pad_end = jnp.cumsum(padded)
    pad_start = pad_end - padded
    e_ids = jnp.arange(N_EXPERTS, dtype=i32)
    tile_off = jnp.cumsum(run, axis=0) - run
    loc_off = jnp.cumsum(run, axis=1) - run
    run_dst = pad_start[None, :] + tile_off
    srcs, dsts, cnts = [], [], []
    for psz in _piece_sizes(tm_t):
        has = (run & psz) != 0
        before = run & ~(2 * psz - 1)
        slot_of = jnp.cumsum(has, axis=1) - 1
        hit = has[:, None, :] & (slot_of[:, None, :] == e_ids[None, :, None])
        srcs.append(jnp.sum(jnp.where(hit, (loc_off + before)[:, None, :], 0), axis=-1))
        dsts.append(jnp.sum(jnp.where(hit, (run_dst + before)[:, None, :], 0), axis=-1))
        cnts.append(jnp.sum(has, axis=1))
    seg_src = jnp.stack(srcs).reshape(-1).astype(i32)
    seg_dst = jnp.stack(dsts).reshape(-1).astype(i32)
    seg_n = jnp.stack(cnts).reshape(-1).astype(i32)
    seg_tot = jnp.sum(run, axis=1).astype(i32)
    loc_tok = jnp.repeat(loc_off, tm_t, axis=0)
    lp_t = jnp.sum(jnp.where(eidx_t[:, :, None] == e_ids, loc_tok[None], 0), axis=-1).astype(f32) + lrank_t
    lmax = -(-(tm_t * TOP_K + N_EXPERTS * (SUBLANES - 1)) // SORT_CHUNK) * SORT_CHUNK
    nblk = -(-(T * TOP_K + ntiles * N_EXPERTS * (SUBLANES - 1)) // blk) + N_EXPERTS
    P = nblk * blk
    blk_start = jnp.arange(nblk, dtype=i32) * blk
    nused = (pad_end[-1] // blk).astype(i32).reshape(1)
    blk_pos = jnp.minimum(blk_start, pad_end[-1] - blk)
    blk_e = jnp.minimum(jnp.sum((pad_end[None, :] <= blk_pos[:, None]).astype(i32), axis=1), N_EXPERTS - 1)
    cand = jnp.where(counts > 0, e_ids, N_EXPERTS)
    later = jnp.where(e_ids[None, :] > blk_e[:, None], cand[None, :], N_EXPERTS)
    nxt_e = jnp.min(later, axis=1)
    nxt_e = jnp.where(nxt_e >= N_EXPERTS, -1, nxt_e).astype(i32)

    xs, ysh = _dispatch(seg_src, seg_dst, seg_n, seg_tot, counts.astype(i32), pad_start.astype(i32), nused, lp_t, h2,
                        p["w_sh_gate"].astype(bf16), p["w_sh_up"].astype(bf16), p["w_sh_down"].astype(bf16),
                        P, blk, tm_t, lmax)
    ys = _experts(blk_e, nxt_e, nused, xs, wexp[0], wexp[1], wexp[2], i, blk)
    x_out = _combine(seg_src, seg_dst, seg_n, seg_tot, lp_t.T, gate_t.T, ys, ysh, x1,
                     mod3, p["g_post_ffn"], S, tm_t, lmax)
    return x_out, v_first


_CFG = dict(tm_norm=512, tm_in=2048, tn_att=1024, tm_in_rwkv=1024, tn_rwkv=1664, tq=512, hp=4, tm_prep=256, nc=8, passes_intra=1, passes_state=1, pg=4, cb=8,
            tm_post=512, blk=256, tm_tile=256)

_LAYER_KEYS = ("g_pre_mix", "g_post_mix", "g_pre_ffn", "g_post_ffn", "w_in", "w_out", "lam_q1", "lam_k1",
               "lam_q2", "lam_k2", "att_subln_g", "rwkv_mu", "rwkv_w0", "rwkv_w2", "rwkv_a0", "rwkv_a2",
               "rwkv_g2", "rwkv_k_k", "rwkv_k_a", "rwkv_r_k", "rwkv_lnx_w", "rwkv_lnx_b", "w_router",
               "router_bias", "w_sh_gate", "w_sh_up", "w_sh_down")


def _forward(x, c, params, cfg):
    B, S, D = x.shape
    L = params["w_in"].shape[0]
    bp = 16
    c_pad = jnp.zeros((bp, D), f32).at[:B].set(c)
    mod = _ada_mod(c_pad, params["w_ada"], params["b_ada"])
    x2 = x.reshape(B * S, D)
    v_first = None
    for i in range(L):
        p = {k: params[k][i] for k in _LAYER_KEYS}
        if i > 0:
            p["rwkv_v0"] = params["rwkv_v0"][i - 1]
            p["rwkv_v1"] = params["rwkv_v1"][i - 1]
            p["rwkv_v2"] = params["rwkv_v2"][i - 1]
        mod3 = mod[i, :B].reshape(B * N_MOD, 1, D)
        wexp = (params["w_exp_gate"], params["w_exp_up"], params["w_exp_down"])
        x2, v_first = _layer(i, x2, mod3, p, wexp, v_first, B, S, cfg)
    return x2.reshape(B, S, D)


def kernel(x, c, w_ada, b_ada, g_pre_mix, g_post_mix, g_pre_ffn, g_post_ffn, w_in, w_out, lam_q1, lam_k1, lam_q2, lam_k2, att_subln_g, rwkv_mu, rwkv_w0, rwkv_w2, rwkv_a0, rwkv_a2, rwkv_g2, rwkv_k_k, rwkv_k_a, rwkv_r_k, rwkv_lnx_w, rwkv_lnx_b, rwkv_v0, rwkv_v1, rwkv_v2, w_router, router_bias, w_exp_gate, w_exp_up, w_exp_down, w_sh_gate, w_sh_up, w_sh_down):
    params = dict(w_ada=w_ada, b_ada=b_ada, g_pre_mix=g_pre_mix, g_post_mix=g_post_mix, g_pre_ffn=g_pre_ffn,
                  g_post_ffn=g_post_ffn, w_in=w_in, w_out=w_out, lam_q1=lam_q1, lam_k1=lam_k1, lam_q2=lam_q2,
                  lam_k2=lam_k2, att_subln_g=att_subln_g, rwkv_mu=rwkv_mu, rwkv_w0=rwkv_w0, rwkv_w2=rwkv_w2,
                  rwkv_a0=rwkv_a0, rwkv_a2=rwkv_a2, rwkv_g2=rwkv_g2, rwkv_k_k=rwkv_k_k, rwkv_k_a=rwkv_k_a,
                  rwkv_r_k=rwkv_r_k, rwkv_lnx_w=rwkv_lnx_w, rwkv_lnx_b=rwkv_lnx_b, rwkv_v0=rwkv_v0,
                  rwkv_v1=rwkv_v1, rwkv_v2=rwkv_v2, w_router=w_router, router_bias=router_bias,
                  w_exp_gate=w_exp_gate, w_exp_up=w_exp_up, w_exp_down=w_exp_down, w_sh_gate=w_sh_gate,
                  w_sh_up=w_sh_up, w_sh_down=w_sh_down)
    return _forward(x, c, params, _CFG)
```

```python
import functools
import math

import jax
import jax.numpy as jnp
from jax import lax
from jax.experimental import pallas as pl
from jax.experimental.pallas import tpu as pltpu

f32 = jnp.float32
bf16 = jnp.bfloat16
i32 = jnp.int32
u32 = jnp.uint32

ATT_QK_DIM = 64
ATT_V_DIM = 128
ALIBI_MAX_BIAS = 8.0
ATT_SUBLN_EPS = 1e-5
RWKV_HEAD = 64
RWKV_W_RANK = 64
RWKV_A_RANK = 64
RWKV_G_RANK = 128
RWKV_V_RANK = 32
RWKV_GN_EPS = 64e-5
N_EXPERTS = 64
N_GROUPS = 8
TOPK_GROUPS = 4
TOP_K = 8
ROUTED_SCALE = 2.5
NORM_EPS = 1e-6
N_MOD = 6

LANES = 128
SUBLANES = 8
CHUNK = 64
VMEM_LIMIT = 56 * 1024 * 1024


def _cparams(sem):
    return pltpu.CompilerParams(dimension_semantics=sem, vmem_limit_bytes=VMEM_LIMIT)


def _dot(a, b):
    return jnp.dot(a, b, preferred_element_type=f32)


def _dot_nt(a, b):
    return lax.dot_general(a, b, (((1,), (1,)), ((), ())), preferred_element_type=f32)


def _split2(x):
    hi = x.astype(bf16)
    lo = (x - hi.astype(f32)).astype(bf16)
    return hi, lo


def _mm(a, b, passes=1):
    if passes == 1:
        return _dot(a.astype(bf16), b.astype(bf16))
    ah, al = _split2(a.astype(f32))
    bh, bl = _split2(b.astype(f32))
    return (_dot(al, bh) + _dot(ah, bl)) + _dot(ah, bh)


def _mm_exact_rhs(a, b_bf16):
    ah, al = _split2(a)
    return _dot(al, b_bf16) + _dot(ah, b_bf16)


def _sigmoid(x):
    return 1.0 / (1.0 + jnp.exp(-x))


def _pack_rows(x):
    half = x.shape[1] // 2
    a = x[:, :half].astype(bf16).astype(f32)
    b = x[:, half:].astype(bf16).astype(f32)
    lo = lax.shift_right_logical(lax.bitcast_convert_type(a, u32), jnp.uint32(16))
    hi = lax.bitcast_convert_type(b, u32) & jnp.uint32(0xFFFF0000)
    return lo | hi


def _unpack_rows(w):
    lo = lax.bitcast_convert_type(lax.shift_left(w, jnp.uint32(16)), f32)
    hi = lax.bitcast_convert_type(w & jnp.uint32(0xFFFF0000), f32)
    return lo, hi


def _ada_kernel(c_ref, w_ref, b_ref, o_ref):
    c = c_ref[...]
    cond = (c * _sigmoid(c)).astype(bf16)
    o_ref[0] = _dot(cond, w_ref[0].astype(bf16)) + b_ref[0]


def _ada_mod(c_pad, w_ada, b_ada, tn=1024):
    L, D, N = w_ada.shape
    bp = c_pad.shape[0]
    return pl.pallas_call(
        _ada_kernel,
        grid=(L, N // tn),
        in_specs=[pl.BlockSpec((bp, D), lambda l, j: (0, 0)),
                  pl.BlockSpec((1, D, tn), lambda l, j: (l, 0, j)),
                  pl.BlockSpec((1, 1, tn), lambda l, j: (l, 0, j))],
        out_specs=pl.BlockSpec((1, bp, tn), lambda l, j: (l, 0, j)),
        out_shape=jax.ShapeDtypeStruct((L, bp, N), f32),
        compiler_params=_cparams(("arbitrary", "arbitrary")),
        name="ada_mod",
    )(c_pad, w_ada, b_ada.reshape(L, 1, N))


def _prenorm_kernel(x_ref, sc_ref, sh_ref, g_ref, h_ref):
    x = x_ref[...]
    ms = jnp.mean(x * x, axis=-1, keepdims=True)
    y = x * lax.rsqrt(ms + NORM_EPS) * g_ref[...]
    h_ref[...] = (y * (1.0 + sc_ref[0]) + sh_ref[0]).astype(h_ref.dtype)


def _prenorm(x2, mod3, g, S, tm, seg_sc, seg_sh):
    T, D = x2.shape
    tpb = S // tm
    tile = pl.BlockSpec((tm, D), lambda i: (i, 0))
    return pl.pallas_call(
        _prenorm_kernel,
        grid=(T // tm,),
        in_specs=[tile,
                  pl.BlockSpec((1, 1, D), lambda i: ((i // tpb) * N_MOD + seg_sc, 0, 0)),
                  pl.BlockSpec((1, 1, D), lambda i: ((i // tpb) * N_MOD + seg_sh, 0, 0)),
                  pl.BlockSpec((1, D), lambda i: (0, 0))],
        out_specs=tile,
        out_shape=jax.ShapeDtypeStruct((T, D), bf16),
        compiler_params=_cparams(("arbitrary",)),
        name="prenorm",
    )(x2, mod3, mod3, g.reshape(1, D))


def _inproj_kernel(h_ref, w_ref, o_ref):
    o_ref[...] = _dot(h_ref[...], w_ref[...]).astype(o_ref.dtype)


def _inproj(h, w_bf, out_dtype, tm, tn):
    T, D = h.shape
    N = w_bf.shape[1]
    return pl.pallas_call(
        _inproj_kernel,
        grid=(T // tm, N // tn),
        in_specs=[pl.BlockSpec((tm, D), lambda i, j: (i, 0)),
                  pl.BlockSpec((D, tn), lambda i, j: (0, j))],
        out_specs=pl.BlockSpec((tm, tn), lambda i, j: (i, j)),
        out_shape=jax.ShapeDtypeStruct((T, N), out_dtype),
        compiler_params=_cparams(("arbitrary", "arbitrary")),
        name="inproj",
    )(h, w_bf)


def _attn_kernel(q_ref, k_ref, v_ref, slope_ref, lamp_ref, g_ref, o_ref,
                 q2t_ref, vt_ref, m_ref, l_ref, acc_ref, *, tq, hp, lam_init):
    qi = pl.program_id(2)
    scale = ATT_QK_DIM ** -0.5
    heads = range(hp)
    hl = [slice(h * LANES, (h + 1) * LANES) for h in heads]
    slope = [slope_ref[h][:, 0:1] for h in heads]

    @pl.when(qi == 0)
    def _():
        for h in heads:
            vt_ref[h] = v_ref[:, hl[h]].astype(f32).T.astype(bf16)

    dim = lax.broadcasted_iota(i32, (LANES, 1), 0)
    first = dim < ATT_QK_DIM
    for h in heads:
        qt = (q_ref[:, hl[h]].astype(f32) * scale).T
        q2t_ref[h, :, 0:tq] = jnp.where(first, qt, 0.0).astype(bf16)
        q2t_ref[h, :, tq:2 * tq] = jnp.where(first, 0.0, qt).astype(bf16)
    m_ref[...] = jnp.full(m_ref.shape, -jnp.inf, f32)
    l_ref[...] = jnp.zeros(l_ref.shape, f32)
    acc_ref[...] = jnp.zeros(acc_ref.shape, f32)

    kr = lax.broadcasted_iota(i32, (tq, 2 * tq), 0)
    qc = lax.broadcasted_iota(i32, (tq, 2 * tq), 1)
    causal = jnp.where(qc >= tq, qc - tq, qc) >= kr
    krow = lax.broadcasted_iota(i32, (tq, 1), 0).astype(f32)

    def step(ki, masked):
        start = pl.multiple_of(ki * tq, tq)
        kpos = krow + (ki * tq).astype(f32)
        s = [_dot(k_ref[pl.ds(start, tq), hl[h]], q2t_ref[h]) for h in heads]
        s = [s[h] + slope[h] * kpos for h in heads]
        if masked:
            s = [jnp.where(causal, s[h], -jnp.inf) for h in heads]
        m_prev = [m_ref[h] for h in heads]
        m_new = [jnp.maximum(m_prev[h], jnp.max(s[h], axis=0, keepdims=True)) for h in heads]
        alpha = [jnp.exp(m_prev[h] - m_new[h]) for h in heads]
        p = [jnp.exp(s[h] - m_new[h]) for h in heads]
        for h in heads:
            l_ref[h] = alpha[h] * l_ref[h] + jnp.sum(p[h], axis=0, keepdims=True)
            acc_ref[h] = alpha[h] * acc_ref[h] + _dot(vt_ref[h, :, pl.ds(start, tq)], p[h].astype(bf16))
            m_ref[h] = m_new[h]

    def body(ki, carry):
        step(ki, False)
        return carry

    lax.fori_loop(0, qi, body, 0)
    step(qi, True)

    lp = lamp_ref[...]
    lam = (jnp.exp(jnp.sum(lp[0:1] * lp[1:2], axis=-1, keepdims=True))
           - jnp.exp(jnp.sum(lp[2:3] * lp[3:4], axis=-1, keepdims=True)) + lam_init)
    for h in heads:
        on = acc_ref[h] * (1.0 / l_ref[h])
        o = on[:, 0:tq] - lam * on[:, tq:2 * tq]
        o = o * lax.rsqrt(jnp.mean(o * o, axis=0, keepdims=True) + ATT_SUBLN_EPS)
        o = o * g_ref[...] * (1.0 - lam_init)
        o_ref[:, hl[h]] = o.T.astype(o_ref.dtype)


def _attention(att, slopes, lamp, subln_g, B, S, H, lam_init, tq, hp):
    T = att.shape[0]
    nq = S // tq
    kern = functools.partial(_attn_kernel, tq=tq, hp=hp, lam_init=lam_init)
    hw = hp * LANES
    return pl.pallas_call(
        kern,
        grid=(B, H // hp, nq),
        in_specs=[pl.BlockSpec((tq, hw), lambda b, g, q: (b * nq + q, g)),
                  pl.BlockSpec((S, hw), lambda b, g, q: (b, H // hp + g)),
                  pl.BlockSpec((S, hw), lambda b, g, q: (b, 2 * (H // hp) + g)),
                  pl.BlockSpec((hp, 1, LANES), lambda b, g, q: (g, 0, 0)),
                  pl.BlockSpec((4, ATT_QK_DIM), lambda b, g, q: (0, 0)),
                  pl.BlockSpec((ATT_V_DIM, 1), lambda b, g, q: (0, 0))],
        out_specs=pl.BlockSpec((tq, hw), lambda b, g, q: (b * nq + q, g)),
        out_shape=jax.ShapeDtypeStruct((T, H * ATT_V_DIM), bf16),
        scratch_shapes=[pltpu.VMEM((hp, LANES, 2 * tq), bf16),
                        pltpu.VMEM((hp, LANES, S), bf16),
                        pltpu.VMEM((hp, 1, 2 * tq), f32),
                        pltpu.VMEM((hp, 1, 2 * tq), f32),
                        pltpu.VMEM((hp, LANES, 2 * tq), f32)],
        compiler_params=_cparams(("arbitrary", "arbitrary", "arbitrary")),
        name="diff_attention",
    )(att, att, att, slopes, lamp, subln_g.reshape(ATT_V_DIM, 1))


def _head_sums(x, ind, indt):
    s = _mm_exact_rhs(x, ind)
    return _mm_exact_rhs(s, indt)


def _rwkv_prep_kernel(*refs, W, has_vres, tpb):
    if has_vres:
        (hx_ref, win_ref, mu_ref, w0_ref, w2_ref, a0_ref, a2_ref, g2_ref, kk_ref, ka_ref, ind_ref, indt_ref,
         vf_ref, v0_ref, v1_ref, v2_ref,
         r_o, lw_o, kh_o, v_o, kn_o, kb_o, g_o, carry_ref, feats_ref) = refs
    else:
        (hx_ref, win_ref, mu_ref, w0_ref, w2_ref, a0_ref, a2_ref, g2_ref, kk_ref, ka_ref, ind_ref, indt_ref,
         r_o, lw_o, kh_o, v_o, kn_o, kb_o, g_o, carry_ref, feats_ref) = refs

    i = pl.program_id(0)

    @pl.when(i == 0)
    def _():
        feats_ref[...] = jnp.zeros(feats_ref.shape, f32)
        carry_ref[...] = jnp.zeros(carry_ref.shape, f32)

    h = feats_ref[...]
    nxt = _dot(hx_ref[...], win_ref[...])
    tm = h.shape[0]

    first_in_batch = lax.rem(jnp.maximum(i - 1, 0), tpb) == 0
    carry = jnp.where(first_in_batch, 0.0, carry_ref[...])
    rolled = pltpu.roll(h, 1, axis=0)
    row = lax.broadcasted_iota(i32, (tm, 1), 0)
    prev = jnp.where(row == 0, carry, rolled)
    carry_ref[...] = h[tm - 1:tm, :]
    feats_ref[...] = nxt
    feats = h + (prev - h) * mu_ref[...]

    r = feats[:, 0:W]
    k = feats[:, W:2 * W]
    v = feats[:, 2 * W:3 * W]
    wa = feats[:, 3 * W:3 * W + LANES]
    g_lo = feats[:, 3 * W + LANES:3 * W + 2 * LANES]

    w = w0_ref[...] + _mm(jnp.tanh(wa), w2_ref[...], passes=3)
    lw_o[...] = -math.exp(-0.5) * _sigmoid(w)
    a = _sigmoid(a0_ref[...] + _mm(wa, a2_ref[...], passes=3))
    g_o[...] = _mm(_sigmoid(g_lo), g2_ref[...]).astype(g_o.dtype)

    if has_vres:
        mix = _sigmoid(v0_ref[...] + _mm(_mm(v, v1_ref[...]), v2_ref[...]))
        v = v + (vf_ref[...].astype(f32) - v) * mix

    kk = k * kk_ref[...]
    ss = _head_sums(kk * kk, ind_ref[...], indt_ref[...])
    kk = kk / jnp.maximum(jnp.sqrt(ss), 1e-12)
    r_o[...] = r.astype(r_o.dtype)
    kh_o[...] = (k * (1.0 + (a - 1.0) * ka_ref[...])).astype(kh_o.dtype)
    v_o[...] = v.astype(v_o.dtype)
    kn_o[...] = kk.astype(kn_o.dtype)
    kb_o[...] = (kk * a).astype(kb_o.dtype)


def _rwkv_prep(hx, w_rwkv, prm, vfirst, B, S, W, tm):
    T, D = hx.shape
    COLS = w_rwkv.shape[1]
    tpb = S // tm
    n = T // tm
    has_vres = vfirst is not None
    row = lambda c: pl.BlockSpec((1, c), lambda i: (0, 0))
    full = lambda a: pl.BlockSpec(a.shape, lambda i: (0, 0))
    tile = pl.BlockSpec((tm, W), lambda i: (jnp.maximum(i - 1, 0), 0))
    args = [hx, w_rwkv, prm["mu"], prm["w0"], prm["w2p"], prm["a0"], prm["a2p"], prm["g2"], prm["k_k"], prm["k_a"],
            prm["ind"], prm["indt"]]
    specs = [pl.BlockSpec((tm, D), lambda i: (jnp.minimum(i, n - 1), 0)),
             pl.BlockSpec((D, COLS), lambda i: (0, 0), pipeline_mode=pl.Buffered(1)),
             row(COLS), row(W), full(prm["w2p"]),
             row(W), full(prm["a2p"]), full(prm["g2"]), row(W), row(W), full(prm["ind"]), full(prm["indt"])]
    if has_vres:
        args += [vfirst, prm["v0"], prm["v1p"], prm["v2p"]]
        specs += [tile, row(W), full(prm["v1p"]), full(prm["v2p"])]
    out_dtypes = (bf16, f32, bf16, bf16, bf16, bf16, bf16)
    kern = functools.partial(_rwkv_prep_kernel, W=W, has_vres=has_vres, tpb=tpb)
    return pl.pallas_call(
        kern,
        grid=(n + 1,),
        in_specs=specs,
        out_specs=[tile] * 7,
        out_shape=[jax.ShapeDtypeStruct((T, W), dt) for dt in out_dtypes],
        scratch_shapes=[pltpu.VMEM((1, COLS), f32), pltpu.VMEM((tm, COLS), f32)],
        compiler_params=_cparams(("arbitrary",)),
        name="rwkv_prep",
    )(*args)


def _wkv_chunks(rs, lws, ks, vs, kns, kbs, passes):
    C = CHUNK
    P2 = 2 * C
    n = range(len(rs))
    ri = lax.broadcasted_iota(i32, (C, C), 0)
    ci = lax.broadcasted_iota(i32, (C, C), 1)
    tri = (ci <= ri).astype(bf16)
    lane = lax.broadcasted_iota(i32, (1, LANES), 1)
    m0 = (lane < RWKV_HEAD).astype(f32)
    m1 = 1.0 - m0
    rr = lax.broadcasted_iota(i32, (P2, P2), 0)
    cc = lax.broadcasted_iota(i32, (P2, P2), 1)
    same = jnp.where(rr >= C, 1, 0) == jnp.where(cc >= C, 1, 0)
    strict = same & (cc < rr)
    incl = same & (cc <= rr)
    incl2 = jnp.concatenate([incl, incl], axis=1)
    eye = (rr == cc).astype(f32)
    zeros_p = jnp.zeros((P2, LANES), f32)
    zeros_c = jnp.zeros((C, LANES), f32)
    stack = lambda x: jnp.concatenate([x * m0, x * m1], axis=0)
    fold = lambda x: x[0:C] + x[C:2 * C]

    def cumsum(lw):
        h1 = lw.astype(bf16)
        r1 = lw - h1.astype(f32)
        h2 = r1.astype(bf16)
        h3 = (r1 - h2.astype(f32)).astype(bf16)
        return (_dot(tri, h3) + _dot(tri, h2)) + _dot(tri, h1)

    cum = [cumsum(lws[j]) for j in n]
    cum_c = [cum[j][C - 1:C, :] for j in n]
    at = [-kns[j] * jnp.exp(cum[j] - lws[j]) for j in n]
    rt = [rs[j] * jnp.exp(cum[j]) for j in n]
    einv = [jnp.exp(-cum[j]) for j in n]
    bt = [kbs[j] * einv[j] for j in n]
    kt = [ks[j] * einv[j] for j in n]
    eh = [jnp.exp(cum_c[j] - cum[j]) for j in n]
    bh = [kbs[j] * eh[j] for j in n]
    kh = [ks[j] * eh[j] for j in n]
    w_c = [jnp.exp(cum_c[j]) for j in n]
    abd = [stack(at[j]) for j in n]
    vst = [stack(vs[j]) for j in n]
    lhs = [jnp.concatenate([abd[j], stack(rt[j])], axis=0) for j in n]
    rhs = [jnp.concatenate([stack(bt[j]), stack(kt[j])], axis=0) for j in n]
    gram = [_mm_nt(lhs[j], rhs[j], passes) for j in n]
    lab = [jnp.where(strict, gram[j][0:P2, 0:P2], 0.0) for j in n]
    lak = [jnp.where(strict, gram[j][0:P2, P2:2 * P2], 0.0) for j in n]
    mrbk = [jnp.where(incl2, gram[j][P2:2 * P2, :], 0.0) for j in n]

    x0 = [_mm(lak[j], vst[j], passes) for j in n]
    tinv = [eye + lab[j] for j in n]
    lp = [_mm(lab[j], lab[j], passes) for j in n]
    n_sq = int(math.log2(C)) - 1
    for it in range(n_sq):
        if it < n_sq - 1:
            both = [_mm(lp[j], jnp.concatenate([lp[j], tinv[j]], axis=1), passes) for j in n]
            tinv = [tinv[j] + both[j][:, P2:2 * P2] for j in n]
            lp = [both[j][:, 0:P2] for j in n]
        else:
            tinv = [tinv[j] + _mm(lp[j], tinv[j], passes) for j in n]

    ta = [_mm(tinv[j], jnp.concatenate([abd[j], x0[j]], axis=1), passes) for j in n]
    rhs2 = [jnp.concatenate([ta[j], jnp.concatenate([zeros_p, vst[j]], axis=1)], axis=0) for j in n]
    z = [_mm(mrbk[j], rhs2[j], passes) for j in n]
    r2 = [rt[j] + fold(z[j][:, 0:LANES]) for j in n]
    y0 = [fold(z[j][:, LANES:2 * LANES]) for j in n]
    lhs3t = [jnp.concatenate([bh[j], kh[j]], axis=0).T for j in n]
    rhs3 = [jnp.concatenate([fold(ta[j]), jnp.concatenate([zeros_c, vs[j]], axis=1)], axis=0) for j in n]
    wmat = [_mm(lhs3t[j], rhs3[j], passes) for j in n]
    mmat = [jnp.where(same, wmat[j][:, 0:LANES], 0.0) + eye * w_c[j] for j in n]
    g0 = [jnp.where(same, wmat[j][:, LANES:2 * LANES], 0.0) for j in n]
    return r2, y0, mmat, g0


def _mm_nt(a, b, passes):
    if passes == 1:
        return _dot_nt(a.astype(bf16), b.astype(bf16))
    ah, al = _split2(a)
    bh, bl = _split2(b)
    return (_dot_nt(al, bh) + _dot_nt(ah, bl)) + _dot_nt(ah, bh)


def _wkv_intra_kernel(r_ref, lw_ref, k_ref, v_ref, kn_ref, kb_ref, r2_o, y0_o, m_o, g_o, *, nc, passes):
    C = CHUNK
    sls = [slice(c * C, (c + 1) * C) for c in range(nc)]
    take = lambda ref: [ref[sl, :].astype(f32) for sl in sls]
    r2, y0, mmat, g0 = _wkv_chunks(take(r_ref), take(lw_ref), take(k_ref), take(v_ref),
                                   take(kn_ref), take(kb_ref), passes)
    for c, sl in enumerate(sls):
        r2_o[sl, :] = r2[c].astype(r2_o.dtype)
        y0_o[sl, :] = y0[c].astype(y0_o.dtype)
        m_o[0, c] = mmat[c].astype(m_o.dtype)
        g_o[0, c] = g0[c]


def _wkv_intra(r, lw, kh, v, kn, kb, nc, passes):
    T, W = r.shape
    npair = W // LANES
    rows = nc * CHUNK
    tile = pl.BlockSpec((rows, LANES), lambda p, i: (i, p))
    mat = pl.BlockSpec((1, nc, LANES, LANES), lambda p, i: (p, i, 0, 0))
    kern = functools.partial(_wkv_intra_kernel, nc=nc, passes=passes)
    return pl.pallas_call(
        kern,
        grid=(npair, T // rows),
        in_specs=[tile] * 6,
        out_specs=[tile, tile, mat, mat],
        out_shape=[jax.ShapeDtypeStruct((T, W), bf16), jax.ShapeDtypeStruct((T, W), bf16),
                   jax.ShapeDtypeStruct((npair, T // CHUNK, LANES, LANES), bf16),
                   jax.ShapeDtypeStruct((npair, T // CHUNK, LANES, LANES), f32)],
        compiler_params=_cparams(("arbitrary", "arbitrary")),
        name="wkv_intra",
    )(r, lw, kh, v, kn, kb)


def _wkv_state_kernel(r2_ref, y0_ref, m_ref, g0_ref, r_ref, kh_ref, v_ref, g_ref,
                      lnw_ref, lnb_ref, rk_ref, o_ref, st_ref, y_ref, *, pg, cb, passes):
    C = CHUNK

    @pl.when(pl.program_id(2) == 0)
    def _():
        st_ref[...] = jnp.zeros(st_ref.shape, f32)

    pairs = range(pg)
    lanes = [slice(p * LANES, (p + 1) * LANES) for p in pairs]
    st = [st_ref[p] for p in pairs]
    for c in range(cb):
        rows = slice(c * C, (c + 1) * C)
        for p in pairs:
            y_ref[rows, lanes[p]] = (_mm(r2_ref[rows, lanes[p]], st[p], passes)
                                     + y0_ref[rows, lanes[p]].astype(f32))
        st = [_mm(m_ref[p, c], st[p], passes) + g0_ref[p, c] for p in pairs]
    for p in pairs:
        st_ref[p] = st[p]

    rr = lax.broadcasted_iota(i32, (LANES, LANES), 0)
    cc = lax.broadcasted_iota(i32, (LANES, LANES), 1)
    ones_bd = (jnp.where(rr >= RWKV_HEAD, 1, 0) == jnp.where(cc >= RWKV_HEAD, 1, 0)).astype(bf16)
    for p in pairs:
        y = y_ref[:, lanes[p]]
        mu = _mm_exact_rhs(y, ones_bd) * (1.0 / RWKV_HEAD)
        d = y - mu
        var = _mm_exact_rhs(d * d, ones_bd) * (1.0 / RWKV_HEAD)
        yn = d * lax.rsqrt(var + RWKV_GN_EPS) * lnw_ref[:, lanes[p]] + lnb_ref[:, lanes[p]]
        rk = r_ref[:, lanes[p]].astype(f32) * kh_ref[:, lanes[p]].astype(f32) * rk_ref[:, lanes[p]]
        bonus = _mm_exact_rhs(rk, ones_bd) * v_ref[:, lanes[p]]
        o_ref[:, lanes[p]] = ((yn + bonus) * g_ref[:, lanes[p]]).astype(o_ref.dtype)


def _wkv_state(r2, y0, mm, g0, r, kh, v, g, lnw, lnb, rk, B, S, pg, cb, passes):
    T, W = r.shape
    npair = W // LANES
    rows = cb * CHUNK
    steps = S // rows
    seq = pl.BlockSpec((rows, pg * LANES), lambda b, q, c: (b * steps + c, q))
    mat = pl.BlockSpec((pg, cb, LANES, LANES), lambda b, q, c: (q, b * steps + c, 0, 0))
    prow = pl.BlockSpec((1, pg * LANES), lambda b, q, c: (0, q))
    kern = functools.partial(_wkv_state_kernel, pg=pg, cb=cb, passes=passes)
    return pl.pallas_call(
        kern,
        grid=(B, npair // pg, steps),
        in_specs=[seq, seq, mat, mat, seq, seq, seq, seq, prow, prow, prow],
        out_specs=seq,
        out_shape=jax.ShapeDtypeStruct((T, W), bf16),
        scratch_shapes=[pltpu.VMEM((pg, LANES, LANES), f32), pltpu.VMEM((rows, pg * LANES), f32)],
        compiler_params=_cparams(("arbitrary", "arbitrary", "arbitrary")),
        name="wkv_state",
    )(r2, y0, mm, g0, r, kh, v, g, lnw, lnb, rk)


def _postmix_kernel(oa_ref, orw_ref, wa_ref, wr_ref, x_ref, ga_ref, gpost_ref, gpre_ref, sc_ref, sh_ref, wrt_ref,
                    x1_o, h2_o, lg_o):
    mixed = _dot(oa_ref[...], wa_ref[...]) + _dot(orw_ref[...], wr_ref[...])
    ms = jnp.mean(mixed * mixed, axis=-1, keepdims=True)
    x1 = x_ref[...] + ga_ref[0] * (mixed * lax.rsqrt(ms + NORM_EPS) * gpost_ref[...])
    x1_o[...] = x1
    ms1 = jnp.mean(x1 * x1, axis=-1, keepdims=True)
    h2 = (x1 * lax.rsqrt(ms1 + NORM_EPS) * gpre_ref[...]) * (1.0 + sc_ref[0]) + sh_ref[0]
    h2_o[...] = _pack_rows(h2)
    lg_o[...] = _mm_nt(wrt_ref[...], h2, 3)


def _postmix(o_att, o_rwkv, w_out_a, w_out_r, x2, mod3, g_post, g_pre, w_rt, S, tm):
    T, D = x2.shape
    WA = o_att.shape[1]
    WR = o_rwkv.shape[1]
    E = w_rt.shape[0]
    tpb = S // tm
    modspec = lambda seg: pl.BlockSpec((1, 1, D), lambda i: ((i // tpb) * N_MOD + seg, 0, 0))
    tile = pl.BlockSpec((tm, D), lambda i: (i, 0))
    return pl.pallas_call(
        _postmix_kernel,
        grid=(T // tm,),
        in_specs=[pl.BlockSpec((tm, WA), lambda i: (i, 0)),
                  pl.BlockSpec((tm, WR), lambda i: (i, 0)),
                  pl.BlockSpec((WA, D), lambda i: (0, 0), pipeline_mode=pl.Buffered(1)),
                  pl.BlockSpec((WR, D), lambda i: (0, 0), pipeline_mode=pl.Buffered(1)),
                  tile, modspec(2),
                  pl.BlockSpec((1, D), lambda i: (0, 0)),
                  pl.BlockSpec((1, D), lambda i: (0, 0)),
                  modspec(4), modspec(3),
                  pl.BlockSpec((E, D), lambda i: (0, 0))],
        out_specs=[tile, pl.BlockSpec((tm, D // 2), lambda i: (i, 0)),
                   pl.BlockSpec((E, tm), lambda i: (0, i))],
        out_shape=[jax.ShapeDtypeStruct((T, D), f32), jax.ShapeDtypeStruct((T, D // 2), u32),
                   jax.ShapeDtypeStruct((E, T), f32)],
        compiler_params=_cparams(("arbitrary",)),
        name="postmix",
    )(o_att, o_rwkv, w_out_a, w_out_r, x2, mod3, g_post.reshape(1, D), g_pre.reshape(1, D), mod3, mod3, w_rt)


def _first_max(x, iota, n):
    mx = jnp.max(x, axis=0, keepdims=True)
    idx = jnp.min(jnp.where(x == mx, iota, n), axis=0, keepdims=True)
    return mx, idx


def _router_kernel(lg_ref, bias_ref, eidx_o, gate_o, rank_o, cnt_o, cnt_ref):
    E = N_EXPERTS
    G = N_GROUPS
    per = E // G
    tm = lg_ref.shape[1]

    @pl.when(pl.program_id(0) == 0)
    def _():
        cnt_ref[...] = jnp.zeros(cnt_ref.shape, f32)

    scores = _sigmoid(lg_ref[...])
    biased = scores + bias_ref[...]
    neg = -jnp.inf

    iota_p = lax.broadcasted_iota(i32, (per, tm), 0).astype(f32)
    gs = []
    for g in range(G):
        xg = biased[g * per:(g + 1) * per, :]
        m1, i1 = _first_max(xg, iota_p, per)
        m2 = jnp.max(jnp.where(iota_p == i1, neg, xg), axis=0, keepdims=True)
        gs.append(m1 + m2)
    gsc = jnp.concatenate(gs, axis=0)
    iota_g = lax.broadcasted_iota(i32, (G, tm), 0).astype(f32)
    gsel = jnp.zeros((G, tm), f32)
    for _ in range(TOPK_GROUPS):
        _, gi = _first_max(gsc, iota_g, G)
        hit = iota_g == gi
        gsel = jnp.where(hit, 1.0, gsel)
        gsc = jnp.where(hit, neg, gsc)
    masked = jnp.concatenate(
        [jnp.where(gsel[g:g + 1, :] > 0.0, biased[g * per:(g + 1) * per, :], neg) for g in range(G)], axis=0)

    iota_e = lax.broadcasted_iota(i32, (E, tm), 0).astype(f32)
    sel = jnp.zeros((E, tm), f32)
    idxs, vals = [], []
    for _ in range(TOP_K):
        _, ei = _first_max(masked, iota_e, E)
        hit = iota_e == ei
        idxs.append(ei)
        vals.append(jnp.sum(jnp.where(hit, scores, 0.0), axis=0, keepdims=True))
        sel = jnp.where(hit, 1.0, sel)
        masked = jnp.where(hit, neg, masked)
    tot = vals[0]
    for vv in vals[1:]:
        tot = tot + vv
    eidx_o[...] = jnp.concatenate(idxs, axis=0).astype(i32)
    gate_o[...] = jnp.concatenate([vv / tot * ROUTED_SCALE for vv in vals], axis=0)

    rr = lax.broadcasted_iota(i32, (tm, tm), 0)
    cc = lax.broadcasted_iota(i32, (tm, tm), 1)
    before = (rr < cc).astype(bf16)
    pos = _dot(sel.astype(bf16), before)
    rank_o[...] = jnp.concatenate(
        [jnp.sum(jnp.where(iota_e == ei, pos, 0.0), axis=0, keepdims=True) for ei in idxs], axis=0)
    lane = lax.broadcasted_iota(i32, cnt_ref.shape, 1)
    cnt_ref[...] = jnp.where(lane == pl.program_id(0), jnp.sum(sel, axis=1, keepdims=True), cnt_ref[...])
    cnt_o[...] = cnt_ref[...].astype(i32)


def _router(logits_t, bias, tm):
    E, T = logits_t.shape
    assert T // tm <= LANES
    k_tile = pl.BlockSpec((TOP_K, tm), lambda i: (0, i))
    return pl.pallas_call(
        _router_kernel,
        grid=(T // tm,),
        in_specs=[pl.BlockSpec((E, tm), lambda i: (0, i)),
                  pl.BlockSpec((E, 1), lambda i: (0, 0))],
        out_specs=[k_tile, k_tile, k_tile, pl.BlockSpec((E, LANES), lambda i: (0, 0))],
        out_shape=[jax.ShapeDtypeStruct((TOP_K, T), i32), jax.ShapeDtypeStruct((TOP_K, T), f32),
                   jax.ShapeDtypeStruct((TOP_K, T), f32), jax.ShapeDtypeStruct((E, LANES), i32)],
        scratch_shapes=[pltpu.VMEM((E, LANES), f32)],
        compiler_params=_cparams(("arbitrary",)),
        name="router",
    )(logits_t, bias.reshape(E, 1))


def _row_copy(src_ref, s, dst_ref, d, sem):
    return pltpu.make_async_copy(src_ref.at[pl.ds(s, 1), :], dst_ref.at[pl.ds(d, 1), :], sem)


def _zero_fill(cnt_ref, pstart_ref, nused_ref, z_ref, xs_out, sem, blk, nblk, start):
    def act(cp):
        if start:
            cp.start()
        else:
            cp.wait()

    def per_expert(e, carry):
        c = cnt_ref[e]
        base = pstart_ref[e] + c
        npad = (blk - (c & (blk - 1))) & (blk - 1)
        head = (-base) & (SUBLANES - 1)

        def one_row(j, carry2):
            act(_row_copy(z_ref, 0, xs_out, base + j, sem))
            return carry2

        lax.fori_loop(0, head, one_row, 0)
        rem = npad - head
        aligned = base + head
        p = blk // 2
        while p >= SUBLANES:
            off = pl.multiple_of(aligned + (rem & ~(2 * p - 1)), SUBLANES)

            @pl.when((rem & p) != 0)
            def _(p=p, off=off):
                act(pltpu.make_async_copy(z_ref.at[pl.ds(0, p), :], xs_out.at[pl.ds(off, p), :], sem))

            p //= 2
        return carry

    lax.fori_loop(0, N_EXPERTS, per_expert, 0)

    def per_block(b, carry):
        act(pltpu.make_async_copy(z_ref, xs_out.at[pl.ds(pl.multiple_of(b * blk, blk), blk), :], sem))
        return carry

    lax.fori_loop(nused_ref[0], nblk, per_block, 0)


def _piece_sizes(tm):
    return [tm >> s for s in range(tm.bit_length()) if (tm >> s) >= SUBLANES]


def _seg_copies(psrc_ref, pdst_ref, pcnt_ref, tile, buf_ref, hbm_ref, sem, tm, to_hbm):
    ntiles = pl.num_programs(0)
    for s, p in enumerate(_piece_sizes(tm)):
        base = (s * ntiles + tile) * N_EXPERTS

        def one(j, carry, p=p, base=base):
            v = buf_ref.at[pl.ds(pl.multiple_of(psrc_ref[base + j], SUBLANES), p), :]
            h = hbm_ref.at[pl.ds(pl.multiple_of(pdst_ref[base + j], SUBLANES), p), :]
            (pltpu.make_async_copy(v, h, sem) if to_hbm else pltpu.make_async_copy(h, v, sem)).start()
            return carry

        lax.fori_loop(0, pcnt_ref[s * ntiles + tile], one, 0)


def _seg_wait(total_rows, buf_ref, hbm_ref, sem, to_hbm):
    p = 1 << (buf_ref.shape[0].bit_length() - 1)
    while p >= SUBLANES:
        @pl.when((total_rows & p) != 0)
        def _(p=p):
            v = buf_ref.at[pl.ds(0, p), :]
            h = hbm_ref.at[pl.ds(0, p), :]
            (pltpu.make_async_copy(v, h, sem) if to_hbm else pltpu.make_async_copy(h, v, sem)).wait()

        p //= 2


SORT_CHUNK = 512


def _dispatch_kernel(ssrc_ref, sdst_ref, sn_ref, stot_ref, cnt_ref, pstart_ref, nused_ref, lp_ref, h_ref,
                     wsg_ref, wsu_ref, wsd_ref, xs_out, ysh_o, sb_ref, z_ref, sems, sem_z, *, blk, nblk):
    i = pl.program_id(0)
    last = pl.num_programs(0) - 1
    tm = h_ref.shape[0]
    lmax = sb_ref.shape[1]
    slot = i % 2
    seg = functools.partial(_seg_copies, ssrc_ref, sdst_ref, sn_ref, hbm_ref=xs_out, tm=tm, to_hbm=True)

    def seg_wait(tile, s):
        _seg_wait(stot_ref[tile], sb_ref.at[s], xs_out, sems.at[s], True)

    @pl.when(i >= 2)
    def _():
        seg_wait(i - 2, slot)

    hlo, hhi = _unpack_rows(h_ref[...])
    hb = jnp.concatenate([hlo.astype(bf16), hhi.astype(bf16)], axis=1)
    lpv = lp_ref[...]
    half = hb.shape[1] // 2
    for c in range(lmax // SORT_CHUNK):
        jj = (lax.broadcasted_iota(i32, (SORT_CHUNK, tm), 0) + c * SORT_CHUNK).astype(f32)
        onehot = jnp.zeros((SORT_CHUNK, tm), f32)
        for k in range(TOP_K):
            onehot = jnp.where(jj == lpv[k:k + 1, :], 1.0, onehot)
        rows = _dot(onehot.astype(bf16), hb)
        lo = lax.shift_right_logical(lax.bitcast_convert_type(rows[:, :half], u32), jnp.uint32(16))
        hi = lax.bitcast_convert_type(rows[:, half:], u32) & jnp.uint32(0xFFFF0000)
        sb_ref[slot, c * SORT_CHUNK:(c + 1) * SORT_CHUNK, :] = lo | hi

    seg(i, buf_ref=sb_ref.at[slot], sem=sems.at[slot])

    gt = _dot(hb, wsg_ref[...])
    up = _dot(hb, wsu_ref[...])
    ysh_o[...] = _dot(((gt * _sigmoid(gt)) * up).astype(bf16), wsd_ref[...]).astype(ysh_o.dtype)

    @pl.when(i == last)
    def _():
        z_ref[...] = jnp.zeros(z_ref.shape, u32)
        _zero_fill(cnt_ref, pstart_ref, nused_ref, z_ref, xs_out, sem_z, blk, nblk, True)
        _zero_fill(cnt_ref, pstart_ref, nused_ref, z_ref, xs_out, sem_z, blk, nblk, False)
        seg_wait(i, slot)

        @pl.when(i >= 1)
        def _():
            seg_wait(i - 1, 1 - slot)


def _dispatch(seg_src, seg_dst, seg_n, seg_tot, counts, pad_start, nused, lp_t, h2p, wsg, wsu, wsd,
              P, blk, tm, lmax):
    T, DW = h2p.shape
    D, DS = wsg.shape
    assert blk & (blk - 1) == 0 and tm & (tm - 1) == 0 and lmax % SORT_CHUNK == 0
    kern = functools.partial(_dispatch_kernel, blk=blk, nblk=P // blk)
    const = lambda shape: pl.BlockSpec(shape, lambda i, *_: (0, 0))
    grid_spec = pltpu.PrefetchScalarGridSpec(
        num_scalar_prefetch=7,
        grid=(T // tm,),
        in_specs=[pl.BlockSpec((TOP_K, tm), lambda i, *_: (0, i)),
                  pl.BlockSpec((tm, DW), lambda i, *_: (i, 0)),
                  const((D, DS)), const((D, DS)), const((DS, D))],
        out_specs=[pl.BlockSpec(memory_space=pl.ANY), pl.BlockSpec((tm, D), lambda i, *_: (i, 0))],
        scratch_shapes=[pltpu.VMEM((2, lmax, DW), u32), pltpu.VMEM((blk, DW), u32),
                        pltpu.SemaphoreType.DMA((2,)), pltpu.SemaphoreType.DMA],
    )
    return pl.pallas_call(
        kern,
        grid_spec=grid_spec,
        out_shape=[jax.ShapeDtypeStruct((P, DW), u32), jax.ShapeDtypeStruct((T, D), bf16)],
        compiler_params=_cparams(("arbitrary",)),
        name="moe_dispatch",
    )(seg_src, seg_dst, seg_n, seg_tot, counts, pad_start, nused, lp_t, h2p, wsg, wsu, wsd)


def _experts_kernel(blk_e_ref, nxt_e_ref, nused_ref, xs_ref, wg_hbm, wu_hbm, wd_hbm, ys_ref,
                    wg_f, wu_f, wd_f, wg_s, wu_s, wd_s, sems, *, layer):
    i = pl.program_id(0)
    e = blk_e_ref[i]
    changed = jnp.logical_or(i == 0, e != blk_e_ref[jnp.maximum(i - 1, 0)])

    def weight_copies(ex):
        return (pltpu.make_async_copy(wg_hbm.at[layer, ex], wg_f, sems.at[0]),
                pltpu.make_async_copy(wu_hbm.at[layer, ex], wu_f, sems.at[1]),
                pltpu.make_async_copy(wd_hbm.at[layer, ex], wd_f, sems.at[2]))

    @pl.when(i == 0)
    def _():
        for cp in weight_copies(e):
            cp.start()

    @pl.when(changed)
    def _():
        for cp in weight_copies(e):
            cp.wait()
        for src, dst in ((wg_f, wg_s), (wu_f, wu_s), (wd_f, wd_s)):
            rows = src.shape[0] // 8
            for c in range(8):
                dst[c * rows:(c + 1) * rows, :] = src[c * rows:(c + 1) * rows, :].astype(bf16)
        nxt = nxt_e_ref[i]

        @pl.when(nxt >= 0)
        def _():
            for cp in weight_copies(nxt):
                cp.start()

    @pl.when(i < nused_ref[0])
    def _():
        lo, hi = _unpack_rows(xs_ref[...])
        x = jnp.concatenate([lo.astype(bf16), hi.astype(bf16)], axis=1)
        gt = _dot(x, wg_s[...])
        up = _dot(x, wu_s[...])
        hmid = (gt * _sigmoid(gt)) * up
        ys_ref[...] = _pack_rows(_dot(hmid.astype(bf16), wd_s[...]))

    @pl.when(i >= nused_ref[0])
    def _():
        ys_ref[...] = jnp.zeros(ys_ref.shape, u32)


def _experts(blk_e, nxt_e, nused, xs, w_gate, w_up, w_down, layer, blk):
    P, DW = xs.shape
    D, DE = w_gate.shape[-2:]
    nblk = P // blk
    row_idx = lambda i, be, nx, nu: (jnp.minimum(i, nu[0] - 1), 0)
    hbm = pl.BlockSpec(memory_space=pl.ANY)
    grid_spec = pltpu.PrefetchScalarGridSpec(
        num_scalar_prefetch=3,
        grid=(nblk,),
        in_specs=[pl.BlockSpec((blk, DW), row_idx), hbm, hbm, hbm],
        out_specs=pl.BlockSpec((blk, DW), lambda i, be, nx, nu: (i, 0)),
        scratch_shapes=[pltpu.VMEM((D, DE), f32), pltpu.VMEM((D, DE), f32), pltpu.VMEM((DE, D), f32),
                        pltpu.VMEM((D, DE), bf16), pltpu.VMEM((D, DE), bf16), pltpu.VMEM((DE, D), bf16),
                        pltpu.SemaphoreType.DMA((3,))],
    )
    return pl.pallas_call(
        functools.partial(_experts_kernel, layer=layer),
        grid_spec=grid_spec,
        out_shape=jax.ShapeDtypeStruct((P, DW), u32),
        compiler_params=_cparams(("arbitrary",)),
        name="moe_experts",
    )(blk_e, nxt_e, nused, xs, w_gate, w_up, w_down)


def _combine_kernel(ssrc_ref, sdst_ref, sn_ref, stot_ref, lp_ref, gate_ref, ys_hbm, ysh_ref, x1_ref, gf_ref,
                    gpost_ref, x2_o, yb_ref, sems):
    i = pl.program_id(0)
    n = pl.num_programs(0)
    tm = x1_ref.shape[0]
    lmax = yb_ref.shape[1]
    slot = i % 2
    seg = functools.partial(_seg_copies, ssrc_ref, sdst_ref, sn_ref, hbm_ref=ys_hbm, tm=tm, to_hbm=False)

    @pl.when(i == 0)
    def _():
        yb_ref[...] = jnp.zeros(yb_ref.shape, u32)
        seg(i, buf_ref=yb_ref.at[slot], sem=sems.at[slot])

    @pl.when(i + 1 < n)
    def _():
        seg(i + 1, buf_ref=yb_ref.at[1 - slot], sem=sems.at[1 - slot])

    _seg_wait(stot_ref[i], yb_ref.at[slot], ys_hbm, sems.at[slot], False)

    lp = lp_ref[...]
    gate = gate_ref[...]
    ysh = ysh_ref[...].astype(f32)
    half = ysh.shape[1] // 2
    lo = ysh[:, :half]
    hi = ysh[:, half:]
    for c in range(lmax // SORT_CHUNK):
        jl = (lax.broadcasted_iota(i32, (tm, SORT_CHUNK), 1) + c * SORT_CHUNK).astype(f32)
        g = jnp.zeros((tm, SORT_CHUNK), f32)
        for k in range(TOP_K):
            g = jnp.where(jl == lp[:, k:k + 1], gate[:, k:k + 1], g)
        gb = g.astype(bf16)
        a, b = _unpack_rows(yb_ref[slot, c * SORT_CHUNK:(c + 1) * SORT_CHUNK, :])
        lo = lo + _dot(gb, a.astype(bf16))
        hi = hi + _dot(gb, b.astype(bf16))
    y = jnp.concatenate([lo, hi], axis=1)
    ms = jnp.mean(y * y, axis=-1, keepdims=True)
    x2_o[...] = x1_ref[...] + gf_ref[0] * (y * lax.rsqrt(ms + NORM_EPS) * gpost_ref[...])


def _combine(seg_src, seg_dst, seg_n, seg_tot, lp_tk, gate_tk, ys, ysh, x1, mod3, g_post, S, tm, lmax):
    T, D = x1.shape
    DW = ys.shape[1]
    tpb = S // tm
    tile = pl.BlockSpec((tm, D), lambda i, *_: (i, 0))
    ktile = pl.BlockSpec((tm, TOP_K), lambda i, *_: (i, 0))
    grid_spec = pltpu.PrefetchScalarGridSpec(
        num_scalar_prefetch=4,
        grid=(T // tm,),
        in_specs=[ktile, ktile,
                  pl.BlockSpec(memory_space=pl.ANY),
                  tile, tile,
                  pl.BlockSpec((1, 1, D), lambda i, *_: ((i // tpb) * N_MOD + 5, 0, 0)),
                  pl.BlockSpec((1, D), lambda i, *_: (0, 0))],
        out_specs=tile,
        scratch_shapes=[pltpu.VMEM((2, lmax, DW), u32), pltpu.SemaphoreType.DMA((2,))],
    )
    return pl.pallas_call(
        _combine_kernel,
        grid_spec=grid_spec,
        out_shape=jax.ShapeDtypeStruct((T, D), f32),
        compiler_params=_cparams(("arbitrary",)),
        name="moe_combine",
    )(seg_src, seg_dst, seg_n, seg_tot, lp_tk, gate_tk, ys, ysh, x1, mod3, g_post.reshape(1, D))


def _tile(n, pref):
    t = min(n, pref)
    assert n % t == 0, (n, t)
    return t


def _layer(i, x2, mod3, p, wexp, v_first, B, S, cfg):
    T, D = x2.shape
    H = (D // 2) // ATT_V_DIM
    W = D - D // 2
    att_cols = 2 * H * 2 * ATT_QK_DIM + H * ATT_V_DIM
    lam_init = 0.8 - 0.6 * math.exp(-0.3 * i)

    h1 = _prenorm(x2, mod3, p["g_pre_mix"], S, _tile(S, cfg["tm_norm"]), 1, 0)
    att = _inproj(h1, p["w_in"][:, :att_cols].astype(bf16), bf16, _tile(T, cfg["tm_in"]), cfg["tn_att"])
    w_rwkv = p["w_in"][:, att_cols:].astype(bf16)

    slopes = jnp.broadcast_to(
        (2.0 ** (-ALIBI_MAX_BIAS * jnp.arange(1, H + 1, dtype=f32) / H))[:, None, None], (H, 1, LANES))
    lamp = jnp.stack([p["lam_q1"], p["lam_k1"], p["lam_q2"], p["lam_k2"]])
    o_att = _attention(att, slopes, lamp, p["att_subln_g"], B, S, H, lam_init, _tile(S, cfg["tq"]), cfg["hp"])

    cols = w_rwkv.shape[1]
    zw = jnp.zeros((RWKV_A_RANK, W), f32)
    heads = W // RWKV_HEAD
    ind = (jnp.arange(W)[:, None] // RWKV_HEAD == jnp.arange(LANES)[None, :]).astype(bf16)
    prm = {
        "mu": p["rwkv_mu"].reshape(1, cols), "w0": p["rwkv_w0"].reshape(1, W),
        "w2p": jnp.concatenate([p["rwkv_w2"], zw], axis=0),
        "a0": p["rwkv_a0"].reshape(1, W),
        "a2p": jnp.concatenate([jnp.zeros((RWKV_W_RANK, W), f32), p["rwkv_a2"]], axis=0),
        "g2": p["rwkv_g2"], "k_k": p["rwkv_k_k"].reshape(1, W), "k_a": p["rwkv_k_a"].reshape(1, W),
        "ind": ind, "indt": ind.T,
    }
    if v_first is not None:
        padc = LANES - RWKV_V_RANK
        prm["v0"] = p["rwkv_v0"].reshape(1, W)
        prm["v1p"] = jnp.pad(p["rwkv_v1"], ((0, 0), (0, padc)))
        prm["v2p"] = jnp.pad(p["rwkv_v2"], ((0, padc), (0, 0)))
    r, lw, kh, v, kn, kb, g = _rwkv_prep(h1, w_rwkv, prm, v_first, B, S, W, _tile(S, cfg["tm_prep"]))
    if v_first is None:
        v_first = v
    r2, y0, mmat, g0 = _wkv_intra(r, lw, kh, v, kn, kb, min(cfg["nc"], S // CHUNK), cfg["passes_intra"])
    o_rwkv = _wkv_state(r2, y0, mmat, g0, r, kh, v, g, p["rwkv_lnx_w"].reshape(1, W),
                        p["rwkv_lnx_b"].reshape(1, W), p["rwkv_r_k"].reshape(1, W), B, S,
                        min(cfg["pg"], W // LANES), min(cfg["cb"], S // CHUNK), cfg["passes_state"])
    del heads

    x1, h2, logits_t = _postmix(o_att, o_rwkv, p["w_out"][:D // 2].astype(bf16), p["w_out"][D // 2:].astype(bf16),
                                x2, mod3,
                                p["g_post_mix"], p["g_pre_ffn"], p["w_router"].T, S, _tile(S, cfg["tm_post"]))

    tm_t = _tile(T, cfg["tm_tile"])
    ntiles = T // tm_t
    eidx_t, gate_t, lrank_t, cnt_tbl = _router(logits_t, p["router_bias"], tm_t)
    blk = cfg["blk"]
    run = (cnt_tbl[:, :ntiles].T + SUBLANES - 1) // SUBLANES * SUBLANES
    counts = jnp.sum(run, axis=0)
    padded = (counts + blk - 1) // blk * blk
    pad_end = jnp.cumsum(padded)
    pad_start = pad_end - padded
    e_ids = jnp.arange(N_EXPERTS, dtype=i32)
    tile_off = jnp.cumsum(run, axis=0) - run
    loc_off = jnp.cumsum(run, axis=1) - run
    run_dst = pad_start[None, :] + tile_off
    psz = jnp.asarray(_piece_sizes(tm_t), i32)[:, None, None]
    has = (run[None] & psz) != 0
    before = run[None] & ~(2 * psz - 1)
    slot_of = jnp.cumsum(has, axis=2) - 1
    hit = has[:, :, None, :] & (slot_of[:, :, None, :] == e_ids[None, None, :, None])
    seg_src = jnp.sum(jnp.where(hit, (loc_off[None] + before)[:, :, None, :], 0), axis=-1).reshape(-1).astype(i32)
    seg_dst = jnp.sum(jnp.where(hit, (run_dst[None] + before)[:, :, None, :], 0), axis=-1).reshape(-1).astype(i32)
    seg_n = jnp.sum(has, axis=2).reshape(-1).astype(i32)
    seg_tot = jnp.sum(run, axis=1).astype(i32)
    loc_tok = jnp.repeat(loc_off, tm_t, axis=0)
    lp_t = jnp.sum(jnp.where(eidx_t[:, :, None] == e_ids, loc_tok[None], 0), axis=-1).astype(f32) + lrank_t
    lmax = -(-(tm_t * TOP_K + N_EXPERTS * (SUBLANES - 1)) // SORT_CHUNK) * SORT_CHUNK
    nblk = -(-(T * TOP_K + ntiles * N_EXPERTS * (SUBLANES - 1)) // blk) + N_EXPERTS
    P = nblk * blk
    blk_start = jnp.arange(nblk, dtype=i32) * blk
    nused = (pad_end[-1] // blk).astype(i32).reshape(1)
    blk_pos = jnp.minimum(blk_start, pad_end[-1] - blk)
    blk_e = jnp.minimum(jnp.sum((pad_end[None, :] <= blk_pos[:, None]).astype(i32), axis=1), N_EXPERTS - 1)
    cand = jnp.where(counts > 0, e_ids, N_EXPERTS)
    later = jnp.where(e_ids[None, :] > blk_e[:, None], cand[None, :], N_EXPERTS)
    nxt_e = jnp.min(later, axis=1)
    nxt_e = jnp.where(nxt_e >= N_EXPERTS, -1, nxt_e).astype(i32)

    xs, ysh = _dispatch(seg_src, seg_dst, seg_n, seg_tot, counts.astype(i32), pad_start.astype(i32), nused, lp_t, h2,
                        p["w_sh_gate"].astype(bf16), p["w_sh_up"].astype(bf16), p["w_sh_down"].astype(bf16),
                        P, blk, tm_t, lmax)
    ys = _experts(blk_e, nxt_e, nused, xs, wexp[0], wexp[1], wexp[2], i, blk)
    x_out = _combine(seg_src, seg_dst, seg_n, seg_tot, lp_t.T, gate_t.T, ys, ysh, x1,
                     mod3, p["g_post_ffn"], S, tm_t, lmax)
    return x_out, v_first


_CFG = dict(tm_norm=512, tm_in=2048, tn_att=1024, tq=512, hp=4, tm_prep=256, nc=16, passes_intra=1, passes_state=1, pg=8, cb=8,
            tm_post=512, blk=256, tm_tile=256)

_LAYER_KEYS = ("g_pre_mix", "g_post_mix", "g_pre_ffn", "g_post_ffn", "w_in", "w_out", "lam_q1", "lam_k1",
               "lam_q2", "lam_k2", "att_subln_g", "rwkv_mu", "rwkv_w0", "rwkv_w2", "rwkv_a0", "rwkv_a2",
               "rwkv_g2", "rwkv_k_k", "rwkv_k_a", "rwkv_r_k", "rwkv_lnx_w", "rwkv_lnx_b", "w_router",
               "router_bias", "w_sh_gate", "w_sh_up", "w_sh_down")


def _forward(x, c, params, cfg):
    B, S, D = x.shape
    L = params["w_in"].shape[0]
    bp = 16
    c_pad = jnp.zeros((bp, D), f32).at[:B].set(c)
    mod = _ada_mod(c_pad, params["w_ada"], params["b_ada"])
    x2 = x.reshape(B * S, D)
    v_first = None
    for i in range(L):
        p = {k: params[k][i] for k in _LAYER_KEYS}
        if i > 0:
            p["rwkv_v0"] = params["rwkv_v0"][i - 1]
            p["rwkv_v1"] = params["rwkv_v1"][i - 1]
            p["rwkv_v2"] = params["rwkv_v2"][i - 1]
        mod3 = mod[i, :B].reshape(B * N_MOD, 1, D)
        wexp = (params["w_exp_gate"], params["w_exp_up"], params["w_exp_down"])
        x2, v_first = _layer(i, x2, mod3, p, wexp, v_first, B, S, cfg)
    return x2.reshape(B, S, D)


def kernel(x, c, w_ada, b_ada, g_pre_mix, g_post_mix, g_pre_ffn, g_post_ffn, w_in, w_out, lam_q1, lam_k1, lam_q2, lam_k2, att_subln_g, rwkv_mu, rwkv_w0, rwkv_w2, rwkv_a0, rwkv_a2, rwkv_g2, rwkv_k_k, rwkv_k_a, rwkv_r_k, rwkv_lnx_w, rwkv_lnx_b, rwkv_v0, rwkv_v1, rwkv_v2, w_router, router_bias, w_exp_gate, w_exp_up, w_exp_down, w_sh_gate, w_sh_up, w_sh_down):
    params = dict(w_ada=w_ada, b_ada=b_ada, g_pre_mix=g_pre_mix, g_post_mix=g_post_mix, g_pre_ffn=g_pre_ffn,
                  g_post_ffn=g_post_ffn, w_in=w_in, w_out=w_out, lam_q1=lam_q1, lam_k1=lam_k1, lam_q2=lam_q2,
                  lam_k2=lam_k2, att_subln_g=att_subln_g, rwkv_mu=rwkv_mu, rwkv_w0=rwkv_w0, rwkv_w2=rwkv_w2,
                  rwkv_a0=rwkv_a0, rwkv_a2=rwkv_a2, rwkv_g2=rwkv_g2, rwkv_k_k=rwkv_k_k, rwkv_k_a=rwkv_k_a,
                  rwkv_r_k=rwkv_r_k, rwkv_lnx_w=rwkv_lnx_w, rwkv_lnx_b=rwkv_lnx_b, rwkv_v0=rwkv_v0,
                  rwkv_v1=rwkv_v1, rwkv_v2=rwkv_v2, w_router=w_router, router_bias=router_bias,
                  w_exp_gate=w_exp_gate, w_exp_up=w_exp_up, w_exp_down=w_exp_down, w_sh_gate=w_sh_gate,
                  w_sh_up=w_sh_up, w_sh_down=w_sh_down)
    return _forward(x, c, params, _CFG)
```

```python
import functools
import math

import jax
import jax.numpy as jnp
from jax import lax
from jax.experimental import pallas as pl
from jax.experimental.pallas import tpu as pltpu

f32 = jnp.float32
bf16 = jnp.bfloat16
i32 = jnp.int32
u32 = jnp.uint32

ATT_QK_DIM = 64
ATT_V_DIM = 128
ALIBI_MAX_BIAS = 8.0
ATT_SUBLN_EPS = 1e-5
RWKV_HEAD = 64
RWKV_W_RANK = 64
RWKV_A_RANK = 64
RWKV_G_RANK = 128
RWKV_V_RANK = 32
RWKV_GN_EPS = 64e-5
N_EXPERTS = 64
N_GROUPS = 8
TOPK_GROUPS = 4
TOP_K = 8
ROUTED_SCALE = 2.5
NORM_EPS = 1e-6
N_MOD = 6

LANES = 128
SUBLANES = 8
CHUNK = 64
VMEM_LIMIT = 56 * 1024 * 1024


def _cparams(sem):
    return pltpu.CompilerParams(dimension_semantics=sem, vmem_limit_bytes=VMEM_LIMIT)


def _dot(a, b):
    return jnp.dot(a, b, preferred_element_type=f32)


def _dot_nt(a, b):
    return lax.dot_general(a, b, (((1,), (1,)), ((), ())), preferred_element_type=f32)


def _split2(x):
    hi = x.astype(bf16)
    lo = (x - hi.astype(f32)).astype(bf16)
    return hi, lo


def _mm(a, b, passes=1):
    if passes == 1:
        return _dot(a.astype(bf16), b.astype(bf16))
    ah, al = _split2(a.astype(f32))
    bh, bl = _split2(b.astype(f32))
    return (_dot(al, bh) + _dot(ah, bl)) + _dot(ah, bh)


def _mm_exact_rhs(a, b_bf16):
    ah, al = _split2(a)
    return _dot(al, b_bf16) + _dot(ah, b_bf16)


def _sigmoid(x):
    return 1.0 / (1.0 + jnp.exp(-x))


def _pack_rows(x):
    half = x.shape[1] // 2
    a = x[:, :half].astype(bf16).astype(f32)
    b = x[:, half:].astype(bf16).astype(f32)
    lo = lax.shift_right_logical(lax.bitcast_convert_type(a, u32), jnp.uint32(16))
    hi = lax.bitcast_convert_type(b, u32) & jnp.uint32(0xFFFF0000)
    return lo | hi


def _unpack_rows(w):
    lo = lax.bitcast_convert_type(lax.shift_left(w, jnp.uint32(16)), f32)
    hi = lax.bitcast_convert_type(w & jnp.uint32(0xFFFF0000), f32)
    return lo, hi


def _ada_kernel(c_ref, w_ref, b_ref, o_ref):
    c = c_ref[...]
    cond = (c * _sigmoid(c)).astype(bf16)
    o_ref[0] = _dot(cond, w_ref[0].astype(bf16)) + b_ref[0]


def _ada_mod(c_pad, w_ada, b_ada, tn=1024):
    L, D, N = w_ada.shape
    bp = c_pad.shape[0]
    return pl.pallas_call(
        _ada_kernel,
        grid=(L, N // tn),
        in_specs=[pl.BlockSpec((bp, D), lambda l, j: (0, 0)),
                  pl.BlockSpec((1, D, tn), lambda l, j: (l, 0, j)),
                  pl.BlockSpec((1, 1, tn), lambda l, j: (l, 0, j))],
        out_specs=pl.BlockSpec((1, bp, tn), lambda l, j: (l, 0, j)),
        out_shape=jax.ShapeDtypeStruct((L, bp, N), f32),
        compiler_params=_cparams(("arbitrary", "arbitrary")),
        name="ada_mod",
    )(c_pad, w_ada, b_ada.reshape(L, 1, N))


def _prenorm_kernel(x_ref, sc_ref, sh_ref, g_ref, h_ref):
    x = x_ref[...]
    ms = jnp.mean(x * x, axis=-1, keepdims=True)
    y = x * lax.rsqrt(ms + NORM_EPS) * g_ref[...]
    h_ref[...] = (y * (1.0 + sc_ref[0]) + sh_ref[0]).astype(h_ref.dtype)


def _prenorm(x2, mod3, g, S, tm, seg_sc, seg_sh):
    T, D = x2.shape
    tpb = S // tm
    tile = pl.BlockSpec((tm, D), lambda i: (i, 0))
    return pl.pallas_call(
        _prenorm_kernel,
        grid=(T // tm,),
        in_specs=[tile,
                  pl.BlockSpec((1, 1, D), lambda i: ((i // tpb) * N_MOD + seg_sc, 0, 0)),
                  pl.BlockSpec((1, 1, D), lambda i: ((i // tpb) * N_MOD + seg_sh, 0, 0)),
                  pl.BlockSpec((1, D), lambda i: (0, 0))],
        out_specs=tile,
        out_shape=jax.ShapeDtypeStruct((T, D), bf16),
        compiler_params=_cparams(("arbitrary",)),
        name="prenorm",
    )(x2, mod3, mod3, g.reshape(1, D))


def _inproj_kernel(h_ref, w_ref, o_ref):
    o_ref[...] = _dot(h_ref[...], w_ref[...]).astype(o_ref.dtype)


def _inproj(h, w_bf, out_dtype, tm, tn):
    T, D = h.shape
    N = w_bf.shape[1]
    return pl.pallas_call(
        _inproj_kernel,
        grid=(T // tm, N // tn),
        in_specs=[pl.BlockSpec((tm, D), lambda i, j: (i, 0)),
                  pl.BlockSpec((D, tn), lambda i, j: (0, j))],
        out_specs=pl.BlockSpec((tm, tn), lambda i, j: (i, j)),
        out_shape=jax.ShapeDtypeStruct((T, N), out_dtype),
        compiler_params=_cparams(("arbitrary", "arbitrary")),
        name="inproj",
    )(h, w_bf)


def _attn_kernel(q_ref, k_ref, v_ref, slope_ref, lamp_ref, g_ref, o_ref,
                 q2t_ref, vt_ref, m_ref, l_ref, acc_ref, *, tq, hp, lam_init):
    qi = pl.program_id(2)
    scale = ATT_QK_DIM ** -0.5
    heads = range(hp)
    hl = [slice(h * LANES, (h + 1) * LANES) for h in heads]
    slope = [slope_ref[h][:, 0:1] for h in heads]

    @pl.when(qi == 0)
    def _():
        for h in heads:
            vt_ref[h] = v_ref[:, hl[h]].astype(f32).T.astype(bf16)

    dim = lax.broadcasted_iota(i32, (LANES, 1), 0)
    first = dim < ATT_QK_DIM
    for h in heads:
        qt = (q_ref[:, hl[h]].astype(f32) * scale).T
        q2t_ref[h, :, 0:tq] = jnp.where(first, qt, 0.0).astype(bf16)
        q2t_ref[h, :, tq:2 * tq] = jnp.where(first, 0.0, qt).astype(bf16)
    m_ref[...] = jnp.full(m_ref.shape, -jnp.inf, f32)
    l_ref[...] = jnp.zeros(l_ref.shape, f32)
    acc_ref[...] = jnp.zeros(acc_ref.shape, f32)

    kr = lax.broadcasted_iota(i32, (tq, 2 * tq), 0)
    qc = lax.broadcasted_iota(i32, (tq, 2 * tq), 1)
    causal = jnp.where(qc >= tq, qc - tq, qc) >= kr
    krow = lax.broadcasted_iota(i32, (tq, 1), 0).astype(f32)

    def step(ki, masked):
        start = pl.multiple_of(ki * tq, tq)
        kpos = krow + (ki * tq).astype(f32)
        s = [_dot(k_ref[pl.ds(start, tq), hl[h]], q2t_ref[h]) for h in heads]
        s = [s[h] + slope[h] * kpos for h in heads]
        if masked:
            s = [jnp.where(causal, s[h], -jnp.inf) for h in heads]
        m_prev = [m_ref[h] for h in heads]
        m_new = [jnp.maximum(m_prev[h], jnp.max(s[h], axis=0, keepdims=True)) for h in heads]
        alpha = [jnp.exp(m_prev[h] - m_new[h]) for h in heads]
        p = [jnp.exp(s[h] - m_new[h]) for h in heads]
        for h in heads:
            l_ref[h] = alpha[h] * l_ref[h] + jnp.sum(p[h], axis=0, keepdims=True)
            acc_ref[h] = alpha[h] * acc_ref[h] + _dot(vt_ref[h, :, pl.ds(start, tq)], p[h].astype(bf16))
            m_ref[h] = m_new[h]

    def body(ki, carry):
        step(ki, False)
        return carry

    lax.fori_loop(0, qi, body, 0)
    step(qi, True)

    lp = lamp_ref[...]
    lam = (jnp.exp(jnp.sum(lp[0:1] * lp[1:2], axis=-1, keepdims=True))
           - jnp.exp(jnp.sum(lp[2:3] * lp[3:4], axis=-1, keepdims=True)) + lam_init)
    for h in heads:
        on = acc_ref[h] * (1.0 / l_ref[h])
        o = on[:, 0:tq] - lam * on[:, tq:2 * tq]
        o = o * lax.rsqrt(jnp.mean(o * o, axis=0, keepdims=True) + ATT_SUBLN_EPS)
        o = o * g_ref[...] * (1.0 - lam_init)
        o_ref[:, hl[h]] = o.T.astype(o_ref.dtype)


def _attention(att, slopes, lamp, subln_g, B, S, H, lam_init, tq, hp):
    T = att.shape[0]
    nq = S // tq
    kern = functools.partial(_attn_kernel, tq=tq, hp=hp, lam_init=lam_init)
    hw = hp * LANES
    return pl.pallas_call(
        kern,
        grid=(B, H // hp, nq),
        in_specs=[pl.BlockSpec((tq, hw), lambda b, g, q: (b * nq + q, g)),
                  pl.BlockSpec((S, hw), lambda b, g, q: (b, H // hp + g)),
                  pl.BlockSpec((S, hw), lambda b, g, q: (b, 2 * (H // hp) + g)),
                  pl.BlockSpec((hp, 1, LANES), lambda b, g, q: (g, 0, 0)),
                  pl.BlockSpec((4, ATT_QK_DIM), lambda b, g, q: (0, 0)),
                  pl.BlockSpec((ATT_V_DIM, 1), lambda b, g, q: (0, 0))],
        out_specs=pl.BlockSpec((tq, hw), lambda b, g, q: (b * nq + q, g)),
        out_shape=jax.ShapeDtypeStruct((T, H * ATT_V_DIM), bf16),
        scratch_shapes=[pltpu.VMEM((hp, LANES, 2 * tq), bf16),
                        pltpu.VMEM((hp, LANES, S), bf16),
                        pltpu.VMEM((hp, 1, 2 * tq), f32),
                        pltpu.VMEM((hp, 1, 2 * tq), f32),
                        pltpu.VMEM((hp, LANES, 2 * tq), f32)],
        compiler_params=_cparams(("arbitrary", "arbitrary", "arbitrary")),
        name="diff_attention",
    )(att, att, att, slopes, lamp, subln_g.reshape(ATT_V_DIM, 1))


def _head_sums(x, ind, indt):
    s = _mm_exact_rhs(x, ind)
    return _mm_exact_rhs(s, indt)


def _rwkv_prep_kernel(*refs, W, has_vres, tpb):
    if has_vres:
        (hx_ref, win_ref, mu_ref, w0_ref, w2_ref, a0_ref, a2_ref, g2_ref, kk_ref, ka_ref, ind_ref, indt_ref,
         vf_ref, v0_ref, v1_ref, v2_ref,
         r_o, lw_o, kh_o, v_o, kn_o, kb_o, g_o, carry_ref, feats_ref) = refs
    else:
        (hx_ref, win_ref, mu_ref, w0_ref, w2_ref, a0_ref, a2_ref, g2_ref, kk_ref, ka_ref, ind_ref, indt_ref,
         r_o, lw_o, kh_o, v_o, kn_o, kb_o, g_o, carry_ref, feats_ref) = refs

    i = pl.program_id(0)

    @pl.when(i == 0)
    def _():
        feats_ref[...] = jnp.zeros(feats_ref.shape, f32)
        carry_ref[...] = jnp.zeros(carry_ref.shape, f32)

    h = feats_ref[...]
    nxt = _dot(hx_ref[...], win_ref[...])
    tm = h.shape[0]

    first_in_batch = lax.rem(jnp.maximum(i - 1, 0), tpb) == 0
    carry = jnp.where(first_in_batch, 0.0, carry_ref[...])
    rolled = pltpu.roll(h, 1, axis=0)
    row = lax.broadcasted_iota(i32, (tm, 1), 0)
    prev = jnp.where(row == 0, carry, rolled)
    carry_ref[...] = h[tm - 1:tm, :]
    feats_ref[...] = nxt
    feats = h + (prev - h) * mu_ref[...]

    r = feats[:, 0:W]
    k = feats[:, W:2 * W]
    v = feats[:, 2 * W:3 * W]
    wa = feats[:, 3 * W:3 * W + LANES]
    g_lo = feats[:, 3 * W + LANES:3 * W + 2 * LANES]

    w = w0_ref[...] + _mm(jnp.tanh(wa), w2_ref[...], passes=3)
    lw_o[...] = -math.exp(-0.5) * _sigmoid(w)
    a = _sigmoid(a0_ref[...] + _mm(wa, a2_ref[...], passes=3))
    g_o[...] = _mm(_sigmoid(g_lo), g2_ref[...]).astype(g_o.dtype)

    if has_vres:
        mix = _sigmoid(v0_ref[...] + _mm(_mm(v, v1_ref[...]), v2_ref[...]))
        v = v + (vf_ref[...].astype(f32) - v) * mix

    kk = k * kk_ref[...]
    ss = _head_sums(kk * kk, ind_ref[...], indt_ref[...])
    kk = kk / jnp.maximum(jnp.sqrt(ss), 1e-12)
    r_o[...] = r.astype(r_o.dtype)
    kh_o[...] = (k * (1.0 + (a - 1.0) * ka_ref[...])).astype(kh_o.dtype)
    v_o[...] = v.astype(v_o.dtype)
    kn_o[...] = kk.astype(kn_o.dtype)
    kb_o[...] = (kk * a).astype(kb_o.dtype)


def _rwkv_prep(hx, w_rwkv, prm, vfirst, B, S, W, tm):
    T, D = hx.shape
    COLS = w_rwkv.shape[1]
    tpb = S // tm
    n = T // tm
    has_vres = vfirst is not None
    row = lambda c: pl.BlockSpec((1, c), lambda i: (0, 0))
    full = lambda a: pl.BlockSpec(a.shape, lambda i: (0, 0))
    tile = pl.BlockSpec((tm, W), lambda i: (jnp.maximum(i - 1, 0), 0))
    args = [hx, w_rwkv, prm["mu"], prm["w0"], prm["w2p"], prm["a0"], prm["a2p"], prm["g2"], prm["k_k"], prm["k_a"],
            prm["ind"], prm["indt"]]
    specs = [pl.BlockSpec((tm, D), lambda i: (jnp.minimum(i, n - 1), 0)),
             pl.BlockSpec((D, COLS), lambda i: (0, 0), pipeline_mode=pl.Buffered(1)),
             row(COLS), row(W), full(prm["w2p"]),
             row(W), full(prm["a2p"]), full(prm["g2"]), row(W), row(W), full(prm["ind"]), full(prm["indt"])]
    if has_vres:
        args += [vfirst, prm["v0"], prm["v1p"], prm["v2p"]]
        specs += [tile, row(W), full(prm["v1p"]), full(prm["v2p"])]
    out_dtypes = (bf16, f32, bf16, bf16, bf16, bf16, bf16)
    kern = functools.partial(_rwkv_prep_kernel, W=W, has_vres=has_vres, tpb=tpb)
    return pl.pallas_call(
        kern,
        grid=(n + 1,),
        in_specs=specs,
        out_specs=[tile] * 7,
        out_shape=[jax.ShapeDtypeStruct((T, W), dt) for dt in out_dtypes],
        scratch_shapes=[pltpu.VMEM((1, COLS), f32), pltpu.VMEM((tm, COLS), f32)],
        compiler_params=_cparams(("arbitrary",)),
        name="rwkv_prep",
    )(*args)


def _wkv_chunks(rs, lws, ks, vs, kns, kbs, passes):
    C = CHUNK
    P2 = 2 * C
    n = range(len(rs))
    ri = lax.broadcasted_iota(i32, (C, C), 0)
    ci = lax.broadcasted_iota(i32, (C, C), 1)
    tri = (ci <= ri).astype(bf16)
    lane = lax.broadcasted_iota(i32, (1, LANES), 1)
    m0 = (lane < RWKV_HEAD).astype(f32)
    m1 = 1.0 - m0
    rr = lax.broadcasted_iota(i32, (P2, P2), 0)
    cc = lax.broadcasted_iota(i32, (P2, P2), 1)
    same = jnp.where(rr >= C, 1, 0) == jnp.where(cc >= C, 1, 0)
    strict = same & (cc < rr)
    incl = same & (cc <= rr)
    incl2 = jnp.concatenate([incl, incl], axis=1)
    eye = (rr == cc).astype(f32)
    zeros_p = jnp.zeros((P2, LANES), f32)
    zeros_c = jnp.zeros((C, LANES), f32)
    stack = lambda x: jnp.concatenate([x * m0, x * m1], axis=0)
    fold = lambda x: x[0:C] + x[C:2 * C]

    def cumsum(lw):
        h1 = lw.astype(bf16)
        r1 = lw - h1.astype(f32)
        h2 = r1.astype(bf16)
        h3 = (r1 - h2.astype(f32)).astype(bf16)
        return (_dot(tri, h3) + _dot(tri, h2)) + _dot(tri, h1)

    cum = [cumsum(lws[j]) for j in n]
    cum_c = [cum[j][C - 1:C, :] for j in n]
    at = [-kns[j] * jnp.exp(cum[j] - lws[j]) for j in n]
    rt = [rs[j] * jnp.exp(cum[j]) for j in n]
    einv = [jnp.exp(-cum[j]) for j in n]
    bt = [kbs[j] * einv[j] for j in n]
    kt = [ks[j] * einv[j] for j in n]
    eh = [jnp.exp(cum_c[j] - cum[j]) for j in n]
    bh = [kbs[j] * eh[j] for j in n]
    kh = [ks[j] * eh[j] for j in n]
    w_c = [jnp.exp(cum_c[j]) for j in n]
    abd = [stack(at[j]) for j in n]
    vst = [stack(vs[j]) for j in n]
    lhs = [jnp.concatenate([abd[j], stack(rt[j])], axis=0) for j in n]
    rhs = [jnp.concatenate([stack(bt[j]), stack(kt[j])], axis=0) for j in n]
    gram = [_mm_nt(lhs[j], rhs[j], passes) for j in n]
    lab = [jnp.where(strict, gram[j][0:P2, 0:P2], 0.0) for j in n]
    lak = [jnp.where(strict, gram[j][0:P2, P2:2 * P2], 0.0) for j in n]
    mrbk = [jnp.where(incl2, gram[j][P2:2 * P2, :], 0.0) for j in n]

    x0 = [_mm(lak[j], vst[j], passes) for j in n]
    tinv = [eye + lab[j] for j in n]
    lp = [_mm(lab[j], lab[j], passes) for j in n]
    n_sq = int(math.log2(C)) - 1
    for it in range(n_sq):
        if it < n_sq - 1:
            both = [_mm(lp[j], jnp.concatenate([lp[j], tinv[j]], axis=1), passes) for j in n]
            tinv = [tinv[j] + both[j][:, P2:2 * P2] for j in n]
            lp = [both[j][:, 0:P2] for j in n]
        else:
            tinv = [tinv[j] + _mm(lp[j], tinv[j], passes) for j in n]

    ta = [_mm(tinv[j], jnp.concatenate([abd[j], x0[j]], axis=1), passes) for j in n]
    rhs2 = [jnp.concatenate([ta[j], jnp.concatenate([zeros_p, vst[j]], axis=1)], axis=0) for j in n]
    z = [_mm(mrbk[j], rhs2[j], passes) for j in n]
    r2 = [rt[j] + fold(z[j][:, 0:LANES]) for j in n]
    y0 = [fold(z[j][:, LANES:2 * LANES]) for j in n]
    lhs3t = [jnp.concatenate([bh[j], kh[j]], axis=0).T for j in n]
    rhs3 = [jnp.concatenate([fold(ta[j]), jnp.concatenate([zeros_c, vs[j]], axis=1)], axis=0) for j in n]
    wmat = [_mm(lhs3t[j], rhs3[j], passes) for j in n]
    mmat = [jnp.where(same, wmat[j][:, 0:LANES], 0.0) + eye * w_c[j] for j in n]
    g0 = [jnp.where(same, wmat[j][:, LANES:2 * LANES], 0.0) for j in n]
    return r2, y0, mmat, g0


def _mm_nt(a, b, passes):
    if passes == 1:
        return _dot_nt(a.astype(bf16), b.astype(bf16))
    ah, al = _split2(a)
    bh, bl = _split2(b)
    return (_dot_nt(al, bh) + _dot_nt(ah, bl)) + _dot_nt(ah, bh)


def _wkv_intra_kernel(r_ref, lw_ref, k_ref, v_ref, kn_ref, kb_ref, r2_o, y0_o, m_o, g_o, *, nc, passes):
    C = CHUNK
    sls = [slice(c * C, (c + 1) * C) for c in range(nc)]
    take = lambda ref: [ref[sl, :].astype(f32) for sl in sls]
    r2, y0, mmat, g0 = _wkv_chunks(take(r_ref), take(lw_ref), take(k_ref), take(v_ref),
                                   take(kn_ref), take(kb_ref), passes)
    for c, sl in enumerate(sls):
        r2_o[sl, :] = r2[c].astype(r2_o.dtype)
        y0_o[sl, :] = y0[c].astype(y0_o.dtype)
        m_o[0, c] = mmat[c].astype(m_o.dtype)
        g_o[0, c] = g0[c]


def _wkv_intra(r, lw, kh, v, kn, kb, nc, passes):
    T, W = r.shape
    npair = W // LANES
    rows = nc * CHUNK
    tile = pl.BlockSpec((rows, LANES), lambda p, i: (i, p))
    mat = pl.BlockSpec((1, nc, LANES, LANES), lambda p, i: (p, i, 0, 0))
    kern = functools.partial(_wkv_intra_kernel, nc=nc, passes=passes)
    return pl.pallas_call(
        kern,
        grid=(npair, T // rows),
        in_specs=[tile] * 6,
        out_specs=[tile, tile, mat, mat],
        out_shape=[jax.ShapeDtypeStruct((T, W), bf16), jax.ShapeDtypeStruct((T, W), bf16),
                   jax.ShapeDtypeStruct((npair, T // CHUNK, LANES, LANES), bf16),
                   jax.ShapeDtypeStruct((npair, T // CHUNK, LANES, LANES), f32)],
        compiler_params=_cparams(("arbitrary", "arbitrary")),
        name="wkv_intra",
    )(r, lw, kh, v, kn, kb)


def _wkv_state_kernel(r2_ref, y0_ref, m_ref, g0_ref, r_ref, kh_ref, v_ref, g_ref,
                      lnw_ref, lnb_ref, rk_ref, o_ref, st_ref, y_ref, *, pg, cb, passes):
    C = CHUNK

    @pl.when(pl.program_id(2) == 0)
    def _():
        st_ref[...] = jnp.zeros(st_ref.shape, f32)

    pairs = range(pg)
    lanes = [slice(p * LANES, (p + 1) * LANES) for p in pairs]
    st = [st_ref[p] for p in pairs]
    for c in range(cb):
        rows = slice(c * C, (c + 1) * C)
        for p in pairs:
            y_ref[rows, lanes[p]] = (_mm(r2_ref[rows, lanes[p]], st[p], passes)
                                     + y0_ref[rows, lanes[p]].astype(f32))
        st = [_mm(m_ref[p, c], st[p], passes) + g0_ref[p, c] for p in pairs]
    for p in pairs:
        st_ref[p] = st[p]

    rr = lax.broadcasted_iota(i32, (LANES, LANES), 0)
    cc = lax.broadcasted_iota(i32, (LANES, LANES), 1)
    ones_bd = (jnp.where(rr >= RWKV_HEAD, 1, 0) == jnp.where(cc >= RWKV_HEAD, 1, 0)).astype(bf16)
    for p in pairs:
        y = y_ref[:, lanes[p]]
        mu = _mm_exact_rhs(y, ones_bd) * (1.0 / RWKV_HEAD)
        d = y - mu
        var = _mm_exact_rhs(d * d, ones_bd) * (1.0 / RWKV_HEAD)
        yn = d * lax.rsqrt(var + RWKV_GN_EPS) * lnw_ref[:, lanes[p]] + lnb_ref[:, lanes[p]]
        rk = r_ref[:, lanes[p]].astype(f32) * kh_ref[:, lanes[p]].astype(f32) * rk_ref[:, lanes[p]]
        bonus = _mm_exact_rhs(rk, ones_bd) * v_ref[:, lanes[p]]
        o_ref[:, lanes[p]] = ((yn + bonus) * g_ref[:, lanes[p]]).astype(o_ref.dtype)


def _wkv_state(r2, y0, mm, g0, r, kh, v, g, lnw, lnb, rk, B, S, pg, cb, passes):
    T, W = r.shape
    npair = W // LANES
    rows = cb * CHUNK
    steps = S // rows
    seq = pl.BlockSpec((rows, pg * LANES), lambda b, q, c: (b * steps + c, q))
    mat = pl.BlockSpec((pg, cb, LANES, LANES), lambda b, q, c: (q, b * steps + c, 0, 0))
    prow = pl.BlockSpec((1, pg * LANES), lambda b, q, c: (0, q))
    kern = functools.partial(_wkv_state_kernel, pg=pg, cb=cb, passes=passes)
    return pl.pallas_call(
        kern,
        grid=(B, npair // pg, steps),
        in_specs=[seq, seq, mat, mat, seq, seq, seq, seq, prow, prow, prow],
        out_specs=seq,
        out_shape=jax.ShapeDtypeStruct((T, W), bf16),
        scratch_shapes=[pltpu.VMEM((pg, LANES, LANES), f32), pltpu.VMEM((rows, pg * LANES), f32)],
        compiler_params=_cparams(("arbitrary", "arbitrary", "arbitrary")),
        name="wkv_state",
    )(r2, y0, mm, g0, r, kh, v, g, lnw, lnb, rk)


def _postmix_kernel(oa_ref, orw_ref, wa_ref, wr_ref, x_ref, ga_ref, gpost_ref, gpre_ref, sc_ref, sh_ref, wrt_ref,
                    x1_o, h2_o, lg_o):
    mixed = _dot(oa_ref[...], wa_ref[...]) + _dot(orw_ref[...], wr_ref[...])
    ms = jnp.mean(mixed * mixed, axis=-1, keepdims=True)
    x1 = x_ref[...] + ga_ref[0] * (mixed * lax.rsqrt(ms + NORM_EPS) * gpost_ref[...])
    x1_o[...] = x1
    ms1 = jnp.mean(x1 * x1, axis=-1, keepdims=True)
    h2 = (x1 * lax.rsqrt(ms1 + NORM_EPS) * gpre_ref[...]) * (1.0 + sc_ref[0]) + sh_ref[0]
    h2_o[...] = _pack_rows(h2)
    wh, wl = _split2(wrt_ref[...])
    hh, hl = _split2(h2)
    e = wh.shape[0]
    top = _dot_nt(jnp.concatenate([wh, wl], axis=0), hh)
    lg_o[...] = (top[e:2 * e] + _dot_nt(wh, hl)) + top[0:e]


def _postmix(o_att, o_rwkv, w_out_a, w_out_r, x2, mod3, g_post, g_pre, w_rt, S, tm):
    T, D = x2.shape
    WA = o_att.shape[1]
    WR = o_rwkv.shape[1]
    E = w_rt.shape[0]
    tpb = S // tm
    modspec = lambda seg: pl.BlockSpec((1, 1, D), lambda i: ((i // tpb) * N_MOD + seg, 0, 0))
    tile = pl.BlockSpec((tm, D), lambda i: (i, 0))
    return pl.pallas_call(
        _postmix_kernel,
        grid=(T // tm,),
        in_specs=[pl.BlockSpec((tm, WA), lambda i: (i, 0)),
                  pl.BlockSpec((tm, WR), lambda i: (i, 0)),
                  pl.BlockSpec((WA, D), lambda i: (0, 0), pipeline_mode=pl.Buffered(1)),
                  pl.BlockSpec((WR, D), lambda i: (0, 0), pipeline_mode=pl.Buffered(1)),
                  tile, modspec(2),
                  pl.BlockSpec((1, D), lambda i: (0, 0)),
                  pl.BlockSpec((1, D), lambda i: (0, 0)),
                  modspec(4), modspec(3),
                  pl.BlockSpec((E, D), lambda i: (0, 0))],
        out_specs=[tile, pl.BlockSpec((tm, D // 2), lambda i: (i, 0)),
                   pl.BlockSpec((E, tm), lambda i: (0, i))],
        out_shape=[jax.ShapeDtypeStruct((T, D), f32), jax.ShapeDtypeStruct((T, D // 2), u32),
                   jax.ShapeDtypeStruct((E, T), f32)],
        compiler_params=_cparams(("arbitrary",)),
        name="postmix",
    )(o_att, o_rwkv, w_out_a, w_out_r, x2, mod3, g_post.reshape(1, D), g_pre.reshape(1, D), mod3, mod3, w_rt)


def _first_max(x, iota, n):
    mx = jnp.max(x, axis=0, keepdims=True)
    idx = jnp.min(jnp.where(x == mx, iota, n), axis=0, keepdims=True)
    return mx, idx


def _router_kernel(lg_ref, bias_ref, eidx_o, gate_o, rank_o, cnt_o, cnt_ref):
    E = N_EXPERTS
    G = N_GROUPS
    per = E // G
    tm = lg_ref.shape[1]

    @pl.when(pl.program_id(0) == 0)
    def _():
        cnt_ref[...] = jnp.zeros(cnt_ref.shape, f32)

    scores = _sigmoid(lg_ref[...])
    biased = scores + bias_ref[...]
    neg = -jnp.inf

    iota_p = lax.broadcasted_iota(i32, (per, tm), 0).astype(f32)
    gs = []
    for g in range(G):
        xg = biased[g * per:(g + 1) * per, :]
        m1, i1 = _first_max(xg, iota_p, per)
        m2 = jnp.max(jnp.where(iota_p == i1, neg, xg), axis=0, keepdims=True)
        gs.append(m1 + m2)
    gsc = jnp.concatenate(gs, axis=0)
    iota_g = lax.broadcasted_iota(i32, (G, tm), 0).astype(f32)
    gsel = jnp.zeros((G, tm), f32)
    for _ in range(TOPK_GROUPS):
        _, gi = _first_max(gsc, iota_g, G)
        hit = iota_g == gi
        gsel = jnp.where(hit, 1.0, gsel)
        gsc = jnp.where(hit, neg, gsc)
    masked = jnp.concatenate(
        [jnp.where(gsel[g:g + 1, :] > 0.0, biased[g * per:(g + 1) * per, :], neg) for g in range(G)], axis=0)

    iota_e = lax.broadcasted_iota(i32, (E, tm), 0).astype(f32)
    sel = jnp.zeros((E, tm), f32)
    idxs, vals = [], []
    for _ in range(TOP_K):
        _, ei = _first_max(masked, iota_e, E)
        hit = iota_e == ei
        idxs.append(ei)
        vals.append(jnp.sum(jnp.where(hit, scores, 0.0), axis=0, keepdims=True))
        sel = jnp.where(hit, 1.0, sel)
        masked = jnp.where(hit, neg, masked)
    tot = vals[0]
    for vv in vals[1:]:
        tot = tot + vv
    eidx_o[...] = jnp.concatenate(idxs, axis=0).astype(i32)
    gate_o[...] = jnp.concatenate([vv / tot * ROUTED_SCALE for vv in vals], axis=0)

    rr = lax.broadcasted_iota(i32, (tm, tm), 0)
    cc = lax.broadcasted_iota(i32, (tm, tm), 1)
    before = (rr < cc).astype(bf16)
    pos = _dot(sel.astype(bf16), before)
    rank_o[...] = jnp.concatenate(
        [jnp.sum(jnp.where(iota_e == ei, pos, 0.0), axis=0, keepdims=True) for ei in idxs], axis=0)
    lane = lax.broadcasted_iota(i32, cnt_ref.shape, 1)
    cnt_ref[...] = jnp.where(lane == pl.program_id(0), jnp.sum(sel, axis=1, keepdims=True), cnt_ref[...])
    cnt_o[...] = cnt_ref[...].astype(i32)


def _router(logits_t, bias, tm):
    E, T = logits_t.shape
    assert T // tm <= LANES
    k_tile = pl.BlockSpec((TOP_K, tm), lambda i: (0, i))
    return pl.pallas_call(
        _router_kernel,
        grid=(T // tm,),
        in_specs=[pl.BlockSpec((E, tm), lambda i: (0, i)),
                  pl.BlockSpec((E, 1), lambda i: (0, 0))],
        out_specs=[k_tile, k_tile, k_tile, pl.BlockSpec((E, LANES), lambda i: (0, 0))],
        out_shape=[jax.ShapeDtypeStruct((TOP_K, T), i32), jax.ShapeDtypeStruct((TOP_K, T), f32),
                   jax.ShapeDtypeStruct((TOP_K, T), f32), jax.ShapeDtypeStruct((E, LANES), i32)],
        scratch_shapes=[pltpu.VMEM((E, LANES), f32)],
        compiler_params=_cparams(("arbitrary",)),
        name="router",
    )(logits_t, bias.reshape(E, 1))


def _row_copy(src_ref, s, dst_ref, d, sem):
    return pltpu.make_async_copy(src_ref.at[pl.ds(s, 1), :], dst_ref.at[pl.ds(d, 1), :], sem)


def _zero_fill(cnt_ref, pstart_ref, nused_ref, z_ref, xs_out, sem, blk, nblk, start):
    def act(cp):
        if start:
            cp.start()
        else:
            cp.wait()

    def per_expert(e, carry):
        c = cnt_ref[e]
        base = pstart_ref[e] + c
        npad = (blk - (c & (blk - 1))) & (blk - 1)
        head = (-base) & (SUBLANES - 1)

        def one_row(j, carry2):
            act(_row_copy(z_ref, 0, xs_out, base + j, sem))
            return carry2

        lax.fori_loop(0, head, one_row, 0)
        rem = npad - head
        aligned = base + head
        p = blk // 2
        while p >= SUBLANES:
            off = pl.multiple_of(aligned + (rem & ~(2 * p - 1)), SUBLANES)

            @pl.when((rem & p) != 0)
            def _(p=p, off=off):
                act(pltpu.make_async_copy(z_ref.at[pl.ds(0, p), :], xs_out.at[pl.ds(off, p), :], sem))

            p //= 2
        return carry

    lax.fori_loop(0, N_EXPERTS, per_expert, 0)

    def per_block(b, carry):
        act(pltpu.make_async_copy(z_ref, xs_out.at[pl.ds(pl.multiple_of(b * blk, blk), blk), :], sem))
        return carry

    lax.fori_loop(nused_ref[0], nblk, per_block, 0)


def _piece_sizes(tm):
    return [tm >> s for s in range(tm.bit_length()) if (tm >> s) >= SUBLANES]


def _seg_copies(psrc_ref, pdst_ref, pcnt_ref, tile, buf_ref, hbm_ref, sem, tm, to_hbm):
    ntiles = pl.num_programs(0)
    for s, p in enumerate(_piece_sizes(tm)):
        base = (s * ntiles + tile) * N_EXPERTS

        def one(j, carry, p=p, base=base):
            v = buf_ref.at[pl.ds(pl.multiple_of(psrc_ref[base + j], SUBLANES), p), :]
            h = hbm_ref.at[pl.ds(pl.multiple_of(pdst_ref[base + j], SUBLANES), p), :]
            (pltpu.make_async_copy(v, h, sem) if to_hbm else pltpu.make_async_copy(h, v, sem)).start()
            return carry

        lax.fori_loop(0, pcnt_ref[s * ntiles + tile], one, 0)


def _seg_wait(total_rows, buf_ref, hbm_ref, sem, to_hbm):
    p = 1 << (buf_ref.shape[0].bit_length() - 1)
    while p >= SUBLANES:
        @pl.when((total_rows & p) != 0)
        def _(p=p):
            v = buf_ref.at[pl.ds(0, p), :]
            h = hbm_ref.at[pl.ds(0, p), :]
            (pltpu.make_async_copy(v, h, sem) if to_hbm else pltpu.make_async_copy(h, v, sem)).wait()

        p //= 2


SORT_CHUNK = 512


def _dispatch_kernel(ssrc_ref, sdst_ref, sn_ref, stot_ref, cnt_ref, pstart_ref, nused_ref, lp_ref, h_ref,
                     wsg_ref, wsu_ref, wsd_ref, xs_out, ysh_o, sb_ref, z_ref, sems, sem_z, *, blk, nblk):
    i = pl.program_id(0)
    last = pl.num_programs(0) - 1
    tm = h_ref.shape[0]
    lmax = sb_ref.shape[1]
    slot = i % 2
    seg = functools.partial(_seg_copies, ssrc_ref, sdst_ref, sn_ref, hbm_ref=xs_out, tm=tm, to_hbm=True)

    def seg_wait(tile, s):
        _seg_wait(stot_ref[tile], sb_ref.at[s], xs_out, sems.at[s], True)

    @pl.when(i >= 2)
    def _():
        seg_wait(i - 2, slot)

    hlo, hhi = _unpack_rows(h_ref[...])
    hb = jnp.concatenate([hlo.astype(bf16), hhi.astype(bf16)], axis=1)
    lpv = lp_ref[...]
    half = hb.shape[1] // 2
    for c in range(lmax // SORT_CHUNK):
        jj = (lax.broadcasted_iota(i32, (SORT_CHUNK, tm), 0) + c * SORT_CHUNK).astype(f32)
        onehot = jnp.zeros((SORT_CHUNK, tm), f32)
        for k in range(TOP_K):
            onehot = jnp.where(jj == lpv[k:k + 1, :], 1.0, onehot)
        rows = _dot(onehot.astype(bf16), hb)
        lo = lax.shift_right_logical(lax.bitcast_convert_type(rows[:, :half], u32), jnp.uint32(16))
        hi = lax.bitcast_convert_type(rows[:, half:], u32) & jnp.uint32(0xFFFF0000)
        sb_ref[slot, c * SORT_CHUNK:(c + 1) * SORT_CHUNK, :] = lo | hi

    seg(i, buf_ref=sb_ref.at[slot], sem=sems.at[slot])

    gt = _dot(hb, wsg_ref[...])
    up = _dot(hb, wsu_ref[...])
    ysh_o[...] = _dot(((gt * _sigmoid(gt)) * up).astype(bf16), wsd_ref[...]).astype(ysh_o.dtype)

    @pl.when(i == last)
    def _():
        z_ref[...] = jnp.zeros(z_ref.shape, u32)
        _zero_fill(cnt_ref, pstart_ref, nused_ref, z_ref, xs_out, sem_z, blk, nblk, True)
        _zero_fill(cnt_ref, pstart_ref, nused_ref, z_ref, xs_out, sem_z, blk, nblk, False)
        seg_wait(i, slot)

        @pl.when(i >= 1)
        def _():
            seg_wait(i - 1, 1 - slot)


def _dispatch(seg_src, seg_dst, seg_n, seg_tot, counts, pad_start, nused, lp_t, h2p, wsg, wsu, wsd,
              P, blk, tm, lmax):
    T, DW = h2p.shape
    D, DS = wsg.shape
    assert blk & (blk - 1) == 0 and tm & (tm - 1) == 0 and lmax % SORT_CHUNK == 0
    kern = functools.partial(_dispatch_kernel, blk=blk, nblk=P // blk)
    const = lambda shape: pl.BlockSpec(shape, lambda i, *_: (0, 0))
    grid_spec = pltpu.PrefetchScalarGridSpec(
        num_scalar_prefetch=7,
        grid=(T // tm,),
        in_specs=[pl.BlockSpec((TOP_K, tm), lambda i, *_: (0, i)),
                  pl.BlockSpec((tm, DW), lambda i, *_: (i, 0)),
                  const((D, DS)), const((D, DS)), const((DS, D))],
        out_specs=[pl.BlockSpec(memory_space=pl.ANY), pl.BlockSpec((tm, D), lambda i, *_: (i, 0))],
        scratch_shapes=[pltpu.VMEM((2, lmax, DW), u32), pltpu.VMEM((blk, DW), u32),
                        pltpu.SemaphoreType.DMA((2,)), pltpu.SemaphoreType.DMA],
    )
    return pl.pallas_call(
        kern,
        grid_spec=grid_spec,
        out_shape=[jax.ShapeDtypeStruct((P, DW), u32), jax.ShapeDtypeStruct((T, D), bf16)],
        compiler_params=_cparams(("arbitrary",)),
        name="moe_dispatch",
    )(seg_src, seg_dst, seg_n, seg_tot, counts, pad_start, nused, lp_t, h2p, wsg, wsu, wsd)


def _experts_kernel(blk_e_ref, nxt_e_ref, nused_ref, xs_ref, wg_hbm, wu_hbm, wd_hbm, ys_ref,
                    wg_f, wu_f, wd_f, wg_s, wu_s, wd_s, sems, *, layer):
    i = pl.program_id(0)
    e = blk_e_ref[i]
    changed = jnp.logical_or(i == 0, e != blk_e_ref[jnp.maximum(i - 1, 0)])

    def weight_copies(ex):
        return (pltpu.make_async_copy(wg_hbm.at[layer, ex], wg_f, sems.at[0]),
                pltpu.make_async_copy(wu_hbm.at[layer, ex], wu_f, sems.at[1]),
                pltpu.make_async_copy(wd_hbm.at[layer, ex], wd_f, sems.at[2]))

    @pl.when(i == 0)
    def _():
        for cp in weight_copies(e):
            cp.start()

    @pl.when(changed)
    def _():
        for cp in weight_copies(e):
            cp.wait()
        for src, dst in ((wg_f, wg_s), (wu_f, wu_s), (wd_f, wd_s)):
            rows = src.shape[0] // 8
            for c in range(8):
                dst[c * rows:(c + 1) * rows, :] = src[c * rows:(c + 1) * rows, :].astype(bf16)
        nxt = nxt_e_ref[i]

        @pl.when(nxt >= 0)
        def _():
            for cp in weight_copies(nxt):
                cp.start()

    @pl.when(i < nused_ref[0])
    def _():
        lo, hi = _unpack_rows(xs_ref[...])
        x = jnp.concatenate([lo.astype(bf16), hi.astype(bf16)], axis=1)
        gt = _dot(x, wg_s[...])
        up = _dot(x, wu_s[...])
        hmid = (gt * _sigmoid(gt)) * up
        ys_ref[...] = _pack_rows(_dot(hmid.astype(bf16), wd_s[...]))

    @pl.when(i >= nused_ref[0])
    def _():
        ys_ref[...] = jnp.zeros(ys_ref.shape, u32)


def _experts(blk_e, nxt_e, nused, xs, w_gate, w_up, w_down, layer, blk):
    P, DW = xs.shape
    D, DE = w_gate.shape[-2:]
    nblk = P // blk
    row_idx = lambda i, be, nx, nu: (jnp.minimum(i, nu[0] - 1), 0)
    hbm = pl.BlockSpec(memory_space=pl.ANY)
    grid_spec = pltpu.PrefetchScalarGridSpec(
        num_scalar_prefetch=3,
        grid=(nblk,),
        in_specs=[pl.BlockSpec((blk, DW), row_idx), hbm, hbm, hbm],
        out_specs=pl.BlockSpec((blk, DW), lambda i, be, nx, nu: (i, 0)),
        scratch_shapes=[pltpu.VMEM((D, DE), f32), pltpu.VMEM((D, DE), f32), pltpu.VMEM((DE, D), f32),
                        pltpu.VMEM((D, DE), bf16), pltpu.VMEM((D, DE), bf16), pltpu.VMEM((DE, D), bf16),
                        pltpu.SemaphoreType.DMA((3,))],
    )
    return pl.pallas_call(
        functools.partial(_experts_kernel, layer=layer),
        grid_spec=grid_spec,
        out_shape=jax.ShapeDtypeStruct((P, DW), u32),
        compiler_params=_cparams(("arbitrary",)),
        name="moe_experts",
    )(blk_e, nxt_e, nused, xs, w_gate, w_up, w_down)


def _combine_kernel(ssrc_ref, sdst_ref, sn_ref, stot_ref, lp_ref, gate_ref, ys_hbm, ysh_ref, x1_ref, gf_ref,
                    gpost_ref, x2_o, yb_ref, sems, nxt_refs=()):
    i = pl.program_id(0)
    n = pl.num_programs(0)
    tm = x1_ref.shape[0]
    lmax = yb_ref.shape[1]
    slot = i % 2
    seg = functools.partial(_seg_copies, ssrc_ref, sdst_ref, sn_ref, hbm_ref=ys_hbm, tm=tm, to_hbm=False)

    @pl.when(i == 0)
    def _():
        yb_ref[...] = jnp.zeros(yb_ref.shape, u32)
        seg(i, buf_ref=yb_ref.at[slot], sem=sems.at[slot])

    @pl.when(i + 1 < n)
    def _():
        seg(i + 1, buf_ref=yb_ref.at[1 - slot], sem=sems.at[1 - slot])

    _seg_wait(stot_ref[i], yb_ref.at[slot], ys_hbm, sems.at[slot], False)

    lp = lp_ref[...]
    gate = gate_ref[...]
    ysh = ysh_ref[...].astype(f32)
    half = ysh.shape[1] // 2
    lo = ysh[:, :half]
    hi = ysh[:, half:]
    for c in range(lmax // SORT_CHUNK):
        jl = (lax.broadcasted_iota(i32, (tm, SORT_CHUNK), 1) + c * SORT_CHUNK).astype(f32)
        g = jnp.zeros((tm, SORT_CHUNK), f32)
        for k in range(TOP_K):
            g = jnp.where(jl == lp[:, k:k + 1], gate[:, k:k + 1], g)
        gb = g.astype(bf16)
        a, b = _unpack_rows(yb_ref[slot, c * SORT_CHUNK:(c + 1) * SORT_CHUNK, :])
        lo = lo + _dot(gb, a.astype(bf16))
        hi = hi + _dot(gb, b.astype(bf16))
    y = jnp.concatenate([lo, hi], axis=1)
    ms = jnp.mean(y * y, axis=-1, keepdims=True)
    x2 = x1_ref[...] + gf_ref[0] * (y * lax.rsqrt(ms + NORM_EPS) * gpost_ref[...])
    x2_o[...] = x2
    if nxt_refs:
        scn_ref, shn_ref, gn_ref, hn_o = nxt_refs
        ms2 = jnp.mean(x2 * x2, axis=-1, keepdims=True)
        hn = x2 * lax.rsqrt(ms2 + NORM_EPS) * gn_ref[...]
        hn_o[...] = (hn * (1.0 + scn_ref[0]) + shn_ref[0]).astype(hn_o.dtype)


def _combine_kernel_next(ssrc_ref, sdst_ref, sn_ref, stot_ref, lp_ref, gate_ref, ys_hbm, ysh_ref, x1_ref, gf_ref,
                         gpost_ref, scn_ref, shn_ref, gn_ref, x2_o, hn_o, yb_ref, sems):
    _combine_kernel(ssrc_ref, sdst_ref, sn_ref, stot_ref, lp_ref, gate_ref, ys_hbm, ysh_ref, x1_ref, gf_ref,
                    gpost_ref, x2_o, yb_ref, sems, nxt_refs=(scn_ref, shn_ref, gn_ref, hn_o))


def _combine(seg_src, seg_dst, seg_n, seg_tot, lp_tk, gate_tk, ys, ysh, x1, mod3, g_post, S, tm, lmax, nxt=None):
    T, D = x1.shape
    DW = ys.shape[1]
    tpb = S // tm
    tile = pl.BlockSpec((tm, D), lambda i, *_: (i, 0))
    ktile = pl.BlockSpec((tm, TOP_K), lambda i, *_: (i, 0))
    modspec = lambda seg: pl.BlockSpec((1, 1, D), lambda i, *_: ((i // tpb) * N_MOD + seg, 0, 0))
    rowspec = pl.BlockSpec((1, D), lambda i, *_: (0, 0))
    in_specs = [ktile, ktile, pl.BlockSpec(memory_space=pl.ANY), tile, tile, modspec(5), rowspec]
    args = [seg_src, seg_dst, seg_n, seg_tot, lp_tk, gate_tk, ys, ysh, x1, mod3, g_post.reshape(1, D)]
    out_specs, out_shape, kern = tile, jax.ShapeDtypeStruct((T, D), f32), _combine_kernel
    if nxt is not None:
        in_specs += [modspec(1), modspec(0), rowspec]
        args += [nxt[0], nxt[0], nxt[1].reshape(1, D)]
        out_specs, out_shape = [tile, tile], [out_shape, jax.ShapeDtypeStruct((T, D), bf16)]
        kern = _combine_kernel_next
    grid_spec = pltpu.PrefetchScalarGridSpec(
        num_scalar_prefetch=4,
        grid=(T // tm,),
        in_specs=in_specs,
        out_specs=out_specs,
        scratch_shapes=[pltpu.VMEM((2, lmax, DW), u32), pltpu.SemaphoreType.DMA((2,))],
    )
    return pl.pallas_call(
        kern,
        grid_spec=grid_spec,
        out_shape=out_shape,
        compiler_params=_cparams(("arbitrary",)),
        name="moe_combine",
    )(*args)


def _tile(n, pref):
    t = min(n, pref)
    assert n % t == 0, (n, t)
    return t


def _layer(i, x2, h1, mod3, p, wexp, v_first, nxt, B, S, cfg):
    T, D = x2.shape
    H = (D // 2) // ATT_V_DIM
    W = D - D // 2
    att_cols = 2 * H * 2 * ATT_QK_DIM + H * ATT_V_DIM
    lam_init = 0.8 - 0.6 * math.exp(-0.3 * i)

    if h1 is None:
        h1 = _prenorm(x2, mod3, p["g_pre_mix"], S, _tile(S, cfg["tm_norm"]), 1, 0)
    att = _inproj(h1, p["w_in"][:, :att_cols].astype(bf16), bf16, _tile(T, cfg["tm_in"]), cfg["tn_att"])
    w_rwkv = p["w_in"][:, att_cols:].astype(bf16)

    slopes = jnp.broadcast_to(
        (2.0 ** (-ALIBI_MAX_BIAS * jnp.arange(1, H + 1, dtype=f32) / H))[:, None, None], (H, 1, LANES))
    lamp = jnp.stack([p["lam_q1"], p["lam_k1"], p["lam_q2"], p["lam_k2"]])
    o_att = _attention(att, slopes, lamp, p["att_subln_g"], B, S, H, lam_init, _tile(S, cfg["tq"]), cfg["hp"])

    cols = w_rwkv.shape[1]
    zw = jnp.zeros((RWKV_A_RANK, W), f32)
    heads = W // RWKV_HEAD
    ind = (jnp.arange(W)[:, None] // RWKV_HEAD == jnp.arange(LANES)[None, :]).astype(bf16)
    prm = {
        "mu": p["rwkv_mu"].reshape(1, cols), "w0": p["rwkv_w0"].reshape(1, W),
        "w2p": jnp.concatenate([p["rwkv_w2"], zw], axis=0),
        "a0": p["rwkv_a0"].reshape(1, W),
        "a2p": jnp.concatenate([jnp.zeros((RWKV_W_RANK, W), f32), p["rwkv_a2"]], axis=0),
        "g2": p["rwkv_g2"], "k_k": p["rwkv_k_k"].reshape(1, W), "k_a": p["rwkv_k_a"].reshape(1, W),
        "ind": ind, "indt": ind.T,
    }
    if v_first is not None:
        padc = LANES - RWKV_V_RANK
        prm["v0"] = p["rwkv_v0"].reshape(1, W)
        prm["v1p"] = jnp.pad(p["rwkv_v1"], ((0, 0), (0, padc)))
        prm["v2p"] = jnp.pad(p["rwkv_v2"], ((0, padc), (0, 0)))
    r, lw, kh, v, kn, kb, g = _rwkv_prep(h1, w_rwkv, prm, v_first, B, S, W, _tile(S, cfg["tm_prep"]))
    if v_first is None:
        v_first = v
    r2, y0, mmat, g0 = _wkv_intra(r, lw, kh, v, kn, kb, min(cfg["nc"], S // CHUNK), cfg["passes_intra"])
    o_rwkv = _wkv_state(r2, y0, mmat, g0, r, kh, v, g, p["rwkv_lnx_w"].reshape(1, W),
                        p["rwkv_lnx_b"].reshape(1, W), p["rwkv_r_k"].reshape(1, W), B, S,
                        min(cfg["pg"], W // LANES), min(cfg["cb"], S // CHUNK), cfg["passes_state"])
    del heads

    x1, h2, logits_t = _postmix(o_att, o_rwkv, p["w_out"][:D // 2].astype(bf16), p["w_out"][D // 2:].astype(bf16),
                                x2, mod3,
                                p["g_post_mix"], p["g_pre_ffn"], p["w_router"].T, S, _tile(S, cfg["tm_post"]))

    tm_t = _tile(T, cfg["tm_tile"])
    ntiles = T // tm_t
    eidx_t, gate_t, lrank_t, cnt_tbl = _router(logits_t, p["router_bias"], tm_t)
    blk = cfg["blk"]
    run = (cnt_tbl[:, :ntiles].T + SUBLANES - 1) // SUBLANES * SUBLANES
    counts = jnp.sum(run, axis=0)
    padded = (counts + blk - 1) // blk * blk
    pad_end = jnp.cumsum(padded)
    pad_start = pad_end - padded
    e_ids = jnp.arange(N_EXPERTS, dtype=i32)
    tile_off = jnp.cumsum(run, axis=0) - run
    loc_off = jnp.cumsum(run, axis=1) - run
    run_dst = pad_start[None, :] + tile_off
    psz = jnp.asarray(_piece_sizes(tm_t), i32)[:, None, None]
    has = (run[None] & psz) != 0
    before = run[None] & ~(2 * psz - 1)
    slot_of = jnp.cumsum(has, axis=2) - 1
    hit = has[:, :, None, :] & (slot_of[:, :, None, :] == e_ids[None, None, :, None])
    seg_src = jnp.sum(jnp.where(hit, (loc_off[None] + before)[:, :, None, :], 0), axis=-1).reshape(-1).astype(i32)
    seg_dst = jnp.sum(jnp.where(hit, (run_dst[None] + before)[:, :, None, :], 0), axis=-1).reshape(-1).astype(i32)
    seg_n = jnp.sum(has, axis=2).reshape(-1).astype(i32)
    seg_tot = jnp.sum(run, axis=1).astype(i32)
    loc_tok = jnp.repeat(loc_off, tm_t, axis=0)
    lp_t = jnp.sum(jnp.where(eidx_t[:, :, None] == e_ids, loc_tok[None], 0), axis=-1).astype(f32) + lrank_t
    lmax = -(-(tm_t * TOP_K + N_EXPERTS * (SUBLANES - 1)) // SORT_CHUNK) * SORT_CHUNK
    nblk = -(-(T * TOP_K + ntiles * N_EXPERTS * (SUBLANES - 1)) // blk) + N_EXPERTS
    P = nblk * blk
    blk_start = jnp.arange(nblk, dtype=i32) * blk
    nused = (pad_end[-1] // blk).astype(i32).reshape(1)
    blk_pos = jnp.minimum(blk_start, pad_end[-1] - blk)
    blk_e = jnp.minimum(jnp.sum((pad_end[None, :] <= blk_pos[:, None]).astype(i32), axis=1), N_EXPERTS - 1)
    cand = jnp.where(counts > 0, e_ids, N_EXPERTS)
    later = jnp.where(e_ids[None, :] > blk_e[:, None], cand[None, :], N_EXPERTS)
    nxt_e = jnp.min(later, axis=1)
    nxt_e = jnp.where(nxt_e >= N_EXPERTS, -1, nxt_e).astype(i32)

    xs, ysh = _dispatch(seg_src, seg_dst, seg_n, seg_tot, counts.astype(i32), pad_start.astype(i32), nused, lp_t, h2,
                        p["w_sh_gate"].astype(bf16), p["w_sh_up"].astype(bf16), p["w_sh_down"].astype(bf16),
                        P, blk, tm_t, lmax)
    ys = _experts(blk_e, nxt_e, nused, xs, wexp[0], wexp[1], wexp[2], i, blk)
    out = _combine(seg_src, seg_dst, seg_n, seg_tot, lp_t.T, gate_t.T, ys, ysh, x1,
                   mod3, p["g_post_ffn"], S, tm_t, lmax, nxt)
    x_out, h_next = (out, None) if nxt is None else out
    return x_out, h_next, v_first


_CFG = dict(tm_norm=512, tm_in=2048, tn_att=1024, tq=512, hp=4, tm_prep=256, nc=16, passes_intra=1, passes_state=1, pg=8, cb=8,
            tm_post=512, blk=256, tm_tile=256)

_LAYER_KEYS = ("g_pre_mix", "g_post_mix", "g_pre_ffn", "g_post_ffn", "w_in", "w_out", "lam_q1", "lam_k1",
               "lam_q2", "lam_k2", "att_subln_g", "rwkv_mu", "rwkv_w0", "rwkv_w2", "rwkv_a0", "rwkv_a2",
               "rwkv_g2", "rwkv_k_k", "rwkv_k_a", "rwkv_r_k", "rwkv_lnx_w", "rwkv_lnx_b", "w_router",
               "router_bias", "w_sh_gate", "w_sh_up", "w_sh_down")


def _forward(x, c, params, cfg):
    B, S, D = x.shape
    L = params["w_in"].shape[0]
    bp = 16
    c_pad = jnp.zeros((bp, D), f32).at[:B].set(c)
    mod = _ada_mod(c_pad, params["w_ada"], params["b_ada"])
    x2 = x.reshape(B * S, D)
    v_first = None
    h1 = None
    for i in range(L):
        p = {k: params[k][i] for k in _LAYER_KEYS}
        if i > 0:
            p["rwkv_v0"] = params["rwkv_v0"][i - 1]
            p["rwkv_v1"] = params["rwkv_v1"][i - 1]
            p["rwkv_v2"] = params["rwkv_v2"][i - 1]
        mod3 = mod[i, :B].reshape(B * N_MOD, 1, D)
        wexp = (params["w_exp_gate"], params["w_exp_up"], params["w_exp_down"])
        nxt = None
        if i + 1 < L:
            nxt = (mod[i + 1, :B].reshape(B * N_MOD, 1, D), params["g_pre_mix"][i + 1])
        x2, h1, v_first = _layer(i, x2, h1, mod3, p, wexp, v_first, nxt, B, S, cfg)
    return x2.reshape(B, S, D)


def kernel(x, c, w_ada, b_ada, g_pre_mix, g_post_mix, g_pre_ffn, g_post_ffn, w_in, w_out, lam_q1, lam_k1, lam_q2, lam_k2, att_subln_g, rwkv_mu, rwkv_w0, rwkv_w2, rwkv_a0, rwkv_a2, rwkv_g2, rwkv_k_k, rwkv_k_a, rwkv_r_k, rwkv_lnx_w, rwkv_lnx_b, rwkv_v0, rwkv_v1, rwkv_v2, w_router, router_bias, w_exp_gate, w_exp_up, w_exp_down, w_sh_gate, w_sh_up, w_sh_down):
    params = dict(w_ada=w_ada, b_ada=b_ada, g_pre_mix=g_pre_mix, g_post_mix=g_post_mix, g_pre_ffn=g_pre_ffn,
                  g_post_ffn=g_post_ffn, w_in=w_in, w_out=w_out, lam_q1=lam_q1, lam_k1=lam_k1, lam_q2=lam_q2,
                  lam_k2=lam_k2, att_subln_g=att_subln_g, rwkv_mu=rwkv_mu, rwkv_w0=rwkv_w0, rwkv_w2=rwkv_w2,
                  rwkv_a0=rwkv_a0, rwkv_a2=rwkv_a2, rwkv_g2=rwkv_g2, rwkv_k_k=rwkv_k_k, rwkv_k_a=rwkv_k_a,
                  rwkv_r_k=rwkv_r_k, rwkv_lnx_w=rwkv_lnx_w, rwkv_lnx_b=rwkv_lnx_b, rwkv_v0=rwkv_v0,
                  rwkv_v1=rwkv_v1, rwkv_v2=rwkv_v2, w_router=w_router, router_bias=router_bias,
                  w_exp_gate=w_exp_gate, w_exp_up=w_exp_up, w_exp_down=w_exp_down, w_sh_gate=w_sh_gate,
                  w_sh_up=w_sh_up, w_sh_down=w_sh_down)
    return _forward(x, c, params, _CFG)
```

```python
import functools
import math

import jax
import jax.numpy as jnp
from jax import lax
from jax.experimental import pallas as pl
from jax.experimental.pallas import tpu as pltpu

f32 = jnp.float32
bf16 = jnp.bfloat16
i32 = jnp.int32
u32 = jnp.uint32

ATT_QK_DIM = 64
ATT_V_DIM = 128
ALIBI_MAX_BIAS = 8.0
ATT_SUBLN_EPS = 1e-5
RWKV_HEAD = 64
RWKV_W_RANK = 64
RWKV_A_RANK = 64
RWKV_G_RANK = 128
RWKV_V_RANK = 32
RWKV_GN_EPS = 64e-5
N_EXPERTS = 64
N_GROUPS = 8
TOPK_GROUPS = 4
TOP_K = 8
ROUTED_SCALE = 2.5
NORM_EPS = 1e-6
N_MOD = 6

LANES = 128
SUBLANES = 8
CHUNK = 64
VMEM_LIMIT = 56 * 1024 * 1024


def _cparams(sem):
    return pltpu.CompilerParams(dimension_semantics=sem, vmem_limit_bytes=VMEM_LIMIT)


def _dot(a, b):
    return jnp.dot(a, b, preferred_element_type=f32)


def _dot_nt(a, b):
    return lax.dot_general(a, b, (((1,), (1,)), ((), ())), preferred_element_type=f32)


def _split2(x):
    hi = x.astype(bf16)
    lo = (x - hi.astype(f32)).astype(bf16)
    return hi, lo


def _mm(a, b, passes=1):
    if passes == 1:
        return _dot(a.astype(bf16), b.astype(bf16))
    ah, al = _split2(a.astype(f32))
    bh, bl = _split2(b.astype(f32))
    return (_dot(al, bh) + _dot(ah, bl)) + _dot(ah, bh)


def _mm_exact_rhs(a, b_bf16):
    ah, al = _split2(a)
    return _dot(al, b_bf16) + _dot(ah, b_bf16)


def _sigmoid(x):
    return 1.0 / (1.0 + jnp.exp(-x))


def _pack_rows(x):
    half = x.shape[1] // 2
    a = x[:, :half].astype(bf16).astype(f32)
    b = x[:, half:].astype(bf16).astype(f32)
    lo = lax.shift_right_logical(lax.bitcast_convert_type(a, u32), jnp.uint32(16))
    hi = lax.bitcast_convert_type(b, u32) & jnp.uint32(0xFFFF0000)
    return lo | hi


def _unpack_rows(w):
    lo = lax.bitcast_convert_type(lax.shift_left(w, jnp.uint32(16)), f32)
    hi = lax.bitcast_convert_type(w & jnp.uint32(0xFFFF0000), f32)
    return lo, hi


def _ada_kernel(c_ref, w_ref, b_ref, o_ref):
    c = c_ref[...]
    cond = (c * _sigmoid(c)).astype(bf16)
    o_ref[0] = _dot(cond, w_ref[0].astype(bf16)) + b_ref[0]


def _ada_mod(c_pad, w_ada, b_ada, tn=1024):
    L, D, N = w_ada.shape
    bp = c_pad.shape[0]
    return pl.pallas_call(
        _ada_kernel,
        grid=(L, N // tn),
        in_specs=[pl.BlockSpec((bp, D), lambda l, j: (0, 0)),
                  pl.BlockSpec((1, D, tn), lambda l, j: (l, 0, j)),
                  pl.BlockSpec((1, 1, tn), lambda l, j: (l, 0, j))],
        out_specs=pl.BlockSpec((1, bp, tn), lambda l, j: (l, 0, j)),
        out_shape=jax.ShapeDtypeStruct((L, bp, N), f32),
        compiler_params=_cparams(("arbitrary", "arbitrary")),
        name="ada_mod",
    )(c_pad, w_ada, b_ada.reshape(L, 1, N))


def _prenorm_kernel(x_ref, sc_ref, sh_ref, g_ref, h_ref):
    x = x_ref[...]
    ms = jnp.mean(x * x, axis=-1, keepdims=True)
    y = x * lax.rsqrt(ms + NORM_EPS) * g_ref[...]
    h_ref[...] = (y * (1.0 + sc_ref[0]) + sh_ref[0]).astype(h_ref.dtype)


def _prenorm(x2, mod3, g, S, tm, seg_sc, seg_sh):
    T, D = x2.shape
    tpb = S // tm
    tile = pl.BlockSpec((tm, D), lambda i: (i, 0))
    return pl.pallas_call(
        _prenorm_kernel,
        grid=(T // tm,),
        in_specs=[tile,
                  pl.BlockSpec((1, 1, D), lambda i: ((i // tpb) * N_MOD + seg_sc, 0, 0)),
                  pl.BlockSpec((1, 1, D), lambda i: ((i // tpb) * N_MOD + seg_sh, 0, 0)),
                  pl.BlockSpec((1, D), lambda i: (0, 0))],
        out_specs=tile,
        out_shape=jax.ShapeDtypeStruct((T, D), bf16),
        compiler_params=_cparams(("arbitrary",)),
        name="prenorm",
    )(x2, mod3, mod3, g.reshape(1, D))


def _inproj_kernel(h_ref, w_ref, o_ref):
    o_ref[...] = _dot(h_ref[...], w_ref[...]).astype(o_ref.dtype)


def _inproj(h, w_bf, out_dtype, tm, tn):
    T, D = h.shape
    N = w_bf.shape[1]
    return pl.pallas_call(
        _inproj_kernel,
        grid=(T // tm, N // tn),
        in_specs=[pl.BlockSpec((tm, D), lambda i, j: (i, 0)),
                  pl.BlockSpec((D, tn), lambda i, j: (0, j))],
        out_specs=pl.BlockSpec((tm, tn), lambda i, j: (i, j)),
        out_shape=jax.ShapeDtypeStruct((T, N), out_dtype),
        compiler_params=_cparams(("arbitrary", "arbitrary")),
        name="inproj",
    )(h, w_bf)


def _attn_kernel(q_ref, k_ref, v_ref, slope_ref, lamp_ref, g_ref, o_ref,
                 q2t_ref, vt_ref, m_ref, l_ref, acc_ref, *, tq, hp, lam_init):
    qi = pl.program_id(2)
    scale = ATT_QK_DIM ** -0.5
    heads = range(hp)
    hl = [slice(h * LANES, (h + 1) * LANES) for h in heads]
    slope = [slope_ref[h][:, 0:1] for h in heads]

    @pl.when(qi == 0)
    def _():
        for h in heads:
            vt_ref[h] = v_ref[:, hl[h]].astype(f32).T.astype(bf16)

    dim = lax.broadcasted_iota(i32, (LANES, 1), 0)
    first = dim < ATT_QK_DIM
    for h in heads:
        qt = (q_ref[:, hl[h]].astype(f32) * scale).T
        q2t_ref[h, :, 0:tq] = jnp.where(first, qt, 0.0).astype(bf16)
        q2t_ref[h, :, tq:2 * tq] = jnp.where(first, 0.0, qt).astype(bf16)
    m_ref[...] = jnp.full(m_ref.shape, -jnp.inf, f32)
    l_ref[...] = jnp.zeros(l_ref.shape, f32)
    acc_ref[...] = jnp.zeros(acc_ref.shape, f32)

    kr = lax.broadcasted_iota(i32, (tq, 2 * tq), 0)
    qc = lax.broadcasted_iota(i32, (tq, 2 * tq), 1)
    causal = jnp.where(qc >= tq, qc - tq, qc) >= kr
    krow = lax.broadcasted_iota(i32, (tq, 1), 0).astype(f32)

    def step(ki, masked):
        start = pl.multiple_of(ki * tq, tq)
        kpos = krow + (ki * tq).astype(f32)
        s = [_dot(k_ref[pl.ds(start, tq), hl[h]], q2t_ref[h]) for h in heads]
        s = [s[h] + slope[h] * kpos for h in heads]
        if masked:
            s = [jnp.where(causal, s[h], -jnp.inf) for h in heads]
        m_prev = [m_ref[h] for h in heads]
        m_new = [jnp.maximum(m_prev[h], jnp.max(s[h], axis=0, keepdims=True)) for h in heads]
        alpha = [jnp.exp(m_prev[h] - m_new[h]) for h in heads]
        p = [jnp.exp(s[h] - m_new[h]) for h in heads]
        for h in heads:
            l_ref[h] = alpha[h] * l_ref[h] + jnp.sum(p[h], axis=0, keepdims=True)
            acc_ref[h] = alpha[h] * acc_ref[h] + _dot(vt_ref[h, :, pl.ds(start, tq)], p[h].astype(bf16))
            m_ref[h] = m_new[h]

    def body(ki, carry):
        step(ki, False)
        return carry

    lax.fori_loop(0, qi, body, 0)
    step(qi, True)

    lp = lamp_ref[...]
    lam = (jnp.exp(jnp.sum(lp[0:1] * lp[1:2], axis=-1, keepdims=True))
           - jnp.exp(jnp.sum(lp[2:3] * lp[3:4], axis=-1, keepdims=True)) + lam_init)
    for h in heads:
        on = acc_ref[h] * (1.0 / l_ref[h])
        o = on[:, 0:tq] - lam * on[:, tq:2 * tq]
        o = o * lax.rsqrt(jnp.mean(o * o, axis=0, keepdims=True) + ATT_SUBLN_EPS)
        o = o * g_ref[...] * (1.0 - lam_init)
        o_ref[:, hl[h]] = o.T.astype(o_ref.dtype)


def _attention(att, slopes, lamp, subln_g, B, S, H, lam_init, tq, hp):
    T = att.shape[0]
    nq = S // tq
    kern = functools.partial(_attn_kernel, tq=tq, hp=hp, lam_init=lam_init)
    hw = hp * LANES
    return pl.pallas_call(
        kern,
        grid=(B, H // hp, nq),
        in_specs=[pl.BlockSpec((tq, hw), lambda b, g, q: (b * nq + q, g)),
                  pl.BlockSpec((S, hw), lambda b, g, q: (b, H // hp + g)),
                  pl.BlockSpec((S, hw), lambda b, g, q: (b, 2 * (H // hp) + g)),
                  pl.BlockSpec((hp, 1, LANES), lambda b, g, q: (g, 0, 0)),
                  pl.BlockSpec((4, ATT_QK_DIM), lambda b, g, q: (0, 0)),
                  pl.BlockSpec((ATT_V_DIM, 1), lambda b, g, q: (0, 0))],
        out_specs=pl.BlockSpec((tq, hw), lambda b, g, q: (b * nq + q, g)),
        out_shape=jax.ShapeDtypeStruct((T, H * ATT_V_DIM), bf16),
        scratch_shapes=[pltpu.VMEM((hp, LANES, 2 * tq), bf16),
                        pltpu.VMEM((hp, LANES, S), bf16),
                        pltpu.VMEM((hp, 1, 2 * tq), f32),
                        pltpu.VMEM((hp, 1, 2 * tq), f32),
                        pltpu.VMEM((hp, LANES, 2 * tq), f32)],
        compiler_params=_cparams(("arbitrary", "arbitrary", "arbitrary")),
        name="diff_attention",
    )(att, att, att, slopes, lamp, subln_g.reshape(ATT_V_DIM, 1))


def _head_sums(x, ind, indt):
    s = _mm_exact_rhs(x, ind)
    return _mm_exact_rhs(s, indt)


def _rwkv_prep_kernel(*refs, W, has_vres, tpb):
    if has_vres:
        (hx_ref, win_ref, mu_ref, w0_ref, w2_ref, a0_ref, a2_ref, g2_ref, kk_ref, ka_ref, ind_ref, indt_ref,
         vf_ref, v0_ref, v1_ref, v2_ref,
         r_o, lw_o, kh_o, v_o, kn_o, kb_o, g_o, carry_ref, feats_ref) = refs
    else:
        (hx_ref, win_ref, mu_ref, w0_ref, w2_ref, a0_ref, a2_ref, g2_ref, kk_ref, ka_ref, ind_ref, indt_ref,
         r_o, lw_o, kh_o, v_o, kn_o, kb_o, g_o, carry_ref, feats_ref) = refs

    i = pl.program_id(0)

    @pl.when(i == 0)
    def _():
        feats_ref[...] = jnp.zeros(feats_ref.shape, f32)
        carry_ref[...] = jnp.zeros(carry_ref.shape, f32)

    h = feats_ref[...]
    nxt = _dot(hx_ref[...], win_ref[...])
    tm = h.shape[0]

    first_in_batch = lax.rem(jnp.maximum(i - 1, 0), tpb) == 0
    carry = jnp.where(first_in_batch, 0.0, carry_ref[...])
    rolled = pltpu.roll(h, 1, axis=0)
    row = lax.broadcasted_iota(i32, (tm, 1), 0)
    prev = jnp.where(row == 0, carry, rolled)
    carry_ref[...] = h[tm - 1:tm, :]
    feats_ref[...] = nxt
    feats = h + (prev - h) * mu_ref[...]

    r = feats[:, 0:W]
    k = feats[:, W:2 * W]
    v = feats[:, 2 * W:3 * W]
    wa = feats[:, 3 * W:3 * W + LANES]
    g_lo = feats[:, 3 * W + LANES:3 * W + 2 * LANES]

    w = w0_ref[...] + _mm(jnp.tanh(wa), w2_ref[...], passes=3)
    lw_o[...] = -math.exp(-0.5) * _sigmoid(w)
    a = _sigmoid(a0_ref[...] + _mm(wa, a2_ref[...], passes=3))
    g_o[...] = _mm(_sigmoid(g_lo), g2_ref[...]).astype(g_o.dtype)

    if has_vres:
        mix = _sigmoid(v0_ref[...] + _mm(_mm(v, v1_ref[...]), v2_ref[...]))
        v = v + (vf_ref[...].astype(f32) - v) * mix

    kk = k * kk_ref[...]
    ss = _head_sums(kk * kk, ind_ref[...], indt_ref[...])
    kk = kk / jnp.maximum(jnp.sqrt(ss), 1e-12)
    r_o[...] = r.astype(r_o.dtype)
    kh_o[...] = (k * (1.0 + (a - 1.0) * ka_ref[...])).astype(kh_o.dtype)
    v_o[...] = v.astype(v_o.dtype)
    kn_o[...] = kk.astype(kn_o.dtype)
    kb_o[...] = (kk * a).astype(kb_o.dtype)


def _rwkv_prep(hx, w_rwkv, prm, vfirst, B, S, W, tm):
    T, D = hx.shape
    COLS = w_rwkv.shape[1]
    tpb = S // tm
    n = T // tm
    has_vres = vfirst is not None
    row = lambda c: pl.BlockSpec((1, c), lambda i: (0, 0))
    full = lambda a: pl.BlockSpec(a.shape, lambda i: (0, 0))
    tile = pl.BlockSpec((tm, W), lambda i: (jnp.maximum(i - 1, 0), 0))
    args = [hx, w_rwkv, prm["mu"], prm["w0"], prm["w2p"], prm["a0"], prm["a2p"], prm["g2"], prm["k_k"], prm["k_a"],
            prm["ind"], prm["indt"]]
    specs = [pl.BlockSpec((tm, D), lambda i: (jnp.minimum(i, n - 1), 0)),
             pl.BlockSpec((D, COLS), lambda i: (0, 0), pipeline_mode=pl.Buffered(1)),
             row(COLS), row(W), full(prm["w2p"]),
             row(W), full(prm["a2p"]), full(prm["g2"]), row(W), row(W), full(prm["ind"]), full(prm["indt"])]
    if has_vres:
        args += [vfirst, prm["v0"], prm["v1p"], prm["v2p"]]
        specs += [tile, row(W), full(prm["v1p"]), full(prm["v2p"])]
    out_dtypes = (bf16, f32, bf16, bf16, bf16, bf16, bf16)
    kern = functools.partial(_rwkv_prep_kernel, W=W, has_vres=has_vres, tpb=tpb)
    return pl.pallas_call(
        kern,
        grid=(n + 1,),
        in_specs=specs,
        out_specs=[tile] * 7,
        out_shape=[jax.ShapeDtypeStruct((T, W), dt) for dt in out_dtypes],
        scratch_shapes=[pltpu.VMEM((1, COLS), f32), pltpu.VMEM((tm, COLS), f32)],
        compiler_params=_cparams(("arbitrary",)),
        name="rwkv_prep",
    )(*args)


def _wkv_chunks(rs, lws, ks, vs, kns, kbs, passes):
    C = CHUNK
    P2 = 2 * C
    n = range(len(rs))
    ri = lax.broadcasted_iota(i32, (C, C), 0)
    ci = lax.broadcasted_iota(i32, (C, C), 1)
    tri = (ci <= ri).astype(bf16)
    lane = lax.broadcasted_iota(i32, (1, LANES), 1)
    m0 = (lane < RWKV_HEAD).astype(f32)
    m1 = 1.0 - m0
    rr = lax.broadcasted_iota(i32, (P2, P2), 0)
    cc = lax.broadcasted_iota(i32, (P2, P2), 1)
    same = jnp.where(rr >= C, 1, 0) == jnp.where(cc >= C, 1, 0)
    strict = same & (cc < rr)
    incl = same & (cc <= rr)
    incl2 = jnp.concatenate([incl, incl], axis=1)
    eye = (rr == cc).astype(f32)
    zeros_p = jnp.zeros((P2, LANES), f32)
    zeros_c = jnp.zeros((C, LANES), f32)
    stack = lambda x: jnp.concatenate([x * m0, x * m1], axis=0)
    fold = lambda x: x[0:C] + x[C:2 * C]

    def cumsum(lw):
        h1 = lw.astype(bf16)
        r1 = lw - h1.astype(f32)
        h2 = r1.astype(bf16)
        h3 = (r1 - h2.astype(f32)).astype(bf16)
        return (_dot(tri, h3) + _dot(tri, h2)) + _dot(tri, h1)

    cum = [cumsum(lws[j]) for j in n]
    cum_c = [cum[j][C - 1:C, :] for j in n]
    at = [-kns[j] * jnp.exp(cum[j] - lws[j]) for j in n]
    rt = [rs[j] * jnp.exp(cum[j]) for j in n]
    einv = [jnp.exp(-cum[j]) for j in n]
    bt = [kbs[j] * einv[j] for j in n]
    kt = [ks[j] * einv[j] for j in n]
    eh = [jnp.exp(cum_c[j] - cum[j]) for j in n]
    bh = [kbs[j] * eh[j] for j in n]
    kh = [ks[j] * eh[j] for j in n]
    w_c = [jnp.exp(cum_c[j]) for j in n]
    abd = [stack(at[j]) for j in n]
    vst = [stack(vs[j]) for j in n]
    lhs = [jnp.concatenate([abd[j], stack(rt[j])], axis=0) for j in n]
    rhs = [jnp.concatenate([stack(bt[j]), stack(kt[j])], axis=0) for j in n]
    gram = [_mm_nt(lhs[j], rhs[j], passes) for j in n]
    lab = [jnp.where(strict, gram[j][0:P2, 0:P2], 0.0) for j in n]
    lak = [jnp.where(strict, gram[j][0:P2, P2:2 * P2], 0.0) for j in n]
    mrbk = [jnp.where(incl2, gram[j][P2:2 * P2, :], 0.0) for j in n]

    x0 = [_mm(lak[j], vst[j], passes) for j in n]
    tinv = [eye + lab[j] for j in n]
    lp = [_mm(lab[j], lab[j], passes) for j in n]
    n_sq = int(math.log2(C)) - 1
    for it in range(n_sq):
        if it < n_sq - 1:
            both = [_mm(lp[j], jnp.concatenate([lp[j], tinv[j]], axis=1), passes) for j in n]
            tinv = [tinv[j] + both[j][:, P2:2 * P2] for j in n]
            lp = [both[j][:, 0:P2] for j in n]
        else:
            tinv = [tinv[j] + _mm(lp[j], tinv[j], passes) for j in n]

    ta = [_mm(tinv[j], jnp.concatenate([abd[j], x0[j]], axis=1), passes) for j in n]
    rhs2 = [jnp.concatenate([ta[j], jnp.concatenate([zeros_p, vst[j]], axis=1)], axis=0) for j in n]
    z = [_mm(mrbk[j], rhs2[j], passes) for j in n]
    r2 = [rt[j] + fold(z[j][:, 0:LANES]) for j in n]
    y0 = [fold(z[j][:, LANES:2 * LANES]) for j in n]
    lhs3t = [jnp.concatenate([bh[j], kh[j]], axis=0).T for j in n]
    rhs3 = [jnp.concatenate([fold(ta[j]), jnp.concatenate([zeros_c, vs[j]], axis=1)], axis=0) for j in n]
    wmat = [_mm(lhs3t[j], rhs3[j], passes) for j in n]
    mmat = [jnp.where(same, wmat[j][:, 0:LANES], 0.0) + eye * w_c[j] for j in n]
    g0 = [jnp.where(same, wmat[j][:, LANES:2 * LANES], 0.0) for j in n]
    return r2, y0, mmat, g0


def _mm_nt(a, b, passes):
    if passes == 1:
        return _dot_nt(a.astype(bf16), b.astype(bf16))
    ah, al = _split2(a)
    bh, bl = _split2(b)
    return (_dot_nt(al, bh) + _dot_nt(ah, bl)) + _dot_nt(ah, bh)


def _wkv_intra_kernel(r_ref, lw_ref, k_ref, v_ref, kn_ref, kb_ref, r2_o, y0_o, m_o, g_o, *, nc, passes):
    C = CHUNK
    sls = [slice(c * C, (c + 1) * C) for c in range(nc)]
    take = lambda ref: [ref[sl, :].astype(f32) for sl in sls]
    r2, y0, mmat, g0 = _wkv_chunks(take(r_ref), take(lw_ref), take(k_ref), take(v_ref),
                                   take(kn_ref), take(kb_ref), passes)
    for c, sl in enumerate(sls):
        r2_o[sl, :] = r2[c].astype(r2_o.dtype)
        y0_o[sl, :] = y0[c].astype(y0_o.dtype)
        m_o[0, c] = mmat[c].astype(m_o.dtype)
        g_o[0, c] = g0[c]


def _wkv_intra(r, lw, kh, v, kn, kb, nc, passes):
    T, W = r.shape
    npair = W // LANES
    rows = nc * CHUNK
    tile = pl.BlockSpec((rows, LANES), lambda p, i: (i, p))
    mat = pl.BlockSpec((1, nc, LANES, LANES), lambda p, i: (p, i, 0, 0))
    kern = functools.partial(_wkv_intra_kernel, nc=nc, passes=passes)
    return pl.pallas_call(
        kern,
        grid=(npair, T // rows),
        in_specs=[tile] * 6,
        out_specs=[tile, tile, mat, mat],
        out_shape=[jax.ShapeDtypeStruct((T, W), bf16), jax.ShapeDtypeStruct((T, W), bf16),
                   jax.ShapeDtypeStruct((npair, T // CHUNK, LANES, LANES), bf16),
                   jax.ShapeDtypeStruct((npair, T // CHUNK, LANES, LANES), f32)],
        compiler_params=_cparams(("arbitrary", "arbitrary")),
        name="wkv_intra",
    )(r, lw, kh, v, kn, kb)


def _wkv_state_kernel(r2_ref, y0_ref, m_ref, g0_ref, r_ref, kh_ref, v_ref, g_ref,
                      lnw_ref, lnb_ref, rk_ref, o_ref, st_ref, y_ref, *, pg, cb, passes):
    C = CHUNK

    @pl.when(pl.program_id(2) == 0)
    def _():
        st_ref[...] = jnp.zeros(st_ref.shape, f32)

    pairs = range(pg)
    lanes = [slice(p * LANES, (p + 1) * LANES) for p in pairs]
    st = [st_ref[p] for p in pairs]
    for c in range(cb):
        rows = slice(c * C, (c + 1) * C)
        for p in pairs:
            y_ref[rows, lanes[p]] = (_mm(r2_ref[rows, lanes[p]], st[p], passes)
                                     + y0_ref[rows, lanes[p]].astype(f32))
        st = [_mm(m_ref[p, c], st[p], passes) + g0_ref[p, c] for p in pairs]
    for p in pairs:
        st_ref[p] = st[p]

    rr = lax.broadcasted_iota(i32, (LANES, LANES), 0)
    cc = lax.broadcasted_iota(i32, (LANES, LANES), 1)
    ones_bd = (jnp.where(rr >= RWKV_HEAD, 1, 0) == jnp.where(cc >= RWKV_HEAD, 1, 0)).astype(bf16)
    for p in pairs:
        y = y_ref[:, lanes[p]]
        mu = _mm_exact_rhs(y, ones_bd) * (1.0 / RWKV_HEAD)
        d = y - mu
        var = _mm_exact_rhs(d * d, ones_bd) * (1.0 / RWKV_HEAD)
        yn = d * lax.rsqrt(var + RWKV_GN_EPS) * lnw_ref[:, lanes[p]] + lnb_ref[:, lanes[p]]
        rk = r_ref[:, lanes[p]].astype(f32) * kh_ref[:, lanes[p]].astype(f32) * rk_ref[:, lanes[p]]
        bonus = _mm_exact_rhs(rk, ones_bd) * v_ref[:, lanes[p]]
        o_ref[:, lanes[p]] = ((yn + bonus) * g_ref[:, lanes[p]]).astype(o_ref.dtype)


def _wkv_state(r2, y0, mm, g0, r, kh, v, g, lnw, lnb, rk, B, S, pg, cb, passes):
    T, W = r.shape
    npair = W // LANES
    rows = cb * CHUNK
    steps = S // rows
    seq = pl.BlockSpec((rows, pg * LANES), lambda b, q, c: (b * steps + c, q))
    mat = pl.BlockSpec((pg, cb, LANES, LANES), lambda b, q, c: (q, b * steps + c, 0, 0))
    prow = pl.BlockSpec((1, pg * LANES), lambda b, q, c: (0, q))
    kern = functools.partial(_wkv_state_kernel, pg=pg, cb=cb, passes=passes)
    return pl.pallas_call(
        kern,
        grid=(B, npair // pg, steps),
        in_specs=[seq, seq, mat, mat, seq, seq, seq, seq, prow, prow, prow],
        out_specs=seq,
        out_shape=jax.ShapeDtypeStruct((T, W), bf16),
        scratch_shapes=[pltpu.VMEM((pg, LANES, LANES), f32), pltpu.VMEM((rows, pg * LANES), f32)],
        compiler_params=_cparams(("arbitrary", "arbitrary", "arbitrary")),
        name="wkv_state",
    )(r2, y0, mm, g0, r, kh, v, g, lnw, lnb, rk)


def _postmix_kernel(oa_ref, orw_ref, wa_ref, wr_ref, x_ref, ga_ref, gpost_ref, gpre_ref, sc_ref, sh_ref, wrt_ref,
                    x1_o, h2_o, lg_o):
    mixed = _dot(oa_ref[...], wa_ref[...]) + _dot(orw_ref[...], wr_ref[...])
    ms = jnp.mean(mixed * mixed, axis=-1, keepdims=True)
    x1 = x_ref[...] + ga_ref[0] * (mixed * lax.rsqrt(ms + NORM_EPS) * gpost_ref[...])
    x1_o[...] = x1
    ms1 = jnp.mean(x1 * x1, axis=-1, keepdims=True)
    h2 = (x1 * lax.rsqrt(ms1 + NORM_EPS) * gpre_ref[...]) * (1.0 + sc_ref[0]) + sh_ref[0]
    h2_o[...] = _pack_rows(h2)
    wh, wl = _split2(wrt_ref[...])
    hh, hl = _split2(h2)
    e = wh.shape[0]
    top = _dot_nt(jnp.concatenate([wh, wl], axis=0), hh)
    lg_o[...] = (top[e:2 * e] + _dot_nt(wh, hl)) + top[0:e]


def _postmix(o_att, o_rwkv, w_out_a, w_out_r, x2, mod3, g_post, g_pre, w_rt, S, tm):
    T, D = x2.shape
    WA = o_att.shape[1]
    WR = o_rwkv.shape[1]
    E = w_rt.shape[0]
    tpb = S // tm
    modspec = lambda seg: pl.BlockSpec((1, 1, D), lambda i: ((i // tpb) * N_MOD + seg, 0, 0))
    tile = pl.BlockSpec((tm, D), lambda i: (i, 0))
    return pl.pallas_call(
        _postmix_kernel,
        grid=(T // tm,),
        in_specs=[pl.BlockSpec((tm, WA), lambda i: (i, 0)),
                  pl.BlockSpec((tm, WR), lambda i: (i, 0)),
                  pl.BlockSpec((WA, D), lambda i: (0, 0), pipeline_mode=pl.Buffered(1)),
                  pl.BlockSpec((WR, D), lambda i: (0, 0), pipeline_mode=pl.Buffered(1)),
                  tile, modspec(2),
                  pl.BlockSpec((1, D), lambda i: (0, 0)),
                  pl.BlockSpec((1, D), lambda i: (0, 0)),
                  modspec(4), modspec(3),
                  pl.BlockSpec((E, D), lambda i: (0, 0))],
        out_specs=[tile, pl.BlockSpec((tm, D // 2), lambda i: (i, 0)),
                   pl.BlockSpec((E, tm), lambda i: (0, i))],
        out_shape=[jax.ShapeDtypeStruct((T, D), f32), jax.ShapeDtypeStruct((T, D // 2), u32),
                   jax.ShapeDtypeStruct((E, T), f32)],
        compiler_params=_cparams(("arbitrary",)),
        name="postmix",
    )(o_att, o_rwkv, w_out_a, w_out_r, x2, mod3, g_post.reshape(1, D), g_pre.reshape(1, D), mod3, mod3, w_rt)


def _first_max(x, iota, n):
    mx = jnp.max(x, axis=0, keepdims=True)
    idx = jnp.min(jnp.where(x == mx, iota, n), axis=0, keepdims=True)
    return mx, idx


def _router_kernel(lg_ref, bias_ref, eidx_o, gate_o, rank_o, cnt_o, cnt_ref):
    E = N_EXPERTS
    G = N_GROUPS
    per = E // G
    tm = lg_ref.shape[1]

    @pl.when(pl.program_id(0) == 0)
    def _():
        cnt_ref[...] = jnp.zeros(cnt_ref.shape, f32)

    scores = _sigmoid(lg_ref[...])
    biased = scores + bias_ref[...]
    neg = -jnp.inf

    iota_p = lax.broadcasted_iota(i32, (per, tm), 0).astype(f32)
    gs = []
    for g in range(G):
        xg = biased[g * per:(g + 1) * per, :]
        m1, i1 = _first_max(xg, iota_p, per)
        m2 = jnp.max(jnp.where(iota_p == i1, neg, xg), axis=0, keepdims=True)
        gs.append(m1 + m2)
    gsc = jnp.concatenate(gs, axis=0)
    iota_g = lax.broadcasted_iota(i32, (G, tm), 0).astype(f32)
    gsel = jnp.zeros((G, tm), f32)
    for _ in range(TOPK_GROUPS):
        _, gi = _first_max(gsc, iota_g, G)
        hit = iota_g == gi
        gsel = jnp.where(hit, 1.0, gsel)
        gsc = jnp.where(hit, neg, gsc)
    masked = jnp.concatenate(
        [jnp.where(gsel[g:g + 1, :] > 0.0, biased[g * per:(g + 1) * per, :], neg) for g in range(G)], axis=0)

    iota_e = lax.broadcasted_iota(i32, (E, tm), 0).astype(f32)
    sel = jnp.zeros((E, tm), f32)
    idxs, vals = [], []
    for _ in range(TOP_K):
        _, ei = _first_max(masked, iota_e, E)
        hit = iota_e == ei
        idxs.append(ei)
        vals.append(jnp.sum(jnp.where(hit, scores, 0.0), axis=0, keepdims=True))
        sel = jnp.where(hit, 1.0, sel)
        masked = jnp.where(hit, neg, masked)
    tot = vals[0]
    for vv in vals[1:]:
        tot = tot + vv
    eidx_o[...] = jnp.concatenate(idxs, axis=0).astype(i32)
    gate_o[...] = jnp.concatenate([vv / tot * ROUTED_SCALE for vv in vals], axis=0)

    rr = lax.broadcasted_iota(i32, (tm, tm), 0)
    cc = lax.broadcasted_iota(i32, (tm, tm), 1)
    before = (rr < cc).astype(bf16)
    pos = _dot(sel.astype(bf16), before)
    rank_o[...] = jnp.concatenate(
        [jnp.sum(jnp.where(iota_e == ei, pos, 0.0), axis=0, keepdims=True) for ei in idxs], axis=0)
    lane = lax.broadcasted_iota(i32, cnt_ref.shape, 1)
    cnt_ref[...] = jnp.where(lane == pl.program_id(0), jnp.sum(sel, axis=1, keepdims=True), cnt_ref[...])
    cnt_o[...] = cnt_ref[...].astype(i32)


def _router(logits_t, bias, tm):
    E, T = logits_t.shape
    assert T // tm <= LANES
    k_tile = pl.BlockSpec((TOP_K, tm), lambda i: (0, i))
    return pl.pallas_call(
        _router_kernel,
        grid=(T // tm,),
        in_specs=[pl.BlockSpec((E, tm), lambda i: (0, i)),
                  pl.BlockSpec((E, 1), lambda i: (0, 0))],
        out_specs=[k_tile, k_tile, k_tile, pl.BlockSpec((E, LANES), lambda i: (0, 0))],
        out_shape=[jax.ShapeDtypeStruct((TOP_K, T), i32), jax.ShapeDtypeStruct((TOP_K, T), f32),
                   jax.ShapeDtypeStruct((TOP_K, T), f32), jax.ShapeDtypeStruct((E, LANES), i32)],
        scratch_shapes=[pltpu.VMEM((E, LANES), f32)],
        compiler_params=_cparams(("arbitrary",)),
        name="router",
    )(logits_t, bias.reshape(E, 1))


def _row_copy(src_ref, s, dst_ref, d, sem):
    return pltpu.make_async_copy(src_ref.at[pl.ds(s, 1), :], dst_ref.at[pl.ds(d, 1), :], sem)


def _zero_fill(cnt_ref, pstart_ref, nused_ref, z_ref, xs_out, sem, blk, nblk, start):
    def act(cp):
        if start:
            cp.start()
        else:
            cp.wait()

    def per_expert(e, carry):
        c = cnt_ref[e]
        base = pstart_ref[e] + c
        npad = (blk - (c & (blk - 1))) & (blk - 1)
        head = (-base) & (SUBLANES - 1)

        def one_row(j, carry2):
            act(_row_copy(z_ref, 0, xs_out, base + j, sem))
            return carry2

        lax.fori_loop(0, head, one_row, 0)
        rem = npad - head
        aligned = base + head
        p = blk // 2
        while p >= SUBLANES:
            off = pl.multiple_of(aligned + (rem & ~(2 * p - 1)), SUBLANES)

            @pl.when((rem & p) != 0)
            def _(p=p, off=off):
                act(pltpu.make_async_copy(z_ref.at[pl.ds(0, p), :], xs_out.at[pl.ds(off, p), :], sem))

            p //= 2
        return carry

    lax.fori_loop(0, N_EXPERTS, per_expert, 0)

    def per_block(b, carry):
        act(pltpu.make_async_copy(z_ref, xs_out.at[pl.ds(pl.multiple_of(b * blk, blk), blk), :], sem))
        return carry

    lax.fori_loop(nused_ref[0], nblk, per_block, 0)


def _piece_sizes(tm):
    return [tm >> s for s in range(tm.bit_length()) if (tm >> s) >= SUBLANES]


def _seg_copies(psrc_ref, pdst_ref, pcnt_ref, tile, buf_ref, hbm_ref, sem, tm, to_hbm):
    ntiles = pl.num_programs(0)
    for s, p in enumerate(_piece_sizes(tm)):
        base = (s * ntiles + tile) * N_EXPERTS

        def one(j, carry, p=p, base=base):
            v = buf_ref.at[pl.ds(pl.multiple_of(psrc_ref[base + j], SUBLANES), p), :]
            h = hbm_ref.at[pl.ds(pl.multiple_of(pdst_ref[base + j], SUBLANES), p), :]
            (pltpu.make_async_copy(v, h, sem) if to_hbm else pltpu.make_async_copy(h, v, sem)).start()
            return carry

        lax.fori_loop(0, pcnt_ref[s * ntiles + tile], one, 0)


def _seg_wait(total_rows, buf_ref, hbm_ref, sem, to_hbm):
    p = 1 << (buf_ref.shape[0].bit_length() - 1)
    while p >= SUBLANES:
        @pl.when((total_rows & p) != 0)
        def _(p=p):
            v = buf_ref.at[pl.ds(0, p), :]
            h = hbm_ref.at[pl.ds(0, p), :]
            (pltpu.make_async_copy(v, h, sem) if to_hbm else pltpu.make_async_copy(h, v, sem)).wait()

        p //= 2


SORT_CHUNK = 512


def _dispatch_kernel(ssrc_ref, sdst_ref, sn_ref, stot_ref, cnt_ref, pstart_ref, nused_ref, lp_ref, h_ref,
                     wsg_ref, wsu_ref, wsd_ref, xs_out, ysh_o, sb_ref, z_ref, sems, sem_z, *, blk, nblk):
    i = pl.program_id(0)
    last = pl.num_programs(0) - 1
    tm = h_ref.shape[0]
    lmax = sb_ref.shape[1]
    slot = i % 2
    seg = functools.partial(_seg_copies, ssrc_ref, sdst_ref, sn_ref, hbm_ref=xs_out, tm=tm, to_hbm=True)

    def seg_wait(tile, s):
        _seg_wait(stot_ref[tile], sb_ref.at[s], xs_out, sems.at[s], True)

    @pl.when(i >= 2)
    def _():
        seg_wait(i - 2, slot)

    hlo, hhi = _unpack_rows(h_ref[...])
    hb = jnp.concatenate([hlo.astype(bf16), hhi.astype(bf16)], axis=1)
    lpv = lp_ref[...]
    half = hb.shape[1] // 2
    for c in range(lmax // SORT_CHUNK):
        jj = (lax.broadcasted_iota(i32, (SORT_CHUNK, tm), 0) + c * SORT_CHUNK).astype(f32)
        onehot = jnp.zeros((SORT_CHUNK, tm), f32)
        for k in range(TOP_K):
            onehot = jnp.where(jj == lpv[k:k + 1, :], 1.0, onehot)
        rows = _dot(onehot.astype(bf16), hb)
        lo = lax.shift_right_logical(lax.bitcast_convert_type(rows[:, :half], u32), jnp.uint32(16))
        hi = lax.bitcast_convert_type(rows[:, half:], u32) & jnp.uint32(0xFFFF0000)
        sb_ref[slot, c * SORT_CHUNK:(c + 1) * SORT_CHUNK, :] = lo | hi

    seg(i, buf_ref=sb_ref.at[slot], sem=sems.at[slot])

    gt = _dot(hb, wsg_ref[...])
    up = _dot(hb, wsu_ref[...])
    ysh_o[...] = _dot(((gt * _sigmoid(gt)) * up).astype(bf16), wsd_ref[...]).astype(ysh_o.dtype)

    @pl.when(i == last)
    def _():
        z_ref[...] = jnp.zeros(z_ref.shape, u32)
        _zero_fill(cnt_ref, pstart_ref, nused_ref, z_ref, xs_out, sem_z, blk, nblk, True)
        _zero_fill(cnt_ref, pstart_ref, nused_ref, z_ref, xs_out, sem_z, blk, nblk, False)
        seg_wait(i, slot)

        @pl.when(i >= 1)
        def _():
            seg_wait(i - 1, 1 - slot)


def _dispatch(seg_src, seg_dst, seg_n, seg_tot, counts, pad_start, nused, lp_t, h2p, wsg, wsu, wsd,
              P, blk, tm, lmax):
    T, DW = h2p.shape
    D, DS = wsg.shape
    assert blk & (blk - 1) == 0 and tm & (tm - 1) == 0 and lmax % SORT_CHUNK == 0
    kern = functools.partial(_dispatch_kernel, blk=blk, nblk=P // blk)
    const = lambda shape: pl.BlockSpec(shape, lambda i, *_: (0, 0))
    grid_spec = pltpu.PrefetchScalarGridSpec(
        num_scalar_prefetch=7,
        grid=(T // tm,),
        in_specs=[pl.BlockSpec((TOP_K, tm), lambda i, *_: (0, i)),
                  pl.BlockSpec((tm, DW), lambda i, *_: (i, 0)),
                  const((D, DS)), const((D, DS)), const((DS, D))],
        out_specs=[pl.BlockSpec(memory_space=pl.ANY), pl.BlockSpec((tm, D), lambda i, *_: (i, 0))],
        scratch_shapes=[pltpu.VMEM((2, lmax, DW), u32), pltpu.VMEM((blk, DW), u32),
                        pltpu.SemaphoreType.DMA((2,)), pltpu.SemaphoreType.DMA],
    )
    return pl.pallas_call(
        kern,
        grid_spec=grid_spec,
        out_shape=[jax.ShapeDtypeStruct((P, DW), u32), jax.ShapeDtypeStruct((T, D), bf16)],
        compiler_params=_cparams(("arbitrary",)),
        name="moe_dispatch",
    )(seg_src, seg_dst, seg_n, seg_tot, counts, pad_start, nused, lp_t, h2p, wsg, wsu, wsd)


def _experts_kernel(blk_e_ref, nxt_e_ref, nused_ref, xs_ref, wg_hbm, wu_hbm, wd_hbm, ys_ref,
                    wg_f, wu_f, wd_f, wg_s, wu_s, wd_s, sems, *, layer):
    i = pl.program_id(0)
    e = blk_e_ref[i]
    changed = jnp.logical_or(i == 0, e != blk_e_ref[jnp.maximum(i - 1, 0)])

    def weight_copies(ex):
        return (pltpu.make_async_copy(wg_hbm.at[layer, ex], wg_f, sems.at[0]),
                pltpu.make_async_copy(wu_hbm.at[layer, ex], wu_f, sems.at[1]),
                pltpu.make_async_copy(wd_hbm.at[layer, ex], wd_f, sems.at[2]))

    @pl.when(i == 0)
    def _():
        for cp in weight_copies(e):
            cp.start()

    @pl.when(changed)
    def _():
        for cp in weight_copies(e):
            cp.wait()
        for src, dst in ((wg_f, wg_s), (wu_f, wu_s), (wd_f, wd_s)):
            rows = src.shape[0] // 8
            for c in range(8):
                dst[c * rows:(c + 1) * rows, :] = src[c * rows:(c + 1) * rows, :].astype(bf16)
        nxt = nxt_e_ref[i]

        @pl.when(nxt >= 0)
        def _():
            for cp in weight_copies(nxt):
                cp.start(priority=1)

    @pl.when(i < nused_ref[0])
    def _():
        lo, hi = _unpack_rows(xs_ref[...])
        x = jnp.concatenate([lo.astype(bf16), hi.astype(bf16)], axis=1)
        gt = _dot(x, wg_s[...])
        up = _dot(x, wu_s[...])
        hmid = (gt * _sigmoid(gt)) * up
        ys_ref[...] = _pack_rows(_dot(hmid.astype(bf16), wd_s[...]))

    @pl.when(i >= nused_ref[0])
    def _():
        ys_ref[...] = jnp.zeros(ys_ref.shape, u32)


def _experts(blk_e, nxt_e, nused, xs, w_gate, w_up, w_down, layer, blk):
    P, DW = xs.shape
    D, DE = w_gate.shape[-2:]
    nblk = P // blk
    row_idx = lambda i, be, nx, nu: (jnp.minimum(i, nu[0] - 1), 0)
    hbm = pl.BlockSpec(memory_space=pl.ANY)
    grid_spec = pltpu.PrefetchScalarGridSpec(
        num_scalar_prefetch=3,
        grid=(nblk,),
        in_specs=[pl.BlockSpec((blk, DW), row_idx), hbm, hbm, hbm],
        out_specs=pl.BlockSpec((blk, DW), lambda i, be, nx, nu: (i, 0)),
        scratch_shapes=[pltpu.VMEM((D, DE), f32), pltpu.VMEM((D, DE), f32), pltpu.VMEM((DE, D), f32),
                        pltpu.VMEM((D, DE), bf16), pltpu.VMEM((D, DE), bf16), pltpu.VMEM((DE, D), bf16),
                        pltpu.SemaphoreType.DMA((3,))],
    )
    return pl.pallas_call(
        functools.partial(_experts_kernel, layer=layer),
        grid_spec=grid_spec,
        out_shape=jax.ShapeDtypeStruct((P, DW), u32),
        compiler_params=_cparams(("arbitrary",)),
        name="moe_experts",
    )(blk_e, nxt_e, nused, xs, w_gate, w_up, w_down)


def _combine_kernel(ssrc_ref, sdst_ref, sn_ref, stot_ref, lp_ref, gate_ref, ys_hbm, ysh_ref, x1_ref, gf_ref,
                    gpost_ref, x2_o, yb_ref, sems, nxt_refs=()):
    i = pl.program_id(0)
    n = pl.num_programs(0)
    tm = x1_ref.shape[0]
    lmax = yb_ref.shape[1]
    slot = i % 2
    seg = functools.partial(_seg_copies, ssrc_ref, sdst_ref, sn_ref, hbm_ref=ys_hbm, tm=tm, to_hbm=False)

    @pl.when(i == 0)
    def _():
        yb_ref[...] = jnp.zeros(yb_ref.shape, u32)
        seg(i, buf_ref=yb_ref.at[slot], sem=sems.at[slot])

    @pl.when(i + 1 < n)
    def _():
        seg(i + 1, buf_ref=yb_ref.at[1 - slot], sem=sems.at[1 - slot])

    _seg_wait(stot_ref[i], yb_ref.at[slot], ys_hbm, sems.at[slot], False)

    lp = lp_ref[...]
    gate = gate_ref[...]
    ysh = ysh_ref[...].astype(f32)
    half = ysh.shape[1] // 2
    lo = ysh[:, :half]
    hi = ysh[:, half:]
    for c in range(lmax // SORT_CHUNK):
        jl = (lax.broadcasted_iota(i32, (tm, SORT_CHUNK), 1) + c * SORT_CHUNK).astype(f32)
        g = jnp.zeros((tm, SORT_CHUNK), f32)
        for k in range(TOP_K):
            g = jnp.where(jl == lp[:, k:k + 1], gate[:, k:k + 1], g)
        gb = g.astype(bf16)
        a, b = _unpack_rows(yb_ref[slot, c * SORT_CHUNK:(c + 1) * SORT_CHUNK, :])
        lo = lo + _dot(gb, a.astype(bf16))
        hi = hi + _dot(gb, b.astype(bf16))
    y = jnp.concatenate([lo, hi], axis=1)
    ms = jnp.mean(y * y, axis=-1, keepdims=True)
    x2 = x1_ref[...] + gf_ref[0] * (y * lax.rsqrt(ms + NORM_EPS) * gpost_ref[...])
    x2_o[...] = x2
    if nxt_refs:
        scn_ref, shn_ref, gn_ref, hn_o = nxt_refs
        ms2 = jnp.mean(x2 * x2, axis=-1, keepdims=True)
        hn = x2 * lax.rsqrt(ms2 + NORM_EPS) * gn_ref[...]
        hn_o[...] = (hn * (1.0 + scn_ref[0]) + shn_ref[0]).astype(hn_o.dtype)


def _combine_kernel_next(ssrc_ref, sdst_ref, sn_ref, stot_ref, lp_ref, gate_ref, ys_hbm, ysh_ref, x1_ref, gf_ref,
                         gpost_ref, scn_ref, shn_ref, gn_ref, x2_o, hn_o, yb_ref, sems):
    _combine_kernel(ssrc_ref, sdst_ref, sn_ref, stot_ref, lp_ref, gate_ref, ys_hbm, ysh_ref, x1_ref, gf_ref,
                    gpost_ref, x2_o, yb_ref, sems, nxt_refs=(scn_ref, shn_ref, gn_ref, hn_o))


def _combine(seg_src, seg_dst, seg_n, seg_tot, lp_tk, gate_tk, ys, ysh, x1, mod3, g_post, S, tm, lmax, nxt=None):
    T, D = x1.shape
    DW = ys.shape[1]
    tpb = S // tm
    tile = pl.BlockSpec((tm, D), lambda i, *_: (i, 0))
    ktile = pl.BlockSpec((tm, TOP_K), lambda i, *_: (i, 0))
    modspec = lambda seg: pl.BlockSpec((1, 1, D), lambda i, *_: ((i // tpb) * N_MOD + seg, 0, 0))
    rowspec = pl.BlockSpec((1, D), lambda i, *_: (0, 0))
    in_specs = [ktile, ktile, pl.BlockSpec(memory_space=pl.ANY), tile, tile, modspec(5), rowspec]
    args = [seg_src, seg_dst, seg_n, seg_tot, lp_tk, gate_tk, ys, ysh, x1, mod3, g_post.reshape(1, D)]
    out_specs, out_shape, kern = tile, jax.ShapeDtypeStruct((T, D), f32), _combine_kernel
    if nxt is not None:
        in_specs += [modspec(1), modspec(0), rowspec]
        args += [nxt[0], nxt[0], nxt[1].reshape(1, D)]
        out_specs, out_shape = [tile, tile], [out_shape, jax.ShapeDtypeStruct((T, D), bf16)]
        kern = _combine_kernel_next
    grid_spec = pltpu.PrefetchScalarGridSpec(
        num_scalar_prefetch=4,
        grid=(T // tm,),
        in_specs=in_specs,
        out_specs=out_specs,
        scratch_shapes=[pltpu.VMEM((2, lmax, DW), u32), pltpu.SemaphoreType.DMA((2,))],
    )
    return pl.pallas_call(
        kern,
        grid_spec=grid_spec,
        out_shape=out_shape,
        compiler_params=_cparams(("arbitrary",)),
        name="moe_combine",
    )(*args)


def _tile(n, pref):
    t = min(n, pref)
    assert n % t == 0, (n, t)
    return t


def _layer(i, x2, h1, mod3, p, wexp, v_first, nxt, B, S, cfg):
    T, D = x2.shape
    H = (D // 2) // ATT_V_DIM
    W = D - D // 2
    att_cols = 2 * H * 2 * ATT_QK_DIM + H * ATT_V_DIM
    lam_init = 0.8 - 0.6 * math.exp(-0.3 * i)

    if h1 is None:
        h1 = _prenorm(x2, mod3, p["g_pre_mix"], S, _tile(S, cfg["tm_norm"]), 1, 0)
    att = _inproj(h1, p["w_in"][:, :att_cols].astype(bf16), bf16, _tile(T, cfg["tm_in"]), cfg["tn_att"])
    w_rwkv = p["w_in"][:, att_cols:].astype(bf16)

    slopes = jnp.broadcast_to(
        (2.0 ** (-ALIBI_MAX_BIAS * jnp.arange(1, H + 1, dtype=f32) / H))[:, None, None], (H, 1, LANES))
    lamp = jnp.stack([p["lam_q1"], p["lam_k1"], p["lam_q2"], p["lam_k2"]])
    o_att = _attention(att, slopes, lamp, p["att_subln_g"], B, S, H, lam_init, _tile(S, cfg["tq"]), cfg["hp"])

    cols = w_rwkv.shape[1]
    zw = jnp.zeros((RWKV_A_RANK, W), f32)
    heads = W // RWKV_HEAD
    ind = (jnp.arange(W)[:, None] // RWKV_HEAD == jnp.arange(LANES)[None, :]).astype(bf16)
    prm = {
        "mu": p["rwkv_mu"].reshape(1, cols), "w0": p["rwkv_w0"].reshape(1, W),
        "w2p": jnp.concatenate([p["rwkv_w2"], zw], axis=0),
        "a0": p["rwkv_a0"].reshape(1, W),
        "a2p": jnp.concatenate([jnp.zeros((RWKV_W_RANK, W), f32), p["rwkv_a2"]], axis=0),
        "g2": p["rwkv_g2"], "k_k": p["rwkv_k_k"].reshape(1, W), "k_a": p["rwkv_k_a"].reshape(1, W),
        "ind": ind, "indt": ind.T,
    }
    if v_first is not None:
        padc = LANES - RWKV_V_RANK
        prm["v0"] = p["rwkv_v0"].reshape(1, W)
        prm["v1p"] = jnp.pad(p["rwkv_v1"], ((0, 0), (0, padc)))
        prm["v2p"] = jnp.pad(p["rwkv_v2"], ((0, padc), (0, 0)))
    r, lw, kh, v, kn, kb, g = _rwkv_prep(h1, w_rwkv, prm, v_first, B, S, W, _tile(S, cfg["tm_prep"]))
    if v_first is None:
        v_first = v
    r2, y0, mmat, g0 = _wkv_intra(r, lw, kh, v, kn, kb, min(cfg["nc"], S // CHUNK), cfg["passes_intra"])
    o_rwkv = _wkv_state(r2, y0, mmat, g0, r, kh, v, g, p["rwkv_lnx_w"].reshape(1, W),
                        p["rwkv_lnx_b"].reshape(1, W), p["rwkv_r_k"].reshape(1, W), B, S,
                        min(cfg["pg"], W // LANES), min(cfg["cb"], S // CHUNK), cfg["passes_state"])
    del heads

    x1, h2, logits_t = _postmix(o_att, o_rwkv, p["w_out"][:D // 2].astype(bf16), p["w_out"][D // 2:].astype(bf16),
                                x2, mod3,
                                p["g_post_mix"], p["g_pre_ffn"], p["w_router"].T, S, _tile(S, cfg["tm_post"]))

    tm_t = _tile(T, cfg["tm_tile"])
    ntiles = T // tm_t
    eidx_t, gate_t, lrank_t, cnt_tbl = _router(logits_t, p["router_bias"], tm_t)
    blk = cfg["blk"]
    run = (cnt_tbl[:, :ntiles].T + SUBLANES - 1) // SUBLANES * SUBLANES
    counts = jnp.sum(run, axis=0)
    padded = (counts + blk - 1) // blk * blk
    pad_end = jnp.cumsum(padded)
    pad_start = pad_end - padded
    e_ids = jnp.arange(N_EXPERTS, dtype=i32)
    tile_off = jnp.cumsum(run, axis=0) - run
    loc_off = jnp.cumsum(run, axis=1) - run
    run_dst = pad_start[None, :] + tile_off
    psz = jnp.asarray(_piece_sizes(tm_t), i32)[:, None, None]
    has = (run[None] & psz) != 0
    before = run[None] & ~(2 * psz - 1)
    slot_of = jnp.cumsum(has, axis=2) - 1
    hit = has[:, :, None, :] & (slot_of[:, :, None, :] == e_ids[None, None, :, None])
    seg_src = jnp.sum(jnp.where(hit, (loc_off[None] + before)[:, :, None, :], 0), axis=-1).reshape(-1).astype(i32)
    seg_dst = jnp.sum(jnp.where(hit, (run_dst[None] + before)[:, :, None, :], 0), axis=-1).reshape(-1).astype(i32)
    seg_n = jnp.sum(has, axis=2).reshape(-1).astype(i32)
    seg_tot = jnp.sum(run, axis=1).astype(i32)
    loc_tok = jnp.repeat(loc_off, tm_t, axis=0)
    lp_t = jnp.sum(jnp.where(eidx_t[:, :, None] == e_ids, loc_tok[None], 0), axis=-1).astype(f32) + lrank_t
    lmax = -(-(tm_t * TOP_K + N_EXPERTS * (SUBLANES - 1)) // SORT_CHUNK) * SORT_CHUNK
    nblk = -(-(T * TOP_K + ntiles * N_EXPERTS * (SUBLANES - 1)) // blk) + N_EXPERTS
    P = nblk * blk
    blk_start = jnp.arange(nblk, dtype=i32) * blk
    nused = (pad_end[-1] // blk).astype(i32).reshape(1)
    blk_pos = jnp.minimum(blk_start, pad_end[-1] - blk)
    blk_e = jnp.minimum(jnp.sum((pad_end[None, :] <= blk_pos[:, None]).astype(i32), axis=1), N_EXPERTS - 1)
    cand = jnp.where(counts > 0, e_ids, N_EXPERTS)
    later = jnp.where(e_ids[None, :] > blk_e[:, None], cand[None, :], N_EXPERTS)
    nxt_e = jnp.min(later, axis=1)
    nxt_e = jnp.where(nxt_e >= N_EXPERTS, -1, nxt_e).astype(i32)

    xs, ysh = _dispatch(seg_src, seg_dst, seg_n, seg_tot, counts.astype(i32), pad_start.astype(i32), nused, lp_t, h2,
                        p["w_sh_gate"].astype(bf16), p["w_sh_up"].astype(bf16), p["w_sh_down"].astype(bf16),
                        P, blk, tm_t, lmax)
    ys = _experts(blk_e, nxt_e, nused, xs, wexp[0], wexp[1], wexp[2], i, blk)
    out = _combine(seg_src, seg_dst, seg_n, seg_tot, lp_t.T, gate_t.T, ys, ysh, x1,
                   mod3, p["g_post_ffn"], S, tm_t, lmax, nxt)
    x_out, h_next = (out, None) if nxt is None else out
    return x_out, h_next, v_first


_CFG = dict(tm_norm=512, tm_in=2048, tn_att=1024, tq=512, hp=4, tm_prep=256, nc=16, passes_intra=1, passes_state=1, pg=8, cb=8,
            tm_post=512, blk=256, tm_tile=256)

_LAYER_KEYS = ("g_pre_mix", "g_post_mix", "g_pre_ffn", "g_post_ffn", "w_in", "w_out", "lam_q1", "lam_k1",
               "lam_q2", "lam_k2", "att_subln_g", "rwkv_mu", "rwkv_w0", "rwkv_w2", "rwkv_a0", "rwkv_a2",
               "rwkv_g2", "rwkv_k_k", "rwkv_k_a", "rwkv_r_k", "rwkv_lnx_w", "rwkv_lnx_b", "w_router",
               "router_bias", "w_sh_gate", "w_sh_up", "w_sh_down")


def _forward(x, c, params, cfg):
    B, S, D = x.shape
    L = params["w_in"].shape[0]
    bp = 16
    c_pad = jnp.zeros((bp, D), f32).at[:B].set(c)
    mod = _ada_mod(c_pad, params["w_ada"], params["b_ada"])
    x2 = x.reshape(B * S, D)
    v_first = None
    h1 = None
    for i in range(L):
        p = {k: params[k][i] for k in _LAYER_KEYS}
        if i > 0:
            p["rwkv_v0"] = params["rwkv_v0"][i - 1]
            p["rwkv_v1"] = params["rwkv_v1"][i - 1]
            p["rwkv_v2"] = params["rwkv_v2"][i - 1]
        mod3 = mod[i, :B].reshape(B * N_MOD, 1, D)
        wexp = (params["w_exp_gate"], params["w_exp_up"], params["w_exp_down"])
        nxt = None
        if i + 1 < L:
            nxt = (mod[i + 1, :B].reshape(B * N_MOD, 1, D), params["g_pre_mix"][i + 1])
        x2, h1, v_first = _layer(i, x2, h1, mod3, p, wexp, v_first, nxt, B, S, cfg)
    return x2.reshape(B, S, D)


def kernel(x, c, w_ada, b_ada, g_pre_mix, g_post_mix, g_pre_ffn, g_post_ffn, w_in, w_out, lam_q1, lam_k1, lam_q2, lam_k2, att_subln_g, rwkv_mu, rwkv_w0, rwkv_w2, rwkv_a0, rwkv_a2, rwkv_g2, rwkv_k_k, rwkv_k_a, rwkv_r_k, rwkv_lnx_w, rwkv_lnx_b, rwkv_v0, rwkv_v1, rwkv_v2, w_router, router_bias, w_exp_gate, w_exp_up, w_exp_down, w_sh_gate, w_sh_up, w_sh_down):
    params = dict(w_ada=w_ada, b_ada=b_ada, g_pre_mix=g_pre_mix, g_post_mix=g_post_mix, g_pre_ffn=g_pre_ffn,
                  g_post_ffn=g_post_ffn, w_in=w_in, w_out=w_out, lam_q1=lam_q1, lam_k1=lam_k1, lam_q2=lam_q2,
                  lam_k2=lam_k2, att_subln_g=att_subln_g, rwkv_mu=rwkv_mu, rwkv_w0=rwkv_w0, rwkv_w2=rwkv_w2,
                  rwkv_a0=rwkv_a0, rwkv_a2=rwkv_a2, rwkv_g2=rwkv_g2, rwkv_k_k=rwkv_k_k, rwkv_k_a=rwkv_k_a,
                  rwkv_r_k=rwkv_r_k, rwkv_lnx_w=rwkv_lnx_w, rwkv_lnx_b=rwkv_lnx_b, rwkv_v0=rwkv_v0,
                  rwkv_v1=rwkv_v1, rwkv_v2=rwkv_v2, w_router=w_router, router_bias=router_bias,
                  w_exp_gate=w_exp_gate, w_exp_up=w_exp_up, w_exp_down=w_exp_down, w_sh_gate=w_sh_gate,
                  w_sh_up=w_sh_up, w_sh_down=w_sh_down)
    return _forward(x, c, params, _CFG)
```

```python
import functools
import math

import jax
import jax.numpy as jnp
from jax import lax
from jax.experimental import pallas as pl
from jax.experimental.pallas import tpu as pltpu

f32 = jnp.float32
bf16 = jnp.bfloat16
i32 = jnp.int32
u32 = jnp.uint32

ATT_QK_DIM = 64
ATT_V_DIM = 128
ALIBI_MAX_BIAS = 8.0
ATT_SUBLN_EPS = 1e-5
RWKV_HEAD = 64
RWKV_W_RANK = 64
RWKV_A_RANK = 64
RWKV_G_RANK = 128
RWKV_V_RANK = 32
RWKV_GN_EPS = 64e-5
N_EXPERTS = 64
N_GROUPS = 8
TOPK_GROUPS = 4
TOP_K = 8
ROUTED_SCALE = 2.5
NORM_EPS = 1e-6
N_MOD = 6

LANES = 128
SUBLANES = 8
CHUNK = 64
VMEM_LIMIT = 56 * 1024 * 1024


def _cparams(sem):
    return pltpu.CompilerParams(dimension_semantics=sem, vmem_limit_bytes=VMEM_LIMIT)


def _dot(a, b):
    return jnp.dot(a, b, preferred_element_type=f32)


def _dot_nt(a, b):
    return lax.dot_general(a, b, (((1,), (1,)), ((), ())), preferred_element_type=f32)


def _split2(x):
    hi = x.astype(bf16)
    lo = (x - hi.astype(f32)).astype(bf16)
    return hi, lo


def _mm(a, b, passes=1):
    if passes == 1:
        return _dot(a.astype(bf16), b.astype(bf16))
    ah, al = _split2(a.astype(f32))
    bh, bl = _split2(b.astype(f32))
    return (_dot(al, bh) + _dot(ah, bl)) + _dot(ah, bh)


def _mm_exact_rhs(a, b_bf16):
    ah, al = _split2(a)
    return _dot(al, b_bf16) + _dot(ah, b_bf16)


def _sigmoid(x):
    return 1.0 / (1.0 + jnp.exp(-x))


def _pack_rows(x):
    half = x.shape[1] // 2
    a = x[:, :half].astype(bf16).astype(f32)
    b = x[:, half:].astype(bf16).astype(f32)
    lo = lax.shift_right_logical(lax.bitcast_convert_type(a, u32), jnp.uint32(16))
    hi = lax.bitcast_convert_type(b, u32) & jnp.uint32(0xFFFF0000)
    return lo | hi


def _unpack_rows(w):
    lo = lax.bitcast_convert_type(lax.shift_left(w, jnp.uint32(16)), f32)
    hi = lax.bitcast_convert_type(w & jnp.uint32(0xFFFF0000), f32)
    return lo, hi


def _ada_kernel(c_ref, w_ref, b_ref, o_ref):
    c = c_ref[...]
    cond = (c * _sigmoid(c)).astype(bf16)
    o_ref[0] = _dot(cond, w_ref[0].astype(bf16)) + b_ref[0]


def _ada_mod(c_pad, w_ada, b_ada, tn=1024):
    L, D, N = w_ada.shape
    bp = c_pad.shape[0]
    return pl.pallas_call(
        _ada_kernel,
        grid=(L, N // tn),
        in_specs=[pl.BlockSpec((bp, D), lambda l, j: (0, 0)),
                  pl.BlockSpec((1, D, tn), lambda l, j: (l, 0, j)),
                  pl.BlockSpec((1, 1, tn), lambda l, j: (l, 0, j))],
        out_specs=pl.BlockSpec((1, bp, tn), lambda l, j: (l, 0, j)),
        out_shape=jax.ShapeDtypeStruct((L, bp, N), f32),
        compiler_params=_cparams(("arbitrary", "arbitrary")),
        name="ada_mod",
    )(c_pad, w_ada, b_ada.reshape(L, 1, N))


def _prenorm_kernel(x_ref, sc_ref, sh_ref, g_ref, h_ref):
    x = x_ref[...]
    ms = jnp.mean(x * x, axis=-1, keepdims=True)
    y = x * lax.rsqrt(ms + NORM_EPS) * g_ref[...]
    h_ref[...] = (y * (1.0 + sc_ref[0]) + sh_ref[0]).astype(h_ref.dtype)


def _prenorm(x2, mod3, g, S, tm, seg_sc, seg_sh):
    T, D = x2.shape
    tpb = S // tm
    tile = pl.BlockSpec((tm, D), lambda i: (i, 0))
    return pl.pallas_call(
        _prenorm_kernel,
        grid=(T // tm,),
        in_specs=[tile,
                  pl.BlockSpec((1, 1, D), lambda i: ((i // tpb) * N_MOD + seg_sc, 0, 0)),
                  pl.BlockSpec((1, 1, D), lambda i: ((i // tpb) * N_MOD + seg_sh, 0, 0)),
                  pl.BlockSpec((1, D), lambda i: (0, 0))],
        out_specs=tile,
        out_shape=jax.ShapeDtypeStruct((T, D), bf16),
        compiler_params=_cparams(("arbitrary",)),
        name="prenorm",
    )(x2, mod3, mod3, g.reshape(1, D))


def _inproj_kernel(h_ref, w_ref, o_ref):
    o_ref[...] = _dot(h_ref[...], w_ref[...]).astype(o_ref.dtype)


def _inproj(h, w_bf, out_dtype, tm, tn):
    T, D = h.shape
    N = w_bf.shape[1]
    return pl.pallas_call(
        _inproj_kernel,
        grid=(T // tm, N // tn),
        in_specs=[pl.BlockSpec((tm, D), lambda i, j: (i, 0)),
                  pl.BlockSpec((D, tn), lambda i, j: (0, j))],
        out_specs=pl.BlockSpec((tm, tn), lambda i, j: (i, j)),
        out_shape=jax.ShapeDtypeStruct((T, N), out_dtype),
        compiler_params=_cparams(("arbitrary", "arbitrary")),
        name="inproj",
    )(h, w_bf)


def _attn_kernel(q_ref, k_ref, v_ref, slope_ref, lamp_ref, g_ref, o_ref,
                 q2t_ref, vt_ref, m_ref, l_ref, acc_ref, *, tq, hp, lam_init):
    qi = pl.program_id(2)
    scale = ATT_QK_DIM ** -0.5
    heads = range(hp)
    hl = [slice(h * LANES, (h + 1) * LANES) for h in heads]
    slope = [slope_ref[h][:, 0:1] for h in heads]

    @pl.when(qi == 0)
    def _():
        for h in heads:
            vt_ref[h] = v_ref[:, hl[h]].astype(f32).T.astype(bf16)

    dim = lax.broadcasted_iota(i32, (LANES, 1), 0)
    first = dim < ATT_QK_DIM
    for h in heads:
        qt = (q_ref[:, hl[h]].astype(f32) * scale).T
        q2t_ref[h, :, 0:tq] = jnp.where(first, qt, 0.0).astype(bf16)
        q2t_ref[h, :, tq:2 * tq] = jnp.where(first, 0.0, qt).astype(bf16)
    m_ref[...] = jnp.full(m_ref.shape, -jnp.inf, f32)
    l_ref[...] = jnp.zeros(l_ref.shape, f32)
    acc_ref[...] = jnp.zeros(acc_ref.shape, f32)

    kr = lax.broadcasted_iota(i32, (tq, 2 * tq), 0)
    qc = lax.broadcasted_iota(i32, (tq, 2 * tq), 1)
    causal = jnp.where(qc >= tq, qc - tq, qc) >= kr
    krow = lax.broadcasted_iota(i32, (tq, 1), 0).astype(f32)

    def step(ki, masked):
        start = pl.multiple_of(ki * tq, tq)
        kpos = krow + (ki * tq).astype(f32)
        s = [_dot(k_ref[pl.ds(start, tq), hl[h]], q2t_ref[h]) for h in heads]
        s = [s[h] + slope[h] * kpos for h in heads]
        if masked:
            s = [jnp.where(causal, s[h], -jnp.inf) for h in heads]
        m_prev = [m_ref[h] for h in heads]
        m_new = [jnp.maximum(m_prev[h], jnp.max(s[h], axis=0, keepdims=True)) for h in heads]
        alpha = [jnp.exp(m_prev[h] - m_new[h]) for h in heads]
        p = [jnp.exp(s[h] - m_new[h]) for h in heads]
        for h in heads:
            l_ref[h] = alpha[h] * l_ref[h] + jnp.sum(p[h], axis=0, keepdims=True)
            acc_ref[h] = alpha[h] * acc_ref[h] + _dot(vt_ref[h, :, pl.ds(start, tq)], p[h].astype(bf16))
            m_ref[h] = m_new[h]

    def body(ki, carry):
        step(ki, False)
        return carry

    lax.fori_loop(0, qi, body, 0)
    step(qi, True)

    lp = lamp_ref[...]
    lam = (jnp.exp(jnp.sum(lp[0:1] * lp[1:2], axis=-1, keepdims=True))
           - jnp.exp(jnp.sum(lp[2:3] * lp[3:4], axis=-1, keepdims=True)) + lam_init)
    for h in heads:
        on = acc_ref[h] * (1.0 / l_ref[h])
        o = on[:, 0:tq] - lam * on[:, tq:2 * tq]
        o = o * lax.rsqrt(jnp.mean(o * o, axis=0, keepdims=True) + ATT_SUBLN_EPS)
        o = o * g_ref[...] * (1.0 - lam_init)
        o_ref[:, hl[h]] = o.T.astype(o_ref.dtype)


def _attention(att, slopes, lamp, subln_g, B, S, H, lam_init, tq, hp):
    T = att.shape[0]
    nq = S // tq
    kern = functools.partial(_attn_kernel, tq=tq, hp=hp, lam_init=lam_init)
    hw = hp * LANES
    return pl.pallas_call(
        kern,
        grid=(B, H // hp, nq),
        in_specs=[pl.BlockSpec((tq, hw), lambda b, g, q: (b * nq + q, g)),
                  pl.BlockSpec((S, hw), lambda b, g, q: (b, H // hp + g)),
                  pl.BlockSpec((S, hw), lambda b, g, q: (b, 2 * (H // hp) + g)),
                  pl.BlockSpec((hp, 1, LANES), lambda b, g, q: (g, 0, 0)),
                  pl.BlockSpec((4, ATT_QK_DIM), lambda b, g, q: (0, 0)),
                  pl.BlockSpec((ATT_V_DIM, 1), lambda b, g, q: (0, 0))],
        out_specs=pl.BlockSpec((tq, hw), lambda b, g, q: (b * nq + q, g)),
        out_shape=jax.ShapeDtypeStruct((T, H * ATT_V_DIM), bf16),
        scratch_shapes=[pltpu.VMEM((hp, LANES, 2 * tq), bf16),
                        pltpu.VMEM((hp, LANES, S), bf16),
                        pltpu.VMEM((hp, 1, 2 * tq), f32),
                        pltpu.VMEM((hp, 1, 2 * tq), f32),
                        pltpu.VMEM((hp, LANES, 2 * tq), f32)],
        compiler_params=_cparams(("arbitrary", "arbitrary", "arbitrary")),
        name="diff_attention",
    )(att, att, att, slopes, lamp, subln_g.reshape(ATT_V_DIM, 1))


def _head_sums(x, ind, indt):
    s = _mm_exact_rhs(x, ind)
    return _mm_exact_rhs(s, indt)


def _rwkv_prep_kernel(*refs, W, has_vres, tpb):
    if has_vres:
        (hx_ref, win_ref, mu_ref, w0_ref, w2_ref, a0_ref, a2_ref, g2_ref, kk_ref, ka_ref, ind_ref, indt_ref,
         vf_ref, v0_ref, v1_ref, v2_ref,
         r_o, lw_o, kh_o, v_o, kn_o, kb_o, g_o, carry_ref, feats_ref) = refs
    else:
        (hx_ref, win_ref, mu_ref, w0_ref, w2_ref, a0_ref, a2_ref, g2_ref, kk_ref, ka_ref, ind_ref, indt_ref,
         r_o, lw_o, kh_o, v_o, kn_o, kb_o, g_o, carry_ref, feats_ref) = refs

    i = pl.program_id(0)

    @pl.when(i == 0)
    def _():
        feats_ref[...] = jnp.zeros(feats_ref.shape, f32)
        carry_ref[...] = jnp.zeros(carry_ref.shape, f32)

    h = feats_ref[...]
    nxt = _dot(hx_ref[...], win_ref[...])
    tm = h.shape[0]

    first_in_batch = lax.rem(jnp.maximum(i - 1, 0), tpb) == 0
    carry = jnp.where(first_in_batch, 0.0, carry_ref[...])
    rolled = pltpu.roll(h, 1, axis=0)
    row = lax.broadcasted_iota(i32, (tm, 1), 0)
    prev = jnp.where(row == 0, carry, rolled)
    carry_ref[...] = h[tm - 1:tm, :]
    feats_ref[...] = nxt
    feats = h + (prev - h) * mu_ref[...]

    r = feats[:, 0:W]
    k = feats[:, W:2 * W]
    v = feats[:, 2 * W:3 * W]
    wa = feats[:, 3 * W:3 * W + LANES]
    g_lo = feats[:, 3 * W + LANES:3 * W + 2 * LANES]

    w = w0_ref[...] + _mm(jnp.tanh(wa), w2_ref[...], passes=3)
    lw_o[...] = -math.exp(-0.5) * _sigmoid(w)
    a = _sigmoid(a0_ref[...] + _mm(wa, a2_ref[...], passes=3))
    g_o[...] = _mm(_sigmoid(g_lo), g2_ref[...]).astype(g_o.dtype)

    if has_vres:
        mix = _sigmoid(v0_ref[...] + _mm(_mm(v, v1_ref[...]), v2_ref[...]))
        v = v + (vf_ref[...].astype(f32) - v) * mix

    kk = k * kk_ref[...]
    ss = _head_sums(kk * kk, ind_ref[...], indt_ref[...])
    kk = kk / jnp.maximum(jnp.sqrt(ss), 1e-12)
    r_o[...] = r.astype(r_o.dtype)
    kh_o[...] = (k * (1.0 + (a - 1.0) * ka_ref[...])).astype(kh_o.dtype)
    v_o[...] = v.astype(v_o.dtype)
    kn_o[...] = kk.astype(kn_o.dtype)
    kb_o[...] = (kk * a).astype(kb_o.dtype)


def _rwkv_prep(hx, w_rwkv, prm, vfirst, B, S, W, tm):
    T, D = hx.shape
    COLS = w_rwkv.shape[1]
    tpb = S // tm
    n = T // tm
    has_vres = vfirst is not None
    row = lambda c: pl.BlockSpec((1, c), lambda i: (0, 0))
    full = lambda a: pl.BlockSpec(a.shape, lambda i: (0, 0))
    tile = pl.BlockSpec((tm, W), lambda i: (jnp.maximum(i - 1, 0), 0))
    args = [hx, w_rwkv, prm["mu"], prm["w0"], prm["w2p"], prm["a0"], prm["a2p"], prm["g2"], prm["k_k"], prm["k_a"],
            prm["ind"], prm["indt"]]
    specs = [pl.BlockSpec((tm, D), lambda i: (jnp.minimum(i, n - 1), 0)),
             pl.BlockSpec((D, COLS), lambda i: (0, 0), pipeline_mode=pl.Buffered(1)),
             row(COLS), row(W), full(prm["w2p"]),
             row(W), full(prm["a2p"]), full(prm["g2"]), row(W), row(W), full(prm["ind"]), full(prm["indt"])]
    if has_vres:
        args += [vfirst, prm["v0"], prm["v1p"], prm["v2p"]]
        specs += [tile, row(W), full(prm["v1p"]), full(prm["v2p"])]
    out_dtypes = (bf16, f32, bf16, bf16, bf16, bf16, bf16)
    kern = functools.partial(_rwkv_prep_kernel, W=W, has_vres=has_vres, tpb=tpb)
    return pl.pallas_call(
        kern,
        grid=(n + 1,),
        in_specs=specs,
        out_specs=[tile] * 7,
        out_shape=[jax.ShapeDtypeStruct((T, W), dt) for dt in out_dtypes],
        scratch_shapes=[pltpu.VMEM((1, COLS), f32), pltpu.VMEM((tm, COLS), f32)],
        compiler_params=_cparams(("arbitrary",)),
        name="rwkv_prep",
    )(*args)


def _wkv_chunks(rs, lws, ks, vs, kns, kbs, passes):
    C = CHUNK
    P2 = 2 * C
    n = range(len(rs))
    ri = lax.broadcasted_iota(i32, (C, C), 0)
    ci = lax.broadcasted_iota(i32, (C, C), 1)
    tri = (ci <= ri).astype(bf16)
    lane = lax.broadcasted_iota(i32, (1, LANES), 1)
    m0 = (lane < RWKV_HEAD).astype(f32)
    m1 = 1.0 - m0
    rr = lax.broadcasted_iota(i32, (P2, P2), 0)
    cc = lax.broadcasted_iota(i32, (P2, P2), 1)
    same = jnp.where(rr >= C, 1, 0) == jnp.where(cc >= C, 1, 0)
    strict = same & (cc < rr)
    incl = same & (cc <= rr)
    incl2 = jnp.concatenate([incl, incl], axis=1)
    eye = (rr == cc).astype(f32)
    zeros_p = jnp.zeros((P2, LANES), f32)
    zeros_c = jnp.zeros((C, LANES), f32)
    stack = lambda x: jnp.concatenate([x * m0, x * m1], axis=0)
    fold = lambda x: x[0:C] + x[C:2 * C]

    def cumsum(lw):
        h1 = lw.astype(bf16)
        r1 = lw - h1.astype(f32)
        h2 = r1.astype(bf16)
        h3 = (r1 - h2.astype(f32)).astype(bf16)
        return (_dot(tri, h3) + _dot(tri, h2)) + _dot(tri, h1)

    cum = [cumsum(lws[j]) for j in n]
    cum_c = [cum[j][C - 1:C, :] for j in n]
    at = [-kns[j] * jnp.exp(cum[j] - lws[j]) for j in n]
    rt = [rs[j] * jnp.exp(cum[j]) for j in n]
    einv = [jnp.exp(-cum[j]) for j in n]
    bt = [kbs[j] * einv[j] for j in n]
    kt = [ks[j] * einv[j] for j in n]
    eh = [jnp.exp(cum_c[j] - cum[j]) for j in n]
    bh = [kbs[j] * eh[j] for j in n]
    kh = [ks[j] * eh[j] for j in n]
    w_c = [jnp.exp(cum_c[j]) for j in n]
    abd = [stack(at[j]) for j in n]
    vst = [stack(vs[j]) for j in n]
    lhs = [jnp.concatenate([abd[j], stack(rt[j])], axis=0) for j in n]
    rhs = [jnp.concatenate([stack(bt[j]), stack(kt[j])], axis=0) for j in n]
    gram = [_mm_nt(lhs[j], rhs[j], passes) for j in n]
    lab = [jnp.where(strict, gram[j][0:P2, 0:P2], 0.0) for j in n]
    lak = [jnp.where(strict, gram[j][0:P2, P2:2 * P2], 0.0) for j in n]
    mrbk = [jnp.where(incl2, gram[j][P2:2 * P2, :], 0.0) for j in n]

    x0 = [_mm(lak[j], vst[j], passes) for j in n]
    tinv = [eye + lab[j] for j in n]
    lp = [_mm(lab[j], lab[j], passes) for j in n]
    n_sq = int(math.log2(C)) - 1
    for it in range(n_sq):
        if it < n_sq - 1:
            both = [_mm(lp[j], jnp.concatenate([lp[j], tinv[j]], axis=1), passes) for j in n]
            tinv = [tinv[j] + both[j][:, P2:2 * P2] for j in n]
            lp = [both[j][:, 0:P2] for j in n]
        else:
            tinv = [tinv[j] + _mm(lp[j], tinv[j], passes) for j in n]

    ta = [_mm(tinv[j], jnp.concatenate([abd[j], x0[j]], axis=1), passes) for j in n]
    rhs2 = [jnp.concatenate([ta[j], jnp.concatenate([zeros_p, vst[j]], axis=1)], axis=0) for j in n]
    z = [_mm(mrbk[j], rhs2[j], passes) for j in n]
    r2 = [rt[j] + fold(z[j][:, 0:LANES]) for j in n]
    y0 = [fold(z[j][:, LANES:2 * LANES]) for j in n]
    lhs3t = [jnp.concatenate([bh[j], kh[j]], axis=0).T for j in n]
    rhs3 = [jnp.concatenate([fold(ta[j]), jnp.concatenate([zeros_c, vs[j]], axis=1)], axis=0) for j in n]
    wmat = [_mm(lhs3t[j], rhs3[j], passes) for j in n]
    mmat = [jnp.where(same, wmat[j][:, 0:LANES], 0.0) + eye * w_c[j] for j in n]
    g0 = [jnp.where(same, wmat[j][:, LANES:2 * LANES], 0.0) for j in n]
    return r2, y0, mmat, g0


def _mm_nt(a, b, passes):
    if passes == 1:
        return _dot_nt(a.astype(bf16), b.astype(bf16))
    ah, al = _split2(a)
    bh, bl = _split2(b)
    return (_dot_nt(al, bh) + _dot_nt(ah, bl)) + _dot_nt(ah, bh)


def _wkv_intra_kernel(r_ref, lw_ref, k_ref, v_ref, kn_ref, kb_ref, r2_o, y0_o, m_o, g_o, *, nc, passes):
    C = CHUNK
    sls = [slice(c * C, (c + 1) * C) for c in range(nc)]
    take = lambda ref: [ref[sl, :].astype(f32) for sl in sls]
    r2, y0, mmat, g0 = _wkv_chunks(take(r_ref), take(lw_ref), take(k_ref), take(v_ref),
                                   take(kn_ref), take(kb_ref), passes)
    for c, sl in enumerate(sls):
        r2_o[sl, :] = r2[c].astype(r2_o.dtype)
        y0_o[sl, :] = y0[c].astype(y0_o.dtype)
        m_o[0, c] = mmat[c].astype(m_o.dtype)
        g_o[0, c] = g0[c]


def _wkv_intra(r, lw, kh, v, kn, kb, nc, passes):
    T, W = r.shape
    npair = W // LANES
    rows = nc * CHUNK
    tile = pl.BlockSpec((rows, LANES), lambda p, i: (i, p))
    mat = pl.BlockSpec((1, nc, LANES, LANES), lambda p, i: (p, i, 0, 0))
    kern = functools.partial(_wkv_intra_kernel, nc=nc, passes=passes)
    return pl.pallas_call(
        kern,
        grid=(npair, T // rows),
        in_specs=[tile] * 6,
        out_specs=[tile, tile, mat, mat],
        out_shape=[jax.ShapeDtypeStruct((T, W), bf16), jax.ShapeDtypeStruct((T, W), bf16),
                   jax.ShapeDtypeStruct((npair, T // CHUNK, LANES, LANES), bf16),
                   jax.ShapeDtypeStruct((npair, T // CHUNK, LANES, LANES), f32)],
        compiler_params=_cparams(("arbitrary", "arbitrary")),
        name="wkv_intra",
    )(r, lw, kh, v, kn, kb)


def _wkv_state_kernel(r2_ref, y0_ref, m_ref, g0_ref, r_ref, kh_ref, v_ref, g_ref,
                      lnw_ref, lnb_ref, rk_ref, o_ref, st_ref, y_ref, *, pg, cb, passes):
    C = CHUNK

    @pl.when(pl.program_id(2) == 0)
    def _():
        st_ref[...] = jnp.zeros(st_ref.shape, f32)

    pairs = range(pg)
    lanes = [slice(p * LANES, (p + 1) * LANES) for p in pairs]
    st = [st_ref[p] for p in pairs]
    for c in range(cb):
        rows = slice(c * C, (c + 1) * C)
        for p in pairs:
            y_ref[rows, lanes[p]] = (_mm(r2_ref[rows, lanes[p]], st[p], passes)
                                     + y0_ref[rows, lanes[p]].astype(f32))
        st = [_mm(m_ref[p, c], st[p], passes) + g0_ref[p, c] for p in pairs]
    for p in pairs:
        st_ref[p] = st[p]

    rr = lax.broadcasted_iota(i32, (LANES, LANES), 0)
    cc = lax.broadcasted_iota(i32, (LANES, LANES), 1)
    ones_bd = (jnp.where(rr >= RWKV_HEAD, 1, 0) == jnp.where(cc >= RWKV_HEAD, 1, 0)).astype(bf16)
    for p in pairs:
        y = y_ref[:, lanes[p]]
        mu = _mm_exact_rhs(y, ones_bd) * (1.0 / RWKV_HEAD)
        d = y - mu
        var = _mm_exact_rhs(d * d, ones_bd) * (1.0 / RWKV_HEAD)
        yn = d * lax.rsqrt(var + RWKV_GN_EPS) * lnw_ref[:, lanes[p]] + lnb_ref[:, lanes[p]]
        rk = r_ref[:, lanes[p]].astype(f32) * kh_ref[:, lanes[p]].astype(f32) * rk_ref[:, lanes[p]]
        bonus = _mm_exact_rhs(rk, ones_bd) * v_ref[:, lanes[p]]
        o_ref[:, lanes[p]] = ((yn + bonus) * g_ref[:, lanes[p]]).astype(o_ref.dtype)


def _wkv_state(r2, y0, mm, g0, r, kh, v, g, lnw, lnb, rk, B, S, pg, cb, passes):
    T, W = r.shape
    npair = W // LANES
    rows = cb * CHUNK
    steps = S // rows
    seq = pl.BlockSpec((rows, pg * LANES), lambda b, q, c: (b * steps + c, q))
    mat = pl.BlockSpec((pg, cb, LANES, LANES), lambda b, q, c: (q, b * steps + c, 0, 0))
    prow = pl.BlockSpec((1, pg * LANES), lambda b, q, c: (0, q))
    kern = functools.partial(_wkv_state_kernel, pg=pg, cb=cb, passes=passes)
    return pl.pallas_call(
        kern,
        grid=(B, npair // pg, steps),
        in_specs=[seq, seq, mat, mat, seq, seq, seq, seq, prow, prow, prow],
        out_specs=seq,
        out_shape=jax.ShapeDtypeStruct((T, W), bf16),
        scratch_shapes=[pltpu.VMEM((pg, LANES, LANES), f32), pltpu.VMEM((rows, pg * LANES), f32)],
        compiler_params=_cparams(("arbitrary", "arbitrary", "arbitrary")),
        name="wkv_state",
    )(r2, y0, mm, g0, r, kh, v, g, lnw, lnb, rk)


def _postmix_kernel(oa_ref, orw_ref, wa_ref, wr_ref, x_ref, ga_ref, gpost_ref, gpre_ref, sc_ref, sh_ref, wrt_ref,
                    x1_o, h2_o, lg_o):
    mixed = _dot(oa_ref[...], wa_ref[...]) + _dot(orw_ref[...], wr_ref[...])
    ms = jnp.mean(mixed * mixed, axis=-1, keepdims=True)
    x1 = x_ref[...] + ga_ref[0] * (mixed * lax.rsqrt(ms + NORM_EPS) * gpost_ref[...])
    x1_o[...] = x1
    ms1 = jnp.mean(x1 * x1, axis=-1, keepdims=True)
    h2 = (x1 * lax.rsqrt(ms1 + NORM_EPS) * gpre_ref[...]) * (1.0 + sc_ref[0]) + sh_ref[0]
    h2_o[...] = _pack_rows(h2)
    wh, wl = _split2(wrt_ref[...])
    hh, hl = _split2(h2)
    e = wh.shape[0]
    top = _dot_nt(jnp.concatenate([wh, wl], axis=0), hh)
    lg_o[...] = (top[e:2 * e] + _dot_nt(wh, hl)) + top[0:e]


def _postmix(o_att, o_rwkv, w_out_a, w_out_r, x2, mod3, g_post, g_pre, w_rt, S, tm):
    T, D = x2.shape
    WA = o_att.shape[1]
    WR = o_rwkv.shape[1]
    E = w_rt.shape[0]
    tpb = S // tm
    modspec = lambda seg: pl.BlockSpec((1, 1, D), lambda i: ((i // tpb) * N_MOD + seg, 0, 0))
    tile = pl.BlockSpec((tm, D), lambda i: (i, 0))
    return pl.pallas_call(
        _postmix_kernel,
        grid=(T // tm,),
        in_specs=[pl.BlockSpec((tm, WA), lambda i: (i, 0)),
                  pl.BlockSpec((tm, WR), lambda i: (i, 0)),
                  pl.BlockSpec((WA, D), lambda i: (0, 0), pipeline_mode=pl.Buffered(1)),
                  pl.BlockSpec((WR, D), lambda i: (0, 0), pipeline_mode=pl.Buffered(1)),
                  tile, modspec(2),
                  pl.BlockSpec((1, D), lambda i: (0, 0)),
                  pl.BlockSpec((1, D), lambda i: (0, 0)),
                  modspec(4), modspec(3),
                  pl.BlockSpec((E, D), lambda i: (0, 0))],
        out_specs=[tile, pl.BlockSpec((tm, D // 2), lambda i: (i, 0)),
                   pl.BlockSpec((E, tm), lambda i: (0, i))],
        out_shape=[jax.ShapeDtypeStruct((T, D), f32), jax.ShapeDtypeStruct((T, D // 2), u32),
                   jax.ShapeDtypeStruct((E, T), f32)],
        compiler_params=_cparams(("arbitrary",)),
        name="postmix",
    )(o_att, o_rwkv, w_out_a, w_out_r, x2, mod3, g_post.reshape(1, D), g_pre.reshape(1, D), mod3, mod3, w_rt)


def _first_max(x, iota, n):
    mx = jnp.max(x, axis=0, keepdims=True)
    idx = jnp.min(jnp.where(x == mx, iota, n), axis=0, keepdims=True)
    return mx, idx


def _router_kernel(lg_ref, bias_ref, eidx_o, gate_o, rank_o, cnt_o, cnt_ref):
    E = N_EXPERTS
    G = N_GROUPS
    per = E // G
    tm = lg_ref.shape[1]

    @pl.when(pl.program_id(0) == 0)
    def _():
        cnt_ref[...] = jnp.zeros(cnt_ref.shape, f32)

    scores = _sigmoid(lg_ref[...])
    biased = scores + bias_ref[...]
    neg = -jnp.inf

    iota_p = lax.broadcasted_iota(i32, (per, tm), 0).astype(f32)
    gs = []
    for g in range(G):
        xg = biased[g * per:(g + 1) * per, :]
        m1, i1 = _first_max(xg, iota_p, per)
        m2 = jnp.max(jnp.where(iota_p == i1, neg, xg), axis=0, keepdims=True)
        gs.append(m1 + m2)
    gsc = jnp.concatenate(gs, axis=0)
    iota_g = lax.broadcasted_iota(i32, (G, tm), 0).astype(f32)
    gsel = jnp.zeros((G, tm), f32)
    for _ in range(TOPK_GROUPS):
        _, gi = _first_max(gsc, iota_g, G)
        hit = iota_g == gi
        gsel = jnp.where(hit, 1.0, gsel)
        gsc = jnp.where(hit, neg, gsc)
    masked = jnp.concatenate(
        [jnp.where(gsel[g:g + 1, :] > 0.0, biased[g * per:(g + 1) * per, :], neg) for g in range(G)], axis=0)

    iota_e = lax.broadcasted_iota(i32, (E, tm), 0).astype(f32)
    sel = jnp.zeros((E, tm), f32)
    idxs, vals = [], []
    for _ in range(TOP_K):
        _, ei = _first_max(masked, iota_e, E)
        hit = iota_e == ei
        idxs.append(ei)
        vals.append(jnp.sum(jnp.where(hit, scores, 0.0), axis=0, keepdims=True))
        sel = jnp.where(hit, 1.0, sel)
        masked = jnp.where(hit, neg, masked)
    tot = vals[0]
    for vv in vals[1:]:
        tot = tot + vv
    eidx_o[...] = jnp.concatenate(idxs, axis=0).astype(i32)
    gate_o[...] = jnp.concatenate([vv / tot * ROUTED_SCALE for vv in vals], axis=0)

    rr = lax.broadcasted_iota(i32, (tm, tm), 0)
    cc = lax.broadcasted_iota(i32, (tm, tm), 1)
    before = (rr < cc).astype(bf16)
    pos = _dot(sel.astype(bf16), before)
    rank_o[...] = jnp.concatenate(
        [jnp.sum(jnp.where(iota_e == ei, pos, 0.0), axis=0, keepdims=True) for ei in idxs], axis=0)
    lane = lax.broadcasted_iota(i32, cnt_ref.shape, 1)
    cnt_ref[...] = jnp.where(lane == pl.program_id(0), jnp.sum(sel, axis=1, keepdims=True), cnt_ref[...])
    cnt_o[...] = cnt_ref[...].astype(i32)


def _router(logits_t, bias, tm):
    E, T = logits_t.shape
    assert T // tm <= LANES
    k_tile = pl.BlockSpec((TOP_K, tm), lambda i: (0, i))
    return pl.pallas_call(
        _router_kernel,
        grid=(T // tm,),
        in_specs=[pl.BlockSpec((E, tm), lambda i: (0, i)),
                  pl.BlockSpec((E, 1), lambda i: (0, 0))],
        out_specs=[k_tile, k_tile, k_tile, pl.BlockSpec((E, LANES), lambda i: (0, 0))],
        out_shape=[jax.ShapeDtypeStruct((TOP_K, T), i32), jax.ShapeDtypeStruct((TOP_K, T), f32),
                   jax.ShapeDtypeStruct((TOP_K, T), f32), jax.ShapeDtypeStruct((E, LANES), i32)],
        scratch_shapes=[pltpu.VMEM((E, LANES), f32)],
        compiler_params=_cparams(("arbitrary",)),
        name="router",
    )(logits_t, bias.reshape(E, 1))


def _row_copy(src_ref, s, dst_ref, d, sem):
    return pltpu.make_async_copy(src_ref.at[pl.ds(s, 1), :], dst_ref.at[pl.ds(d, 1), :], sem)


def _zero_fill(cnt_ref, pstart_ref, nused_ref, z_ref, xs_out, sem, blk, nblk, start):
    def act(cp):
        if start:
            cp.start()
        else:
            cp.wait()

    def per_expert(e, carry):
        c = cnt_ref[e]
        base = pstart_ref[e] + c
        npad = (blk - (c & (blk - 1))) & (blk - 1)
        head = (-base) & (SUBLANES - 1)

        def one_row(j, carry2):
            act(_row_copy(z_ref, 0, xs_out, base + j, sem))
            return carry2

        lax.fori_loop(0, head, one_row, 0)
        rem = npad - head
        aligned = base + head
        p = blk // 2
        while p >= SUBLANES:
            off = pl.multiple_of(aligned + (rem & ~(2 * p - 1)), SUBLANES)

            @pl.when((rem & p) != 0)
            def _(p=p, off=off):
                act(pltpu.make_async_copy(z_ref.at[pl.ds(0, p), :], xs_out.at[pl.ds(off, p), :], sem))

            p //= 2
        return carry

    lax.fori_loop(0, N_EXPERTS, per_expert, 0)

    def per_block(b, carry):
        act(pltpu.make_async_copy(z_ref, xs_out.at[pl.ds(pl.multiple_of(b * blk, blk), blk), :], sem))
        return carry

    lax.fori_loop(nused_ref[0], nblk, per_block, 0)


def _piece_sizes(tm):
    return [tm >> s for s in range(tm.bit_length()) if (tm >> s) >= SUBLANES]


def _seg_copies(psrc_ref, pdst_ref, pcnt_ref, tile, buf_ref, hbm_ref, sem, tm, to_hbm):
    ntiles = pl.num_programs(0)
    for s, p in enumerate(_piece_sizes(tm)):
        base = (s * ntiles + tile) * N_EXPERTS

        def one(j, carry, p=p, base=base, s=s):
            v = buf_ref.at[pl.ds(pl.multiple_of(psrc_ref[base + j], SUBLANES), p), :]
            h = hbm_ref.at[pl.ds(pl.multiple_of(pdst_ref[base + j], SUBLANES), p), :]
            (pltpu.make_async_copy(v, h, sem) if to_hbm else pltpu.make_async_copy(h, v, sem)).start(priority=s % 2)
            return carry

        lax.fori_loop(0, pcnt_ref[s * ntiles + tile], one, 0)


def _seg_wait(total_rows, buf_ref, hbm_ref, sem, to_hbm):
    p = 1 << (buf_ref.shape[0].bit_length() - 1)
    while p >= SUBLANES:
        @pl.when((total_rows & p) != 0)
        def _(p=p):
            v = buf_ref.at[pl.ds(0, p), :]
            h = hbm_ref.at[pl.ds(0, p), :]
            (pltpu.make_async_copy(v, h, sem) if to_hbm else pltpu.make_async_copy(h, v, sem)).wait()

        p //= 2


SORT_CHUNK = 512


def _dispatch_kernel(ssrc_ref, sdst_ref, sn_ref, stot_ref, cnt_ref, pstart_ref, nused_ref, lp_ref, h_ref,
                     wsg_ref, wsu_ref, wsd_ref, xs_out, ysh_o, sb_ref, z_ref, sems, sem_z, *, blk, nblk):
    i = pl.program_id(0)
    last = pl.num_programs(0) - 1
    tm = h_ref.shape[0]
    lmax = sb_ref.shape[1]
    slot = i % 2
    seg = functools.partial(_seg_copies, ssrc_ref, sdst_ref, sn_ref, hbm_ref=xs_out, tm=tm, to_hbm=True)

    def seg_wait(tile, s):
        _seg_wait(stot_ref[tile], sb_ref.at[s], xs_out, sems.at[s], True)

    @pl.when(i >= 2)
    def _():
        seg_wait(i - 2, slot)

    hlo, hhi = _unpack_rows(h_ref[...])
    hb = jnp.concatenate([hlo.astype(bf16), hhi.astype(bf16)], axis=1)
    lpv = lp_ref[...]
    half = hb.shape[1] // 2
    for c in range(lmax // SORT_CHUNK):
        jj = (lax.broadcasted_iota(i32, (SORT_CHUNK, tm), 0) + c * SORT_CHUNK).astype(f32)
        onehot = jnp.zeros((SORT_CHUNK, tm), f32)
        for k in range(TOP_K):
            onehot = jnp.where(jj == lpv[k:k + 1, :], 1.0, onehot)
        rows = _dot(onehot.astype(bf16), hb)
        lo = lax.shift_right_logical(lax.bitcast_convert_type(rows[:, :half], u32), jnp.uint32(16))
        hi = lax.bitcast_convert_type(rows[:, half:], u32) & jnp.uint32(0xFFFF0000)
        sb_ref[slot, c * SORT_CHUNK:(c + 1) * SORT_CHUNK, :] = lo | hi

    seg(i, buf_ref=sb_ref.at[slot], sem=sems.at[slot])

    gt = _dot(hb, wsg_ref[...])
    up = _dot(hb, wsu_ref[...])
    ysh_o[...] = _dot(((gt * _sigmoid(gt)) * up).astype(bf16), wsd_ref[...]).astype(ysh_o.dtype)

    @pl.when(i == last)
    def _():
        z_ref[...] = jnp.zeros(z_ref.shape, u32)
        _zero_fill(cnt_ref, pstart_ref, nused_ref, z_ref, xs_out, sem_z, blk, nblk, True)
        _zero_fill(cnt_ref, pstart_ref, nused_ref, z_ref, xs_out, sem_z, blk, nblk, False)
        seg_wait(i, slot)

        @pl.when(i >= 1)
        def _():
            seg_wait(i - 1, 1 - slot)


def _dispatch(seg_src, seg_dst, seg_n, seg_tot, counts, pad_start, nused, lp_t, h2p, wsg, wsu, wsd,
              P, blk, tm, lmax):
    T, DW = h2p.shape
    D, DS = wsg.shape
    assert blk & (blk - 1) == 0 and tm & (tm - 1) == 0 and lmax % SORT_CHUNK == 0
    kern = functools.partial(_dispatch_kernel, blk=blk, nblk=P // blk)
    const = lambda shape: pl.BlockSpec(shape, lambda i, *_: (0, 0))
    grid_spec = pltpu.PrefetchScalarGridSpec(
        num_scalar_prefetch=7,
        grid=(T // tm,),
        in_specs=[pl.BlockSpec((TOP_K, tm), lambda i, *_: (0, i)),
                  pl.BlockSpec((tm, DW), lambda i, *_: (i, 0)),
                  const((D, DS)), const((D, DS)), const((DS, D))],
        out_specs=[pl.BlockSpec(memory_space=pl.ANY), pl.BlockSpec((tm, D), lambda i, *_: (i, 0))],
        scratch_shapes=[pltpu.VMEM((2, lmax, DW), u32), pltpu.VMEM((blk, DW), u32),
                        pltpu.SemaphoreType.DMA((2,)), pltpu.SemaphoreType.DMA],
    )
    return pl.pallas_call(
        kern,
        grid_spec=grid_spec,
        out_shape=[jax.ShapeDtypeStruct((P, DW), u32), jax.ShapeDtypeStruct((T, D), bf16)],
        compiler_params=_cparams(("arbitrary",)),
        name="moe_dispatch",
    )(seg_src, seg_dst, seg_n, seg_tot, counts, pad_start, nused, lp_t, h2p, wsg, wsu, wsd)


def _experts_kernel(blk_e_ref, nxt_e_ref, nused_ref, xs_ref, wg_hbm, wu_hbm, wd_hbm, ys_ref,
                    wg_f, wu_f, wd_f, wg_s, wu_s, wd_s, sems, *, layer):
    i = pl.program_id(0)
    e = blk_e_ref[i]
    changed = jnp.logical_or(i == 0, e != blk_e_ref[jnp.maximum(i - 1, 0)])

    def weight_copies(ex):
        return (pltpu.make_async_copy(wg_hbm.at[layer, ex], wg_f, sems.at[0]),
                pltpu.make_async_copy(wu_hbm.at[layer, ex], wu_f, sems.at[1]),
                pltpu.make_async_copy(wd_hbm.at[layer, ex], wd_f, sems.at[2]))

    @pl.when(i == 0)
    def _():
        for cp in weight_copies(e):
            cp.start()

    @pl.when(changed)
    def _():
        for cp in weight_copies(e):
            cp.wait()
        for src, dst in ((wg_f, wg_s), (wu_f, wu_s), (wd_f, wd_s)):
            rows = src.shape[0] // 8
            for c in range(8):
                dst[c * rows:(c + 1) * rows, :] = src[c * rows:(c + 1) * rows, :].astype(bf16)
        nxt = nxt_e_ref[i]

        @pl.when(nxt >= 0)
        def _():
            for cp in weight_copies(nxt):
                cp.start(priority=1)

    @pl.when(i < nused_ref[0])
    def _():
        lo, hi = _unpack_rows(xs_ref[...])
        x = jnp.concatenate([lo.astype(bf16), hi.astype(bf16)], axis=1)
        gt = _dot(x, wg_s[...])
        up = _dot(x, wu_s[...])
        hmid = (gt * _sigmoid(gt)) * up
        ys_ref[...] = _pack_rows(_dot(hmid.astype(bf16), wd_s[...]))

    @pl.when(i >= nused_ref[0])
    def _():
        ys_ref[...] = jnp.zeros(ys_ref.shape, u32)


def _experts(blk_e, nxt_e, nused, xs, w_gate, w_up, w_down, layer, blk):
    P, DW = xs.shape
    D, DE = w_gate.shape[-2:]
    nblk = P // blk
    row_idx = lambda i, be, nx, nu: (jnp.minimum(i, nu[0] - 1), 0)
    hbm = pl.BlockSpec(memory_space=pl.ANY)
    grid_spec = pltpu.PrefetchScalarGridSpec(
        num_scalar_prefetch=3,
        grid=(nblk,),
        in_specs=[pl.BlockSpec((blk, DW), row_idx), hbm, hbm, hbm],
        out_specs=pl.BlockSpec((blk, DW), lambda i, be, nx, nu: (i, 0)),
        scratch_shapes=[pltpu.VMEM((D, DE), f32), pltpu.VMEM((D, DE), f32), pltpu.VMEM((DE, D), f32),
                        pltpu.VMEM((D, DE), bf16), pltpu.VMEM((D, DE), bf16), pltpu.VMEM((DE, D), bf16),
                        pltpu.SemaphoreType.DMA((3,))],
    )
    return pl.pallas_call(
        functools.partial(_experts_kernel, layer=layer),
        grid_spec=grid_spec,
        out_shape=jax.ShapeDtypeStruct((P, DW), u32),
        compiler_params=_cparams(("arbitrary",)),
        name="moe_experts",
    )(blk_e, nxt_e, nused, xs, w_gate, w_up, w_down)


def _combine_kernel(ssrc_ref, sdst_ref, sn_ref, stot_ref, lp_ref, gate_ref, ys_hbm, ysh_ref, x1_ref, gf_ref,
                    gpost_ref, x2_o, yb_ref, sems, nxt_refs=()):
    i = pl.program_id(0)
    n = pl.num_programs(0)
    tm = x1_ref.shape[0]
    lmax = yb_ref.shape[1]
    slot = i % 2
    seg = functools.partial(_seg_copies, ssrc_ref, sdst_ref, sn_ref, hbm_ref=ys_hbm, tm=tm, to_hbm=False)

    @pl.when(i == 0)
    def _():
        yb_ref[...] = jnp.zeros(yb_ref.shape, u32)
        seg(i, buf_ref=yb_ref.at[slot], sem=sems.at[slot])

    @pl.when(i + 1 < n)
    def _():
        seg(i + 1, buf_ref=yb_ref.at[1 - slot], sem=sems.at[1 - slot])

    _seg_wait(stot_ref[i], yb_ref.at[slot], ys_hbm, sems.at[slot], False)

    lp = lp_ref[...]
    gate = gate_ref[...]
    ysh = ysh_ref[...].astype(f32)
    half = ysh.shape[1] // 2
    lo = ysh[:, :half]
    hi = ysh[:, half:]
    for c in range(lmax // SORT_CHUNK):
        jl = (lax.broadcasted_iota(i32, (tm, SORT_CHUNK), 1) + c * SORT_CHUNK).astype(f32)
        g = jnp.zeros((tm, SORT_CHUNK), f32)
        for k in range(TOP_K):
            g = jnp.where(jl == lp[:, k:k + 1], gate[:, k:k + 1], g)
        gb = g.astype(bf16)
        a, b = _unpack_rows(yb_ref[slot, c * SORT_CHUNK:(c + 1) * SORT_CHUNK, :])
        lo = lo + _dot(gb, a.astype(bf16))
        hi = hi + _dot(gb, b.astype(bf16))
    y = jnp.concatenate([lo, hi], axis=1)
    ms = jnp.mean(y * y, axis=-1, keepdims=True)
    x2 = x1_ref[...] + gf_ref[0] * (y * lax.rsqrt(ms + NORM_EPS) * gpost_ref[...])
    x2_o[...] = x2
    if nxt_refs:
        scn_ref, shn_ref, gn_ref, hn_o = nxt_refs
        ms2 = jnp.mean(x2 * x2, axis=-1, keepdims=True)
        hn = x2 * lax.rsqrt(ms2 + NORM_EPS) * gn_ref[...]
        hn_o[...] = (hn * (1.0 + scn_ref[0]) + shn_ref[0]).astype(hn_o.dtype)


def _combine_kernel_next(ssrc_ref, sdst_ref, sn_ref, stot_ref, lp_ref, gate_ref, ys_hbm, ysh_ref, x1_ref, gf_ref,
                         gpost_ref, scn_ref, shn_ref, gn_ref, x2_o, hn_o, yb_ref, sems):
    _combine_kernel(ssrc_ref, sdst_ref, sn_ref, stot_ref, lp_ref, gate_ref, ys_hbm, ysh_ref, x1_ref, gf_ref,
                    gpost_ref, x2_o, yb_ref, sems, nxt_refs=(scn_ref, shn_ref, gn_ref, hn_o))


def _combine(seg_src, seg_dst, seg_n, seg_tot, lp_tk, gate_tk, ys, ysh, x1, mod3, g_post, S, tm, lmax, nxt=None):
    T, D = x1.shape
    DW = ys.shape[1]
    tpb = S // tm
    tile = pl.BlockSpec((tm, D), lambda i, *_: (i, 0))
    ktile = pl.BlockSpec((tm, TOP_K), lambda i, *_: (i, 0))
    modspec = lambda seg: pl.BlockSpec((1, 1, D), lambda i, *_: ((i // tpb) * N_MOD + seg, 0, 0))
    rowspec = pl.BlockSpec((1, D), lambda i, *_: (0, 0))
    in_specs = [ktile, ktile, pl.BlockSpec(memory_space=pl.ANY), tile, tile, modspec(5), rowspec]
    args = [seg_src, seg_dst, seg_n, seg_tot, lp_tk, gate_tk, ys, ysh, x1, mod3, g_post.reshape(1, D)]
    out_specs, out_shape, kern = tile, jax.ShapeDtypeStruct((T, D), f32), _combine_kernel
    if nxt is not None:
        in_specs += [modspec(1), modspec(0), rowspec]
        args += [nxt[0], nxt[0], nxt[1].reshape(1, D)]
        out_specs, out_shape = [tile, tile], [out_shape, jax.ShapeDtypeStruct((T, D), bf16)]
        kern = _combine_kernel_next
    grid_spec = pltpu.PrefetchScalarGridSpec(
        num_scalar_prefetch=4,
        grid=(T // tm,),
        in_specs=in_specs,
        out_specs=out_specs,
        scratch_shapes=[pltpu.VMEM((2, lmax, DW), u32), pltpu.SemaphoreType.DMA((2,))],
    )
    return pl.pallas_call(
        kern,
        grid_spec=grid_spec,
        out_shape=out_shape,
        compiler_params=_cparams(("arbitrary",)),
        name="moe_combine",
    )(*args)


def _tile(n, pref):
    t = min(n, pref)
    assert n % t == 0, (n, t)
    return t


def _layer(i, x2, h1, mod3, p, wexp, v_first, nxt, B, S, cfg):
    T, D = x2.shape
    H = (D // 2) // ATT_V_DIM
    W = D - D // 2
    att_cols = 2 * H * 2 * ATT_QK_DIM + H * ATT_V_DIM
    lam_init = 0.8 - 0.6 * math.exp(-0.3 * i)

    if h1 is None:
        h1 = _prenorm(x2, mod3, p["g_pre_mix"], S, _tile(S, cfg["tm_norm"]), 1, 0)
    att = _inproj(h1, p["w_in"][:, :att_cols].astype(bf16), bf16, _tile(T, cfg["tm_in"]), cfg["tn_att"])
    w_rwkv = p["w_in"][:, att_cols:].astype(bf16)

    slopes = jnp.broadcast_to(
        (2.0 ** (-ALIBI_MAX_BIAS * jnp.arange(1, H + 1, dtype=f32) / H))[:, None, None], (H, 1, LANES))
    lamp = jnp.stack([p["lam_q1"], p["lam_k1"], p["lam_q2"], p["lam_k2"]])
    o_att = _attention(att, slopes, lamp, p["att_subln_g"], B, S, H, lam_init, _tile(S, cfg["tq"]), cfg["hp"])

    cols = w_rwkv.shape[1]
    zw = jnp.zeros((RWKV_A_RANK, W), f32)
    heads = W // RWKV_HEAD
    ind = (jnp.arange(W)[:, None] // RWKV_HEAD == jnp.arange(LANES)[None, :]).astype(bf16)
    prm = {
        "mu": p["rwkv_mu"].reshape(1, cols), "w0": p["rwkv_w0"].reshape(1, W),
        "w2p": jnp.concatenate([p["rwkv_w2"], zw], axis=0),
        "a0": p["rwkv_a0"].reshape(1, W),
        "a2p": jnp.concatenate([jnp.zeros((RWKV_W_RANK, W), f32), p["rwkv_a2"]], axis=0),
        "g2": p["rwkv_g2"], "k_k": p["rwkv_k_k"].reshape(1, W), "k_a": p["rwkv_k_a"].reshape(1, W),
        "ind": ind, "indt": ind.T,
    }
    if v_first is not None:
        padc = LANES - RWKV_V_RANK
        prm["v0"] = p["rwkv_v0"].reshape(1, W)
        prm["v1p"] = jnp.pad(p["rwkv_v1"], ((0, 0), (0, padc)))
        prm["v2p"] = jnp.pad(p["rwkv_v2"], ((0, padc), (0, 0)))
    r, lw, kh, v, kn, kb, g = _rwkv_prep(h1, w_rwkv, prm, v_first, B, S, W, _tile(S, cfg["tm_prep"]))
    if v_first is None:
        v_first = v
    r2, y0, mmat, g0 = _wkv_intra(r, lw, kh, v, kn, kb, min(cfg["nc"], S // CHUNK), cfg["passes_intra"])
    o_rwkv = _wkv_state(r2, y0, mmat, g0, r, kh, v, g, p["rwkv_lnx_w"].reshape(1, W),
                        p["rwkv_lnx_b"].reshape(1, W), p["rwkv_r_k"].reshape(1, W), B, S,
                        min(cfg["pg"], W // LANES), min(cfg["cb"], S // CHUNK), cfg["passes_state"])
    del heads

    x1, h2, logits_t = _postmix(o_att, o_rwkv, p["w_out"][:D // 2].astype(bf16), p["w_out"][D // 2:].astype(bf16),
                                x2, mod3,
                                p["g_post_mix"], p["g_pre_ffn"], p["w_router"].T, S, _tile(S, cfg["tm_post"]))

    tm_t = _tile(T, cfg["tm_tile"])
    ntiles = T // tm_t
    eidx_t, gate_t, lrank_t, cnt_tbl = _router(logits_t, p["router_bias"], tm_t)
    blk = cfg["blk"]
    run = (cnt_tbl[:, :ntiles].T + SUBLANES - 1) // SUBLANES * SUBLANES
    counts = jnp.sum(run, axis=0)
    padded = (counts + blk - 1) // blk * blk
    pad_end = jnp.cumsum(padded)
    pad_start = pad_end - padded
    e_ids = jnp.arange(N_EXPERTS, dtype=i32)
    tile_off = jnp.cumsum(run, axis=0) - run
    loc_off = jnp.cumsum(run, axis=1) - run
    run_dst = pad_start[None, :] + tile_off
    psz = jnp.asarray(_piece_sizes(tm_t), i32)[:, None, None]
    has = (run[None] & psz) != 0
    before = run[None] & ~(2 * psz - 1)
    slot_of = jnp.cumsum(has, axis=2) - 1
    hit = has[:, :, None, :] & (slot_of[:, :, None, :] == e_ids[None, None, :, None])
    seg_src = jnp.sum(jnp.where(hit, (loc_off[None] + before)[:, :, None, :], 0), axis=-1).reshape(-1).astype(i32)
    seg_dst = jnp.sum(jnp.where(hit, (run_dst[None] + before)[:, :, None, :], 0), axis=-1).reshape(-1).astype(i32)
    seg_n = jnp.sum(has, axis=2).reshape(-1).astype(i32)
    seg_tot = jnp.sum(run, axis=1).astype(i32)
    loc_tok = jnp.repeat(loc_off, tm_t, axis=0)
    lp_t = jnp.sum(jnp.where(eidx_t[:, :, None] == e_ids, loc_tok[None], 0), axis=-1).astype(f32) + lrank_t
    lmax = -(-(tm_t * TOP_K + N_EXPERTS * (SUBLANES - 1)) // SORT_CHUNK) * SORT_CHUNK
    nblk = -(-(T * TOP_K + ntiles * N_EXPERTS * (SUBLANES - 1)) // blk) + N_EXPERTS
    P = nblk * blk
    blk_start = jnp.arange(nblk, dtype=i32) * blk
    nused = (pad_end[-1] // blk).astype(i32).reshape(1)
    blk_pos = jnp.minimum(blk_start, pad_end[-1] - blk)
    blk_e = jnp.minimum(jnp.sum((pad_end[None, :] <= blk_pos[:, None]).astype(i32), axis=1), N_EXPERTS - 1)
    cand = jnp.where(counts > 0, e_ids, N_EXPERTS)
    later = jnp.where(e_ids[None, :] > blk_e[:, None], cand[None, :], N_EXPERTS)
    nxt_e = jnp.min(later, axis=1)
    nxt_e = jnp.where(nxt_e >= N_EXPERTS, -1, nxt_e).astype(i32)

    xs, ysh = _dispatch(seg_src, seg_dst, seg_n, seg_tot, counts.astype(i32), pad_start.astype(i32), nused, lp_t, h2,
                        p["w_sh_gate"].astype(bf16), p["w_sh_up"].astype(bf16), p["w_sh_down"].astype(bf16),
                        P, blk, tm_t, lmax)
    ys = _experts(blk_e, nxt_e, nused, xs, wexp[0], wexp[1], wexp[2], i, blk)
    out = _combine(seg_src, seg_dst, seg_n, seg_tot, lp_t.T, gate_t.T, ys, ysh, x1,
                   mod3, p["g_post_ffn"], S, tm_t, lmax, nxt)
    x_out, h_next = (out, None) if nxt is None else out
    return x_out, h_next, v_first


_CFG = dict(tm_norm=512, tm_in=2048, tn_att=1024, tq=512, hp=4, tm_prep=256, nc=16, passes_intra=1, passes_state=1, pg=8, cb=8,
            tm_post=512, blk=256, tm_tile=256)

_LAYER_KEYS = ("g_pre_mix", "g_post_mix", "g_pre_ffn", "g_post_ffn", "w_in", "w_out", "lam_q1", "lam_k1",
               "lam_q2", "lam_k2", "att_subln_g", "rwkv_mu", "rwkv_w0", "rwkv_w2", "rwkv_a0", "rwkv_a2",
               "rwkv_g2", "rwkv_k_k", "rwkv_k_a", "rwkv_r_k", "rwkv_lnx_w", "rwkv_lnx_b", "w_router",
               "router_bias", "w_sh_gate", "w_sh_up", "w_sh_down")


def _forward(x, c, params, cfg):
    B, S, D = x.shape
    L = params["w_in"].shape[0]
    bp = 16
    c_pad = jnp.zeros((bp, D), f32).at[:B].set(c)
    mod = _ada_mod(c_pad, params["w_ada"], params["b_ada"])
    x2 = x.reshape(B * S, D)
    v_first = None
    h1 = None
    for i in range(L):
        p = {k: params[k][i] for k in _LAYER_KEYS}
        if i > 0:
            p["rwkv_v0"] = params["rwkv_v0"][i - 1]
            p["rwkv_v1"] = params["rwkv_v1"][i - 1]
            p["rwkv_v2"] = params["rwkv_v2"][i - 1]
        mod3 = mod[i, :B].reshape(B * N_MOD, 1, D)
        wexp = (params["w_exp_gate"], params["w_exp_up"], params["w_exp_down"])
        nxt = None
        if i + 1 < L:
            nxt = (mod[i + 1, :B].reshape(B * N_MOD, 1, D), params["g_pre_mix"][i + 1])
        x2, h1, v_first = _layer(i, x2, h1, mod3, p, wexp, v_first, nxt, B, S, cfg)
    return x2.reshape(B, S, D)


def kernel(x, c, w_ada, b_ada, g_pre_mix, g_post_mix, g_pre_ffn, g_post_ffn, w_in, w_out, lam_q1, lam_k1, lam_q2, lam_k2, att_subln_g, rwkv_mu, rwkv_w0, rwkv_w2, rwkv_a0, rwkv_a2, rwkv_g2, rwkv_k_k, rwkv_k_a, rwkv_r_k, rwkv_lnx_w, rwkv_lnx_b, rwkv_v0, rwkv_v1, rwkv_v2, w_router, router_bias, w_exp_gate, w_exp_up, w_exp_down, w_sh_gate, w_sh_up, w_sh_down):
    params = dict(w_ada=w_ada, b_ada=b_ada, g_pre_mix=g_pre_mix, g_post_mix=g_post_mix, g_pre_ffn=g_pre_ffn,
                  g_post_ffn=g_post_ffn, w_in=w_in, w_out=w_out, lam_q1=lam_q1, lam_k1=lam_k1, lam_q2=lam_q2,
                  lam_k2=lam_k2, att_subln_g=att_subln_g, rwkv_mu=rwkv_mu, rwkv_w0=rwkv_w0, rwkv_w2=rwkv_w2,
                  rwkv_a0=rwkv_a0, rwkv_a2=rwkv_a2, rwkv_g2=rwkv_g2, rwkv_k_k=rwkv_k_k, rwkv_k_a=rwkv_k_a,
                  rwkv_r_k=rwkv_r_k, rwkv_lnx_w=rwkv_lnx_w, rwkv_lnx_b=rwkv_lnx_b, rwkv_v0=rwkv_v0,
                  rwkv_v1=rwkv_v1, rwkv_v2=rwkv_v2, w_router=w_router, router_bias=router_bias,
                  w_exp_gate=w_exp_gate, w_exp_up=w_exp_up, w_exp_down=w_exp_down, w_sh_gate=w_sh_gate,
                  w_sh_up=w_sh_up, w_sh_down=w_sh_down)
    return _forward(x, c, params, _CFG)
```
